```python
import jax, jax.numpy as jnp
from jax import lax
import numpy as np

D_MODEL = 2048
BATCH = 8
SEQ = 4096
DEPTH = 2

SWA_Q_HEADS = 16
SWA_KV_HEADS = 2
SWA_HEAD_DIM = 64
SWA_WINDOW = 128
SWA_BLOCK = 128
MLA_HEADS = 16
MLA_Q_RANK = 512
MLA_KV_RANK = 512
MLA_NOPE_DIM = 128
MLA_ROPE_DIM = 64
MLA_V_DIM = 128
MLA_BLOCK = 128
ROPE_THETA = 10000.0
SGU_GROUPS = 8
SGU_GROUP_DIM = 128
SGU_CHUNK = 128
SGU_WIDTH = SGU_GROUPS * SGU_GROUP_DIM
D_FF = 5632
CONV_WIDTH = 3
N_BRANCHES = 3
EPS = 1e-5
MASK_VALUE = -1e30
DN_ALPHA = (2 * DEPTH) ** 0.25
DN_BETA = (8 * DEPTH) ** -0.25

A_Q = SWA_Q_HEADS * SWA_HEAD_DIM
A_KV = SWA_KV_HEADS * SWA_HEAD_DIM
B_OUT = MLA_HEADS * MLA_V_DIM
N_IN = A_Q + 2 * A_KV + MLA_Q_RANK + MLA_KV_RANK + MLA_ROPE_DIM + 2 * SGU_WIDTH + N_BRANCHES * D_MODEL

kernel_name = "hybrid_swa_mla_sgu_deepnorm"


def _layer_norm(x, g, b):
    xf = x.astype(jnp.float32)
    mu = xf.mean(-1, keepdims=True)
    var = jnp.mean(jnp.square(xf - mu), -1, keepdims=True)
    y = (xf - mu) * lax.rsqrt(var + EPS) * g.astype(jnp.float32) + b.astype(jnp.float32)
    return y.astype(x.dtype)


def _rms_norm(x, g):
    xf = x.astype(jnp.float32)
    y = xf * lax.rsqrt(jnp.mean(jnp.square(xf), -1, keepdims=True) + EPS) * g.astype(jnp.float32)
    return y.astype(x.dtype)


def _rope(x, cos, sin):
    x1, x2 = jnp.split(x, 2, axis=-1)
    return jnp.concatenate([x1 * cos - x2 * sin, x2 * cos + x1 * sin], axis=-1)


def _sliding_window_gqa(q, k, v, sinks):
    B, S = q.shape[:2]
    nb = S // SWA_BLOCK
    G = SWA_Q_HEADS // SWA_KV_HEADS
    qb = q.reshape(B, nb, SWA_BLOCK, SWA_KV_HEADS, G, SWA_HEAD_DIM)
    kb = k.reshape(B, nb, SWA_BLOCK, SWA_KV_HEADS, SWA_HEAD_DIM)
    vb = v.reshape(B, nb, SWA_BLOCK, SWA_KV_HEADS, SWA_HEAD_DIM)

    def with_prev(t):
        prev = jnp.pad(t[:, :-1], ((0, 0), (1, 0), (0, 0), (0, 0), (0, 0)))
        return jnp.concatenate([prev, t], axis=2)

    kw, vw = with_prev(kb), with_prev(vb)
    scores = jnp.einsum('bnqhgd,bnkhd->bnhgqk', qb, kw,
                        preferred_element_type=jnp.float32) * (SWA_HEAD_DIM ** -0.5)
    q_off = jnp.arange(SWA_BLOCK)[:, None] + SWA_BLOCK
    k_off = jnp.arange(2 * SWA_BLOCK)[None, :]
    rel = q_off - k_off
    band = (rel >= 0) & (rel < SWA_WINDOW)
    not_first = (jnp.arange(nb) > 0)[:, None, None]
    valid = band[None] & (not_first | (k_off >= SWA_BLOCK)[None])
    scores = jnp.where(valid[None, :, None, None], scores, MASK_VALUE)
    sink = sinks.astype(jnp.float32).reshape(SWA_KV_HEADS, G)[None, None, :, :, None, None]
    m = jnp.maximum(scores.max(-1, keepdims=True), sink)
    p = jnp.exp(scores - m)
    p = (p / (p.sum(-1, keepdims=True) + jnp.exp(sink - m))).astype(v.dtype)
    out = jnp.einsum('bnhgqk,bnkhd->bnqhgd', p, vw)
    return out.reshape(B, S, A_Q)


def _mla(c_q, c_kv, k_rope, cos, sin, q_norm_g, kv_norm_g, w_uq, w_ukv):
    B, S = c_q.shape[:2]
    q = (_rms_norm(c_q, q_norm_g) @ w_uq).reshape(B, S, MLA_HEADS, MLA_NOPE_DIM + MLA_ROPE_DIM)
    q_nope = q[..., :MLA_NOPE_DIM]
    q_rope = _rope(q[..., MLA_NOPE_DIM:], cos[:, :, None], sin[:, :, None])
    kv = (_rms_norm(c_kv, kv_norm_g) @ w_ukv).reshape(B, S, MLA_HEADS, MLA_NOPE_DIM + MLA_V_DIM)
    k_nope, v = kv[..., :MLA_NOPE_DIM], kv[..., MLA_NOPE_DIM:]
    k_r = _rope(k_rope, cos, sin)
    nb = S // MLA_BLOCK
    scale = (MLA_NOPE_DIM + MLA_ROPE_DIM) ** -0.5
    qn_b = q_nope.reshape(B, nb, MLA_BLOCK, MLA_HEADS, MLA_NOPE_DIM).transpose(1, 0, 2, 3, 4)
    qr_b = q_rope.reshape(B, nb, MLA_BLOCK, MLA_HEADS, MLA_ROPE_DIM).transpose(1, 0, 2, 3, 4)
    key_idx = jnp.arange(S)

    def block(args):
        qn, qr, i = args
        s = (jnp.einsum('bqhd,bkhd->bhqk', qn, k_nope, preferred_element_type=jnp.float32)
             + jnp.einsum('bqhr,bkr->bhqk', qr, k_r, preferred_element_type=jnp.float32)) * scale
        q_idx = i * MLA_BLOCK + jnp.arange(MLA_BLOCK)
        s = jnp.where((key_idx[None, :] <= q_idx[:, None])[None, None], s, MASK_VALUE)
        p = jax.nn.softmax(s, axis=-1).astype(v.dtype)
        return jnp.einsum('bhqk,bkhd->bqhd', p, v)

    out = lax.map(block, (qn_b, qr_b, jnp.arange(nb)))
    return out.transpose(1, 0, 2, 3, 4).reshape(B, S, B_OUT)


def _chunked_sgu(u, v, ln_g, ln_b, w_s, b_s):
    B, S = u.shape[:2]
    nc = S // SGU_CHUNK
    vn = _layer_norm(v, ln_g, ln_b).reshape(B, nc, SGU_CHUNK, SGU_GROUPS, SGU_GROUP_DIM)
    causal = jnp.tril(jnp.ones((SGU_CHUNK, SGU_CHUNK), dtype=bool))
    w = jnp.where(causal[None], w_s, 0.0)
    mixed = jnp.einsum('gts,bnsgc->bntgc', w, vn) + b_s.T[None, None, :, :, None]
    return u * mixed.reshape(B, S, SGU_WIDTH)


def _fwd_setup_inputs(seed: int = 0) -> dict:
    key = jax.random.key(seed)
    ks = jax.random.split(key, 26)
    L, D = DEPTH, D_MODEL
    f32 = jnp.float32
    nrm = lambda k, shape, s: jax.random.normal(k, shape, f32) * s
    x = jax.random.normal(ks[0], (BATCH, SEQ, D), f32)
    offset = jax.random.randint(ks[1], (BATCH, 1), 0, 1024, dtype=jnp.int32)
    positions = (offset + jnp.arange(SEQ, dtype=jnp.int32)[None, :]).astype(jnp.int32)
    return {
        "x": x,
        "positions": positions,
        "w_in": nrm(ks[2], (L, D, N_IN), D ** -0.5),
        "b_gate": nrm(ks[3], (L, N_BRANCHES, D), 0.1),
        "sinks": nrm(ks[4], (L, SWA_Q_HEADS), 0.5),
        "q_norm_g": 1.0 + nrm(ks[5], (L, MLA_Q_RANK), 0.02),
        "kv_norm_g": 1.0 + nrm(ks[6], (L, MLA_KV_RANK), 0.02),
        "w_uq": nrm(ks[7], (L, MLA_Q_RANK, MLA_HEADS * (MLA_NOPE_DIM + MLA_ROPE_DIM)), MLA_Q_RANK ** -0.5),
        "w_ukv": nrm(ks[8], (L, MLA_KV_RANK, MLA_HEADS * (MLA_NOPE_DIM + MLA_V_DIM)), MLA_KV_RANK ** -0.5),
        "sgu_ln_g": 1.0 + nrm(ks[9], (L, SGU_WIDTH), 0.02),
        "sgu_ln_b": nrm(ks[10], (L, SGU_WIDTH), 0.02),
        "sgu_w": nrm(ks[11], (L, SGU_GROUPS, SGU_CHUNK, SGU_CHUNK), SGU_CHUNK ** -0.5),
        "sgu_b": 1.0 + nrm(ks[12], (L, SGU_GROUPS, SGU_CHUNK), 0.02),
        "w_proj_a": nrm(ks[13], (L, A_Q, D), A_Q ** -0.5),
        "w_proj_b": nrm(ks[14], (L, B_OUT, D), B_OUT ** -0.5),
        "w_proj_c": nrm(ks[15], (L, SGU_WIDTH, D), SGU_WIDTH ** -0.5),
        "w_o": nrm(ks[16], (L, D, D), DN_BETA * D ** -0.5),
        "ln1_g": 1.0 + nrm(ks[17], (L, D), 0.02),
        "ln1_b": nrm(ks[18], (L, D), 0.02),
        "w_up": nrm(ks[19], (L, D, 2 * D_FF), D ** -0.5),
        "conv_w": nrm(ks[20], (L, CONV_WIDTH, 2 * D_FF), CONV_WIDTH ** -0.5),
        "conv_b": nrm(ks[21], (L, 2 * D_FF), 0.02),
        "w_down": nrm(ks[22], (L, D_FF, D), DN_BETA * D_FF ** -0.5),
        "ln2_g": 1.0 + nrm(ks[23], (L, D), 0.02),
        "ln2_b": nrm(ks[24], (L, D), 0.02),
    }


def _fwd_reference(x, positions, w_in, b_gate, sinks, q_norm_g, kv_norm_g, w_uq, w_ukv,
              sgu_ln_g, sgu_ln_b, sgu_w, sgu_b, w_proj_a, w_proj_b, w_proj_c, w_o,
              ln1_g, ln1_b, w_up, conv_w, conv_b, w_down, ln2_g, ln2_b):
    B, S, D = x.shape
    inv_freq = ROPE_THETA ** (-jnp.arange(0, MLA_ROPE_DIM, 2, dtype=jnp.float32) / MLA_ROPE_DIM)
    ang = positions.astype(jnp.float32)[..., None] * inv_freq
    cos, sin = jnp.cos(ang).astype(x.dtype), jnp.sin(ang).astype(x.dtype)
    split_at = [A_Q, A_Q + A_KV, A_Q + 2 * A_KV]
    split_at += [split_at[-1] + MLA_Q_RANK]
    split_at += [split_at[-1] + MLA_KV_RANK]
    split_at += [split_at[-1] + MLA_ROPE_DIM]
    split_at += [split_at[-1] + SGU_WIDTH]
    split_at += [split_at[-1] + SGU_WIDTH]

    for l in range(DEPTH):
        h = x @ w_in[l]
        qa, ka, va, c_q, c_kv, k_rope, hu, hv, g_logit = jnp.split(h, split_at, axis=-1)
        y_a = _sliding_window_gqa(qa, ka, va, sinks[l])
        y_b = _mla(c_q, c_kv, k_rope, cos, sin, q_norm_g[l], kv_norm_g[l], w_uq[l], w_ukv[l])
        y_c = _chunked_sgu(jax.nn.gelu(hu, approximate=False), jax.nn.gelu(hv, approximate=False),
                           sgu_ln_g[l], sgu_ln_b[l], sgu_w[l], sgu_b[l])
        gates = jax.nn.sigmoid((g_logit.reshape(B, S, N_BRANCHES, D) + b_gate[l]).astype(jnp.float32)).astype(x.dtype)
        merged = (gates[:, :, 0] * (y_a @ w_proj_a[l])
                  + gates[:, :, 1] * (y_b @ w_proj_b[l])
                  + gates[:, :, 2] * (y_c @ w_proj_c[l]))
        x = _layer_norm(DN_ALPHA * x + merged @ w_o[l], ln1_g[l], ln1_b[l])

        up = x @ w_up[l]
        up_pad = jnp.pad(up, ((0, 0), (CONV_WIDTH - 1, 0), (0, 0)))
        conv = conv_b[l] + sum(up_pad[:, j:j + S] * conv_w[l, j] for j in range(CONV_WIDTH))
        gate, val = jnp.split(conv, 2, axis=-1)
        x = _layer_norm(DN_ALPHA * x + (jax.nn.silu(gate) * val) @ w_down[l], ln2_g[l], ln2_b[l])
    return x


import jax as _jax
import jax.numpy as _jnp

TWIN_FORMAT = 'train_step'
FWD_PARAMS = ['x', 'positions', 'w_in', 'b_gate', 'sinks', 'q_norm_g', 'kv_norm_g', 'w_uq', 'w_ukv', 'sgu_ln_g', 'sgu_ln_b', 'sgu_w', 'sgu_b', 'w_proj_a', 'w_proj_b', 'w_proj_c', 'w_o', 'ln1_g', 'ln1_b', 'w_up', 'conv_w', 'conv_b', 'w_down', 'ln2_g', 'ln2_b']
TWIN_WEIGHTS = ['w_in', 'b_gate', 'sinks', 'q_norm_g', 'kv_norm_g', 'w_uq', 'w_ukv', 'sgu_ln_g', 'sgu_ln_b', 'sgu_w', 'sgu_b', 'w_proj_a', 'w_proj_b', 'w_proj_c', 'w_o', 'ln1_g', 'ln1_b', 'w_up', 'conv_w', 'conv_b', 'w_down', 'ln2_g', 'ln2_b']
TWIN_DIFF_INPUT = 'x'
TWIN_INPUTS = ['x', 'positions', 'w_in', 'b_gate', 'sinks', 'q_norm_g', 'kv_norm_g', 'w_uq', 'w_ukv', 'sgu_ln_g', 'sgu_ln_b', 'sgu_w', 'sgu_b', 'w_proj_a', 'w_proj_b', 'w_proj_c', 'w_o', 'ln1_g', 'ln1_b', 'w_up', 'conv_w', 'conv_b', 'w_down', 'ln2_g', 'ln2_b', 'loss_target', 'm_w_in', 'm_b_gate', 'm_sinks', 'm_q_norm_g', 'm_kv_norm_g', 'm_w_uq', 'm_w_ukv', 'm_sgu_ln_g', 'm_sgu_ln_b', 'm_sgu_w', 'm_sgu_b', 'm_w_proj_a', 'm_w_proj_b', 'm_w_proj_c', 'm_w_o', 'm_ln1_g', 'm_ln1_b', 'm_w_up', 'm_conv_w', 'm_conv_b', 'm_w_down', 'm_ln2_g', 'm_ln2_b', 'v_w_in', 'v_b_gate', 'v_sinks', 'v_q_norm_g', 'v_kv_norm_g', 'v_w_uq', 'v_w_ukv', 'v_sgu_ln_g', 'v_sgu_ln_b', 'v_sgu_w', 'v_sgu_b', 'v_w_proj_a', 'v_w_proj_b', 'v_w_proj_c', 'v_w_o', 'v_ln1_g', 'v_ln1_b', 'v_w_up', 'v_conv_w', 'v_conv_b', 'v_w_down', 'v_ln2_g', 'v_ln2_b']
TWIN_OUTPUTS = ['loss', 'grad_x', 'grad_w_in', 'grad_b_gate', 'grad_sinks', 'grad_q_norm_g', 'grad_kv_norm_g', 'grad_w_uq', 'grad_w_ukv', 'grad_sgu_ln_g', 'grad_sgu_ln_b', 'grad_sgu_w', 'grad_sgu_b', 'grad_w_proj_a', 'grad_w_proj_b', 'grad_w_proj_c', 'grad_w_o', 'grad_ln1_g', 'grad_ln1_b', 'grad_w_up', 'grad_conv_w', 'grad_conv_b', 'grad_w_down', 'grad_ln2_g', 'grad_ln2_b', 'delta_w_in', 'delta_b_gate', 'delta_sinks', 'delta_q_norm_g', 'delta_kv_norm_g', 'delta_w_uq', 'delta_w_ukv', 'delta_sgu_ln_g', 'delta_sgu_ln_b', 'delta_sgu_w', 'delta_sgu_b', 'delta_w_proj_a', 'delta_w_proj_b', 'delta_w_proj_c', 'delta_w_o', 'delta_ln1_g', 'delta_ln1_b', 'delta_w_up', 'delta_conv_w', 'delta_conv_b', 'delta_w_down', 'delta_ln2_g', 'delta_ln2_b', 'new_m_w_in', 'new_m_b_gate', 'new_m_sinks', 'new_m_q_norm_g', 'new_m_kv_norm_g', 'new_m_w_uq', 'new_m_w_ukv', 'new_m_sgu_ln_g', 'new_m_sgu_ln_b', 'new_m_sgu_w', 'new_m_sgu_b', 'new_m_w_proj_a', 'new_m_w_proj_b', 'new_m_w_proj_c', 'new_m_w_o', 'new_m_ln1_g', 'new_m_ln1_b', 'new_m_w_up', 'new_m_conv_w', 'new_m_conv_b', 'new_m_w_down', 'new_m_ln2_g', 'new_m_ln2_b', 'new_v_w_in', 'new_v_b_gate', 'new_v_sinks', 'new_v_q_norm_g', 'new_v_kv_norm_g', 'new_v_w_uq', 'new_v_w_ukv', 'new_v_sgu_ln_g', 'new_v_sgu_ln_b', 'new_v_sgu_w', 'new_v_sgu_b', 'new_v_w_proj_a', 'new_v_w_proj_b', 'new_v_w_proj_c', 'new_v_w_o', 'new_v_ln1_g', 'new_v_ln1_b', 'new_v_w_up', 'new_v_conv_w', 'new_v_conv_b', 'new_v_w_down', 'new_v_ln2_g', 'new_v_ln2_b']
TWIN_LEAF_KINDS = {'loss': 'loss', 'grad_x': 'grad_x', 'grad_w_in': 'grad_w', 'grad_b_gate': 'grad_w', 'grad_sinks': 'grad_w', 'grad_q_norm_g': 'grad_w', 'grad_kv_norm_g': 'grad_w', 'grad_w_uq': 'grad_w', 'grad_w_ukv': 'grad_w', 'grad_sgu_ln_g': 'grad_w', 'grad_sgu_ln_b': 'grad_w', 'grad_sgu_w': 'grad_w', 'grad_sgu_b': 'grad_w', 'grad_w_proj_a': 'grad_w', 'grad_w_proj_b': 'grad_w', 'grad_w_proj_c': 'grad_w', 'grad_w_o': 'grad_w', 'grad_ln1_g': 'grad_w', 'grad_ln1_b': 'grad_w', 'grad_w_up': 'grad_w', 'grad_conv_w': 'grad_w', 'grad_conv_b': 'grad_w', 'grad_w_down': 'grad_w', 'grad_ln2_g': 'grad_w', 'grad_ln2_b': 'grad_w', 'delta_w_in': 'delta_w', 'delta_b_gate': 'delta_w', 'delta_sinks': 'delta_w', 'delta_q_norm_g': 'delta_w', 'delta_kv_norm_g': 'delta_w', 'delta_w_uq': 'delta_w', 'delta_w_ukv': 'delta_w', 'delta_sgu_ln_g': 'delta_w', 'delta_sgu_ln_b': 'delta_w', 'delta_sgu_w': 'delta_w', 'delta_sgu_b': 'delta_w', 'delta_w_proj_a': 'delta_w', 'delta_w_proj_b': 'delta_w', 'delta_w_proj_c': 'delta_w', 'delta_w_o': 'delta_w', 'delta_ln1_g': 'delta_w', 'delta_ln1_b': 'delta_w', 'delta_w_up': 'delta_w', 'delta_conv_w': 'delta_w', 'delta_conv_b': 'delta_w', 'delta_w_down': 'delta_w', 'delta_ln2_g': 'delta_w', 'delta_ln2_b': 'delta_w', 'new_m_w_in': 'new_m', 'new_m_b_gate': 'new_m', 'new_m_sinks': 'new_m', 'new_m_q_norm_g': 'new_m', 'new_m_kv_norm_g': 'new_m', 'new_m_w_uq': 'new_m', 'new_m_w_ukv': 'new_m', 'new_m_sgu_ln_g': 'new_m', 'new_m_sgu_ln_b': 'new_m', 'new_m_sgu_w': 'new_m', 'new_m_sgu_b': 'new_m', 'new_m_w_proj_a': 'new_m', 'new_m_w_proj_b': 'new_m', 'new_m_w_proj_c': 'new_m', 'new_m_w_o': 'new_m', 'new_m_ln1_g': 'new_m', 'new_m_ln1_b': 'new_m', 'new_m_w_up': 'new_m', 'new_m_conv_w': 'new_m', 'new_m_conv_b': 'new_m', 'new_m_w_down': 'new_m', 'new_m_ln2_g': 'new_m', 'new_m_ln2_b': 'new_m', 'new_v_w_in': 'new_v', 'new_v_b_gate': 'new_v', 'new_v_sinks': 'new_v', 'new_v_q_norm_g': 'new_v', 'new_v_kv_norm_g': 'new_v', 'new_v_w_uq': 'new_v', 'new_v_w_ukv': 'new_v', 'new_v_sgu_ln_g': 'new_v', 'new_v_sgu_ln_b': 'new_v', 'new_v_sgu_w': 'new_v', 'new_v_sgu_b': 'new_v', 'new_v_w_proj_a': 'new_v', 'new_v_w_proj_b': 'new_v', 'new_v_w_proj_c': 'new_v', 'new_v_w_o': 'new_v', 'new_v_ln1_g': 'new_v', 'new_v_ln1_b': 'new_v', 'new_v_w_up': 'new_v', 'new_v_conv_w': 'new_v', 'new_v_conv_b': 'new_v', 'new_v_w_down': 'new_v', 'new_v_ln2_g': 'new_v', 'new_v_ln2_b': 'new_v'}


def _forward(args):
    return _fwd_reference(*[args[k] for k in FWD_PARAMS])


def _output_shape():
    def fwd():
        inp = _fwd_setup_inputs(0)
        return _fwd_reference(*[inp[k] for k in FWD_PARAMS])
    out = _jax.eval_shape(fwd)
    return out.shape, out.dtype

N_MICROBATCH = 1
ADAM_LR = 0.001
ADAM_B1 = 0.9
ADAM_B2 = 0.999
ADAM_EPS = 1e-08
ADAM_WD = 0.01
ADAM_STEP = 10
PER_EXAMPLE_BATCH_AXIS = {'x': 0, 'positions': 0, 'loss_target': 0}
SHARED_INPUTS = []
_WEIGHT_DTYPES = {'w_in': _jnp.float32, 'b_gate': _jnp.float32, 'sinks': _jnp.float32, 'q_norm_g': _jnp.float32, 'kv_norm_g': _jnp.float32, 'w_uq': _jnp.float32, 'w_ukv': _jnp.float32, 'sgu_ln_g': _jnp.float32, 'sgu_ln_b': _jnp.float32, 'sgu_w': _jnp.float32, 'sgu_b': _jnp.float32, 'w_proj_a': _jnp.float32, 'w_proj_b': _jnp.float32, 'w_proj_c': _jnp.float32, 'w_o': _jnp.float32, 'ln1_g': _jnp.float32, 'ln1_b': _jnp.float32, 'w_up': _jnp.float32, 'conv_w': _jnp.float32, 'conv_b': _jnp.float32, 'w_down': _jnp.float32, 'ln2_g': _jnp.float32, 'ln2_b': _jnp.float32}
MOMENT_SCALE = {'w_in': 8.735020e-03, 'b_gate': 4.157005e-03, 'sinks': 4.712302e-03, 'q_norm_g': 6.155214e-03, 'kv_norm_g': 9.245017e-03, 'w_uq': 2.528920e-03, 'w_ukv': 3.320432e-03, 'sgu_ln_g': 1.155210e-02, 'sgu_ln_b': 1.198670e-02, 'sgu_w': 1.140251e-02, 'sgu_b': 1.605459e-02, 'w_proj_a': 3.729987e-03, 'w_proj_b': 3.944137e-03, 'w_proj_c': 1.775765e-02, 'w_o': 3.519926e-02, 'ln1_g': 5.676171e-01, 'ln1_b': 2.787325e-01, 'w_up': 1.178181e-02, 'conv_w': 1.190622e-02, 'conv_b': 1.377563e-02, 'w_down': 3.851216e-02, 'ln2_g': 1.133938e+01, 'ln2_b': 8.619498e-01}


def _to_microbatches(a, axis):
    t = _jnp.moveaxis(a, axis, 0)
    t = t.reshape((N_MICROBATCH, t.shape[0] // N_MICROBATCH) + t.shape[1:])
    return _jnp.moveaxis(t, 1, axis + 1)


def setup_inputs(seed: int = 0) -> dict:
    inp = _fwd_setup_inputs(seed)
    key = _jax.random.fold_in(_jax.random.key(seed), 7919)
    shape, _ = _output_shape()
    out = dict(inp)
    out["loss_target"] = _jax.random.normal(_jax.random.fold_in(key, 0), shape, _jnp.float32)
    for i, name in enumerate(TWIN_WEIGHTS):
        w = inp[name].astype(_jnp.float32)
        if MOMENT_SCALE is None:
            s = _jnp.sqrt(_jnp.mean(_jnp.square(w)) + 1e-30)
        else:
            s = MOMENT_SCALE[name]
        km, kv = _jax.random.split(_jax.random.fold_in(key, i + 1))
        out[name] = w
        out["m_" + name] = s * _jax.random.normal(km, w.shape, _jnp.float32)
        out["v_" + name] = (s * s) * _jax.random.uniform(kv, w.shape, _jnp.float32, 0.5, 1.5)
    if N_MICROBATCH > 1:
        for name, axis in PER_EXAMPLE_BATCH_AXIS.items():
            out[name] = _to_microbatches(out[name], axis)
    return {'x': out['x'], 'positions': out['positions'], 'w_in': out['w_in'], 'b_gate': out['b_gate'], 'sinks': out['sinks'], 'q_norm_g': out['q_norm_g'], 'kv_norm_g': out['kv_norm_g'], 'w_uq': out['w_uq'], 'w_ukv': out['w_ukv'], 'sgu_ln_g': out['sgu_ln_g'], 'sgu_ln_b': out['sgu_ln_b'], 'sgu_w': out['sgu_w'], 'sgu_b': out['sgu_b'], 'w_proj_a': out['w_proj_a'], 'w_proj_b': out['w_proj_b'], 'w_proj_c': out['w_proj_c'], 'w_o': out['w_o'], 'ln1_g': out['ln1_g'], 'ln1_b': out['ln1_b'], 'w_up': out['w_up'], 'conv_w': out['conv_w'], 'conv_b': out['conv_b'], 'w_down': out['w_down'], 'ln2_g': out['ln2_g'], 'ln2_b': out['ln2_b'], 'loss_target': out['loss_target'], 'm_w_in': out['m_w_in'], 'm_b_gate': out['m_b_gate'], 'm_sinks': out['m_sinks'], 'm_q_norm_g': out['m_q_norm_g'], 'm_kv_norm_g': out['m_kv_norm_g'], 'm_w_uq': out['m_w_uq'], 'm_w_ukv': out['m_w_ukv'], 'm_sgu_ln_g': out['m_sgu_ln_g'], 'm_sgu_ln_b': out['m_sgu_ln_b'], 'm_sgu_w': out['m_sgu_w'], 'm_sgu_b': out['m_sgu_b'], 'm_w_proj_a': out['m_w_proj_a'], 'm_w_proj_b': out['m_w_proj_b'], 'm_w_proj_c': out['m_w_proj_c'], 'm_w_o': out['m_w_o'], 'm_ln1_g': out['m_ln1_g'], 'm_ln1_b': out['m_ln1_b'], 'm_w_up': out['m_w_up'], 'm_conv_w': out['m_conv_w'], 'm_conv_b': out['m_conv_b'], 'm_w_down': out['m_w_down'], 'm_ln2_g': out['m_ln2_g'], 'm_ln2_b': out['m_ln2_b'], 'v_w_in': out['v_w_in'], 'v_b_gate': out['v_b_gate'], 'v_sinks': out['v_sinks'], 'v_q_norm_g': out['v_q_norm_g'], 'v_kv_norm_g': out['v_kv_norm_g'], 'v_w_uq': out['v_w_uq'], 'v_w_ukv': out['v_w_ukv'], 'v_sgu_ln_g': out['v_sgu_ln_g'], 'v_sgu_ln_b': out['v_sgu_ln_b'], 'v_sgu_w': out['v_sgu_w'], 'v_sgu_b': out['v_sgu_b'], 'v_w_proj_a': out['v_w_proj_a'], 'v_w_proj_b': out['v_w_proj_b'], 'v_w_proj_c': out['v_w_proj_c'], 'v_w_o': out['v_w_o'], 'v_ln1_g': out['v_ln1_g'], 'v_ln1_b': out['v_ln1_b'], 'v_w_up': out['v_w_up'], 'v_conv_w': out['v_conv_w'], 'v_conv_b': out['v_conv_b'], 'v_w_down': out['v_w_down'], 'v_ln2_g': out['v_ln2_g'], 'v_ln2_b': out['v_ln2_b']}


def _loss(weights, diff, rest, loss_target):
    with _jax.named_scope("forward"):
        args = {**rest, TWIN_DIFF_INPUT: diff, **{k: w.astype(_WEIGHT_DTYPES[k]) for k, w in weights.items()}}
        y = _forward(args)
    with _jax.named_scope("loss_head"):
        err = _jnp.square(y.astype(_jnp.float32) - loss_target)
        return 0.5 * _jnp.sum(_jnp.mean(err, axis=-1)) if err.ndim else 0.5 * err


def _adamw(w, g, m, v):
    m = ADAM_B1 * m + (1.0 - ADAM_B1) * g
    v = ADAM_B2 * v + (1.0 - ADAM_B2) * _jnp.square(g)
    m_hat = m / (1.0 - ADAM_B1 ** ADAM_STEP)
    v_hat = v / (1.0 - ADAM_B2 ** ADAM_STEP)
    delta = -ADAM_LR * (m_hat / (_jnp.sqrt(v_hat) + ADAM_EPS) + ADAM_WD * w)
    return delta, m, v


def reference(x, positions, w_in, b_gate, sinks, q_norm_g, kv_norm_g, w_uq, w_ukv, sgu_ln_g, sgu_ln_b, sgu_w, sgu_b, w_proj_a, w_proj_b, w_proj_c, w_o, ln1_g, ln1_b, w_up, conv_w, conv_b, w_down, ln2_g, ln2_b, loss_target, m_w_in, m_b_gate, m_sinks, m_q_norm_g, m_kv_norm_g, m_w_uq, m_w_ukv, m_sgu_ln_g, m_sgu_ln_b, m_sgu_w, m_sgu_b, m_w_proj_a, m_w_proj_b, m_w_proj_c, m_w_o, m_ln1_g, m_ln1_b, m_w_up, m_conv_w, m_conv_b, m_w_down, m_ln2_g, m_ln2_b, v_w_in, v_b_gate, v_sinks, v_q_norm_g, v_kv_norm_g, v_w_uq, v_w_ukv, v_sgu_ln_g, v_sgu_ln_b, v_sgu_w, v_sgu_b, v_w_proj_a, v_w_proj_b, v_w_proj_c, v_w_o, v_ln1_g, v_ln1_b, v_w_up, v_conv_w, v_conv_b, v_w_down, v_ln2_g, v_ln2_b):
    given = dict(x=x, positions=positions, w_in=w_in, b_gate=b_gate, sinks=sinks, q_norm_g=q_norm_g, kv_norm_g=kv_norm_g, w_uq=w_uq, w_ukv=w_ukv, sgu_ln_g=sgu_ln_g, sgu_ln_b=sgu_ln_b, sgu_w=sgu_w, sgu_b=sgu_b, w_proj_a=w_proj_a, w_proj_b=w_proj_b, w_proj_c=w_proj_c, w_o=w_o, ln1_g=ln1_g, ln1_b=ln1_b, w_up=w_up, conv_w=conv_w, conv_b=conv_b, w_down=w_down, ln2_g=ln2_g, ln2_b=ln2_b, loss_target=loss_target, m_w_in=m_w_in, m_b_gate=m_b_gate, m_sinks=m_sinks, m_q_norm_g=m_q_norm_g, m_kv_norm_g=m_kv_norm_g, m_w_uq=m_w_uq, m_w_ukv=m_w_ukv, m_sgu_ln_g=m_sgu_ln_g, m_sgu_ln_b=m_sgu_ln_b, m_sgu_w=m_sgu_w, m_sgu_b=m_sgu_b, m_w_proj_a=m_w_proj_a, m_w_proj_b=m_w_proj_b, m_w_proj_c=m_w_proj_c, m_w_o=m_w_o, m_ln1_g=m_ln1_g, m_ln1_b=m_ln1_b, m_w_up=m_w_up, m_conv_w=m_conv_w, m_conv_b=m_conv_b, m_w_down=m_w_down, m_ln2_g=m_ln2_g, m_ln2_b=m_ln2_b, v_w_in=v_w_in, v_b_gate=v_b_gate, v_sinks=v_sinks, v_q_norm_g=v_q_norm_g, v_kv_norm_g=v_kv_norm_g, v_w_uq=v_w_uq, v_w_ukv=v_w_ukv, v_sgu_ln_g=v_sgu_ln_g, v_sgu_ln_b=v_sgu_ln_b, v_sgu_w=v_sgu_w, v_sgu_b=v_sgu_b, v_w_proj_a=v_w_proj_a, v_w_proj_b=v_w_proj_b, v_w_proj_c=v_w_proj_c, v_w_o=v_w_o, v_ln1_g=v_ln1_g, v_ln1_b=v_ln1_b, v_w_up=v_w_up, v_conv_w=v_conv_w, v_conv_b=v_conv_b, v_w_down=v_w_down, v_ln2_g=v_ln2_g, v_ln2_b=v_ln2_b)
    weights = {n: given[n] for n in TWIN_WEIGHTS}
    shared = {n: given[n] for n in SHARED_INPUTS}
    per_example = {n: given[n] for n in ['x', 'positions']}
    grad_fn = _jax.value_and_grad(_loss, argnums=(0, 1))

    def one_microbatch(ex, loss_target):
        ex = dict(ex)
        diff = ex.pop(TWIN_DIFF_INPUT)
        return grad_fn(weights, diff, {**shared, **ex}, loss_target)

    if N_MICROBATCH == 1:
        loss, (grad_w, grad_x) = one_microbatch(per_example, given["loss_target"])
    else:
        def body(carry, xs):
            loss_sum, grad_sum = carry
            l_k, (gw_k, gx_k) = one_microbatch(xs[0], xs[1])
            with _jax.named_scope("update"):
                return (loss_sum + l_k, _jax.tree.map(_jnp.add, grad_sum, gw_k)), gx_k

        init = (_jnp.zeros((), _jnp.float32), _jax.tree.map(_jnp.zeros_like, weights))
        (loss, grad_w), grad_x = _jax.lax.scan(body, init, (per_example, given["loss_target"]))
    with _jax.named_scope("update"):
        delta_w, new_m, new_v = {}, {}, {}
        for n in TWIN_WEIGHTS:
            delta_w[n], new_m[n], new_v[n] = _adamw(weights[n], grad_w[n], given["m_" + n], given["v_" + n])
    return (loss, grad_x, *[grad_w[n] for n in TWIN_WEIGHTS], *[delta_w[n] for n in TWIN_WEIGHTS],
            *[new_m[n] for n in TWIN_WEIGHTS], *[new_v[n] for n in TWIN_WEIGHTS])
```

```python
import functools
import math

import jax
import jax.numpy as jnp
from jax import lax
from jax.experimental import pallas as pl
from jax.experimental.pallas import tpu as pltpu

F32 = jnp.float32
BF16 = jnp.bfloat16

SWA_Q_HEADS = 16
SWA_KV_HEADS = 2
SWA_HEAD_DIM = 64
SWA_BLOCK = 128
MLA_HEADS = 16
MLA_NOPE = 128
MLA_ROPE = 64
MLA_V = 128
SGU_GROUPS = 8
SGU_DIM = 128
SGU_CHUNK = 128
ROPE_THETA = 10000.0
EPS = 1e-5
MASK = -1e30
ADAM_LR = 0.001
ADAM_B1 = 0.9
ADAM_B2 = 0.999
ADAM_EPS = 1e-08
ADAM_WD = 0.01
ADAM_STEP = 10

N_DEV = 8
AXES = ("x", "y", "c")
VMEM_LIMIT = 56 * 1024 * 1024
MLA_TILE = 512
ROW_TILE = 512
SUBLANES = 8


def _tile(n, prefs):
    for p in prefs:
        if n % p == 0:
            return p
    return n


def _params(sem):
    return pltpu.CompilerParams(dimension_semantics=sem, vmem_limit_bytes=VMEM_LIMIT)


def _cols(tm, width, off):
    assert off % width == 0, (off, width)
    blk = off // width
    return pl.BlockSpec((tm, width), lambda i, *_: (i, blk))


def _full(shape):
    nd = len(shape)
    return pl.BlockSpec(shape, lambda *_: (0,) * nd)


def _sigmoid(v):
    return 1.0 / (1.0 + jnp.exp(-v))


def _gelu(v):
    return 0.5 * v * (1.0 + lax.erf(v * (2.0 ** -0.5)))


def _gelu_grad(v):
    return 0.5 * (1.0 + lax.erf(v * (2.0 ** -0.5))) + v * jnp.exp(-0.5 * v * v) * (1.0 / math.sqrt(2.0 * math.pi))


_DIMS = {"nn": (((1,), (0,)), ((), ())), "nt": (((1,), (1,)), ((), ())), "tn": (((0,), (0,)), ((), ()))}


def _mm(a, b, mode, outs, name, *, extras=(), epilogue=None, full_n=False):
    if mode == "nn":
        (M, K), (K2, N) = a.shape, b.shape
    elif mode == "nt":
        (M, K), (N, K2) = a.shape, b.shape
    else:
        (K, M), (K2, N) = a.shape, b.shape
    assert K == K2, (a.shape, b.shape, mode)
    tm = _tile(M, (1024, 512, 256, 128))
    tn = N if full_n else _tile(N, (1024, 768, 512, 384, 256, 128))
    if full_n:
        tm = _tile(M, (512, 256, 128))
    tk = _tile(K, (512, 384, 256, 128))
    nk = K // tk
    if mode == "nn":
        a_spec = pl.BlockSpec((tm, tk), lambda i, j, k: (i, k))
        b_spec = pl.BlockSpec((tk, tn), lambda i, j, k: (k, j))
    elif mode == "nt":
        a_spec = pl.BlockSpec((tm, tk), lambda i, j, k: (i, k))
        b_spec = pl.BlockSpec((tn, tk), lambda i, j, k: (j, k))
    else:
        a_spec = pl.BlockSpec((tk, tm), lambda i, j, k: (k, i))
        b_spec = pl.BlockSpec((tk, tn), lambda i, j, k: (k, j))
    in_specs = [a_spec, b_spec]
    for arr, kind in extras:
        if kind == "tile":
            in_specs.append(pl.BlockSpec((tm, tn), lambda i, j, k: (i, j)))
        else:
            in_specs.append(pl.BlockSpec((1, tn), lambda i, j, k: (0, j)))
    out_specs, out_shape = [], []
    for dt, kind in outs:
        if kind == "n":
            out_specs.append(pl.BlockSpec((tm, tn), lambda i, j, k: (i, j)))
            out_shape.append(jax.ShapeDtypeStruct((M, N), dt))
        else:
            assert tn == N
            out_specs.append(pl.BlockSpec((tm, 1), lambda i, j, k: (i, 0)))
            out_shape.append(jax.ShapeDtypeStruct((M, 1), dt))
    ne, no = len(extras), len(outs)
    dims = _DIMS[mode]
    if epilogue is None:
        epilogue = lambda acc: (acc,) * no

    def body(*refs):
        a_ref, b_ref = refs[0], refs[1]
        ex = refs[2:2 + ne]
        out = refs[2 + ne:2 + ne + no]
        acc = refs[-1]
        k = pl.program_id(2)

        @pl.when(k == 0)
        def _():
            acc[...] = jnp.zeros_like(acc)

        acc[...] += lax.dot_general(a_ref[...].astype(BF16), b_ref[...].astype(BF16), dims,
                                    preferred_element_type=F32)

        @pl.when(k == nk - 1)
        def _():
            res = epilogue(acc[...], *[e[...] for e in ex])
            for o, r in zip(out, res):
                o[...] = r.astype(o.dtype)

    res = pl.pallas_call(
        body, name=name, grid=(M // tm, N // tn, nk), in_specs=in_specs, out_specs=out_specs, out_shape=out_shape,
        scratch_shapes=[pltpu.VMEM((tm, tn), F32)],
        compiler_params=_params(("parallel", "parallel", "arbitrary")),
    )(a, b, *[e[0] for e in extras])
    return res


def _ln_epilogue(alpha):
    def epi(acc, x, g, b):
        r = alpha * x + acc
        mu = jnp.mean(r, axis=-1, keepdims=True)
        d = r - mu
        var = jnp.mean(d * d, axis=-1, keepdims=True)
        rstd = lax.rsqrt(var + EPS)
        xhat = d * rstd
        y = xhat * g + b
        return y, y, xhat, rstd
    return epi


def _mm_ln(a, w, x, g, b, alpha, name):
    return _mm(a, w, "nn", [(F32, "n"), (BF16, "n"), (F32, "n"), (F32, "1")], name,
               extras=[(x, "tile"), (g, "row"), (b, "row")], epilogue=_ln_epilogue(alpha), full_n=True)


def _mm_axpy(a, w, mode, r, alpha, name):
    return _mm(a, w, mode, [(F32, "n")], name, extras=[(r, "tile")],
               epilogue=lambda acc, rv: (acc + alpha * rv,))[0]


def _ln_bwd(dy, xhat, rstd, g, name):
    S, D = dy.shape
    tm = _tile(S, (256, 128))

    def body(dy_ref, xh_ref, rs_ref, g_ref, dr_ref, dr16_ref, dg_ref, db_ref):
        @pl.when(pl.program_id(0) == 0)
        def _():
            dg_ref[...] = jnp.zeros_like(dg_ref)
            db_ref[...] = jnp.zeros_like(db_ref)

        dyv, xh = dy_ref[...], xh_ref[...]
        dxh = dyv * g_ref[...]
        m1 = jnp.mean(dxh, axis=-1, keepdims=True)
        m2 = jnp.mean(dxh * xh, axis=-1, keepdims=True)
        dr = rs_ref[...] * (dxh - m1 - xh * m2)
        dr_ref[...] = dr
        dr16_ref[...] = dr.astype(BF16)
        dg_ref[...] += jnp.sum(dyv * xh, axis=0, keepdims=True)
        db_ref[...] += jnp.sum(dyv, axis=0, keepdims=True)

    row = pl.BlockSpec((tm, D), lambda i: (i, 0))
    return pl.pallas_call(
        body, name=name, grid=(S // tm,),
        in_specs=[row, row, pl.BlockSpec((tm, 1), lambda i: (i, 0)), _full((1, D))],
        out_specs=[row, row, _full((1, D)), _full((1, D))],
        out_shape=[jax.ShapeDtypeStruct((S, D), F32), jax.ShapeDtypeStruct((S, D), BF16),
                   jax.ShapeDtypeStruct((1, D), F32), jax.ShapeDtypeStruct((1, D), F32)],
        compiler_params=_params(("arbitrary",)),
    )(dy, xhat, rstd, g)


def _rms_fwd(h, off, width, g, name):
    S = h.shape[0]
    tm = _tile(S, (ROW_TILE, 256, 128))

    def body(c_ref, g_ref, y_ref, r_ref):
        c = c_ref[...]
        r = lax.rsqrt(jnp.mean(c * c, axis=-1, keepdims=True) + EPS)
        y_ref[...] = (c * r * g_ref[...]).astype(BF16)
        r_ref[...] = r

    return pl.pallas_call(
        body, name=name, grid=(S // tm,),
        in_specs=[_cols(tm, width, off), _full((1, width))],
        out_specs=[pl.BlockSpec((tm, width), lambda i: (i, 0)), pl.BlockSpec((tm, 1), lambda i: (i, 0))],
        out_shape=[jax.ShapeDtypeStruct((S, width), BF16), jax.ShapeDtypeStruct((S, 1), F32)],
        compiler_params=_params(("parallel",)),
    )(h, g)


def _rms_bwd(dy, h, off, width, rstd, g, name):
    S = h.shape[0]
    tm = _tile(S, (ROW_TILE, 256, 128))

    def body(dy_ref, c_ref, r_ref, g_ref, dc_ref, dg_ref):
        @pl.when(pl.program_id(0) == 0)
        def _():
            dg_ref[...] = jnp.zeros_like(dg_ref)

        dyv, c, r = dy_ref[...], c_ref[...], r_ref[...]
        dyg = dyv * g_ref[...]
        m = jnp.mean(dyg * c, axis=-1, keepdims=True)
        dc_ref[...] = (r * dyg - c * (r * r * r) * m).astype(BF16)
        dg_ref[...] += jnp.sum(dyv * c * r, axis=0, keepdims=True)

    return pl.pallas_call(
        body, name=name, grid=(S // tm,),
        in_specs=[pl.BlockSpec((tm, width), lambda i: (i, 0)), _cols(tm, width, off),
                  pl.BlockSpec((tm, 1), lambda i: (i, 0)), _full((1, width))],
        out_specs=[pl.BlockSpec((tm, width), lambda i: (i, 0)), _full((1, width))],
        out_shape=[jax.ShapeDtypeStruct((S, width), BF16), jax.ShapeDtypeStruct((1, width), F32)],
        compiler_params=_params(("arbitrary",)),
    )(dy, h, rstd, g)


def _loss(y, target, name):
    S, D = y.shape
    tm = _tile(S, (256, 128))

    def body(y_ref, t_ref, l_ref, dy_ref):
        @pl.when(pl.program_id(0) == 0)
        def _():
            l_ref[...] = jnp.zeros_like(l_ref)

        err = y_ref[...] - t_ref[...]
        dy_ref[...] = err * (1.0 / D)
        per_tok = jnp.mean(err * err, axis=-1, keepdims=True)
        l_ref[...] += 0.5 * jnp.sum(per_tok, axis=0, keepdims=True)

    row = pl.BlockSpec((tm, D), lambda i: (i, 0))
    return pl.pallas_call(
        body, name=name, grid=(S // tm,), in_specs=[row, row], out_specs=[_full((1, 1)), row],
        out_shape=[jax.ShapeDtypeStruct((1, 1), F32), jax.ShapeDtypeStruct((S, D), F32)],
        compiler_params=_params(("arbitrary",)),
    )(y, target)


def _rope(a1, a2, cs, sn, n, bwd, name):
    S = cs.shape[0]
    tm = _tile(S, (ROW_TILE, 256, 128))
    stacked = not isinstance(a1, tuple)

    def body(a1_ref, a2_ref, c_ref, s_ref, y1_ref, y2_ref):
        if stacked:
            v1 = jnp.sum(a1_ref[...], axis=0)
            v2 = jnp.sum(a2_ref[...], axis=0)
        else:
            v1, v2 = a1_ref[...].astype(F32), a2_ref[...].astype(F32)
        c, s = c_ref[...], s_ref[...]
        if bwd:
            y1_ref[...] = (v1 * c + v2 * s).astype(BF16)
            y2_ref[...] = (v2 * c - v1 * s).astype(BF16)
        else:
            y1_ref[...] = (v1 * c - v2 * s).astype(BF16)
            y2_ref[...] = (v2 * c + v1 * s).astype(BF16)

    row = pl.BlockSpec((tm, n), lambda i: (i, 0))
    if stacked:
        H = a1.shape[0]
        a_specs = [pl.BlockSpec((H, tm, n), lambda i: (0, i, 0))] * 2
        arrs = [a1, a2]
    else:
        a_specs = [_cols(tm, n, a1[1]), _cols(tm, n, a2[1])]
        arrs = [a1[0], a2[0]]
    return pl.pallas_call(
        body, name=name, grid=(S // tm,), in_specs=a_specs + [row, row], out_specs=[row, row],
        out_shape=[jax.ShapeDtypeStruct((S, n), BF16)] * 2,
        compiler_params=_params(("parallel",)),
    )(*arrs, cs, sn)


def _swa_mask(n):
    blk = SWA_BLOCK
    row = lax.broadcasted_iota(jnp.int32, (blk, 2 * blk), 0)
    col = lax.broadcasted_iota(jnp.int32, (blk, 2 * blk), 1)
    rel = row + blk - col
    return (rel >= 0) & (rel < blk) & ((n > 0) | (col >= blk))


def _swa_specs(off_q, off_k, off_v):
    blk, aq, akv = SWA_BLOCK, SWA_Q_HEADS * SWA_HEAD_DIM, SWA_KV_HEADS * SWA_HEAD_DIM
    assert off_q % aq == 0 and off_k % akv == 0 and off_v % akv == 0
    prev = lambda off: pl.BlockSpec((blk, akv), lambda n: (jnp.maximum(n - 1, 0), off // akv))
    cur = lambda off: pl.BlockSpec((blk, akv), lambda n: (n, off // akv))
    return [pl.BlockSpec(memory_space=pltpu.SMEM), _cols(blk, aq, off_q), prev(off_k), cur(off_k), prev(off_v), cur(off_v)]


def _swa_fwd(h, off_q, off_k, off_v, sinks, name):
    S = h.shape[0]
    blk, hd, nh = SWA_BLOCK, SWA_HEAD_DIM, SWA_Q_HEADS
    grp = nh // SWA_KV_HEADS
    aq = nh * hd
    scale = hd ** -0.5

    def body(sink_ref, q_ref, kp_ref, kc_ref, vp_ref, vc_ref, o_ref, lse_ref):
        valid = _swa_mask(pl.program_id(0))
        q = q_ref[...].astype(BF16)
        k2 = jnp.concatenate([kp_ref[...], kc_ref[...]], axis=0).astype(BF16)
        v2 = jnp.concatenate([vp_ref[...], vc_ref[...]], axis=0).astype(BF16)
        for hh in range(nh):
            kv = hh // grp
            qh = q[:, hh * hd:(hh + 1) * hd]
            kh = k2[:, kv * hd:(kv + 1) * hd]
            vh = v2[:, kv * hd:(kv + 1) * hd]
            s = lax.dot_general(qh, kh, _DIMS["nt"], preferred_element_type=F32) * scale
            s = jnp.where(valid, s, MASK)
            sk = sink_ref[hh]
            m = jnp.maximum(jnp.max(s, axis=1, keepdims=True), sk)
            p = jnp.exp(s - m)
            l = jnp.sum(p, axis=1, keepdims=True) + jnp.exp(sk - m)
            o_ref[:, hh * hd:(hh + 1) * hd] = jnp.dot((p / l).astype(BF16), vh, preferred_element_type=F32)
            lse_ref[:, hh:hh + 1] = m + jnp.log(l)

    return pl.pallas_call(
        body, name=name, grid=(S // blk,), in_specs=_swa_specs(off_q, off_k, off_v),
        out_specs=[pl.BlockSpec((blk, aq), lambda n: (n, 0)), pl.BlockSpec((blk, nh), lambda n: (n, 0))],
        out_shape=[jax.ShapeDtypeStruct((S, aq), F32), jax.ShapeDtypeStruct((S, nh), F32)],
        compiler_params=_params(("parallel",)),
    )(sinks, h, h, h, h, h)


def _swa_bwd(h, off_q, off_k, off_v, sinks, dout, lse, name):
    S = h.shape[0]
    blk, hd, nh, nkv = SWA_BLOCK, SWA_HEAD_DIM, SWA_Q_HEADS, SWA_KV_HEADS
    grp = nh // nkv
    aq, akv = nh * hd, nkv * hd
    scale = hd ** -0.5

    def body(sink_ref, q_ref, kp_ref, kc_ref, vp_ref, vc_ref, do_ref, lse_ref, dq_ref, dk_ref, dv_ref, ds_ref):
        n = pl.program_id(0)

        @pl.when(n == 0)
        def _():
            dk_ref[...] = jnp.zeros_like(dk_ref)
            dv_ref[...] = jnp.zeros_like(dv_ref)
            ds_ref[...] = jnp.zeros_like(ds_ref)

        valid = _swa_mask(n)
        q = q_ref[...].astype(BF16)
        k2 = jnp.concatenate([kp_ref[...], kc_ref[...]], axis=0).astype(BF16)
        v2 = jnp.concatenate([vp_ref[...], vc_ref[...]], axis=0).astype(BF16)
        do = do_ref[...]
        lane = lax.broadcasted_iota(jnp.int32, (1, 128), 1)
        dsink = jnp.zeros((1, 128), F32)
        cur = pl.ds(pl.multiple_of(n * blk, blk), blk)
        prev = pl.ds(pl.multiple_of(jnp.maximum(n - 1, 0) * blk, blk), blk)
        for kv in range(nkv):
            kh = k2[:, kv * hd:(kv + 1) * hd]
            vh = v2[:, kv * hd:(kv + 1) * hd]
            dk_acc = jnp.zeros((2 * blk, hd), F32)
            dv_acc = jnp.zeros((2 * blk, hd), F32)
            for g in range(grp):
                hh = kv * grp + g
                qh = q[:, hh * hd:(hh + 1) * hd]
                doh = do[:, hh * hd:(hh + 1) * hd]
                doh16 = doh.astype(BF16)
                lse_h = lse_ref[:, hh:hh + 1]
                s = lax.dot_general(qh, kh, _DIMS["nt"], preferred_element_type=F32) * scale
                s = jnp.where(valid, s, MASK)
                p = jnp.exp(s - lse_h)
                p16 = p.astype(BF16)
                o = jnp.dot(p16, vh, preferred_element_type=F32)
                delta = jnp.sum(doh * o, axis=1, keepdims=True)
                dp = lax.dot_general(doh16, vh, _DIMS["nt"], preferred_element_type=F32)
                ds16 = (p * (dp - delta) * scale).astype(BF16)
                dq_ref[:, hh * hd:(hh + 1) * hd] = jnp.dot(ds16, kh, preferred_element_type=F32).astype(BF16)
                dk_acc += lax.dot_general(ds16, qh, _DIMS["tn"], preferred_element_type=F32)
                dv_acc += lax.dot_general(p16, doh16, _DIMS["tn"], preferred_element_type=F32)
                dsk = -jnp.sum(jnp.exp(sink_ref[hh] - lse_h) * delta, axis=0, keepdims=True)
                dsink += jnp.where(lane == hh, dsk, 0.0)
            cols = slice(kv * hd, (kv + 1) * hd)
            dk_ref[cur, cols] += dk_acc[blk:]
            dv_ref[cur, cols] += dv_acc[blk:]

            @pl.when(n > 0)
            def _():
                dk_ref[prev, cols] += dk_acc[:blk]
                dv_ref[prev, cols] += dv_acc[:blk]

        ds_ref[...] += dsink

    return pl.pallas_call(
        body, name=name, grid=(S // blk,),
        in_specs=_swa_specs(off_q, off_k, off_v) + [pl.BlockSpec((blk, aq), lambda n: (n, 0)),
                                                    pl.BlockSpec((blk, nh), lambda n: (n, 0))],
        out_specs=[pl.BlockSpec((blk, aq), lambda n: (n, 0)), _full((S, akv)), _full((S, akv)), _full((1, 128))],
        out_shape=[jax.ShapeDtypeStruct((S, aq), BF16), jax.ShapeDtypeStruct((S, akv), F32),
                   jax.ShapeDtypeStruct((S, akv), F32), jax.ShapeDtypeStruct((1, 128), F32)],
        compiler_params=_params(("arbitrary",)),
    )(sinks, h, h, h, h, h, dout, lse)


def _causal(i, j, t):
    row = i * t + lax.broadcasted_iota(jnp.int32, (t, t), 0)
    col = j * t + lax.broadcasted_iota(jnp.int32, (t, t), 1)
    return col <= row


def _mla_fwd(q, k, v, name):
    H, S, dq = q.shape
    dv = v.shape[2]
    t = _tile(S, (MLA_TILE, 256, 128))
    nq = S // t
    scale = (MLA_NOPE + MLA_ROPE) ** -0.5

    def body(q_ref, k_ref, v_ref, o_ref, lse_ref, m_s, l_s, acc_s):
        i, j = pl.program_id(1), pl.program_id(2)

        @pl.when(j == 0)
        def _():
            m_s[...] = jnp.full_like(m_s, -jnp.inf)
            l_s[...] = jnp.zeros_like(l_s)
            acc_s[...] = jnp.zeros_like(acc_s)

        @pl.when(j <= i)
        def _():
            s = lax.dot_general(q_ref[0], k_ref[0], _DIMS["nt"], preferred_element_type=F32) * scale
            s = jnp.where(_causal(i, j, t), s, MASK)
            m_new = jnp.maximum(m_s[...], jnp.max(s, axis=1, keepdims=True))
            alpha = jnp.exp(m_s[...] - m_new)
            p = jnp.exp(s - m_new)
            l_s[...] = alpha * l_s[...] + jnp.sum(p, axis=1, keepdims=True)
            acc_s[...] = alpha * acc_s[...] + jnp.dot(p.astype(BF16), v_ref[0], preferred_element_type=F32)
            m_s[...] = m_new

        @pl.when(j == i)
        def _():
            o_ref[0] = acc_s[...] / l_s[...]
            lse_ref[0] = m_s[...] + jnp.log(l_s[...])

    kv_map = lambda h, i, j: (h, jnp.minimum(j, i), 0)
    return pl.pallas_call(
        body, name=name, grid=(H, nq, nq),
        in_specs=[pl.BlockSpec((1, t, dq), lambda h, i, j: (h, i, 0)), pl.BlockSpec((1, t, dq), kv_map),
                  pl.BlockSpec((1, t, dv), kv_map)],
        out_specs=[pl.BlockSpec((1, t, dv), lambda h, i, j: (h, i, 0)), pl.BlockSpec((1, t, 1), lambda h, i, j: (h, i, 0))],
        out_shape=[jax.ShapeDtypeStruct((H, S, dv), F32), jax.ShapeDtypeStruct((H, S, 1), F32)],
        scratch_shapes=[pltpu.VMEM((t, 1), F32), pltpu.VMEM((t, 1), F32), pltpu.VMEM((t, dv), F32)],
        compiler_params=_params(("parallel", "parallel", "arbitrary")),
    )(q, k, v)


def _rowdot(a, b, name):
    H, S, d = a.shape
    t = _tile(S, (ROW_TILE, 256, 128))

    def body(a_ref, b_ref, o_ref):
        o_ref[...] = jnp.sum(a_ref[...] * b_ref[...], axis=-1, keepdims=True)

    spec = pl.BlockSpec((1, t, d), lambda h, i: (h, i, 0))
    return pl.pallas_call(
        body, name=name, grid=(H, S // t), in_specs=[spec, spec],
        out_specs=pl.BlockSpec((1, t, 1), lambda h, i: (h, i, 0)),
        out_shape=jax.ShapeDtypeStruct((H, S, 1), F32),
        compiler_params=_params(("parallel", "parallel")),
    )(a, b)


def _mla_bwd(q, k, v, do, lse, delta, name):
    H, S, dq = q.shape
    dv = v.shape[2]
    t = _tile(S, (MLA_TILE, 256, 128))
    nq = S // t
    scale = (MLA_NOPE + MLA_ROPE) ** -0.5

    def body(q_ref, k_ref, v_ref, do_ref, lse_ref, dl_ref, dq_ref, dk_ref, dv_ref, dk_s, dv_s):
        j, i = pl.program_id(1), pl.program_id(2)

        @pl.when((j == 0) & (i == 0))
        def _():
            dq_ref[...] = jnp.zeros_like(dq_ref)

        @pl.when(i == j)
        def _():
            dk_s[...] = jnp.zeros_like(dk_s)
            dv_s[...] = jnp.zeros_like(dv_s)

        @pl.when(i >= j)
        def _():
            qv, kv_, vv, dov = q_ref[0], k_ref[0], v_ref[0], do_ref[0]
            s = lax.dot_general(qv, kv_, _DIMS["nt"], preferred_element_type=F32) * scale
            s = jnp.where(_causal(i, j, t), s, MASK)
            p = jnp.exp(s - lse_ref[0])
            p16 = p.astype(BF16)
            dp = lax.dot_general(dov, vv, _DIMS["nt"], preferred_element_type=F32)
            ds16 = (p * (dp - dl_ref[0]) * scale).astype(BF16)
            dv_s[...] += lax.dot_general(p16, dov, _DIMS["tn"], preferred_element_type=F32)
            dk_s[...] += lax.dot_general(ds16, qv, _DIMS["tn"], preferred_element_type=F32)
            rows = pl.ds(pl.multiple_of(i * t, t), t)
            dq_ref[0, rows, :] += jnp.dot(ds16, kv_, preferred_element_type=F32)

        @pl.when(i == nq - 1)
        def _():
            dk_ref[0] = dk_s[...]
            dv_ref[0] = dv_s[...]

    q_map = lambda h, j, i: (h, jnp.maximum(i, j), 0)
    kv_map = lambda h, j, i: (h, j, 0)
    return pl.pallas_call(
        body, name=name, grid=(H, nq, nq),
        in_specs=[pl.BlockSpec((1, t, dq), q_map), pl.BlockSpec((1, t, dq), kv_map), pl.BlockSpec((1, t, dv), kv_map),
                  pl.BlockSpec((1, t, dv), q_map), pl.BlockSpec((1, t, 1), q_map), pl.BlockSpec((1, t, 1), q_map)],
        out_specs=[pl.BlockSpec((1, S, dq), lambda h, j, i: (h, 0, 0)), pl.BlockSpec((1, t, dq), kv_map),
                   pl.BlockSpec((1, t, dv), kv_map)],
        out_shape=[jax.ShapeDtypeStruct((H, S, dq), F32), jax.ShapeDtypeStruct((H, S, dq), F32),
                   jax.ShapeDtypeStruct((H, S, dv), F32)],
        scratch_shapes=[pltpu.VMEM((t, dq), F32), pltpu.VMEM((t, dv), F32)],
        compiler_params=_params(("parallel", "arbitrary", "arbitrary")),
    )(q, k, v, do, lse, delta)


def _sgu_norm(hv, lg, lb):
    vg = _gelu(hv)
    mu = jnp.mean(vg, axis=-1, keepdims=True)
    d = vg - mu
    rstd = lax.rsqrt(jnp.mean(d * d, axis=-1, keepdims=True) + EPS)
    xhat = d * rstd
    return xhat, rstd, xhat * lg + lb


def _sgu_fwd(h, off_u, off_v, lg, lb, w16, bt, name):
    S = h.shape[0]
    T, G, C = SGU_CHUNK, SGU_GROUPS, SGU_DIM
    W = G * C

    def body(hu_ref, hv_ref, lg_ref, lb_ref, w_ref, bt_ref, y_ref):
        u = _gelu(hu_ref[...])
        _, _, vn = _sgu_norm(hv_ref[...], lg_ref[...], lb_ref[...])
        vn16 = vn.astype(BF16)
        for g in range(G):
            cols = slice(g * C, (g + 1) * C)
            mixed = jnp.dot(w_ref[g], vn16[:, cols], preferred_element_type=F32) + bt_ref[:, g:g + 1]
            y_ref[:, cols] = (u[:, cols] * mixed).astype(BF16)

    return pl.pallas_call(
        body, name=name, grid=(S // T,),
        in_specs=[_cols(T, W, off_u), _cols(T, W, off_v), _full((1, W)), _full((1, W)), _full((G, T, T)), _full((T, G))],
        out_specs=pl.BlockSpec((T, W), lambda n: (n, 0)),
        out_shape=jax.ShapeDtypeStruct((S, W), BF16),
        compiler_params=_params(("parallel",)),
    )(h, h, lg, lb, w16, bt)


def _sgu_bwd(h, off_u, off_v, lg, lb, w16, bt, dy, name):
    S = h.shape[0]
    T, G, C = SGU_CHUNK, SGU_GROUPS, SGU_DIM
    W = G * C
    nc = S // T

    def body(hu_ref, hv_ref, lg_ref, lb_ref, w_ref, bt_ref, dy_ref, dhu_ref, dhv_ref, dw_ref, db_ref, dlg_ref, dlb_ref,
             dmix_s, dvn_s):
        n = pl.program_id(0)

        @pl.when(n == 0)
        def _():
            dw_ref[...] = jnp.zeros_like(dw_ref)
            dlg_ref[...] = jnp.zeros_like(dlg_ref)
            dlb_ref[...] = jnp.zeros_like(dlb_ref)
            dmix_s[...] = jnp.zeros_like(dmix_s)

        hu, hv, lgv = hu_ref[...], hv_ref[...], lg_ref[...]
        u = _gelu(hu)
        xhat, rstd, vn = _sgu_norm(hv, lgv, lb_ref[...])
        vn16 = vn.astype(BF16)
        dyv = dy_ref[...]
        dmixed = dyv * u
        dmix_s[...] += dmixed
        dmixed16 = dmixed.astype(BF16)
        for g in range(G):
            cols = slice(g * C, (g + 1) * C)
            mixed = jnp.dot(w_ref[g], vn16[:, cols], preferred_element_type=F32) + bt_ref[:, g:g + 1]
            dhu_ref[:, cols] = (dyv[:, cols] * mixed * _gelu_grad(hu[:, cols])).astype(BF16)
            dvn_s[:, cols] = lax.dot_general(w_ref[g], dmixed16[:, cols], _DIMS["tn"], preferred_element_type=F32)
            dw_ref[g] += lax.dot_general(dmixed16[:, cols], vn16[:, cols], _DIMS["nt"], preferred_element_type=F32)
        dvn = dvn_s[...]
        dlg_ref[...] += jnp.sum(dvn * xhat, axis=0, keepdims=True)
        dlb_ref[...] += jnp.sum(dvn, axis=0, keepdims=True)
        dxh = dvn * lgv
        m1 = jnp.mean(dxh, axis=-1, keepdims=True)
        m2 = jnp.mean(dxh * xhat, axis=-1, keepdims=True)
        dvg = rstd * (dxh - m1 - xhat * m2)
        dhv_ref[...] = (dvg * _gelu_grad(hv)).astype(BF16)

        @pl.when(n == nc - 1)
        def _():
            tril = lax.broadcasted_iota(jnp.int32, (T, T), 1) <= lax.broadcasted_iota(jnp.int32, (T, T), 0)
            lane = lax.broadcasted_iota(jnp.int32, (T, 128), 1)
            db = jnp.zeros((T, 128), F32)
            for g in range(G):
                dw_ref[g] = jnp.where(tril, dw_ref[g], 0.0)
                db += jnp.where(lane == g, jnp.sum(dmix_s[:, g * C:(g + 1) * C], axis=1, keepdims=True), 0.0)
            db_ref[...] = db

    row = pl.BlockSpec((T, W), lambda n: (n, 0))
    return pl.pallas_call(
        body, name=name, grid=(nc,),
        in_specs=[_cols(T, W, off_u), _cols(T, W, off_v), _full((1, W)), _full((1, W)), _full((G, T, T)), _full((T, G)), row],
        out_specs=[row, row, _full((G, T, T)), _full((T, 128)), _full((1, W)), _full((1, W))],
        out_shape=[jax.ShapeDtypeStruct((S, W), BF16), jax.ShapeDtypeStruct((S, W), BF16),
                   jax.ShapeDtypeStruct((G, T, T), F32), jax.ShapeDtypeStruct((T, 128), F32),
                   jax.ShapeDtypeStruct((1, W), F32), jax.ShapeDtypeStruct((1, W), F32)],
        scratch_shapes=[pltpu.VMEM((T, W), F32), pltpu.VMEM((T, W), F32)],
        compiler_params=_params(("arbitrary",)),
    )(h, h, lg, lb, w16, bt, dy)


def _merge_fwd(ys, ps, h, bg, name):
    S = h.shape[0]
    D = ps[0].shape[1]
    tm = _tile(S, (512, 256, 128))
    tn = _tile(D, (512, 256, 128))
    nb = len(ys)

    def body(*refs):
        y_refs, p_refs, l_refs = refs[:nb], refs[nb:2 * nb], refs[2 * nb:3 * nb]
        bg_ref, mg_ref, z_ref = refs[3 * nb:]
        acc = jnp.zeros((tm, tn), F32)
        for b in range(nb):
            z = jnp.dot(y_refs[b][...].astype(BF16), p_refs[b][...], preferred_element_type=F32)
            z_ref[b] = z
            acc += _sigmoid(l_refs[b][...] + bg_ref[b:b + 1, :]) * z
        mg_ref[...] = acc.astype(BF16)

    in_specs = [pl.BlockSpec((tm, y.shape[1]), lambda i, j: (i, 0)) for y in ys]
    in_specs += [pl.BlockSpec((p.shape[0], tn), lambda i, j: (0, j)) for p in ps]
    in_specs += [pl.BlockSpec((tm, tn), functools.partial(lambda i, j, b: (i, b * (D // tn) + j), b=b)) for b in range(nb)]
    in_specs += [pl.BlockSpec((nb, tn), lambda i, j: (0, j))]
    return pl.pallas_call(
        body, name=name, grid=(S // tm, D // tn), in_specs=in_specs,
        out_specs=[pl.BlockSpec((tm, tn), lambda i, j: (i, j)), pl.BlockSpec((nb, tm, tn), lambda i, j: (0, i, j))],
        out_shape=[jax.ShapeDtypeStruct((S, D), BF16), jax.ShapeDtypeStruct((nb, S, D), F32)],
        compiler_params=_params(("parallel", "parallel")),
    )(*ys, *ps, *([h] * nb), bg)


def _merge_bwd(dm, z, h, bg, name):
    nb, S, D = z.shape
    tm = _tile(S, (256, 128))
    tn = _tile(D, (512, 256, 128))

    def body(*refs):
        dm_ref, z_ref = refs[0], refs[1]
        l_refs = refs[2:2 + nb]
        bg_ref, dz_ref, dl_ref, dbg_ref = refs[2 + nb:]

        @pl.when(pl.program_id(1) == 0)
        def _():
            dbg_ref[...] = jnp.zeros_like(dbg_ref)

        dmv = dm_ref[...]
        rows = lax.broadcasted_iota(jnp.int32, (SUBLANES, tn), 0)
        dbg = jnp.zeros((SUBLANES, tn), F32)
        for b in range(nb):
            gt = _sigmoid(l_refs[b][...] + bg_ref[b:b + 1, :])
            dz_ref[b] = (dmv * gt).astype(BF16)
            dl = dmv * z_ref[b] * gt * (1.0 - gt)
            dl_ref[b] = dl.astype(BF16)
            dbg += jnp.where(rows == b, jnp.sum(dl, axis=0, keepdims=True), 0.0)
        dbg_ref[...] += dbg

    in_specs = [pl.BlockSpec((tm, tn), lambda j, i: (i, j)), pl.BlockSpec((nb, tm, tn), lambda j, i: (0, i, j))]
    in_specs += [pl.BlockSpec((tm, tn), functools.partial(lambda j, i, b: (i, b * (D // tn) + j), b=b)) for b in range(nb)]
    in_specs += [pl.BlockSpec((nb, tn), lambda j, i: (0, j))]
    blk3 = pl.BlockSpec((nb, tm, tn), lambda j, i: (0, i, j))
    return pl.pallas_call(
        body, name=name, grid=(D // tn, S // tm), in_specs=in_specs,
        out_specs=[blk3, blk3, pl.BlockSpec((SUBLANES, tn), lambda j, i: (0, j))],
        out_shape=[jax.ShapeDtypeStruct((nb, S, D), BF16), jax.ShapeDtypeStruct((nb, S, D), BF16),
                   jax.ShapeDtypeStruct((SUBLANES, D), F32)],
        compiler_params=_params(("parallel", "arbitrary")),
    )(dm, z, *([h] * nb), bg)


def _shift_down(x, halo, k):
    xr = pltpu.roll(x, k, 0)
    hr = pltpu.roll(halo, k, 0)
    rows = lax.broadcasted_iota(jnp.int32, halo.shape, 0)
    top = jnp.where(rows < k, hr, xr[:SUBLANES])
    return jnp.concatenate([top, xr[SUBLANES:]], axis=0)


def _shift_up(x, halo, k):
    tm = x.shape[0]
    xr = pltpu.roll(x, tm - k, 0)
    hr = pltpu.roll(halo, SUBLANES - k, 0)
    rows = lax.broadcasted_iota(jnp.int32, halo.shape, 0)
    bot = jnp.where(rows >= SUBLANES - k, hr, xr[tm - SUBLANES:])
    return jnp.concatenate([xr[:tm - SUBLANES], bot], axis=0)


def _conv_tiles(S, F):
    return _tile(S, (ROW_TILE, 256, 128)), _tile(F, (512, 256, 128))


def _conv_in_specs(tm, tn, F):
    r8 = tm // SUBLANES
    nf = F // tn
    specs = []
    for half in range(2):
        specs.append(pl.BlockSpec((tm, tn), functools.partial(lambda j, i, o: (i, o + j), o=half * nf)))
        specs.append(pl.BlockSpec((SUBLANES, tn), functools.partial(lambda j, i, o: (jnp.maximum(i * r8 - 1, 0), o + j), o=half * nf)))
    for half in range(2):
        specs.append(pl.BlockSpec((3, tn), functools.partial(lambda j, i, o: (0, o + j), o=half * nf)))
        specs.append(pl.BlockSpec((1, tn), functools.partial(lambda j, i, o: (0, o + j), o=half * nf)))
    return specs


def _conv_apply(x, halo, w, b, first):
    halo = jnp.where(first, 0.0, halo)
    x1 = _shift_down(x, halo, 1)
    x2 = _shift_down(x, halo, 2)
    return b + x2 * w[0:1, :] + x1 * w[1:2, :] + x * w[2:3, :], x1, x2


def _glu_fwd(up, cw, cb, name):
    S, F2 = up.shape
    F = F2 // 2
    tm, tn = _conv_tiles(S, F)

    def body(ug, hg, uv, hv, wg, bgr, wv, bvr, a_ref):
        first = pl.program_id(1) == 0
        cg, _, _ = _conv_apply(ug[...], hg[...], wg[...], bgr[...], first)
        cv, _, _ = _conv_apply(uv[...], hv[...], wv[...], bvr[...], first)
        a_ref[...] = (cg * _sigmoid(cg) * cv).astype(BF16)

    return pl.pallas_call(
        body, name=name, grid=(F // tn, S // tm), in_specs=_conv_in_specs(tm, tn, F),
        out_specs=pl.BlockSpec((tm, tn), lambda j, i: (i, j)),
        out_shape=jax.ShapeDtypeStruct((S, F), BF16),
        compiler_params=_params(("parallel", "parallel")),
    )(up, up, up, up, cw, cb, cw, cb)


def _glu_bwd(up, cw, cb, da, name):
    S, F2 = up.shape
    F = F2 // 2
    tm, tn = _conv_tiles(S, F)

    def body(ug, hg, uv, hv, wg, bgr, wv, bvr, da_ref, dg_ref, dv_ref, sg_ref, sv_ref):
        i = pl.program_id(1)

        @pl.when(i == 0)
        def _():
            sg_ref[...] = jnp.zeros_like(sg_ref)
            sv_ref[...] = jnp.zeros_like(sv_ref)

        first = i == 0
        xg, xv = ug[...], uv[...]
        cg, xg1, xg2 = _conv_apply(xg, hg[...], wg[...], bgr[...], first)
        cv, xv1, xv2 = _conv_apply(xv, hv[...], wv[...], bvr[...], first)
        dav = da_ref[...]
        sg = _sigmoid(cg)
        dcv = dav * cg * sg
        dcg = dav * cv * sg * (1.0 + cg * (1.0 - sg))
        dg_ref[...] = dcg
        dv_ref[...] = dcv
        rows = lax.broadcasted_iota(jnp.int32, (SUBLANES, tn), 0)

        def stats(dc, x, x1, x2):
            acc = jnp.zeros((SUBLANES, tn), F32)
            for r, val in enumerate((dc * x2, dc * x1, dc * x, dc)):
                acc += jnp.where(rows == r, jnp.sum(val, axis=0, keepdims=True), 0.0)
            return acc

        sg_ref[...] += stats(dcg, xg, xg1, xg2)
        sv_ref[...] += stats(dcv, xv, xv1, xv2)

    tile = pl.BlockSpec((tm, tn), lambda j, i: (i, j))
    stat = pl.BlockSpec((SUBLANES, tn), lambda j, i: (0, j))
    return pl.pallas_call(
        body, name=name, grid=(F // tn, S // tm), in_specs=_conv_in_specs(tm, tn, F) + [tile],
        out_specs=[tile, tile, stat, stat],
        out_shape=[jax.ShapeDtypeStruct((S, F), F32), jax.ShapeDtypeStruct((S, F), F32),
                   jax.ShapeDtypeStruct((SUBLANES, F), F32), jax.ShapeDtypeStruct((SUBLANES, F), F32)],
        compiler_params=_params(("parallel", "arbitrary")),
    )(up, up, up, up, cw, cb, cw, cb, da)


def _conv_bwd(dc, w, name):
    S, F = dc.shape
    tm, tn = _conv_tiles(S, F)
    r8 = tm // SUBLANES
    ni = S // tm

    def body(x_ref, h_ref, w_ref, o_ref):
        x = x_ref[...]
        halo = jnp.where(pl.program_id(1) == ni - 1, 0.0, h_ref[...])
        wv = w_ref[...]
        o_ref[...] = (x * wv[2:3, :] + _shift_up(x, halo, 1) * wv[1:2, :] + _shift_up(x, halo, 2) * wv[0:1, :]).astype(BF16)

    return pl.pallas_call(
        body, name=name, grid=(F // tn, ni),
        in_specs=[pl.BlockSpec((tm, tn), lambda j, i: (i, j)),
                  pl.BlockSpec((SUBLANES, tn), lambda j, i: (jnp.minimum((i + 1) * r8, S // SUBLANES - 1), j)),
                  pl.BlockSpec((3, tn), lambda j, i: (0, j))],
        out_specs=pl.BlockSpec((tm, tn), lambda j, i: (i, j)),
        out_shape=jax.ShapeDtypeStruct((S, F), BF16),
        compiler_params=_params(("parallel", "parallel")),
    )(dc, dc, w)


def _adamw(slots, w, m, v, name):
    P, R, C = slots.shape
    tr = _tile(R, (256, 128, 64, 32, 16))
    while tr * C * 4 > (1 << 20) and tr % 32 == 0:
        tr //= 2

    def body(s_ref, w_ref, m_ref, v_ref, g_ref, d_ref, nm_ref, nv_ref):
        g = s_ref[0].astype(F32)
        for p in range(1, P):
            g = g + s_ref[p].astype(F32)
        nm = ADAM_B1 * m_ref[...] + (1.0 - ADAM_B1) * g
        nv = ADAM_B2 * v_ref[...] + (1.0 - ADAM_B2) * (g * g)
        m_hat = nm / (1.0 - ADAM_B1 ** ADAM_STEP)
        v_hat = nv / (1.0 - ADAM_B2 ** ADAM_STEP)
        g_ref[...] = g
        d_ref[...] = -ADAM_LR * (m_hat / (jnp.sqrt(v_hat) + ADAM_EPS) + ADAM_WD * w_ref[...])
        nm_ref[...] = nm
        nv_ref[...] = nv

    blk = pl.BlockSpec((tr, C), lambda i: (i, 0))
    return pl.pallas_call(
        body, name=name, grid=(R // tr,), in_specs=[pl.BlockSpec((P, tr, C), lambda i: (0, i, 0)), blk, blk, blk],
        out_specs=[blk] * 4, out_shape=[jax.ShapeDtypeStruct((R, C), F32)] * 4,
        compiler_params=_params(("parallel",)),
    )(slots, w, m, v)


def _exchange(srcs, scatter, name):
    na = len(srcs)
    out_shape = [jax.ShapeDtypeStruct(s.shape if scatter else (N_DEV,) + s.shape, s.dtype) for s in srcs]

    def body(*refs):
        src_refs, out_refs = refs[:na], refs[na:2 * na]
        send_sems, recv_sems, local_sems = refs[2 * na:]
        x, y, c = lax.axis_index("x"), lax.axis_index("y"), lax.axis_index("c")
        me = 4 * x + 2 * y + c

        def flip(v, bit):
            return 1 - v if bit else v

        def peer(k):
            return (flip(x, (k >> 2) & 1), flip(y, (k >> 1) & 1), flip(c, k & 1))

        def peer_index(k):
            px, py, pc = peer(k)
            return 4 * px + 2 * py + pc

        def remote(a, k):
            src = src_refs[a].at[peer_index(k)] if scatter else src_refs[a]
            return pltpu.make_async_remote_copy(
                src_ref=src, dst_ref=out_refs[a].at[me], send_sem=send_sems.at[a, k - 1], recv_sem=recv_sems.at[a, k - 1],
                device_id=peer(k), device_id_type=pl.DeviceIdType.MESH)

        def arrival(a, k):
            src = src_refs[a].at[me] if scatter else src_refs[a]
            return pltpu.make_async_remote_copy(
                src_ref=src, dst_ref=out_refs[a].at[peer_index(k)], send_sem=send_sems.at[a, k - 1],
                recv_sem=recv_sems.at[a, k - 1], device_id=peer(k), device_id_type=pl.DeviceIdType.MESH)

        own = [pltpu.make_async_copy(src_refs[a].at[me] if scatter else src_refs[a], out_refs[a].at[me], local_sems.at[a])
               for a in range(na)]
        sends = [remote(a, k) for k in range(1, N_DEV) for a in range(na)]
        for cp in own + sends:
            cp.start()
        for k in range(1, N_DEV):
            for a in range(na):
                arrival(a, k).wait_recv()
        for cp in sends:
            cp.wait_send()
        for cp in own:
            cp.wait()

    any_spec = pl.BlockSpec(memory_space=pl.ANY)
    return pl.pallas_call(
        body, name=name, in_specs=[any_spec] * na, out_specs=[any_spec] * na, out_shape=out_shape,
        scratch_shapes=[pltpu.SemaphoreType.DMA((na, N_DEV - 1)), pltpu.SemaphoreType.DMA((na, N_DEV - 1)),
                        pltpu.SemaphoreType.DMA((na,))],
    )(*srcs)


def _layout(D):
    aq, akv = SWA_Q_HEADS * SWA_HEAD_DIM, SWA_KV_HEADS * SWA_HEAD_DIM
    w = SGU_GROUPS * SGU_DIM
    return aq, akv, w


class _Seg:
    def __init__(self, D, rq, rkv):
        aq, akv, w = _layout(D)
        src = {}
        o = 0
        for nm, wd in (("qa", aq), ("ka", akv), ("va", akv), ("cq", rq), ("ckv", rkv), ("kr", MLA_ROPE), ("hu", w), ("hv", w),
                       ("g", 3 * D)):
            src[nm] = (o, wd)
            o += wd
        self.n_in = o
        self.order = ("g", "qa", "hu", "hv", "cq", "ckv", "ka", "va", "kr")
        self.src = src
        self.off = {}
        o = 0
        for nm in self.order:
            self.off[nm] = o
            o += src[nm][1]
        self.width = {nm: src[nm][1] for nm in self.order}
        self.n_pad = -(-o // 1536) * 1536 if o > 1536 else -(-o // 512) * 512
        self.used = o

    def permute(self, w):
        parts = [w[:, self.src[nm][0]:self.src[nm][0] + self.src[nm][1]] for nm in self.order]
        parts.append(jnp.zeros((w.shape[0], self.n_pad - self.used), w.dtype))
        return jnp.concatenate(parts, axis=1)

    def unpermute(self, w):
        names = sorted(self.order, key=lambda nm: self.src[nm][0])
        return jnp.concatenate([w[:, self.off[nm]:self.off[nm] + self.width[nm]] for nm in names], axis=1)


def _uq_permute(w):
    R = w.shape[0]
    H, half = MLA_HEADS, MLA_ROPE // 2
    w3 = w.reshape(R, H, MLA_NOPE + MLA_ROPE)
    return jnp.concatenate([w3[:, :, :MLA_NOPE].reshape(R, H * MLA_NOPE),
                            w3[:, :, MLA_NOPE:MLA_NOPE + half].reshape(R, H * half),
                            w3[:, :, MLA_NOPE + half:].reshape(R, H * half)], axis=1)


def _uq_unpermute(w):
    R = w.shape[0]
    H, half = MLA_HEADS, MLA_ROPE // 2
    n = w[:, :H * MLA_NOPE].reshape(R, H, MLA_NOPE)
    r1 = w[:, H * MLA_NOPE:H * (MLA_NOPE + half)].reshape(R, H, half)
    r2 = w[:, H * (MLA_NOPE + half):].reshape(R, H, half)
    return jnp.concatenate([n, r1, r2], axis=2).reshape(R, H * (MLA_NOPE + MLA_ROPE))


def _ukv_permute(w):
    R = w.shape[0]
    w3 = w.reshape(R, MLA_HEADS, MLA_NOPE + MLA_V)
    return jnp.concatenate([w3[:, :, :MLA_NOPE].reshape(R, -1), w3[:, :, MLA_NOPE:].reshape(R, -1)], axis=1)


def _ukv_unpermute(w):
    R = w.shape[0]
    H = MLA_HEADS
    k = w[:, :H * MLA_NOPE].reshape(R, H, MLA_NOPE)
    v = w[:, H * MLA_NOPE:].reshape(R, H, MLA_V)
    return jnp.concatenate([k, v], axis=2).reshape(R, H * (MLA_NOPE + MLA_V))


def _heads(a, d):
    S = a.shape[0]
    return a.reshape(S, MLA_HEADS, d).transpose(1, 0, 2)


def _unheads(a):
    H, S, d = a.shape
    return a.transpose(1, 0, 2).reshape(S, H * d)


def _layer_fwd(l, x, x16, W, P, cs, sn, seg, alpha):
    S, D = x.shape
    H, half = MLA_HEADS, MLA_ROPE // 2
    off = seg.off
    nm = lambda s: f"l{l}_{s}"
    sv = {"x16": x16}
    h = _mm(x16, W["w_in"], "nn", [(F32, "n")], nm("h"))[0]
    sv["h"] = h
    ya, lse_a = _swa_fwd(h, off["qa"], off["ka"], off["va"], P["sinks"], nm("swa_fwd"))
    cqn, rq = _rms_fwd(h, off["cq"], seg.width["cq"], P["q_norm_g"], nm("rmsq_fwd"))
    ckvn, rkv = _rms_fwd(h, off["ckv"], seg.width["ckv"], P["kv_norm_g"], nm("rmskv_fwd"))
    qf = _mm(cqn, W["w_uq"], "nn", [(F32, "n")], nm("uq"))[0]
    kvf = _mm(ckvn, W["w_ukv"], "nn", [(BF16, "n")], nm("ukv"))[0]
    cs_h, sn_h = jnp.tile(cs, (1, H)), jnp.tile(sn, (1, H))
    qy1, qy2 = _rope((qf, H * MLA_NOPE), (qf, H * MLA_NOPE + H * half), cs_h, sn_h, H * half, False, nm("ropeq_fwd"))
    kr = h[:, off["kr"]:off["kr"] + MLA_ROPE]
    ky1, ky2 = _rope((kr[:, :half], 0), (kr[:, half:], 0), cs, sn, half, False, nm("ropek_fwd"))
    qh = jnp.concatenate([qf[:, :H * MLA_NOPE].astype(BF16).reshape(S, H, MLA_NOPE), qy1.reshape(S, H, half),
                          qy2.reshape(S, H, half)], axis=2).transpose(1, 0, 2)
    kh = jnp.concatenate([kvf[:, :H * MLA_NOPE].reshape(S, H, MLA_NOPE),
                          jnp.broadcast_to(ky1[:, None, :], (S, H, half)),
                          jnp.broadcast_to(ky2[:, None, :], (S, H, half))], axis=2).transpose(1, 0, 2)
    vh = _heads(kvf[:, H * MLA_NOPE:], MLA_V)
    ob, lse_b = _mla_fwd(qh, kh, vh, nm("mla_fwd"))
    yb = _unheads(ob).astype(BF16)
    w16 = jnp.where(jnp.tril(jnp.ones((SGU_CHUNK, SGU_CHUNK), bool))[None], P["sgu_w"], 0.0).astype(BF16)
    bt = P["sgu_b"].T
    yc = _sgu_fwd(h, off["hu"], off["hv"], P["sgu_ln_g"], P["sgu_ln_b"], w16, bt, nm("sgu_fwd"))
    merged, z = _merge_fwd([ya, yb, yc], [W["w_proj_a"], W["w_proj_b"], W["w_proj_c"]], h, P["b_gate"], nm("merge_fwd"))
    x1, x1_16, xh1, rs1 = _mm_ln(merged, W["w_o"], x, P["ln1_g"], P["ln1_b"], alpha, nm("wo_ln1"))
    up = _mm(x1_16, W["w_up"], "nn", [(F32, "n")], nm("up"))[0]
    a = _glu_fwd(up, P["conv_w"], P["conv_b"], nm("glu_fwd"))
    x2, x2_16, xh2, rs2 = _mm_ln(a, W["w_down"], x1, P["ln2_g"], P["ln2_b"], alpha, nm("down_ln2"))
    sv.update(ya=ya, lse_a=lse_a, cqn=cqn, rq=rq, ckvn=ckvn, rkv=rkv, qh=qh, kh=kh, vh=vh, ob=ob, lse_b=lse_b, yb=yb,
              w16=w16, bt=bt, yc=yc, merged=merged, z=z, x1_16=x1_16, xh1=xh1, rs1=rs1, up=up, a=a, xh2=xh2, rs2=rs2,
              cs_h=cs_h, sn_h=sn_h)
    return x2, x2_16, sv


def _layer_bwd(l, dx2, sv, W, P, cs, sn, seg, alpha):
    S, D = dx2.shape
    H, half = MLA_HEADS, MLA_ROPE // 2
    off = seg.off
    h = sv["h"]
    nm = lambda s: f"l{l}_{s}"
    g = {}
    dr2, dr2_16, g["ln2_g"], g["ln2_b"] = _ln_bwd(dx2, sv["xh2"], sv["rs2"], P["ln2_g"], nm("ln2_bwd"))
    g["w_down"] = _mm(sv["a"], dr2_16, "tn", [(F32, "n")], nm("dw_down"))[0]
    da = _mm(dr2_16, W["w_down"], "nt", [(F32, "n")], nm("da"))[0]
    dcg, dcv, st_g, st_v = _glu_bwd(sv["up"], P["conv_w"], P["conv_b"], da, nm("glu_bwd"))
    F = dcg.shape[1]
    g["conv_w"] = jnp.concatenate([st_g[0:3], st_v[0:3]], axis=1)
    g["conv_b"] = jnp.concatenate([st_g[3:4], st_v[3:4]], axis=1)
    dup = jnp.concatenate([_conv_bwd(dcg, P["conv_w"][:, :F], nm("convg_bwd")),
                           _conv_bwd(dcv, P["conv_w"][:, F:], nm("convv_bwd"))], axis=1)
    g["w_up"] = _mm(sv["x1_16"], dup, "tn", [(F32, "n")], nm("dw_up"))[0]
    dx1 = _mm_axpy(dup, W["w_up"], "nt", dr2, alpha, nm("dx1"))
    dr1, dr1_16, g["ln1_g"], g["ln1_b"] = _ln_bwd(dx1, sv["xh1"], sv["rs1"], P["ln1_g"], nm("ln1_bwd"))
    g["w_o"] = _mm(sv["merged"], dr1_16, "tn", [(F32, "n")], nm("dw_o"))[0]
    dmerged = _mm(dr1_16, W["w_o"], "nt", [(F32, "n")], nm("dmerged"))[0]
    dz, dlog, dbg = _merge_bwd(dmerged, sv["z"], h, P["b_gate"], nm("merge_bwd"))
    g["b_gate"] = dbg[0:3]
    g["w_proj_a"] = _mm(sv["ya"], dz[0], "tn", [(F32, "n")], nm("dw_pa"))[0]
    g["w_proj_b"] = _mm(sv["yb"], dz[1], "tn", [(F32, "n")], nm("dw_pb"))[0]
    g["w_proj_c"] = _mm(sv["yc"], dz[2], "tn", [(F32, "n")], nm("dw_pc"))[0]
    dya = _mm(dz[0], W["w_proj_a"], "nt", [(F32, "n")], nm("dya"))[0]
    dyb = _mm(dz[1], W["w_proj_b"], "nt", [(F32, "n")], nm("dyb"))[0]
    dyc = _mm(dz[2], W["w_proj_c"], "nt", [(F32, "n")], nm("dyc"))[0]
    dhu, dhv, g["sgu_w"], db_s, g["sgu_ln_g"], g["sgu_ln_b"] = _sgu_bwd(
        h, off["hu"], off["hv"], P["sgu_ln_g"], P["sgu_ln_b"], sv["w16"], sv["bt"], dyc, nm("sgu_bwd"))
    g["sgu_b"] = db_s[:, :SGU_GROUPS].T
    dqa, dka, dva, dsk = _swa_bwd(h, off["qa"], off["ka"], off["va"], P["sinks"], dya, sv["lse_a"], nm("swa_bwd"))
    g["sinks"] = dsk[0, :SWA_Q_HEADS]
    dob = _heads(dyb, MLA_V)
    delta = _rowdot(dob, sv["ob"], nm("mla_delta"))
    dqh, dkh, dvh = _mla_bwd(sv["qh"], sv["kh"], sv["vh"], dob.astype(BF16), sv["lse_b"], delta, nm("mla_bwd"))
    n0, n1 = MLA_NOPE, MLA_NOPE + half
    dqx1, dqx2 = _rope((_unheads(dqh[:, :, n0:n1]), 0), (_unheads(dqh[:, :, n1:]), 0), sv["cs_h"], sv["sn_h"], H * half, True,
                       nm("ropeq_bwd"))
    dqf = jnp.concatenate([_unheads(dqh[:, :, :n0]).astype(BF16), dqx1, dqx2], axis=1)
    dkvf = jnp.concatenate([_unheads(dkh[:, :, :n0]), _unheads(dvh)], axis=1).astype(BF16)
    dk1, dk2 = _rope(dkh[:, :, n0:n1], dkh[:, :, n1:], cs, sn, half, True, nm("ropek_bwd"))
    g["w_uq"] = _mm(sv["cqn"], dqf, "tn", [(F32, "n")], nm("dw_uq"))[0]
    g["w_ukv"] = _mm(sv["ckvn"], dkvf, "tn", [(F32, "n")], nm("dw_ukv"))[0]
    dcqn = _mm(dqf, W["w_uq"], "nt", [(F32, "n")], nm("dcqn"))[0]
    dckvn = _mm(dkvf, W["w_ukv"], "nt", [(F32, "n")], nm("dckvn"))[0]
    dcq, g["q_norm_g"] = _rms_bwd(dcqn, h, off["cq"], seg.width["cq"], sv["rq"], P["q_norm_g"], nm("rmsq_bwd"))
    dckv, g["kv_norm_g"] = _rms_bwd(dckvn, h, off["ckv"], seg.width["ckv"], sv["rkv"], P["kv_norm_g"], nm("rmskv_bwd"))
    parts = {"g": jnp.concatenate([dlog[0], dlog[1], dlog[2]], axis=1), "qa": dqa, "hu": dhu, "hv": dhv, "cq": dcq, "ckv": dckv,
             "ka": dka.astype(BF16), "va": dva.astype(BF16), "kr": jnp.concatenate([dk1, dk2], axis=1)}
    dh = jnp.concatenate([parts[k] for k in seg.order] + [jnp.zeros((S, seg.n_pad - seg.used), BF16)], axis=1)
    g["w_in"] = _mm(sv["x16"], dh, "tn", [(F32, "n")], nm("dw_in"))[0]
    dx = _mm_axpy(dh, W["w_in"], "nt", dr1, alpha, nm("dx"))
    return dx, g


BIG = ("w_in", "w_uq", "w_ukv", "w_proj_a", "w_proj_b", "w_proj_c", "w_o", "w_up", "w_down")
ROW_SHARDED = ("w_proj_b", "w_o", "w_down")
SHARDED_F32 = ("b_gate", "conv_w")
REPLICATED = ("sinks", "q_norm_g", "kv_norm_g", "sgu_ln_g", "sgu_ln_b", "sgu_w", "sgu_b", "ln1_g", "ln1_b", "conv_b", "ln2_g",
              "ln2_b")
WEIGHTS = ("w_in", "b_gate", "sinks", "q_norm_g", "kv_norm_g", "w_uq", "w_ukv", "sgu_ln_g", "sgu_ln_b", "sgu_w", "sgu_b",
           "w_proj_a", "w_proj_b", "w_proj_c", "w_o", "ln1_g", "ln1_b", "w_up", "conv_w", "conv_b", "w_down", "ln2_g", "ln2_b")


def _prepare(l, full, small, seg):
    W = {k: full[k][l] for k in BIG}
    W["w_in"] = seg.permute(W["w_in"])
    W["w_uq"] = _uq_permute(W["w_uq"])
    W["w_ukv"] = _ukv_permute(W["w_ukv"])
    P = {k: small[k][l] for k in small}
    for k in ("q_norm_g", "kv_norm_g", "sgu_ln_g", "sgu_ln_b", "ln1_g", "ln1_b", "conv_b", "ln2_g", "ln2_b"):
        P[k] = P[k].reshape(1, -1)
    return W, P


def _step_local(x, positions, target, full, small):
    S, D = x.shape
    L = full["w_in"].shape[0]
    alpha = (2 * L) ** 0.25
    seg = _Seg(D, full["w_uq"].shape[1], full["w_ukv"].shape[1])
    assert seg.n_in == full["w_in"].shape[2]
    inv_freq = ROPE_THETA ** (-jnp.arange(0, MLA_ROPE, 2, dtype=F32) / MLA_ROPE)
    ang = positions.astype(F32)[:, None] * inv_freq
    cs, sn = jnp.cos(ang), jnp.sin(ang)
    layers = [_prepare(l, full, small, seg) for l in range(L)]
    saved = []
    x16 = x.astype(BF16)
    for l in range(L):
        W, P = layers[l]
        x, x16, sv = _layer_fwd(l, x, x16, W, P, cs, sn, seg, alpha)
        saved.append(sv)
    loss, dx = _loss(x, target, "loss")
    grads = [None] * L
    for l in reversed(range(L)):
        W, P = layers[l]
        dx, g = _layer_bwd(l, dx, saved[l], W, P, cs, sn, seg, alpha)
        g["w_in"] = seg.unpermute(g["w_in"])
        g["w_uq"] = _uq_unpermute(g["w_uq"])
        g["w_ukv"] = _ukv_unpermute(g["w_ukv"])
        grads[l] = g
    out = {}
    for k in WEIGHTS:
        out[k] = jnp.stack([grads[l][k].reshape(small[k].shape[1:] if k in small else full[k].shape[1:]) for l in range(L)])
    return loss, dx, out


def _unshard(k, gathered):
    if k in ROW_SHARDED:
        n, L, r, c = gathered.shape
        return gathered.transpose(1, 0, 2, 3).reshape(L, n * r, c)
    n, L, r, c = gathered.shape
    return gathered.transpose(1, 2, 0, 3).reshape(L, r, n * c)


def _to_chunks(k, gfull):
    L, r, c = gfull.shape
    if k in ROW_SHARDED:
        return gfull.reshape(L, N_DEV, r // N_DEV, c).transpose(1, 0, 2, 3)
    return gfull.reshape(L, r, N_DEV, c // N_DEV).transpose(2, 0, 1, 3)


def _pack(arrs):
    P = arrs[0].shape[0]
    flat, sizes = [], []
    for a in arrs:
        f = a.reshape(P, -1)
        n = f.shape[1]
        pad = -n % (SUBLANES * 128)
        flat.append(jnp.pad(f, ((0, 0), (0, pad))))
        sizes.append((n, n + pad))
    return jnp.concatenate(flat, axis=1).reshape(P, -1, 128), sizes


def _unpack(packed, sizes, shapes):
    flat = packed.reshape(-1)
    out, o = [], 0
    for (n, npad), shp in zip(sizes, shapes):
        out.append(flat[o:o + n].reshape(shp))
        o += npad
    return out


def kernel(x, positions, w_in, b_gate, sinks, q_norm_g, kv_norm_g, w_uq, w_ukv, sgu_ln_g, sgu_ln_b, sgu_w, sgu_b, w_proj_a, w_proj_b, w_proj_c, w_o, ln1_g, ln1_b, w_up, conv_w, conv_b, w_down, ln2_g, ln2_b, loss_target, m_w_in, m_b_gate, m_sinks, m_q_norm_g, m_kv_norm_g, m_w_uq, m_w_ukv, m_sgu_ln_g, m_sgu_ln_b, m_sgu_w, m_sgu_b, m_w_proj_a, m_w_proj_b, m_w_proj_c, m_w_o, m_ln1_g, m_ln1_b, m_w_up, m_conv_w, m_conv_b, m_w_down, m_ln2_g, m_ln2_b, v_w_in, v_b_gate, v_sinks, v_q_norm_g, v_kv_norm_g, v_w_uq, v_w_ukv, v_sgu_ln_g, v_sgu_ln_b, v_sgu_w, v_sgu_b, v_w_proj_a, v_w_proj_b, v_w_proj_c, v_w_o, v_ln1_g, v_ln1_b, v_w_up, v_conv_w, v_conv_b, v_w_down, v_ln2_g, v_ln2_b):
    given = dict(locals())
    w = {k: given[k] for k in WEIGHTS}
    mom = {k: given["m_" + k] for k in WEIGHTS}
    var = {k: given["v_" + k] for k in WEIGHTS}

    gathered = _exchange([w[k].astype(BF16) for k in BIG] + [w[k] for k in SHARDED_F32], False, "gather_weights")
    full = {k: _unshard(k, gathered[i]) for i, k in enumerate(BIG)}
    small = {k: w[k] for k in REPLICATED}
    for i, k in enumerate(SHARDED_F32):
        small[k] = _unshard(k, gathered[len(BIG) + i])

    loss, grad_x, g = _step_local(x[0], positions[0], loss_target[0], full, small)
    loss = lax.psum(loss[0, 0], AXES)

    sharded = BIG + SHARDED_F32
    chunks = [_to_chunks(k, g[k]).astype(BF16 if k in BIG else F32) for k in sharded]
    slots = _exchange(chunks, True, "scatter_grads")
    res = {}
    for i, k in enumerate(sharded):
        shp = w[k].shape
        r2 = (shp[0] * shp[1], shp[2])
        outs = _adamw(slots[i].reshape(N_DEV, *r2), w[k].reshape(r2), mom[k].reshape(r2), var[k].reshape(r2), "adamw_" + k)
        res[k] = [o.reshape(shp) for o in outs]

    packed, sizes = _pack([g[k][None] for k in REPLICATED])
    parts = _exchange([packed[0]], False, "gather_small_grads")[0]
    shapes = [w[k].shape for k in REPLICATED]
    pw, _ = _pack([w[k][None] for k in REPLICATED])
    pm, _ = _pack([mom[k][None] for k in REPLICATED])
    pv, _ = _pack([var[k][None] for k in REPLICATED])
    outs = _adamw(parts, pw[0], pm[0], pv[0], "adamw_small")
    unpacked = [_unpack(o, sizes, shapes) for o in outs]
    for i, k in enumerate(REPLICATED):
        res[k] = [unpacked[j][i] for j in range(4)]

    return (loss, grad_x[None], *[res[k][0] for k in WEIGHTS], *[res[k][1] for k in WEIGHTS],
            *[res[k][2] for k in WEIGHTS], *[res[k][3] for k in WEIGHTS])
```

```python
import functools
import math

import jax
import jax.numpy as jnp
from jax import lax
from jax.experimental import pallas as pl
from jax.experimental.pallas import tpu as pltpu

F32 = jnp.float32
BF16 = jnp.bfloat16

SWA_Q_HEADS = 16
SWA_KV_HEADS = 2
SWA_HEAD_DIM = 64
SWA_BLOCK = 128
MLA_HEADS = 16
MLA_NOPE = 128
MLA_ROPE = 64
MLA_V = 128
SGU_GROUPS = 8
SGU_DIM = 128
SGU_CHUNK = 128
ROPE_THETA = 10000.0
EPS = 1e-5
MASK = -1e30
ADAM_LR = 0.001
ADAM_B1 = 0.9
ADAM_B2 = 0.999
ADAM_EPS = 1e-08
ADAM_WD = 0.01
ADAM_STEP = 10

N_DEV = 8
AXES = ("x", "y", "c")
VMEM_LIMIT = 56 * 1024 * 1024
MLA_TILE = 512
ROW_TILE = 512
SUBLANES = 8


def _tile(n, prefs):
    for p in prefs:
        if n % p == 0:
            return p
    return n


def _params(sem):
    return pltpu.CompilerParams(dimension_semantics=sem, vmem_limit_bytes=VMEM_LIMIT)


def _cols(tm, width, off):
    assert off % width == 0, (off, width)
    blk = off // width
    return pl.BlockSpec((tm, width), lambda i, *_: (i, blk))


def _full(shape):
    nd = len(shape)
    return pl.BlockSpec(shape, lambda *_: (0,) * nd)


def _sigmoid(v):
    return 1.0 / (1.0 + jnp.exp(-v))


def _gelu(v):
    return 0.5 * v * (1.0 + lax.erf(v * (2.0 ** -0.5)))


def _gelu_grad(v):
    return 0.5 * (1.0 + lax.erf(v * (2.0 ** -0.5))) + v * jnp.exp(-0.5 * v * v) * (1.0 / math.sqrt(2.0 * math.pi))


_DIMS = {"nn": (((1,), (0,)), ((), ())), "nt": (((1,), (1,)), ((), ())), "tn": (((0,), (0,)), ((), ()))}


def _mm(a, b, mode, outs, name, *, extras=(), epilogue=None, full_n=False, dep=None):
    if mode == "nn":
        (M, K), (K2, N) = a.shape, b.shape
    elif mode == "nt":
        (M, K), (N, K2) = a.shape, b.shape
    else:
        (K, M), (K2, N) = a.shape, b.shape
    assert K == K2, (a.shape, b.shape, mode)
    tm = _tile(M, (1024, 512, 256, 128))
    tn = N if full_n else _tile(N, (1024, 768, 512, 384, 256, 128))
    if full_n:
        tm = _tile(M, (512, 256, 128))
    tk = _tile(K, (1024, 512, 384, 256, 128))
    nk = K // tk
    if mode == "nn":
        a_spec = pl.BlockSpec((tm, tk), lambda i, j, k: (i, k))
        b_spec = pl.BlockSpec((tk, tn), lambda i, j, k: (k, j))
    elif mode == "nt":
        a_spec = pl.BlockSpec((tm, tk), lambda i, j, k: (i, k))
        b_spec = pl.BlockSpec((tn, tk), lambda i, j, k: (j, k))
    else:
        a_spec = pl.BlockSpec((tk, tm), lambda i, j, k: (k, i))
        b_spec = pl.BlockSpec((tk, tn), lambda i, j, k: (k, j))
    in_specs = [a_spec, b_spec]
    for arr, kind in extras:
        if kind == "tile":
            in_specs.append(pl.BlockSpec((tm, tn), lambda i, j, k: (i, j)))
        else:
            in_specs.append(pl.BlockSpec((1, tn), lambda i, j, k: (0, j)))
    out_specs, out_shape = [], []
    for dt, kind in outs:
        if kind == "n":
            out_specs.append(pl.BlockSpec((tm, tn), lambda i, j, k: (i, j)))
            out_shape.append(jax.ShapeDtypeStruct((M, N), dt))
        else:
            assert tn == N
            out_specs.append(pl.BlockSpec((tm, 1), lambda i, j, k: (i, 0)))
            out_shape.append(jax.ShapeDtypeStruct((M, 1), dt))
    ne, no = len(extras), len(outs)
    deps = []
    if dep is not None:
        in_specs.append(_full(dep.shape))
        deps = [dep]
    dims = _DIMS[mode]
    if epilogue is None:
        epilogue = lambda acc: (acc,) * no

    def body(*refs):
        a_ref, b_ref = refs[0], refs[1]
        ex = refs[2:2 + ne]
        out = refs[len(refs) - 1 - no:len(refs) - 1]
        acc = refs[-1]
        k = pl.program_id(2)

        @pl.when(k == 0)
        def _():
            acc[...] = jnp.zeros_like(acc)

        acc[...] += lax.dot_general(a_ref[...].astype(BF16), b_ref[...].astype(BF16), dims,
                                    preferred_element_type=F32)

        @pl.when(k == nk - 1)
        def _():
            res = epilogue(acc[...], *[e[...] for e in ex])
            for o, r in zip(out, res):
                o[...] = r.astype(o.dtype)

    res = pl.pallas_call(
        body, name=name, grid=(M // tm, N // tn, nk), in_specs=in_specs, out_specs=out_specs, out_shape=out_shape,
        scratch_shapes=[pltpu.VMEM((tm, tn), F32)],
        compiler_params=_params(("parallel", "parallel", "arbitrary")),
    )(a, b, *[e[0] for e in extras], *deps)
    return res


def _ln_epilogue(alpha):
    def epi(acc, x, g, b):
        r = alpha * x + acc
        mu = jnp.mean(r, axis=-1, keepdims=True)
        d = r - mu
        var = jnp.mean(d * d, axis=-1, keepdims=True)
        rstd = lax.rsqrt(var + EPS)
        xhat = d * rstd
        y = xhat * g + b
        return y, y, xhat, rstd
    return epi


def _mm_ln(a, w, x, g, b, alpha, name):
    return _mm(a, w, "nn", [(F32, "n"), (BF16, "n"), (F32, "n"), (F32, "1")], name,
               extras=[(x, "tile"), (g, "row"), (b, "row")], epilogue=_ln_epilogue(alpha), full_n=True)


def _mm_axpy(a, w, mode, r, alpha, name, dep=None):
    return _mm(a, w, mode, [(F32, "n")], name, extras=[(r, "tile")],
               epilogue=lambda acc, rv: (acc + alpha * rv,), dep=dep)[0]


def _ln_bwd(dy, xhat, rstd, g, name):
    S, D = dy.shape
    tm = _tile(S, (256, 128))

    def body(dy_ref, xh_ref, rs_ref, g_ref, dr_ref, dr16_ref, dg_ref, db_ref):
        @pl.when(pl.program_id(0) == 0)
        def _():
            dg_ref[...] = jnp.zeros_like(dg_ref)
            db_ref[...] = jnp.zeros_like(db_ref)

        dyv, xh = dy_ref[...], xh_ref[...]
        dxh = dyv * g_ref[...]
        m1 = jnp.mean(dxh, axis=-1, keepdims=True)
        m2 = jnp.mean(dxh * xh, axis=-1, keepdims=True)
        dr = rs_ref[...] * (dxh - m1 - xh * m2)
        dr_ref[...] = dr
        dr16_ref[...] = dr.astype(BF16)
        dg_ref[...] += jnp.sum(dyv * xh, axis=0, keepdims=True)
        db_ref[...] += jnp.sum(dyv, axis=0, keepdims=True)

    row = pl.BlockSpec((tm, D), lambda i: (i, 0))
    return pl.pallas_call(
        body, name=name, grid=(S // tm,),
        in_specs=[row, row, pl.BlockSpec((tm, 1), lambda i: (i, 0)), _full((1, D))],
        out_specs=[row, row, _full((1, D)), _full((1, D))],
        out_shape=[jax.ShapeDtypeStruct((S, D), F32), jax.ShapeDtypeStruct((S, D), BF16),
                   jax.ShapeDtypeStruct((1, D), F32), jax.ShapeDtypeStruct((1, D), F32)],
        compiler_params=_params(("arbitrary",)),
    )(dy, xhat, rstd, g)


def _rms_fwd(h, off, width, g, name):
    S = h.shape[0]
    tm = _tile(S, (ROW_TILE, 256, 128))

    def body(c_ref, g_ref, y_ref, r_ref):
        c = c_ref[...]
        r = lax.rsqrt(jnp.mean(c * c, axis=-1, keepdims=True) + EPS)
        y_ref[...] = (c * r * g_ref[...]).astype(BF16)
        r_ref[...] = r

    return pl.pallas_call(
        body, name=name, grid=(S // tm,),
        in_specs=[_cols(tm, width, off), _full((1, width))],
        out_specs=[pl.BlockSpec((tm, width), lambda i: (i, 0)), pl.BlockSpec((tm, 1), lambda i: (i, 0))],
        out_shape=[jax.ShapeDtypeStruct((S, width), BF16), jax.ShapeDtypeStruct((S, 1), F32)],
        compiler_params=_params(("parallel",)),
    )(h, g)


def _rms_bwd(dy, h, off, width, rstd, g, name):
    S = h.shape[0]
    tm = _tile(S, (ROW_TILE, 256, 128))

    def body(dy_ref, c_ref, r_ref, g_ref, dc_ref, dg_ref):
        @pl.when(pl.program_id(0) == 0)
        def _():
            dg_ref[...] = jnp.zeros_like(dg_ref)

        dyv, c, r = dy_ref[...], c_ref[...], r_ref[...]
        dyg = dyv * g_ref[...]
        m = jnp.mean(dyg * c, axis=-1, keepdims=True)
        dc_ref[...] = (r * dyg - c * (r * r * r) * m).astype(BF16)
        dg_ref[...] += jnp.sum(dyv * c * r, axis=0, keepdims=True)

    return pl.pallas_call(
        body, name=name, grid=(S // tm,),
        in_specs=[pl.BlockSpec((tm, width), lambda i: (i, 0)), _cols(tm, width, off),
                  pl.BlockSpec((tm, 1), lambda i: (i, 0)), _full((1, width))],
        out_specs=[pl.BlockSpec((tm, width), lambda i: (i, 0)), _full((1, width))],
        out_shape=[jax.ShapeDtypeStruct((S, width), BF16), jax.ShapeDtypeStruct((1, width), F32)],
        compiler_params=_params(("arbitrary",)),
    )(dy, h, rstd, g)


def _loss(y, target, name):
    S, D = y.shape
    tm = _tile(S, (256, 128))

    def body(y_ref, t_ref, l_ref, dy_ref):
        @pl.when(pl.program_id(0) == 0)
        def _():
            l_ref[...] = jnp.zeros_like(l_ref)

        err = y_ref[...] - t_ref[...]
        dy_ref[...] = err * (1.0 / D)
        per_tok = jnp.mean(err * err, axis=-1, keepdims=True)
        l_ref[...] += 0.5 * jnp.sum(per_tok, axis=0, keepdims=True)

    row = pl.BlockSpec((tm, D), lambda i: (i, 0))
    return pl.pallas_call(
        body, name=name, grid=(S // tm,), in_specs=[row, row], out_specs=[_full((1, 1)), row],
        out_shape=[jax.ShapeDtypeStruct((1, 1), F32), jax.ShapeDtypeStruct((S, D), F32)],
        compiler_params=_params(("arbitrary",)),
    )(y, target)


def _rope(a1, a2, cs, sn, n, bwd, name):
    S = cs.shape[0]
    tm = _tile(S, (ROW_TILE, 256, 128))
    stacked = not isinstance(a1, tuple)

    def body(a1_ref, a2_ref, c_ref, s_ref, y1_ref, y2_ref):
        if stacked:
            v1 = jnp.sum(a1_ref[...], axis=0)
            v2 = jnp.sum(a2_ref[...], axis=0)
        else:
            v1, v2 = a1_ref[...].astype(F32), a2_ref[...].astype(F32)
        c, s = c_ref[...], s_ref[...]
        if bwd:
            y1_ref[...] = (v1 * c + v2 * s).astype(BF16)
            y2_ref[...] = (v2 * c - v1 * s).astype(BF16)
        else:
            y1_ref[...] = (v1 * c - v2 * s).astype(BF16)
            y2_ref[...] = (v2 * c + v1 * s).astype(BF16)

    row = pl.BlockSpec((tm, n), lambda i: (i, 0))
    if stacked:
        H = a1.shape[0]
        a_specs = [pl.BlockSpec((H, tm, n), lambda i: (0, i, 0))] * 2
        arrs = [a1, a2]
    else:
        a_specs = [_cols(tm, n, a1[1]), _cols(tm, n, a2[1])]
        arrs = [a1[0], a2[0]]
    return pl.pallas_call(
        body, name=name, grid=(S // tm,), in_specs=a_specs + [row, row], out_specs=[row, row],
        out_shape=[jax.ShapeDtypeStruct((S, n), BF16)] * 2,
        compiler_params=_params(("parallel",)),
    )(*arrs, cs, sn)


def _swa_mask(n):
    blk = SWA_BLOCK
    row = lax.broadcasted_iota(jnp.int32, (blk, 2 * blk), 0)
    col = lax.broadcasted_iota(jnp.int32, (blk, 2 * blk), 1)
    rel = row + blk - col
    return (rel >= 0) & (rel < blk) & ((n > 0) | (col >= blk))


def _swa_specs(off_q, off_k, off_v):
    blk, aq, akv = SWA_BLOCK, SWA_Q_HEADS * SWA_HEAD_DIM, SWA_KV_HEADS * SWA_HEAD_DIM
    assert off_q % aq == 0 and off_k % akv == 0 and off_v % akv == 0
    prev = lambda off: pl.BlockSpec((blk, akv), lambda n: (jnp.maximum(n - 1, 0), off // akv))
    cur = lambda off: pl.BlockSpec((blk, akv), lambda n: (n, off // akv))
    return [pl.BlockSpec(memory_space=pltpu.SMEM), _cols(blk, aq, off_q), prev(off_k), cur(off_k), prev(off_v), cur(off_v)]


def _swa_fwd(h, off_q, off_k, off_v, sinks, name):
    S = h.shape[0]
    blk, hd, nh = SWA_BLOCK, SWA_HEAD_DIM, SWA_Q_HEADS
    grp = nh // SWA_KV_HEADS
    aq = nh * hd
    scale = hd ** -0.5

    def body(sink_ref, q_ref, kp_ref, kc_ref, vp_ref, vc_ref, o_ref, lse_ref):
        valid = _swa_mask(pl.program_id(0))
        q = q_ref[...].astype(BF16)
        k2 = jnp.concatenate([kp_ref[...], kc_ref[...]], axis=0).astype(BF16)
        v2 = jnp.concatenate([vp_ref[...], vc_ref[...]], axis=0).astype(BF16)
        for hh in range(nh):
            kv = hh // grp
            qh = q[:, hh * hd:(hh + 1) * hd]
            kh = k2[:, kv * hd:(kv + 1) * hd]
            vh = v2[:, kv * hd:(kv + 1) * hd]
            s = lax.dot_general(qh, kh, _DIMS["nt"], preferred_element_type=F32) * scale
            s = jnp.where(valid, s, MASK)
            sk = sink_ref[hh]
            m = jnp.maximum(jnp.max(s, axis=1, keepdims=True), sk)
            p = jnp.exp(s - m)
            l = jnp.sum(p, axis=1, keepdims=True) + jnp.exp(sk - m)
            o_ref[:, hh * hd:(hh + 1) * hd] = jnp.dot((p / l).astype(BF16), vh, preferred_element_type=F32)
            lse_ref[:, hh:hh + 1] = m + jnp.log(l)

    return pl.pallas_call(
        body, name=name, grid=(S // blk,), in_specs=_swa_specs(off_q, off_k, off_v),
        out_specs=[pl.BlockSpec((blk, aq), lambda n: (n, 0)), pl.BlockSpec((blk, nh), lambda n: (n, 0))],
        out_shape=[jax.ShapeDtypeStruct((S, aq), F32), jax.ShapeDtypeStruct((S, nh), F32)],
        compiler_params=_params(("parallel",)),
    )(sinks, h, h, h, h, h)


def _swa_bwd(h, off_q, off_k, off_v, sinks, dout, lse, name):
    S = h.shape[0]
    blk, hd, nh, nkv = SWA_BLOCK, SWA_HEAD_DIM, SWA_Q_HEADS, SWA_KV_HEADS
    grp = nh // nkv
    aq, akv = nh * hd, nkv * hd
    scale = hd ** -0.5

    def body(sink_ref, q_ref, kp_ref, kc_ref, vp_ref, vc_ref, do_ref, lse_ref, dq_ref, dk_ref, dv_ref, ds_ref):
        n = pl.program_id(0)

        @pl.when(n == 0)
        def _():
            dk_ref[...] = jnp.zeros_like(dk_ref)
            dv_ref[...] = jnp.zeros_like(dv_ref)
            ds_ref[...] = jnp.zeros_like(ds_ref)

        valid = _swa_mask(n)
        q = q_ref[...].astype(BF16)
        k2 = jnp.concatenate([kp_ref[...], kc_ref[...]], axis=0).astype(BF16)
        v2 = jnp.concatenate([vp_ref[...], vc_ref[...]], axis=0).astype(BF16)
        do = do_ref[...]
        lane = lax.broadcasted_iota(jnp.int32, (1, 128), 1)
        dsink = jnp.zeros((1, 128), F32)
        cur = pl.ds(pl.multiple_of(n * blk, blk), blk)
        prev = pl.ds(pl.multiple_of(jnp.maximum(n - 1, 0) * blk, blk), blk)
        for kv in range(nkv):
            kh = k2[:, kv * hd:(kv + 1) * hd]
            vh = v2[:, kv * hd:(kv + 1) * hd]
            dk_acc = jnp.zeros((2 * blk, hd), F32)
            dv_acc = jnp.zeros((2 * blk, hd), F32)
            for g in range(grp):
                hh = kv * grp + g
                qh = q[:, hh * hd:(hh + 1) * hd]
                doh = do[:, hh * hd:(hh + 1) * hd]
                doh16 = doh.astype(BF16)
                lse_h = lse_ref[:, hh:hh + 1]
                s = lax.dot_general(qh, kh, _DIMS["nt"], preferred_element_type=F32) * scale
                s = jnp.where(valid, s, MASK)
                p = jnp.exp(s - lse_h)
                p16 = p.astype(BF16)
                o = jnp.dot(p16, vh, preferred_element_type=F32)
                delta = jnp.sum(doh * o, axis=1, keepdims=True)
                dp = lax.dot_general(doh16, vh, _DIMS["nt"], preferred_element_type=F32)
                ds16 = (p * (dp - delta) * scale).astype(BF16)
                dq_ref[:, hh * hd:(hh + 1) * hd] = jnp.dot(ds16, kh, preferred_element_type=F32).astype(BF16)
                dk_acc += lax.dot_general(ds16, qh, _DIMS["tn"], preferred_element_type=F32)
                dv_acc += lax.dot_general(p16, doh16, _DIMS["tn"], preferred_element_type=F32)
                dsk = -jnp.sum(jnp.exp(sink_ref[hh] - lse_h) * delta, axis=0, keepdims=True)
                dsink += jnp.where(lane == hh, dsk, 0.0)
            cols = slice(kv * hd, (kv + 1) * hd)
            dk_ref[cur, cols] += dk_acc[blk:]
            dv_ref[cur, cols] += dv_acc[blk:]

            @pl.when(n > 0)
            def _():
                dk_ref[prev, cols] += dk_acc[:blk]
                dv_ref[prev, cols] += dv_acc[:blk]

        ds_ref[...] += dsink

    return pl.pallas_call(
        body, name=name, grid=(S // blk,),
        in_specs=_swa_specs(off_q, off_k, off_v) + [pl.BlockSpec((blk, aq), lambda n: (n, 0)),
                                                    pl.BlockSpec((blk, nh), lambda n: (n, 0))],
        out_specs=[pl.BlockSpec((blk, aq), lambda n: (n, 0)), _full((S, akv)), _full((S, akv)), _full((1, 128))],
        out_shape=[jax.ShapeDtypeStruct((S, aq), BF16), jax.ShapeDtypeStruct((S, akv), F32),
                   jax.ShapeDtypeStruct((S, akv), F32), jax.ShapeDtypeStruct((1, 128), F32)],
        compiler_params=_params(("arbitrary",)),
    )(sinks, h, h, h, h, h, dout, lse)


def _causal(i, j, t):
    row = i * t + lax.broadcasted_iota(jnp.int32, (t, t), 0)
    col = j * t + lax.broadcasted_iota(jnp.int32, (t, t), 1)
    return col <= row


def _mla_fwd(q, k, v, name):
    H, S, dq = q.shape
    dv = v.shape[2]
    t = _tile(S, (MLA_TILE, 256, 128))
    nq = S // t
    scale = (MLA_NOPE + MLA_ROPE) ** -0.5

    def body(q_ref, k_ref, v_ref, o_ref, lse_ref, m_s, l_s, acc_s):
        i, j = pl.program_id(1), pl.program_id(2)

        @pl.when(j == 0)
        def _():
            m_s[...] = jnp.full_like(m_s, -jnp.inf)
            l_s[...] = jnp.zeros_like(l_s)
            acc_s[...] = jnp.zeros_like(acc_s)

        @pl.when(j <= i)
        def _():
            s = lax.dot_general(q_ref[0], k_ref[0], _DIMS["nt"], preferred_element_type=F32) * scale
            s = jnp.where(_causal(i, j, t), s, MASK)
            m_new = jnp.maximum(m_s[...], jnp.max(s, axis=1, keepdims=True))
            alpha = jnp.exp(m_s[...] - m_new)
            p = jnp.exp(s - m_new)
            l_s[...] = alpha * l_s[...] + jnp.sum(p, axis=1, keepdims=True)
            acc_s[...] = alpha * acc_s[...] + jnp.dot(p.astype(BF16), v_ref[0], preferred_element_type=F32)
            m_s[...] = m_new

        @pl.when(j == i)
        def _():
            o_ref[0] = acc_s[...] / l_s[...]
            lse_ref[0] = m_s[...] + jnp.log(l_s[...])

    kv_map = lambda h, i, j: (h, jnp.minimum(j, i), 0)
    return pl.pallas_call(
        body, name=name, grid=(H, nq, nq),
        in_specs=[pl.BlockSpec((1, t, dq), lambda h, i, j: (h, i, 0)), pl.BlockSpec((1, t, dq), kv_map),
                  pl.BlockSpec((1, t, dv), kv_map)],
        out_specs=[pl.BlockSpec((1, t, dv), lambda h, i, j: (h, i, 0)), pl.BlockSpec((1, t, 1), lambda h, i, j: (h, i, 0))],
        out_shape=[jax.ShapeDtypeStruct((H, S, dv), F32), jax.ShapeDtypeStruct((H, S, 1), F32)],
        scratch_shapes=[pltpu.VMEM((t, 1), F32), pltpu.VMEM((t, 1), F32), pltpu.VMEM((t, dv), F32)],
        compiler_params=_params(("parallel", "parallel", "arbitrary")),
    )(q, k, v)


def _rowdot(a, b, name):
    H, S, d = a.shape
    t = _tile(S, (ROW_TILE, 256, 128))

    def body(a_ref, b_ref, o_ref):
        o_ref[...] = jnp.sum(a_ref[...] * b_ref[...], axis=-1, keepdims=True)

    spec = pl.BlockSpec((1, t, d), lambda h, i: (h, i, 0))
    return pl.pallas_call(
        body, name=name, grid=(H, S // t), in_specs=[spec, spec],
        out_specs=pl.BlockSpec((1, t, 1), lambda h, i: (h, i, 0)),
        out_shape=jax.ShapeDtypeStruct((H, S, 1), F32),
        compiler_params=_params(("parallel", "parallel")),
    )(a, b)


def _mla_bwd(q, k, v, do, lse, delta, name):
    H, S, dq = q.shape
    dv = v.shape[2]
    t = _tile(S, (MLA_TILE, 256, 128))
    nq = S // t
    scale = (MLA_NOPE + MLA_ROPE) ** -0.5

    def body(q_ref, k_ref, v_ref, do_ref, lse_ref, dl_ref, dq_ref, dk_ref, dv_ref, dk_s, dv_s):
        j, i = pl.program_id(1), pl.program_id(2)

        @pl.when((j == 0) & (i == 0))
        def _():
            dq_ref[...] = jnp.zeros_like(dq_ref)

        @pl.when(i == j)
        def _():
            dk_s[...] = jnp.zeros_like(dk_s)
            dv_s[...] = jnp.zeros_like(dv_s)

        @pl.when(i >= j)
        def _():
            qv, kv_, vv, dov = q_ref[0], k_ref[0], v_ref[0], do_ref[0]
            s = lax.dot_general(qv, kv_, _DIMS["nt"], preferred_element_type=F32) * scale
            s = jnp.where(_causal(i, j, t), s, MASK)
            p = jnp.exp(s - lse_ref[0])
            p16 = p.astype(BF16)
            dp = lax.dot_general(dov, vv, _DIMS["nt"], preferred_element_type=F32)
            ds16 = (p * (dp - dl_ref[0]) * scale).astype(BF16)
            dv_s[...] += lax.dot_general(p16, dov, _DIMS["tn"], preferred_element_type=F32)
            dk_s[...] += lax.dot_general(ds16, qv, _DIMS["tn"], preferred_element_type=F32)
            rows = pl.ds(pl.multiple_of(i * t, t), t)
            dq_ref[0, rows, :] += jnp.dot(ds16, kv_, preferred_element_type=F32)

        @pl.when(i == nq - 1)
        def _():
            dk_ref[0] = dk_s[...]
            dv_ref[0] = dv_s[...]

    q_map = lambda h, j, i: (h, jnp.maximum(i, j), 0)
    kv_map = lambda h, j, i: (h, j, 0)
    return pl.pallas_call(
        body, name=name, grid=(H, nq, nq),
        in_specs=[pl.BlockSpec((1, t, dq), q_map), pl.BlockSpec((1, t, dq), kv_map), pl.BlockSpec((1, t, dv), kv_map),
                  pl.BlockSpec((1, t, dv), q_map), pl.BlockSpec((1, t, 1), q_map), pl.BlockSpec((1, t, 1), q_map)],
        out_specs=[pl.BlockSpec((1, S, dq), lambda h, j, i: (h, 0, 0)), pl.BlockSpec((1, t, dq), kv_map),
                   pl.BlockSpec((1, t, dv), kv_map)],
        out_shape=[jax.ShapeDtypeStruct((H, S, dq), F32), jax.ShapeDtypeStruct((H, S, dq), F32),
                   jax.ShapeDtypeStruct((H, S, dv), F32)],
        scratch_shapes=[pltpu.VMEM((t, dq), F32), pltpu.VMEM((t, dv), F32)],
        compiler_params=_params(("parallel", "arbitrary", "arbitrary")),
    )(q, k, v, do, lse, delta)


def _sgu_norm(hv, lg, lb):
    vg = _gelu(hv)
    mu = jnp.mean(vg, axis=-1, keepdims=True)
    d = vg - mu
    rstd = lax.rsqrt(jnp.mean(d * d, axis=-1, keepdims=True) + EPS)
    xhat = d * rstd
    return xhat, rstd, xhat * lg + lb


def _sgu_fwd(h, off_u, off_v, lg, lb, w16, bt, name):
    S = h.shape[0]
    T, G, C = SGU_CHUNK, SGU_GROUPS, SGU_DIM
    W = G * C

    def body(hu_ref, hv_ref, lg_ref, lb_ref, w_ref, bt_ref, y_ref):
        u = _gelu(hu_ref[...])
        _, _, vn = _sgu_norm(hv_ref[...], lg_ref[...], lb_ref[...])
        vn16 = vn.astype(BF16)
        for g in range(G):
            cols = slice(g * C, (g + 1) * C)
            mixed = jnp.dot(w_ref[g], vn16[:, cols], preferred_element_type=F32) + bt_ref[:, g:g + 1]
            y_ref[:, cols] = (u[:, cols] * mixed).astype(BF16)

    return pl.pallas_call(
        body, name=name, grid=(S // T,),
        in_specs=[_cols(T, W, off_u), _cols(T, W, off_v), _full((1, W)), _full((1, W)), _full((G, T, T)), _full((T, G))],
        out_specs=pl.BlockSpec((T, W), lambda n: (n, 0)),
        out_shape=jax.ShapeDtypeStruct((S, W), BF16),
        compiler_params=_params(("parallel",)),
    )(h, h, lg, lb, w16, bt)


def _sgu_bwd(h, off_u, off_v, lg, lb, w16, bt, dy, name):
    S = h.shape[0]
    T, G, C = SGU_CHUNK, SGU_GROUPS, SGU_DIM
    W = G * C
    nc = S // T

    def body(hu_ref, hv_ref, lg_ref, lb_ref, w_ref, bt_ref, dy_ref, dhu_ref, dhv_ref, dw_ref, db_ref, dlg_ref, dlb_ref,
             dmix_s, dvn_s):
        n = pl.program_id(0)

        @pl.when(n == 0)
        def _():
            dw_ref[...] = jnp.zeros_like(dw_ref)
            dlg_ref[...] = jnp.zeros_like(dlg_ref)
            dlb_ref[...] = jnp.zeros_like(dlb_ref)
            dmix_s[...] = jnp.zeros_like(dmix_s)

        hu, hv, lgv = hu_ref[...], hv_ref[...], lg_ref[...]
        u = _gelu(hu)
        xhat, rstd, vn = _sgu_norm(hv, lgv, lb_ref[...])
        vn16 = vn.astype(BF16)
        dyv = dy_ref[...]
        dmixed = dyv * u
        dmix_s[...] += dmixed
        dmixed16 = dmixed.astype(BF16)
        for g in range(G):
            cols = slice(g * C, (g + 1) * C)
            mixed = jnp.dot(w_ref[g], vn16[:, cols], preferred_element_type=F32) + bt_ref[:, g:g + 1]
            dhu_ref[:, cols] = (dyv[:, cols] * mixed * _gelu_grad(hu[:, cols])).astype(BF16)
            dvn_s[:, cols] = lax.dot_general(w_ref[g], dmixed16[:, cols], _DIMS["tn"], preferred_element_type=F32)
            dw_ref[g] += lax.dot_general(dmixed16[:, cols], vn16[:, cols], _DIMS["nt"], preferred_element_type=F32)
        dvn = dvn_s[...]
        dlg_ref[...] += jnp.sum(dvn * xhat, axis=0, keepdims=True)
        dlb_ref[...] += jnp.sum(dvn, axis=0, keepdims=True)
        dxh = dvn * lgv
        m1 = jnp.mean(dxh, axis=-1, keepdims=True)
        m2 = jnp.mean(dxh * xhat, axis=-1, keepdims=True)
        dvg = rstd * (dxh - m1 - xhat * m2)
        dhv_ref[...] = (dvg * _gelu_grad(hv)).astype(BF16)

        @pl.when(n == nc - 1)
        def _():
            tril = lax.broadcasted_iota(jnp.int32, (T, T), 1) <= lax.broadcasted_iota(jnp.int32, (T, T), 0)
            lane = lax.broadcasted_iota(jnp.int32, (T, 128), 1)
            db = jnp.zeros((T, 128), F32)
            for g in range(G):
                dw_ref[g] = jnp.where(tril, dw_ref[g], 0.0)
                db += jnp.where(lane == g, jnp.sum(dmix_s[:, g * C:(g + 1) * C], axis=1, keepdims=True), 0.0)
            db_ref[...] = db

    row = pl.BlockSpec((T, W), lambda n: (n, 0))
    return pl.pallas_call(
        body, name=name, grid=(nc,),
        in_specs=[_cols(T, W, off_u), _cols(T, W, off_v), _full((1, W)), _full((1, W)), _full((G, T, T)), _full((T, G)), row],
        out_specs=[row, row, _full((G, T, T)), _full((T, 128)), _full((1, W)), _full((1, W))],
        out_shape=[jax.ShapeDtypeStruct((S, W), BF16), jax.ShapeDtypeStruct((S, W), BF16),
                   jax.ShapeDtypeStruct((G, T, T), F32), jax.ShapeDtypeStruct((T, 128), F32),
                   jax.ShapeDtypeStruct((1, W), F32), jax.ShapeDtypeStruct((1, W), F32)],
        scratch_shapes=[pltpu.VMEM((T, W), F32), pltpu.VMEM((T, W), F32)],
        compiler_params=_params(("arbitrary",)),
    )(h, h, lg, lb, w16, bt, dy)


def _merge_fwd(ys, ps, h, bg, name):
    S = h.shape[0]
    D = ps[0].shape[1]
    tm = _tile(S, (512, 256, 128))
    tn = _tile(D, (512, 256, 128))
    nb = len(ys)

    def body(*refs):
        y_refs, p_refs, l_refs = refs[:nb], refs[nb:2 * nb], refs[2 * nb:3 * nb]
        bg_ref, mg_ref, z_ref = refs[3 * nb:]
        acc = jnp.zeros((tm, tn), F32)
        for b in range(nb):
            z = jnp.dot(y_refs[b][...].astype(BF16), p_refs[b][...], preferred_element_type=F32)
            z_ref[b] = z
            acc += _sigmoid(l_refs[b][...] + bg_ref[b:b + 1, :]) * z
        mg_ref[...] = acc.astype(BF16)

    in_specs = [pl.BlockSpec((tm, y.shape[1]), lambda i, j: (i, 0)) for y in ys]
    in_specs += [pl.BlockSpec((p.shape[0], tn), lambda i, j: (0, j)) for p in ps]
    in_specs += [pl.BlockSpec((tm, tn), functools.partial(lambda i, j, b: (i, b * (D // tn) + j), b=b)) for b in range(nb)]
    in_specs += [pl.BlockSpec((nb, tn), lambda i, j: (0, j))]
    return pl.pallas_call(
        body, name=name, grid=(S // tm, D // tn), in_specs=in_specs,
        out_specs=[pl.BlockSpec((tm, tn), lambda i, j: (i, j)), pl.BlockSpec((nb, tm, tn), lambda i, j: (0, i, j))],
        out_shape=[jax.ShapeDtypeStruct((S, D), BF16), jax.ShapeDtypeStruct((nb, S, D), F32)],
        compiler_params=_params(("parallel", "parallel")),
    )(*ys, *ps, *([h] * nb), bg)


def _merge_bwd(dm, z, h, bg, name):
    nb, S, D = z.shape
    tm = _tile(S, (256, 128))
    tn = _tile(D, (512, 256, 128))

    def body(*refs):
        dm_ref, z_ref = refs[0], refs[1]
        l_refs = refs[2:2 + nb]
        bg_ref, dz_ref, dl_ref, dbg_ref = refs[2 + nb:]

        @pl.when(pl.program_id(1) == 0)
        def _():
            dbg_ref[...] = jnp.zeros_like(dbg_ref)

        dmv = dm_ref[...]
        rows = lax.broadcasted_iota(jnp.int32, (SUBLANES, tn), 0)
        dbg = jnp.zeros((SUBLANES, tn), F32)
        for b in range(nb):
            gt = _sigmoid(l_refs[b][...] + bg_ref[b:b + 1, :])
            dz_ref[b] = (dmv * gt).astype(BF16)
            dl = dmv * z_ref[b] * gt * (1.0 - gt)
            dl_ref[b] = dl.astype(BF16)
            dbg += jnp.where(rows == b, jnp.sum(dl, axis=0, keepdims=True), 0.0)
        dbg_ref[...] += dbg

    in_specs = [pl.BlockSpec((tm, tn), lambda j, i: (i, j)), pl.BlockSpec((nb, tm, tn), lambda j, i: (0, i, j))]
    in_specs += [pl.BlockSpec((tm, tn), functools.partial(lambda j, i, b: (i, b * (D // tn) + j), b=b)) for b in range(nb)]
    in_specs += [pl.BlockSpec((nb, tn), lambda j, i: (0, j))]
    blk3 = pl.BlockSpec((nb, tm, tn), lambda j, i: (0, i, j))
    return pl.pallas_call(
        body, name=name, grid=(D // tn, S // tm), in_specs=in_specs,
        out_specs=[blk3, blk3, pl.BlockSpec((SUBLANES, tn), lambda j, i: (0, j))],
        out_shape=[jax.ShapeDtypeStruct((nb, S, D), BF16), jax.ShapeDtypeStruct((nb, S, D), BF16),
                   jax.ShapeDtypeStruct((SUBLANES, D), F32)],
        compiler_params=_params(("parallel", "arbitrary")),
    )(dm, z, *([h] * nb), bg)


def _shift_down(x, halo, k):
    xr = pltpu.roll(x, k, 0)
    hr = pltpu.roll(halo, k, 0)
    rows = lax.broadcasted_iota(jnp.int32, halo.shape, 0)
    top = jnp.where(rows < k, hr, xr[:SUBLANES])
    return jnp.concatenate([top, xr[SUBLANES:]], axis=0)


def _shift_up(x, halo, k):
    tm = x.shape[0]
    xr = pltpu.roll(x, tm - k, 0)
    hr = pltpu.roll(halo, SUBLANES - k, 0)
    rows = lax.broadcasted_iota(jnp.int32, halo.shape, 0)
    bot = jnp.where(rows >= SUBLANES - k, hr, xr[tm - SUBLANES:])
    return jnp.concatenate([xr[:tm - SUBLANES], bot], axis=0)


def _conv_tiles(S, F):
    return _tile(S, (ROW_TILE, 256, 128)), _tile(F, (512, 256, 128))


def _conv_in_specs(tm, tn, F):
    r8 = tm // SUBLANES
    nf = F // tn
    specs = []
    for half in range(2):
        specs.append(pl.BlockSpec((tm, tn), functools.partial(lambda j, i, o: (i, o + j), o=half * nf)))
        specs.append(pl.BlockSpec((SUBLANES, tn), functools.partial(lambda j, i, o: (jnp.maximum(i * r8 - 1, 0), o + j), o=half * nf)))
    for half in range(2):
        specs.append(pl.BlockSpec((3, tn), functools.partial(lambda j, i, o: (0, o + j), o=half * nf)))
        specs.append(pl.BlockSpec((1, tn), functools.partial(lambda j, i, o: (0, o + j), o=half * nf)))
    return specs


def _conv_apply(x, halo, w, b, first):
    halo = jnp.where(first, 0.0, halo)
    x1 = _shift_down(x, halo, 1)
    x2 = _shift_down(x, halo, 2)
    return b + x2 * w[0:1, :] + x1 * w[1:2, :] + x * w[2:3, :], x1, x2


def _glu_fwd(up, cw, cb, name):
    S, F2 = up.shape
    F = F2 // 2
    tm, tn = _conv_tiles(S, F)

    def body(ug, hg, uv, hv, wg, bgr, wv, bvr, a_ref):
        first = pl.program_id(1) == 0
        cg, _, _ = _conv_apply(ug[...], hg[...], wg[...], bgr[...], first)
        cv, _, _ = _conv_apply(uv[...], hv[...], wv[...], bvr[...], first)
        a_ref[...] = (cg * _sigmoid(cg) * cv).astype(BF16)

    return pl.pallas_call(
        body, name=name, grid=(F // tn, S // tm), in_specs=_conv_in_specs(tm, tn, F),
        out_specs=pl.BlockSpec((tm, tn), lambda j, i: (i, j)),
        out_shape=jax.ShapeDtypeStruct((S, F), BF16),
        compiler_params=_params(("parallel", "parallel")),
    )(up, up, up, up, cw, cb, cw, cb)


def _glu_bwd(up, cw, cb, da, name):
    S, F2 = up.shape
    F = F2 // 2
    tm, tn = _conv_tiles(S, F)

    def body(ug, hg, uv, hv, wg, bgr, wv, bvr, da_ref, dg_ref, dv_ref, sg_ref, sv_ref):
        i = pl.program_id(1)

        @pl.when(i == 0)
        def _():
            sg_ref[...] = jnp.zeros_like(sg_ref)
            sv_ref[...] = jnp.zeros_like(sv_ref)

        first = i == 0
        xg, xv = ug[...], uv[...]
        cg, xg1, xg2 = _conv_apply(xg, hg[...], wg[...], bgr[...], first)
        cv, xv1, xv2 = _conv_apply(xv, hv[...], wv[...], bvr[...], first)
        dav = da_ref[...]
        sg = _sigmoid(cg)
        dcv = dav * cg * sg
        dcg = dav * cv * sg * (1.0 + cg * (1.0 - sg))
        dg_ref[...] = dcg
        dv_ref[...] = dcv
        rows = lax.broadcasted_iota(jnp.int32, (SUBLANES, tn), 0)

        def stats(dc, x, x1, x2):
            acc = jnp.zeros((SUBLANES, tn), F32)
            for r, val in enumerate((dc * x2, dc * x1, dc * x, dc)):
                acc += jnp.where(rows == r, jnp.sum(val, axis=0, keepdims=True), 0.0)
            return acc

        sg_ref[...] += stats(dcg, xg, xg1, xg2)
        sv_ref[...] += stats(dcv, xv, xv1, xv2)

    tile = pl.BlockSpec((tm, tn), lambda j, i: (i, j))
    stat = pl.BlockSpec((SUBLANES, tn), lambda j, i: (0, j))
    return pl.pallas_call(
        body, name=name, grid=(F // tn, S // tm), in_specs=_conv_in_specs(tm, tn, F) + [tile],
        out_specs=[tile, tile, stat, stat],
        out_shape=[jax.ShapeDtypeStruct((S, F), F32), jax.ShapeDtypeStruct((S, F), F32),
                   jax.ShapeDtypeStruct((SUBLANES, F), F32), jax.ShapeDtypeStruct((SUBLANES, F), F32)],
        compiler_params=_params(("parallel", "arbitrary")),
    )(up, up, up, up, cw, cb, cw, cb, da)


def _conv_bwd(dc, w, name):
    S, F = dc.shape
    tm, tn = _conv_tiles(S, F)
    r8 = tm // SUBLANES
    ni = S // tm

    def body(x_ref, h_ref, w_ref, o_ref):
        x = x_ref[...]
        halo = jnp.where(pl.program_id(1) == ni - 1, 0.0, h_ref[...])
        wv = w_ref[...]
        o_ref[...] = (x * wv[2:3, :] + _shift_up(x, halo, 1) * wv[1:2, :] + _shift_up(x, halo, 2) * wv[0:1, :]).astype(BF16)

    return pl.pallas_call(
        body, name=name, grid=(F // tn, ni),
        in_specs=[pl.BlockSpec((tm, tn), lambda j, i: (i, j)),
                  pl.BlockSpec((SUBLANES, tn), lambda j, i: (jnp.minimum((i + 1) * r8, S // SUBLANES - 1), j)),
                  pl.BlockSpec((3, tn), lambda j, i: (0, j))],
        out_specs=pl.BlockSpec((tm, tn), lambda j, i: (i, j)),
        out_shape=jax.ShapeDtypeStruct((S, F), BF16),
        compiler_params=_params(("parallel", "parallel")),
    )(dc, dc, w)


def _adamw(slot_list, w, m, v, name):
    L = len(slot_list)
    P, K, C = slot_list[0].shape
    tr = _tile(K, (256, 128, 64, 32, 16))
    while tr * C * 4 > (1 << 20) and tr % 32 == 0:
        tr //= 2
    nb = K // tr

    def body(*refs):
        s_refs = refs[:L]
        w_ref, m_ref, v_ref, g_ref, d_ref, nm_ref, nv_ref = refs[L:]
        layer = pl.program_id(0)
        g = None
        for l in range(L):
            gl = s_refs[l][0].astype(F32)
            for p in range(1, P):
                gl = gl + s_refs[l][p].astype(F32)
            g = gl if g is None else jnp.where(layer == l, gl, g)
        nm = ADAM_B1 * m_ref[...] + (1.0 - ADAM_B1) * g
        nv = ADAM_B2 * v_ref[...] + (1.0 - ADAM_B2) * (g * g)
        m_hat = nm / (1.0 - ADAM_B1 ** ADAM_STEP)
        v_hat = nv / (1.0 - ADAM_B2 ** ADAM_STEP)
        g_ref[...] = g
        d_ref[...] = -ADAM_LR * (m_hat / (jnp.sqrt(v_hat) + ADAM_EPS) + ADAM_WD * w_ref[...])
        nm_ref[...] = nm
        nv_ref[...] = nv

    blk = pl.BlockSpec((tr, C), lambda li, i: (li * nb + i, 0))
    slot_specs = [pl.BlockSpec((P, tr, C), functools.partial(lambda li, i, l: (0, jnp.where(li == l, i, 0), 0), l=l))
                  for l in range(L)]
    return pl.pallas_call(
        body, name=name, grid=(L, nb), in_specs=slot_specs + [blk, blk, blk],
        out_specs=[blk] * 4, out_shape=[jax.ShapeDtypeStruct((L * K, C), F32)] * 4,
        compiler_params=_params(("arbitrary", "arbitrary")),
    )(*slot_list, w, m, v)


def _exchange(srcs, scatter, name):
    na = len(srcs)
    out_shape = [jax.ShapeDtypeStruct(s.shape if scatter else (N_DEV,) + s.shape, s.dtype) for s in srcs]

    def body(*refs):
        src_refs, out_refs = refs[:na], refs[na:2 * na]
        send_sems, recv_sems, local_sems = refs[2 * na:]
        x, y, c = lax.axis_index("x"), lax.axis_index("y"), lax.axis_index("c")
        me = 4 * x + 2 * y + c

        def flip(v, bit):
            return 1 - v if bit else v

        def peer(k):
            return (flip(x, (k >> 2) & 1), flip(y, (k >> 1) & 1), flip(c, k & 1))

        def peer_index(k):
            px, py, pc = peer(k)
            return 4 * px + 2 * py + pc

        def remote(a, k):
            src = src_refs[a].at[peer_index(k)] if scatter else src_refs[a]
            return pltpu.make_async_remote_copy(
                src_ref=src, dst_ref=out_refs[a].at[me], send_sem=send_sems.at[a, k - 1], recv_sem=recv_sems.at[a, k - 1],
                device_id=peer(k), device_id_type=pl.DeviceIdType.MESH)

        def arrival(a, k):
            src = src_refs[a].at[me] if scatter else src_refs[a]
            return pltpu.make_async_remote_copy(
                src_ref=src, dst_ref=out_refs[a].at[peer_index(k)], send_sem=send_sems.at[a, k - 1],
                recv_sem=recv_sems.at[a, k - 1], device_id=peer(k), device_id_type=pl.DeviceIdType.MESH)

        own = [pltpu.make_async_copy(src_refs[a].at[me] if scatter else src_refs[a], out_refs[a].at[me], local_sems.at[a])
               for a in range(na)]
        sends = [remote(a, k) for k in range(1, N_DEV) for a in range(na)]
        for cp in own + sends:
            cp.start()
        for k in range(1, N_DEV):
            for a in range(na):
                arrival(a, k).wait_recv()
        for cp in sends:
            cp.wait_send()
        for cp in own:
            cp.wait()

    any_spec = pl.BlockSpec(memory_space=pl.ANY)
    return pl.pallas_call(
        body, name=name, in_specs=[any_spec] * na, out_specs=[any_spec] * na, out_shape=out_shape,
        scratch_shapes=[pltpu.SemaphoreType.DMA((na, N_DEV - 1)), pltpu.SemaphoreType.DMA((na, N_DEV - 1)),
                        pltpu.SemaphoreType.DMA((na,))],
    )(*srcs)


_HBM = pl.BlockSpec(memory_space=pltpu.HBM)
_SEM = pl.BlockSpec(memory_space=pltpu.SEMAPHORE)
_ANY = pl.BlockSpec(memory_space=pl.ANY)


def _peers():
    x, y, c = lax.axis_index("x"), lax.axis_index("y"), lax.axis_index("c")

    def flip(v, bit):
        return 1 - v if bit else v

    def peer(k):
        return (flip(x, (k >> 2) & 1), flip(y, (k >> 1) & 1), flip(c, k & 1))

    def peer_index(k):
        px, py, pc = peer(k)
        return 4 * px + 2 * py + pc

    return 4 * x + 2 * y + c, peer, peer_index


def _split_copy(src_refs, land_refs, send_sems, recv_sems, scatter, a, k, outgoing):
    me, peer, peer_index = _peers()
    if outgoing:
        src = src_refs[a].at[peer_index(k)] if scatter else src_refs[a]
        dst = land_refs[a].at[me]
    else:
        src = src_refs[a].at[me] if scatter else src_refs[a]
        dst = land_refs[a].at[peer_index(k)]
    pair = a * (N_DEV - 1) + k - 1
    return pltpu.make_async_remote_copy(src_ref=src, dst_ref=dst, send_sem=send_sems.at[pair],
                                        recv_sem=recv_sems.at[pair], device_id=peer(k),
                                        device_id_type=pl.DeviceIdType.MESH)


def _exchange_start(srcs, scatter, after, name):
    na = len(srcs)
    land_shapes = [s.shape if scatter else (N_DEV,) + s.shape for s in srcs]
    has_after = after is not None

    def body(*refs):
        src_refs, land_refs = refs[:na], refs[na:2 * na]
        send_sems, recv_sems = refs[2 * na + has_after], refs[2 * na + has_after + 1]
        token = refs[-1]
        for k in range(1, N_DEV):
            for a in range(na):
                _split_copy(src_refs, land_refs, send_sems, recv_sems, scatter, a, k, True).start()
        token[...] = jnp.zeros_like(token)

    sems = pltpu.SemaphoreType.DMA((na * (N_DEV - 1),))
    out_shape = ([sems, sems] + [pltpu.HBM(s.shape, s.dtype) for s in srcs]
                 + [pltpu.HBM(shp, s.dtype) for shp, s in zip(land_shapes, srcs)] + [jax.ShapeDtypeStruct((SUBLANES, 128), F32)])
    args = [pltpu.with_memory_space_constraint(s, pltpu.HBM) for s in srcs]
    args += [pltpu.with_memory_space_constraint(lax.empty(shp, s.dtype), pltpu.HBM) for shp, s in zip(land_shapes, srcs)]
    if has_after:
        args.append(after)
    res = pl.pallas_call(
        body, name=name, in_specs=[_HBM] * (2 * na) + [_ANY] * has_after,
        out_specs=[_SEM, _SEM] + [_HBM] * (2 * na) + [pl.BlockSpec(memory_space=pltpu.VMEM)], out_shape=out_shape,
        input_output_aliases={i: 2 + i for i in range(2 * na)},
        compiler_params=pltpu.CompilerParams(has_side_effects=pltpu.SideEffectType.DATAFLOW_SIDE_EFFECTING),
    )(*args)
    handle = dict(send=res[0], recv=res[1], srcs=list(res[2:2 + na]), lands=list(res[2 + na:2 + 2 * na]), scatter=scatter)
    return handle, res[-1]


def _exchange_wait(handle, after, name):
    srcs, lands, scatter = handle["srcs"], handle["lands"], handle["scatter"]
    na = len(srcs)

    def body(*refs):
        src_refs, land_refs = refs[:na], refs[na:2 * na]
        send_sems, recv_sems = refs[2 * na], refs[2 * na + 1]
        for k in range(1, N_DEV):
            for a in range(na):
                _split_copy(src_refs, land_refs, send_sems, recv_sems, scatter, a, k, True).wait_send()
                _split_copy(src_refs, land_refs, send_sems, recv_sems, scatter, a, k, False).wait_recv()

    res = pl.pallas_call(
        body, name=name, in_specs=[_HBM] * (2 * na) + [_SEM, _SEM, _ANY], out_specs=[_HBM] * (2 * na),
        out_shape=[pltpu.HBM(s.shape, s.dtype) for s in srcs] + [pltpu.HBM(s.shape, s.dtype) for s in lands],
        input_output_aliases={i: i for i in range(2 * na)},
        compiler_params=pltpu.CompilerParams(has_side_effects=pltpu.SideEffectType.DATAFLOW_SIDE_EFFECTING),
    )(*srcs, *lands, handle["send"], handle["recv"], after)
    return list(res[:na]), list(res[na:])


def _fill_own(lands, srcs, scatter, name):
    na = len(lands)

    def body(*refs):
        src_refs, out_refs, sems = refs[na:2 * na], refs[2 * na:3 * na], refs[3 * na]
        me = 4 * lax.axis_index("x") + 2 * lax.axis_index("y") + lax.axis_index("c")
        copies = [pltpu.make_async_copy(src_refs[a].at[me] if scatter else src_refs[a], out_refs[a].at[me], sems.at[a])
                  for a in range(na)]
        for cp in copies:
            cp.start()
        for cp in copies:
            cp.wait()

    return pl.pallas_call(
        body, name=name, in_specs=[_ANY] * (2 * na), out_specs=[_ANY] * na,
        out_shape=[jax.ShapeDtypeStruct(s.shape, s.dtype) for s in lands],
        input_output_aliases={i: i for i in range(na)}, scratch_shapes=[pltpu.SemaphoreType.DMA((na,))],
    )(*lands, *srcs)


def _layout(D):
    aq, akv = SWA_Q_HEADS * SWA_HEAD_DIM, SWA_KV_HEADS * SWA_HEAD_DIM
    w = SGU_GROUPS * SGU_DIM
    return aq, akv, w


class _Seg:
    def __init__(self, D, rq, rkv):
        aq, akv, w = _layout(D)
        src = {}
        o = 0
        for nm, wd in (("qa", aq), ("ka", akv), ("va", akv), ("cq", rq), ("ckv", rkv), ("kr", MLA_ROPE), ("hu", w), ("hv", w),
                       ("g", 3 * D)):
            src[nm] = (o, wd)
            o += wd
        self.n_in = o
        self.order = ("g", "qa", "hu", "hv", "cq", "ckv", "ka", "va", "kr")
        self.src = src
        self.off = {}
        o = 0
        for nm in self.order:
            self.off[nm] = o
            o += src[nm][1]
        self.width = {nm: src[nm][1] for nm in self.order}
        self.n_pad = -(-o // 1536) * 1536 if o > 1536 else -(-o // 512) * 512
        self.used = o

    def permute(self, w):
        parts = [w[:, self.src[nm][0]:self.src[nm][0] + self.src[nm][1]] for nm in self.order]
        parts.append(jnp.zeros((w.shape[0], self.n_pad - self.used), w.dtype))
        return jnp.concatenate(parts, axis=1)

    def unpermute(self, w):
        names = sorted(self.order, key=lambda nm: self.src[nm][0])
        return jnp.concatenate([w[:, self.off[nm]:self.off[nm] + self.width[nm]] for nm in names], axis=1)


def _uq_permute(w):
    R = w.shape[0]
    H, half = MLA_HEADS, MLA_ROPE // 2
    w3 = w.reshape(R, H, MLA_NOPE + MLA_ROPE)
    return jnp.concatenate([w3[:, :, :MLA_NOPE].reshape(R, H * MLA_NOPE),
                            w3[:, :, MLA_NOPE:MLA_NOPE + half].reshape(R, H * half),
                            w3[:, :, MLA_NOPE + half:].reshape(R, H * half)], axis=1)


def _uq_unpermute(w):
    R = w.shape[0]
    H, half = MLA_HEADS, MLA_ROPE // 2
    n = w[:, :H * MLA_NOPE].reshape(R, H, MLA_NOPE)
    r1 = w[:, H * MLA_NOPE:H * (MLA_NOPE + half)].reshape(R, H, half)
    r2 = w[:, H * (MLA_NOPE + half):].reshape(R, H, half)
    return jnp.concatenate([n, r1, r2], axis=2).reshape(R, H * (MLA_NOPE + MLA_ROPE))


def _ukv_permute(w):
    R = w.shape[0]
    w3 = w.reshape(R, MLA_HEADS, MLA_NOPE + MLA_V)
    return jnp.concatenate([w3[:, :, :MLA_NOPE].reshape(R, -1), w3[:, :, MLA_NOPE:].reshape(R, -1)], axis=1)


def _ukv_unpermute(w):
    R = w.shape[0]
    H = MLA_HEADS
    k = w[:, :H * MLA_NOPE].reshape(R, H, MLA_NOPE)
    v = w[:, H * MLA_NOPE:].reshape(R, H, MLA_V)
    return jnp.concatenate([k, v], axis=2).reshape(R, H * (MLA_NOPE + MLA_V))


def _heads(a, d):
    S = a.shape[0]
    return a.reshape(S, MLA_HEADS, d).transpose(1, 0, 2)


def _unheads(a):
    H, S, d = a.shape
    return a.transpose(1, 0, 2).reshape(S, H * d)


GROUPS = {"a": ("w_in",), "b": ("w_uq", "w_ukv", "w_proj_a", "w_proj_b", "w_proj_c", "w_o", "b_gate"),
          "c": ("w_up", "w_down", "conv_w")}


def _layer_fwd(l, x, x16, fetch, P, cs, sn, seg, alpha):
    S, D = x.shape
    H, half = MLA_HEADS, MLA_ROPE // 2
    off = seg.off
    nm = lambda s: f"l{l}_{s}"
    sv = {"x16": x16}
    W = {"w_in": seg.permute(fetch(l, "a", x16)["w_in"])}
    h = _mm(x16, W["w_in"], "nn", [(F32, "n")], nm("h"))[0]
    sv["h"] = h
    ya, lse_a = _swa_fwd(h, off["qa"], off["ka"], off["va"], P["sinks"], nm("swa_fwd"))
    cqn, rq = _rms_fwd(h, off["cq"], seg.width["cq"], P["q_norm_g"], nm("rmsq_fwd"))
    ckvn, rkv = _rms_fwd(h, off["ckv"], seg.width["ckv"], P["kv_norm_g"], nm("rmskv_fwd"))
    W.update(fetch(l, "b", cqn))
    W["w_uq"] = _uq_permute(W["w_uq"])
    W["w_ukv"] = _ukv_permute(W["w_ukv"])
    qf = _mm(cqn, W["w_uq"], "nn", [(F32, "n")], nm("uq"))[0]
    kvf = _mm(ckvn, W["w_ukv"], "nn", [(BF16, "n")], nm("ukv"))[0]
    cs_h, sn_h = jnp.tile(cs, (1, H)), jnp.tile(sn, (1, H))
    qy1, qy2 = _rope((qf, H * MLA_NOPE), (qf, H * MLA_NOPE + H * half), cs_h, sn_h, H * half, False, nm("ropeq_fwd"))
    kr = h[:, off["kr"]:off["kr"] + MLA_ROPE]
    ky1, ky2 = _rope((kr[:, :half], 0), (kr[:, half:], 0), cs, sn, half, False, nm("ropek_fwd"))
    qh = jnp.concatenate([qf[:, :H * MLA_NOPE].astype(BF16).reshape(S, H, MLA_NOPE), qy1.reshape(S, H, half),
                          qy2.reshape(S, H, half)], axis=2).transpose(1, 0, 2)
    kh = jnp.concatenate([kvf[:, :H * MLA_NOPE].reshape(S, H, MLA_NOPE),
                          jnp.broadcast_to(ky1[:, None, :], (S, H, half)),
                          jnp.broadcast_to(ky2[:, None, :], (S, H, half))], axis=2).transpose(1, 0, 2)
    vh = _heads(kvf[:, H * MLA_NOPE:], MLA_V)
    ob, lse_b = _mla_fwd(qh, kh, vh, nm("mla_fwd"))
    yb = _unheads(ob).astype(BF16)
    w16 = jnp.where(jnp.tril(jnp.ones((SGU_CHUNK, SGU_CHUNK), bool))[None], P["sgu_w"], 0.0).astype(BF16)
    bt = P["sgu_b"].T
    yc = _sgu_fwd(h, off["hu"], off["hv"], P["sgu_ln_g"], P["sgu_ln_b"], w16, bt, nm("sgu_fwd"))
    merged, z = _merge_fwd([ya, yb, yc], [W["w_proj_a"], W["w_proj_b"], W["w_proj_c"]], h, W["b_gate"], nm("merge_fwd"))
    x1, x1_16, xh1, rs1 = _mm_ln(merged, W["w_o"], x, P["ln1_g"], P["ln1_b"], alpha, nm("wo_ln1"))
    W.update(fetch(l, "c", x1_16))
    up = _mm(x1_16, W["w_up"], "nn", [(F32, "n")], nm("up"))[0]
    a = _glu_fwd(up, W["conv_w"], P["conv_b"], nm("glu_fwd"))
    x2, x2_16, xh2, rs2 = _mm_ln(a, W["w_down"], x1, P["ln2_g"], P["ln2_b"], alpha, nm("down_ln2"))
    sv.update(W=W, ya=ya, lse_a=lse_a, cqn=cqn, rq=rq, ckvn=ckvn, rkv=rkv, qh=qh, kh=kh, vh=vh, ob=ob, lse_b=lse_b, yb=yb,
              w16=w16, bt=bt, yc=yc, merged=merged, z=z, x1_16=x1_16, xh1=xh1, rs1=rs1, up=up, a=a, xh2=xh2, rs2=rs2,
              cs_h=cs_h, sn_h=sn_h)
    return x2, x2_16, sv


def _after(arr, token):
    return arr if token is None else arr + token[0:1, 0:1].astype(arr.dtype)


def _layer_bwd(l, dx2, sv, P, cs, sn, seg, alpha, emit):
    S, D = dx2.shape
    H, half = MLA_HEADS, MLA_ROPE // 2
    off = seg.off
    h, W = sv["h"], sv["W"]
    nm = lambda s: f"l{l}_{s}"
    g = {}
    dr2, dr2_16, g["ln2_g"], g["ln2_b"] = _ln_bwd(dx2, sv["xh2"], sv["rs2"], P["ln2_g"], nm("ln2_bwd"))
    g["w_down"] = _mm(sv["a"], dr2_16, "tn", [(F32, "n")], nm("dw_down"))[0]
    da = _mm(dr2_16, W["w_down"], "nt", [(F32, "n")], nm("da"))[0]
    dcg, dcv, st_g, st_v = _glu_bwd(sv["up"], W["conv_w"], P["conv_b"], da, nm("glu_bwd"))
    F = dcg.shape[1]
    g["conv_w"] = jnp.concatenate([st_g[0:3], st_v[0:3]], axis=1)
    g["conv_b"] = jnp.concatenate([st_g[3:4], st_v[3:4]], axis=1)
    dup = jnp.concatenate([_conv_bwd(dcg, W["conv_w"][:, :F], nm("convg_bwd")),
                           _conv_bwd(dcv, W["conv_w"][:, F:], nm("convv_bwd"))], axis=1)
    g["w_up"] = _mm(sv["x1_16"], dup, "tn", [(F32, "n")], nm("dw_up"))[0]
    token = emit(l, "c", {k: g.pop(k) for k in GROUPS["c"]})
    dx1 = _mm_axpy(dup, W["w_up"], "nt", dr2, alpha, nm("dx1"), dep=token)
    dr1, dr1_16, g["ln1_g"], g["ln1_b"] = _ln_bwd(dx1, sv["xh1"], sv["rs1"], P["ln1_g"], nm("ln1_bwd"))
    g["w_o"] = _mm(sv["merged"], dr1_16, "tn", [(F32, "n")], nm("dw_o"))[0]
    dmerged = _mm(dr1_16, W["w_o"], "nt", [(F32, "n")], nm("dmerged"))[0]
    dz, dlog, dbg = _merge_bwd(dmerged, sv["z"], h, W["b_gate"], nm("merge_bwd"))
    g["b_gate"] = dbg[0:3]
    g["w_proj_a"] = _mm(sv["ya"], dz[0], "tn", [(F32, "n")], nm("dw_pa"))[0]
    g["w_proj_b"] = _mm(sv["yb"], dz[1], "tn", [(F32, "n")], nm("dw_pb"))[0]
    g["w_proj_c"] = _mm(sv["yc"], dz[2], "tn", [(F32, "n")], nm("dw_pc"))[0]
    dya = _mm(dz[0], W["w_proj_a"], "nt", [(F32, "n")], nm("dya"))[0]
    dyb = _mm(dz[1], W["w_proj_b"], "nt", [(F32, "n")], nm("dyb"))[0]
    dyc = _mm(dz[2], W["w_proj_c"], "nt", [(F32, "n")], nm("dyc"))[0]
    dhu, dhv, g["sgu_w"], db_s, g["sgu_ln_g"], g["sgu_ln_b"] = _sgu_bwd(
        h, off["hu"], off["hv"], P["sgu_ln_g"], P["sgu_ln_b"], sv["w16"], sv["bt"], dyc, nm("sgu_bwd"))
    g["sgu_b"] = db_s[:, :SGU_GROUPS].T
    dqa, dka, dva, dsk = _swa_bwd(h, off["qa"], off["ka"], off["va"], P["sinks"], dya, sv["lse_a"], nm("swa_bwd"))
    g["sinks"] = dsk[0, :SWA_Q_HEADS]
    dob = _heads(dyb, MLA_V)
    delta = _rowdot(dob, sv["ob"], nm("mla_delta"))
    dqh, dkh, dvh = _mla_bwd(sv["qh"], sv["kh"], sv["vh"], dob.astype(BF16), sv["lse_b"], delta, nm("mla_bwd"))
    n0, n1 = MLA_NOPE, MLA_NOPE + half
    dqx1, dqx2 = _rope((_unheads(dqh[:, :, n0:n1]), 0), (_unheads(dqh[:, :, n1:]), 0), sv["cs_h"], sv["sn_h"], H * half, True,
                       nm("ropeq_bwd"))
    dqf = jnp.concatenate([_unheads(dqh[:, :, :n0]).astype(BF16), dqx1, dqx2], axis=1)
    dkvf = jnp.concatenate([_unheads(dkh[:, :, :n0]), _unheads(dvh)], axis=1).astype(BF16)
    dk1, dk2 = _rope(dkh[:, :, n0:n1], dkh[:, :, n1:], cs, sn, half, True, nm("ropek_bwd"))
    g["w_uq"] = _mm(sv["cqn"], dqf, "tn", [(F32, "n")], nm("dw_uq"))[0]
    g["w_ukv"] = _mm(sv["ckvn"], dkvf, "tn", [(F32, "n")], nm("dw_ukv"))[0]
    g["w_uq"] = _uq_unpermute(g["w_uq"])
    g["w_ukv"] = _ukv_unpermute(g["w_ukv"])
    token = emit(l, "b", {k: g.pop(k) for k in GROUPS["b"]})
    dcqn = _mm(dqf, W["w_uq"], "nt", [(F32, "n")], nm("dcqn"), dep=token)[0]
    dckvn = _mm(dkvf, W["w_ukv"], "nt", [(F32, "n")], nm("dckvn"), dep=token)[0]
    dcq, g["q_norm_g"] = _rms_bwd(dcqn, h, off["cq"], seg.width["cq"], sv["rq"], P["q_norm_g"], nm("rmsq_bwd"))
    dckv, g["kv_norm_g"] = _rms_bwd(dckvn, h, off["ckv"], seg.width["ckv"], sv["rkv"], P["kv_norm_g"], nm("rmskv_bwd"))
    parts = {"g": jnp.concatenate([dlog[0], dlog[1], dlog[2]], axis=1), "qa": dqa, "hu": dhu, "hv": dhv, "cq": dcq, "ckv": dckv,
             "ka": dka.astype(BF16), "va": dva.astype(BF16), "kr": jnp.concatenate([dk1, dk2], axis=1)}
    dh = jnp.concatenate([parts[k] for k in seg.order] + [jnp.zeros((S, seg.n_pad - seg.used), BF16)], axis=1)
    token = emit(l, "a", {"w_in": seg.unpermute(_mm(sv["x16"], dh, "tn", [(F32, "n")], nm("dw_in"))[0])})
    dx = _mm_axpy(dh, W["w_in"], "nt", dr1, alpha, nm("dx"), dep=token)
    return dx, g


BIG = ("w_in", "w_uq", "w_ukv", "w_proj_a", "w_proj_b", "w_proj_c", "w_o", "w_up", "w_down")
ROW_SHARDED = ("w_proj_b", "w_o", "w_down")
SHARDED_F32 = ("b_gate", "conv_w")
REPLICATED = ("sinks", "q_norm_g", "kv_norm_g", "sgu_ln_g", "sgu_ln_b", "sgu_w", "sgu_b", "ln1_g", "ln1_b", "conv_b", "ln2_g",
              "ln2_b")
WEIGHTS = ("w_in", "b_gate", "sinks", "q_norm_g", "kv_norm_g", "w_uq", "w_ukv", "sgu_ln_g", "sgu_ln_b", "sgu_w", "sgu_b",
           "w_proj_a", "w_proj_b", "w_proj_c", "w_o", "ln1_g", "ln1_b", "w_up", "conv_w", "conv_b", "w_down", "ln2_g", "ln2_b")


def _step_local(x, positions, target, small, rq, rkv, fetch, emit, token=None):
    S, D = x.shape
    L = small["sinks"].shape[0]
    alpha = (2 * L) ** 0.25
    seg = _Seg(D, rq, rkv)
    inv_freq = ROPE_THETA ** (-jnp.arange(0, MLA_ROPE, 2, dtype=F32) / MLA_ROPE)
    ang = positions.astype(F32)[:, None] * inv_freq
    cs, sn = jnp.cos(ang), jnp.sin(ang)
    rows = ("q_norm_g", "kv_norm_g", "sgu_ln_g", "sgu_ln_b", "ln1_g", "ln1_b", "conv_b", "ln2_g", "ln2_b")
    layers = [{k: small[k][l].reshape(1, -1) if k in rows else small[k][l] for k in small} for l in range(L)]
    saved = []
    x16 = _after(x, token).astype(BF16)
    for l in range(L):
        x, x16, sv = _layer_fwd(l, x, x16, fetch, layers[l], cs, sn, seg, alpha)
        saved.append(sv)
    loss, dx = _loss(x, target, "loss")
    grads = [None] * L
    for l in reversed(range(L)):
        dx, grads[l] = _layer_bwd(l, dx, saved[l], layers[l], cs, sn, seg, alpha, emit)
    out = {k: jnp.stack([grads[l][k].reshape(small[k].shape[1:]) for l in range(L)]) for k in small}
    return loss, dx, out


def _unshard(k, gathered):
    n, r, c = gathered.shape
    if k in ROW_SHARDED:
        return gathered.reshape(n * r, c)
    return gathered.transpose(1, 0, 2).reshape(r, n * c)


def _to_chunks(k, gfull):
    r, c = gfull.shape
    if k in ROW_SHARDED:
        return gfull.reshape(N_DEV, r // N_DEV, c)
    return gfull.reshape(r, N_DEV, c // N_DEV).transpose(1, 0, 2)


def _pack(arrs):
    P = arrs[0].shape[0]
    flat, sizes = [], []
    for a in arrs:
        f = a.reshape(P, -1)
        n = f.shape[1]
        pad = -n % (SUBLANES * 128)
        flat.append(jnp.pad(f, ((0, 0), (0, pad))))
        sizes.append((n, n + pad))
    return jnp.concatenate(flat, axis=1).reshape(P, -1, 128), sizes


def _unpack(packed, sizes, shapes):
    flat = packed.reshape(-1)
    out, o = [], 0
    for (n, npad), shp in zip(sizes, shapes):
        out.append(flat[o:o + n].reshape(shp))
        o += npad
    return out


def kernel(x, positions, w_in, b_gate, sinks, q_norm_g, kv_norm_g, w_uq, w_ukv, sgu_ln_g, sgu_ln_b, sgu_w, sgu_b, w_proj_a, w_proj_b, w_proj_c, w_o, ln1_g, ln1_b, w_up, conv_w, conv_b, w_down, ln2_g, ln2_b, loss_target, m_w_in, m_b_gate, m_sinks, m_q_norm_g, m_kv_norm_g, m_w_uq, m_w_ukv, m_sgu_ln_g, m_sgu_ln_b, m_sgu_w, m_sgu_b, m_w_proj_a, m_w_proj_b, m_w_proj_c, m_w_o, m_ln1_g, m_ln1_b, m_w_up, m_conv_w, m_conv_b, m_w_down, m_ln2_g, m_ln2_b, v_w_in, v_b_gate, v_sinks, v_q_norm_g, v_kv_norm_g, v_w_uq, v_w_ukv, v_sgu_ln_g, v_sgu_ln_b, v_sgu_w, v_sgu_b, v_w_proj_a, v_w_proj_b, v_w_proj_c, v_w_o, v_ln1_g, v_ln1_b, v_w_up, v_conv_w, v_conv_b, v_w_down, v_ln2_g, v_ln2_b):
    given = dict(locals())
    w = {k: given[k] for k in WEIGHTS}
    mom = {k: given["m_" + k] for k in WEIGHTS}
    var = {k: given["v_" + k] for k in WEIGHTS}

    L = w_in.shape[0]
    order = [(l, grp) for l in range(L) for grp in ("a", "b", "c")]

    gathers, token = {}, None
    for l, grp in order:
        srcs = [w[k][l].astype(BF16) if k in BIG else w[k][l] for k in GROUPS[grp]]
        gathers[l, grp], token = _exchange_start(srcs, False, token, f"gather_start_l{l}{grp}")

    def fetch(l, grp, after):
        srcs, lands = _exchange_wait(gathers[l, grp], after, f"gather_wait_l{l}{grp}")
        lands = _fill_own(lands, srcs, False, f"gather_own_l{l}{grp}")
        return {k: _unshard(k, lands[i]) for i, k in enumerate(GROUPS[grp])}

    scatters = {}

    def emit(l, grp, grads):
        chunks = [_to_chunks(k, grads[k]).astype(BF16 if k in BIG else F32) for k in GROUPS[grp]]
        scatters[l, grp], tok = _exchange_start(chunks, True, None, f"scatter_start_l{l}{grp}")
        return tok

    small = {k: w[k] for k in REPLICATED}
    loss, grad_x, g = _step_local(x[0], positions[0], loss_target[0], small, w_uq.shape[1], w_ukv.shape[1], fetch, emit, token)
    loss = lax.psum(loss[0, 0], AXES)

    slots = {}
    for l, grp in [(l, grp) for grp in ("c", "b", "a") for l in reversed(range(L))]:
        srcs, lands = _exchange_wait(scatters[l, grp], grad_x, f"scatter_wait_l{l}{grp}")
        lands = _fill_own(lands, srcs, True, f"scatter_own_l{l}{grp}")
        for i, k in enumerate(GROUPS[grp]):
            slots[k, l] = lands[i]
    res = {}
    for k in [k for grp in ("c", "b", "a") for k in GROUPS[grp]]:
        shp = w[k].shape
        r2 = (shp[0] * shp[1], shp[2])
        if k in BIG:
            slot_list = [slots[k, l] for l in range(L)]
        else:
            slot_list = [jnp.concatenate([slots[k, l] for l in range(L)], axis=1)]
        outs = _adamw(slot_list, w[k].reshape(r2), mom[k].reshape(r2), var[k].reshape(r2), "adamw_" + k)
        res[k] = [o.reshape(shp) for o in outs]

    packed, sizes = _pack([g[k][None] for k in REPLICATED])
    parts = _exchange([packed[0]], False, "gather_small_grads")[0]
    shapes = [w[k].shape for k in REPLICATED]
    pw, _ = _pack([w[k][None] for k in REPLICATED])
    pm, _ = _pack([mom[k][None] for k in REPLICATED])
    pv, _ = _pack([var[k][None] for k in REPLICATED])
    outs = _adamw([parts], pw[0], pm[0], pv[0], "adamw_small")
    unpacked = [_unpack(o, sizes, shapes) for o in outs]
    for i, k in enumerate(REPLICATED):
        res[k] = [unpacked[j][i] for j in range(4)]

    return (loss, grad_x[None], *[res[k][0] for k in WEIGHTS], *[res[k][1] for k in WEIGHTS],
            *[res[k][2] for k in WEIGHTS], *[res[k][3] for k in WEIGHTS])
```

```python
import functools
import math

import jax
import jax.numpy as jnp
from jax import lax
from jax.experimental import pallas as pl
from jax.experimental.pallas import tpu as pltpu

F32 = jnp.float32
BF16 = jnp.bfloat16

SWA_Q_HEADS = 16
SWA_KV_HEADS = 2
SWA_HEAD_DIM = 64
SWA_BLOCK = 128
MLA_HEADS = 16
MLA_NOPE = 128
MLA_ROPE = 64
MLA_V = 128
SGU_GROUPS = 8
SGU_DIM = 128
SGU_CHUNK = 128
ROPE_THETA = 10000.0
EPS = 1e-5
MASK = -1e30
ADAM_LR = 0.001
ADAM_B1 = 0.9
ADAM_B2 = 0.999
ADAM_EPS = 1e-08
ADAM_WD = 0.01
ADAM_STEP = 10

N_DEV = 8
AXES = ("x", "y", "c")
VMEM_LIMIT = 56 * 1024 * 1024
MLA_TILE = 512
ROW_TILE = 512
MAX_TK = 2816
SUBLANES = 8


def _tile(n, prefs):
    for p in prefs:
        if n % p == 0:
            return p
    return n


def _params(sem):
    return pltpu.CompilerParams(dimension_semantics=sem, vmem_limit_bytes=VMEM_LIMIT)


def _cols(tm, width, off):
    assert off % width == 0, (off, width)
    blk = off // width
    return pl.BlockSpec((tm, width), lambda i, *_: (i, blk))


def _full(shape):
    nd = len(shape)
    return pl.BlockSpec(shape, lambda *_: (0,) * nd)


def _sigmoid(v):
    return 1.0 / (1.0 + jnp.exp(-v))


def _gelu(v):
    return 0.5 * v * (1.0 + lax.erf(v * (2.0 ** -0.5)))


def _gelu_grad(v):
    return 0.5 * (1.0 + lax.erf(v * (2.0 ** -0.5))) + v * jnp.exp(-0.5 * v * v) * (1.0 / math.sqrt(2.0 * math.pi))


_DIMS = {"nn": (((1,), (0,)), ((), ())), "nt": (((1,), (1,)), ((), ())), "tn": (((0,), (0,)), ((), ()))}


def _mm(a, b, mode, outs, name, *, extras=(), epilogue=None, full_n=False, dep=None):
    if mode == "nn":
        (M, K), (K2, N) = a.shape, b.shape
    elif mode == "nt":
        (M, K), (N, K2) = a.shape, b.shape
    else:
        (K, M), (K2, N) = a.shape, b.shape
    assert K == K2, (a.shape, b.shape, mode)
    tm = _tile(M, (1024, 512, 256, 128))
    tn = N if full_n else _tile(N, (1024, 768, 512, 384, 256, 128))
    if full_n:
        tm = _tile(M, (512, 256, 128))
    max_tk = MAX_TK // 2 if full_n else MAX_TK
    tk = max(d for d in range(128, min(K, max_tk) + 1, 128) if K % d == 0) if K % 128 == 0 else K
    nk = K // tk
    if mode == "nn":
        a_spec = pl.BlockSpec((tm, tk), lambda i, j, k: (i, k))
        b_spec = pl.BlockSpec((tk, tn), lambda i, j, k: (k, j))
    elif mode == "nt":
        a_spec = pl.BlockSpec((tm, tk), lambda i, j, k: (i, k))
        b_spec = pl.BlockSpec((tn, tk), lambda i, j, k: (j, k))
    else:
        a_spec = pl.BlockSpec((tk, tm), lambda i, j, k: (k, i))
        b_spec = pl.BlockSpec((tk, tn), lambda i, j, k: (k, j))
    in_specs = [a_spec, b_spec]
    for arr, kind in extras:
        if kind == "tile":
            in_specs.append(pl.BlockSpec((tm, tn), lambda i, j, k: (i, j)))
        else:
            in_specs.append(pl.BlockSpec((1, tn), lambda i, j, k: (0, j)))
    out_specs, out_shape = [], []
    for dt, kind in outs:
        if kind == "n":
            out_specs.append(pl.BlockSpec((tm, tn), lambda i, j, k: (i, j)))
            out_shape.append(jax.ShapeDtypeStruct((M, N), dt))
        else:
            assert tn == N
            out_specs.append(pl.BlockSpec((tm, 1), lambda i, j, k: (i, 0)))
            out_shape.append(jax.ShapeDtypeStruct((M, 1), dt))
    ne, no = len(extras), len(outs)
    deps = []
    if dep is not None:
        in_specs.append(_full(dep.shape))
        deps = [dep]
    dims = _DIMS[mode]
    if epilogue is None:
        epilogue = lambda acc: (acc,) * no

    def body(*refs):
        a_ref, b_ref = refs[0], refs[1]
        ex = refs[2:2 + ne]
        out = refs[len(refs) - 1 - no:len(refs) - 1]
        acc = refs[-1]
        k = pl.program_id(2)
        part = lax.dot_general(a_ref[...].astype(BF16), b_ref[...].astype(BF16), dims, preferred_element_type=F32)

        def finish(total):
            res = epilogue(total, *[e[...] for e in ex])
            for o, r in zip(out, res):
                o[...] = r.astype(o.dtype)

        if nk == 1:
            finish(part)
            return

        @pl.when(k == 0)
        def _():
            acc[...] = part

        @pl.when((k > 0) & (k < nk - 1))
        def _():
            acc[...] += part

        @pl.when(k == nk - 1)
        def _():
            finish(acc[...] + part)

    res = pl.pallas_call(
        body, name=name, grid=(M // tm, N // tn, nk), in_specs=in_specs, out_specs=out_specs, out_shape=out_shape,
        scratch_shapes=[pltpu.VMEM((tm, tn), F32)],
        compiler_params=_params(("parallel", "parallel", "arbitrary")),
    )(a, b, *[e[0] for e in extras], *deps)
    return res


def _ln_epilogue(alpha):
    def epi(acc, x, g, b):
        r = alpha * x + acc
        mu = jnp.mean(r, axis=-1, keepdims=True)
        d = r - mu
        var = jnp.mean(d * d, axis=-1, keepdims=True)
        rstd = lax.rsqrt(var + EPS)
        xhat = d * rstd
        y = xhat * g + b
        return y, y, xhat, rstd
    return epi


def _mm_ln(a, w, x, g, b, alpha, name):
    return _mm(a, w, "nn", [(F32, "n"), (BF16, "n"), (F32, "n"), (F32, "1")], name,
               extras=[(x, "tile"), (g, "row"), (b, "row")], epilogue=_ln_epilogue(alpha), full_n=True)


def _mm_axpy(a, w, mode, r, alpha, name, dep=None):
    return _mm(a, w, mode, [(F32, "n")], name, extras=[(r, "tile")],
               epilogue=lambda acc, rv: (acc + alpha * rv,), dep=dep)[0]


def _ln_bwd(dy, xhat, rstd, g, name):
    S, D = dy.shape
    tm = _tile(S, (256, 128))

    def body(dy_ref, xh_ref, rs_ref, g_ref, dr_ref, dr16_ref, dg_ref, db_ref):
        @pl.when(pl.program_id(0) == 0)
        def _():
            dg_ref[...] = jnp.zeros_like(dg_ref)
            db_ref[...] = jnp.zeros_like(db_ref)

        dyv, xh = dy_ref[...], xh_ref[...]
        dxh = dyv * g_ref[...]
        m1 = jnp.mean(dxh, axis=-1, keepdims=True)
        m2 = jnp.mean(dxh * xh, axis=-1, keepdims=True)
        dr = rs_ref[...] * (dxh - m1 - xh * m2)
        dr_ref[...] = dr
        dr16_ref[...] = dr.astype(BF16)
        dg_ref[...] += jnp.sum(dyv * xh, axis=0, keepdims=True)
        db_ref[...] += jnp.sum(dyv, axis=0, keepdims=True)

    row = pl.BlockSpec((tm, D), lambda i: (i, 0))
    return pl.pallas_call(
        body, name=name, grid=(S // tm,),
        in_specs=[row, row, pl.BlockSpec((tm, 1), lambda i: (i, 0)), _full((1, D))],
        out_specs=[row, row, _full((1, D)), _full((1, D))],
        out_shape=[jax.ShapeDtypeStruct((S, D), F32), jax.ShapeDtypeStruct((S, D), BF16),
                   jax.ShapeDtypeStruct((1, D), F32), jax.ShapeDtypeStruct((1, D), F32)],
        compiler_params=_params(("arbitrary",)),
    )(dy, xhat, rstd, g)


def _rms_fwd(h, off, width, g, name):
    S = h.shape[0]
    tm = _tile(S, (ROW_TILE, 256, 128))

    def body(c_ref, g_ref, y_ref, r_ref):
        c = c_ref[...]
        r = lax.rsqrt(jnp.mean(c * c, axis=-1, keepdims=True) + EPS)
        y_ref[...] = (c * r * g_ref[...]).astype(BF16)
        r_ref[...] = r

    return pl.pallas_call(
        body, name=name, grid=(S // tm,),
        in_specs=[_cols(tm, width, off), _full((1, width))],
        out_specs=[pl.BlockSpec((tm, width), lambda i: (i, 0)), pl.BlockSpec((tm, 1), lambda i: (i, 0))],
        out_shape=[jax.ShapeDtypeStruct((S, width), BF16), jax.ShapeDtypeStruct((S, 1), F32)],
        compiler_params=_params(("parallel",)),
    )(h, g)


def _rms_bwd(dy, h, off, width, rstd, g, name):
    S = h.shape[0]
    tm = _tile(S, (ROW_TILE, 256, 128))

    def body(dy_ref, c_ref, r_ref, g_ref, dc_ref, dg_ref):
        @pl.when(pl.program_id(0) == 0)
        def _():
            dg_ref[...] = jnp.zeros_like(dg_ref)

        dyv, c, r = dy_ref[...], c_ref[...], r_ref[...]
        dyg = dyv * g_ref[...]
        m = jnp.mean(dyg * c, axis=-1, keepdims=True)
        dc_ref[...] = (r * dyg - c * (r * r * r) * m).astype(BF16)
        dg_ref[...] += jnp.sum(dyv * c * r, axis=0, keepdims=True)

    return pl.pallas_call(
        body, name=name, grid=(S // tm,),
        in_specs=[pl.BlockSpec((tm, width), lambda i: (i, 0)), _cols(tm, width, off),
                  pl.BlockSpec((tm, 1), lambda i: (i, 0)), _full((1, width))],
        out_specs=[pl.BlockSpec((tm, width), lambda i: (i, 0)), _full((1, width))],
        out_shape=[jax.ShapeDtypeStruct((S, width), BF16), jax.ShapeDtypeStruct((1, width), F32)],
        compiler_params=_params(("arbitrary",)),
    )(dy, h, rstd, g)


def _loss(y, target, name):
    S, D = y.shape
    tm = _tile(S, (256, 128))

    def body(y_ref, t_ref, l_ref, dy_ref):
        @pl.when(pl.program_id(0) == 0)
        def _():
            l_ref[...] = jnp.zeros_like(l_ref)

        err = y_ref[...] - t_ref[...]
        dy_ref[...] = err * (1.0 / D)
        per_tok = jnp.mean(err * err, axis=-1, keepdims=True)
        l_ref[...] += 0.5 * jnp.sum(per_tok, axis=0, keepdims=True)

    row = pl.BlockSpec((tm, D), lambda i: (i, 0))
    return pl.pallas_call(
        body, name=name, grid=(S // tm,), in_specs=[row, row], out_specs=[_full((1, 1)), row],
        out_shape=[jax.ShapeDtypeStruct((1, 1), F32), jax.ShapeDtypeStruct((S, D), F32)],
        compiler_params=_params(("arbitrary",)),
    )(y, target)


def _rope(a1, a2, cs, sn, n, bwd, name):
    S = cs.shape[0]
    tm = _tile(S, (ROW_TILE, 256, 128))
    stacked = not isinstance(a1, tuple)

    def body(a1_ref, a2_ref, c_ref, s_ref, y1_ref, y2_ref):
        if stacked:
            v1 = jnp.sum(a1_ref[...], axis=0)
            v2 = jnp.sum(a2_ref[...], axis=0)
        else:
            v1, v2 = a1_ref[...].astype(F32), a2_ref[...].astype(F32)
        c, s = c_ref[...], s_ref[...]
        if bwd:
            y1_ref[...] = (v1 * c + v2 * s).astype(BF16)
            y2_ref[...] = (v2 * c - v1 * s).astype(BF16)
        else:
            y1_ref[...] = (v1 * c - v2 * s).astype(BF16)
            y2_ref[...] = (v2 * c + v1 * s).astype(BF16)

    row = pl.BlockSpec((tm, n), lambda i: (i, 0))
    if stacked:
        H = a1.shape[0]
        a_specs = [pl.BlockSpec((H, tm, n), lambda i: (0, i, 0))] * 2
        arrs = [a1, a2]
    else:
        a_specs = [_cols(tm, n, a1[1]), _cols(tm, n, a2[1])]
        arrs = [a1[0], a2[0]]
    return pl.pallas_call(
        body, name=name, grid=(S // tm,), in_specs=a_specs + [row, row], out_specs=[row, row],
        out_shape=[jax.ShapeDtypeStruct((S, n), BF16)] * 2,
        compiler_params=_params(("parallel",)),
    )(*arrs, cs, sn)


def _swa_mask(n):
    blk = SWA_BLOCK
    row = lax.broadcasted_iota(jnp.int32, (blk, 2 * blk), 0)
    col = lax.broadcasted_iota(jnp.int32, (blk, 2 * blk), 1)
    rel = row + blk - col
    return (rel >= 0) & (rel < blk) & ((n > 0) | (col >= blk))


def _swa_specs(off_q, off_k, off_v):
    blk, aq, akv = SWA_BLOCK, SWA_Q_HEADS * SWA_HEAD_DIM, SWA_KV_HEADS * SWA_HEAD_DIM
    assert off_q % aq == 0 and off_k % akv == 0 and off_v % akv == 0
    prev = lambda off: pl.BlockSpec((blk, akv), lambda n: (jnp.maximum(n - 1, 0), off // akv))
    cur = lambda off: pl.BlockSpec((blk, akv), lambda n: (n, off // akv))
    return [pl.BlockSpec(memory_space=pltpu.SMEM), _cols(blk, aq, off_q), prev(off_k), cur(off_k), prev(off_v), cur(off_v)]


def _swa_fwd(h, off_q, off_k, off_v, sinks, name):
    S = h.shape[0]
    blk, hd, nh = SWA_BLOCK, SWA_HEAD_DIM, SWA_Q_HEADS
    grp = nh // SWA_KV_HEADS
    aq = nh * hd
    scale = hd ** -0.5

    def body(sink_ref, q_ref, kp_ref, kc_ref, vp_ref, vc_ref, o_ref, lse_ref):
        valid = _swa_mask(pl.program_id(0))
        q = q_ref[...].astype(BF16)
        k2 = jnp.concatenate([kp_ref[...], kc_ref[...]], axis=0).astype(BF16)
        v2 = jnp.concatenate([vp_ref[...], vc_ref[...]], axis=0).astype(BF16)
        for hh in range(nh):
            kv = hh // grp
            qh = q[:, hh * hd:(hh + 1) * hd]
            kh = k2[:, kv * hd:(kv + 1) * hd]
            vh = v2[:, kv * hd:(kv + 1) * hd]
            s = lax.dot_general(qh, kh, _DIMS["nt"], preferred_element_type=F32) * scale
            s = jnp.where(valid, s, MASK)
            sk = sink_ref[hh]
            m = jnp.maximum(jnp.max(s, axis=1, keepdims=True), sk)
            p = jnp.exp(s - m)
            l = jnp.sum(p, axis=1, keepdims=True) + jnp.exp(sk - m)
            o_ref[:, hh * hd:(hh + 1) * hd] = jnp.dot((p / l).astype(BF16), vh, preferred_element_type=F32)
            lse_ref[:, hh:hh + 1] = m + jnp.log(l)

    return pl.pallas_call(
        body, name=name, grid=(S // blk,), in_specs=_swa_specs(off_q, off_k, off_v),
        out_specs=[pl.BlockSpec((blk, aq), lambda n: (n, 0)), pl.BlockSpec((blk, nh), lambda n: (n, 0))],
        out_shape=[jax.ShapeDtypeStruct((S, aq), F32), jax.ShapeDtypeStruct((S, nh), F32)],
        compiler_params=_params(("parallel",)),
    )(sinks, h, h, h, h, h)


def _swa_bwd(h, off_q, off_k, off_v, sinks, dout, lse, name):
    S = h.shape[0]
    blk, hd, nh, nkv = SWA_BLOCK, SWA_HEAD_DIM, SWA_Q_HEADS, SWA_KV_HEADS
    grp = nh // nkv
    aq, akv = nh * hd, nkv * hd
    scale = hd ** -0.5

    def body(sink_ref, q_ref, kp_ref, kc_ref, vp_ref, vc_ref, do_ref, lse_ref, dq_ref, dk_ref, dv_ref, ds_ref):
        n = pl.program_id(0)

        @pl.when(n == 0)
        def _():
            dk_ref[...] = jnp.zeros_like(dk_ref)
            dv_ref[...] = jnp.zeros_like(dv_ref)
            ds_ref[...] = jnp.zeros_like(ds_ref)

        valid = _swa_mask(n)
        q = q_ref[...].astype(BF16)
        k2 = jnp.concatenate([kp_ref[...], kc_ref[...]], axis=0).astype(BF16)
        v2 = jnp.concatenate([vp_ref[...], vc_ref[...]], axis=0).astype(BF16)
        do = do_ref[...]
        lane = lax.broadcasted_iota(jnp.int32, (1, 128), 1)
        dsink = jnp.zeros((1, 128), F32)
        cur = pl.ds(pl.multiple_of(n * blk, blk), blk)
        prev = pl.ds(pl.multiple_of(jnp.maximum(n - 1, 0) * blk, blk), blk)
        for kv in range(nkv):
            kh = k2[:, kv * hd:(kv + 1) * hd]
            vh = v2[:, kv * hd:(kv + 1) * hd]
            dk_acc = jnp.zeros((2 * blk, hd), F32)
            dv_acc = jnp.zeros((2 * blk, hd), F32)
            for g in range(grp):
                hh = kv * grp + g
                qh = q[:, hh * hd:(hh + 1) * hd]
                doh = do[:, hh * hd:(hh + 1) * hd]
                doh16 = doh.astype(BF16)
                lse_h = lse_ref[:, hh:hh + 1]
                s = lax.dot_general(qh, kh, _DIMS["nt"], preferred_element_type=F32) * scale
                s = jnp.where(valid, s, MASK)
                p = jnp.exp(s - lse_h)
                p16 = p.astype(BF16)
                o = jnp.dot(p16, vh, preferred_element_type=F32)
                delta = jnp.sum(doh * o, axis=1, keepdims=True)
                dp = lax.dot_general(doh16, vh, _DIMS["nt"], preferred_element_type=F32)
                ds16 = (p * (dp - delta) * scale).astype(BF16)
                dq_ref[:, hh * hd:(hh + 1) * hd] = jnp.dot(ds16, kh, preferred_element_type=F32).astype(BF16)
                dk_acc += lax.dot_general(ds16, qh, _DIMS["tn"], preferred_element_type=F32)
                dv_acc += lax.dot_general(p16, doh16, _DIMS["tn"], preferred_element_type=F32)
                dsk = -jnp.sum(jnp.exp(sink_ref[hh] - lse_h) * delta, axis=0, keepdims=True)
                dsink += jnp.where(lane == hh, dsk, 0.0)
            cols = slice(kv * hd, (kv + 1) * hd)
            dk_ref[cur, cols] += dk_acc[blk:]
            dv_ref[cur, cols] += dv_acc[blk:]

            @pl.when(n > 0)
            def _():
                dk_ref[prev, cols] += dk_acc[:blk]
                dv_ref[prev, cols] += dv_acc[:blk]

        ds_ref[...] += dsink

    return pl.pallas_call(
        body, name=name, grid=(S // blk,),
        in_specs=_swa_specs(off_q, off_k, off_v) + [pl.BlockSpec((blk, aq), lambda n: (n, 0)),
                                                    pl.BlockSpec((blk, nh), lambda n: (n, 0))],
        out_specs=[pl.BlockSpec((blk, aq), lambda n: (n, 0)), _full((S, akv)), _full((S, akv)), _full((1, 128))],
        out_shape=[jax.ShapeDtypeStruct((S, aq), BF16), jax.ShapeDtypeStruct((S, akv), F32),
                   jax.ShapeDtypeStruct((S, akv), F32), jax.ShapeDtypeStruct((1, 128), F32)],
        compiler_params=_params(("arbitrary",)),
    )(sinks, h, h, h, h, h, dout, lse)


def _causal(i, j, t):
    row = i * t + lax.broadcasted_iota(jnp.int32, (t, t), 0)
    col = j * t + lax.broadcasted_iota(jnp.int32, (t, t), 1)
    return col <= row


def _mla_fwd(q, kt, v, name):
    H, S, dq = q.shape
    dv = v.shape[2]
    t = _tile(S, (MLA_TILE, 256, 128))
    nq = S // t
    hp = 2 if H % 2 == 0 else 1
    scale = (MLA_NOPE + MLA_ROPE) ** -0.5

    def body(q_ref, kt_ref, v_ref, o_ref, lse_ref, *state):
        i = pl.program_id(1)
        for hh in range(hp):
            m_s, l_s, acc_s = state[3 * hh:3 * hh + 3]
            m_s[...] = jnp.full_like(m_s, -jnp.inf)
            l_s[...] = jnp.zeros_like(l_s)
            acc_s[...] = jnp.zeros_like(acc_s)

        def block(j, masked):
            rows = pl.ds(pl.multiple_of(j * t, t), t)
            for hh in range(hp):
                m_s, l_s, acc_s = state[3 * hh:3 * hh + 3]
                s = jnp.dot(q_ref[hh], kt_ref[hh, :, rows], preferred_element_type=F32) * scale
                if masked:
                    s = jnp.where(_causal(0, 0, t), s, MASK)
                m_old = m_s[...]
                m_new = jnp.maximum(m_old, jnp.max(s, axis=1, keepdims=True))
                alpha = jnp.exp(m_old - m_new)
                p = jnp.exp(s - m_new)
                l_s[...] = alpha * l_s[...] + jnp.sum(p, axis=1, keepdims=True)
                acc_s[...] = alpha * acc_s[...] + jnp.dot(p.astype(BF16), v_ref[hh, rows, :], preferred_element_type=F32)
                m_s[...] = m_new

        def full_block(j, carry):
            block(j, False)
            return carry

        lax.fori_loop(0, i, full_block, 0)
        block(i, True)
        for hh in range(hp):
            m_s, l_s, acc_s = state[3 * hh:3 * hh + 3]
            o_ref[hh] = acc_s[...] / l_s[...]
            lse_ref[hh] = m_s[...] + jnp.log(l_s[...])

    tile = lambda d: pl.BlockSpec((hp, t, d), lambda h, i: (h, i, 0))
    return pl.pallas_call(
        body, name=name, grid=(H // hp, nq),
        in_specs=[tile(dq), pl.BlockSpec((hp, dq, S), lambda h, i: (h, 0, 0)), pl.BlockSpec((hp, S, dv), lambda h, i: (h, 0, 0))],
        out_specs=[tile(dv), tile(1)],
        out_shape=[jax.ShapeDtypeStruct((H, S, dv), F32), jax.ShapeDtypeStruct((H, S, 1), F32)],
        scratch_shapes=[pltpu.VMEM((t, 1), F32), pltpu.VMEM((t, 1), F32), pltpu.VMEM((t, dv), F32)] * hp,
        compiler_params=_params(("parallel", "arbitrary")),
    )(q, kt, v)


def _rowdot(a, b, name):
    H, S, d = a.shape
    t = _tile(S, (ROW_TILE, 256, 128))

    def body(a_ref, b_ref, o_ref):
        o_ref[...] = jnp.sum(a_ref[...] * b_ref[...], axis=-1, keepdims=True)

    spec = pl.BlockSpec((1, t, d), lambda h, i: (h, i, 0))
    return pl.pallas_call(
        body, name=name, grid=(H, S // t), in_specs=[spec, spec],
        out_specs=pl.BlockSpec((1, t, 1), lambda h, i: (h, i, 0)),
        out_shape=jax.ShapeDtypeStruct((H, S, 1), F32),
        compiler_params=_params(("parallel", "parallel")),
    )(a, b)


def _mla_bwd(q, k, v, do, lse, delta, name):
    H, S, dq = q.shape
    dv = v.shape[2]
    t = _tile(S, (MLA_TILE, 256, 128))
    nq = S // t
    hp = 2 if H % 2 == 0 else 1
    scale = (MLA_NOPE + MLA_ROPE) ** -0.5

    def body(q_ref, k_ref, v_ref, do_ref, lse_ref, dl_ref, dq_ref, dk_ref, dv_ref, dk_s, dv_s):
        j, i = pl.program_id(1), pl.program_id(2)

        @pl.when((j == 0) & (i == 0))
        def _():
            dq_ref[...] = jnp.zeros_like(dq_ref)

        @pl.when(i == j)
        def _():
            dk_s[...] = jnp.zeros_like(dk_s)
            dv_s[...] = jnp.zeros_like(dv_s)

        def block(masked):
            rows = pl.ds(pl.multiple_of(i * t, t), t)
            for hh in range(hp):
                qv, kv_, vv, dov = q_ref[hh], k_ref[hh], v_ref[hh], do_ref[hh]
                s = lax.dot_general(qv, kv_, _DIMS["nt"], preferred_element_type=F32) * scale
                if masked:
                    s = jnp.where(_causal(0, 0, t), s, MASK)
                p = jnp.exp(s - lse_ref[hh])
                p16 = p.astype(BF16)
                dp = lax.dot_general(dov, vv, _DIMS["nt"], preferred_element_type=F32)
                ds16 = (p * (dp - dl_ref[hh]) * scale).astype(BF16)
                dv_s[hh] += lax.dot_general(p16, dov, _DIMS["tn"], preferred_element_type=F32)
                dk_s[hh] += lax.dot_general(ds16, qv, _DIMS["tn"], preferred_element_type=F32)
                dq_ref[hh, rows, :] += jnp.dot(ds16, kv_, preferred_element_type=F32)

        @pl.when(i == j)
        def _():
            block(True)

        @pl.when(i > j)
        def _():
            block(False)

        @pl.when(i == nq - 1)
        def _():
            dk_ref[...] = dk_s[...]
            dv_ref[...] = dv_s[...]

    q_map = lambda h, j, i: (h, jnp.maximum(i, j), 0)
    kv_map = lambda h, j, i: (h, j, 0)
    return pl.pallas_call(
        body, name=name, grid=(H // hp, nq, nq),
        in_specs=[pl.BlockSpec((hp, t, dq), q_map), pl.BlockSpec((hp, t, dq), kv_map), pl.BlockSpec((hp, t, dv), kv_map),
                  pl.BlockSpec((hp, t, dv), q_map), pl.BlockSpec((hp, t, 1), q_map), pl.BlockSpec((hp, t, 1), q_map)],
        out_specs=[pl.BlockSpec((hp, S, dq), lambda h, j, i: (h, 0, 0)), pl.BlockSpec((hp, t, dq), kv_map),
                   pl.BlockSpec((hp, t, dv), kv_map)],
        out_shape=[jax.ShapeDtypeStruct((H, S, dq), F32), jax.ShapeDtypeStruct((H, S, dq), F32),
                   jax.ShapeDtypeStruct((H, S, dv), F32)],
        scratch_shapes=[pltpu.VMEM((hp, t, dq), F32), pltpu.VMEM((hp, t, dv), F32)],
        compiler_params=_params(("parallel", "arbitrary", "arbitrary")),
    )(q, k, v, do, lse, delta)


def _sgu_norm(hv, lg, lb):
    vg = _gelu(hv)
    mu = jnp.mean(vg, axis=-1, keepdims=True)
    d = vg - mu
    rstd = lax.rsqrt(jnp.mean(d * d, axis=-1, keepdims=True) + EPS)
    xhat = d * rstd
    return xhat, rstd, xhat * lg + lb


def _sgu_fwd(h, off_u, off_v, lg, lb, w16, bt, name):
    S = h.shape[0]
    T, G, C = SGU_CHUNK, SGU_GROUPS, SGU_DIM
    W = G * C

    def body(hu_ref, hv_ref, lg_ref, lb_ref, w_ref, bt_ref, y_ref):
        u = _gelu(hu_ref[...])
        _, _, vn = _sgu_norm(hv_ref[...], lg_ref[...], lb_ref[...])
        vn16 = vn.astype(BF16)
        for g in range(G):
            cols = slice(g * C, (g + 1) * C)
            mixed = jnp.dot(w_ref[g], vn16[:, cols], preferred_element_type=F32) + bt_ref[:, g:g + 1]
            y_ref[:, cols] = (u[:, cols] * mixed).astype(BF16)

    return pl.pallas_call(
        body, name=name, grid=(S // T,),
        in_specs=[_cols(T, W, off_u), _cols(T, W, off_v), _full((1, W)), _full((1, W)), _full((G, T, T)), _full((T, G))],
        out_specs=pl.BlockSpec((T, W), lambda n: (n, 0)),
        out_shape=jax.ShapeDtypeStruct((S, W), BF16),
        compiler_params=_params(("parallel",)),
    )(h, h, lg, lb, w16, bt)


def _sgu_bwd(h, off_u, off_v, lg, lb, w16, bt, dy, name):
    S = h.shape[0]
    T, G, C = SGU_CHUNK, SGU_GROUPS, SGU_DIM
    W = G * C
    nc = S // T

    def body(hu_ref, hv_ref, lg_ref, lb_ref, w_ref, bt_ref, dy_ref, dhu_ref, dhv_ref, dw_ref, db_ref, dlg_ref, dlb_ref,
             dmix_s, dvn_s):
        n = pl.program_id(0)

        @pl.when(n == 0)
        def _():
            dw_ref[...] = jnp.zeros_like(dw_ref)
            dlg_ref[...] = jnp.zeros_like(dlg_ref)
            dlb_ref[...] = jnp.zeros_like(dlb_ref)
            dmix_s[...] = jnp.zeros_like(dmix_s)

        hu, hv, lgv = hu_ref[...], hv_ref[...], lg_ref[...]
        u = _gelu(hu)
        xhat, rstd, vn = _sgu_norm(hv, lgv, lb_ref[...])
        vn16 = vn.astype(BF16)
        dyv = dy_ref[...]
        dmixed = dyv * u
        dmix_s[...] += dmixed
        dmixed16 = dmixed.astype(BF16)
        for g in range(G):
            cols = slice(g * C, (g + 1) * C)
            mixed = jnp.dot(w_ref[g], vn16[:, cols], preferred_element_type=F32) + bt_ref[:, g:g + 1]
            dhu_ref[:, cols] = (dyv[:, cols] * mixed * _gelu_grad(hu[:, cols])).astype(BF16)
            dvn_s[:, cols] = lax.dot_general(w_ref[g], dmixed16[:, cols], _DIMS["tn"], preferred_element_type=F32)
            dw_ref[g] += lax.dot_general(dmixed16[:, cols], vn16[:, cols], _DIMS["nt"], preferred_element_type=F32)
        dvn = dvn_s[...]
        dlg_ref[...] += jnp.sum(dvn * xhat, axis=0, keepdims=True)
        dlb_ref[...] += jnp.sum(dvn, axis=0, keepdims=True)
        dxh = dvn * lgv
        m1 = jnp.mean(dxh, axis=-1, keepdims=True)
        m2 = jnp.mean(dxh * xhat, axis=-1, keepdims=True)
        dvg = rstd * (dxh - m1 - xhat * m2)
        dhv_ref[...] = (dvg * _gelu_grad(hv)).astype(BF16)

        @pl.when(n == nc - 1)
        def _():
            tril = lax.broadcasted_iota(jnp.int32, (T, T), 1) <= lax.broadcasted_iota(jnp.int32, (T, T), 0)
            lane = lax.broadcasted_iota(jnp.int32, (T, 128), 1)
            db = jnp.zeros((T, 128), F32)
            for g in range(G):
                dw_ref[g] = jnp.where(tril, dw_ref[g], 0.0)
                db += jnp.where(lane == g, jnp.sum(dmix_s[:, g * C:(g + 1) * C], axis=1, keepdims=True), 0.0)
            db_ref[...] = db

    row = pl.BlockSpec((T, W), lambda n: (n, 0))
    return pl.pallas_call(
        body, name=name, grid=(nc,),
        in_specs=[_cols(T, W, off_u), _cols(T, W, off_v), _full((1, W)), _full((1, W)), _full((G, T, T)), _full((T, G)), row],
        out_specs=[row, row, _full((G, T, T)), _full((T, 128)), _full((1, W)), _full((1, W))],
        out_shape=[jax.ShapeDtypeStruct((S, W), BF16), jax.ShapeDtypeStruct((S, W), BF16),
                   jax.ShapeDtypeStruct((G, T, T), F32), jax.ShapeDtypeStruct((T, 128), F32),
                   jax.ShapeDtypeStruct((1, W), F32), jax.ShapeDtypeStruct((1, W), F32)],
        scratch_shapes=[pltpu.VMEM((T, W), F32), pltpu.VMEM((T, W), F32)],
        compiler_params=_params(("arbitrary",)),
    )(h, h, lg, lb, w16, bt, dy)


def _merge_fwd(ys, ps, h, bg, name):
    S = h.shape[0]
    D = ps[0].shape[1]
    tm = _tile(S, (512, 256, 128))
    tn = _tile(D, (512, 256, 128))
    nb = len(ys)

    def body(*refs):
        y_refs, p_refs, l_refs = refs[:nb], refs[nb:2 * nb], refs[2 * nb:3 * nb]
        bg_ref, mg_ref, z_ref = refs[3 * nb:]
        acc = jnp.zeros((tm, tn), F32)
        for b in range(nb):
            z = jnp.dot(y_refs[b][...].astype(BF16), p_refs[b][...], preferred_element_type=F32)
            z_ref[b] = z
            acc += _sigmoid(l_refs[b][...] + bg_ref[b:b + 1, :]) * z
        mg_ref[...] = acc.astype(BF16)

    in_specs = [pl.BlockSpec((tm, y.shape[1]), lambda i, j: (i, 0)) for y in ys]
    in_specs += [pl.BlockSpec((p.shape[0], tn), lambda i, j: (0, j)) for p in ps]
    in_specs += [pl.BlockSpec((tm, tn), functools.partial(lambda i, j, b: (i, b * (D // tn) + j), b=b)) for b in range(nb)]
    in_specs += [pl.BlockSpec((nb, tn), lambda i, j: (0, j))]
    return pl.pallas_call(
        body, name=name, grid=(S // tm, D // tn), in_specs=in_specs,
        out_specs=[pl.BlockSpec((tm, tn), lambda i, j: (i, j)), pl.BlockSpec((nb, tm, tn), lambda i, j: (0, i, j))],
        out_shape=[jax.ShapeDtypeStruct((S, D), BF16), jax.ShapeDtypeStruct((nb, S, D), F32)],
        compiler_params=_params(("parallel", "parallel")),
    )(*ys, *ps, *([h] * nb), bg)


def _merge_bwd(dm, z, h, bg, name):
    nb, S, D = z.shape
    tm = _tile(S, (256, 128))
    tn = _tile(D, (512, 256, 128))

    def body(*refs):
        dm_ref, z_ref = refs[0], refs[1]
        l_refs = refs[2:2 + nb]
        bg_ref, dz_ref, dl_ref, dbg_ref = refs[2 + nb:]

        @pl.when(pl.program_id(1) == 0)
        def _():
            dbg_ref[...] = jnp.zeros_like(dbg_ref)

        dmv = dm_ref[...]
        rows = lax.broadcasted_iota(jnp.int32, (SUBLANES, tn), 0)
        dbg = jnp.zeros((SUBLANES, tn), F32)
        for b in range(nb):
            gt = _sigmoid(l_refs[b][...] + bg_ref[b:b + 1, :])
            dz_ref[b] = (dmv * gt).astype(BF16)
            dl = dmv * z_ref[b] * gt * (1.0 - gt)
            dl_ref[b] = dl.astype(BF16)
            dbg += jnp.where(rows == b, jnp.sum(dl, axis=0, keepdims=True), 0.0)
        dbg_ref[...] += dbg

    in_specs = [pl.BlockSpec((tm, tn), lambda j, i: (i, j)), pl.BlockSpec((nb, tm, tn), lambda j, i: (0, i, j))]
    in_specs += [pl.BlockSpec((tm, tn), functools.partial(lambda j, i, b: (i, b * (D // tn) + j), b=b)) for b in range(nb)]
    in_specs += [pl.BlockSpec((nb, tn), lambda j, i: (0, j))]
    blk3 = pl.BlockSpec((nb, tm, tn), lambda j, i: (0, i, j))
    return pl.pallas_call(
        body, name=name, grid=(D // tn, S // tm), in_specs=in_specs,
        out_specs=[blk3, blk3, pl.BlockSpec((SUBLANES, tn), lambda j, i: (0, j))],
        out_shape=[jax.ShapeDtypeStruct((nb, S, D), BF16), jax.ShapeDtypeStruct((nb, S, D), BF16),
                   jax.ShapeDtypeStruct((SUBLANES, D), F32)],
        compiler_params=_params(("parallel", "arbitrary")),
    )(dm, z, *([h] * nb), bg)


def _shift_down(x, halo, k):
    xr = pltpu.roll(x, k, 0)
    hr = pltpu.roll(halo, k, 0)
    rows = lax.broadcasted_iota(jnp.int32, halo.shape, 0)
    top = jnp.where(rows < k, hr, xr[:SUBLANES])
    return jnp.concatenate([top, xr[SUBLANES:]], axis=0)


def _shift_up(x, halo, k):
    tm = x.shape[0]
    xr = pltpu.roll(x, tm - k, 0)
    hr = pltpu.roll(halo, SUBLANES - k, 0)
    rows = lax.broadcasted_iota(jnp.int32, halo.shape, 0)
    bot = jnp.where(rows >= SUBLANES - k, hr, xr[tm - SUBLANES:])
    return jnp.concatenate([xr[:tm - SUBLANES], bot], axis=0)


def _conv_tiles(S, F):
    return _tile(S, (ROW_TILE, 256, 128)), _tile(F, (512, 256, 128))


def _conv_in_specs(tm, tn, F):
    r8 = tm // SUBLANES
    nf = F // tn
    specs = []
    for half in range(2):
        specs.append(pl.BlockSpec((tm, tn), functools.partial(lambda j, i, o: (i, o + j), o=half * nf)))
        specs.append(pl.BlockSpec((SUBLANES, tn), functools.partial(lambda j, i, o: (jnp.maximum(i * r8 - 1, 0), o + j), o=half * nf)))
    for half in range(2):
        specs.append(pl.BlockSpec((3, tn), functools.partial(lambda j, i, o: (0, o + j), o=half * nf)))
        specs.append(pl.BlockSpec((1, tn), functools.partial(lambda j, i, o: (0, o + j), o=half * nf)))
    return specs


def _conv_apply(x, halo, w, b, first):
    halo = jnp.where(first, 0.0, halo)
    x1 = _shift_down(x, halo, 1)
    x2 = _shift_down(x, halo, 2)
    return b + x2 * w[0:1, :] + x1 * w[1:2, :] + x * w[2:3, :], x1, x2


def _glu_fwd(up, cw, cb, name):
    S, F2 = up.shape
    F = F2 // 2
    tm, tn = _conv_tiles(S, F)

    def body(ug, hg, uv, hv, wg, bgr, wv, bvr, a_ref):
        first = pl.program_id(1) == 0
        cg, _, _ = _conv_apply(ug[...], hg[...], wg[...], bgr[...], first)
        cv, _, _ = _conv_apply(uv[...], hv[...], wv[...], bvr[...], first)
        a_ref[...] = (cg * _sigmoid(cg) * cv).astype(BF16)

    return pl.pallas_call(
        body, name=name, grid=(F // tn, S // tm), in_specs=_conv_in_specs(tm, tn, F),
        out_specs=pl.BlockSpec((tm, tn), lambda j, i: (i, j)),
        out_shape=jax.ShapeDtypeStruct((S, F), BF16),
        compiler_params=_params(("parallel", "parallel")),
    )(up, up, up, up, cw, cb, cw, cb)


def _glu_bwd(up, cw, cb, da, name):
    S, F2 = up.shape
    F = F2 // 2
    tm, tn = _conv_tiles(S, F)

    def body(ug, hg, uv, hv, wg, bgr, wv, bvr, da_ref, dg_ref, dv_ref, sg_ref, sv_ref):
        i = pl.program_id(1)

        @pl.when(i == 0)
        def _():
            sg_ref[...] = jnp.zeros_like(sg_ref)
            sv_ref[...] = jnp.zeros_like(sv_ref)

        first = i == 0
        xg, xv = ug[...], uv[...]
        cg, xg1, xg2 = _conv_apply(xg, hg[...], wg[...], bgr[...], first)
        cv, xv1, xv2 = _conv_apply(xv, hv[...], wv[...], bvr[...], first)
        dav = da_ref[...]
        sg = _sigmoid(cg)
        dcv = dav * cg * sg
        dcg = dav * cv * sg * (1.0 + cg * (1.0 - sg))
        dg_ref[...] = dcg
        dv_ref[...] = dcv
        rows = lax.broadcasted_iota(jnp.int32, (SUBLANES, tn), 0)

        def stats(dc, x, x1, x2):
            acc = jnp.zeros((SUBLANES, tn), F32)
            for r, val in enumerate((dc * x2, dc * x1, dc * x, dc)):
                acc += jnp.where(rows == r, jnp.sum(val, axis=0, keepdims=True), 0.0)
            return acc

        sg_ref[...] += stats(dcg, xg, xg1, xg2)
        sv_ref[...] += stats(dcv, xv, xv1, xv2)

    tile = pl.BlockSpec((tm, tn), lambda j, i: (i, j))
    stat = pl.BlockSpec((SUBLANES, tn), lambda j, i: (0, j))
    return pl.pallas_call(
        body, name=name, grid=(F // tn, S // tm), in_specs=_conv_in_specs(tm, tn, F) + [tile],
        out_specs=[tile, tile, stat, stat],
        out_shape=[jax.ShapeDtypeStruct((S, F), F32), jax.ShapeDtypeStruct((S, F), F32),
                   jax.ShapeDtypeStruct((SUBLANES, F), F32), jax.ShapeDtypeStruct((SUBLANES, F), F32)],
        compiler_params=_params(("parallel", "arbitrary")),
    )(up, up, up, up, cw, cb, cw, cb, da)


def _conv_bwd(dc, w, name):
    S, F = dc.shape
    tm, tn = _conv_tiles(S, F)
    r8 = tm // SUBLANES
    ni = S // tm

    def body(x_ref, h_ref, w_ref, o_ref):
        x = x_ref[...]
        halo = jnp.where(pl.program_id(1) == ni - 1, 0.0, h_ref[...])
        wv = w_ref[...]
        o_ref[...] = (x * wv[2:3, :] + _shift_up(x, halo, 1) * wv[1:2, :] + _shift_up(x, halo, 2) * wv[0:1, :]).astype(BF16)

    return pl.pallas_call(
        body, name=name, grid=(F // tn, ni),
        in_specs=[pl.BlockSpec((tm, tn), lambda j, i: (i, j)),
                  pl.BlockSpec((SUBLANES, tn), lambda j, i: (jnp.minimum((i + 1) * r8, S // SUBLANES - 1), j)),
                  pl.BlockSpec((3, tn), lambda j, i: (0, j))],
        out_specs=pl.BlockSpec((tm, tn), lambda j, i: (i, j)),
        out_shape=jax.ShapeDtypeStruct((S, F), BF16),
        compiler_params=_params(("parallel", "parallel")),
    )(dc, dc, w)


def _adamw(slot_list, own_list, me, w, m, v, name):
    L = len(slot_list)
    P, K, C = slot_list[0].shape
    tr = _tile(K, (256, 128, 64, 32, 16))
    while tr * C * 4 > (1 << 20) and tr % 32 == 0:
        tr //= 2
    nb = K // tr
    has_own = own_list is not None

    def body(me_ref, *refs):
        s_refs = refs[:L]
        o_refs = refs[L:2 * L] if has_own else None
        w_ref, m_ref, v_ref, g_ref, d_ref, nm_ref, nv_ref = refs[L * (1 + has_own):]
        layer = pl.program_id(0)
        g = None
        for l in range(L):
            gl = None
            for p in range(P):
                term = s_refs[l][p].astype(F32)
                if has_own:
                    term = jnp.where(me_ref[0] == p, o_refs[l][0].astype(F32), term)
                gl = term if gl is None else gl + term
            g = gl if g is None else jnp.where(layer == l, gl, g)
        nm = ADAM_B1 * m_ref[...] + (1.0 - ADAM_B1) * g
        nv = ADAM_B2 * v_ref[...] + (1.0 - ADAM_B2) * (g * g)
        m_hat = nm / (1.0 - ADAM_B1 ** ADAM_STEP)
        v_hat = nv / (1.0 - ADAM_B2 ** ADAM_STEP)
        g_ref[...] = g
        d_ref[...] = -ADAM_LR * (m_hat / (jnp.sqrt(v_hat) + ADAM_EPS) + ADAM_WD * w_ref[...])
        nm_ref[...] = nm
        nv_ref[...] = nv

    blk = pl.BlockSpec((tr, C), lambda li, i, me_ref: (li * nb + i, 0))
    specs = [pl.BlockSpec((P, tr, C), functools.partial(lambda li, i, me_ref, l: (0, jnp.where(li == l, i, 0), 0), l=l))
             for l in range(L)]
    if has_own:
        specs += [pl.BlockSpec((1, tr, C), functools.partial(lambda li, i, me_ref, l: (me_ref[0], jnp.where(li == l, i, 0), 0), l=l))
                  for l in range(L)]
    return pl.pallas_call(
        body, name=name,
        grid_spec=pltpu.PrefetchScalarGridSpec(num_scalar_prefetch=1, grid=(L, nb), in_specs=specs + [blk, blk, blk],
                                               out_specs=[blk] * 4),
        out_shape=[jax.ShapeDtypeStruct((L * K, C), F32)] * 4,
        compiler_params=_params(("arbitrary", "arbitrary")),
    )(me, *slot_list, *(own_list if has_own else []), w, m, v)


def _exchange(srcs, scatter, name):
    na = len(srcs)
    out_shape = [jax.ShapeDtypeStruct(s.shape if scatter else (N_DEV,) + s.shape, s.dtype) for s in srcs]

    def body(*refs):
        src_refs, out_refs = refs[:na], refs[na:2 * na]
        send_sems, recv_sems, local_sems = refs[2 * na:]
        x, y, c = lax.axis_index("x"), lax.axis_index("y"), lax.axis_index("c")
        me = 4 * x + 2 * y + c

        def flip(v, bit):
            return 1 - v if bit else v

        def peer(k):
            return (flip(x, (k >> 2) & 1), flip(y, (k >> 1) & 1), flip(c, k & 1))

        def peer_index(k):
            px, py, pc = peer(k)
            return 4 * px + 2 * py + pc

        def remote(a, k):
            src = src_refs[a].at[peer_index(k)] if scatter else src_refs[a]
            return pltpu.make_async_remote_copy(
                src_ref=src, dst_ref=out_refs[a].at[me], send_sem=send_sems.at[a, k - 1], recv_sem=recv_sems.at[a, k - 1],
                device_id=peer(k), device_id_type=pl.DeviceIdType.MESH)

        def arrival(a, k):
            src = src_refs[a].at[me] if scatter else src_refs[a]
            return pltpu.make_async_remote_copy(
                src_ref=src, dst_ref=out_refs[a].at[peer_index(k)], send_sem=send_sems.at[a, k - 1],
                recv_sem=recv_sems.at[a, k - 1], device_id=peer(k), device_id_type=pl.DeviceIdType.MESH)

        own = [pltpu.make_async_copy(src_refs[a].at[me] if scatter else src_refs[a], out_refs[a].at[me], local_sems.at[a])
               for a in range(na)]
        sends = [remote(a, k) for k in range(1, N_DEV) for a in range(na)]
        for cp in own + sends:
            cp.start()
        for k in range(1, N_DEV):
            for a in range(na):
                arrival(a, k).wait_recv()
        for cp in sends:
            cp.wait_send()
        for cp in own:
            cp.wait()

    any_spec = pl.BlockSpec(memory_space=pl.ANY)
    return pl.pallas_call(
        body, name=name, in_specs=[any_spec] * na, out_specs=[any_spec] * na, out_shape=out_shape,
        scratch_shapes=[pltpu.SemaphoreType.DMA((na, N_DEV - 1)), pltpu.SemaphoreType.DMA((na, N_DEV - 1)),
                        pltpu.SemaphoreType.DMA((na,))],
    )(*srcs)


_HBM = pl.BlockSpec(memory_space=pltpu.HBM)
_SEM = pl.BlockSpec(memory_space=pltpu.SEMAPHORE)
_ANY = pl.BlockSpec(memory_space=pl.ANY)


def _peers():
    x, y, c = lax.axis_index("x"), lax.axis_index("y"), lax.axis_index("c")

    def flip(v, bit):
        return 1 - v if bit else v

    def peer(k):
        return (flip(x, (k >> 2) & 1), flip(y, (k >> 1) & 1), flip(c, k & 1))

    def peer_index(k):
        px, py, pc = peer(k)
        return 4 * px + 2 * py + pc

    return 4 * x + 2 * y + c, peer, peer_index


def _split_copy(src_refs, land_refs, send_sems, recv_sems, scatter, a, k, outgoing):
    me, peer, peer_index = _peers()
    if outgoing:
        src = src_refs[a].at[peer_index(k)] if scatter else src_refs[a]
        dst = land_refs[a].at[me]
    else:
        src = src_refs[a].at[me] if scatter else src_refs[a]
        dst = land_refs[a].at[peer_index(k)]
    pair = a * (N_DEV - 1) + k - 1
    return pltpu.make_async_remote_copy(src_ref=src, dst_ref=dst, send_sem=send_sems.at[pair],
                                        recv_sem=recv_sems.at[pair], device_id=peer(k),
                                        device_id_type=pl.DeviceIdType.MESH)


def _exchange_start(srcs, scatter, after, name):
    na = len(srcs)
    land_shapes = [s.shape if scatter else (N_DEV,) + s.shape for s in srcs]
    has_after = after is not None

    def body(*refs):
        src_refs, land_refs = refs[:na], refs[na:2 * na]
        send_sems, recv_sems = refs[2 * na + has_after], refs[2 * na + has_after + 1]
        token = refs[-1]
        for k in range(1, N_DEV):
            for a in range(na):
                _split_copy(src_refs, land_refs, send_sems, recv_sems, scatter, a, k, True).start()
        token[...] = jnp.zeros_like(token)

    sems = pltpu.SemaphoreType.DMA((na * (N_DEV - 1),))
    out_shape = ([sems, sems] + [pltpu.HBM(s.shape, s.dtype) for s in srcs]
                 + [pltpu.HBM(shp, s.dtype) for shp, s in zip(land_shapes, srcs)] + [jax.ShapeDtypeStruct((SUBLANES, 128), F32)])
    args = [pltpu.with_memory_space_constraint(s, pltpu.HBM) for s in srcs]
    args += [pltpu.with_memory_space_constraint(lax.empty(shp, s.dtype), pltpu.HBM) for shp, s in zip(land_shapes, srcs)]
    if has_after:
        args.append(after)
    res = pl.pallas_call(
        body, name=name, in_specs=[_HBM] * (2 * na) + [_ANY] * has_after,
        out_specs=[_SEM, _SEM] + [_HBM] * (2 * na) + [pl.BlockSpec(memory_space=pltpu.VMEM)], out_shape=out_shape,
        input_output_aliases={i: 2 + i for i in range(2 * na)},
        compiler_params=pltpu.CompilerParams(has_side_effects=pltpu.SideEffectType.DATAFLOW_SIDE_EFFECTING),
    )(*args)
    handle = dict(send=res[0], recv=res[1], srcs=list(res[2:2 + na]), lands=list(res[2 + na:2 + 2 * na]), scatter=scatter)
    return handle, res[-1]


def _exchange_wait(handle, after, name):
    srcs, lands, scatter = handle["srcs"], handle["lands"], handle["scatter"]
    na = len(srcs)

    def body(*refs):
        src_refs, land_refs = refs[:na], refs[na:2 * na]
        send_sems, recv_sems = refs[2 * na], refs[2 * na + 1]
        for k in range(1, N_DEV):
            for a in range(na):
                _split_copy(src_refs, land_refs, send_sems, recv_sems, scatter, a, k, True).wait_send()
                _split_copy(src_refs, land_refs, send_sems, recv_sems, scatter, a, k, False).wait_recv()

    res = pl.pallas_call(
        body, name=name, in_specs=[_HBM] * (2 * na) + [_SEM, _SEM, _ANY], out_specs=[_HBM] * (2 * na),
        out_shape=[pltpu.HBM(s.shape, s.dtype) for s in srcs] + [pltpu.HBM(s.shape, s.dtype) for s in lands],
        input_output_aliases={i: i for i in range(2 * na)},
        compiler_params=pltpu.CompilerParams(has_side_effects=pltpu.SideEffectType.DATAFLOW_SIDE_EFFECTING),
    )(*srcs, *lands, handle["send"], handle["recv"], after)
    return list(res[:na]), list(res[na:])


def _layout(D):
    aq, akv = SWA_Q_HEADS * SWA_HEAD_DIM, SWA_KV_HEADS * SWA_HEAD_DIM
    w = SGU_GROUPS * SGU_DIM
    return aq, akv, w


class _Seg:
    def __init__(self, D, rq, rkv):
        aq, akv, w = _layout(D)
        src = {}
        o = 0
        for nm, wd in (("qa", aq), ("ka", akv), ("va", akv), ("cq", rq), ("ckv", rkv), ("kr", MLA_ROPE), ("hu", w), ("hv", w),
                       ("g", 3 * D)):
            src[nm] = (o, wd)
            o += wd
        self.n_in = o
        self.order = ("g", "qa", "hu", "hv", "cq", "ckv", "ka", "va", "kr")
        self.src = src
        self.off = {}
        o = 0
        for nm in self.order:
            self.off[nm] = o
            o += src[nm][1]
        self.width = {nm: src[nm][1] for nm in self.order}
        self.n_pad = -(-o // 1536) * 1536 if o > 1536 else -(-o // 512) * 512
        self.used = o

    def permute(self, w):
        parts = [w[:, self.src[nm][0]:self.src[nm][0] + self.src[nm][1]] for nm in self.order]
        parts.append(jnp.zeros((w.shape[0], self.n_pad - self.used), w.dtype))
        return jnp.concatenate(parts, axis=1)

    def unpermute(self, w):
        names = sorted(self.order, key=lambda nm: self.src[nm][0])
        return jnp.concatenate([w[:, self.off[nm]:self.off[nm] + self.width[nm]] for nm in names], axis=1)


def _uq_permute(w):
    R = w.shape[0]
    H, half = MLA_HEADS, MLA_ROPE // 2
    w3 = w.reshape(R, H, MLA_NOPE + MLA_ROPE)
    return jnp.concatenate([w3[:, :, :MLA_NOPE].reshape(R, H * MLA_NOPE),
                            w3[:, :, MLA_NOPE:MLA_NOPE + half].reshape(R, H * half),
                            w3[:, :, MLA_NOPE + half:].reshape(R, H * half)], axis=1)


def _uq_unpermute(w):
    R = w.shape[0]
    H, half = MLA_HEADS, MLA_ROPE // 2
    n = w[:, :H * MLA_NOPE].reshape(R, H, MLA_NOPE)
    r1 = w[:, H * MLA_NOPE:H * (MLA_NOPE + half)].reshape(R, H, half)
    r2 = w[:, H * (MLA_NOPE + half):].reshape(R, H, half)
    return jnp.concatenate([n, r1, r2], axis=2).reshape(R, H * (MLA_NOPE + MLA_ROPE))


def _ukv_permute(w):
    R = w.shape[0]
    w3 = w.reshape(R, MLA_HEADS, MLA_NOPE + MLA_V)
    return jnp.concatenate([w3[:, :, :MLA_NOPE].reshape(R, -1), w3[:, :, MLA_NOPE:].reshape(R, -1)], axis=1)


def _ukv_unpermute(w):
    R = w.shape[0]
    H = MLA_HEADS
    k = w[:, :H * MLA_NOPE].reshape(R, H, MLA_NOPE)
    v = w[:, H * MLA_NOPE:].reshape(R, H, MLA_V)
    return jnp.concatenate([k, v], axis=2).reshape(R, H * (MLA_NOPE + MLA_V))


def _heads(a, d):
    S = a.shape[0]
    return a.reshape(S, MLA_HEADS, d).transpose(1, 0, 2)


def _unheads(a):
    H, S, d = a.shape
    return a.transpose(1, 0, 2).reshape(S, H * d)


GROUPS = {"a": ("w_in",), "b": ("w_uq", "w_ukv", "w_proj_a", "w_proj_b", "w_proj_c", "w_o", "b_gate"),
          "c": ("w_up", "w_down", "conv_w")}


def _layer_fwd(l, x, x16, fetch, P, cs, sn, seg, alpha):
    S, D = x.shape
    H, half = MLA_HEADS, MLA_ROPE // 2
    off = seg.off
    nm = lambda s: f"l{l}_{s}"
    sv = {"x16": x16}
    W = {"w_in": seg.permute(fetch(l, "a", x16)["w_in"])}
    h = _mm(x16, W["w_in"], "nn", [(F32, "n")], nm("h"))[0]
    sv["h"] = h
    ya, lse_a = _swa_fwd(h, off["qa"], off["ka"], off["va"], P["sinks"], nm("swa_fwd"))
    cqn, rq = _rms_fwd(h, off["cq"], seg.width["cq"], P["q_norm_g"], nm("rmsq_fwd"))
    ckvn, rkv = _rms_fwd(h, off["ckv"], seg.width["ckv"], P["kv_norm_g"], nm("rmskv_fwd"))
    W.update(fetch(l, "b", cqn))
    W["w_uq"] = _uq_permute(W["w_uq"])
    W["w_ukv"] = _ukv_permute(W["w_ukv"])
    qf = _mm(cqn, W["w_uq"], "nn", [(F32, "n")], nm("uq"))[0]
    kvf = _mm(ckvn, W["w_ukv"], "nn", [(BF16, "n")], nm("ukv"))[0]
    cs_h, sn_h = jnp.tile(cs, (1, H)), jnp.tile(sn, (1, H))
    qy1, qy2 = _rope((qf, H * MLA_NOPE), (qf, H * MLA_NOPE + H * half), cs_h, sn_h, H * half, False, nm("ropeq_fwd"))
    kr = h[:, off["kr"]:off["kr"] + MLA_ROPE]
    ky1, ky2 = _rope((kr[:, :half], 0), (kr[:, half:], 0), cs, sn, half, False, nm("ropek_fwd"))
    qh = jnp.concatenate([qf[:, :H * MLA_NOPE].astype(BF16).reshape(S, H, MLA_NOPE), qy1.reshape(S, H, half),
                          qy2.reshape(S, H, half)], axis=2).transpose(1, 0, 2)
    kh = jnp.concatenate([kvf[:, :H * MLA_NOPE].reshape(S, H, MLA_NOPE),
                          jnp.broadcast_to(ky1[:, None, :], (S, H, half)),
                          jnp.broadcast_to(ky2[:, None, :], (S, H, half))], axis=2).transpose(1, 0, 2)
    vh = _heads(kvf[:, H * MLA_NOPE:], MLA_V)
    ob, lse_b = _mla_fwd(qh, kh.transpose(0, 2, 1), vh, nm("mla_fwd"))
    yb = _unheads(ob).astype(BF16)
    w16 = jnp.where(jnp.tril(jnp.ones((SGU_CHUNK, SGU_CHUNK), bool))[None], P["sgu_w"], 0.0).astype(BF16)
    bt = P["sgu_b"].T
    yc = _sgu_fwd(h, off["hu"], off["hv"], P["sgu_ln_g"], P["sgu_ln_b"], w16, bt, nm("sgu_fwd"))
    merged, z = _merge_fwd([ya, yb, yc], [W["w_proj_a"], W["w_proj_b"], W["w_proj_c"]], h, W["b_gate"], nm("merge_fwd"))
    x1, x1_16, xh1, rs1 = _mm_ln(merged, W["w_o"], x, P["ln1_g"], P["ln1_b"], alpha, nm("wo_ln1"))
    W.update(fetch(l, "c", x1_16))
    up = _mm(x1_16, W["w_up"], "nn", [(F32, "n")], nm("up"))[0]
    a = _glu_fwd(up, W["conv_w"], P["conv_b"], nm("glu_fwd"))
    x2, x2_16, xh2, rs2 = _mm_ln(a, W["w_down"], x1, P["ln2_g"], P["ln2_b"], alpha, nm("down_ln2"))
    sv.update(W=W, ya=ya, lse_a=lse_a, cqn=cqn, rq=rq, ckvn=ckvn, rkv=rkv, qh=qh, kh=kh, vh=vh, ob=ob, lse_b=lse_b, yb=yb,
              w16=w16, bt=bt, yc=yc, merged=merged, z=z, x1_16=x1_16, xh1=xh1, rs1=rs1, up=up, a=a, xh2=xh2, rs2=rs2,
              cs_h=cs_h, sn_h=sn_h)
    return x2, x2_16, sv


def _after(arr, token):
    return arr if token is None else arr + token[0:1, 0:1].astype(arr.dtype)


def _layer_bwd(l, dx2, sv, P, cs, sn, seg, alpha, emit):
    S, D = dx2.shape
    H, half = MLA_HEADS, MLA_ROPE // 2
    off = seg.off
    h, W = sv["h"], sv["W"]
    nm = lambda s: f"l{l}_{s}"
    g = {}
    dr2, dr2_16, g["ln2_g"], g["ln2_b"] = _ln_bwd(dx2, sv["xh2"], sv["rs2"], P["ln2_g"], nm("ln2_bwd"))
    g["w_down"] = _mm(sv["a"], dr2_16, "tn", [(F32, "n")], nm("dw_down"))[0]
    da = _mm(dr2_16, W["w_down"], "nt", [(F32, "n")], nm("da"))[0]
    dcg, dcv, st_g, st_v = _glu_bwd(sv["up"], W["conv_w"], P["conv_b"], da, nm("glu_bwd"))
    F = dcg.shape[1]
    g["conv_w"] = jnp.concatenate([st_g[0:3], st_v[0:3]], axis=1)
    g["conv_b"] = jnp.concatenate([st_g[3:4], st_v[3:4]], axis=1)
    dup = jnp.concatenate([_conv_bwd(dcg, W["conv_w"][:, :F], nm("convg_bwd")),
                           _conv_bwd(dcv, W["conv_w"][:, F:], nm("convv_bwd"))], axis=1)
    g["w_up"] = _mm(sv["x1_16"], dup, "tn", [(F32, "n")], nm("dw_up"))[0]
    token = emit(l, "c", {k: g.pop(k) for k in GROUPS["c"]})
    dx1 = _mm_axpy(dup, W["w_up"], "nt", dr2, alpha, nm("dx1"), dep=token)
    dr1, dr1_16, g["ln1_g"], g["ln1_b"] = _ln_bwd(dx1, sv["xh1"], sv["rs1"], P["ln1_g"], nm("ln1_bwd"))
    g["w_o"] = _mm(sv["merged"], dr1_16, "tn", [(F32, "n")], nm("dw_o"))[0]
    dmerged = _mm(dr1_16, W["w_o"], "nt", [(F32, "n")], nm("dmerged"))[0]
    dz, dlog, dbg = _merge_bwd(dmerged, sv["z"], h, W["b_gate"], nm("merge_bwd"))
    g["b_gate"] = dbg[0:3]
    g["w_proj_a"] = _mm(sv["ya"], dz[0], "tn", [(F32, "n")], nm("dw_pa"))[0]
    g["w_proj_b"] = _mm(sv["yb"], dz[1], "tn", [(F32, "n")], nm("dw_pb"))[0]
    g["w_proj_c"] = _mm(sv["yc"], dz[2], "tn", [(F32, "n")], nm("dw_pc"))[0]
    dya = _mm(dz[0], W["w_proj_a"], "nt", [(F32, "n")], nm("dya"))[0]
    dyb = _mm(dz[1], W["w_proj_b"], "nt", [(F32, "n")], nm("dyb"))[0]
    dyc = _mm(dz[2], W["w_proj_c"], "nt", [(F32, "n")], nm("dyc"))[0]
    dhu, dhv, g["sgu_w"], db_s, g["sgu_ln_g"], g["sgu_ln_b"] = _sgu_bwd(
        h, off["hu"], off["hv"], P["sgu_ln_g"], P["sgu_ln_b"], sv["w16"], sv["bt"], dyc, nm("sgu_bwd"))
    g["sgu_b"] = db_s[:, :SGU_GROUPS].T
    dqa, dka, dva, dsk = _swa_bwd(h, off["qa"], off["ka"], off["va"], P["sinks"], dya, sv["lse_a"], nm("swa_bwd"))
    g["sinks"] = dsk[0, :SWA_Q_HEADS]
    dob = _heads(dyb, MLA_V)
    delta = _rowdot(dob, sv["ob"], nm("mla_delta"))
    dqh, dkh, dvh = _mla_bwd(sv["qh"], sv["kh"], sv["vh"], dob.astype(BF16), sv["lse_b"], delta, nm("mla_bwd"))
    n0, n1 = MLA_NOPE, MLA_NOPE + half
    dqx1, dqx2 = _rope((_unheads(dqh[:, :, n0:n1]), 0), (_unheads(dqh[:, :, n1:]), 0), sv["cs_h"], sv["sn_h"], H * half, True,
                       nm("ropeq_bwd"))
    dqf = jnp.concatenate([_unheads(dqh[:, :, :n0]).astype(BF16), dqx1, dqx2], axis=1)
    dkvf = jnp.concatenate([_unheads(dkh[:, :, :n0]), _unheads(dvh)], axis=1).astype(BF16)
    dk1, dk2 = _rope(dkh[:, :, n0:n1], dkh[:, :, n1:], cs, sn, half, True, nm("ropek_bwd"))
    g["w_uq"] = _mm(sv["cqn"], dqf, "tn", [(F32, "n")], nm("dw_uq"))[0]
    g["w_ukv"] = _mm(sv["ckvn"], dkvf, "tn", [(F32, "n")], nm("dw_ukv"))[0]
    g["w_uq"] = _uq_unpermute(g["w_uq"])
    g["w_ukv"] = _ukv_unpermute(g["w_ukv"])
    token = emit(l, "b", {k: g.pop(k) for k in GROUPS["b"]})
    dcqn = _mm(dqf, W["w_uq"], "nt", [(F32, "n")], nm("dcqn"), dep=token)[0]
    dckvn = _mm(dkvf, W["w_ukv"], "nt", [(F32, "n")], nm("dckvn"), dep=token)[0]
    dcq, g["q_norm_g"] = _rms_bwd(dcqn, h, off["cq"], seg.width["cq"], sv["rq"], P["q_norm_g"], nm("rmsq_bwd"))
    dckv, g["kv_norm_g"] = _rms_bwd(dckvn, h, off["ckv"], seg.width["ckv"], sv["rkv"], P["kv_norm_g"], nm("rmskv_bwd"))
    parts = {"g": jnp.concatenate([dlog[0], dlog[1], dlog[2]], axis=1), "qa": dqa, "hu": dhu, "hv": dhv, "cq": dcq, "ckv": dckv,
             "ka": dka.astype(BF16), "va": dva.astype(BF16), "kr": jnp.concatenate([dk1, dk2], axis=1)}
    dh = jnp.concatenate([parts[k] for k in seg.order] + [jnp.zeros((S, seg.n_pad - seg.used), BF16)], axis=1)
    token = emit(l, "a", {"w_in": seg.unpermute(_mm(sv["x16"], dh, "tn", [(F32, "n")], nm("dw_in"))[0])})
    dx = _mm_axpy(dh, W["w_in"], "nt", dr1, alpha, nm("dx"), dep=token)
    return dx, g


BIG = ("w_in", "w_uq", "w_ukv", "w_proj_a", "w_proj_b", "w_proj_c", "w_o", "w_up", "w_down")
ROW_SHARDED = ("w_proj_b", "w_o", "w_down")
SHARDED_F32 = ("b_gate", "conv_w")
REPLICATED = ("sinks", "q_norm_g", "kv_norm_g", "sgu_ln_g", "sgu_ln_b", "sgu_w", "sgu_b", "ln1_g", "ln1_b", "conv_b", "ln2_g",
              "ln2_b")
WEIGHTS = ("w_in", "b_gate", "sinks", "q_norm_g", "kv_norm_g", "w_uq", "w_ukv", "sgu_ln_g", "sgu_ln_b", "sgu_w", "sgu_b",
           "w_proj_a", "w_proj_b", "w_proj_c", "w_o", "ln1_g", "ln1_b", "w_up", "conv_w", "conv_b", "w_down", "ln2_g", "ln2_b")


def _step_local(x, positions, target, small, rq, rkv, fetch, emit, token=None):
    S, D = x.shape
    L = small["sinks"].shape[0]
    alpha = (2 * L) ** 0.25
    seg = _Seg(D, rq, rkv)
    inv_freq = ROPE_THETA ** (-jnp.arange(0, MLA_ROPE, 2, dtype=F32) / MLA_ROPE)
    ang = positions.astype(F32)[:, None] * inv_freq
    cs, sn = jnp.cos(ang), jnp.sin(ang)
    rows = ("q_norm_g", "kv_norm_g", "sgu_ln_g", "sgu_ln_b", "ln1_g", "ln1_b", "conv_b", "ln2_g", "ln2_b")
    layers = [{k: small[k][l].reshape(1, -1) if k in rows else small[k][l] for k in small} for l in range(L)]
    saved = []
    x16 = _after(x, token).astype(BF16)
    for l in range(L):
        x, x16, sv = _layer_fwd(l, x, x16, fetch, layers[l], cs, sn, seg, alpha)
        saved.append(sv)
    loss, dx = _loss(x, target, "loss")
    grads = [None] * L
    for l in reversed(range(L)):
        dx, grads[l] = _layer_bwd(l, dx, saved[l], layers[l], cs, sn, seg, alpha, emit)
    out = {k: jnp.stack([grads[l][k].reshape(small[k].shape[1:]) for l in range(L)]) for k in small}
    return loss, dx, out


def _unshard(k, gathered):
    n, r, c = gathered.shape
    if k in ROW_SHARDED:
        return gathered.reshape(n * r, c)
    return gathered.transpose(1, 0, 2).reshape(r, n * c)


def _to_chunks(k, gfull):
    r, c = gfull.shape
    if k in ROW_SHARDED:
        return gfull.reshape(N_DEV, r // N_DEV, c)
    return gfull.reshape(r, N_DEV, c // N_DEV).transpose(1, 0, 2)


def _pack(arrs):
    P = arrs[0].shape[0]
    flat, sizes = [], []
    for a in arrs:
        f = a.reshape(P, -1)
        n = f.shape[1]
        pad = -n % (SUBLANES * 128)
        flat.append(jnp.pad(f, ((0, 0), (0, pad))))
        sizes.append((n, n + pad))
    return jnp.concatenate(flat, axis=1).reshape(P, -1, 128), sizes


def _unpack(packed, sizes, shapes):
    flat = packed.reshape(-1)
    out, o = [], 0
    for (n, npad), shp in zip(sizes, shapes):
        out.append(flat[o:o + n].reshape(shp))
        o += npad
    return out


def kernel(x, positions, w_in, b_gate, sinks, q_norm_g, kv_norm_g, w_uq, w_ukv, sgu_ln_g, sgu_ln_b, sgu_w, sgu_b, w_proj_a, w_proj_b, w_proj_c, w_o, ln1_g, ln1_b, w_up, conv_w, conv_b, w_down, ln2_g, ln2_b, loss_target, m_w_in, m_b_gate, m_sinks, m_q_norm_g, m_kv_norm_g, m_w_uq, m_w_ukv, m_sgu_ln_g, m_sgu_ln_b, m_sgu_w, m_sgu_b, m_w_proj_a, m_w_proj_b, m_w_proj_c, m_w_o, m_ln1_g, m_ln1_b, m_w_up, m_conv_w, m_conv_b, m_w_down, m_ln2_g, m_ln2_b, v_w_in, v_b_gate, v_sinks, v_q_norm_g, v_kv_norm_g, v_w_uq, v_w_ukv, v_sgu_ln_g, v_sgu_ln_b, v_sgu_w, v_sgu_b, v_w_proj_a, v_w_proj_b, v_w_proj_c, v_w_o, v_ln1_g, v_ln1_b, v_w_up, v_conv_w, v_conv_b, v_w_down, v_ln2_g, v_ln2_b):
    given = dict(locals())
    w = {k: given[k] for k in WEIGHTS}
    mom = {k: given["m_" + k] for k in WEIGHTS}
    var = {k: given["v_" + k] for k in WEIGHTS}

    L = w_in.shape[0]
    order = [(l, grp) for l in range(L) for grp in ("a", "b", "c")]

    gathers, token = {}, None
    for l, grp in order:
        srcs = [w[k][l].astype(BF16) if k in BIG else w[k][l] for k in GROUPS[grp]]
        gathers[l, grp], token = _exchange_start(srcs, False, token, f"gather_start_l{l}{grp}")

    me = 4 * lax.axis_index("x") + 2 * lax.axis_index("y") + lax.axis_index("c")
    mine = (jnp.arange(N_DEV) == me)[:, None, None]

    def fetch(l, grp, after):
        srcs, lands = _exchange_wait(gathers[l, grp], after, f"gather_wait_l{l}{grp}")
        return {k: _unshard(k, jnp.where(mine, srcs[i][None], lands[i])) for i, k in enumerate(GROUPS[grp])}

    scatters = {}

    def emit(l, grp, grads):
        chunks = [_to_chunks(k, grads[k]).astype(BF16 if k in BIG else F32) for k in GROUPS[grp]]
        scatters[l, grp], tok = _exchange_start(chunks, True, None, f"scatter_start_l{l}{grp}")
        return tok

    small = {k: w[k] for k in REPLICATED}
    loss, grad_x, g = _step_local(x[0], positions[0], loss_target[0], small, w_uq.shape[1], w_ukv.shape[1], fetch, emit, token)
    loss = lax.psum(loss[0, 0], AXES)

    slots, own = {}, {}
    for l, grp in [(l, grp) for grp in ("c", "b", "a") for l in reversed(range(L))]:
        srcs, lands = _exchange_wait(scatters[l, grp], grad_x, f"scatter_wait_l{l}{grp}")
        for i, k in enumerate(GROUPS[grp]):
            slots[k, l], own[k, l] = lands[i], srcs[i]
    me1 = me.astype(jnp.int32).reshape(1)
    res = {}
    for k in [k for grp in ("c", "b", "a") for k in GROUPS[grp]]:
        shp = w[k].shape
        r2 = (shp[0] * shp[1], shp[2])
        if k in BIG:
            slot_list, own_list = [slots[k, l] for l in range(L)], [own[k, l] for l in range(L)]
        else:
            slot_list = [jnp.concatenate([slots[k, l] for l in range(L)], axis=1)]
            own_list = [jnp.concatenate([own[k, l] for l in range(L)], axis=1)]
        outs = _adamw(slot_list, own_list, me1, w[k].reshape(r2), mom[k].reshape(r2), var[k].reshape(r2), "adamw_" + k)
        res[k] = [o.reshape(shp) for o in outs]

    packed, sizes = _pack([g[k][None] for k in REPLICATED])
    parts = _exchange([packed[0]], False, "gather_small_grads")[0]
    shapes = [w[k].shape for k in REPLICATED]
    pw, _ = _pack([w[k][None] for k in REPLICATED])
    pm, _ = _pack([mom[k][None] for k in REPLICATED])
    pv, _ = _pack([var[k][None] for k in REPLICATED])
    outs = _adamw([parts], None, me1, pw[0], pm[0], pv[0], "adamw_small")
    unpacked = [_unpack(o, sizes, shapes) for o in outs]
    for i, k in enumerate(REPLICATED):
        res[k] = [unpacked[j][i] for j in range(4)]

    return (loss, grad_x[None], *[res[k][0] for k in WEIGHTS], *[res[k][1] for k in WEIGHTS],
            *[res[k][2] for k in WEIGHTS], *[res[k][3] for k in WEIGHTS])
```

```python
import functools
import math

import jax
import jax.numpy as jnp
from jax import lax
from jax.experimental import pallas as pl
from jax.experimental.pallas import tpu as pltpu

F32 = jnp.float32
BF16 = jnp.bfloat16

SWA_Q_HEADS = 16
SWA_KV_HEADS = 2
SWA_HEAD_DIM = 64
SWA_BLOCK = 128
MLA_HEADS = 16
MLA_NOPE = 128
MLA_ROPE = 64
MLA_V = 128
SGU_GROUPS = 8
SGU_DIM = 128
SGU_CHUNK = 128
ROPE_THETA = 10000.0
EPS = 1e-5
MASK = -1e30
ADAM_LR = 0.001
ADAM_B1 = 0.9
ADAM_B2 = 0.999
ADAM_EPS = 1e-08
ADAM_WD = 0.01
ADAM_STEP = 10

N_DEV = 8
AXES = ("x", "y", "c")
VMEM_LIMIT = 56 * 1024 * 1024
MLA_TILE = 512
MLA_FWD_TILE = 1024
ROW_TILE = 512
MAX_TK = 2816
SUBLANES = 8


def _tile(n, prefs):
    for p in prefs:
        if n % p == 0:
            return p
    return n


def _params(sem):
    return pltpu.CompilerParams(dimension_semantics=sem, vmem_limit_bytes=VMEM_LIMIT)


def _cols(tm, width, off):
    assert off % width == 0, (off, width)
    blk = off // width
    return pl.BlockSpec((tm, width), lambda i, *_: (i, blk))


def _full(shape):
    nd = len(shape)
    return pl.BlockSpec(shape, lambda *_: (0,) * nd)


def _sigmoid(v):
    return 1.0 / (1.0 + jnp.exp(-v))


def _gelu(v):
    return 0.5 * v * (1.0 + lax.erf(v * (2.0 ** -0.5)))


def _gelu_grad(v):
    return 0.5 * (1.0 + lax.erf(v * (2.0 ** -0.5))) + v * jnp.exp(-0.5 * v * v) * (1.0 / math.sqrt(2.0 * math.pi))


_DIMS = {"nn": (((1,), (0,)), ((), ())), "nt": (((1,), (1,)), ((), ())), "tn": (((0,), (0,)), ((), ()))}


def _mm(a, b, mode, outs, name, *, extras=(), epilogue=None, full_n=False, dep=None, chunk=None):
    if mode == "nn":
        (M, K), (K2, N) = a.shape, b.shape
    elif mode == "nt":
        (M, K), (N, K2) = a.shape, b.shape
    else:
        (K, M), (K2, N) = a.shape, b.shape
    assert K == K2, (a.shape, b.shape, mode)
    tm = _tile(M, (1024, 512, 256, 128))
    tn = N if full_n else _tile(N, (1024, 768, 512, 384, 256, 128))
    if full_n:
        tm = _tile(M, (512, 256, 128))
    if chunk is not None:
        tn = chunk if chunk <= 1536 else _tile(chunk, (1024, 768, 512, 384, 256, 128))
        assert N % chunk == 0 and chunk % tn == 0 and tn % 128 == 0, (N, chunk, tn)
    max_tk = MAX_TK // 2 if full_n else MAX_TK
    tk = max(d for d in range(128, min(K, max_tk) + 1, 128) if K % d == 0) if K % 128 == 0 else K
    nk = K // tk
    if mode == "nn":
        a_spec = pl.BlockSpec((tm, tk), lambda i, j, k: (i, k))
        b_spec = pl.BlockSpec((tk, tn), lambda i, j, k: (k, j))
    elif mode == "nt":
        a_spec = pl.BlockSpec((tm, tk), lambda i, j, k: (i, k))
        b_spec = pl.BlockSpec((tn, tk), lambda i, j, k: (j, k))
    else:
        a_spec = pl.BlockSpec((tk, tm), lambda i, j, k: (k, i))
        b_spec = pl.BlockSpec((tk, tn), lambda i, j, k: (k, j))
    in_specs = [a_spec, b_spec]
    for arr, kind in extras:
        if kind == "tile":
            in_specs.append(pl.BlockSpec((tm, tn), lambda i, j, k: (i, j)))
        else:
            in_specs.append(pl.BlockSpec((1, tn), lambda i, j, k: (0, j)))
    out_specs, out_shape = [], []
    for dt, kind in outs:
        if kind == "n" and chunk is not None:
            per = chunk // tn
            out_specs.append(pl.BlockSpec((None, tm, tn), lambda i, j, k: (lax.div(j, per), i, lax.rem(j, per))))
            out_shape.append(jax.ShapeDtypeStruct((N // chunk, M, chunk), dt))
        elif kind == "n":
            out_specs.append(pl.BlockSpec((tm, tn), lambda i, j, k: (i, j)))
            out_shape.append(jax.ShapeDtypeStruct((M, N), dt))
        else:
            assert tn == N
            out_specs.append(pl.BlockSpec((tm, 1), lambda i, j, k: (i, 0)))
            out_shape.append(jax.ShapeDtypeStruct((M, 1), dt))
    ne, no = len(extras), len(outs)
    deps = []
    if dep is not None:
        in_specs.append(_full(dep.shape))
        deps = [dep]
    dims = _DIMS[mode]
    if epilogue is None:
        epilogue = lambda acc: (acc,) * no

    def body(*refs):
        a_ref, b_ref = refs[0], refs[1]
        ex = refs[2:2 + ne]
        out = refs[len(refs) - 1 - no:len(refs) - 1]
        acc = refs[-1]
        k = pl.program_id(2)
        part = lax.dot_general(a_ref[...].astype(BF16), b_ref[...].astype(BF16), dims, preferred_element_type=F32)

        def finish(total):
            res = epilogue(total, *[e[...] for e in ex])
            for o, r in zip(out, res):
                o[...] = r.astype(o.dtype)

        if nk == 1:
            finish(part)
            return

        @pl.when(k == 0)
        def _():
            acc[...] = part

        @pl.when((k > 0) & (k < nk - 1))
        def _():
            acc[...] += part

        @pl.when(k == nk - 1)
        def _():
            finish(acc[...] + part)

    res = pl.pallas_call(
        body, name=name, grid=(M // tm, N // tn, nk), in_specs=in_specs, out_specs=out_specs, out_shape=out_shape,
        scratch_shapes=[pltpu.VMEM((tm, tn), F32)],
        compiler_params=_params(("parallel", "parallel", "arbitrary")),
    )(a, b, *[e[0] for e in extras], *deps)
    return res


def _ln_epilogue(alpha):
    def epi(acc, x, g, b):
        r = alpha * x + acc
        mu = jnp.mean(r, axis=-1, keepdims=True)
        d = r - mu
        var = jnp.mean(d * d, axis=-1, keepdims=True)
        rstd = lax.rsqrt(var + EPS)
        xhat = d * rstd
        y = xhat * g + b
        return y, y, xhat, rstd
    return epi


def _mm_ln(a, w, x, g, b, alpha, name):
    return _mm(a, w, "nn", [(F32, "n"), (BF16, "n"), (F32, "n"), (F32, "1")], name,
               extras=[(x, "tile"), (g, "row"), (b, "row")], epilogue=_ln_epilogue(alpha), full_n=True)


def _mm_axpy(a, w, mode, r, alpha, name, dep=None):
    return _mm(a, w, mode, [(F32, "n")], name, extras=[(r, "tile")],
               epilogue=lambda acc, rv: (acc + alpha * rv,), dep=dep)[0]


def _ln_bwd(dy, xhat, rstd, g, name):
    S, D = dy.shape
    tm = _tile(S, (256, 128))

    def body(dy_ref, xh_ref, rs_ref, g_ref, dr_ref, dr16_ref, dg_ref, db_ref):
        @pl.when(pl.program_id(0) == 0)
        def _():
            dg_ref[...] = jnp.zeros_like(dg_ref)
            db_ref[...] = jnp.zeros_like(db_ref)

        dyv, xh = dy_ref[...], xh_ref[...]
        dxh = dyv * g_ref[...]
        m1 = jnp.mean(dxh, axis=-1, keepdims=True)
        m2 = jnp.mean(dxh * xh, axis=-1, keepdims=True)
        dr = rs_ref[...] * (dxh - m1 - xh * m2)
        dr_ref[...] = dr
        dr16_ref[...] = dr.astype(BF16)
        dg_ref[...] += jnp.sum(dyv * xh, axis=0, keepdims=True)
        db_ref[...] += jnp.sum(dyv, axis=0, keepdims=True)

    row = pl.BlockSpec((tm, D), lambda i: (i, 0))
    return pl.pallas_call(
        body, name=name, grid=(S // tm,),
        in_specs=[row, row, pl.BlockSpec((tm, 1), lambda i: (i, 0)), _full((1, D))],
        out_specs=[row, row, _full((1, D)), _full((1, D))],
        out_shape=[jax.ShapeDtypeStruct((S, D), F32), jax.ShapeDtypeStruct((S, D), BF16),
                   jax.ShapeDtypeStruct((1, D), F32), jax.ShapeDtypeStruct((1, D), F32)],
        compiler_params=_params(("arbitrary",)),
    )(dy, xhat, rstd, g)


def _rms_fwd(h, off, width, g, name):
    S = h.shape[0]
    tm = _tile(S, (ROW_TILE, 256, 128))

    def body(c_ref, g_ref, y_ref, r_ref):
        c = c_ref[...]
        r = lax.rsqrt(jnp.mean(c * c, axis=-1, keepdims=True) + EPS)
        y_ref[...] = (c * r * g_ref[...]).astype(BF16)
        r_ref[...] = r

    return pl.pallas_call(
        body, name=name, grid=(S // tm,),
        in_specs=[_cols(tm, width, off), _full((1, width))],
        out_specs=[pl.BlockSpec((tm, width), lambda i: (i, 0)), pl.BlockSpec((tm, 1), lambda i: (i, 0))],
        out_shape=[jax.ShapeDtypeStruct((S, width), BF16), jax.ShapeDtypeStruct((S, 1), F32)],
        compiler_params=_params(("parallel",)),
    )(h, g)


def _rms_bwd(dy, h, off, width, rstd, g, name):
    S = h.shape[0]
    tm = _tile(S, (ROW_TILE, 256, 128))

    def body(dy_ref, c_ref, r_ref, g_ref, dc_ref, dg_ref):
        @pl.when(pl.program_id(0) == 0)
        def _():
            dg_ref[...] = jnp.zeros_like(dg_ref)

        dyv, c, r = dy_ref[...], c_ref[...], r_ref[...]
        dyg = dyv * g_ref[...]
        m = jnp.mean(dyg * c, axis=-1, keepdims=True)
        dc_ref[...] = (r * dyg - c * (r * r * r) * m).astype(BF16)
        dg_ref[...] += jnp.sum(dyv * c * r, axis=0, keepdims=True)

    return pl.pallas_call(
        body, name=name, grid=(S // tm,),
        in_specs=[pl.BlockSpec((tm, width), lambda i: (i, 0)), _cols(tm, width, off),
                  pl.BlockSpec((tm, 1), lambda i: (i, 0)), _full((1, width))],
        out_specs=[pl.BlockSpec((tm, width), lambda i: (i, 0)), _full((1, width))],
        out_shape=[jax.ShapeDtypeStruct((S, width), BF16), jax.ShapeDtypeStruct((1, width), F32)],
        compiler_params=_params(("arbitrary",)),
    )(dy, h, rstd, g)


def _loss(y, target, name):
    S, D = y.shape
    tm = _tile(S, (256, 128))

    def body(y_ref, t_ref, l_ref, dy_ref):
        @pl.when(pl.program_id(0) == 0)
        def _():
            l_ref[...] = jnp.zeros_like(l_ref)

        err = y_ref[...] - t_ref[...]
        dy_ref[...] = err * (1.0 / D)
        per_tok = jnp.mean(err * err, axis=-1, keepdims=True)
        l_ref[...] += 0.5 * jnp.sum(per_tok, axis=0, keepdims=True)

    row = pl.BlockSpec((tm, D), lambda i: (i, 0))
    return pl.pallas_call(
        body, name=name, grid=(S // tm,), in_specs=[row, row], out_specs=[_full((1, 1)), row],
        out_shape=[jax.ShapeDtypeStruct((1, 1), F32), jax.ShapeDtypeStruct((S, D), F32)],
        compiler_params=_params(("arbitrary",)),
    )(y, target)


def _rope(a1, a2, cs, sn, n, bwd, name):
    S = cs.shape[0]
    tm = _tile(S, (ROW_TILE, 256, 128))
    stacked = not isinstance(a1, tuple)

    def body(a1_ref, a2_ref, c_ref, s_ref, y1_ref, y2_ref):
        if stacked:
            v1 = jnp.sum(a1_ref[...], axis=0)
            v2 = jnp.sum(a2_ref[...], axis=0)
        else:
            v1, v2 = a1_ref[...].astype(F32), a2_ref[...].astype(F32)
        c, s = c_ref[...], s_ref[...]
        if bwd:
            y1_ref[...] = (v1 * c + v2 * s).astype(BF16)
            y2_ref[...] = (v2 * c - v1 * s).astype(BF16)
        else:
            y1_ref[...] = (v1 * c - v2 * s).astype(BF16)
            y2_ref[...] = (v2 * c + v1 * s).astype(BF16)

    row = pl.BlockSpec((tm, n), lambda i: (i, 0))
    if stacked:
        H = a1.shape[0]
        a_specs = [pl.BlockSpec((H, tm, n), lambda i: (0, i, 0))] * 2
        arrs = [a1, a2]
    else:
        a_specs = [_cols(tm, n, a1[1]), _cols(tm, n, a2[1])]
        arrs = [a1[0], a2[0]]
    return pl.pallas_call(
        body, name=name, grid=(S // tm,), in_specs=a_specs + [row, row], out_specs=[row, row],
        out_shape=[jax.ShapeDtypeStruct((S, n), BF16)] * 2,
        compiler_params=_params(("parallel",)),
    )(*arrs, cs, sn)


def _swa_mask(n):
    blk = SWA_BLOCK
    row = lax.broadcasted_iota(jnp.int32, (blk, 2 * blk), 0)
    col = lax.broadcasted_iota(jnp.int32, (blk, 2 * blk), 1)
    rel = row + blk - col
    return (rel >= 0) & (rel < blk) & ((n > 0) | (col >= blk))


def _swa_specs(off_q, off_k, off_v):
    blk, aq, akv = SWA_BLOCK, SWA_Q_HEADS * SWA_HEAD_DIM, SWA_KV_HEADS * SWA_HEAD_DIM
    assert off_q % aq == 0 and off_k % akv == 0 and off_v % akv == 0
    prev = lambda off: pl.BlockSpec((blk, akv), lambda n: (jnp.maximum(n - 1, 0), off // akv))
    cur = lambda off: pl.BlockSpec((blk, akv), lambda n: (n, off // akv))
    return [pl.BlockSpec(memory_space=pltpu.SMEM), _cols(blk, aq, off_q), prev(off_k), cur(off_k), prev(off_v), cur(off_v)]


def _swa_fwd(h, off_q, off_k, off_v, sinks, name):
    S = h.shape[0]
    blk, hd, nh = SWA_BLOCK, SWA_HEAD_DIM, SWA_Q_HEADS
    grp = nh // SWA_KV_HEADS
    aq = nh * hd
    scale = hd ** -0.5

    def body(sink_ref, q_ref, kp_ref, kc_ref, vp_ref, vc_ref, o_ref, lse_ref):
        valid = _swa_mask(pl.program_id(0))
        q = q_ref[...].astype(BF16)
        k2 = jnp.concatenate([kp_ref[...], kc_ref[...]], axis=0).astype(BF16)
        v2 = jnp.concatenate([vp_ref[...], vc_ref[...]], axis=0).astype(BF16)
        for hh in range(nh):
            kv = hh // grp
            qh = q[:, hh * hd:(hh + 1) * hd]
            kh = k2[:, kv * hd:(kv + 1) * hd]
            vh = v2[:, kv * hd:(kv + 1) * hd]
            s = lax.dot_general(qh, kh, _DIMS["nt"], preferred_element_type=F32) * scale
            s = jnp.where(valid, s, MASK)
            sk = sink_ref[hh]
            m = jnp.maximum(jnp.max(s, axis=1, keepdims=True), sk)
            p = jnp.exp(s - m)
            l = jnp.sum(p, axis=1, keepdims=True) + jnp.exp(sk - m)
            o_ref[:, hh * hd:(hh + 1) * hd] = jnp.dot((p / l).astype(BF16), vh, preferred_element_type=F32)
            lse_ref[:, hh:hh + 1] = m + jnp.log(l)

    return pl.pallas_call(
        body, name=name, grid=(S // blk,), in_specs=_swa_specs(off_q, off_k, off_v),
        out_specs=[pl.BlockSpec((blk, aq), lambda n: (n, 0)), pl.BlockSpec((blk, nh), lambda n: (n, 0))],
        out_shape=[jax.ShapeDtypeStruct((S, aq), F32), jax.ShapeDtypeStruct((S, nh), F32)],
        compiler_params=_params(("parallel",)),
    )(sinks, h, h, h, h, h)


def _swa_bwd(h, off_q, off_k, off_v, sinks, dout, lse, name):
    S = h.shape[0]
    blk, hd, nh, nkv = SWA_BLOCK, SWA_HEAD_DIM, SWA_Q_HEADS, SWA_KV_HEADS
    grp = nh // nkv
    aq, akv = nh * hd, nkv * hd
    scale = hd ** -0.5

    def body(sink_ref, q_ref, kp_ref, kc_ref, vp_ref, vc_ref, do_ref, lse_ref, dq_ref, dk_ref, dv_ref, ds_ref):
        n = pl.program_id(0)

        @pl.when(n == 0)
        def _():
            dk_ref[...] = jnp.zeros_like(dk_ref)
            dv_ref[...] = jnp.zeros_like(dv_ref)
            ds_ref[...] = jnp.zeros_like(ds_ref)

        valid = _swa_mask(n)
        q = q_ref[...].astype(BF16)
        k2 = jnp.concatenate([kp_ref[...], kc_ref[...]], axis=0).astype(BF16)
        v2 = jnp.concatenate([vp_ref[...], vc_ref[...]], axis=0).astype(BF16)
        do = do_ref[...]
        lane = lax.broadcasted_iota(jnp.int32, (1, 128), 1)
        dsink = jnp.zeros((1, 128), F32)
        cur = pl.ds(pl.multiple_of(n * blk, blk), blk)
        prev = pl.ds(pl.multiple_of(jnp.maximum(n - 1, 0) * blk, blk), blk)
        for kv in range(nkv):
            kh = k2[:, kv * hd:(kv + 1) * hd]
            vh = v2[:, kv * hd:(kv + 1) * hd]
            dk_acc = jnp.zeros((2 * blk, hd), F32)
            dv_acc = jnp.zeros((2 * blk, hd), F32)
            for g in range(grp):
                hh = kv * grp + g
                qh = q[:, hh * hd:(hh + 1) * hd]
                doh = do[:, hh * hd:(hh + 1) * hd]
                doh16 = doh.astype(BF16)
                lse_h = lse_ref[:, hh:hh + 1]
                s = lax.dot_general(qh, kh, _DIMS["nt"], preferred_element_type=F32) * scale
                s = jnp.where(valid, s, MASK)
                p = jnp.exp(s - lse_h)
                p16 = p.astype(BF16)
                o = jnp.dot(p16, vh, preferred_element_type=F32)
                delta = jnp.sum(doh * o, axis=1, keepdims=True)
                dp = lax.dot_general(doh16, vh, _DIMS["nt"], preferred_element_type=F32)
                ds16 = (p * (dp - delta) * scale).astype(BF16)
                dq_ref[:, hh * hd:(hh + 1) * hd] = jnp.dot(ds16, kh, preferred_element_type=F32).astype(BF16)
                dk_acc += lax.dot_general(ds16, qh, _DIMS["tn"], preferred_element_type=F32)
                dv_acc += lax.dot_general(p16, doh16, _DIMS["tn"], preferred_element_type=F32)
                dsk = -jnp.sum(jnp.exp(sink_ref[hh] - lse_h) * delta, axis=0, keepdims=True)
                dsink += jnp.where(lane == hh, dsk, 0.0)
            cols = slice(kv * hd, (kv + 1) * hd)
            dk_ref[cur, cols] += dk_acc[blk:]
            dv_ref[cur, cols] += dv_acc[blk:]

            @pl.when(n > 0)
            def _():
                dk_ref[prev, cols] += dk_acc[:blk]
                dv_ref[prev, cols] += dv_acc[:blk]

        ds_ref[...] += dsink

    return pl.pallas_call(
        body, name=name, grid=(S // blk,),
        in_specs=_swa_specs(off_q, off_k, off_v) + [pl.BlockSpec((blk, aq), lambda n: (n, 0)),
                                                    pl.BlockSpec((blk, nh), lambda n: (n, 0))],
        out_specs=[pl.BlockSpec((blk, aq), lambda n: (n, 0)), _full((S, akv)), _full((S, akv)), _full((1, 128))],
        out_shape=[jax.ShapeDtypeStruct((S, aq), BF16), jax.ShapeDtypeStruct((S, akv), F32),
                   jax.ShapeDtypeStruct((S, akv), F32), jax.ShapeDtypeStruct((1, 128), F32)],
        compiler_params=_params(("arbitrary",)),
    )(sinks, h, h, h, h, h, dout, lse)


def _causal(i, j, t):
    row = i * t + lax.broadcasted_iota(jnp.int32, (t, t), 0)
    col = j * t + lax.broadcasted_iota(jnp.int32, (t, t), 1)
    return col <= row


def _mla_fwd(q, kt, v, name):
    H, S, dq = q.shape
    dv = v.shape[2]
    t = _tile(S, (MLA_FWD_TILE, 512, 256, 128))
    nq = S // t
    hp = 2 if H % 2 == 0 else 1
    scale = (MLA_NOPE + MLA_ROPE) ** -0.5

    def body(q_ref, kt_ref, v_ref, o_ref, lse_ref, *state):
        i = pl.program_id(1)
        for hh in range(hp):
            m_s, l_s, acc_s = state[3 * hh:3 * hh + 3]
            m_s[...] = jnp.full_like(m_s, -jnp.inf)
            l_s[...] = jnp.zeros_like(l_s)
            acc_s[...] = jnp.zeros_like(acc_s)

        def block(j, masked):
            rows = pl.ds(pl.multiple_of(j * t, t), t)
            for hh in range(hp):
                m_s, l_s, acc_s = state[3 * hh:3 * hh + 3]
                s = jnp.dot(q_ref[hh], kt_ref[hh, :, rows], preferred_element_type=F32) * scale
                if masked:
                    s = jnp.where(_causal(0, 0, t), s, MASK)
                m_old = m_s[...]
                m_new = jnp.maximum(m_old, jnp.max(s, axis=1, keepdims=True))
                alpha = jnp.exp(m_old - m_new)
                p = jnp.exp(s - m_new)
                l_s[...] = alpha * l_s[...] + jnp.sum(p, axis=1, keepdims=True)
                acc_s[...] = alpha * acc_s[...] + jnp.dot(p.astype(BF16), v_ref[hh, rows, :], preferred_element_type=F32)
                m_s[...] = m_new

        def full_block(j, carry):
            block(j, False)
            return carry

        lax.fori_loop(0, i, full_block, 0)
        block(i, True)
        for hh in range(hp):
            m_s, l_s, acc_s = state[3 * hh:3 * hh + 3]
            o_ref[hh] = acc_s[...] / l_s[...]
            lse_ref[hh] = m_s[...] + jnp.log(l_s[...])

    tile = lambda d: pl.BlockSpec((hp, t, d), lambda h, i: (h, i, 0))
    return pl.pallas_call(
        body, name=name, grid=(H // hp, nq),
        in_specs=[tile(dq), pl.BlockSpec((hp, dq, S), lambda h, i: (h, 0, 0)), pl.BlockSpec((hp, S, dv), lambda h, i: (h, 0, 0))],
        out_specs=[tile(dv), tile(1)],
        out_shape=[jax.ShapeDtypeStruct((H, S, dv), F32), jax.ShapeDtypeStruct((H, S, 1), F32)],
        scratch_shapes=[pltpu.VMEM((t, 1), F32), pltpu.VMEM((t, 1), F32), pltpu.VMEM((t, dv), F32)] * hp,
        compiler_params=_params(("parallel", "arbitrary")),
    )(q, kt, v)


def _rowdot(a, b, name):
    H, S, d = a.shape
    t = _tile(S, (ROW_TILE, 256, 128))

    def body(a_ref, b_ref, o_ref):
        o_ref[...] = jnp.sum(a_ref[...] * b_ref[...], axis=-1, keepdims=True)

    spec = pl.BlockSpec((1, t, d), lambda h, i: (h, i, 0))
    return pl.pallas_call(
        body, name=name, grid=(H, S // t), in_specs=[spec, spec],
        out_specs=pl.BlockSpec((1, t, 1), lambda h, i: (h, i, 0)),
        out_shape=jax.ShapeDtypeStruct((H, S, 1), F32),
        compiler_params=_params(("parallel", "parallel")),
    )(a, b)


def _mla_bwd(q, k, v, do, lse, delta, name):
    H, S, dq = q.shape
    dv = v.shape[2]
    t = _tile(S, (MLA_TILE, 256, 128))
    nq = S // t
    hp = 2 if H % 2 == 0 else 1
    scale = (MLA_NOPE + MLA_ROPE) ** -0.5

    def body(q_ref, k_ref, v_ref, do_ref, lse_ref, dl_ref, dq_ref, dk_ref, dv_ref, dk_s, dv_s):
        j, i = pl.program_id(1), pl.program_id(2)

        @pl.when((j == 0) & (i == 0))
        def _():
            dq_ref[...] = jnp.zeros_like(dq_ref)

        @pl.when(i == j)
        def _():
            dk_s[...] = jnp.zeros_like(dk_s)
            dv_s[...] = jnp.zeros_like(dv_s)

        def block(masked):
            rows = pl.ds(pl.multiple_of(i * t, t), t)
            for hh in range(hp):
                qv, kv_, vv, dov = q_ref[hh], k_ref[hh], v_ref[hh], do_ref[hh]
                s = lax.dot_general(qv, kv_, _DIMS["nt"], preferred_element_type=F32) * scale
                if masked:
                    s = jnp.where(_causal(0, 0, t), s, MASK)
                p = jnp.exp(s - lse_ref[hh])
                p16 = p.astype(BF16)
                dp = lax.dot_general(dov, vv, _DIMS["nt"], preferred_element_type=F32)
                ds16 = (p * (dp - dl_ref[hh]) * scale).astype(BF16)
                dv_s[hh] += lax.dot_general(p16, dov, _DIMS["tn"], preferred_element_type=F32)
                dk_s[hh] += lax.dot_general(ds16, qv, _DIMS["tn"], preferred_element_type=F32)
                dq_ref[hh, rows, :] += jnp.dot(ds16, kv_, preferred_element_type=F32)

        @pl.when(i == j)
        def _():
            block(True)

        @pl.when(i > j)
        def _():
            block(False)

        @pl.when(i == nq - 1)
        def _():
            dk_ref[...] = dk_s[...]
            dv_ref[...] = dv_s[...]

    q_map = lambda h, j, i: (h, jnp.maximum(i, j), 0)
    kv_map = lambda h, j, i: (h, j, 0)
    return pl.pallas_call(
        body, name=name, grid=(H // hp, nq, nq),
        in_specs=[pl.BlockSpec((hp, t, dq), q_map), pl.BlockSpec((hp, t, dq), kv_map), pl.BlockSpec((hp, t, dv), kv_map),
                  pl.BlockSpec((hp, t, dv), q_map), pl.BlockSpec((hp, t, 1), q_map), pl.BlockSpec((hp, t, 1), q_map)],
        out_specs=[pl.BlockSpec((hp, S, dq), lambda h, j, i: (h, 0, 0)), pl.BlockSpec((hp, t, dq), kv_map),
                   pl.BlockSpec((hp, t, dv), kv_map)],
        out_shape=[jax.ShapeDtypeStruct((H, S, dq), F32), jax.ShapeDtypeStruct((H, S, dq), F32),
                   jax.ShapeDtypeStruct((H, S, dv), F32)],
        scratch_shapes=[pltpu.VMEM((hp, t, dq), F32), pltpu.VMEM((hp, t, dv), F32)],
        compiler_params=_params(("parallel", "arbitrary", "arbitrary")),
    )(q, k, v, do, lse, delta)


def _sgu_norm(hv, lg, lb):
    vg = _gelu(hv)
    mu = jnp.mean(vg, axis=-1, keepdims=True)
    d = vg - mu
    rstd = lax.rsqrt(jnp.mean(d * d, axis=-1, keepdims=True) + EPS)
    xhat = d * rstd
    return xhat, rstd, xhat * lg + lb


def _sgu_fwd(h, off_u, off_v, lg, lb, w16, bt, name):
    S = h.shape[0]
    T, G, C = SGU_CHUNK, SGU_GROUPS, SGU_DIM
    W = G * C

    def body(hu_ref, hv_ref, lg_ref, lb_ref, w_ref, bt_ref, y_ref):
        u = _gelu(hu_ref[...])
        _, _, vn = _sgu_norm(hv_ref[...], lg_ref[...], lb_ref[...])
        vn16 = vn.astype(BF16)
        for g in range(G):
            cols = slice(g * C, (g + 1) * C)
            mixed = jnp.dot(w_ref[g], vn16[:, cols], preferred_element_type=F32) + bt_ref[:, g:g + 1]
            y_ref[:, cols] = (u[:, cols] * mixed).astype(BF16)

    return pl.pallas_call(
        body, name=name, grid=(S // T,),
        in_specs=[_cols(T, W, off_u), _cols(T, W, off_v), _full((1, W)), _full((1, W)), _full((G, T, T)), _full((T, G))],
        out_specs=pl.BlockSpec((T, W), lambda n: (n, 0)),
        out_shape=jax.ShapeDtypeStruct((S, W), BF16),
        compiler_params=_params(("parallel",)),
    )(h, h, lg, lb, w16, bt)


def _sgu_bwd(h, off_u, off_v, lg, lb, w16, bt, dy, name):
    S = h.shape[0]
    T, G, C = SGU_CHUNK, SGU_GROUPS, SGU_DIM
    W = G * C
    nc = S // T

    def body(hu_ref, hv_ref, lg_ref, lb_ref, w_ref, bt_ref, dy_ref, dhu_ref, dhv_ref, dw_ref, db_ref, dlg_ref, dlb_ref,
             dmix_s, dvn_s):
        n = pl.program_id(0)

        @pl.when(n == 0)
        def _():
            dw_ref[...] = jnp.zeros_like(dw_ref)
            dlg_ref[...] = jnp.zeros_like(dlg_ref)
            dlb_ref[...] = jnp.zeros_like(dlb_ref)
            dmix_s[...] = jnp.zeros_like(dmix_s)

        hu, hv, lgv = hu_ref[...], hv_ref[...], lg_ref[...]
        u = _gelu(hu)
        xhat, rstd, vn = _sgu_norm(hv, lgv, lb_ref[...])
        vn16 = vn.astype(BF16)
        dyv = dy_ref[...]
        dmixed = dyv * u
        dmix_s[...] += dmixed
        dmixed16 = dmixed.astype(BF16)
        for g in range(G):
            cols = slice(g * C, (g + 1) * C)
            mixed = jnp.dot(w_ref[g], vn16[:, cols], preferred_element_type=F32) + bt_ref[:, g:g + 1]
            dhu_ref[:, cols] = (dyv[:, cols] * mixed * _gelu_grad(hu[:, cols])).astype(BF16)
            dvn_s[:, cols] = lax.dot_general(w_ref[g], dmixed16[:, cols], _DIMS["tn"], preferred_element_type=F32)
            dw_ref[g] += lax.dot_general(dmixed16[:, cols], vn16[:, cols], _DIMS["nt"], preferred_element_type=F32)
        dvn = dvn_s[...]
        dlg_ref[...] += jnp.sum(dvn * xhat, axis=0, keepdims=True)
        dlb_ref[...] += jnp.sum(dvn, axis=0, keepdims=True)
        dxh = dvn * lgv
        m1 = jnp.mean(dxh, axis=-1, keepdims=True)
        m2 = jnp.mean(dxh * xhat, axis=-1, keepdims=True)
        dvg = rstd * (dxh - m1 - xhat * m2)
        dhv_ref[...] = (dvg * _gelu_grad(hv)).astype(BF16)

        @pl.when(n == nc - 1)
        def _():
            tril = lax.broadcasted_iota(jnp.int32, (T, T), 1) <= lax.broadcasted_iota(jnp.int32, (T, T), 0)
            lane = lax.broadcasted_iota(jnp.int32, (T, 128), 1)
            db = jnp.zeros((T, 128), F32)
            for g in range(G):
                dw_ref[g] = jnp.where(tril, dw_ref[g], 0.0)
                db += jnp.where(lane == g, jnp.sum(dmix_s[:, g * C:(g + 1) * C], axis=1, keepdims=True), 0.0)
            db_ref[...] = db

    row = pl.BlockSpec((T, W), lambda n: (n, 0))
    return pl.pallas_call(
        body, name=name, grid=(nc,),
        in_specs=[_cols(T, W, off_u), _cols(T, W, off_v), _full((1, W)), _full((1, W)), _full((G, T, T)), _full((T, G)), row],
        out_specs=[row, row, _full((G, T, T)), _full((T, 128)), _full((1, W)), _full((1, W))],
        out_shape=[jax.ShapeDtypeStruct((S, W), BF16), jax.ShapeDtypeStruct((S, W), BF16),
                   jax.ShapeDtypeStruct((G, T, T), F32), jax.ShapeDtypeStruct((T, 128), F32),
                   jax.ShapeDtypeStruct((1, W), F32), jax.ShapeDtypeStruct((1, W), F32)],
        scratch_shapes=[pltpu.VMEM((T, W), F32), pltpu.VMEM((T, W), F32)],
        compiler_params=_params(("arbitrary",)),
    )(h, h, lg, lb, w16, bt, dy)


def _merge_fwd(ys, ps, h, bg, name):
    S = h.shape[0]
    D = ps[0].shape[1]
    tm = _tile(S, (512, 256, 128))
    tn = _tile(D, (512, 256, 128))
    nb = len(ys)

    def body(*refs):
        y_refs, p_refs, l_refs = refs[:nb], refs[nb:2 * nb], refs[2 * nb:3 * nb]
        bg_ref, mg_ref, z_ref = refs[3 * nb:]
        acc = jnp.zeros((tm, tn), F32)
        for b in range(nb):
            z = jnp.dot(y_refs[b][...].astype(BF16), p_refs[b][...], preferred_element_type=F32)
            z_ref[b] = z
            acc += _sigmoid(l_refs[b][...] + bg_ref[b:b + 1, :]) * z
        mg_ref[...] = acc.astype(BF16)

    in_specs = [pl.BlockSpec((tm, y.shape[1]), lambda i, j: (i, 0)) for y in ys]
    in_specs += [pl.BlockSpec((p.shape[0], tn), lambda i, j: (0, j)) for p in ps]
    in_specs += [pl.BlockSpec((tm, tn), functools.partial(lambda i, j, b: (i, b * (D // tn) + j), b=b)) for b in range(nb)]
    in_specs += [pl.BlockSpec((nb, tn), lambda i, j: (0, j))]
    return pl.pallas_call(
        body, name=name, grid=(S // tm, D // tn), in_specs=in_specs,
        out_specs=[pl.BlockSpec((tm, tn), lambda i, j: (i, j)), pl.BlockSpec((nb, tm, tn), lambda i, j: (0, i, j))],
        out_shape=[jax.ShapeDtypeStruct((S, D), BF16), jax.ShapeDtypeStruct((nb, S, D), F32)],
        compiler_params=_params(("parallel", "parallel")),
    )(*ys, *ps, *([h] * nb), bg)


def _merge_bwd(dm, z, h, bg, name):
    nb, S, D = z.shape
    tm = _tile(S, (256, 128))
    tn = _tile(D, (512, 256, 128))

    def body(*refs):
        dm_ref, z_ref = refs[0], refs[1]
        l_refs = refs[2:2 + nb]
        bg_ref, dz_ref, dl_ref, dbg_ref = refs[2 + nb:]

        @pl.when(pl.program_id(1) == 0)
        def _():
            dbg_ref[...] = jnp.zeros_like(dbg_ref)

        dmv = dm_ref[...]
        rows = lax.broadcasted_iota(jnp.int32, (SUBLANES, tn), 0)
        dbg = jnp.zeros((SUBLANES, tn), F32)
        for b in range(nb):
            gt = _sigmoid(l_refs[b][...] + bg_ref[b:b + 1, :])
            dz_ref[b] = (dmv * gt).astype(BF16)
            dl = dmv * z_ref[b] * gt * (1.0 - gt)
            dl_ref[b] = dl.astype(BF16)
            dbg += jnp.where(rows == b, jnp.sum(dl, axis=0, keepdims=True), 0.0)
        dbg_ref[...] += dbg

    in_specs = [pl.BlockSpec((tm, tn), lambda j, i: (i, j)), pl.BlockSpec((nb, tm, tn), lambda j, i: (0, i, j))]
    in_specs += [pl.BlockSpec((tm, tn), functools.partial(lambda j, i, b: (i, b * (D // tn) + j), b=b)) for b in range(nb)]
    in_specs += [pl.BlockSpec((nb, tn), lambda j, i: (0, j))]
    blk3 = pl.BlockSpec((nb, tm, tn), lambda j, i: (0, i, j))
    return pl.pallas_call(
        body, name=name, grid=(D // tn, S // tm), in_specs=in_specs,
        out_specs=[blk3, blk3, pl.BlockSpec((SUBLANES, tn), lambda j, i: (0, j))],
        out_shape=[jax.ShapeDtypeStruct((nb, S, D), BF16), jax.ShapeDtypeStruct((nb, S, D), BF16),
                   jax.ShapeDtypeStruct((SUBLANES, D), F32)],
        compiler_params=_params(("parallel", "arbitrary")),
    )(dm, z, *([h] * nb), bg)


def _shift_down(x, halo, k):
    xr = pltpu.roll(x, k, 0)
    hr = pltpu.roll(halo, k, 0)
    rows = lax.broadcasted_iota(jnp.int32, halo.shape, 0)
    top = jnp.where(rows < k, hr, xr[:SUBLANES])
    return jnp.concatenate([top, xr[SUBLANES:]], axis=0)


def _shift_up(x, halo, k):
    tm = x.shape[0]
    xr = pltpu.roll(x, tm - k, 0)
    hr = pltpu.roll(halo, SUBLANES - k, 0)
    rows = lax.broadcasted_iota(jnp.int32, halo.shape, 0)
    bot = jnp.where(rows >= SUBLANES - k, hr, xr[tm - SUBLANES:])
    return jnp.concatenate([xr[:tm - SUBLANES], bot], axis=0)


def _conv_tiles(S, F):
    return _tile(S, (ROW_TILE, 256, 128)), _tile(F, (512, 256, 128))


def _conv_in_specs(tm, tn, F):
    r8 = tm // SUBLANES
    nf = F // tn
    specs = []
    for half in range(2):
        specs.append(pl.BlockSpec((tm, tn), functools.partial(lambda j, i, o: (i, o + j), o=half * nf)))
        specs.append(pl.BlockSpec((SUBLANES, tn), functools.partial(lambda j, i, o: (jnp.maximum(i * r8 - 1, 0), o + j), o=half * nf)))
    for half in range(2):
        specs.append(pl.BlockSpec((3, tn), functools.partial(lambda j, i, o: (0, o + j), o=half * nf)))
        specs.append(pl.BlockSpec((1, tn), functools.partial(lambda j, i, o: (0, o + j), o=half * nf)))
    return specs


def _conv_apply(x, halo, w, b, first):
    halo = jnp.where(first, 0.0, halo)
    x1 = _shift_down(x, halo, 1)
    x2 = _shift_down(x, halo, 2)
    return b + x2 * w[0:1, :] + x1 * w[1:2, :] + x * w[2:3, :], x1, x2


def _glu_fwd(up, cw, cb, name):
    S, F2 = up.shape
    F = F2 // 2
    tm, tn = _conv_tiles(S, F)

    def body(ug, hg, uv, hv, wg, bgr, wv, bvr, a_ref):
        first = pl.program_id(1) == 0
        cg, _, _ = _conv_apply(ug[...], hg[...], wg[...], bgr[...], first)
        cv, _, _ = _conv_apply(uv[...], hv[...], wv[...], bvr[...], first)
        a_ref[...] = (cg * _sigmoid(cg) * cv).astype(BF16)

    return pl.pallas_call(
        body, name=name, grid=(F // tn, S // tm), in_specs=_conv_in_specs(tm, tn, F),
        out_specs=pl.BlockSpec((tm, tn), lambda j, i: (i, j)),
        out_shape=jax.ShapeDtypeStruct((S, F), BF16),
        compiler_params=_params(("parallel", "parallel")),
    )(up, up, up, up, cw, cb, cw, cb)


def _glu_bwd(up, cw, cb, da, name):
    S, F2 = up.shape
    F = F2 // 2
    tm, tn = _conv_tiles(S, F)

    def body(ug, hg, uv, hv, wg, bgr, wv, bvr, da_ref, dg_ref, dv_ref, sg_ref, sv_ref):
        i = pl.program_id(1)

        @pl.when(i == 0)
        def _():
            sg_ref[...] = jnp.zeros_like(sg_ref)
            sv_ref[...] = jnp.zeros_like(sv_ref)

        first = i == 0
        xg, xv = ug[...], uv[...]
        cg, xg1, xg2 = _conv_apply(xg, hg[...], wg[...], bgr[...], first)
        cv, xv1, xv2 = _conv_apply(xv, hv[...], wv[...], bvr[...], first)
        dav = da_ref[...]
        sg = _sigmoid(cg)
        dcv = dav * cg * sg
        dcg = dav * cv * sg * (1.0 + cg * (1.0 - sg))
        dg_ref[...] = dcg
        dv_ref[...] = dcv
        rows = lax.broadcasted_iota(jnp.int32, (SUBLANES, tn), 0)

        def stats(dc, x, x1, x2):
            acc = jnp.zeros((SUBLANES, tn), F32)
            for r, val in enumerate((dc * x2, dc * x1, dc * x, dc)):
                acc += jnp.where(rows == r, jnp.sum(val, axis=0, keepdims=True), 0.0)
            return acc

        sg_ref[...] += stats(dcg, xg, xg1, xg2)
        sv_ref[...] += stats(dcv, xv, xv1, xv2)

    tile = pl.BlockSpec((tm, tn), lambda j, i: (i, j))
    stat = pl.BlockSpec((SUBLANES, tn), lambda j, i: (0, j))
    return pl.pallas_call(
        body, name=name, grid=(F // tn, S // tm), in_specs=_conv_in_specs(tm, tn, F) + [tile],
        out_specs=[tile, tile, stat, stat],
        out_shape=[jax.ShapeDtypeStruct((S, F), F32), jax.ShapeDtypeStruct((S, F), F32),
                   jax.ShapeDtypeStruct((SUBLANES, F), F32), jax.ShapeDtypeStruct((SUBLANES, F), F32)],
        compiler_params=_params(("parallel", "arbitrary")),
    )(up, up, up, up, cw, cb, cw, cb, da)


def _conv_bwd(dcg, dcv, w, name):
    S, F = dcg.shape
    tm, tn = _conv_tiles(S, F)
    r8 = tm // SUBLANES
    ni = S // tm
    nf = F // tn

    def body(g_ref, gh_ref, v_ref, vh_ref, w_ref, o_ref):
        gate = pl.program_id(0) == 0
        x = jnp.where(gate, g_ref[...], v_ref[...])
        halo = jnp.where(gate, gh_ref[...], vh_ref[...])
        halo = jnp.where(pl.program_id(2) == ni - 1, 0.0, halo)
        wv = w_ref[...]
        o_ref[...] = (x * wv[2:3, :] + _shift_up(x, halo, 1) * wv[1:2, :] + _shift_up(x, halo, 2) * wv[0:1, :]).astype(BF16)

    def tile(half):
        return pl.BlockSpec((tm, tn), lambda h, j, i: (jnp.where(h == half, i, 0), jnp.where(h == half, j, 0)))

    def below(half):
        return pl.BlockSpec((SUBLANES, tn), lambda h, j, i: (
            jnp.where(h == half, jnp.minimum((i + 1) * r8, S // SUBLANES - 1), 0), jnp.where(h == half, j, 0)))

    return pl.pallas_call(
        body, name=name, grid=(2, nf, ni),
        in_specs=[tile(0), below(0), tile(1), below(1), pl.BlockSpec((3, tn), lambda h, j, i: (0, h * nf + j))],
        out_specs=pl.BlockSpec((tm, tn), lambda h, j, i: (i, h * nf + j)),
        out_shape=jax.ShapeDtypeStruct((S, 2 * F), BF16),
        compiler_params=_params(("parallel", "parallel", "parallel")),
    )(dcg, dcg, dcv, dcv, w)


def _adamw(slot_list, own_list, me, w, m, v, name):
    L = len(slot_list)
    P, K, C = slot_list[0].shape
    tr = _tile(K, (256, 128, 64, 32, 16))
    while tr * C * 4 > (1 << 20) and tr % 32 == 0:
        tr //= 2
    nb = K // tr
    has_own = own_list is not None

    def body(me_ref, *refs):
        s_refs = refs[:L]
        o_refs = refs[L:2 * L] if has_own else None
        w_ref, m_ref, v_ref, g_ref, d_ref, nm_ref, nv_ref = refs[L * (1 + has_own):]
        layer = pl.program_id(0)
        g = None
        for l in range(L):
            gl = None
            for p in range(P):
                term = s_refs[l][p].astype(F32)
                if has_own:
                    term = jnp.where(me_ref[0] == p, o_refs[l][0].astype(F32), term)
                gl = term if gl is None else gl + term
            g = gl if g is None else jnp.where(layer == l, gl, g)
        nm = ADAM_B1 * m_ref[...] + (1.0 - ADAM_B1) * g
        nv = ADAM_B2 * v_ref[...] + (1.0 - ADAM_B2) * (g * g)
        m_hat = nm / (1.0 - ADAM_B1 ** ADAM_STEP)
        v_hat = nv / (1.0 - ADAM_B2 ** ADAM_STEP)
        g_ref[...] = g
        d_ref[...] = -ADAM_LR * (m_hat / (jnp.sqrt(v_hat) + ADAM_EPS) + ADAM_WD * w_ref[...])
        nm_ref[...] = nm
        nv_ref[...] = nv

    blk = pl.BlockSpec((tr, C), lambda li, i, me_ref: (li * nb + i, 0))
    specs = [pl.BlockSpec((P, tr, C), functools.partial(lambda li, i, me_ref, l: (0, jnp.where(li == l, i, 0), 0), l=l))
             for l in range(L)]
    if has_own:
        specs += [pl.BlockSpec((1, tr, C), functools.partial(lambda li, i, me_ref, l: (me_ref[0], jnp.where(li == l, i, 0), 0), l=l))
                  for l in range(L)]
    return pl.pallas_call(
        body, name=name,
        grid_spec=pltpu.PrefetchScalarGridSpec(num_scalar_prefetch=1, grid=(L, nb), in_specs=specs + [blk, blk, blk],
                                               out_specs=[blk] * 4),
        out_shape=[jax.ShapeDtypeStruct((L * K, C), F32)] * 4,
        compiler_params=_params(("arbitrary", "arbitrary")),
    )(me, *slot_list, *(own_list if has_own else []), w, m, v)


def _exchange(srcs, scatter, name):
    na = len(srcs)
    out_shape = [jax.ShapeDtypeStruct(s.shape if scatter else (N_DEV,) + s.shape, s.dtype) for s in srcs]

    def body(*refs):
        src_refs, out_refs = refs[:na], refs[na:2 * na]
        send_sems, recv_sems, local_sems = refs[2 * na:]
        x, y, c = lax.axis_index("x"), lax.axis_index("y"), lax.axis_index("c")
        me = 4 * x + 2 * y + c

        def flip(v, bit):
            return 1 - v if bit else v

        def peer(k):
            return (flip(x, (k >> 2) & 1), flip(y, (k >> 1) & 1), flip(c, k & 1))

        def peer_index(k):
            px, py, pc = peer(k)
            return 4 * px + 2 * py + pc

        def remote(a, k):
            src = src_refs[a].at[peer_index(k)] if scatter else src_refs[a]
            return pltpu.make_async_remote_copy(
                src_ref=src, dst_ref=out_refs[a].at[me], send_sem=send_sems.at[a, k - 1], recv_sem=recv_sems.at[a, k - 1],
                device_id=peer(k), device_id_type=pl.DeviceIdType.MESH)

        def arrival(a, k):
            src = src_refs[a].at[me] if scatter else src_refs[a]
            return pltpu.make_async_remote_copy(
                src_ref=src, dst_ref=out_refs[a].at[peer_index(k)], send_sem=send_sems.at[a, k - 1],
                recv_sem=recv_sems.at[a, k - 1], device_id=peer(k), device_id_type=pl.DeviceIdType.MESH)

        own = [pltpu.make_async_copy(src_refs[a].at[me] if scatter else src_refs[a], out_refs[a].at[me], local_sems.at[a])
               for a in range(na)]
        sends = [remote(a, k) for k in range(1, N_DEV) for a in range(na)]
        for cp in own + sends:
            cp.start()
        for k in range(1, N_DEV):
            for a in range(na):
                arrival(a, k).wait_recv()
        for cp in sends:
            cp.wait_send()
        for cp in own:
            cp.wait()

    any_spec = pl.BlockSpec(memory_space=pl.ANY)
    return pl.pallas_call(
        body, name=name, in_specs=[any_spec] * na, out_specs=[any_spec] * na, out_shape=out_shape,
        scratch_shapes=[pltpu.SemaphoreType.DMA((na, N_DEV - 1)), pltpu.SemaphoreType.DMA((na, N_DEV - 1)),
                        pltpu.SemaphoreType.DMA((na,))],
    )(*srcs)


_HBM = pl.BlockSpec(memory_space=pltpu.HBM)
_SEM = pl.BlockSpec(memory_space=pltpu.SEMAPHORE)
_ANY = pl.BlockSpec(memory_space=pl.ANY)


def _peers():
    x, y, c = lax.axis_index("x"), lax.axis_index("y"), lax.axis_index("c")

    def flip(v, bit):
        return 1 - v if bit else v

    def peer(k):
        return (flip(x, (k >> 2) & 1), flip(y, (k >> 1) & 1), flip(c, k & 1))

    def peer_index(k):
        px, py, pc = peer(k)
        return 4 * px + 2 * py + pc

    return 4 * x + 2 * y + c, peer, peer_index


def _split_copy(src_refs, land_refs, send_sems, recv_sems, scatter, a, k, outgoing):
    me, peer, peer_index = _peers()
    if outgoing:
        src = src_refs[a].at[peer_index(k)] if scatter else src_refs[a]
        dst = land_refs[a].at[me]
    else:
        src = src_refs[a].at[me] if scatter else src_refs[a]
        dst = land_refs[a].at[peer_index(k)]
    pair = a * (N_DEV - 1) + k - 1
    return pltpu.make_async_remote_copy(src_ref=src, dst_ref=dst, send_sem=send_sems.at[pair],
                                        recv_sem=recv_sems.at[pair], device_id=peer(k),
                                        device_id_type=pl.DeviceIdType.MESH)


def _exchange_start(srcs, scatter, after, name):
    na = len(srcs)
    land_shapes = [s.shape if scatter else (N_DEV,) + s.shape for s in srcs]
    has_after = after is not None

    def body(*refs):
        src_refs, land_refs = refs[:na], refs[na:2 * na]
        send_sems, recv_sems = refs[2 * na + has_after], refs[2 * na + has_after + 1]
        token = refs[-1]
        for k in range(1, N_DEV):
            for a in range(na):
                _split_copy(src_refs, land_refs, send_sems, recv_sems, scatter, a, k, True).start()
        token[...] = jnp.zeros_like(token)

    sems = pltpu.SemaphoreType.DMA((na * (N_DEV - 1),))
    out_shape = ([sems, sems] + [pltpu.HBM(s.shape, s.dtype) for s in srcs]
                 + [pltpu.HBM(shp, s.dtype) for shp, s in zip(land_shapes, srcs)] + [jax.ShapeDtypeStruct((SUBLANES, 128), F32)])
    args = [pltpu.with_memory_space_constraint(s, pltpu.HBM) for s in srcs]
    args += [pltpu.with_memory_space_constraint(lax.empty(shp, s.dtype), pltpu.HBM) for shp, s in zip(land_shapes, srcs)]
    if has_after:
        args.append(after)
    res = pl.pallas_call(
        body, name=name, in_specs=[_HBM] * (2 * na) + [_ANY] * has_after,
        out_specs=[_SEM, _SEM] + [_HBM] * (2 * na) + [pl.BlockSpec(memory_space=pltpu.VMEM)], out_shape=out_shape,
        input_output_aliases={i: 2 + i for i in range(2 * na)},
        compiler_params=pltpu.CompilerParams(has_side_effects=pltpu.SideEffectType.DATAFLOW_SIDE_EFFECTING),
    )(*args)
    handle = dict(send=res[0], recv=res[1], srcs=list(res[2:2 + na]), lands=list(res[2 + na:2 + 2 * na]), scatter=scatter)
    return handle, res[-1]


def _exchange_wait(handle, after, name):
    srcs, lands, scatter = handle["srcs"], handle["lands"], handle["scatter"]
    na = len(srcs)

    def body(*refs):
        src_refs, land_refs = refs[:na], refs[na:2 * na]
        send_sems, recv_sems = refs[2 * na], refs[2 * na + 1]
        for k in range(1, N_DEV):
            for a in range(na):
                _split_copy(src_refs, land_refs, send_sems, recv_sems, scatter, a, k, True).wait_send()
                _split_copy(src_refs, land_refs, send_sems, recv_sems, scatter, a, k, False).wait_recv()

    res = pl.pallas_call(
        body, name=name, in_specs=[_HBM] * (2 * na) + [_SEM, _SEM, _ANY], out_specs=[_HBM] * (2 * na),
        out_shape=[pltpu.HBM(s.shape, s.dtype) for s in srcs] + [pltpu.HBM(s.shape, s.dtype) for s in lands],
        input_output_aliases={i: i for i in range(2 * na)},
        compiler_params=pltpu.CompilerParams(has_side_effects=pltpu.SideEffectType.DATAFLOW_SIDE_EFFECTING),
    )(*srcs, *lands, handle["send"], handle["recv"], after)
    return list(res[:na]), list(res[na:])


def _layout(D):
    aq, akv = SWA_Q_HEADS * SWA_HEAD_DIM, SWA_KV_HEADS * SWA_HEAD_DIM
    w = SGU_GROUPS * SGU_DIM
    return aq, akv, w


class _Seg:
    def __init__(self, D, rq, rkv):
        aq, akv, w = _layout(D)
        src = {}
        o = 0
        for nm, wd in (("qa", aq), ("ka", akv), ("va", akv), ("cq", rq), ("ckv", rkv), ("kr", MLA_ROPE), ("hu", w), ("hv", w),
                       ("g", 3 * D)):
            src[nm] = (o, wd)
            o += wd
        self.n_in = o
        self.order = ("g", "qa", "hu", "hv", "cq", "ckv", "ka", "va", "kr")
        self.src = src
        self.off = {}
        o = 0
        for nm in self.order:
            self.off[nm] = o
            o += src[nm][1]
        self.width = {nm: src[nm][1] for nm in self.order}
        self.n_pad = -(-o // 1536) * 1536 if o > 1536 else -(-o // 512) * 512
        self.used = o

    def permute(self, w):
        parts = [w[:, self.src[nm][0]:self.src[nm][0] + self.src[nm][1]] for nm in self.order]
        parts.append(jnp.zeros((w.shape[0], self.n_pad - self.used), w.dtype))
        return jnp.concatenate(parts, axis=1)

    def unpermute(self, w):
        names = sorted(self.order, key=lambda nm: self.src[nm][0])
        return jnp.concatenate([w[:, self.off[nm]:self.off[nm] + self.width[nm]] for nm in names], axis=1)


def _uq_permute(w):
    R = w.shape[0]
    H, half = MLA_HEADS, MLA_ROPE // 2
    w3 = w.reshape(R, H, MLA_NOPE + MLA_ROPE)
    return jnp.concatenate([w3[:, :, :MLA_NOPE].reshape(R, H * MLA_NOPE),
                            w3[:, :, MLA_NOPE:MLA_NOPE + half].reshape(R, H * half),
                            w3[:, :, MLA_NOPE + half:].reshape(R, H * half)], axis=1)


def _uq_unpermute(w):
    R = w.shape[0]
    H, half = MLA_HEADS, MLA_ROPE // 2
    n = w[:, :H * MLA_NOPE].reshape(R, H, MLA_NOPE)
    r1 = w[:, H * MLA_NOPE:H * (MLA_NOPE + half)].reshape(R, H, half)
    r2 = w[:, H * (MLA_NOPE + half):].reshape(R, H, half)
    return jnp.concatenate([n, r1, r2], axis=2).reshape(R, H * (MLA_NOPE + MLA_ROPE))


def _ukv_permute(w):
    R = w.shape[0]
    w3 = w.reshape(R, MLA_HEADS, MLA_NOPE + MLA_V)
    return jnp.concatenate([w3[:, :, :MLA_NOPE].reshape(R, -1), w3[:, :, MLA_NOPE:].reshape(R, -1)], axis=1)


def _ukv_unpermute(w):
    R = w.shape[0]
    H = MLA_HEADS
    k = w[:, :H * MLA_NOPE].reshape(R, H, MLA_NOPE)
    v = w[:, H * MLA_NOPE:].reshape(R, H, MLA_V)
    return jnp.concatenate([k, v], axis=2).reshape(R, H * (MLA_NOPE + MLA_V))


def _heads(a, d):
    S = a.shape[0]
    return a.reshape(S, MLA_HEADS, d).transpose(1, 0, 2)


def _unheads(a):
    H, S, d = a.shape
    return a.transpose(1, 0, 2).reshape(S, H * d)


GROUPS = {"a": ("w_in",), "b": ("w_uq", "w_ukv", "w_proj_a", "w_proj_b", "w_proj_c", "w_o", "b_gate"),
          "c": ("w_up", "w_down", "conv_w")}


def _layer_fwd(l, x, x16, fetch, P, cs, sn, seg, alpha):
    S, D = x.shape
    H, half = MLA_HEADS, MLA_ROPE // 2
    off = seg.off
    nm = lambda s: f"l{l}_{s}"
    sv = {"x16": x16}
    W = {"w_in": seg.permute(fetch(l, "a", x16)["w_in"])}
    h = _mm(x16, W["w_in"], "nn", [(F32, "n")], nm("h"))[0]
    sv["h"] = h
    ya, lse_a = _swa_fwd(h, off["qa"], off["ka"], off["va"], P["sinks"], nm("swa_fwd"))
    cqn, rq = _rms_fwd(h, off["cq"], seg.width["cq"], P["q_norm_g"], nm("rmsq_fwd"))
    ckvn, rkv = _rms_fwd(h, off["ckv"], seg.width["ckv"], P["kv_norm_g"], nm("rmskv_fwd"))
    W.update(fetch(l, "b", cqn))
    W["w_uq"] = _uq_permute(W["w_uq"])
    W["w_ukv"] = _ukv_permute(W["w_ukv"])
    qf = _mm(cqn, W["w_uq"], "nn", [(F32, "n")], nm("uq"))[0]
    kvf = _mm(ckvn, W["w_ukv"], "nn", [(BF16, "n")], nm("ukv"))[0]
    cs_h, sn_h = jnp.tile(cs, (1, H)), jnp.tile(sn, (1, H))
    qy1, qy2 = _rope((qf, H * MLA_NOPE), (qf, H * MLA_NOPE + H * half), cs_h, sn_h, H * half, False, nm("ropeq_fwd"))
    kr = h[:, off["kr"]:off["kr"] + MLA_ROPE]
    ky1, ky2 = _rope((kr[:, :half], 0), (kr[:, half:], 0), cs, sn, half, False, nm("ropek_fwd"))
    qh = jnp.concatenate([qf[:, :H * MLA_NOPE].astype(BF16).reshape(S, H, MLA_NOPE), qy1.reshape(S, H, half),
                          qy2.reshape(S, H, half)], axis=2).transpose(1, 0, 2)
    kh = jnp.concatenate([kvf[:, :H * MLA_NOPE].reshape(S, H, MLA_NOPE),
                          jnp.broadcast_to(ky1[:, None, :], (S, H, half)),
                          jnp.broadcast_to(ky2[:, None, :], (S, H, half))], axis=2).transpose(1, 0, 2)
    vh = _heads(kvf[:, H * MLA_NOPE:], MLA_V)
    ob, lse_b = _mla_fwd(qh, kh.transpose(0, 2, 1), vh, nm("mla_fwd"))
    yb = _unheads(ob).astype(BF16)
    w16 = jnp.where(jnp.tril(jnp.ones((SGU_CHUNK, SGU_CHUNK), bool))[None], P["sgu_w"], 0.0).astype(BF16)
    bt = P["sgu_b"].T
    yc = _sgu_fwd(h, off["hu"], off["hv"], P["sgu_ln_g"], P["sgu_ln_b"], w16, bt, nm("sgu_fwd"))
    merged, z = _merge_fwd([ya, yb, yc], [W["w_proj_a"], W["w_proj_b"], W["w_proj_c"]], h, W["b_gate"], nm("merge_fwd"))
    x1, x1_16, xh1, rs1 = _mm_ln(merged, W["w_o"], x, P["ln1_g"], P["ln1_b"], alpha, nm("wo_ln1"))
    W.update(fetch(l, "c", x1_16))
    up = _mm(x1_16, W["w_up"], "nn", [(F32, "n")], nm("up"))[0]
    a = _glu_fwd(up, W["conv_w"], P["conv_b"], nm("glu_fwd"))
    x2, x2_16, xh2, rs2 = _mm_ln(a, W["w_down"], x1, P["ln2_g"], P["ln2_b"], alpha, nm("down_ln2"))
    sv.update(W=W, ya=ya, lse_a=lse_a, cqn=cqn, rq=rq, ckvn=ckvn, rkv=rkv, qh=qh, kh=kh, vh=vh, ob=ob, lse_b=lse_b, yb=yb,
              w16=w16, bt=bt, yc=yc, merged=merged, z=z, x1_16=x1_16, xh1=xh1, rs1=rs1, up=up, a=a, xh2=xh2, rs2=rs2,
              cs_h=cs_h, sn_h=sn_h)
    return x2, x2_16, sv


def _dw_chunks(k, a, dy, name, post=None):
    n = dy.shape[1]
    if k not in ROW_SHARDED and post is None and (n // N_DEV) % 128 == 0:
        return _mm(a, dy, "tn", [(BF16, "n")], name, chunk=n // N_DEV)[0]
    g = _mm(a, dy, "tn", [(BF16, "n")], name)[0]
    return _to_chunks(k, g if post is None else post(g))


def _after(arr, token):
    return arr if token is None else arr + token[0:1, 0:1].astype(arr.dtype)


def _layer_bwd(l, dx2, sv, P, cs, sn, seg, alpha, emit):
    S, D = dx2.shape
    H, half = MLA_HEADS, MLA_ROPE // 2
    off = seg.off
    h, W = sv["h"], sv["W"]
    nm = lambda s: f"l{l}_{s}"
    g = {}
    dr2, dr2_16, g["ln2_g"], g["ln2_b"] = _ln_bwd(dx2, sv["xh2"], sv["rs2"], P["ln2_g"], nm("ln2_bwd"))
    g["w_down"] = _dw_chunks("w_down", sv["a"], dr2_16, nm("dw_down"))
    da = _mm(dr2_16, W["w_down"], "nt", [(F32, "n")], nm("da"))[0]
    dcg, dcv, st_g, st_v = _glu_bwd(sv["up"], W["conv_w"], P["conv_b"], da, nm("glu_bwd"))
    F = dcg.shape[1]
    g["conv_w"] = _to_chunks("conv_w", jnp.concatenate([st_g[0:3], st_v[0:3]], axis=1))
    g["conv_b"] = jnp.concatenate([st_g[3:4], st_v[3:4]], axis=1)
    dup = _conv_bwd(dcg, dcv, W["conv_w"], nm("conv_bwd"))
    g["w_up"] = _dw_chunks("w_up", sv["x1_16"], dup, nm("dw_up"))
    token = emit(l, "c", {k: g.pop(k) for k in GROUPS["c"]})
    dx1 = _mm_axpy(dup, W["w_up"], "nt", dr2, alpha, nm("dx1"), dep=token)
    dr1, dr1_16, g["ln1_g"], g["ln1_b"] = _ln_bwd(dx1, sv["xh1"], sv["rs1"], P["ln1_g"], nm("ln1_bwd"))
    g["w_o"] = _dw_chunks("w_o", sv["merged"], dr1_16, nm("dw_o"))
    dmerged = _mm(dr1_16, W["w_o"], "nt", [(F32, "n")], nm("dmerged"))[0]
    dz, dlog, dbg = _merge_bwd(dmerged, sv["z"], h, W["b_gate"], nm("merge_bwd"))
    g["b_gate"] = _to_chunks("b_gate", dbg[0:3])
    g["w_proj_a"] = _dw_chunks("w_proj_a", sv["ya"], dz[0], nm("dw_pa"))
    g["w_proj_b"] = _dw_chunks("w_proj_b", sv["yb"], dz[1], nm("dw_pb"))
    g["w_proj_c"] = _dw_chunks("w_proj_c", sv["yc"], dz[2], nm("dw_pc"))
    dya = _mm(dz[0], W["w_proj_a"], "nt", [(F32, "n")], nm("dya"))[0]
    dyb = _mm(dz[1], W["w_proj_b"], "nt", [(F32, "n")], nm("dyb"))[0]
    dyc = _mm(dz[2], W["w_proj_c"], "nt", [(F32, "n")], nm("dyc"))[0]
    dhu, dhv, g["sgu_w"], db_s, g["sgu_ln_g"], g["sgu_ln_b"] = _sgu_bwd(
        h, off["hu"], off["hv"], P["sgu_ln_g"], P["sgu_ln_b"], sv["w16"], sv["bt"], dyc, nm("sgu_bwd"))
    g["sgu_b"] = db_s[:, :SGU_GROUPS].T
    dqa, dka, dva, dsk = _swa_bwd(h, off["qa"], off["ka"], off["va"], P["sinks"], dya, sv["lse_a"], nm("swa_bwd"))
    g["sinks"] = dsk[0, :SWA_Q_HEADS]
    dob = _heads(dyb, MLA_V)
    delta = _rowdot(dob, sv["ob"], nm("mla_delta"))
    dqh, dkh, dvh = _mla_bwd(sv["qh"], sv["kh"], sv["vh"], dob.astype(BF16), sv["lse_b"], delta, nm("mla_bwd"))
    n0, n1 = MLA_NOPE, MLA_NOPE + half
    dqx1, dqx2 = _rope((_unheads(dqh[:, :, n0:n1]), 0), (_unheads(dqh[:, :, n1:]), 0), sv["cs_h"], sv["sn_h"], H * half, True,
                       nm("ropeq_bwd"))
    dqf = jnp.concatenate([_unheads(dqh[:, :, :n0]).astype(BF16), dqx1, dqx2], axis=1)
    dkvf = jnp.concatenate([_unheads(dkh[:, :, :n0]), _unheads(dvh)], axis=1).astype(BF16)
    dk1, dk2 = _rope(dkh[:, :, n0:n1], dkh[:, :, n1:], cs, sn, half, True, nm("ropek_bwd"))
    g["w_uq"] = _dw_chunks("w_uq", sv["cqn"], dqf, nm("dw_uq"), _uq_unpermute)
    g["w_ukv"] = _dw_chunks("w_ukv", sv["ckvn"], dkvf, nm("dw_ukv"), _ukv_unpermute)
    token = emit(l, "b", {k: g.pop(k) for k in GROUPS["b"]})
    dcqn = _mm(dqf, W["w_uq"], "nt", [(F32, "n")], nm("dcqn"), dep=token)[0]
    dckvn = _mm(dkvf, W["w_ukv"], "nt", [(F32, "n")], nm("dckvn"), dep=token)[0]
    dcq, g["q_norm_g"] = _rms_bwd(dcqn, h, off["cq"], seg.width["cq"], sv["rq"], P["q_norm_g"], nm("rmsq_bwd"))
    dckv, g["kv_norm_g"] = _rms_bwd(dckvn, h, off["ckv"], seg.width["ckv"], sv["rkv"], P["kv_norm_g"], nm("rmskv_bwd"))
    parts = {"g": jnp.concatenate([dlog[0], dlog[1], dlog[2]], axis=1), "qa": dqa, "hu": dhu, "hv": dhv, "cq": dcq, "ckv": dckv,
             "ka": dka.astype(BF16), "va": dva.astype(BF16), "kr": jnp.concatenate([dk1, dk2], axis=1)}
    dh = jnp.concatenate([parts[k] for k in seg.order] + [jnp.zeros((S, seg.n_pad - seg.used), BF16)], axis=1)
    token = emit(l, "a", {"w_in": _dw_chunks("w_in", sv["x16"], dh, nm("dw_in"), seg.unpermute)})
    dx = _mm_axpy(dh, W["w_in"], "nt", dr1, alpha, nm("dx"), dep=token)
    return dx, g


BIG = ("w_in", "w_uq", "w_ukv", "w_proj_a", "w_proj_b", "w_proj_c", "w_o", "w_up", "w_down")
ROW_SHARDED = ("w_proj_b", "w_o", "w_down")
SHARDED_F32 = ("b_gate", "conv_w")
REPLICATED = ("sinks", "q_norm_g", "kv_norm_g", "sgu_ln_g", "sgu_ln_b", "sgu_w", "sgu_b", "ln1_g", "ln1_b", "conv_b", "ln2_g",
              "ln2_b")
WEIGHTS = ("w_in", "b_gate", "sinks", "q_norm_g", "kv_norm_g", "w_uq", "w_ukv", "sgu_ln_g", "sgu_ln_b", "sgu_w", "sgu_b",
           "w_proj_a", "w_proj_b", "w_proj_c", "w_o", "ln1_g", "ln1_b", "w_up", "conv_w", "conv_b", "w_down", "ln2_g", "ln2_b")


def _step_local(x, positions, target, small, rq, rkv, fetch, emit, token=None):
    S, D = x.shape
    L = small["sinks"].shape[0]
    alpha = (2 * L) ** 0.25
    seg = _Seg(D, rq, rkv)
    inv_freq = ROPE_THETA ** (-jnp.arange(0, MLA_ROPE, 2, dtype=F32) / MLA_ROPE)
    ang = positions.astype(F32)[:, None] * inv_freq
    cs, sn = jnp.cos(ang), jnp.sin(ang)
    rows = ("q_norm_g", "kv_norm_g", "sgu_ln_g", "sgu_ln_b", "ln1_g", "ln1_b", "conv_b", "ln2_g", "ln2_b")
    layers = [{k: small[k][l].reshape(1, -1) if k in rows else small[k][l] for k in small} for l in range(L)]
    saved = []
    x16 = _after(x, token).astype(BF16)
    for l in range(L):
        x, x16, sv = _layer_fwd(l, x, x16, fetch, layers[l], cs, sn, seg, alpha)
        saved.append(sv)
    loss, dx = _loss(x, target, "loss")
    grads = [None] * L
    for l in reversed(range(L)):
        dx, grads[l] = _layer_bwd(l, dx, saved[l], layers[l], cs, sn, seg, alpha, emit)
    out = {k: jnp.stack([grads[l][k].reshape(small[k].shape[1:]) for l in range(L)]) for k in small}
    return loss, dx, out


def _unshard(k, gathered):
    n, r, c = gathered.shape
    if k in ROW_SHARDED:
        return gathered.reshape(n * r, c)
    return gathered.transpose(1, 0, 2).reshape(r, n * c)


def _to_chunks(k, gfull):
    r, c = gfull.shape
    if k in ROW_SHARDED:
        return gfull.reshape(N_DEV, r // N_DEV, c)
    return gfull.reshape(r, N_DEV, c // N_DEV).transpose(1, 0, 2)


def _pack(arrs):
    P = arrs[0].shape[0]
    flat, sizes = [], []
    for a in arrs:
        f = a.reshape(P, -1)
        n = f.shape[1]
        pad = -n % (SUBLANES * 128)
        flat.append(jnp.pad(f, ((0, 0), (0, pad))))
        sizes.append((n, n + pad))
    return jnp.concatenate(flat, axis=1).reshape(P, -1, 128), sizes


def _unpack(packed, sizes, shapes):
    flat = packed.reshape(-1)
    out, o = [], 0
    for (n, npad), shp in zip(sizes, shapes):
        out.append(flat[o:o + n].reshape(shp))
        o += npad
    return out


def kernel(x, positions, w_in, b_gate, sinks, q_norm_g, kv_norm_g, w_uq, w_ukv, sgu_ln_g, sgu_ln_b, sgu_w, sgu_b, w_proj_a, w_proj_b, w_proj_c, w_o, ln1_g, ln1_b, w_up, conv_w, conv_b, w_down, ln2_g, ln2_b, loss_target, m_w_in, m_b_gate, m_sinks, m_q_norm_g, m_kv_norm_g, m_w_uq, m_w_ukv, m_sgu_ln_g, m_sgu_ln_b, m_sgu_w, m_sgu_b, m_w_proj_a, m_w_proj_b, m_w_proj_c, m_w_o, m_ln1_g, m_ln1_b, m_w_up, m_conv_w, m_conv_b, m_w_down, m_ln2_g, m_ln2_b, v_w_in, v_b_gate, v_sinks, v_q_norm_g, v_kv_norm_g, v_w_uq, v_w_ukv, v_sgu_ln_g, v_sgu_ln_b, v_sgu_w, v_sgu_b, v_w_proj_a, v_w_proj_b, v_w_proj_c, v_w_o, v_ln1_g, v_ln1_b, v_w_up, v_conv_w, v_conv_b, v_w_down, v_ln2_g, v_ln2_b):
    given = dict(locals())
    w = {k: given[k] for k in WEIGHTS}
    mom = {k: given["m_" + k] for k in WEIGHTS}
    var = {k: given["v_" + k] for k in WEIGHTS}

    L = w_in.shape[0]
    order = [(l, grp) for l in range(L) for grp in ("a", "b", "c")]

    gathers, token = {}, None
    for l, grp in order:
        srcs = [w[k][l].astype(BF16) if k in BIG else w[k][l] for k in GROUPS[grp]]
        gathers[l, grp], token = _exchange_start(srcs, False, token, f"gather_start_l{l}{grp}")

    me = 4 * lax.axis_index("x") + 2 * lax.axis_index("y") + lax.axis_index("c")
    mine = (jnp.arange(N_DEV) == me)[:, None, None]

    def fetch(l, grp, after):
        srcs, lands = _exchange_wait(gathers[l, grp], after, f"gather_wait_l{l}{grp}")
        return {k: _unshard(k, jnp.where(mine, srcs[i][None], lands[i])) for i, k in enumerate(GROUPS[grp])}

    scatters = {}

    def emit(l, grp, chunks):
        scatters[l, grp], tok = _exchange_start([chunks[k] for k in GROUPS[grp]], True, None, f"scatter_start_l{l}{grp}")
        return tok

    small = {k: w[k] for k in REPLICATED}
    loss, grad_x, g = _step_local(x[0], positions[0], loss_target[0], small, w_uq.shape[1], w_ukv.shape[1], fetch, emit, token)
    loss = lax.psum(loss[0, 0], AXES)

    slots, own = {}, {}
    for l, grp in [(l, grp) for grp in ("c", "b", "a") for l in reversed(range(L))]:
        srcs, lands = _exchange_wait(scatters[l, grp], grad_x, f"scatter_wait_l{l}{grp}")
        for i, k in enumerate(GROUPS[grp]):
            slots[k, l], own[k, l] = lands[i], srcs[i]
    me1 = me.astype(jnp.int32).reshape(1)
    res = {}
    for k in [k for grp in ("c", "b", "a") for k in GROUPS[grp]]:
        shp = w[k].shape
        r2 = (shp[0] * shp[1], shp[2])
        if k in BIG:
            slot_list, own_list = [slots[k, l] for l in range(L)], [own[k, l] for l in range(L)]
        else:
            slot_list = [jnp.concatenate([slots[k, l] for l in range(L)], axis=1)]
            own_list = [jnp.concatenate([own[k, l] for l in range(L)], axis=1)]
        outs = _adamw(slot_list, own_list, me1, w[k].reshape(r2), mom[k].reshape(r2), var[k].reshape(r2), "adamw_" + k)
        res[k] = [o.reshape(shp) for o in outs]

    packed, sizes = _pack([g[k][None] for k in REPLICATED])
    parts = _exchange([packed[0]], False, "gather_small_grads")[0]
    shapes = [w[k].shape for k in REPLICATED]
    pw, _ = _pack([w[k][None] for k in REPLICATED])
    pm, _ = _pack([mom[k][None] for k in REPLICATED])
    pv, _ = _pack([var[k][None] for k in REPLICATED])
    outs = _adamw([parts], None, me1, pw[0], pm[0], pv[0], "adamw_small")
    unpacked = [_unpack(o, sizes, shapes) for o in outs]
    for i, k in enumerate(REPLICATED):
        res[k] = [unpacked[j][i] for j in range(4)]

    return (loss, grad_x[None], *[res[k][0] for k in WEIGHTS], *[res[k][1] for k in WEIGHTS],
            *[res[k][2] for k in WEIGHTS], *[res[k][3] for k in WEIGHTS])
```

```python
import functools
import math

import jax
import jax.numpy as jnp
from jax import lax
from jax.experimental import pallas as pl
from jax.experimental.pallas import tpu as pltpu

F32 = jnp.float32
BF16 = jnp.bfloat16

SWA_Q_HEADS = 16
SWA_KV_HEADS = 2
SWA_HEAD_DIM = 64
SWA_BLOCK = 128
MLA_HEADS = 16
MLA_NOPE = 128
MLA_ROPE = 64
MLA_V = 128
SGU_GROUPS = 8
SGU_DIM = 128
SGU_CHUNK = 128
ROPE_THETA = 10000.0
EPS = 1e-5
MASK = -1e30
ADAM_LR = 0.001
ADAM_B1 = 0.9
ADAM_B2 = 0.999
ADAM_EPS = 1e-08
ADAM_WD = 0.01
ADAM_STEP = 10

N_DEV = 8
AXES = ("x", "y", "c")
VMEM_LIMIT = 56 * 1024 * 1024
MLA_TILE = 512
MLA_FWD_TILE = 1024
ROW_TILE = 512
MAX_TK = 2816
SUBLANES = 8


def _tile(n, prefs):
    for p in prefs:
        if n % p == 0:
            return p
    return n


def _params(sem):
    return pltpu.CompilerParams(dimension_semantics=sem, vmem_limit_bytes=VMEM_LIMIT)


def _cols(tm, width, off):
    assert off % width == 0, (off, width)
    blk = off // width
    return pl.BlockSpec((tm, width), lambda i, *_: (i, blk))


def _full(shape):
    nd = len(shape)
    return pl.BlockSpec(shape, lambda *_: (0,) * nd)


def _sigmoid(v):
    return 1.0 / (1.0 + jnp.exp(-v))


def _gelu(v):
    return 0.5 * v * (1.0 + lax.erf(v * (2.0 ** -0.5)))


def _gelu_grad(v):
    return 0.5 * (1.0 + lax.erf(v * (2.0 ** -0.5))) + v * jnp.exp(-0.5 * v * v) * (1.0 / math.sqrt(2.0 * math.pi))


_DIMS = {"nn": (((1,), (0,)), ((), ())), "nt": (((1,), (1,)), ((), ())), "tn": (((0,), (0,)), ((), ()))}


def _mm(a, b, mode, outs, name, *, extras=(), epilogue=None, full_n=False, dep=None, chunk=None):
    if mode == "nn":
        (M, K), (K2, N) = a.shape, b.shape
    elif mode == "nt":
        (M, K), (N, K2) = a.shape, b.shape
    else:
        (K, M), (K2, N) = a.shape, b.shape
    assert K == K2, (a.shape, b.shape, mode)
    tm = _tile(M, (1024, 512, 256, 128))
    tn = N if full_n else _tile(N, (1024, 768, 512, 384, 256, 128))
    if full_n:
        tm = _tile(M, (512, 256, 128))
    if chunk is not None:
        tn = chunk if chunk <= 1536 else _tile(chunk, (1024, 768, 512, 384, 256, 128))
        assert N % chunk == 0 and chunk % tn == 0 and tn % 128 == 0, (N, chunk, tn)
    max_tk = MAX_TK // 2 if full_n else MAX_TK
    tk = max(d for d in range(128, min(K, max_tk) + 1, 128) if K % d == 0) if K % 128 == 0 else K
    nk = K // tk
    if mode == "nn":
        a_spec = pl.BlockSpec((tm, tk), lambda i, j, k: (i, k))
        b_spec = pl.BlockSpec((tk, tn), lambda i, j, k: (k, j))
    elif mode == "nt":
        a_spec = pl.BlockSpec((tm, tk), lambda i, j, k: (i, k))
        b_spec = pl.BlockSpec((tn, tk), lambda i, j, k: (j, k))
    else:
        a_spec = pl.BlockSpec((tk, tm), lambda i, j, k: (k, i))
        b_spec = pl.BlockSpec((tk, tn), lambda i, j, k: (k, j))
    in_specs = [a_spec, b_spec]
    for arr, kind in extras:
        if kind == "tile":
            in_specs.append(pl.BlockSpec((tm, tn), lambda i, j, k: (i, j)))
        else:
            in_specs.append(pl.BlockSpec((1, tn), lambda i, j, k: (0, j)))
    out_specs, out_shape = [], []
    for dt, kind in outs:
        if kind == "n" and chunk is not None:
            per = chunk // tn
            out_specs.append(pl.BlockSpec((None, tm, tn), lambda i, j, k: (lax.div(j, per), i, lax.rem(j, per))))
            out_shape.append(jax.ShapeDtypeStruct((N // chunk, M, chunk), dt))
        elif kind == "n":
            out_specs.append(pl.BlockSpec((tm, tn), lambda i, j, k: (i, j)))
            out_shape.append(jax.ShapeDtypeStruct((M, N), dt))
        else:
            assert tn == N
            out_specs.append(pl.BlockSpec((tm, 1), lambda i, j, k: (i, 0)))
            out_shape.append(jax.ShapeDtypeStruct((M, 1), dt))
    ne, no = len(extras), len(outs)
    deps = []
    if dep is not None:
        in_specs.append(_full(dep.shape))
        deps = [dep]
    dims = _DIMS[mode]
    if epilogue is None:
        epilogue = lambda acc: (acc,) * no

    def body(*refs):
        a_ref, b_ref = refs[0], refs[1]
        ex = refs[2:2 + ne]
        out = refs[len(refs) - 1 - no:len(refs) - 1]
        acc = refs[-1]
        k = pl.program_id(2)
        part = lax.dot_general(a_ref[...].astype(BF16), b_ref[...].astype(BF16), dims, preferred_element_type=F32)

        def finish(total):
            res = epilogue(total, *[e[...] for e in ex])
            for o, r in zip(out, res):
                o[...] = r.astype(o.dtype)

        if nk == 1:
            finish(part)
            return

        @pl.when(k == 0)
        def _():
            acc[...] = part

        @pl.when((k > 0) & (k < nk - 1))
        def _():
            acc[...] += part

        @pl.when(k == nk - 1)
        def _():
            finish(acc[...] + part)

    res = pl.pallas_call(
        body, name=name, grid=(M // tm, N // tn, nk), in_specs=in_specs, out_specs=out_specs, out_shape=out_shape,
        scratch_shapes=[pltpu.VMEM((tm, tn), F32)],
        compiler_params=_params(("parallel", "parallel", "arbitrary")),
    )(a, b, *[e[0] for e in extras], *deps)
    return res


def _ln_epilogue(alpha):
    def epi(acc, x, g, b):
        r = alpha * x + acc
        mu = jnp.mean(r, axis=-1, keepdims=True)
        d = r - mu
        var = jnp.mean(d * d, axis=-1, keepdims=True)
        rstd = lax.rsqrt(var + EPS)
        xhat = d * rstd
        y = xhat * g + b
        return y, y, xhat, rstd
    return epi


def _mm_ln(a, w, x, g, b, alpha, name):
    return _mm(a, w, "nn", [(F32, "n"), (BF16, "n"), (F32, "n"), (F32, "1")], name,
               extras=[(x, "tile"), (g, "row"), (b, "row")], epilogue=_ln_epilogue(alpha), full_n=True)


def _mm_axpy(a, w, mode, r, alpha, name, dep=None):
    return _mm(a, w, mode, [(F32, "n")], name, extras=[(r, "tile")],
               epilogue=lambda acc, rv: (acc + alpha * rv,), dep=dep)[0]


def _ln_bwd(dy, xhat, rstd, g, name):
    S, D = dy.shape
    tm = _tile(S, (256, 128))

    def body(dy_ref, xh_ref, rs_ref, g_ref, dr_ref, dr16_ref, dg_ref, db_ref):
        @pl.when(pl.program_id(0) == 0)
        def _():
            dg_ref[...] = jnp.zeros_like(dg_ref)
            db_ref[...] = jnp.zeros_like(db_ref)

        dyv, xh = dy_ref[...], xh_ref[...]
        dxh = dyv * g_ref[...]
        m1 = jnp.mean(dxh, axis=-1, keepdims=True)
        m2 = jnp.mean(dxh * xh, axis=-1, keepdims=True)
        dr = rs_ref[...] * (dxh - m1 - xh * m2)
        dr_ref[...] = dr
        dr16_ref[...] = dr.astype(BF16)
        dg_ref[...] += jnp.sum(dyv * xh, axis=0, keepdims=True)
        db_ref[...] += jnp.sum(dyv, axis=0, keepdims=True)

    row = pl.BlockSpec((tm, D), lambda i: (i, 0))
    return pl.pallas_call(
        body, name=name, grid=(S // tm,),
        in_specs=[row, row, pl.BlockSpec((tm, 1), lambda i: (i, 0)), _full((1, D))],
        out_specs=[row, row, _full((1, D)), _full((1, D))],
        out_shape=[jax.ShapeDtypeStruct((S, D), F32), jax.ShapeDtypeStruct((S, D), BF16),
                   jax.ShapeDtypeStruct((1, D), F32), jax.ShapeDtypeStruct((1, D), F32)],
        compiler_params=_params(("arbitrary",)),
    )(dy, xhat, rstd, g)


def _rms_fwd(h, off, width, g, name):
    S = h.shape[0]
    tm = _tile(S, (ROW_TILE, 256, 128))

    def body(c_ref, g_ref, y_ref, r_ref):
        c = c_ref[...]
        r = lax.rsqrt(jnp.mean(c * c, axis=-1, keepdims=True) + EPS)
        y_ref[...] = (c * r * g_ref[...]).astype(BF16)
        r_ref[...] = r

    return pl.pallas_call(
        body, name=name, grid=(S // tm,),
        in_specs=[_cols(tm, width, off), _full((1, width))],
        out_specs=[pl.BlockSpec((tm, width), lambda i: (i, 0)), pl.BlockSpec((tm, 1), lambda i: (i, 0))],
        out_shape=[jax.ShapeDtypeStruct((S, width), BF16), jax.ShapeDtypeStruct((S, 1), F32)],
        compiler_params=_params(("parallel",)),
    )(h, g)


def _rms_bwd(dy, h, off, width, rstd, g, name):
    S = h.shape[0]
    tm = _tile(S, (ROW_TILE, 256, 128))

    def body(dy_ref, c_ref, r_ref, g_ref, dc_ref, dg_ref):
        @pl.when(pl.program_id(0) == 0)
        def _():
            dg_ref[...] = jnp.zeros_like(dg_ref)

        dyv, c, r = dy_ref[...], c_ref[...], r_ref[...]
        dyg = dyv * g_ref[...]
        m = jnp.mean(dyg * c, axis=-1, keepdims=True)
        dc_ref[...] = (r * dyg - c * (r * r * r) * m).astype(BF16)
        dg_ref[...] += jnp.sum(dyv * c * r, axis=0, keepdims=True)

    return pl.pallas_call(
        body, name=name, grid=(S // tm,),
        in_specs=[pl.BlockSpec((tm, width), lambda i: (i, 0)), _cols(tm, width, off),
                  pl.BlockSpec((tm, 1), lambda i: (i, 0)), _full((1, width))],
        out_specs=[pl.BlockSpec((tm, width), lambda i: (i, 0)), _full((1, width))],
        out_shape=[jax.ShapeDtypeStruct((S, width), BF16), jax.ShapeDtypeStruct((1, width), F32)],
        compiler_params=_params(("arbitrary",)),
    )(dy, h, rstd, g)


def _loss(y, target, name):
    S, D = y.shape
    tm = _tile(S, (256, 128))

    def body(y_ref, t_ref, l_ref, dy_ref):
        @pl.when(pl.program_id(0) == 0)
        def _():
            l_ref[...] = jnp.zeros_like(l_ref)

        err = y_ref[...] - t_ref[...]
        dy_ref[...] = err * (1.0 / D)
        per_tok = jnp.mean(err * err, axis=-1, keepdims=True)
        l_ref[...] += 0.5 * jnp.sum(per_tok, axis=0, keepdims=True)

    row = pl.BlockSpec((tm, D), lambda i: (i, 0))
    return pl.pallas_call(
        body, name=name, grid=(S // tm,), in_specs=[row, row], out_specs=[_full((1, 1)), row],
        out_shape=[jax.ShapeDtypeStruct((1, 1), F32), jax.ShapeDtypeStruct((S, D), F32)],
        compiler_params=_params(("arbitrary",)),
    )(y, target)


def _rope(a1, a2, cs, sn, n, bwd, name):
    S = cs.shape[0]
    tm = _tile(S, (ROW_TILE, 256, 128))
    stacked = not isinstance(a1, tuple)

    def body(a1_ref, a2_ref, c_ref, s_ref, y1_ref, y2_ref):
        if stacked:
            v1 = jnp.sum(a1_ref[...], axis=0)
            v2 = jnp.sum(a2_ref[...], axis=0)
        else:
            v1, v2 = a1_ref[...].astype(F32), a2_ref[...].astype(F32)
        c, s = c_ref[...], s_ref[...]
        if bwd:
            y1_ref[...] = (v1 * c + v2 * s).astype(BF16)
            y2_ref[...] = (v2 * c - v1 * s).astype(BF16)
        else:
            y1_ref[...] = (v1 * c - v2 * s).astype(BF16)
            y2_ref[...] = (v2 * c + v1 * s).astype(BF16)

    row = pl.BlockSpec((tm, n), lambda i: (i, 0))
    if stacked:
        H = a1.shape[0]
        a_specs = [pl.BlockSpec((H, tm, n), lambda i: (0, i, 0))] * 2
        arrs = [a1, a2]
    else:
        a_specs = [_cols(tm, n, a1[1]), _cols(tm, n, a2[1])]
        arrs = [a1[0], a2[0]]
    return pl.pallas_call(
        body, name=name, grid=(S // tm,), in_specs=a_specs + [row, row], out_specs=[row, row],
        out_shape=[jax.ShapeDtypeStruct((S, n), BF16)] * 2,
        compiler_params=_params(("parallel",)),
    )(*arrs, cs, sn)


def _swa_mask(n, rows):
    blk = SWA_BLOCK
    row = lax.broadcasted_iota(jnp.int32, (rows, 2 * blk), 0) & (blk - 1)
    col = lax.broadcasted_iota(jnp.int32, (rows, 2 * blk), 1)
    rel = row + blk - col
    return (rel >= 0) & (rel < blk) & ((n > 0) | (col >= blk))


def _swa_specs(off_q, off_k, off_v, stacked):
    blk, aq, akv = SWA_BLOCK, SWA_Q_HEADS * SWA_HEAD_DIM, SWA_KV_HEADS * SWA_HEAD_DIM
    grp = SWA_Q_HEADS // SWA_KV_HEADS
    assert off_q % aq == 0 and off_k % akv == 0 and off_v % akv == 0 and blk & (blk - 1) == 0
    prev = lambda off: pl.BlockSpec((blk, akv), lambda n: (jnp.maximum(n - 1, 0), off // akv))
    cur = lambda off: pl.BlockSpec((blk, akv), lambda n: (n, off // akv))
    sink = _full((SWA_KV_HEADS, grp * blk, 1)) if stacked else pl.BlockSpec(memory_space=pltpu.SMEM)
    return [sink, _cols(blk, aq, off_q), prev(off_k), cur(off_k), prev(off_v), cur(off_v)]


def _swa_sinks(sinks):
    grp = SWA_Q_HEADS // SWA_KV_HEADS
    return jnp.repeat(sinks.reshape(SWA_KV_HEADS, grp), SWA_BLOCK, axis=1)[:, :, None]


def _swa_stack(x, kv):
    hd, grp = SWA_HEAD_DIM, SWA_Q_HEADS // SWA_KV_HEADS
    return jnp.concatenate([x[:, (kv * grp + g) * hd:(kv * grp + g + 1) * hd] for g in range(grp)], axis=0)


def _swa_fwd(h, off_q, off_k, off_v, sinks, name):
    S = h.shape[0]
    blk, hd, nh, nkv = SWA_BLOCK, SWA_HEAD_DIM, SWA_Q_HEADS, SWA_KV_HEADS
    grp = nh // nkv
    aq = nh * hd
    scale = hd ** -0.5

    def body(sink_ref, q_ref, kp_ref, kc_ref, vp_ref, vc_ref, o_ref, lse_ref):
        valid = _swa_mask(pl.program_id(0), blk)
        q = q_ref[...].astype(BF16)
        k2 = jnp.concatenate([kp_ref[...], kc_ref[...]], axis=0).astype(BF16)
        v2 = jnp.concatenate([vp_ref[...], vc_ref[...]], axis=0).astype(BF16)
        for hh in range(nh):
            kv = hh // grp
            qh = q[:, hh * hd:(hh + 1) * hd]
            kh = k2[:, kv * hd:(kv + 1) * hd]
            vh = v2[:, kv * hd:(kv + 1) * hd]
            s = lax.dot_general(qh, kh, _DIMS["nt"], preferred_element_type=F32) * scale
            s = jnp.where(valid, s, MASK)
            sk = sink_ref[hh]
            m = jnp.maximum(jnp.max(s, axis=1, keepdims=True), sk)
            p = jnp.exp(s - m)
            l = jnp.sum(p, axis=1, keepdims=True) + jnp.exp(sk - m)
            o_ref[:, hh * hd:(hh + 1) * hd] = jnp.dot((p / l).astype(BF16), vh, preferred_element_type=F32)
            lse_ref[:, hh:hh + 1] = m + jnp.log(l)

    return pl.pallas_call(
        body, name=name, grid=(S // blk,), in_specs=_swa_specs(off_q, off_k, off_v, False),
        out_specs=[pl.BlockSpec((blk, aq), lambda n: (n, 0)), pl.BlockSpec((blk, nh), lambda n: (n, 0))],
        out_shape=[jax.ShapeDtypeStruct((S, aq), F32), jax.ShapeDtypeStruct((S, nh), F32)],
        compiler_params=_params(("parallel",)),
    )(sinks, h, h, h, h, h)


def _swa_bwd(h, off_q, off_k, off_v, sinks, dout, lse, name):
    S = h.shape[0]
    blk, hd, nh, nkv = SWA_BLOCK, SWA_HEAD_DIM, SWA_Q_HEADS, SWA_KV_HEADS
    grp = nh // nkv
    aq, akv = nh * hd, nkv * hd
    scale = hd ** -0.5

    def body(sink_ref, q_ref, kp_ref, kc_ref, vp_ref, vc_ref, do_ref, lse_ref, dq_ref, dk_ref, dv_ref, ds_ref):
        n = pl.program_id(0)

        @pl.when(n == 0)
        def _():
            dk_ref[...] = jnp.zeros_like(dk_ref)
            dv_ref[...] = jnp.zeros_like(dv_ref)
            ds_ref[...] = jnp.zeros_like(ds_ref)

        valid = _swa_mask(n, grp * blk)
        q = q_ref[...].astype(BF16)
        k2 = jnp.concatenate([kp_ref[...], kc_ref[...]], axis=0).astype(BF16)
        v2 = jnp.concatenate([vp_ref[...], vc_ref[...]], axis=0).astype(BF16)
        do = do_ref[...]
        lane = lax.broadcasted_iota(jnp.int32, (1, 128), 1)
        dsink = jnp.zeros((1, 128), F32)
        cur = pl.ds(pl.multiple_of(n * blk, blk), blk)
        prev = pl.ds(pl.multiple_of(jnp.maximum(n - 1, 0) * blk, blk), blk)
        for kv in range(nkv):
            kh = k2[:, kv * hd:(kv + 1) * hd]
            vh = v2[:, kv * hd:(kv + 1) * hd]
            qs = _swa_stack(q, kv)
            dos = _swa_stack(do, kv)
            dos16 = dos.astype(BF16)
            lse = jnp.concatenate([lse_ref[:, kv * grp + g:kv * grp + g + 1] for g in range(grp)], axis=0)
            s = lax.dot_general(qs, kh, _DIMS["nt"], preferred_element_type=F32) * scale
            s = jnp.where(valid, s, MASK)
            p = jnp.exp(s - lse)
            p16 = p.astype(BF16)
            o = jnp.dot(p16, vh, preferred_element_type=F32)
            delta = jnp.sum(dos * o, axis=1, keepdims=True)
            dp = lax.dot_general(dos16, vh, _DIMS["nt"], preferred_element_type=F32)
            ds16 = (p * (dp - delta) * scale).astype(BF16)
            dqs = jnp.dot(ds16, kh, preferred_element_type=F32).astype(BF16)
            dk_acc = lax.dot_general(ds16, qs, _DIMS["tn"], preferred_element_type=F32)
            dv_acc = lax.dot_general(p16, dos16, _DIMS["tn"], preferred_element_type=F32)
            dsk = jnp.exp(sink_ref[kv] - lse) * delta
            for g in range(grp):
                hh = kv * grp + g
                dq_ref[:, hh * hd:(hh + 1) * hd] = dqs[g * blk:(g + 1) * blk]
                dsink += jnp.where(lane == hh, -jnp.sum(dsk[g * blk:(g + 1) * blk], axis=0, keepdims=True), 0.0)
            cols = slice(kv * hd, (kv + 1) * hd)
            dk_ref[cur, cols] += dk_acc[blk:]
            dv_ref[cur, cols] += dv_acc[blk:]

            @pl.when(n > 0)
            def _():
                dk_ref[prev, cols] += dk_acc[:blk]
                dv_ref[prev, cols] += dv_acc[:blk]

        ds_ref[...] += dsink

    return pl.pallas_call(
        body, name=name, grid=(S // blk,),
        in_specs=_swa_specs(off_q, off_k, off_v, True) + [pl.BlockSpec((blk, aq), lambda n: (n, 0)),
                                                    pl.BlockSpec((blk, nh), lambda n: (n, 0))],
        out_specs=[pl.BlockSpec((blk, aq), lambda n: (n, 0)), _full((S, akv)), _full((S, akv)), _full((1, 128))],
        out_shape=[jax.ShapeDtypeStruct((S, aq), BF16), jax.ShapeDtypeStruct((S, akv), F32),
                   jax.ShapeDtypeStruct((S, akv), F32), jax.ShapeDtypeStruct((1, 128), F32)],
        compiler_params=_params(("arbitrary",)),
    )(_swa_sinks(sinks), h, h, h, h, h, dout, lse)


def _causal(i, j, t):
    row = i * t + lax.broadcasted_iota(jnp.int32, (t, t), 0)
    col = j * t + lax.broadcasted_iota(jnp.int32, (t, t), 1)
    return col <= row


def _mla_fwd(q, kt, v, name):
    H, S, dq = q.shape
    dv = v.shape[2]
    t = _tile(S, (MLA_FWD_TILE, 512, 256, 128))
    nq = S // t
    hp = 2 if H % 2 == 0 else 1
    scale = (MLA_NOPE + MLA_ROPE) ** -0.5

    def body(q_ref, kt_ref, v_ref, o_ref, lse_ref, *state):
        i = pl.program_id(1)
        for hh in range(hp):
            m_s, l_s, acc_s = state[3 * hh:3 * hh + 3]
            m_s[...] = jnp.full_like(m_s, -jnp.inf)
            l_s[...] = jnp.zeros_like(l_s)
            acc_s[...] = jnp.zeros_like(acc_s)

        def block(j, masked):
            rows = pl.ds(pl.multiple_of(j * t, t), t)
            for hh in range(hp):
                m_s, l_s, acc_s = state[3 * hh:3 * hh + 3]
                s = jnp.dot(q_ref[hh], kt_ref[hh, :, rows], preferred_element_type=F32) * scale
                if masked:
                    s = jnp.where(_causal(0, 0, t), s, MASK)
                m_old = m_s[...]
                m_new = jnp.maximum(m_old, jnp.max(s, axis=1, keepdims=True))
                alpha = jnp.exp(m_old - m_new)
                p = jnp.exp(s - m_new)
                l_s[...] = alpha * l_s[...] + jnp.sum(p, axis=1, keepdims=True)
                acc_s[...] = alpha * acc_s[...] + jnp.dot(p.astype(BF16), v_ref[hh, rows, :], preferred_element_type=F32)
                m_s[...] = m_new

        def full_block(j, carry):
            block(j, False)
            return carry

        lax.fori_loop(0, i, full_block, 0)
        block(i, True)
        for hh in range(hp):
            m_s, l_s, acc_s = state[3 * hh:3 * hh + 3]
            o_ref[hh] = acc_s[...] / l_s[...]
            lse_ref[hh] = m_s[...] + jnp.log(l_s[...])

    tile = lambda d: pl.BlockSpec((hp, t, d), lambda h, i: (h, i, 0))
    return pl.pallas_call(
        body, name=name, grid=(H // hp, nq),
        in_specs=[tile(dq), pl.BlockSpec((hp, dq, S), lambda h, i: (h, 0, 0)), pl.BlockSpec((hp, S, dv), lambda h, i: (h, 0, 0))],
        out_specs=[tile(dv), tile(1)],
        out_shape=[jax.ShapeDtypeStruct((H, S, dv), F32), jax.ShapeDtypeStruct((H, S, 1), F32)],
        scratch_shapes=[pltpu.VMEM((t, 1), F32), pltpu.VMEM((t, 1), F32), pltpu.VMEM((t, dv), F32)] * hp,
        compiler_params=_params(("parallel", "arbitrary")),
    )(q, kt, v)


def _rowdot(a, b, name):
    H, S, d = a.shape
    t = _tile(S, (ROW_TILE, 256, 128))

    def body(a_ref, b_ref, o_ref):
        o_ref[...] = jnp.sum(a_ref[...] * b_ref[...], axis=-1, keepdims=True)

    spec = pl.BlockSpec((1, t, d), lambda h, i: (h, i, 0))
    return pl.pallas_call(
        body, name=name, grid=(H, S // t), in_specs=[spec, spec],
        out_specs=pl.BlockSpec((1, t, 1), lambda h, i: (h, i, 0)),
        out_shape=jax.ShapeDtypeStruct((H, S, 1), F32),
        compiler_params=_params(("parallel", "parallel")),
    )(a, b)


def _mla_bwd(q, k, v, do, lse, delta, name):
    H, S, dq = q.shape
    dv = v.shape[2]
    t = _tile(S, (MLA_TILE, 256, 128))
    nq = S // t
    hp = 2 if H % 2 == 0 else 1
    scale = (MLA_NOPE + MLA_ROPE) ** -0.5

    def body(q_ref, k_ref, v_ref, do_ref, lse_ref, dl_ref, dq_ref, dk_ref, dv_ref, dk_s, dv_s):
        j, i = pl.program_id(1), pl.program_id(2)

        @pl.when((j == 0) & (i == 0))
        def _():
            dq_ref[...] = jnp.zeros_like(dq_ref)

        @pl.when(i == j)
        def _():
            dk_s[...] = jnp.zeros_like(dk_s)
            dv_s[...] = jnp.zeros_like(dv_s)

        def block(masked):
            rows = pl.ds(pl.multiple_of(i * t, t), t)
            for hh in range(hp):
                qv, kv_, vv, dov = q_ref[hh], k_ref[hh], v_ref[hh], do_ref[hh]
                s = lax.dot_general(qv, kv_, _DIMS["nt"], preferred_element_type=F32) * scale
                if masked:
                    s = jnp.where(_causal(0, 0, t), s, MASK)
                p = jnp.exp(s - lse_ref[hh])
                p16 = p.astype(BF16)
                dp = lax.dot_general(dov, vv, _DIMS["nt"], preferred_element_type=F32)
                ds16 = (p * (dp - dl_ref[hh]) * scale).astype(BF16)
                dv_s[hh] += lax.dot_general(p16, dov, _DIMS["tn"], preferred_element_type=F32)
                dk_s[hh] += lax.dot_general(ds16, qv, _DIMS["tn"], preferred_element_type=F32)
                dq_ref[hh, rows, :] += jnp.dot(ds16, kv_, preferred_element_type=F32)

        @pl.when(i == j)
        def _():
            block(True)

        @pl.when(i > j)
        def _():
            block(False)

        @pl.when(i == nq - 1)
        def _():
            dk_ref[...] = dk_s[...]
            dv_ref[...] = dv_s[...]

    q_map = lambda h, j, i: (h, jnp.maximum(i, j), 0)
    kv_map = lambda h, j, i: (h, j, 0)
    return pl.pallas_call(
        body, name=name, grid=(H // hp, nq, nq),
        in_specs=[pl.BlockSpec((hp, t, dq), q_map), pl.BlockSpec((hp, t, dq), kv_map), pl.BlockSpec((hp, t, dv), kv_map),
                  pl.BlockSpec((hp, t, dv), q_map), pl.BlockSpec((hp, t, 1), q_map), pl.BlockSpec((hp, t, 1), q_map)],
        out_specs=[pl.BlockSpec((hp, S, dq), lambda h, j, i: (h, 0, 0)), pl.BlockSpec((hp, t, dq), kv_map),
                   pl.BlockSpec((hp, t, dv), kv_map)],
        out_shape=[jax.ShapeDtypeStruct((H, S, dq), F32), jax.ShapeDtypeStruct((H, S, dq), F32),
                   jax.ShapeDtypeStruct((H, S, dv), F32)],
        scratch_shapes=[pltpu.VMEM((hp, t, dq), F32), pltpu.VMEM((hp, t, dv), F32)],
        compiler_params=_params(("parallel", "arbitrary", "arbitrary")),
    )(q, k, v, do, lse, delta)


def _sgu_norm(hv, lg, lb):
    vg = _gelu(hv)
    mu = jnp.mean(vg, axis=-1, keepdims=True)
    d = vg - mu
    rstd = lax.rsqrt(jnp.mean(d * d, axis=-1, keepdims=True) + EPS)
    xhat = d * rstd
    return xhat, rstd, xhat * lg + lb


def _sgu_fwd(h, off_u, off_v, lg, lb, w16, bt, name):
    S = h.shape[0]
    T, G, C = SGU_CHUNK, SGU_GROUPS, SGU_DIM
    W = G * C

    def body(hu_ref, hv_ref, lg_ref, lb_ref, w_ref, bt_ref, y_ref):
        u = _gelu(hu_ref[...])
        _, _, vn = _sgu_norm(hv_ref[...], lg_ref[...], lb_ref[...])
        vn16 = vn.astype(BF16)
        for g in range(G):
            cols = slice(g * C, (g + 1) * C)
            mixed = jnp.dot(w_ref[g], vn16[:, cols], preferred_element_type=F32) + bt_ref[:, g:g + 1]
            y_ref[:, cols] = (u[:, cols] * mixed).astype(BF16)

    return pl.pallas_call(
        body, name=name, grid=(S // T,),
        in_specs=[_cols(T, W, off_u), _cols(T, W, off_v), _full((1, W)), _full((1, W)), _full((G, T, T)), _full((T, G))],
        out_specs=pl.BlockSpec((T, W), lambda n: (n, 0)),
        out_shape=jax.ShapeDtypeStruct((S, W), BF16),
        compiler_params=_params(("parallel",)),
    )(h, h, lg, lb, w16, bt)


def _sgu_bwd(h, off_u, off_v, lg, lb, w16, bt, dy, name):
    S = h.shape[0]
    T, G, C = SGU_CHUNK, SGU_GROUPS, SGU_DIM
    W = G * C
    nc = S // T

    def body(hu_ref, hv_ref, lg_ref, lb_ref, w_ref, bt_ref, dy_ref, dhu_ref, dhv_ref, dw_ref, db_ref, dlg_ref, dlb_ref,
             dmix_s, dvn_s):
        n = pl.program_id(0)

        @pl.when(n == 0)
        def _():
            dw_ref[...] = jnp.zeros_like(dw_ref)
            dlg_ref[...] = jnp.zeros_like(dlg_ref)
            dlb_ref[...] = jnp.zeros_like(dlb_ref)
            dmix_s[...] = jnp.zeros_like(dmix_s)

        hu, hv, lgv = hu_ref[...], hv_ref[...], lg_ref[...]
        u = _gelu(hu)
        xhat, rstd, vn = _sgu_norm(hv, lgv, lb_ref[...])
        vn16 = vn.astype(BF16)
        dyv = dy_ref[...]
        dmixed = dyv * u
        dmix_s[...] += dmixed
        dmixed16 = dmixed.astype(BF16)
        for g in range(G):
            cols = slice(g * C, (g + 1) * C)
            mixed = jnp.dot(w_ref[g], vn16[:, cols], preferred_element_type=F32) + bt_ref[:, g:g + 1]
            dhu_ref[:, cols] = (dyv[:, cols] * mixed * _gelu_grad(hu[:, cols])).astype(BF16)
            dvn_s[:, cols] = lax.dot_general(w_ref[g], dmixed16[:, cols], _DIMS["tn"], preferred_element_type=F32)
            dw_ref[g] += lax.dot_general(dmixed16[:, cols], vn16[:, cols], _DIMS["nt"], preferred_element_type=F32)
        dvn = dvn_s[...]
        dlg_ref[...] += jnp.sum(dvn * xhat, axis=0, keepdims=True)
        dlb_ref[...] += jnp.sum(dvn, axis=0, keepdims=True)
        dxh = dvn * lgv
        m1 = jnp.mean(dxh, axis=-1, keepdims=True)
        m2 = jnp.mean(dxh * xhat, axis=-1, keepdims=True)
        dvg = rstd * (dxh - m1 - xhat * m2)
        dhv_ref[...] = (dvg * _gelu_grad(hv)).astype(BF16)

        @pl.when(n == nc - 1)
        def _():
            tril = lax.broadcasted_iota(jnp.int32, (T, T), 1) <= lax.broadcasted_iota(jnp.int32, (T, T), 0)
            lane = lax.broadcasted_iota(jnp.int32, (T, 128), 1)
            db = jnp.zeros((T, 128), F32)
            for g in range(G):
                dw_ref[g] = jnp.where(tril, dw_ref[g], 0.0)
                db += jnp.where(lane == g, jnp.sum(dmix_s[:, g * C:(g + 1) * C], axis=1, keepdims=True), 0.0)
            db_ref[...] = db

    row = pl.BlockSpec((T, W), lambda n: (n, 0))
    return pl.pallas_call(
        body, name=name, grid=(nc,),
        in_specs=[_cols(T, W, off_u), _cols(T, W, off_v), _full((1, W)), _full((1, W)), _full((G, T, T)), _full((T, G)), row],
        out_specs=[row, row, _full((G, T, T)), _full((T, 128)), _full((1, W)), _full((1, W))],
        out_shape=[jax.ShapeDtypeStruct((S, W), BF16), jax.ShapeDtypeStruct((S, W), BF16),
                   jax.ShapeDtypeStruct((G, T, T), F32), jax.ShapeDtypeStruct((T, 128), F32),
                   jax.ShapeDtypeStruct((1, W), F32), jax.ShapeDtypeStruct((1, W), F32)],
        scratch_shapes=[pltpu.VMEM((T, W), F32), pltpu.VMEM((T, W), F32)],
        compiler_params=_params(("arbitrary",)),
    )(h, h, lg, lb, w16, bt, dy)


def _merge_fwd(ys, ps, h, bg, name):
    S = h.shape[0]
    D = ps[0].shape[1]
    tm = _tile(S, (512, 256, 128))
    tn = _tile(D, (512, 256, 128))
    nb = len(ys)

    def body(*refs):
        y_refs, p_refs, l_refs = refs[:nb], refs[nb:2 * nb], refs[2 * nb:3 * nb]
        bg_ref, mg_ref, z_ref = refs[3 * nb:]
        acc = jnp.zeros((tm, tn), F32)
        for b in range(nb):
            z = jnp.dot(y_refs[b][...].astype(BF16), p_refs[b][...], preferred_element_type=F32)
            z_ref[b] = z
            acc += _sigmoid(l_refs[b][...] + bg_ref[b:b + 1, :]) * z
        mg_ref[...] = acc.astype(BF16)

    in_specs = [pl.BlockSpec((tm, y.shape[1]), lambda i, j: (i, 0)) for y in ys]
    in_specs += [pl.BlockSpec((p.shape[0], tn), lambda i, j: (0, j)) for p in ps]
    in_specs += [pl.BlockSpec((tm, tn), functools.partial(lambda i, j, b: (i, b * (D // tn) + j), b=b)) for b in range(nb)]
    in_specs += [pl.BlockSpec((nb, tn), lambda i, j: (0, j))]
    return pl.pallas_call(
        body, name=name, grid=(S // tm, D // tn), in_specs=in_specs,
        out_specs=[pl.BlockSpec((tm, tn), lambda i, j: (i, j)), pl.BlockSpec((nb, tm, tn), lambda i, j: (0, i, j))],
        out_shape=[jax.ShapeDtypeStruct((S, D), BF16), jax.ShapeDtypeStruct((nb, S, D), F32)],
        compiler_params=_params(("parallel", "parallel")),
    )(*ys, *ps, *([h] * nb), bg)


def _merge_bwd(dm, z, h, bg, name):
    nb, S, D = z.shape
    tm = _tile(S, (256, 128))
    tn = _tile(D, (512, 256, 128))

    def body(*refs):
        dm_ref, z_ref = refs[0], refs[1]
        l_refs = refs[2:2 + nb]
        bg_ref, dz_ref, dl_ref, dbg_ref = refs[2 + nb:]

        @pl.when(pl.program_id(1) == 0)
        def _():
            dbg_ref[...] = jnp.zeros_like(dbg_ref)

        dmv = dm_ref[...]
        rows = lax.broadcasted_iota(jnp.int32, (SUBLANES, tn), 0)
        dbg = jnp.zeros((SUBLANES, tn), F32)
        for b in range(nb):
            gt = _sigmoid(l_refs[b][...] + bg_ref[b:b + 1, :])
            dz_ref[b] = (dmv * gt).astype(BF16)
            dl = dmv * z_ref[b] * gt * (1.0 - gt)
            dl_ref[b] = dl.astype(BF16)
            dbg += jnp.where(rows == b, jnp.sum(dl, axis=0, keepdims=True), 0.0)
        dbg_ref[...] += dbg

    in_specs = [pl.BlockSpec((tm, tn), lambda j, i: (i, j)), pl.BlockSpec((nb, tm, tn), lambda j, i: (0, i, j))]
    in_specs += [pl.BlockSpec((tm, tn), functools.partial(lambda j, i, b: (i, b * (D // tn) + j), b=b)) for b in range(nb)]
    in_specs += [pl.BlockSpec((nb, tn), lambda j, i: (0, j))]
    blk3 = pl.BlockSpec((nb, tm, tn), lambda j, i: (0, i, j))
    return pl.pallas_call(
        body, name=name, grid=(D // tn, S // tm), in_specs=in_specs,
        out_specs=[blk3, blk3, pl.BlockSpec((SUBLANES, tn), lambda j, i: (0, j))],
        out_shape=[jax.ShapeDtypeStruct((nb, S, D), BF16), jax.ShapeDtypeStruct((nb, S, D), BF16),
                   jax.ShapeDtypeStruct((SUBLANES, D), F32)],
        compiler_params=_params(("parallel", "arbitrary")),
    )(dm, z, *([h] * nb), bg)


def _shift_down(x, halo, k):
    xr = pltpu.roll(x, k, 0)
    hr = pltpu.roll(halo, k, 0)
    rows = lax.broadcasted_iota(jnp.int32, halo.shape, 0)
    top = jnp.where(rows < k, hr, xr[:SUBLANES])
    return jnp.concatenate([top, xr[SUBLANES:]], axis=0)


def _shift_up(x, halo, k):
    tm = x.shape[0]
    xr = pltpu.roll(x, tm - k, 0)
    hr = pltpu.roll(halo, SUBLANES - k, 0)
    rows = lax.broadcasted_iota(jnp.int32, halo.shape, 0)
    bot = jnp.where(rows >= SUBLANES - k, hr, xr[tm - SUBLANES:])
    return jnp.concatenate([xr[:tm - SUBLANES], bot], axis=0)


def _conv_tiles(S, F):
    return _tile(S, (ROW_TILE, 256, 128)), _tile(F, (512, 256, 128))


def _conv_in_specs(tm, tn, F):
    r8 = tm // SUBLANES
    nf = F // tn
    specs = []
    for half in range(2):
        specs.append(pl.BlockSpec((tm, tn), functools.partial(lambda j, i, o: (i, o + j), o=half * nf)))
        specs.append(pl.BlockSpec((SUBLANES, tn), functools.partial(lambda j, i, o: (jnp.maximum(i * r8 - 1, 0), o + j), o=half * nf)))
    for half in range(2):
        specs.append(pl.BlockSpec((3, tn), functools.partial(lambda j, i, o: (0, o + j), o=half * nf)))
        specs.append(pl.BlockSpec((1, tn), functools.partial(lambda j, i, o: (0, o + j), o=half * nf)))
    return specs


def _conv_apply(x, halo, w, b, first):
    halo = jnp.where(first, 0.0, halo)
    x1 = _shift_down(x, halo, 1)
    x2 = _shift_down(x, halo, 2)
    return b + x2 * w[0:1, :] + x1 * w[1:2, :] + x * w[2:3, :], x1, x2


def _glu_fwd(up, cw, cb, name):
    S, F2 = up.shape
    F = F2 // 2
    tm, tn = _conv_tiles(S, F)

    def body(ug, hg, uv, hv, wg, bgr, wv, bvr, a_ref):
        first = pl.program_id(1) == 0
        cg, _, _ = _conv_apply(ug[...], hg[...], wg[...], bgr[...], first)
        cv, _, _ = _conv_apply(uv[...], hv[...], wv[...], bvr[...], first)
        a_ref[...] = (cg * _sigmoid(cg) * cv).astype(BF16)

    return pl.pallas_call(
        body, name=name, grid=(F // tn, S // tm), in_specs=_conv_in_specs(tm, tn, F),
        out_specs=pl.BlockSpec((tm, tn), lambda j, i: (i, j)),
        out_shape=jax.ShapeDtypeStruct((S, F), BF16),
        compiler_params=_params(("parallel", "parallel")),
    )(up, up, up, up, cw, cb, cw, cb)


def _glu_bwd(up, cw, cb, da, name):
    S, F2 = up.shape
    F = F2 // 2
    tm, tn = _conv_tiles(S, F)

    def body(ug, hg, uv, hv, wg, bgr, wv, bvr, da_ref, dg_ref, dv_ref, sg_ref, sv_ref):
        i = pl.program_id(1)

        @pl.when(i == 0)
        def _():
            sg_ref[...] = jnp.zeros_like(sg_ref)
            sv_ref[...] = jnp.zeros_like(sv_ref)

        first = i == 0
        xg, xv = ug[...], uv[...]
        cg, xg1, xg2 = _conv_apply(xg, hg[...], wg[...], bgr[...], first)
        cv, xv1, xv2 = _conv_apply(xv, hv[...], wv[...], bvr[...], first)
        dav = da_ref[...]
        sg = _sigmoid(cg)
        dcv = dav * cg * sg
        dcg = dav * cv * sg * (1.0 + cg * (1.0 - sg))
        dg_ref[...] = dcg
        dv_ref[...] = dcv
        rows = lax.broadcasted_iota(jnp.int32, (SUBLANES, tn), 0)

        def stats(dc, x, x1, x2):
            acc = jnp.zeros((SUBLANES, tn), F32)
            for r, val in enumerate((dc * x2, dc * x1, dc * x, dc)):
                acc += jnp.where(rows == r, jnp.sum(val, axis=0, keepdims=True), 0.0)
            return acc

        sg_ref[...] += stats(dcg, xg, xg1, xg2)
        sv_ref[...] += stats(dcv, xv, xv1, xv2)

    tile = pl.BlockSpec((tm, tn), lambda j, i: (i, j))
    stat = pl.BlockSpec((SUBLANES, tn), lambda j, i: (0, j))
    return pl.pallas_call(
        body, name=name, grid=(F // tn, S // tm), in_specs=_conv_in_specs(tm, tn, F) + [tile],
        out_specs=[tile, tile, stat, stat],
        out_shape=[jax.ShapeDtypeStruct((S, F), F32), jax.ShapeDtypeStruct((S, F), F32),
                   jax.ShapeDtypeStruct((SUBLANES, F), F32), jax.ShapeDtypeStruct((SUBLANES, F), F32)],
        compiler_params=_params(("parallel", "arbitrary")),
    )(up, up, up, up, cw, cb, cw, cb, da)


def _conv_bwd(dcg, dcv, w, name):
    S, F = dcg.shape
    tm, tn = _conv_tiles(S, F)
    r8 = tm // SUBLANES
    ni = S // tm
    nf = F // tn

    def body(g_ref, gh_ref, v_ref, vh_ref, w_ref, o_ref):
        gate = pl.program_id(0) == 0
        x = jnp.where(gate, g_ref[...], v_ref[...])
        halo = jnp.where(gate, gh_ref[...], vh_ref[...])
        halo = jnp.where(pl.program_id(2) == ni - 1, 0.0, halo)
        wv = w_ref[...]
        o_ref[...] = (x * wv[2:3, :] + _shift_up(x, halo, 1) * wv[1:2, :] + _shift_up(x, halo, 2) * wv[0:1, :]).astype(BF16)

    def tile(half):
        return pl.BlockSpec((tm, tn), lambda h, j, i: (jnp.where(h == half, i, 0), jnp.where(h == half, j, 0)))

    def below(half):
        return pl.BlockSpec((SUBLANES, tn), lambda h, j, i: (
            jnp.where(h == half, jnp.minimum((i + 1) * r8, S // SUBLANES - 1), 0), jnp.where(h == half, j, 0)))

    return pl.pallas_call(
        body, name=name, grid=(2, nf, ni),
        in_specs=[tile(0), below(0), tile(1), below(1), pl.BlockSpec((3, tn), lambda h, j, i: (0, h * nf + j))],
        out_specs=pl.BlockSpec((tm, tn), lambda h, j, i: (i, h * nf + j)),
        out_shape=jax.ShapeDtypeStruct((S, 2 * F), BF16),
        compiler_params=_params(("parallel", "parallel", "parallel")),
    )(dcg, dcg, dcv, dcv, w)


def _adamw(slot_list, own_list, me, w, m, v, name):
    L = len(slot_list)
    P, K, C = slot_list[0].shape
    tr = _tile(K, (256, 128, 64, 32, 16))
    while tr * C * 4 > (1 << 20) and tr % 32 == 0:
        tr //= 2
    nb = K // tr
    has_own = own_list is not None

    def body(me_ref, *refs):
        s_refs = refs[:L]
        o_refs = refs[L:2 * L] if has_own else None
        w_ref, m_ref, v_ref, g_ref, d_ref, nm_ref, nv_ref = refs[L * (1 + has_own):]
        layer = pl.program_id(0)
        g = None
        for l in range(L):
            gl = None
            for p in range(P):
                term = s_refs[l][p].astype(F32)
                if has_own:
                    term = jnp.where(me_ref[0] == p, o_refs[l][0].astype(F32), term)
                gl = term if gl is None else gl + term
            g = gl if g is None else jnp.where(layer == l, gl, g)
        nm = ADAM_B1 * m_ref[...] + (1.0 - ADAM_B1) * g
        nv = ADAM_B2 * v_ref[...] + (1.0 - ADAM_B2) * (g * g)
        m_hat = nm / (1.0 - ADAM_B1 ** ADAM_STEP)
        v_hat = nv / (1.0 - ADAM_B2 ** ADAM_STEP)
        g_ref[...] = g
        d_ref[...] = -ADAM_LR * (m_hat / (jnp.sqrt(v_hat) + ADAM_EPS) + ADAM_WD * w_ref[...])
        nm_ref[...] = nm
        nv_ref[...] = nv

    blk = pl.BlockSpec((None, tr, C), lambda li, i, me_ref: (li, i, 0))
    specs = [pl.BlockSpec((P, tr, C), functools.partial(lambda li, i, me_ref, l: (0, jnp.where(li == l, i, 0), 0), l=l))
             for l in range(L)]
    if has_own:
        specs += [pl.BlockSpec((1, tr, C), functools.partial(lambda li, i, me_ref, l: (me_ref[0], jnp.where(li == l, i, 0), 0), l=l))
                  for l in range(L)]
    return pl.pallas_call(
        body, name=name,
        grid_spec=pltpu.PrefetchScalarGridSpec(num_scalar_prefetch=1, grid=(L, nb), in_specs=specs + [blk, blk, blk],
                                               out_specs=[blk] * 4),
        out_shape=[jax.ShapeDtypeStruct((L, K, C), F32)] * 4,
        compiler_params=_params(("arbitrary", "arbitrary")),
    )(me, *slot_list, *(own_list if has_own else []), w, m, v)


_HBM = pl.BlockSpec(memory_space=pltpu.HBM)
_SEM = pl.BlockSpec(memory_space=pltpu.SEMAPHORE)
_ANY = pl.BlockSpec(memory_space=pl.ANY)


def _peers():
    x, y, c = lax.axis_index("x"), lax.axis_index("y"), lax.axis_index("c")

    def flip(v, bit):
        return 1 - v if bit else v

    def peer(k):
        return (flip(x, (k >> 2) & 1), flip(y, (k >> 1) & 1), flip(c, k & 1))

    def peer_index(k):
        px, py, pc = peer(k)
        return 4 * px + 2 * py + pc

    return 4 * x + 2 * y + c, peer, peer_index


def _split_copy(src_refs, land_refs, send_sems, recv_sems, scatter, a, k, outgoing):
    me, peer, peer_index = _peers()
    if outgoing:
        src = src_refs[a].at[peer_index(k)] if scatter else src_refs[a]
        dst = land_refs[a].at[me]
    else:
        src = src_refs[a].at[me] if scatter else src_refs[a]
        dst = land_refs[a].at[peer_index(k)]
    pair = a * (N_DEV - 1) + k - 1
    return pltpu.make_async_remote_copy(src_ref=src, dst_ref=dst, send_sem=send_sems.at[pair],
                                        recv_sem=recv_sems.at[pair], device_id=peer(k),
                                        device_id_type=pl.DeviceIdType.MESH)


def _exchange_start(srcs, scatter, after, name):
    na = len(srcs)
    land_shapes = [s.shape if scatter else (N_DEV,) + s.shape for s in srcs]
    has_after = after is not None

    def body(*refs):
        src_refs, land_refs = refs[:na], refs[na:2 * na]
        send_sems, recv_sems = refs[2 * na + has_after], refs[2 * na + has_after + 1]
        token = refs[-1]
        for k in range(1, N_DEV):
            for a in range(na):
                _split_copy(src_refs, land_refs, send_sems, recv_sems, scatter, a, k, True).start()
        token[...] = jnp.zeros_like(token)

    sems = pltpu.SemaphoreType.DMA((na * (N_DEV - 1),))
    out_shape = ([sems, sems] + [pltpu.HBM(s.shape, s.dtype) for s in srcs]
                 + [pltpu.HBM(shp, s.dtype) for shp, s in zip(land_shapes, srcs)] + [jax.ShapeDtypeStruct((SUBLANES, 128), F32)])
    args = [pltpu.with_memory_space_constraint(s, pltpu.HBM) for s in srcs]
    args += [pltpu.with_memory_space_constraint(lax.empty(shp, s.dtype), pltpu.HBM) for shp, s in zip(land_shapes, srcs)]
    if has_after:
        args.append(after)
    res = pl.pallas_call(
        body, name=name, in_specs=[_HBM] * (2 * na) + [_ANY] * has_after,
        out_specs=[_SEM, _SEM] + [_HBM] * (2 * na) + [pl.BlockSpec(memory_space=pltpu.VMEM)], out_shape=out_shape,
        input_output_aliases={i: 2 + i for i in range(2 * na)},
        compiler_params=pltpu.CompilerParams(has_side_effects=pltpu.SideEffectType.DATAFLOW_SIDE_EFFECTING),
    )(*args)
    handle = dict(send=res[0], recv=res[1], srcs=list(res[2:2 + na]), lands=list(res[2 + na:2 + 2 * na]), scatter=scatter)
    return handle, res[-1]


def _exchange_wait(handle, after, name):
    srcs, lands, scatter = handle["srcs"], handle["lands"], handle["scatter"]
    na = len(srcs)

    def body(*refs):
        src_refs, land_refs = refs[:na], refs[na:2 * na]
        send_sems, recv_sems = refs[2 * na], refs[2 * na + 1]
        for k in range(1, N_DEV):
            for a in range(na):
                _split_copy(src_refs, land_refs, send_sems, recv_sems, scatter, a, k, True).wait_send()
                _split_copy(src_refs, land_refs, send_sems, recv_sems, scatter, a, k, False).wait_recv()

    res = pl.pallas_call(
        body, name=name, in_specs=[_HBM] * (2 * na) + [_SEM, _SEM, _ANY], out_specs=[_HBM] * (2 * na),
        out_shape=[pltpu.HBM(s.shape, s.dtype) for s in srcs] + [pltpu.HBM(s.shape, s.dtype) for s in lands],
        input_output_aliases={i: i for i in range(2 * na)},
        compiler_params=pltpu.CompilerParams(has_side_effects=pltpu.SideEffectType.DATAFLOW_SIDE_EFFECTING),
    )(*srcs, *lands, handle["send"], handle["recv"], after)
    return list(res[:na]), list(res[na:])


def _layout(D):
    aq, akv = SWA_Q_HEADS * SWA_HEAD_DIM, SWA_KV_HEADS * SWA_HEAD_DIM
    w = SGU_GROUPS * SGU_DIM
    return aq, akv, w


class _Seg:
    def __init__(self, D, rq, rkv):
        aq, akv, w = _layout(D)
        src = {}
        o = 0
        for nm, wd in (("qa", aq), ("ka", akv), ("va", akv), ("cq", rq), ("ckv", rkv), ("kr", MLA_ROPE), ("hu", w), ("hv", w),
                       ("g", 3 * D)):
            src[nm] = (o, wd)
            o += wd
        self.n_in = o
        self.order = ("g", "qa", "hu", "hv", "cq", "ckv", "ka", "va", "kr")
        self.src = src
        self.off = {}
        o = 0
        for nm in self.order:
            self.off[nm] = o
            o += src[nm][1]
        self.width = {nm: src[nm][1] for nm in self.order}
        self.n_pad = -(-o // 1536) * 1536 if o > 1536 else -(-o // 512) * 512
        self.used = o

    def permute(self, w):
        parts = [w[:, self.src[nm][0]:self.src[nm][0] + self.src[nm][1]] for nm in self.order]
        parts.append(jnp.zeros((w.shape[0], self.n_pad - self.used), w.dtype))
        return jnp.concatenate(parts, axis=1)

    def unpermute(self, w):
        names = sorted(self.order, key=lambda nm: self.src[nm][0])
        return jnp.concatenate([w[:, self.off[nm]:self.off[nm] + self.width[nm]] for nm in names], axis=1)


def _uq_permute(w):
    R = w.shape[0]
    H, half = MLA_HEADS, MLA_ROPE // 2
    w3 = w.reshape(R, H, MLA_NOPE + MLA_ROPE)
    return jnp.concatenate([w3[:, :, :MLA_NOPE].reshape(R, H * MLA_NOPE),
                            w3[:, :, MLA_NOPE:MLA_NOPE + half].reshape(R, H * half),
                            w3[:, :, MLA_NOPE + half:].reshape(R, H * half)], axis=1)


def _uq_unpermute(w):
    R = w.shape[0]
    H, half = MLA_HEADS, MLA_ROPE // 2
    n = w[:, :H * MLA_NOPE].reshape(R, H, MLA_NOPE)
    r1 = w[:, H * MLA_NOPE:H * (MLA_NOPE + half)].reshape(R, H, half)
    r2 = w[:, H * (MLA_NOPE + half):].reshape(R, H, half)
    return jnp.concatenate([n, r1, r2], axis=2).reshape(R, H * (MLA_NOPE + MLA_ROPE))


def _ukv_permute(w):
    R = w.shape[0]
    w3 = w.reshape(R, MLA_HEADS, MLA_NOPE + MLA_V)
    return jnp.concatenate([w3[:, :, :MLA_NOPE].reshape(R, -1), w3[:, :, MLA_NOPE:].reshape(R, -1)], axis=1)


def _ukv_unpermute(w):
    R = w.shape[0]
    H = MLA_HEADS
    k = w[:, :H * MLA_NOPE].reshape(R, H, MLA_NOPE)
    v = w[:, H * MLA_NOPE:].reshape(R, H, MLA_V)
    return jnp.concatenate([k, v], axis=2).reshape(R, H * (MLA_NOPE + MLA_V))


def _heads(a, d):
    S = a.shape[0]
    return a.reshape(S, MLA_HEADS, d).transpose(1, 0, 2)


def _unheads(a):
    H, S, d = a.shape
    return a.transpose(1, 0, 2).reshape(S, H * d)


GROUPS = {"a": ("w_in",), "b": ("w_uq", "w_ukv", "w_proj_a", "w_proj_b", "w_proj_c", "w_o", "b_gate"),
          "c": ("w_up", "w_down", "conv_w")}


def _layer_fwd(l, x, x16, fetch, P, cs, sn, seg, alpha):
    S, D = x.shape
    H, half = MLA_HEADS, MLA_ROPE // 2
    off = seg.off
    nm = lambda s: f"l{l}_{s}"
    sv = {"x16": x16}
    W = {"w_in": seg.permute(fetch(l, "a", x16)["w_in"])}
    h = _mm(x16, W["w_in"], "nn", [(F32, "n")], nm("h"))[0]
    sv["h"] = h
    ya, lse_a = _swa_fwd(h, off["qa"], off["ka"], off["va"], P["sinks"], nm("swa_fwd"))
    cqn, rq = _rms_fwd(h, off["cq"], seg.width["cq"], P["q_norm_g"], nm("rmsq_fwd"))
    ckvn, rkv = _rms_fwd(h, off["ckv"], seg.width["ckv"], P["kv_norm_g"], nm("rmskv_fwd"))
    W.update(fetch(l, "b", cqn))
    W["w_uq"] = _uq_permute(W["w_uq"])
    W["w_ukv"] = _ukv_permute(W["w_ukv"])
    qf = _mm(cqn, W["w_uq"], "nn", [(F32, "n")], nm("uq"))[0]
    kvf = _mm(ckvn, W["w_ukv"], "nn", [(BF16, "n")], nm("ukv"))[0]
    cs_h, sn_h = jnp.tile(cs, (1, H)), jnp.tile(sn, (1, H))
    qy1, qy2 = _rope((qf, H * MLA_NOPE), (qf, H * MLA_NOPE + H * half), cs_h, sn_h, H * half, False, nm("ropeq_fwd"))
    kr = h[:, off["kr"]:off["kr"] + MLA_ROPE]
    ky1, ky2 = _rope((kr[:, :half], 0), (kr[:, half:], 0), cs, sn, half, False, nm("ropek_fwd"))
    qh = jnp.concatenate([qf[:, :H * MLA_NOPE].astype(BF16).reshape(S, H, MLA_NOPE), qy1.reshape(S, H, half),
                          qy2.reshape(S, H, half)], axis=2).transpose(1, 0, 2)
    kh = jnp.concatenate([kvf[:, :H * MLA_NOPE].reshape(S, H, MLA_NOPE),
                          jnp.broadcast_to(ky1[:, None, :], (S, H, half)),
                          jnp.broadcast_to(ky2[:, None, :], (S, H, half))], axis=2).transpose(1, 0, 2)
    vh = _heads(kvf[:, H * MLA_NOPE:], MLA_V)
    ob, lse_b = _mla_fwd(qh, kh.transpose(0, 2, 1), vh, nm("mla_fwd"))
    yb = _unheads(ob).astype(BF16)
    w16 = jnp.where(jnp.tril(jnp.ones((SGU_CHUNK, SGU_CHUNK), bool))[None], P["sgu_w"], 0.0).astype(BF16)
    bt = P["sgu_b"].T
    yc = _sgu_fwd(h, off["hu"], off["hv"], P["sgu_ln_g"], P["sgu_ln_b"], w16, bt, nm("sgu_fwd"))
    merged, z = _merge_fwd([ya, yb, yc], [W["w_proj_a"], W["w_proj_b"], W["w_proj_c"]], h, W["b_gate"], nm("merge_fwd"))
    x1, x1_16, xh1, rs1 = _mm_ln(merged, W["w_o"], x, P["ln1_g"], P["ln1_b"], alpha, nm("wo_ln1"))
    W.update(fetch(l, "c", x1_16))
    up = _mm(x1_16, W["w_up"], "nn", [(F32, "n")], nm("up"))[0]
    a = _glu_fwd(up, W["conv_w"], P["conv_b"], nm("glu_fwd"))
    x2, x2_16, xh2, rs2 = _mm_ln(a, W["w_down"], x1, P["ln2_g"], P["ln2_b"], alpha, nm("down_ln2"))
    sv.update(W=W, ya=ya, lse_a=lse_a, cqn=cqn, rq=rq, ckvn=ckvn, rkv=rkv, qh=qh, kh=kh, vh=vh, ob=ob, lse_b=lse_b, yb=yb,
              w16=w16, bt=bt, yc=yc, merged=merged, z=z, x1_16=x1_16, xh1=xh1, rs1=rs1, up=up, a=a, xh2=xh2, rs2=rs2,
              cs_h=cs_h, sn_h=sn_h)
    return x2, x2_16, sv


def _dw_chunks(k, a, dy, name, post=None):
    n = dy.shape[1]
    if k not in ROW_SHARDED and post is None and (n // N_DEV) % 128 == 0:
        return _mm(a, dy, "tn", [(BF16, "n")], name, chunk=n // N_DEV)[0]
    g = _mm(a, dy, "tn", [(BF16, "n")], name)[0]
    return _to_chunks(k, g if post is None else post(g))


def _after(arr, token):
    return arr if token is None else arr + token[0:1, 0:1].astype(arr.dtype)


def _layer_bwd(l, dx2, sv, P, cs, sn, seg, alpha, emit):
    S, D = dx2.shape
    H, half = MLA_HEADS, MLA_ROPE // 2
    off = seg.off
    h, W = sv["h"], sv["W"]
    nm = lambda s: f"l{l}_{s}"
    g = {}
    dr2, dr2_16, g["ln2_g"], g["ln2_b"] = _ln_bwd(dx2, sv["xh2"], sv["rs2"], P["ln2_g"], nm("ln2_bwd"))
    g["w_down"] = _dw_chunks("w_down", sv["a"], dr2_16, nm("dw_down"))
    da = _mm(dr2_16, W["w_down"], "nt", [(F32, "n")], nm("da"))[0]
    dcg, dcv, st_g, st_v = _glu_bwd(sv["up"], W["conv_w"], P["conv_b"], da, nm("glu_bwd"))
    F = dcg.shape[1]
    g["conv_w"] = _to_chunks("conv_w", jnp.concatenate([st_g[0:3], st_v[0:3]], axis=1))
    g["conv_b"] = jnp.concatenate([st_g[3:4], st_v[3:4]], axis=1)
    dup = _conv_bwd(dcg, dcv, W["conv_w"], nm("conv_bwd"))
    g["w_up"] = _dw_chunks("w_up", sv["x1_16"], dup, nm("dw_up"))
    token = emit(l, "c", {k: g.pop(k) for k in GROUPS["c"]})
    dx1 = _mm_axpy(dup, W["w_up"], "nt", dr2, alpha, nm("dx1"), dep=token)
    dr1, dr1_16, g["ln1_g"], g["ln1_b"] = _ln_bwd(dx1, sv["xh1"], sv["rs1"], P["ln1_g"], nm("ln1_bwd"))
    g["w_o"] = _dw_chunks("w_o", sv["merged"], dr1_16, nm("dw_o"))
    dmerged = _mm(dr1_16, W["w_o"], "nt", [(F32, "n")], nm("dmerged"))[0]
    dz, dlog, dbg = _merge_bwd(dmerged, sv["z"], h, W["b_gate"], nm("merge_bwd"))
    g["b_gate"] = _to_chunks("b_gate", dbg[0:3])
    g["w_proj_a"] = _dw_chunks("w_proj_a", sv["ya"], dz[0], nm("dw_pa"))
    g["w_proj_b"] = _dw_chunks("w_proj_b", sv["yb"], dz[1], nm("dw_pb"))
    g["w_proj_c"] = _dw_chunks("w_proj_c", sv["yc"], dz[2], nm("dw_pc"))
    dya = _mm(dz[0], W["w_proj_a"], "nt", [(F32, "n")], nm("dya"))[0]
    dyb = _mm(dz[1], W["w_proj_b"], "nt", [(F32, "n")], nm("dyb"))[0]
    dyc = _mm(dz[2], W["w_proj_c"], "nt", [(F32, "n")], nm("dyc"))[0]
    dhu, dhv, g["sgu_w"], db_s, g["sgu_ln_g"], g["sgu_ln_b"] = _sgu_bwd(
        h, off["hu"], off["hv"], P["sgu_ln_g"], P["sgu_ln_b"], sv["w16"], sv["bt"], dyc, nm("sgu_bwd"))
    g["sgu_b"] = db_s[:, :SGU_GROUPS].T
    dqa, dka, dva, dsk = _swa_bwd(h, off["qa"], off["ka"], off["va"], P["sinks"], dya, sv["lse_a"], nm("swa_bwd"))
    g["sinks"] = dsk[0, :SWA_Q_HEADS]
    dob = _heads(dyb, MLA_V)
    delta = _rowdot(dob, sv["ob"], nm("mla_delta"))
    dqh, dkh, dvh = _mla_bwd(sv["qh"], sv["kh"], sv["vh"], dob.astype(BF16), sv["lse_b"], delta, nm("mla_bwd"))
    n0, n1 = MLA_NOPE, MLA_NOPE + half
    dqx1, dqx2 = _rope((_unheads(dqh[:, :, n0:n1]), 0), (_unheads(dqh[:, :, n1:]), 0), sv["cs_h"], sv["sn_h"], H * half, True,
                       nm("ropeq_bwd"))
    dqf = jnp.concatenate([_unheads(dqh[:, :, :n0]).astype(BF16), dqx1, dqx2], axis=1)
    dkvf = jnp.concatenate([_unheads(dkh[:, :, :n0]), _unheads(dvh)], axis=1).astype(BF16)
    dk1, dk2 = _rope(dkh[:, :, n0:n1], dkh[:, :, n1:], cs, sn, half, True, nm("ropek_bwd"))
    g["w_uq"] = _dw_chunks("w_uq", sv["cqn"], dqf, nm("dw_uq"), _uq_unpermute)
    g["w_ukv"] = _dw_chunks("w_ukv", sv["ckvn"], dkvf, nm("dw_ukv"), _ukv_unpermute)
    token = emit(l, "b", {k: g.pop(k) for k in GROUPS["b"]})
    dcqn = _mm(dqf, W["w_uq"], "nt", [(F32, "n")], nm("dcqn"), dep=token)[0]
    dckvn = _mm(dkvf, W["w_ukv"], "nt", [(F32, "n")], nm("dckvn"), dep=token)[0]
    dcq, g["q_norm_g"] = _rms_bwd(dcqn, h, off["cq"], seg.width["cq"], sv["rq"], P["q_norm_g"], nm("rmsq_bwd"))
    dckv, g["kv_norm_g"] = _rms_bwd(dckvn, h, off["ckv"], seg.width["ckv"], sv["rkv"], P["kv_norm_g"], nm("rmskv_bwd"))
    parts = {"g": jnp.concatenate([dlog[0], dlog[1], dlog[2]], axis=1), "qa": dqa, "hu": dhu, "hv": dhv, "cq": dcq, "ckv": dckv,
             "ka": dka.astype(BF16), "va": dva.astype(BF16), "kr": jnp.concatenate([dk1, dk2], axis=1)}
    dh = jnp.concatenate([parts[k] for k in seg.order] + [jnp.zeros((S, seg.n_pad - seg.used), BF16)], axis=1)
    token = emit(l, "a", {"w_in": _dw_chunks("w_in", sv["x16"], dh, nm("dw_in"), seg.unpermute)})
    dx = _mm_axpy(dh, W["w_in"], "nt", dr1, alpha, nm("dx"), dep=token)
    return dx, g


BIG = ("w_in", "w_uq", "w_ukv", "w_proj_a", "w_proj_b", "w_proj_c", "w_o", "w_up", "w_down")
ROW_SHARDED = ("w_proj_b", "w_o", "w_down")
SHARDED_F32 = ("b_gate", "conv_w")
REPLICATED = ("sinks", "q_norm_g", "kv_norm_g", "sgu_ln_g", "sgu_ln_b", "sgu_w", "sgu_b", "ln1_g", "ln1_b", "conv_b", "ln2_g",
              "ln2_b")
WEIGHTS = ("w_in", "b_gate", "sinks", "q_norm_g", "kv_norm_g", "w_uq", "w_ukv", "sgu_ln_g", "sgu_ln_b", "sgu_w", "sgu_b",
           "w_proj_a", "w_proj_b", "w_proj_c", "w_o", "ln1_g", "ln1_b", "w_up", "conv_w", "conv_b", "w_down", "ln2_g", "ln2_b")


def _step_local(x, positions, target, small, rq, rkv, fetch, emit, token=None):
    S, D = x.shape
    L = small["sinks"].shape[0]
    alpha = (2 * L) ** 0.25
    seg = _Seg(D, rq, rkv)
    inv_freq = ROPE_THETA ** (-jnp.arange(0, MLA_ROPE, 2, dtype=F32) / MLA_ROPE)
    ang = positions.astype(F32)[:, None] * inv_freq
    cs, sn = jnp.cos(ang), jnp.sin(ang)
    rows = ("q_norm_g", "kv_norm_g", "sgu_ln_g", "sgu_ln_b", "ln1_g", "ln1_b", "conv_b", "ln2_g", "ln2_b")
    layers = [{k: small[k][l].reshape(1, -1) if k in rows else small[k][l] for k in small} for l in range(L)]
    saved = []
    x16 = _after(x, token).astype(BF16)
    for l in range(L):
        x, x16, sv = _layer_fwd(l, x, x16, fetch, layers[l], cs, sn, seg, alpha)
        saved.append(sv)
    loss, dx = _loss(x, target, "loss")
    grads = [None] * L
    for l in reversed(range(L)):
        dx, grads[l] = _layer_bwd(l, dx, saved[l], layers[l], cs, sn, seg, alpha, emit)
    out = {k: jnp.stack([grads[l][k].reshape(small[k].shape[1:]) for l in range(L)]) for k in small}
    return loss, dx, out


def _unshard(k, gathered):
    n, r, c = gathered.shape
    if k in ROW_SHARDED:
        return gathered.reshape(n * r, c)
    return gathered.transpose(1, 0, 2).reshape(r, n * c)


def _to_chunks(k, gfull):
    r, c = gfull.shape
    if k in ROW_SHARDED:
        return gfull.reshape(N_DEV, r // N_DEV, c)
    return gfull.reshape(r, N_DEV, c // N_DEV).transpose(1, 0, 2)


def _pack(arrs):
    P = arrs[0].shape[0]
    flat, sizes = [], []
    for a in arrs:
        f = a.reshape(P, -1)
        n = f.shape[1]
        pad = -n % (SUBLANES * 128)
        flat.append(jnp.pad(f, ((0, 0), (0, pad))))
        sizes.append((n, n + pad))
    return jnp.concatenate(flat, axis=1).reshape(P, -1, 128), sizes


def _unpack(packed, sizes, shapes):
    flat = packed.reshape(-1)
    out, o = [], 0
    for (n, npad), shp in zip(sizes, shapes):
        out.append(flat[o:o + n].reshape(shp))
        o += npad
    return out


def kernel(x, positions, w_in, b_gate, sinks, q_norm_g, kv_norm_g, w_uq, w_ukv, sgu_ln_g, sgu_ln_b, sgu_w, sgu_b, w_proj_a, w_proj_b, w_proj_c, w_o, ln1_g, ln1_b, w_up, conv_w, conv_b, w_down, ln2_g, ln2_b, loss_target, m_w_in, m_b_gate, m_sinks, m_q_norm_g, m_kv_norm_g, m_w_uq, m_w_ukv, m_sgu_ln_g, m_sgu_ln_b, m_sgu_w, m_sgu_b, m_w_proj_a, m_w_proj_b, m_w_proj_c, m_w_o, m_ln1_g, m_ln1_b, m_w_up, m_conv_w, m_conv_b, m_w_down, m_ln2_g, m_ln2_b, v_w_in, v_b_gate, v_sinks, v_q_norm_g, v_kv_norm_g, v_w_uq, v_w_ukv, v_sgu_ln_g, v_sgu_ln_b, v_sgu_w, v_sgu_b, v_w_proj_a, v_w_proj_b, v_w_proj_c, v_w_o, v_ln1_g, v_ln1_b, v_w_up, v_conv_w, v_conv_b, v_w_down, v_ln2_g, v_ln2_b):
    given = dict(locals())
    w = {k: given[k] for k in WEIGHTS}
    mom = {k: given["m_" + k] for k in WEIGHTS}
    var = {k: given["v_" + k] for k in WEIGHTS}

    L = w_in.shape[0]
    order = [(l, grp) for l in range(L) for grp in ("a", "b", "c")]

    gathers, token = {}, None
    for l, grp in order:
        srcs = [w[k][l].astype(BF16) if k in BIG else w[k][l] for k in GROUPS[grp]]
        gathers[l, grp], token = _exchange_start(srcs, False, token, f"gather_start_l{l}{grp}")

    me = 4 * lax.axis_index("x") + 2 * lax.axis_index("y") + lax.axis_index("c")
    mine = (jnp.arange(N_DEV) == me)[:, None, None]

    def fetch(l, grp, after):
        srcs, lands = _exchange_wait(gathers[l, grp], after, f"gather_wait_l{l}{grp}")
        return {k: _unshard(k, jnp.where(mine, srcs[i][None], lands[i])) for i, k in enumerate(GROUPS[grp])}

    scatters = {}

    def emit(l, grp, chunks):
        scatters[l, grp], tok = _exchange_start([chunks[k] for k in GROUPS[grp]], True, None, f"scatter_start_l{l}{grp}")
        return tok

    small = {k: w[k] for k in REPLICATED}
    loss, grad_x, g = _step_local(x[0], positions[0], loss_target[0], small, w_uq.shape[1], w_ukv.shape[1], fetch, emit, token)
    loss = lax.psum(loss[0, 0], AXES)

    packed, sizes = _pack([g[k][None] for k in REPLICATED])
    small_grads, after = _exchange_start([packed[0]], False, grad_x, "gather_small_grads_start")

    me1 = me.astype(jnp.int32).reshape(1)
    res = {}
    for grp in ("c", "b", "a"):
        slots, own = {}, {}
        for l in reversed(range(L)):
            srcs, lands = _exchange_wait(scatters[l, grp], after, f"scatter_wait_l{l}{grp}")
            for i, k in enumerate(GROUPS[grp]):
                slots[k, l], own[k, l] = lands[i], srcs[i]
        for k in GROUPS[grp]:
            res[k] = _adamw([slots[k, l] for l in range(L)], [own[k, l] for l in range(L)], me1, w[k], mom[k], var[k],
                            "adamw_" + k)
            after = res[k][1]

    srcs, lands = _exchange_wait(small_grads, after, "gather_small_grads_wait")
    parts = jnp.where(mine, srcs[0][None], lands[0])
    shapes = [w[k].shape for k in REPLICATED]
    pw, _ = _pack([w[k][None] for k in REPLICATED])
    pm, _ = _pack([mom[k][None] for k in REPLICATED])
    pv, _ = _pack([var[k][None] for k in REPLICATED])
    outs = _adamw([parts], None, me1, pw, pm, pv, "adamw_small")
    unpacked = [_unpack(o, sizes, shapes) for o in outs]
    for i, k in enumerate(REPLICATED):
        res[k] = [unpacked[j][i] for j in range(4)]

    return (loss, grad_x[None], *[res[k][0] for k in WEIGHTS], *[res[k][1] for k in WEIGHTS],
            *[res[k][2] for k in WEIGHTS], *[res[k][3] for k in WEIGHTS])
```

```python
import functools
import math

import jax
import jax.numpy as jnp
from jax import lax
from jax.experimental import pallas as pl
from jax.experimental.pallas import tpu as pltpu

F32 = jnp.float32
BF16 = jnp.bfloat16

SWA_Q_HEADS = 16
SWA_KV_HEADS = 2
SWA_HEAD_DIM = 64
SWA_BLOCK = 128
MLA_HEADS = 16
MLA_NOPE = 128
MLA_ROPE = 64
MLA_V = 128
SGU_GROUPS = 8
SGU_DIM = 128
SGU_CHUNK = 128
ROPE_THETA = 10000.0
EPS = 1e-5
MASK = -1e30
ADAM_LR = 0.001
ADAM_B1 = 0.9
ADAM_B2 = 0.999
ADAM_EPS = 1e-08
ADAM_WD = 0.01
ADAM_STEP = 10

N_DEV = 8
AXES = ("x", "y", "c")
VMEM_LIMIT = 56 * 1024 * 1024
MLA_TILE = 512
MLA_FWD_TILE = 1024
ROW_TILE = 512
MAX_TK = 2816
SUBLANES = 8


def _tile(n, prefs):
    for p in prefs:
        if n % p == 0:
            return p
    return n


def _params(sem):
    return pltpu.CompilerParams(dimension_semantics=sem, vmem_limit_bytes=VMEM_LIMIT)


def _cols(tm, width, off):
    assert off % width == 0, (off, width)
    blk = off // width
    return pl.BlockSpec((tm, width), lambda i, *_: (i, blk))


def _full(shape):
    nd = len(shape)
    return pl.BlockSpec(shape, lambda *_: (0,) * nd)


def _sigmoid(v):
    return 1.0 / (1.0 + jnp.exp(-v))


def _gelu(v):
    return 0.5 * v * (1.0 + lax.erf(v * (2.0 ** -0.5)))


def _gelu_grad(v):
    return 0.5 * (1.0 + lax.erf(v * (2.0 ** -0.5))) + v * jnp.exp(-0.5 * v * v) * (1.0 / math.sqrt(2.0 * math.pi))


_DIMS = {"nn": (((1,), (0,)), ((), ())), "nt": (((1,), (1,)), ((), ())), "tn": (((0,), (0,)), ((), ()))}


def _mm(a, b, mode, outs, name, *, extras=(), epilogue=None, full_n=False, dep=None, chunk=None):
    if mode == "nn":
        (M, K), (K2, N) = a.shape, b.shape
    elif mode == "nt":
        (M, K), (N, K2) = a.shape, b.shape
    else:
        (K, M), (K2, N) = a.shape, b.shape
    assert K == K2, (a.shape, b.shape, mode)
    tm = _tile(M, (1024, 512, 256, 128))
    tn = N if full_n else _tile(N, (1024, 768, 512, 384, 256, 128))
    if full_n:
        tm = _tile(M, (512, 256, 128))
    if chunk is not None:
        tn = chunk if chunk <= 1536 else _tile(chunk, (1024, 768, 512, 384, 256, 128))
        assert N % chunk == 0 and chunk % tn == 0 and tn % 128 == 0, (N, chunk, tn)
    max_tk = MAX_TK // 2 if full_n else MAX_TK
    tk = max(d for d in range(128, min(K, max_tk) + 1, 128) if K % d == 0) if K % 128 == 0 else K
    nk = K // tk
    if mode == "nn":
        a_spec = pl.BlockSpec((tm, tk), lambda i, j, k: (i, k))
        b_spec = pl.BlockSpec((tk, tn), lambda i, j, k: (k, j))
    elif mode == "nt":
        a_spec = pl.BlockSpec((tm, tk), lambda i, j, k: (i, k))
        b_spec = pl.BlockSpec((tn, tk), lambda i, j, k: (j, k))
    else:
        a_spec = pl.BlockSpec((tk, tm), lambda i, j, k: (k, i))
        b_spec = pl.BlockSpec((tk, tn), lambda i, j, k: (k, j))
    in_specs = [a_spec, b_spec]
    for arr, kind in extras:
        if kind == "tile":
            in_specs.append(pl.BlockSpec((tm, tn), lambda i, j, k: (i, j)))
        else:
            in_specs.append(pl.BlockSpec((1, tn), lambda i, j, k: (0, j)))
    out_specs, out_shape = [], []
    for dt, kind in outs:
        if kind == "n" and chunk is not None:
            per = chunk // tn
            out_specs.append(pl.BlockSpec((None, tm, tn), lambda i, j, k: (lax.div(j, per), i, lax.rem(j, per))))
            out_shape.append(jax.ShapeDtypeStruct((N // chunk, M, chunk), dt))
        elif kind == "n":
            out_specs.append(pl.BlockSpec((tm, tn), lambda i, j, k: (i, j)))
            out_shape.append(jax.ShapeDtypeStruct((M, N), dt))
        else:
            assert tn == N
            out_specs.append(pl.BlockSpec((tm, 1), lambda i, j, k: (i, 0)))
            out_shape.append(jax.ShapeDtypeStruct((M, 1), dt))
    ne, no = len(extras), len(outs)
    deps = []
    if dep is not None:
        in_specs.append(_full(dep.shape))
        deps = [dep]
    dims = _DIMS[mode]
    if epilogue is None:
        epilogue = lambda acc: (acc,) * no

    def body(*refs):
        a_ref, b_ref = refs[0], refs[1]
        ex = refs[2:2 + ne]
        out = refs[len(refs) - 1 - no:len(refs) - 1]
        acc = refs[-1]
        k = pl.program_id(2)
        part = lax.dot_general(a_ref[...].astype(BF16), b_ref[...].astype(BF16), dims, preferred_element_type=F32)

        def finish(total):
            res = epilogue(total, *[e[...] for e in ex])
            for o, r in zip(out, res):
                o[...] = r.astype(o.dtype)

        if nk == 1:
            finish(part)
            return

        @pl.when(k == 0)
        def _():
            acc[...] = part

        @pl.when((k > 0) & (k < nk - 1))
        def _():
            acc[...] += part

        @pl.when(k == nk - 1)
        def _():
            finish(acc[...] + part)

    res = pl.pallas_call(
        body, name=name, grid=(M // tm, N // tn, nk), in_specs=in_specs, out_specs=out_specs, out_shape=out_shape,
        scratch_shapes=[pltpu.VMEM((tm, tn), F32)],
        compiler_params=_params(("parallel", "parallel", "arbitrary")),
    )(a, b, *[e[0] for e in extras], *deps)
    return res


def _ln_epilogue(alpha):
    def epi(acc, x, g, b):
        r = alpha * x + acc
        mu = jnp.mean(r, axis=-1, keepdims=True)
        d = r - mu
        var = jnp.mean(d * d, axis=-1, keepdims=True)
        rstd = lax.rsqrt(var + EPS)
        xhat = d * rstd
        y = xhat * g + b
        return y, y, xhat, rstd
    return epi


def _mm_ln(a, w, x, g, b, alpha, name):
    return _mm(a, w, "nn", [(F32, "n"), (BF16, "n"), (F32, "n"), (F32, "1")], name,
               extras=[(x, "tile"), (g, "row"), (b, "row")], epilogue=_ln_epilogue(alpha), full_n=True)


def _mm_axpy(a, w, mode, r, alpha, name, dep=None):
    return _mm(a, w, mode, [(F32, "n")], name, extras=[(r, "tile")],
               epilogue=lambda acc, rv: (acc + alpha * rv,), dep=dep)[0]


def _ln_bwd(dy, xhat, rstd, g, name):
    S, D = dy.shape
    tm = _tile(S, (256, 128))

    def body(dy_ref, xh_ref, rs_ref, g_ref, dr_ref, dr16_ref, dg_ref, db_ref):
        @pl.when(pl.program_id(0) == 0)
        def _():
            dg_ref[...] = jnp.zeros_like(dg_ref)
            db_ref[...] = jnp.zeros_like(db_ref)

        dyv, xh = dy_ref[...], xh_ref[...]
        dxh = dyv * g_ref[...]
        m1 = jnp.mean(dxh, axis=-1, keepdims=True)
        m2 = jnp.mean(dxh * xh, axis=-1, keepdims=True)
        dr = rs_ref[...] * (dxh - m1 - xh * m2)
        dr_ref[...] = dr
        dr16_ref[...] = dr.astype(BF16)
        dg_ref[...] += jnp.sum(dyv * xh, axis=0, keepdims=True)
        db_ref[...] += jnp.sum(dyv, axis=0, keepdims=True)

    row = pl.BlockSpec((tm, D), lambda i: (i, 0))
    return pl.pallas_call(
        body, name=name, grid=(S // tm,),
        in_specs=[row, row, pl.BlockSpec((tm, 1), lambda i: (i, 0)), _full((1, D))],
        out_specs=[row, row, _full((1, D)), _full((1, D))],
        out_shape=[jax.ShapeDtypeStruct((S, D), F32), jax.ShapeDtypeStruct((S, D), BF16),
                   jax.ShapeDtypeStruct((1, D), F32), jax.ShapeDtypeStruct((1, D), F32)],
        compiler_params=_params(("arbitrary",)),
    )(dy, xhat, rstd, g)


def _rms_fwd(h, off, width, g, name):
    S = h.shape[0]
    tm = _tile(S, (ROW_TILE, 256, 128))

    def body(c_ref, g_ref, y_ref, r_ref):
        c = c_ref[...]
        r = lax.rsqrt(jnp.mean(c * c, axis=-1, keepdims=True) + EPS)
        y_ref[...] = (c * r * g_ref[...]).astype(BF16)
        r_ref[...] = r

    return pl.pallas_call(
        body, name=name, grid=(S // tm,),
        in_specs=[_cols(tm, width, off), _full((1, width))],
        out_specs=[pl.BlockSpec((tm, width), lambda i: (i, 0)), pl.BlockSpec((tm, 1), lambda i: (i, 0))],
        out_shape=[jax.ShapeDtypeStruct((S, width), BF16), jax.ShapeDtypeStruct((S, 1), F32)],
        compiler_params=_params(("parallel",)),
    )(h, g)


def _rms_bwd(dy, h, off, width, rstd, g, name):
    S = h.shape[0]
    tm = _tile(S, (ROW_TILE, 256, 128))

    def body(dy_ref, c_ref, r_ref, g_ref, dc_ref, dg_ref):
        @pl.when(pl.program_id(0) == 0)
        def _():
            dg_ref[...] = jnp.zeros_like(dg_ref)

        dyv, c, r = dy_ref[...], c_ref[...], r_ref[...]
        dyg = dyv * g_ref[...]
        m = jnp.mean(dyg * c, axis=-1, keepdims=True)
        dc_ref[...] = (r * dyg - c * (r * r * r) * m).astype(BF16)
        dg_ref[...] += jnp.sum(dyv * c * r, axis=0, keepdims=True)

    return pl.pallas_call(
        body, name=name, grid=(S // tm,),
        in_specs=[pl.BlockSpec((tm, width), lambda i: (i, 0)), _cols(tm, width, off),
                  pl.BlockSpec((tm, 1), lambda i: (i, 0)), _full((1, width))],
        out_specs=[pl.BlockSpec((tm, width), lambda i: (i, 0)), _full((1, width))],
        out_shape=[jax.ShapeDtypeStruct((S, width), BF16), jax.ShapeDtypeStruct((1, width), F32)],
        compiler_params=_params(("arbitrary",)),
    )(dy, h, rstd, g)


def _loss(y, target, name):
    S, D = y.shape
    tm = _tile(S, (256, 128))

    def body(y_ref, t_ref, l_ref, dy_ref):
        @pl.when(pl.program_id(0) == 0)
        def _():
            l_ref[...] = jnp.zeros_like(l_ref)

        err = y_ref[...] - t_ref[...]
        dy_ref[...] = err * (1.0 / D)
        per_tok = jnp.mean(err * err, axis=-1, keepdims=True)
        l_ref[...] += 0.5 * jnp.sum(per_tok, axis=0, keepdims=True)

    row = pl.BlockSpec((tm, D), lambda i: (i, 0))
    return pl.pallas_call(
        body, name=name, grid=(S // tm,), in_specs=[row, row], out_specs=[_full((1, 1)), row],
        out_shape=[jax.ShapeDtypeStruct((1, 1), F32), jax.ShapeDtypeStruct((S, D), F32)],
        compiler_params=_params(("arbitrary",)),
    )(y, target)


def _rope(a1, a2, cs, sn, n, bwd, name):
    S = cs.shape[0]
    tm = _tile(S, (ROW_TILE, 256, 128))
    stacked = not isinstance(a1, tuple)

    def body(a1_ref, a2_ref, c_ref, s_ref, y1_ref, y2_ref):
        if stacked:
            v1 = jnp.sum(a1_ref[...], axis=0)
            v2 = jnp.sum(a2_ref[...], axis=0)
        else:
            v1, v2 = a1_ref[...].astype(F32), a2_ref[...].astype(F32)
        c, s = c_ref[...], s_ref[...]
        if bwd:
            y1_ref[...] = (v1 * c + v2 * s).astype(BF16)
            y2_ref[...] = (v2 * c - v1 * s).astype(BF16)
        else:
            y1_ref[...] = (v1 * c - v2 * s).astype(BF16)
            y2_ref[...] = (v2 * c + v1 * s).astype(BF16)

    row = pl.BlockSpec((tm, n), lambda i: (i, 0))
    if stacked:
        H = a1.shape[0]
        a_specs = [pl.BlockSpec((H, tm, n), lambda i: (0, i, 0))] * 2
        arrs = [a1, a2]
    else:
        a_specs = [_cols(tm, n, a1[1]), _cols(tm, n, a2[1])]
        arrs = [a1[0], a2[0]]
    return pl.pallas_call(
        body, name=name, grid=(S // tm,), in_specs=a_specs + [row, row], out_specs=[row, row],
        out_shape=[jax.ShapeDtypeStruct((S, n), BF16)] * 2,
        compiler_params=_params(("parallel",)),
    )(*arrs, cs, sn)


def _swa_mask(n, rows):
    blk = SWA_BLOCK
    row = lax.broadcasted_iota(jnp.int32, (rows, 2 * blk), 0) & (blk - 1)
    col = lax.broadcasted_iota(jnp.int32, (rows, 2 * blk), 1)
    rel = row + blk - col
    return (rel >= 0) & (rel < blk) & ((n > 0) | (col >= blk))


def _swa_specs(off_q, off_k, off_v, stacked):
    blk, aq, akv = SWA_BLOCK, SWA_Q_HEADS * SWA_HEAD_DIM, SWA_KV_HEADS * SWA_HEAD_DIM
    grp = SWA_Q_HEADS // SWA_KV_HEADS
    assert off_q % aq == 0 and off_k % akv == 0 and off_v % akv == 0 and blk & (blk - 1) == 0
    prev = lambda off: pl.BlockSpec((blk, akv), lambda n: (jnp.maximum(n - 1, 0), off // akv))
    cur = lambda off: pl.BlockSpec((blk, akv), lambda n: (n, off // akv))
    sink = _full((SWA_KV_HEADS, grp * blk, 1)) if stacked else pl.BlockSpec(memory_space=pltpu.SMEM)
    return [sink, _cols(blk, aq, off_q), prev(off_k), cur(off_k), prev(off_v), cur(off_v)]


def _swa_sinks(sinks):
    grp = SWA_Q_HEADS // SWA_KV_HEADS
    return jnp.repeat(sinks.reshape(SWA_KV_HEADS, grp), SWA_BLOCK, axis=1)[:, :, None]


def _swa_stack(x, kv):
    hd, grp = SWA_HEAD_DIM, SWA_Q_HEADS // SWA_KV_HEADS
    return jnp.concatenate([x[:, (kv * grp + g) * hd:(kv * grp + g + 1) * hd] for g in range(grp)], axis=0)


def _swa_fwd(h, off_q, off_k, off_v, sinks, name):
    S = h.shape[0]
    blk, hd, nh, nkv = SWA_BLOCK, SWA_HEAD_DIM, SWA_Q_HEADS, SWA_KV_HEADS
    grp = nh // nkv
    aq = nh * hd
    scale = hd ** -0.5

    def body(sink_ref, q_ref, kp_ref, kc_ref, vp_ref, vc_ref, o_ref, lse_ref):
        valid = _swa_mask(pl.program_id(0), blk)
        q = q_ref[...].astype(BF16)
        k2 = jnp.concatenate([kp_ref[...], kc_ref[...]], axis=0).astype(BF16)
        v2 = jnp.concatenate([vp_ref[...], vc_ref[...]], axis=0).astype(BF16)
        for hh in range(nh):
            kv = hh // grp
            qh = q[:, hh * hd:(hh + 1) * hd]
            kh = k2[:, kv * hd:(kv + 1) * hd]
            vh = v2[:, kv * hd:(kv + 1) * hd]
            s = lax.dot_general(qh, kh, _DIMS["nt"], preferred_element_type=F32) * scale
            s = jnp.where(valid, s, MASK)
            sk = sink_ref[hh]
            m = jnp.maximum(jnp.max(s, axis=1, keepdims=True), sk)
            p = jnp.exp(s - m)
            l = jnp.sum(p, axis=1, keepdims=True) + jnp.exp(sk - m)
            o_ref[:, hh * hd:(hh + 1) * hd] = jnp.dot((p / l).astype(BF16), vh, preferred_element_type=F32)
            lse_ref[:, hh:hh + 1] = m + jnp.log(l)

    return pl.pallas_call(
        body, name=name, grid=(S // blk,), in_specs=_swa_specs(off_q, off_k, off_v, False),
        out_specs=[pl.BlockSpec((blk, aq), lambda n: (n, 0)), pl.BlockSpec((blk, nh), lambda n: (n, 0))],
        out_shape=[jax.ShapeDtypeStruct((S, aq), F32), jax.ShapeDtypeStruct((S, nh), F32)],
        compiler_params=_params(("parallel",)),
    )(sinks, h, h, h, h, h)


def _swa_bwd(h, off_q, off_k, off_v, sinks, dout, lse, name):
    S = h.shape[0]
    blk, hd, nh, nkv = SWA_BLOCK, SWA_HEAD_DIM, SWA_Q_HEADS, SWA_KV_HEADS
    grp = nh // nkv
    aq, akv = nh * hd, nkv * hd
    scale = hd ** -0.5

    def body(sink_ref, q_ref, kp_ref, kc_ref, vp_ref, vc_ref, do_ref, lse_ref, dq_ref, dk_ref, dv_ref, ds_ref):
        n = pl.program_id(0)

        @pl.when(n == 0)
        def _():
            dk_ref[...] = jnp.zeros_like(dk_ref)
            dv_ref[...] = jnp.zeros_like(dv_ref)
            ds_ref[...] = jnp.zeros_like(ds_ref)

        valid = _swa_mask(n, grp * blk)
        q = q_ref[...].astype(BF16)
        k2 = jnp.concatenate([kp_ref[...], kc_ref[...]], axis=0).astype(BF16)
        v2 = jnp.concatenate([vp_ref[...], vc_ref[...]], axis=0).astype(BF16)
        do = do_ref[...]
        lane = lax.broadcasted_iota(jnp.int32, (1, 128), 1)
        dsink = jnp.zeros((1, 128), F32)
        cur = pl.ds(pl.multiple_of(n * blk, blk), blk)
        prev = pl.ds(pl.multiple_of(jnp.maximum(n - 1, 0) * blk, blk), blk)
        for kv in range(nkv):
            kh = k2[:, kv * hd:(kv + 1) * hd]
            vh = v2[:, kv * hd:(kv + 1) * hd]
            qs = _swa_stack(q, kv)
            dos = _swa_stack(do, kv)
            dos16 = dos.astype(BF16)
            lse = jnp.concatenate([lse_ref[:, kv * grp + g:kv * grp + g + 1] for g in range(grp)], axis=0)
            s = lax.dot_general(qs, kh, _DIMS["nt"], preferred_element_type=F32) * scale
            s = jnp.where(valid, s, MASK)
            p = jnp.exp(s - lse)
            p16 = p.astype(BF16)
            o = jnp.dot(p16, vh, preferred_element_type=F32)
            delta = jnp.sum(dos * o, axis=1, keepdims=True)
            dp = lax.dot_general(dos16, vh, _DIMS["nt"], preferred_element_type=F32)
            ds16 = (p * (dp - delta) * scale).astype(BF16)
            dqs = jnp.dot(ds16, kh, preferred_element_type=F32).astype(BF16)
            dk_acc = lax.dot_general(ds16, qs, _DIMS["tn"], preferred_element_type=F32)
            dv_acc = lax.dot_general(p16, dos16, _DIMS["tn"], preferred_element_type=F32)
            dsk = jnp.exp(sink_ref[kv] - lse) * delta
            for g in range(grp):
                hh = kv * grp + g
                dq_ref[:, hh * hd:(hh + 1) * hd] = dqs[g * blk:(g + 1) * blk]
                dsink += jnp.where(lane == hh, -jnp.sum(dsk[g * blk:(g + 1) * blk], axis=0, keepdims=True), 0.0)
            cols = slice(kv * hd, (kv + 1) * hd)
            dk_ref[cur, cols] += dk_acc[blk:]
            dv_ref[cur, cols] += dv_acc[blk:]

            @pl.when(n > 0)
            def _():
                dk_ref[prev, cols] += dk_acc[:blk]
                dv_ref[prev, cols] += dv_acc[:blk]

        ds_ref[...] += dsink

    return pl.pallas_call(
        body, name=name, grid=(S // blk,),
        in_specs=_swa_specs(off_q, off_k, off_v, True) + [pl.BlockSpec((blk, aq), lambda n: (n, 0)),
                                                    pl.BlockSpec((blk, nh), lambda n: (n, 0))],
        out_specs=[pl.BlockSpec((blk, aq), lambda n: (n, 0)), _full((S, akv)), _full((S, akv)), _full((1, 128))],
        out_shape=[jax.ShapeDtypeStruct((S, aq), BF16), jax.ShapeDtypeStruct((S, akv), F32),
                   jax.ShapeDtypeStruct((S, akv), F32), jax.ShapeDtypeStruct((1, 128), F32)],
        compiler_params=_params(("arbitrary",)),
    )(_swa_sinks(sinks), h, h, h, h, h, dout, lse)


def _causal(i, j, t):
    row = i * t + lax.broadcasted_iota(jnp.int32, (t, t), 0)
    col = j * t + lax.broadcasted_iota(jnp.int32, (t, t), 1)
    return col <= row


def _mla_fwd(q, kt, v, name):
    H, S, dq = q.shape
    dv = v.shape[2]
    t = _tile(S, (MLA_FWD_TILE, 512, 256, 128))
    nq = S // t
    hp = 2 if H % 2 == 0 else 1
    scale = (MLA_NOPE + MLA_ROPE) ** -0.5

    def body(q_ref, kt_ref, v_ref, o_ref, lse_ref, *state):
        i = pl.program_id(1)
        for hh in range(hp):
            m_s, l_s, acc_s = state[3 * hh:3 * hh + 3]
            m_s[...] = jnp.full_like(m_s, -jnp.inf)
            l_s[...] = jnp.zeros_like(l_s)
            acc_s[...] = jnp.zeros_like(acc_s)

        def block(j, masked):
            rows = pl.ds(pl.multiple_of(j * t, t), t)
            for hh in range(hp):
                m_s, l_s, acc_s = state[3 * hh:3 * hh + 3]
                s = jnp.dot(q_ref[hh], kt_ref[hh, :, rows], preferred_element_type=F32) * scale
                if masked:
                    s = jnp.where(_causal(0, 0, t), s, MASK)
                m_old = m_s[...]
                m_new = jnp.maximum(m_old, jnp.max(s, axis=1, keepdims=True))
                alpha = jnp.exp(m_old - m_new)
                p = jnp.exp(s - m_new)
                l_s[...] = alpha * l_s[...] + jnp.sum(p, axis=1, keepdims=True)
                acc_s[...] = alpha * acc_s[...] + jnp.dot(p.astype(BF16), v_ref[hh, rows, :], preferred_element_type=F32)
                m_s[...] = m_new

        def full_block(j, carry):
            block(j, False)
            return carry

        lax.fori_loop(0, i, full_block, 0)
        block(i, True)
        for hh in range(hp):
            m_s, l_s, acc_s = state[3 * hh:3 * hh + 3]
            o_ref[hh] = acc_s[...] / l_s[...]
            lse_ref[hh] = m_s[...] + jnp.log(l_s[...])

    tile = lambda d: pl.BlockSpec((hp, t, d), lambda h, i: (h, i, 0))
    return pl.pallas_call(
        body, name=name, grid=(H // hp, nq),
        in_specs=[tile(dq), pl.BlockSpec((hp, dq, S), lambda h, i: (h, 0, 0)), pl.BlockSpec((hp, S, dv), lambda h, i: (h, 0, 0))],
        out_specs=[tile(dv), tile(1)],
        out_shape=[jax.ShapeDtypeStruct((H, S, dv), F32), jax.ShapeDtypeStruct((H, S, 1), F32)],
        scratch_shapes=[pltpu.VMEM((t, 1), F32), pltpu.VMEM((t, 1), F32), pltpu.VMEM((t, dv), F32)] * hp,
        compiler_params=_params(("parallel", "arbitrary")),
    )(q, kt, v)


def _rowdot(a, b, name):
    H, S, d = a.shape
    t = _tile(S, (ROW_TILE, 256, 128))

    def body(a_ref, b_ref, o_ref):
        o_ref[...] = jnp.sum(a_ref[...] * b_ref[...], axis=-1, keepdims=True)

    spec = pl.BlockSpec((1, t, d), lambda h, i: (h, i, 0))
    return pl.pallas_call(
        body, name=name, grid=(H, S // t), in_specs=[spec, spec],
        out_specs=pl.BlockSpec((1, t, 1), lambda h, i: (h, i, 0)),
        out_shape=jax.ShapeDtypeStruct((H, S, 1), F32),
        compiler_params=_params(("parallel", "parallel")),
    )(a, b)


def _mla_bwd(q, k, v, do, lse, delta, name):
    H, S, dq = q.shape
    dv = v.shape[2]
    t = _tile(S, (MLA_TILE, 256, 128))
    nq = S // t
    hp = 2 if H % 2 == 0 else 1
    scale = (MLA_NOPE + MLA_ROPE) ** -0.5

    def body(q_ref, k_ref, v_ref, do_ref, lse_ref, dl_ref, dq_ref, dk_ref, dv_ref, dk_s, dv_s):
        j, i = pl.program_id(1), pl.program_id(2)

        @pl.when((j == 0) & (i == 0))
        def _():
            dq_ref[...] = jnp.zeros_like(dq_ref)

        @pl.when(i == j)
        def _():
            dk_s[...] = jnp.zeros_like(dk_s)
            dv_s[...] = jnp.zeros_like(dv_s)

        def block(masked):
            rows = pl.ds(pl.multiple_of(i * t, t), t)
            for hh in range(hp):
                qv, kv_, vv, dov = q_ref[hh], k_ref[hh], v_ref[hh], do_ref[hh]
                s = lax.dot_general(qv, kv_, _DIMS["nt"], preferred_element_type=F32) * scale
                if masked:
                    s = jnp.where(_causal(0, 0, t), s, MASK)
                p = jnp.exp(s - lse_ref[hh])
                p16 = p.astype(BF16)
                dp = lax.dot_general(dov, vv, _DIMS["nt"], preferred_element_type=F32)
                ds16 = (p * (dp - dl_ref[hh]) * scale).astype(BF16)
                dv_s[hh] += lax.dot_general(p16, dov, _DIMS["tn"], preferred_element_type=F32)
                dk_s[hh] += lax.dot_general(ds16, qv, _DIMS["tn"], preferred_element_type=F32)
                dq_ref[hh, rows, :] += jnp.dot(ds16, kv_, preferred_element_type=F32)

        @pl.when(i == j)
        def _():
            block(True)

        @pl.when(i > j)
        def _():
            block(False)

        @pl.when(i == nq - 1)
        def _():
            dk_ref[...] = dk_s[...]
            dv_ref[...] = dv_s[...]

    q_map = lambda h, j, i: (h, jnp.maximum(i, j), 0)
    kv_map = lambda h, j, i: (h, j, 0)
    return pl.pallas_call(
        body, name=name, grid=(H // hp, nq, nq),
        in_specs=[pl.BlockSpec((hp, t, dq), q_map), pl.BlockSpec((hp, t, dq), kv_map), pl.BlockSpec((hp, t, dv), kv_map),
                  pl.BlockSpec((hp, t, dv), q_map), pl.BlockSpec((hp, t, 1), q_map), pl.BlockSpec((hp, t, 1), q_map)],
        out_specs=[pl.BlockSpec((hp, S, dq), lambda h, j, i: (h, 0, 0)), pl.BlockSpec((hp, t, dq), kv_map),
                   pl.BlockSpec((hp, t, dv), kv_map)],
        out_shape=[jax.ShapeDtypeStruct((H, S, dq), F32), jax.ShapeDtypeStruct((H, S, dq), F32),
                   jax.ShapeDtypeStruct((H, S, dv), F32)],
        scratch_shapes=[pltpu.VMEM((hp, t, dq), F32), pltpu.VMEM((hp, t, dv), F32)],
        compiler_params=_params(("parallel", "arbitrary", "arbitrary")),
    )(q, k, v, do, lse, delta)


def _sgu_norm(hv, lg, lb):
    vg = _gelu(hv)
    mu = jnp.mean(vg, axis=-1, keepdims=True)
    d = vg - mu
    rstd = lax.rsqrt(jnp.mean(d * d, axis=-1, keepdims=True) + EPS)
    xhat = d * rstd
    return xhat, rstd, xhat * lg + lb


def _sgu_fwd(h, off_u, off_v, lg, lb, w16, bt, name):
    S = h.shape[0]
    T, G, C = SGU_CHUNK, SGU_GROUPS, SGU_DIM
    W = G * C

    def body(hu_ref, hv_ref, lg_ref, lb_ref, w_ref, bt_ref, y_ref):
        u = _gelu(hu_ref[...])
        _, _, vn = _sgu_norm(hv_ref[...], lg_ref[...], lb_ref[...])
        vn16 = vn.astype(BF16)
        for g in range(G):
            cols = slice(g * C, (g + 1) * C)
            mixed = jnp.dot(w_ref[g], vn16[:, cols], preferred_element_type=F32) + bt_ref[:, g:g + 1]
            y_ref[:, cols] = (u[:, cols] * mixed).astype(BF16)

    return pl.pallas_call(
        body, name=name, grid=(S // T,),
        in_specs=[_cols(T, W, off_u), _cols(T, W, off_v), _full((1, W)), _full((1, W)), _full((G, T, T)), _full((T, G))],
        out_specs=pl.BlockSpec((T, W), lambda n: (n, 0)),
        out_shape=jax.ShapeDtypeStruct((S, W), BF16),
        compiler_params=_params(("parallel",)),
    )(h, h, lg, lb, w16, bt)


def _sgu_bwd(h, off_u, off_v, lg, lb, w16, bt, dy, name):
    S = h.shape[0]
    T, G, C = SGU_CHUNK, SGU_GROUPS, SGU_DIM
    W = G * C
    nc = S // T

    def body(hu_ref, hv_ref, lg_ref, lb_ref, w_ref, bt_ref, dy_ref, dhu_ref, dhv_ref, dw_ref, db_ref, dlg_ref, dlb_ref,
             dmix_s, dvn_s):
        n = pl.program_id(0)

        @pl.when(n == 0)
        def _():
            dw_ref[...] = jnp.zeros_like(dw_ref)
            dlg_ref[...] = jnp.zeros_like(dlg_ref)
            dlb_ref[...] = jnp.zeros_like(dlb_ref)
            dmix_s[...] = jnp.zeros_like(dmix_s)

        hu, hv, lgv = hu_ref[...], hv_ref[...], lg_ref[...]
        u = _gelu(hu)
        xhat, rstd, vn = _sgu_norm(hv, lgv, lb_ref[...])
        vn16 = vn.astype(BF16)
        dyv = dy_ref[...]
        dmixed = dyv * u
        dmix_s[...] += dmixed
        dmixed16 = dmixed.astype(BF16)
        for g in range(G):
            cols = slice(g * C, (g + 1) * C)
            mixed = jnp.dot(w_ref[g], vn16[:, cols], preferred_element_type=F32) + bt_ref[:, g:g + 1]
            dhu_ref[:, cols] = (dyv[:, cols] * mixed * _gelu_grad(hu[:, cols])).astype(BF16)
            dvn_s[:, cols] = lax.dot_general(w_ref[g], dmixed16[:, cols], _DIMS["tn"], preferred_element_type=F32)
            dw_ref[g] += lax.dot_general(dmixed16[:, cols], vn16[:, cols], _DIMS["nt"], preferred_element_type=F32)
        dvn = dvn_s[...]
        dlg_ref[...] += jnp.sum(dvn * xhat, axis=0, keepdims=True)
        dlb_ref[...] += jnp.sum(dvn, axis=0, keepdims=True)
        dxh = dvn * lgv
        m1 = jnp.mean(dxh, axis=-1, keepdims=True)
        m2 = jnp.mean(dxh * xhat, axis=-1, keepdims=True)
        dvg = rstd * (dxh - m1 - xhat * m2)
        dhv_ref[...] = (dvg * _gelu_grad(hv)).astype(BF16)

        @pl.when(n == nc - 1)
        def _():
            tril = lax.broadcasted_iota(jnp.int32, (T, T), 1) <= lax.broadcasted_iota(jnp.int32, (T, T), 0)
            lane = lax.broadcasted_iota(jnp.int32, (T, 128), 1)
            db = jnp.zeros((T, 128), F32)
            for g in range(G):
                dw_ref[g] = jnp.where(tril, dw_ref[g], 0.0)
                db += jnp.where(lane == g, jnp.sum(dmix_s[:, g * C:(g + 1) * C], axis=1, keepdims=True), 0.0)
            db_ref[...] = db

    row = pl.BlockSpec((T, W), lambda n: (n, 0))
    return pl.pallas_call(
        body, name=name, grid=(nc,),
        in_specs=[_cols(T, W, off_u), _cols(T, W, off_v), _full((1, W)), _full((1, W)), _full((G, T, T)), _full((T, G)), row],
        out_specs=[row, row, _full((G, T, T)), _full((T, 128)), _full((1, W)), _full((1, W))],
        out_shape=[jax.ShapeDtypeStruct((S, W), BF16), jax.ShapeDtypeStruct((S, W), BF16),
                   jax.ShapeDtypeStruct((G, T, T), F32), jax.ShapeDtypeStruct((T, 128), F32),
                   jax.ShapeDtypeStruct((1, W), F32), jax.ShapeDtypeStruct((1, W), F32)],
        scratch_shapes=[pltpu.VMEM((T, W), F32), pltpu.VMEM((T, W), F32)],
        compiler_params=_params(("arbitrary",)),
    )(h, h, lg, lb, w16, bt, dy)


def _merge_fwd(ys, ps, h, bg, name):
    S = h.shape[0]
    D = ps[0].shape[1]
    tm = _tile(S, (512, 256, 128))
    tn = _tile(D, (512, 256, 128))
    nb = len(ys)

    def body(*refs):
        y_refs, p_refs, l_refs = refs[:nb], refs[nb:2 * nb], refs[2 * nb:3 * nb]
        bg_ref, mg_ref, z_ref = refs[3 * nb:]
        acc = jnp.zeros((tm, tn), F32)
        for b in range(nb):
            z = jnp.dot(y_refs[b][...].astype(BF16), p_refs[b][...], preferred_element_type=F32)
            z_ref[b] = z
            acc += _sigmoid(l_refs[b][...] + bg_ref[b:b + 1, :]) * z
        mg_ref[...] = acc.astype(BF16)

    in_specs = [pl.BlockSpec((tm, y.shape[1]), lambda i, j: (i, 0)) for y in ys]
    in_specs += [pl.BlockSpec((p.shape[0], tn), lambda i, j: (0, j)) for p in ps]
    in_specs += [pl.BlockSpec((tm, tn), functools.partial(lambda i, j, b: (i, b * (D // tn) + j), b=b)) for b in range(nb)]
    in_specs += [pl.BlockSpec((nb, tn), lambda i, j: (0, j))]
    return pl.pallas_call(
        body, name=name, grid=(S // tm, D // tn), in_specs=in_specs,
        out_specs=[pl.BlockSpec((tm, tn), lambda i, j: (i, j)), pl.BlockSpec((nb, tm, tn), lambda i, j: (0, i, j))],
        out_shape=[jax.ShapeDtypeStruct((S, D), BF16), jax.ShapeDtypeStruct((nb, S, D), F32)],
        compiler_params=_params(("parallel", "parallel")),
    )(*ys, *ps, *([h] * nb), bg)


def _merge_bwd(dm, z, h, bg, name):
    nb, S, D = z.shape
    tm = _tile(S, (256, 128))
    tn = _tile(D, (512, 256, 128))

    def body(*refs):
        dm_ref, z_ref = refs[0], refs[1]
        l_refs = refs[2:2 + nb]
        bg_ref, dz_ref, dl_ref, dbg_ref = refs[2 + nb:]

        @pl.when(pl.program_id(1) == 0)
        def _():
            dbg_ref[...] = jnp.zeros_like(dbg_ref)

        dmv = dm_ref[...]
        rows = lax.broadcasted_iota(jnp.int32, (SUBLANES, tn), 0)
        dbg = jnp.zeros((SUBLANES, tn), F32)
        for b in range(nb):
            gt = _sigmoid(l_refs[b][...] + bg_ref[b:b + 1, :])
            dz_ref[b] = (dmv * gt).astype(BF16)
            dl = dmv * z_ref[b] * gt * (1.0 - gt)
            dl_ref[b] = dl.astype(BF16)
            dbg += jnp.where(rows == b, jnp.sum(dl, axis=0, keepdims=True), 0.0)
        dbg_ref[...] += dbg

    in_specs = [pl.BlockSpec((tm, tn), lambda j, i: (i, j)), pl.BlockSpec((nb, tm, tn), lambda j, i: (0, i, j))]
    in_specs += [pl.BlockSpec((tm, tn), functools.partial(lambda j, i, b: (i, b * (D // tn) + j), b=b)) for b in range(nb)]
    in_specs += [pl.BlockSpec((nb, tn), lambda j, i: (0, j))]
    blk3 = pl.BlockSpec((nb, tm, tn), lambda j, i: (0, i, j))
    return pl.pallas_call(
        body, name=name, grid=(D // tn, S // tm), in_specs=in_specs,
        out_specs=[blk3, blk3, pl.BlockSpec((SUBLANES, tn), lambda j, i: (0, j))],
        out_shape=[jax.ShapeDtypeStruct((nb, S, D), BF16), jax.ShapeDtypeStruct((nb, S, D), BF16),
                   jax.ShapeDtypeStruct((SUBLANES, D), F32)],
        compiler_params=_params(("parallel", "arbitrary")),
    )(dm, z, *([h] * nb), bg)


def _shift_down(x, halo, k):
    xr = pltpu.roll(x, k, 0)
    hr = pltpu.roll(halo, k, 0)
    rows = lax.broadcasted_iota(jnp.int32, halo.shape, 0)
    top = jnp.where(rows < k, hr, xr[:SUBLANES])
    return jnp.concatenate([top, xr[SUBLANES:]], axis=0)


def _shift_up(x, halo, k):
    tm = x.shape[0]
    xr = pltpu.roll(x, tm - k, 0)
    hr = pltpu.roll(halo, SUBLANES - k, 0)
    rows = lax.broadcasted_iota(jnp.int32, halo.shape, 0)
    bot = jnp.where(rows >= SUBLANES - k, hr, xr[tm - SUBLANES:])
    return jnp.concatenate([xr[:tm - SUBLANES], bot], axis=0)


def _conv_tiles(S, F):
    return _tile(S, (ROW_TILE, 256, 128)), _tile(F, (512, 256, 128))


def _conv_in_specs(tm, tn, F):
    r8 = tm // SUBLANES
    nf = F // tn
    specs = []
    for half in range(2):
        specs.append(pl.BlockSpec((tm, tn), functools.partial(lambda j, i, o: (i, o + j), o=half * nf)))
        specs.append(pl.BlockSpec((SUBLANES, tn), functools.partial(lambda j, i, o: (jnp.maximum(i * r8 - 1, 0), o + j), o=half * nf)))
    for half in range(2):
        specs.append(pl.BlockSpec((3, tn), functools.partial(lambda j, i, o: (0, o + j), o=half * nf)))
        specs.append(pl.BlockSpec((1, tn), functools.partial(lambda j, i, o: (0, o + j), o=half * nf)))
    return specs


def _conv_apply(x, halo, w, b, first):
    halo = jnp.where(first, 0.0, halo)
    x1 = _shift_down(x, halo, 1)
    x2 = _shift_down(x, halo, 2)
    return b + x2 * w[0:1, :] + x1 * w[1:2, :] + x * w[2:3, :], x1, x2


def _glu_fwd(up, cw, cb, name):
    S, F2 = up.shape
    F = F2 // 2
    tm, tn = _conv_tiles(S, F)

    def body(ug, hg, uv, hv, wg, bgr, wv, bvr, a_ref):
        first = pl.program_id(1) == 0
        cg, _, _ = _conv_apply(ug[...], hg[...], wg[...], bgr[...], first)
        cv, _, _ = _conv_apply(uv[...], hv[...], wv[...], bvr[...], first)
        a_ref[...] = (cg * _sigmoid(cg) * cv).astype(BF16)

    return pl.pallas_call(
        body, name=name, grid=(F // tn, S // tm), in_specs=_conv_in_specs(tm, tn, F),
        out_specs=pl.BlockSpec((tm, tn), lambda j, i: (i, j)),
        out_shape=jax.ShapeDtypeStruct((S, F), BF16),
        compiler_params=_params(("parallel", "parallel")),
    )(up, up, up, up, cw, cb, cw, cb)


def _glu_bwd(up, cw, cb, da, name):
    S, F2 = up.shape
    F = F2 // 2
    tm, tn = _conv_tiles(S, F)

    def body(ug, hg, uv, hv, wg, bgr, wv, bvr, da_ref, dg_ref, dv_ref, sg_ref, sv_ref):
        i = pl.program_id(1)

        @pl.when(i == 0)
        def _():
            sg_ref[...] = jnp.zeros_like(sg_ref)
            sv_ref[...] = jnp.zeros_like(sv_ref)

        first = i == 0
        xg, xv = ug[...], uv[...]
        cg, xg1, xg2 = _conv_apply(xg, hg[...], wg[...], bgr[...], first)
        cv, xv1, xv2 = _conv_apply(xv, hv[...], wv[...], bvr[...], first)
        dav = da_ref[...]
        sg = _sigmoid(cg)
        dcv = dav * cg * sg
        dcg = dav * cv * sg * (1.0 + cg * (1.0 - sg))
        dg_ref[...] = dcg
        dv_ref[...] = dcv
        rows = lax.broadcasted_iota(jnp.int32, (SUBLANES, tn), 0)

        def stats(dc, x, x1, x2):
            acc = jnp.zeros((SUBLANES, tn), F32)
            for r, val in enumerate((dc * x2, dc * x1, dc * x, dc)):
                acc += jnp.where(rows == r, jnp.sum(val, axis=0, keepdims=True), 0.0)
            return acc

        sg_ref[...] += stats(dcg, xg, xg1, xg2)
        sv_ref[...] += stats(dcv, xv, xv1, xv2)

    tile = pl.BlockSpec((tm, tn), lambda j, i: (i, j))
    stat = pl.BlockSpec((SUBLANES, tn), lambda j, i: (0, j))
    return pl.pallas_call(
        body, name=name, grid=(F // tn, S // tm), in_specs=_conv_in_specs(tm, tn, F) + [tile],
        out_specs=[tile, tile, stat, stat],
        out_shape=[jax.ShapeDtypeStruct((S, F), F32), jax.ShapeDtypeStruct((S, F), F32),
                   jax.ShapeDtypeStruct((SUBLANES, F), F32), jax.ShapeDtypeStruct((SUBLANES, F), F32)],
        compiler_params=_params(("parallel", "arbitrary")),
    )(up, up, up, up, cw, cb, cw, cb, da)


def _conv_bwd(dcg, dcv, w, name):
    S, F = dcg.shape
    tm, tn = _conv_tiles(S, F)
    r8 = tm // SUBLANES
    ni = S // tm
    nf = F // tn

    def body(g_ref, gh_ref, v_ref, vh_ref, w_ref, o_ref):
        gate = pl.program_id(0) == 0
        x = jnp.where(gate, g_ref[...], v_ref[...])
        halo = jnp.where(gate, gh_ref[...], vh_ref[...])
        halo = jnp.where(pl.program_id(2) == ni - 1, 0.0, halo)
        wv = w_ref[...]
        o_ref[...] = (x * wv[2:3, :] + _shift_up(x, halo, 1) * wv[1:2, :] + _shift_up(x, halo, 2) * wv[0:1, :]).astype(BF16)

    def tile(half):
        return pl.BlockSpec((tm, tn), lambda h, j, i: (jnp.where(h == half, i, 0), jnp.where(h == half, j, 0)))

    def below(half):
        return pl.BlockSpec((SUBLANES, tn), lambda h, j, i: (
            jnp.where(h == half, jnp.minimum((i + 1) * r8, S // SUBLANES - 1), 0), jnp.where(h == half, j, 0)))

    return pl.pallas_call(
        body, name=name, grid=(2, nf, ni),
        in_specs=[tile(0), below(0), tile(1), below(1), pl.BlockSpec((3, tn), lambda h, j, i: (0, h * nf + j))],
        out_specs=pl.BlockSpec((tm, tn), lambda h, j, i: (i, h * nf + j)),
        out_shape=jax.ShapeDtypeStruct((S, 2 * F), BF16),
        compiler_params=_params(("parallel", "parallel", "parallel")),
    )(dcg, dcg, dcv, dcv, w)


def _adamw(slot_list, own_list, me, w, m, v, name, dep=None):
    L = len(slot_list)
    P, K, C = slot_list[0].shape
    tr = _tile(K, (256, 128, 64, 32, 16))
    while tr * C * 4 > (1 << 20) and tr % 32 == 0:
        tr //= 2
    nb = K // tr
    has_own = own_list is not None

    def body(me_ref, *refs):
        s_refs = refs[:L]
        o_refs = refs[L:2 * L] if has_own else None
        w_ref, m_ref, v_ref = refs[L * (1 + has_own):L * (1 + has_own) + 3]
        g_ref, d_ref, nm_ref, nv_ref = refs[-4:]
        layer = pl.program_id(0)
        g = None
        for l in range(L):
            gl = None
            for p in range(P):
                term = s_refs[l][p].astype(F32)
                if has_own:
                    term = jnp.where(me_ref[0] == p, o_refs[l][0].astype(F32), term)
                gl = term if gl is None else gl + term
            g = gl if g is None else jnp.where(layer == l, gl, g)
        nm = ADAM_B1 * m_ref[...] + (1.0 - ADAM_B1) * g
        nv = ADAM_B2 * v_ref[...] + (1.0 - ADAM_B2) * (g * g)
        m_hat = nm / (1.0 - ADAM_B1 ** ADAM_STEP)
        v_hat = nv / (1.0 - ADAM_B2 ** ADAM_STEP)
        g_ref[...] = g
        d_ref[...] = -ADAM_LR * (m_hat / (jnp.sqrt(v_hat) + ADAM_EPS) + ADAM_WD * w_ref[...])
        nm_ref[...] = nm
        nv_ref[...] = nv

    blk = pl.BlockSpec((None, tr, C), lambda li, i, me_ref: (li, i, 0))
    specs = [pl.BlockSpec((P, tr, C), functools.partial(lambda li, i, me_ref, l: (0, jnp.where(li == l, i, 0), 0), l=l))
             for l in range(L)]
    if has_own:
        specs += [pl.BlockSpec((1, tr, C), functools.partial(lambda li, i, me_ref, l: (me_ref[0], jnp.where(li == l, i, 0), 0), l=l))
                  for l in range(L)]
    return pl.pallas_call(
        body, name=name,
        grid_spec=pltpu.PrefetchScalarGridSpec(
            num_scalar_prefetch=1, grid=(L, nb), in_specs=specs + [blk, blk, blk] + [_ANY] * (dep is not None),
            out_specs=[blk] * 4),
        out_shape=[jax.ShapeDtypeStruct((L, K, C), F32)] * 4,
        compiler_params=_params(("arbitrary", "arbitrary")),
    )(me, *slot_list, *(own_list if has_own else []), w, m, v, *([] if dep is None else [dep]))


_HBM = pl.BlockSpec(memory_space=pltpu.HBM)
_SEM = pl.BlockSpec(memory_space=pltpu.SEMAPHORE)
_ANY = pl.BlockSpec(memory_space=pl.ANY)


def _peers():
    x, y, c = lax.axis_index("x"), lax.axis_index("y"), lax.axis_index("c")

    def flip(v, bit):
        return 1 - v if bit else v

    def peer(k):
        return (flip(x, (k >> 2) & 1), flip(y, (k >> 1) & 1), flip(c, k & 1))

    def peer_index(k):
        px, py, pc = peer(k)
        return 4 * px + 2 * py + pc

    return 4 * x + 2 * y + c, peer, peer_index


def _split_copy(src_refs, land_refs, send_sems, recv_sems, scatter, a, k, outgoing):
    me, peer, peer_index = _peers()
    if outgoing:
        src = src_refs[a].at[peer_index(k)] if scatter else src_refs[a]
        dst = land_refs[a].at[me]
    else:
        src = src_refs[a].at[me] if scatter else src_refs[a]
        dst = land_refs[a].at[peer_index(k)]
    pair = a * (N_DEV - 1) + k - 1
    return pltpu.make_async_remote_copy(src_ref=src, dst_ref=dst, send_sem=send_sems.at[pair],
                                        recv_sem=recv_sems.at[pair], device_id=peer(k),
                                        device_id_type=pl.DeviceIdType.MESH)


def _gather_two_level(srcs, name):
    na = len(srcs)

    def body(*refs):
        src_refs, out_refs = refs[:na], refs[na:2 * na]
        send_sems, recv_sems = refs[2 * na:]
        x, y, c = lax.axis_index("x"), lax.axis_index("y"), lax.axis_index("c")
        me, sibling = (x, y, c), (x, y, 1 - c)
        chips = [(1 - x, y), (x, 1 - y), (1 - x, 1 - y)]

        def copy(a, k, block, to, src=None):
            px, py, pc = block
            slot = out_refs[a].at[4 * px + 2 * py + pc]
            return pltpu.make_async_remote_copy(
                src_ref=slot if src is None else src, dst_ref=slot, send_sem=send_sems.at[a * (N_DEV - 1) + k],
                recv_sem=recv_sems.at[a * (N_DEV - 1) + k], device_id=to, device_id_type=pl.DeviceIdType.MESH)

        first = [copy(a, 0, me, sibling, src_refs[a]) for a in range(na)]
        first += [copy(a, 1 + j, me, (*chip, c), src_refs[a]) for j, chip in enumerate(chips) for a in range(na)]
        for cp in first:
            cp.start()
        passed = []
        for j, chip in enumerate(chips):
            for a in range(na):
                copy(a, 1 + j, (*chip, c), me).wait_recv()
                passed.append(copy(a, 4 + j, (*chip, c), sibling))
                passed[-1].start()
        for a in range(na):
            copy(a, 0, sibling, me).wait_recv()
        for j, chip in enumerate(chips):
            for a in range(na):
                copy(a, 4 + j, (*chip, 1 - c), me).wait_recv()
        for cp in first + passed:
            cp.wait_send()

    return pl.pallas_call(
        body, name=name, in_specs=[_ANY] * na, out_specs=[_ANY] * na,
        out_shape=[jax.ShapeDtypeStruct((N_DEV,) + s.shape, s.dtype) for s in srcs],
        scratch_shapes=[pltpu.SemaphoreType.DMA((na * (N_DEV - 1),)), pltpu.SemaphoreType.DMA((na * (N_DEV - 1),))],
    )(*srcs)


def _exchange_start(srcs, scatter, after, name):
    na = len(srcs)
    land_shapes = [s.shape if scatter else (N_DEV,) + s.shape for s in srcs]
    has_after = after is not None

    def body(*refs):
        src_refs, land_refs = refs[:na], refs[na:2 * na]
        send_sems, recv_sems = refs[2 * na + has_after], refs[2 * na + has_after + 1]
        token = refs[-1]
        for k in range(1, N_DEV):
            for a in range(na):
                _split_copy(src_refs, land_refs, send_sems, recv_sems, scatter, a, k, True).start()
        token[...] = jnp.zeros_like(token)

    sems = pltpu.SemaphoreType.DMA((na * (N_DEV - 1),))
    out_shape = ([sems, sems] + [pltpu.HBM(s.shape, s.dtype) for s in srcs]
                 + [pltpu.HBM(shp, s.dtype) for shp, s in zip(land_shapes, srcs)] + [jax.ShapeDtypeStruct((SUBLANES, 128), F32)])
    args = [pltpu.with_memory_space_constraint(s, pltpu.HBM) for s in srcs]
    args += [pltpu.with_memory_space_constraint(lax.empty(shp, s.dtype), pltpu.HBM) for shp, s in zip(land_shapes, srcs)]
    if has_after:
        args.append(after)
    res = pl.pallas_call(
        body, name=name, in_specs=[_HBM] * (2 * na) + [_ANY] * has_after,
        out_specs=[_SEM, _SEM] + [_HBM] * (2 * na) + [pl.BlockSpec(memory_space=pltpu.VMEM)], out_shape=out_shape,
        input_output_aliases={i: 2 + i for i in range(2 * na)},
        compiler_params=pltpu.CompilerParams(has_side_effects=pltpu.SideEffectType.DATAFLOW_SIDE_EFFECTING),
    )(*args)
    handle = dict(send=res[0], recv=res[1], srcs=list(res[2:2 + na]), lands=list(res[2 + na:2 + 2 * na]), scatter=scatter)
    return handle, res[-1]


def _exchange_wait(handle, after, name):
    srcs, lands, scatter = handle["srcs"], handle["lands"], handle["scatter"]
    na = len(srcs)

    def body(*refs):
        src_refs, land_refs = refs[:na], refs[na:2 * na]
        send_sems, recv_sems = refs[2 * na], refs[2 * na + 1]
        for k in range(1, N_DEV):
            for a in range(na):
                _split_copy(src_refs, land_refs, send_sems, recv_sems, scatter, a, k, True).wait_send()
                _split_copy(src_refs, land_refs, send_sems, recv_sems, scatter, a, k, False).wait_recv()

    res = pl.pallas_call(
        body, name=name, in_specs=[_HBM] * (2 * na) + [_SEM, _SEM, _ANY], out_specs=[_HBM] * (2 * na),
        out_shape=[pltpu.HBM(s.shape, s.dtype) for s in srcs] + [pltpu.HBM(s.shape, s.dtype) for s in lands],
        input_output_aliases={i: i for i in range(2 * na)},
        compiler_params=pltpu.CompilerParams(has_side_effects=pltpu.SideEffectType.DATAFLOW_SIDE_EFFECTING),
    )(*srcs, *lands, handle["send"], handle["recv"], after)
    return list(res[:na]), list(res[na:])


def _layout(D):
    aq, akv = SWA_Q_HEADS * SWA_HEAD_DIM, SWA_KV_HEADS * SWA_HEAD_DIM
    w = SGU_GROUPS * SGU_DIM
    return aq, akv, w


class _Seg:
    def __init__(self, D, rq, rkv):
        aq, akv, w = _layout(D)
        src = {}
        o = 0
        for nm, wd in (("qa", aq), ("ka", akv), ("va", akv), ("cq", rq), ("ckv", rkv), ("kr", MLA_ROPE), ("hu", w), ("hv", w),
                       ("g", 3 * D)):
            src[nm] = (o, wd)
            o += wd
        self.n_in = o
        self.order = ("g", "qa", "hu", "hv", "cq", "ckv", "ka", "va", "kr")
        self.src = src
        self.off = {}
        o = 0
        for nm in self.order:
            self.off[nm] = o
            o += src[nm][1]
        self.width = {nm: src[nm][1] for nm in self.order}
        self.n_pad = -(-o // 1536) * 1536 if o > 1536 else -(-o // 512) * 512
        self.used = o

    def permute(self, w):
        parts = [w[:, self.src[nm][0]:self.src[nm][0] + self.src[nm][1]] for nm in self.order]
        parts.append(jnp.zeros((w.shape[0], self.n_pad - self.used), w.dtype))
        return jnp.concatenate(parts, axis=1)

    def unpermute(self, w):
        names = sorted(self.order, key=lambda nm: self.src[nm][0])
        return jnp.concatenate([w[:, self.off[nm]:self.off[nm] + self.width[nm]] for nm in names], axis=1)


def _uq_permute(w):
    R = w.shape[0]
    H, half = MLA_HEADS, MLA_ROPE // 2
    w3 = w.reshape(R, H, MLA_NOPE + MLA_ROPE)
    return jnp.concatenate([w3[:, :, :MLA_NOPE].reshape(R, H * MLA_NOPE),
                            w3[:, :, MLA_NOPE:MLA_NOPE + half].reshape(R, H * half),
                            w3[:, :, MLA_NOPE + half:].reshape(R, H * half)], axis=1)


def _uq_unpermute(w):
    R = w.shape[0]
    H, half = MLA_HEADS, MLA_ROPE // 2
    n = w[:, :H * MLA_NOPE].reshape(R, H, MLA_NOPE)
    r1 = w[:, H * MLA_NOPE:H * (MLA_NOPE + half)].reshape(R, H, half)
    r2 = w[:, H * (MLA_NOPE + half):].reshape(R, H, half)
    return jnp.concatenate([n, r1, r2], axis=2).reshape(R, H * (MLA_NOPE + MLA_ROPE))


def _ukv_permute(w):
    R = w.shape[0]
    w3 = w.reshape(R, MLA_HEADS, MLA_NOPE + MLA_V)
    return jnp.concatenate([w3[:, :, :MLA_NOPE].reshape(R, -1), w3[:, :, MLA_NOPE:].reshape(R, -1)], axis=1)


def _ukv_unpermute(w):
    R = w.shape[0]
    H = MLA_HEADS
    k = w[:, :H * MLA_NOPE].reshape(R, H, MLA_NOPE)
    v = w[:, H * MLA_NOPE:].reshape(R, H, MLA_V)
    return jnp.concatenate([k, v], axis=2).reshape(R, H * (MLA_NOPE + MLA_V))


def _heads(a, d):
    S = a.shape[0]
    return a.reshape(S, MLA_HEADS, d).transpose(1, 0, 2)


def _unheads(a):
    H, S, d = a.shape
    return a.transpose(1, 0, 2).reshape(S, H * d)


GROUPS = {"a": ("w_in",), "b": ("w_uq", "w_ukv", "w_proj_a", "w_proj_b", "w_proj_c", "w_o", "b_gate"),
          "c": ("w_up", "w_down", "conv_w")}


def _layer_fwd(l, x, x16, fetch, P, cs, sn, seg, alpha):
    S, D = x.shape
    H, half = MLA_HEADS, MLA_ROPE // 2
    off = seg.off
    nm = lambda s: f"l{l}_{s}"
    sv = {"x16": x16}
    W = {"w_in": seg.permute(fetch(l, "a", x16)["w_in"])}
    h = _mm(x16, W["w_in"], "nn", [(F32, "n")], nm("h"))[0]
    sv["h"] = h
    ya, lse_a = _swa_fwd(h, off["qa"], off["ka"], off["va"], P["sinks"], nm("swa_fwd"))
    cqn, rq = _rms_fwd(h, off["cq"], seg.width["cq"], P["q_norm_g"], nm("rmsq_fwd"))
    ckvn, rkv = _rms_fwd(h, off["ckv"], seg.width["ckv"], P["kv_norm_g"], nm("rmskv_fwd"))
    W.update(fetch(l, "b", cqn))
    W["w_uq"] = _uq_permute(W["w_uq"])
    W["w_ukv"] = _ukv_permute(W["w_ukv"])
    qf = _mm(cqn, W["w_uq"], "nn", [(F32, "n")], nm("uq"))[0]
    kvf = _mm(ckvn, W["w_ukv"], "nn", [(BF16, "n")], nm("ukv"))[0]
    cs_h, sn_h = jnp.tile(cs, (1, H)), jnp.tile(sn, (1, H))
    qy1, qy2 = _rope((qf, H * MLA_NOPE), (qf, H * MLA_NOPE + H * half), cs_h, sn_h, H * half, False, nm("ropeq_fwd"))
    kr = h[:, off["kr"]:off["kr"] + MLA_ROPE]
    ky1, ky2 = _rope((kr[:, :half], 0), (kr[:, half:], 0), cs, sn, half, False, nm("ropek_fwd"))
    qh = jnp.concatenate([qf[:, :H * MLA_NOPE].astype(BF16).reshape(S, H, MLA_NOPE), qy1.reshape(S, H, half),
                          qy2.reshape(S, H, half)], axis=2).transpose(1, 0, 2)
    kh = jnp.concatenate([kvf[:, :H * MLA_NOPE].reshape(S, H, MLA_NOPE),
                          jnp.broadcast_to(ky1[:, None, :], (S, H, half)),
                          jnp.broadcast_to(ky2[:, None, :], (S, H, half))], axis=2).transpose(1, 0, 2)
    vh = _heads(kvf[:, H * MLA_NOPE:], MLA_V)
    ob, lse_b = _mla_fwd(qh, kh.transpose(0, 2, 1), vh, nm("mla_fwd"))
    yb = _unheads(ob).astype(BF16)
    w16 = jnp.where(jnp.tril(jnp.ones((SGU_CHUNK, SGU_CHUNK), bool))[None], P["sgu_w"], 0.0).astype(BF16)
    bt = P["sgu_b"].T
    yc = _sgu_fwd(h, off["hu"], off["hv"], P["sgu_ln_g"], P["sgu_ln_b"], w16, bt, nm("sgu_fwd"))
    merged, z = _merge_fwd([ya, yb, yc], [W["w_proj_a"], W["w_proj_b"], W["w_proj_c"]], h, W["b_gate"], nm("merge_fwd"))
    x1, x1_16, xh1, rs1 = _mm_ln(merged, W["w_o"], x, P["ln1_g"], P["ln1_b"], alpha, nm("wo_ln1"))
    W.update(fetch(l, "c", x1_16))
    up = _mm(x1_16, W["w_up"], "nn", [(F32, "n")], nm("up"))[0]
    a = _glu_fwd(up, W["conv_w"], P["conv_b"], nm("glu_fwd"))
    x2, x2_16, xh2, rs2 = _mm_ln(a, W["w_down"], x1, P["ln2_g"], P["ln2_b"], alpha, nm("down_ln2"))
    sv.update(W=W, ya=ya, lse_a=lse_a, cqn=cqn, rq=rq, ckvn=ckvn, rkv=rkv, qh=qh, kh=kh, vh=vh, ob=ob, lse_b=lse_b, yb=yb,
              w16=w16, bt=bt, yc=yc, merged=merged, z=z, x1_16=x1_16, xh1=xh1, rs1=rs1, up=up, a=a, xh2=xh2, rs2=rs2,
              cs_h=cs_h, sn_h=sn_h)
    return x2, x2_16, sv


def _dw_chunks(k, a, dy, name, post=None):
    n = dy.shape[1]
    if k not in ROW_SHARDED and post is None and (n // N_DEV) % 128 == 0:
        return _mm(a, dy, "tn", [(BF16, "n")], name, chunk=n // N_DEV)[0]
    g = _mm(a, dy, "tn", [(BF16, "n")], name)[0]
    return _to_chunks(k, g if post is None else post(g))


def _after(arr, token):
    return arr if token is None else arr + token[0:1, 0:1].astype(arr.dtype)


def _layer_bwd(l, dx2, sv, P, cs, sn, seg, alpha, emit):
    S, D = dx2.shape
    H, half = MLA_HEADS, MLA_ROPE // 2
    off = seg.off
    h, W = sv["h"], sv["W"]
    nm = lambda s: f"l{l}_{s}"
    g = {}
    dr2, dr2_16, g["ln2_g"], g["ln2_b"] = _ln_bwd(dx2, sv["xh2"], sv["rs2"], P["ln2_g"], nm("ln2_bwd"))
    g["w_down"] = _dw_chunks("w_down", sv["a"], dr2_16, nm("dw_down"))
    da = _mm(dr2_16, W["w_down"], "nt", [(F32, "n")], nm("da"))[0]
    dcg, dcv, st_g, st_v = _glu_bwd(sv["up"], W["conv_w"], P["conv_b"], da, nm("glu_bwd"))
    F = dcg.shape[1]
    g["conv_w"] = _to_chunks("conv_w", jnp.concatenate([st_g[0:3], st_v[0:3]], axis=1))
    g["conv_b"] = jnp.concatenate([st_g[3:4], st_v[3:4]], axis=1)
    dup = _conv_bwd(dcg, dcv, W["conv_w"], nm("conv_bwd"))
    g["w_up"] = _dw_chunks("w_up", sv["x1_16"], dup, nm("dw_up"))
    token = emit(l, "c", {k: g.pop(k) for k in GROUPS["c"]})
    dx1 = _mm_axpy(dup, W["w_up"], "nt", dr2, alpha, nm("dx1"), dep=token)
    dr1, dr1_16, g["ln1_g"], g["ln1_b"] = _ln_bwd(dx1, sv["xh1"], sv["rs1"], P["ln1_g"], nm("ln1_bwd"))
    g["w_o"] = _dw_chunks("w_o", sv["merged"], dr1_16, nm("dw_o"))
    dmerged = _mm(dr1_16, W["w_o"], "nt", [(F32, "n")], nm("dmerged"))[0]
    dz, dlog, dbg = _merge_bwd(dmerged, sv["z"], h, W["b_gate"], nm("merge_bwd"))
    g["b_gate"] = _to_chunks("b_gate", dbg[0:3])
    g["w_proj_a"] = _dw_chunks("w_proj_a", sv["ya"], dz[0], nm("dw_pa"))
    g["w_proj_b"] = _dw_chunks("w_proj_b", sv["yb"], dz[1], nm("dw_pb"))
    g["w_proj_c"] = _dw_chunks("w_proj_c", sv["yc"], dz[2], nm("dw_pc"))
    dya = _mm(dz[0], W["w_proj_a"], "nt", [(F32, "n")], nm("dya"))[0]
    dyb = _mm(dz[1], W["w_proj_b"], "nt", [(F32, "n")], nm("dyb"))[0]
    dyc = _mm(dz[2], W["w_proj_c"], "nt", [(F32, "n")], nm("dyc"))[0]
    dhu, dhv, g["sgu_w"], db_s, g["sgu_ln_g"], g["sgu_ln_b"] = _sgu_bwd(
        h, off["hu"], off["hv"], P["sgu_ln_g"], P["sgu_ln_b"], sv["w16"], sv["bt"], dyc, nm("sgu_bwd"))
    g["sgu_b"] = db_s[:, :SGU_GROUPS].T
    dqa, dka, dva, dsk = _swa_bwd(h, off["qa"], off["ka"], off["va"], P["sinks"], dya, sv["lse_a"], nm("swa_bwd"))
    g["sinks"] = dsk[0, :SWA_Q_HEADS]
    dob = _heads(dyb, MLA_V)
    delta = _rowdot(dob, sv["ob"], nm("mla_delta"))
    dqh, dkh, dvh = _mla_bwd(sv["qh"], sv["kh"], sv["vh"], dob.astype(BF16), sv["lse_b"], delta, nm("mla_bwd"))
    n0, n1 = MLA_NOPE, MLA_NOPE + half
    dqx1, dqx2 = _rope((_unheads(dqh[:, :, n0:n1]), 0), (_unheads(dqh[:, :, n1:]), 0), sv["cs_h"], sv["sn_h"], H * half, True,
                       nm("ropeq_bwd"))
    dqf = jnp.concatenate([_unheads(dqh[:, :, :n0]).astype(BF16), dqx1, dqx2], axis=1)
    dkvf = jnp.concatenate([_unheads(dkh[:, :, :n0]), _unheads(dvh)], axis=1).astype(BF16)
    dk1, dk2 = _rope(dkh[:, :, n0:n1], dkh[:, :, n1:], cs, sn, half, True, nm("ropek_bwd"))
    g["w_uq"] = _dw_chunks("w_uq", sv["cqn"], dqf, nm("dw_uq"), _uq_unpermute)
    g["w_ukv"] = _dw_chunks("w_ukv", sv["ckvn"], dkvf, nm("dw_ukv"), _ukv_unpermute)
    token = emit(l, "b", {k: g.pop(k) for k in GROUPS["b"]})
    dcqn = _mm(dqf, W["w_uq"], "nt", [(F32, "n")], nm("dcqn"), dep=token)[0]
    dckvn = _mm(dkvf, W["w_ukv"], "nt", [(F32, "n")], nm("dckvn"), dep=token)[0]
    dcq, g["q_norm_g"] = _rms_bwd(dcqn, h, off["cq"], seg.width["cq"], sv["rq"], P["q_norm_g"], nm("rmsq_bwd"))
    dckv, g["kv_norm_g"] = _rms_bwd(dckvn, h, off["ckv"], seg.width["ckv"], sv["rkv"], P["kv_norm_g"], nm("rmskv_bwd"))
    parts = {"g": jnp.concatenate([dlog[0], dlog[1], dlog[2]], axis=1), "qa": dqa, "hu": dhu, "hv": dhv, "cq": dcq, "ckv": dckv,
             "ka": dka.astype(BF16), "va": dva.astype(BF16), "kr": jnp.concatenate([dk1, dk2], axis=1)}
    dh = jnp.concatenate([parts[k] for k in seg.order] + [jnp.zeros((S, seg.n_pad - seg.used), BF16)], axis=1)
    token = emit(l, "a", {"w_in": _dw_chunks("w_in", sv["x16"], dh, nm("dw_in"), seg.unpermute)})
    dx = _mm_axpy(dh, W["w_in"], "nt", dr1, alpha, nm("dx"), dep=token)
    return dx, g


BIG = ("w_in", "w_uq", "w_ukv", "w_proj_a", "w_proj_b", "w_proj_c", "w_o", "w_up", "w_down")
ROW_SHARDED = ("w_proj_b", "w_o", "w_down")
SHARDED_F32 = ("b_gate", "conv_w")
REPLICATED = ("sinks", "q_norm_g", "kv_norm_g", "sgu_ln_g", "sgu_ln_b", "sgu_w", "sgu_b", "ln1_g", "ln1_b", "conv_b", "ln2_g",
              "ln2_b")
WEIGHTS = ("w_in", "b_gate", "sinks", "q_norm_g", "kv_norm_g", "w_uq", "w_ukv", "sgu_ln_g", "sgu_ln_b", "sgu_w", "sgu_b",
           "w_proj_a", "w_proj_b", "w_proj_c", "w_o", "ln1_g", "ln1_b", "w_up", "conv_w", "conv_b", "w_down", "ln2_g", "ln2_b")


def _step_local(x, positions, target, small, rq, rkv, fetch, emit, token=None):
    S, D = x.shape
    L = small["sinks"].shape[0]
    alpha = (2 * L) ** 0.25
    seg = _Seg(D, rq, rkv)
    inv_freq = ROPE_THETA ** (-jnp.arange(0, MLA_ROPE, 2, dtype=F32) / MLA_ROPE)
    ang = positions.astype(F32)[:, None] * inv_freq
    cs, sn = jnp.cos(ang), jnp.sin(ang)
    rows = ("q_norm_g", "kv_norm_g", "sgu_ln_g", "sgu_ln_b", "ln1_g", "ln1_b", "conv_b", "ln2_g", "ln2_b")
    layers = [{k: small[k][l].reshape(1, -1) if k in rows else small[k][l] for k in small} for l in range(L)]
    saved = []
    x16 = _after(x, token).astype(BF16)
    for l in range(L):
        x, x16, sv = _layer_fwd(l, x, x16, fetch, layers[l], cs, sn, seg, alpha)
        saved.append(sv)
    loss, dx = _loss(x, target, "loss")
    grads = [None] * L
    for l in reversed(range(L)):
        dx, grads[l] = _layer_bwd(l, dx, saved[l], layers[l], cs, sn, seg, alpha, emit)
    out = {k: jnp.stack([grads[l][k].reshape(small[k].shape[1:]) for l in range(L)]) for k in small}
    return loss, dx, out


def _unshard(k, gathered):
    n, r, c = gathered.shape
    if k in ROW_SHARDED:
        return gathered.reshape(n * r, c)
    return gathered.transpose(1, 0, 2).reshape(r, n * c)


def _to_chunks(k, gfull):
    r, c = gfull.shape
    if k in ROW_SHARDED:
        return gfull.reshape(N_DEV, r // N_DEV, c)
    return gfull.reshape(r, N_DEV, c // N_DEV).transpose(1, 0, 2)


def _pack(arrs):
    P = arrs[0].shape[0]
    flat, sizes = [], []
    for a in arrs:
        f = a.reshape(P, -1)
        n = f.shape[1]
        pad = -n % (SUBLANES * 128)
        flat.append(jnp.pad(f, ((0, 0), (0, pad))))
        sizes.append((n, n + pad))
    return jnp.concatenate(flat, axis=1).reshape(P, -1, 128), sizes


def _unpack(packed, sizes, shapes):
    flat = packed.reshape(-1)
    out, o = [], 0
    for (n, npad), shp in zip(sizes, shapes):
        out.append(flat[o:o + n].reshape(shp))
        o += npad
    return out


def kernel(x, positions, w_in, b_gate, sinks, q_norm_g, kv_norm_g, w_uq, w_ukv, sgu_ln_g, sgu_ln_b, sgu_w, sgu_b, w_proj_a, w_proj_b, w_proj_c, w_o, ln1_g, ln1_b, w_up, conv_w, conv_b, w_down, ln2_g, ln2_b, loss_target, m_w_in, m_b_gate, m_sinks, m_q_norm_g, m_kv_norm_g, m_w_uq, m_w_ukv, m_sgu_ln_g, m_sgu_ln_b, m_sgu_w, m_sgu_b, m_w_proj_a, m_w_proj_b, m_w_proj_c, m_w_o, m_ln1_g, m_ln1_b, m_w_up, m_conv_w, m_conv_b, m_w_down, m_ln2_g, m_ln2_b, v_w_in, v_b_gate, v_sinks, v_q_norm_g, v_kv_norm_g, v_w_uq, v_w_ukv, v_sgu_ln_g, v_sgu_ln_b, v_sgu_w, v_sgu_b, v_w_proj_a, v_w_proj_b, v_w_proj_c, v_w_o, v_ln1_g, v_ln1_b, v_w_up, v_conv_w, v_conv_b, v_w_down, v_ln2_g, v_ln2_b):
    given = dict(locals())
    w = {k: given[k] for k in WEIGHTS}
    mom = {k: given["m_" + k] for k in WEIGHTS}
    var = {k: given["v_" + k] for k in WEIGHTS}

    L = w_in.shape[0]
    order = [(l, grp) for l in range(L) for grp in ("a", "b", "c")]

    first = [w[k][0].astype(BF16) for k in GROUPS["a"]]
    first_lands = _gather_two_level(first, "gather_first")
    gathers, token = {}, first_lands[0]
    for l, grp in order[1:]:
        srcs = [w[k][l].astype(BF16) if k in BIG else w[k][l] for k in GROUPS[grp]]
        gathers[l, grp], token = _exchange_start(srcs, False, token, f"gather_start_l{l}{grp}")

    me = 4 * lax.axis_index("x") + 2 * lax.axis_index("y") + lax.axis_index("c")
    mine = (jnp.arange(N_DEV) == me)[:, None, None]

    def fetch(l, grp, after):
        if (l, grp) == order[0]:
            srcs, lands = first, first_lands
        else:
            srcs, lands = _exchange_wait(gathers[l, grp], after, f"gather_wait_l{l}{grp}")
        return {k: _unshard(k, jnp.where(mine, srcs[i][None], lands[i])) for i, k in enumerate(GROUPS[grp])}

    scatters = {}

    def emit(l, grp, chunks):
        scatters[l, grp], tok = _exchange_start([chunks[k] for k in GROUPS[grp]], True, None, f"scatter_start_l{l}{grp}")
        return tok

    small = {k: w[k] for k in REPLICATED}
    loss, grad_x, g = _step_local(x[0], positions[0], loss_target[0], small, w_uq.shape[1], w_ukv.shape[1], fetch, emit, token)
    loss = lax.psum(loss[0, 0], AXES)

    packed, sizes = _pack([g[k][None] for k in REPLICATED])
    small_grads, after = _exchange_start([packed[0]], False, grad_x, "gather_small_grads_start")

    me1 = me.astype(jnp.int32).reshape(1)
    res = {}
    for grp in ("c", "b", "a"):
        slots, own = {}, {}
        for l in reversed(range(L)):
            srcs, lands = _exchange_wait(scatters[l, grp], after, f"scatter_wait_l{l}{grp}")
            for i, k in enumerate(GROUPS[grp]):
                slots[k, l], own[k, l] = lands[i], srcs[i]
        for k in GROUPS[grp]:
            res[k] = _adamw([slots[k, l] for l in range(L)], [own[k, l] for l in range(L)], me1, w[k], mom[k], var[k],
                            "adamw_" + k, dep=after)
            after = res[k][1]

    srcs, lands = _exchange_wait(small_grads, after, "gather_small_grads_wait")
    parts = jnp.where(mine, srcs[0][None], lands[0])
    shapes = [w[k].shape for k in REPLICATED]
    pw, _ = _pack([w[k][None] for k in REPLICATED])
    pm, _ = _pack([mom[k][None] for k in REPLICATED])
    pv, _ = _pack([var[k][None] for k in REPLICATED])
    outs = _adamw([parts], None, me1, pw, pm, pv, "adamw_small")
    unpacked = [_unpack(o, sizes, shapes) for o in outs]
    for i, k in enumerate(REPLICATED):
        res[k] = [unpacked[j][i] for j in range(4)]

    return (loss, grad_x[None], *[res[k][0] for k in WEIGHTS], *[res[k][1] for k in WEIGHTS],
            *[res[k][2] for k in WEIGHTS], *[res[k][3] for k in WEIGHTS])
```

```python
import functools
import math

import jax
import jax.numpy as jnp
from jax import lax
from jax.experimental import pallas as pl
from jax.experimental.pallas import tpu as pltpu

F32 = jnp.float32
BF16 = jnp.bfloat16

SWA_Q_HEADS = 16
SWA_KV_HEADS = 2
SWA_HEAD_DIM = 64
SWA_BLOCK = 128
MLA_HEADS = 16
MLA_NOPE = 128
MLA_ROPE = 64
MLA_V = 128
SGU_GROUPS = 8
SGU_DIM = 128
SGU_CHUNK = 128
ROPE_THETA = 10000.0
EPS = 1e-5
MASK = -1e30
ADAM_LR = 0.001
ADAM_B1 = 0.9
ADAM_B2 = 0.999
ADAM_EPS = 1e-08
ADAM_WD = 0.01
ADAM_STEP = 10

N_DEV = 8
AXES = ("x", "y", "c")
VMEM_LIMIT = 56 * 1024 * 1024
MLA_TILE = 512
MLA_FWD_TILE = 1024
ROW_TILE = 512
MAX_TK = 2816
SUBLANES = 8


def _tile(n, prefs):
    for p in prefs:
        if n % p == 0:
            return p
    return n


def _params(sem):
    return pltpu.CompilerParams(dimension_semantics=sem, vmem_limit_bytes=VMEM_LIMIT)


def _cols(tm, width, off):
    assert off % width == 0, (off, width)
    blk = off // width
    return pl.BlockSpec((tm, width), lambda i, *_: (i, blk))


def _full(shape):
    nd = len(shape)
    return pl.BlockSpec(shape, lambda *_: (0,) * nd)


def _sigmoid(v):
    return 1.0 / (1.0 + jnp.exp(-v))


def _gelu(v):
    return 0.5 * v * (1.0 + lax.erf(v * (2.0 ** -0.5)))


def _gelu_grad(v):
    return 0.5 * (1.0 + lax.erf(v * (2.0 ** -0.5))) + v * jnp.exp(-0.5 * v * v) * (1.0 / math.sqrt(2.0 * math.pi))


_DIMS = {"nn": (((1,), (0,)), ((), ())), "nt": (((1,), (1,)), ((), ())), "tn": (((0,), (0,)), ((), ()))}


def _mm(a, b, mode, outs, name, *, extras=(), epilogue=None, full_n=False, dep=None, chunk=None):
    if mode == "nn":
        (M, K), (K2, N) = a.shape, b.shape
    elif mode == "nt":
        (M, K), (N, K2) = a.shape, b.shape
    else:
        (K, M), (K2, N) = a.shape, b.shape
    assert K == K2, (a.shape, b.shape, mode)
    tm = _tile(M, (1024, 512, 256, 128))
    tn = N if full_n else _tile(N, (1024, 768, 512, 384, 256, 128))
    if full_n:
        tm = _tile(M, (512, 256, 128))
    if chunk is not None:
        tn = chunk if chunk <= 1536 else _tile(chunk, (1024, 768, 512, 384, 256, 128))
        assert N % chunk == 0 and chunk % tn == 0 and tn % 128 == 0, (N, chunk, tn)
    max_tk = MAX_TK // 2 if full_n else MAX_TK
    tk = max(d for d in range(128, min(K, max_tk) + 1, 128) if K % d == 0) if K % 128 == 0 else K
    nk = K // tk
    if mode == "nn":
        a_spec = pl.BlockSpec((tm, tk), lambda i, j, k: (i, k))
        b_spec = pl.BlockSpec((tk, tn), lambda i, j, k: (k, j))
    elif mode == "nt":
        a_spec = pl.BlockSpec((tm, tk), lambda i, j, k: (i, k))
        b_spec = pl.BlockSpec((tn, tk), lambda i, j, k: (j, k))
    else:
        a_spec = pl.BlockSpec((tk, tm), lambda i, j, k: (k, i))
        b_spec = pl.BlockSpec((tk, tn), lambda i, j, k: (k, j))
    in_specs = [a_spec, b_spec]
    for arr, kind in extras:
        if kind == "tile":
            in_specs.append(pl.BlockSpec((tm, tn), lambda i, j, k: (i, j)))
        else:
            in_specs.append(pl.BlockSpec((1, tn), lambda i, j, k: (0, j)))
    out_specs, out_shape = [], []
    for dt, kind in outs:
        if kind == "n" and chunk is not None:
            per = chunk // tn
            out_specs.append(pl.BlockSpec((None, tm, tn), lambda i, j, k: (lax.div(j, per), i, lax.rem(j, per))))
            out_shape.append(jax.ShapeDtypeStruct((N // chunk, M, chunk), dt))
        elif kind == "n":
            out_specs.append(pl.BlockSpec((tm, tn), lambda i, j, k: (i, j)))
            out_shape.append(jax.ShapeDtypeStruct((M, N), dt))
        else:
            assert tn == N
            out_specs.append(pl.BlockSpec((tm, 1), lambda i, j, k: (i, 0)))
            out_shape.append(jax.ShapeDtypeStruct((M, 1), dt))
    ne, no = len(extras), len(outs)
    deps = []
    if dep is not None:
        in_specs.append(_full(dep.shape))
        deps = [dep]
    dims = _DIMS[mode]
    if epilogue is None:
        epilogue = lambda acc: (acc,) * no

    def body(*refs):
        a_ref, b_ref = refs[0], refs[1]
        ex = refs[2:2 + ne]
        out = refs[len(refs) - 1 - no:len(refs) - 1]
        acc = refs[-1]
        k = pl.program_id(2)
        part = lax.dot_general(a_ref[...].astype(BF16), b_ref[...].astype(BF16), dims, preferred_element_type=F32)

        def finish(total):
            res = epilogue(total, *[e[...] for e in ex])
            for o, r in zip(out, res):
                o[...] = r.astype(o.dtype)

        if nk == 1:
            finish(part)
            return

        @pl.when(k == 0)
        def _():
            acc[...] = part

        @pl.when((k > 0) & (k < nk - 1))
        def _():
            acc[...] += part

        @pl.when(k == nk - 1)
        def _():
            finish(acc[...] + part)

    res = pl.pallas_call(
        body, name=name, grid=(M // tm, N // tn, nk), in_specs=in_specs, out_specs=out_specs, out_shape=out_shape,
        scratch_shapes=[pltpu.VMEM((tm, tn), F32)],
        compiler_params=_params(("parallel", "parallel", "arbitrary")),
    )(a, b, *[e[0] for e in extras], *deps)
    return res


def _ln_epilogue(alpha):
    def epi(acc, x, g, b):
        r = alpha * x + acc
        mu = jnp.mean(r, axis=-1, keepdims=True)
        d = r - mu
        var = jnp.mean(d * d, axis=-1, keepdims=True)
        rstd = lax.rsqrt(var + EPS)
        xhat = d * rstd
        y = xhat * g + b
        return y, y, xhat, rstd
    return epi


def _mm_ln(a, w, x, g, b, alpha, name):
    return _mm(a, w, "nn", [(F32, "n"), (BF16, "n"), (F32, "n"), (F32, "1")], name,
               extras=[(x, "tile"), (g, "row"), (b, "row")], epilogue=_ln_epilogue(alpha), full_n=True)


def _mm_axpy(a, w, mode, r, alpha, name, dep=None):
    return _mm(a, w, mode, [(F32, "n")], name, extras=[(r, "tile")],
               epilogue=lambda acc, rv: (acc + alpha * rv,), dep=dep)[0]


def _ln_bwd(dy, xhat, rstd, g, name):
    S, D = dy.shape
    tm = _tile(S, (256, 128))

    def body(dy_ref, xh_ref, rs_ref, g_ref, dr_ref, dr16_ref, dg_ref, db_ref):
        @pl.when(pl.program_id(0) == 0)
        def _():
            dg_ref[...] = jnp.zeros_like(dg_ref)
            db_ref[...] = jnp.zeros_like(db_ref)

        dyv, xh = dy_ref[...], xh_ref[...]
        dxh = dyv * g_ref[...]
        m1 = jnp.mean(dxh, axis=-1, keepdims=True)
        m2 = jnp.mean(dxh * xh, axis=-1, keepdims=True)
        dr = rs_ref[...] * (dxh - m1 - xh * m2)
        dr_ref[...] = dr
        dr16_ref[...] = dr.astype(BF16)
        dg_ref[...] += jnp.sum(dyv * xh, axis=0, keepdims=True)
        db_ref[...] += jnp.sum(dyv, axis=0, keepdims=True)

    row = pl.BlockSpec((tm, D), lambda i: (i, 0))
    return pl.pallas_call(
        body, name=name, grid=(S // tm,),
        in_specs=[row, row, pl.BlockSpec((tm, 1), lambda i: (i, 0)), _full((1, D))],
        out_specs=[row, row, _full((1, D)), _full((1, D))],
        out_shape=[jax.ShapeDtypeStruct((S, D), F32), jax.ShapeDtypeStruct((S, D), BF16),
                   jax.ShapeDtypeStruct((1, D), F32), jax.ShapeDtypeStruct((1, D), F32)],
        compiler_params=_params(("arbitrary",)),
    )(dy, xhat, rstd, g)


def _rms_fwd(h, off, width, g, name):
    S = h.shape[0]
    tm = _tile(S, (ROW_TILE, 256, 128))

    def body(c_ref, g_ref, y_ref, r_ref):
        c = c_ref[...]
        r = lax.rsqrt(jnp.mean(c * c, axis=-1, keepdims=True) + EPS)
        y_ref[...] = (c * r * g_ref[...]).astype(BF16)
        r_ref[...] = r

    return pl.pallas_call(
        body, name=name, grid=(S // tm,),
        in_specs=[_cols(tm, width, off), _full((1, width))],
        out_specs=[pl.BlockSpec((tm, width), lambda i: (i, 0)), pl.BlockSpec((tm, 1), lambda i: (i, 0))],
        out_shape=[jax.ShapeDtypeStruct((S, width), BF16), jax.ShapeDtypeStruct((S, 1), F32)],
        compiler_params=_params(("parallel",)),
    )(h, g)


def _rms_bwd(dy, h, off, width, rstd, g, name):
    S = h.shape[0]
    tm = _tile(S, (ROW_TILE, 256, 128))

    def body(dy_ref, c_ref, r_ref, g_ref, dc_ref, dg_ref):
        @pl.when(pl.program_id(0) == 0)
        def _():
            dg_ref[...] = jnp.zeros_like(dg_ref)

        dyv, c, r = dy_ref[...], c_ref[...], r_ref[...]
        dyg = dyv * g_ref[...]
        m = jnp.mean(dyg * c, axis=-1, keepdims=True)
        dc_ref[...] = (r * dyg - c * (r * r * r) * m).astype(BF16)
        dg_ref[...] += jnp.sum(dyv * c * r, axis=0, keepdims=True)

    return pl.pallas_call(
        body, name=name, grid=(S // tm,),
        in_specs=[pl.BlockSpec((tm, width), lambda i: (i, 0)), _cols(tm, width, off),
                  pl.BlockSpec((tm, 1), lambda i: (i, 0)), _full((1, width))],
        out_specs=[pl.BlockSpec((tm, width), lambda i: (i, 0)), _full((1, width))],
        out_shape=[jax.ShapeDtypeStruct((S, width), BF16), jax.ShapeDtypeStruct((1, width), F32)],
        compiler_params=_params(("arbitrary",)),
    )(dy, h, rstd, g)


def _loss(y, target, name):
    S, D = y.shape
    tm = _tile(S, (256, 128))

    def body(y_ref, t_ref, l_ref, dy_ref):
        @pl.when(pl.program_id(0) == 0)
        def _():
            l_ref[...] = jnp.zeros_like(l_ref)

        err = y_ref[...] - t_ref[...]
        dy_ref[...] = err * (1.0 / D)
        per_tok = jnp.mean(err * err, axis=-1, keepdims=True)
        l_ref[...] += 0.5 * jnp.sum(per_tok, axis=0, keepdims=True)

    row = pl.BlockSpec((tm, D), lambda i: (i, 0))
    return pl.pallas_call(
        body, name=name, grid=(S // tm,), in_specs=[row, row], out_specs=[_full((1, 1)), row],
        out_shape=[jax.ShapeDtypeStruct((1, 1), F32), jax.ShapeDtypeStruct((S, D), F32)],
        compiler_params=_params(("arbitrary",)),
    )(y, target)


LANES = 128


def _rope(x, off, width, cs2, sn2, bwd, name):
    S = cs2.shape[0]
    tm = _tile(S, (ROW_TILE, 256, 128))
    stacked = x.ndim == 3
    half = MLA_ROPE // 2
    assert width % LANES == 0 and MLA_ROPE * 2 == LANES

    def rot(v):
        lane = lax.broadcasted_iota(jnp.int32, v.shape, 1)
        return jnp.where((lane & (MLA_ROPE - 1)) < half, -pltpu.roll(v, LANES - half, 1), pltpu.roll(v, half, 1))

    def body(x_ref, c_ref, s_ref, y_ref):
        c, s = c_ref[...], s_ref[...]
        for g in range(width // LANES):
            cols = slice(g * LANES, (g + 1) * LANES)
            v = jnp.sum(x_ref[...], axis=0) if stacked else x_ref[:, cols].astype(F32)
            y = v * c - rot(v * s) if bwd else v * c + rot(v) * s
            y_ref[:, cols] = y.astype(BF16)

    row = pl.BlockSpec((tm, LANES), lambda i: (i, 0))
    x_spec = pl.BlockSpec((x.shape[0], tm, LANES), lambda i: (0, i, 0)) if stacked else _cols(tm, width, off)
    return pl.pallas_call(
        body, name=name, grid=(S // tm,), in_specs=[x_spec, row, row],
        out_specs=pl.BlockSpec((tm, width), lambda i: (i, 0)), out_shape=jax.ShapeDtypeStruct((S, width), BF16),
        compiler_params=_params(("parallel",)),
    )(x, cs2, sn2)


def _swa_mask(n, rows):
    blk = SWA_BLOCK
    row = lax.broadcasted_iota(jnp.int32, (rows, 2 * blk), 0) & (blk - 1)
    col = lax.broadcasted_iota(jnp.int32, (rows, 2 * blk), 1)
    rel = row + blk - col
    return (rel >= 0) & (rel < blk) & ((n > 0) | (col >= blk))


def _swa_specs(off_q, off_k, off_v, stacked):
    blk, aq, akv = SWA_BLOCK, SWA_Q_HEADS * SWA_HEAD_DIM, SWA_KV_HEADS * SWA_HEAD_DIM
    grp = SWA_Q_HEADS // SWA_KV_HEADS
    assert off_q % aq == 0 and off_k % akv == 0 and off_v % akv == 0 and blk & (blk - 1) == 0
    prev = lambda off: pl.BlockSpec((blk, akv), lambda n: (jnp.maximum(n - 1, 0), off // akv))
    cur = lambda off: pl.BlockSpec((blk, akv), lambda n: (n, off // akv))
    sink = _full((SWA_KV_HEADS, grp * blk, 1)) if stacked else pl.BlockSpec(memory_space=pltpu.SMEM)
    return [sink, _cols(blk, aq, off_q), prev(off_k), cur(off_k), prev(off_v), cur(off_v)]


def _swa_sinks(sinks):
    grp = SWA_Q_HEADS // SWA_KV_HEADS
    return jnp.repeat(sinks.reshape(SWA_KV_HEADS, grp), SWA_BLOCK, axis=1)[:, :, None]


def _swa_stack(x, kv):
    hd, grp = SWA_HEAD_DIM, SWA_Q_HEADS // SWA_KV_HEADS
    return jnp.concatenate([x[:, (kv * grp + g) * hd:(kv * grp + g + 1) * hd] for g in range(grp)], axis=0)


def _swa_fwd(h, off_q, off_k, off_v, sinks, name):
    S = h.shape[0]
    blk, hd, nh, nkv = SWA_BLOCK, SWA_HEAD_DIM, SWA_Q_HEADS, SWA_KV_HEADS
    grp = nh // nkv
    aq = nh * hd
    scale = hd ** -0.5

    def body(sink_ref, q_ref, kp_ref, kc_ref, vp_ref, vc_ref, o_ref, lse_ref):
        valid = _swa_mask(pl.program_id(0), blk)
        q = q_ref[...].astype(BF16)
        k2 = jnp.concatenate([kp_ref[...], kc_ref[...]], axis=0).astype(BF16)
        v2 = jnp.concatenate([vp_ref[...], vc_ref[...]], axis=0).astype(BF16)
        for hh in range(nh):
            kv = hh // grp
            qh = q[:, hh * hd:(hh + 1) * hd]
            kh = k2[:, kv * hd:(kv + 1) * hd]
            vh = v2[:, kv * hd:(kv + 1) * hd]
            s = lax.dot_general(qh, kh, _DIMS["nt"], preferred_element_type=F32) * scale
            s = jnp.where(valid, s, MASK)
            sk = sink_ref[hh]
            m = jnp.maximum(jnp.max(s, axis=1, keepdims=True), sk)
            p = jnp.exp(s - m)
            l = jnp.sum(p, axis=1, keepdims=True) + jnp.exp(sk - m)
            o_ref[:, hh * hd:(hh + 1) * hd] = jnp.dot((p / l).astype(BF16), vh, preferred_element_type=F32)
            lse_ref[:, hh:hh + 1] = m + jnp.log(l)

    return pl.pallas_call(
        body, name=name, grid=(S // blk,), in_specs=_swa_specs(off_q, off_k, off_v, False),
        out_specs=[pl.BlockSpec((blk, aq), lambda n: (n, 0)), pl.BlockSpec((blk, nh), lambda n: (n, 0))],
        out_shape=[jax.ShapeDtypeStruct((S, aq), F32), jax.ShapeDtypeStruct((S, nh), F32)],
        compiler_params=_params(("parallel",)),
    )(sinks, h, h, h, h, h)


def _swa_bwd(h, off_q, off_k, off_v, sinks, dout, lse, name):
    S = h.shape[0]
    blk, hd, nh, nkv = SWA_BLOCK, SWA_HEAD_DIM, SWA_Q_HEADS, SWA_KV_HEADS
    grp = nh // nkv
    aq, akv = nh * hd, nkv * hd
    scale = hd ** -0.5

    def body(sink_ref, q_ref, kp_ref, kc_ref, vp_ref, vc_ref, do_ref, lse_ref, dq_ref, dk_ref, dv_ref, ds_ref):
        n = pl.program_id(0)

        @pl.when(n == 0)
        def _():
            dk_ref[...] = jnp.zeros_like(dk_ref)
            dv_ref[...] = jnp.zeros_like(dv_ref)
            ds_ref[...] = jnp.zeros_like(ds_ref)

        valid = _swa_mask(n, grp * blk)
        q = q_ref[...].astype(BF16)
        k2 = jnp.concatenate([kp_ref[...], kc_ref[...]], axis=0).astype(BF16)
        v2 = jnp.concatenate([vp_ref[...], vc_ref[...]], axis=0).astype(BF16)
        do = do_ref[...]
        lane = lax.broadcasted_iota(jnp.int32, (1, 128), 1)
        dsink = jnp.zeros((1, 128), F32)
        cur = pl.ds(pl.multiple_of(n * blk, blk), blk)
        prev = pl.ds(pl.multiple_of(jnp.maximum(n - 1, 0) * blk, blk), blk)
        for kv in range(nkv):
            kh = k2[:, kv * hd:(kv + 1) * hd]
            vh = v2[:, kv * hd:(kv + 1) * hd]
            qs = _swa_stack(q, kv)
            dos = _swa_stack(do, kv)
            dos16 = dos.astype(BF16)
            lse = jnp.concatenate([lse_ref[:, kv * grp + g:kv * grp + g + 1] for g in range(grp)], axis=0)
            s = lax.dot_general(qs, kh, _DIMS["nt"], preferred_element_type=F32) * scale
            s = jnp.where(valid, s, MASK)
            p = jnp.exp(s - lse)
            p16 = p.astype(BF16)
            o = jnp.dot(p16, vh, preferred_element_type=F32)
            delta = jnp.sum(dos * o, axis=1, keepdims=True)
            dp = lax.dot_general(dos16, vh, _DIMS["nt"], preferred_element_type=F32)
            ds16 = (p * (dp - delta) * scale).astype(BF16)
            dqs = jnp.dot(ds16, kh, preferred_element_type=F32).astype(BF16)
            dk_acc = lax.dot_general(ds16, qs, _DIMS["tn"], preferred_element_type=F32)
            dv_acc = lax.dot_general(p16, dos16, _DIMS["tn"], preferred_element_type=F32)
            dsk = jnp.exp(sink_ref[kv] - lse) * delta
            for g in range(grp):
                hh = kv * grp + g
                dq_ref[:, hh * hd:(hh + 1) * hd] = dqs[g * blk:(g + 1) * blk]
                dsink += jnp.where(lane == hh, -jnp.sum(dsk[g * blk:(g + 1) * blk], axis=0, keepdims=True), 0.0)
            cols = slice(kv * hd, (kv + 1) * hd)
            dk_ref[cur, cols] += dk_acc[blk:]
            dv_ref[cur, cols] += dv_acc[blk:]

            @pl.when(n > 0)
            def _():
                dk_ref[prev, cols] += dk_acc[:blk]
                dv_ref[prev, cols] += dv_acc[:blk]

        ds_ref[...] += dsink

    return pl.pallas_call(
        body, name=name, grid=(S // blk,),
        in_specs=_swa_specs(off_q, off_k, off_v, True) + [pl.BlockSpec((blk, aq), lambda n: (n, 0)),
                                                    pl.BlockSpec((blk, nh), lambda n: (n, 0))],
        out_specs=[pl.BlockSpec((blk, aq), lambda n: (n, 0)), _full((S, akv)), _full((S, akv)), _full((1, 128))],
        out_shape=[jax.ShapeDtypeStruct((S, aq), BF16), jax.ShapeDtypeStruct((S, akv), F32),
                   jax.ShapeDtypeStruct((S, akv), F32), jax.ShapeDtypeStruct((1, 128), F32)],
        compiler_params=_params(("arbitrary",)),
    )(_swa_sinks(sinks), h, h, h, h, h, dout, lse)


def _causal(i, j, t):
    row = i * t + lax.broadcasted_iota(jnp.int32, (t, t), 0)
    col = j * t + lax.broadcasted_iota(jnp.int32, (t, t), 1)
    return col <= row


def _pair_rope(qr, hh):
    lane = lax.broadcasted_iota(jnp.int32, qr.shape, 1)
    return jnp.where(lane < MLA_ROPE, qr if hh == 0 else pltpu.roll(qr, MLA_ROPE, 1), jnp.zeros_like(qr))


def _mla_fwd(qf, qr, kvf, kr, name):
    S = qr.shape[0]
    H, dn, dv = MLA_HEADS, MLA_NOPE, MLA_V
    assert H % 2 == 0 and dn == LANES and dv == LANES and 2 * MLA_ROPE == LANES
    t = _tile(S, (MLA_FWD_TILE, 512, 256, 128))
    nq = S // t
    scale = (MLA_NOPE + MLA_ROPE) ** -0.5

    def body(qn_ref, qr_ref, kn_ref, kr_ref, v_ref, o_ref, lse_ref, *state):
        i = pl.program_id(1)
        for hh in range(2):
            m_s, l_s, acc_s = state[3 * hh:3 * hh + 3]
            m_s[...] = jnp.full_like(m_s, -jnp.inf)
            l_s[...] = jnp.zeros_like(l_s)
            acc_s[...] = jnp.zeros_like(acc_s)

        def block(j, masked):
            rows = pl.ds(pl.multiple_of(j * t, t), t)
            for hh in range(2):
                m_s, l_s, acc_s = state[3 * hh:3 * hh + 3]
                cols = slice(hh * LANES, (hh + 1) * LANES)
                q = jnp.concatenate([qn_ref[:, cols].astype(BF16), _pair_rope(qr_ref[...], hh)], axis=1)
                k = jnp.concatenate([kn_ref[rows, cols], kr_ref[rows, :]], axis=1)
                s = lax.dot_general(q, k, _DIMS["nt"], preferred_element_type=F32) * scale
                if masked:
                    s = jnp.where(_causal(0, 0, t), s, MASK)
                m_old = m_s[...]
                m_new = jnp.maximum(m_old, jnp.max(s, axis=1, keepdims=True))
                alpha = jnp.exp(m_old - m_new)
                p = jnp.exp(s - m_new)
                l_s[...] = alpha * l_s[...] + jnp.sum(p, axis=1, keepdims=True)
                acc_s[...] = alpha * acc_s[...] + jnp.dot(p.astype(BF16), v_ref[rows, cols], preferred_element_type=F32)
                m_s[...] = m_new

        def full_block(j, carry):
            block(j, False)
            return carry

        lax.fori_loop(0, i, full_block, 0)
        block(i, True)
        for hh in range(2):
            m_s, l_s, acc_s = state[3 * hh:3 * hh + 3]
            o_ref[:, hh * LANES:(hh + 1) * LANES] = acc_s[...] / l_s[...]
            lse_ref[hh] = m_s[...] + jnp.log(l_s[...])

    P = H // 2
    return pl.pallas_call(
        body, name=name, grid=(P, nq),
        in_specs=[pl.BlockSpec((t, 2 * LANES), lambda p, i: (i, p)), pl.BlockSpec((t, LANES), lambda p, i: (i, p)),
                  pl.BlockSpec((S, 2 * LANES), lambda p, i: (0, p)), pl.BlockSpec((S, LANES), lambda p, i: (0, 0)),
                  pl.BlockSpec((S, 2 * LANES), lambda p, i: (0, P + p))],
        out_specs=[pl.BlockSpec((t, 2 * LANES), lambda p, i: (i, p)), pl.BlockSpec((2, t, 1), lambda p, i: (p, i, 0))],
        out_shape=[jax.ShapeDtypeStruct((S, H * dv), F32), jax.ShapeDtypeStruct((H, S, 1), F32)],
        scratch_shapes=[pltpu.VMEM((t, 1), F32), pltpu.VMEM((t, 1), F32), pltpu.VMEM((t, dv), F32)] * 2,
        compiler_params=_params(("parallel", "arbitrary")),
    )(qf, qr, kvf, kr, kvf)


def _rowdot(a, b, name):
    S = a.shape[0]
    H = a.shape[1] // LANES
    t = _tile(S, (ROW_TILE, 256, 128))

    def body(a_ref, b_ref, o_ref):
        o_ref[0] = jnp.sum(a_ref[...] * b_ref[...], axis=-1, keepdims=True)

    spec = pl.BlockSpec((t, LANES), lambda h, i: (i, h))
    return pl.pallas_call(
        body, name=name, grid=(H, S // t), in_specs=[spec, spec],
        out_specs=pl.BlockSpec((1, t, 1), lambda h, i: (h, i, 0)),
        out_shape=jax.ShapeDtypeStruct((H, S, 1), F32),
        compiler_params=_params(("parallel", "parallel")),
    )(a, b)


def _mla_bwd(qf, qr, kvf, kr, do, lse, delta, name):
    S = qr.shape[0]
    H, dv = MLA_HEADS, MLA_V
    P = H // 2
    t = _tile(S, (MLA_TILE, 256, 128))
    nq = S // t
    scale = (MLA_NOPE + MLA_ROPE) ** -0.5

    def body(qn_ref, qr_ref, kn_ref, kr_ref, v_ref, do_ref, lse_ref, dl_ref, dqn_ref, dqr_ref, dkn_ref, dv_ref, dkr_ref,
             dkn_s, dv_s, dkr_s):
        j, i = pl.program_id(1), pl.program_id(2)

        @pl.when((j == 0) & (i == 0))
        def _():
            dqn_ref[...] = jnp.zeros_like(dqn_ref)
            dqr_ref[...] = jnp.zeros_like(dqr_ref)

        @pl.when(i == j)
        def _():
            dkn_s[...] = jnp.zeros_like(dkn_s)
            dv_s[...] = jnp.zeros_like(dv_s)
            dkr_s[...] = jnp.zeros_like(dkr_s)

        def block(masked):
            rows = pl.ds(pl.multiple_of(i * t, t), t)
            krv = kr_ref[...]
            for hh in range(2):
                cols = slice(hh * LANES, (hh + 1) * LANES)
                q = jnp.concatenate([qn_ref[:, cols].astype(BF16), _pair_rope(qr_ref[...], hh)], axis=1)
                k = jnp.concatenate([kn_ref[:, cols], krv], axis=1)
                vv, dov = v_ref[:, cols], do_ref[:, cols].astype(BF16)
                s = lax.dot_general(q, k, _DIMS["nt"], preferred_element_type=F32) * scale
                if masked:
                    s = jnp.where(_causal(0, 0, t), s, MASK)
                p = jnp.exp(s - lse_ref[hh])
                p16 = p.astype(BF16)
                dp = lax.dot_general(dov, vv, _DIMS["nt"], preferred_element_type=F32)
                ds16 = (p * (dp - dl_ref[hh]) * scale).astype(BF16)
                dv_s[:, cols] += lax.dot_general(p16, dov, _DIMS["tn"], preferred_element_type=F32)
                dk = lax.dot_general(ds16, q, _DIMS["tn"], preferred_element_type=F32)
                dkn_s[:, cols] += dk[:, :LANES]
                dkr_s[...] += dk[:, LANES:]
                dq = jnp.dot(ds16, k, preferred_element_type=F32)
                dqn_ref[rows, cols] += dq[:, :LANES]
                dqr = dq[:, LANES:]
                dqr_ref[rows, :] += dqr if hh == 0 else pltpu.roll(dqr, MLA_ROPE, 1)

        @pl.when(i == j)
        def _():
            block(True)

        @pl.when(i > j)
        def _():
            block(False)

        @pl.when(i == nq - 1)
        def _():
            dkn_ref[...] = dkn_s[...]
            dv_ref[...] = dv_s[...]
            dkr_ref[...] = dkr_s[...]

    qi = lambda i, j: jnp.maximum(i, j)
    return pl.pallas_call(
        body, name=name, grid=(P, nq, nq),
        in_specs=[pl.BlockSpec((t, 2 * LANES), lambda p, j, i: (qi(i, j), p)), pl.BlockSpec((t, LANES), lambda p, j, i: (qi(i, j), p)),
                  pl.BlockSpec((t, 2 * LANES), lambda p, j, i: (j, p)), pl.BlockSpec((t, LANES), lambda p, j, i: (j, 0)),
                  pl.BlockSpec((t, 2 * LANES), lambda p, j, i: (j, P + p)),
                  pl.BlockSpec((t, 2 * LANES), lambda p, j, i: (qi(i, j), p)),
                  pl.BlockSpec((2, t, 1), lambda p, j, i: (p, qi(i, j), 0)), pl.BlockSpec((2, t, 1), lambda p, j, i: (p, qi(i, j), 0))],
        out_specs=[pl.BlockSpec((S, 2 * LANES), lambda p, j, i: (0, p)), pl.BlockSpec((S, LANES), lambda p, j, i: (0, p)),
                   pl.BlockSpec((t, 2 * LANES), lambda p, j, i: (j, p)), pl.BlockSpec((t, 2 * LANES), lambda p, j, i: (j, p)),
                   pl.BlockSpec((None, t, LANES), lambda p, j, i: (p, j, 0))],
        out_shape=[jax.ShapeDtypeStruct((S, H * MLA_NOPE), F32), jax.ShapeDtypeStruct((S, H * MLA_ROPE), F32),
                   jax.ShapeDtypeStruct((S, H * MLA_NOPE), F32), jax.ShapeDtypeStruct((S, H * dv), F32),
                   jax.ShapeDtypeStruct((P, S, LANES), F32)],
        scratch_shapes=[pltpu.VMEM((t, 2 * LANES), F32), pltpu.VMEM((t, 2 * LANES), F32), pltpu.VMEM((t, LANES), F32)],
        compiler_params=_params(("parallel", "arbitrary", "arbitrary")),
    )(qf, qr, kvf, kr, kvf, do, lse, delta)


def _sgu_norm(hv, lg, lb):
    vg = _gelu(hv)
    mu = jnp.mean(vg, axis=-1, keepdims=True)
    d = vg - mu
    rstd = lax.rsqrt(jnp.mean(d * d, axis=-1, keepdims=True) + EPS)
    xhat = d * rstd
    return xhat, rstd, xhat * lg + lb


def _sgu_fwd(h, off_u, off_v, lg, lb, w16, bt, name):
    S = h.shape[0]
    T, G, C = SGU_CHUNK, SGU_GROUPS, SGU_DIM
    W = G * C

    def body(hu_ref, hv_ref, lg_ref, lb_ref, w_ref, bt_ref, y_ref):
        u = _gelu(hu_ref[...])
        _, _, vn = _sgu_norm(hv_ref[...], lg_ref[...], lb_ref[...])
        vn16 = vn.astype(BF16)
        for g in range(G):
            cols = slice(g * C, (g + 1) * C)
            mixed = jnp.dot(w_ref[g], vn16[:, cols], preferred_element_type=F32) + bt_ref[:, g:g + 1]
            y_ref[:, cols] = (u[:, cols] * mixed).astype(BF16)

    return pl.pallas_call(
        body, name=name, grid=(S // T,),
        in_specs=[_cols(T, W, off_u), _cols(T, W, off_v), _full((1, W)), _full((1, W)), _full((G, T, T)), _full((T, G))],
        out_specs=pl.BlockSpec((T, W), lambda n: (n, 0)),
        out_shape=jax.ShapeDtypeStruct((S, W), BF16),
        compiler_params=_params(("parallel",)),
    )(h, h, lg, lb, w16, bt)


def _sgu_bwd(h, off_u, off_v, lg, lb, w16, bt, dy, name):
    S = h.shape[0]
    T, G, C = SGU_CHUNK, SGU_GROUPS, SGU_DIM
    W = G * C
    nc = S // T

    def body(hu_ref, hv_ref, lg_ref, lb_ref, w_ref, bt_ref, dy_ref, dhu_ref, dhv_ref, dw_ref, db_ref, dlg_ref, dlb_ref,
             dmix_s, dvn_s):
        n = pl.program_id(0)

        @pl.when(n == 0)
        def _():
            dw_ref[...] = jnp.zeros_like(dw_ref)
            dlg_ref[...] = jnp.zeros_like(dlg_ref)
            dlb_ref[...] = jnp.zeros_like(dlb_ref)
            dmix_s[...] = jnp.zeros_like(dmix_s)

        hu, hv, lgv = hu_ref[...], hv_ref[...], lg_ref[...]
        u = _gelu(hu)
        xhat, rstd, vn = _sgu_norm(hv, lgv, lb_ref[...])
        vn16 = vn.astype(BF16)
        dyv = dy_ref[...]
        dmixed = dyv * u
        dmix_s[...] += dmixed
        dmixed16 = dmixed.astype(BF16)
        for g in range(G):
            cols = slice(g * C, (g + 1) * C)
            mixed = jnp.dot(w_ref[g], vn16[:, cols], preferred_element_type=F32) + bt_ref[:, g:g + 1]
            dhu_ref[:, cols] = (dyv[:, cols] * mixed * _gelu_grad(hu[:, cols])).astype(BF16)
            dvn_s[:, cols] = lax.dot_general(w_ref[g], dmixed16[:, cols], _DIMS["tn"], preferred_element_type=F32)
            dw_ref[g] += lax.dot_general(dmixed16[:, cols], vn16[:, cols], _DIMS["nt"], preferred_element_type=F32)
        dvn = dvn_s[...]
        dlg_ref[...] += jnp.sum(dvn * xhat, axis=0, keepdims=True)
        dlb_ref[...] += jnp.sum(dvn, axis=0, keepdims=True)
        dxh = dvn * lgv
        m1 = jnp.mean(dxh, axis=-1, keepdims=True)
        m2 = jnp.mean(dxh * xhat, axis=-1, keepdims=True)
        dvg = rstd * (dxh - m1 - xhat * m2)
        dhv_ref[...] = (dvg * _gelu_grad(hv)).astype(BF16)

        @pl.when(n == nc - 1)
        def _():
            tril = lax.broadcasted_iota(jnp.int32, (T, T), 1) <= lax.broadcasted_iota(jnp.int32, (T, T), 0)
            lane = lax.broadcasted_iota(jnp.int32, (T, 128), 1)
            db = jnp.zeros((T, 128), F32)
            for g in range(G):
                dw_ref[g] = jnp.where(tril, dw_ref[g], 0.0)
                db += jnp.where(lane == g, jnp.sum(dmix_s[:, g * C:(g + 1) * C], axis=1, keepdims=True), 0.0)
            db_ref[...] = db

    row = pl.BlockSpec((T, W), lambda n: (n, 0))
    return pl.pallas_call(
        body, name=name, grid=(nc,),
        in_specs=[_cols(T, W, off_u), _cols(T, W, off_v), _full((1, W)), _full((1, W)), _full((G, T, T)), _full((T, G)), row],
        out_specs=[row, row, _full((G, T, T)), _full((T, 128)), _full((1, W)), _full((1, W))],
        out_shape=[jax.ShapeDtypeStruct((S, W), BF16), jax.ShapeDtypeStruct((S, W), BF16),
                   jax.ShapeDtypeStruct((G, T, T), F32), jax.ShapeDtypeStruct((T, 128), F32),
                   jax.ShapeDtypeStruct((1, W), F32), jax.ShapeDtypeStruct((1, W), F32)],
        scratch_shapes=[pltpu.VMEM((T, W), F32), pltpu.VMEM((T, W), F32)],
        compiler_params=_params(("arbitrary",)),
    )(h, h, lg, lb, w16, bt, dy)


def _merge_fwd(ys, ps, h, bg, name):
    S = h.shape[0]
    D = ps[0].shape[1]
    tm = _tile(S, (512, 256, 128))
    tn = _tile(D, (512, 256, 128))
    nb = len(ys)

    def body(*refs):
        y_refs, p_refs, l_refs = refs[:nb], refs[nb:2 * nb], refs[2 * nb:3 * nb]
        bg_ref, mg_ref, z_ref = refs[3 * nb:]
        acc = jnp.zeros((tm, tn), F32)
        for b in range(nb):
            z = jnp.dot(y_refs[b][...].astype(BF16), p_refs[b][...], preferred_element_type=F32)
            z_ref[b] = z
            acc += _sigmoid(l_refs[b][...] + bg_ref[b:b + 1, :]) * z
        mg_ref[...] = acc.astype(BF16)

    in_specs = [pl.BlockSpec((tm, y.shape[1]), lambda i, j: (i, 0)) for y in ys]
    in_specs += [pl.BlockSpec((p.shape[0], tn), lambda i, j: (0, j)) for p in ps]
    in_specs += [pl.BlockSpec((tm, tn), functools.partial(lambda i, j, b: (i, b * (D // tn) + j), b=b)) for b in range(nb)]
    in_specs += [pl.BlockSpec((nb, tn), lambda i, j: (0, j))]
    return pl.pallas_call(
        body, name=name, grid=(S // tm, D // tn), in_specs=in_specs,
        out_specs=[pl.BlockSpec((tm, tn), lambda i, j: (i, j)), pl.BlockSpec((nb, tm, tn), lambda i, j: (0, i, j))],
        out_shape=[jax.ShapeDtypeStruct((S, D), BF16), jax.ShapeDtypeStruct((nb, S, D), F32)],
        compiler_params=_params(("parallel", "parallel")),
    )(*ys, *ps, *([h] * nb), bg)


def _merge_bwd(dm, z, h, bg, name):
    nb, S, D = z.shape
    tm = _tile(S, (256, 128))
    tn = _tile(D, (512, 256, 128))

    def body(*refs):
        dm_ref, z_ref = refs[0], refs[1]
        l_refs = refs[2:2 + nb]
        bg_ref, dz_ref, dl_ref, dbg_ref = refs[2 + nb:]

        @pl.when(pl.program_id(1) == 0)
        def _():
            dbg_ref[...] = jnp.zeros_like(dbg_ref)

        dmv = dm_ref[...]
        rows = lax.broadcasted_iota(jnp.int32, (SUBLANES, tn), 0)
        dbg = jnp.zeros((SUBLANES, tn), F32)
        for b in range(nb):
            gt = _sigmoid(l_refs[b][...] + bg_ref[b:b + 1, :])
            dz_ref[b] = (dmv * gt).astype(BF16)
            dl = dmv * z_ref[b] * gt * (1.0 - gt)
            dl_ref[b] = dl.astype(BF16)
            dbg += jnp.where(rows == b, jnp.sum(dl, axis=0, keepdims=True), 0.0)
        dbg_ref[...] += dbg

    in_specs = [pl.BlockSpec((tm, tn), lambda j, i: (i, j)), pl.BlockSpec((nb, tm, tn), lambda j, i: (0, i, j))]
    in_specs += [pl.BlockSpec((tm, tn), functools.partial(lambda j, i, b: (i, b * (D // tn) + j), b=b)) for b in range(nb)]
    in_specs += [pl.BlockSpec((nb, tn), lambda j, i: (0, j))]
    blk3 = pl.BlockSpec((nb, tm, tn), lambda j, i: (0, i, j))
    return pl.pallas_call(
        body, name=name, grid=(D // tn, S // tm), in_specs=in_specs,
        out_specs=[blk3, blk3, pl.BlockSpec((SUBLANES, tn), lambda j, i: (0, j))],
        out_shape=[jax.ShapeDtypeStruct((nb, S, D), BF16), jax.ShapeDtypeStruct((nb, S, D), BF16),
                   jax.ShapeDtypeStruct((SUBLANES, D), F32)],
        compiler_params=_params(("parallel", "arbitrary")),
    )(dm, z, *([h] * nb), bg)


def _shift_down(x, halo, k):
    xr = pltpu.roll(x, k, 0)
    hr = pltpu.roll(halo, k, 0)
    rows = lax.broadcasted_iota(jnp.int32, halo.shape, 0)
    top = jnp.where(rows < k, hr, xr[:SUBLANES])
    return jnp.concatenate([top, xr[SUBLANES:]], axis=0)


def _shift_up(x, halo, k):
    tm = x.shape[0]
    xr = pltpu.roll(x, tm - k, 0)
    hr = pltpu.roll(halo, SUBLANES - k, 0)
    rows = lax.broadcasted_iota(jnp.int32, halo.shape, 0)
    bot = jnp.where(rows >= SUBLANES - k, hr, xr[tm - SUBLANES:])
    return jnp.concatenate([xr[:tm - SUBLANES], bot], axis=0)


def _conv_tiles(S, F):
    return _tile(S, (ROW_TILE, 256, 128)), _tile(F, (512, 256, 128))


def _conv_in_specs(tm, tn, F):
    r8 = tm // SUBLANES
    nf = F // tn
    specs = []
    for half in range(2):
        specs.append(pl.BlockSpec((tm, tn), functools.partial(lambda j, i, o: (i, o + j), o=half * nf)))
        specs.append(pl.BlockSpec((SUBLANES, tn), functools.partial(lambda j, i, o: (jnp.maximum(i * r8 - 1, 0), o + j), o=half * nf)))
    for half in range(2):
        specs.append(pl.BlockSpec((3, tn), functools.partial(lambda j, i, o: (0, o + j), o=half * nf)))
        specs.append(pl.BlockSpec((1, tn), functools.partial(lambda j, i, o: (0, o + j), o=half * nf)))
    return specs


def _conv_apply(x, halo, w, b, first):
    halo = jnp.where(first, 0.0, halo)
    x1 = _shift_down(x, halo, 1)
    x2 = _shift_down(x, halo, 2)
    return b + x2 * w[0:1, :] + x1 * w[1:2, :] + x * w[2:3, :], x1, x2


def _glu_fwd(up, cw, cb, name):
    S, F2 = up.shape
    F = F2 // 2
    tm, tn = _conv_tiles(S, F)

    def body(ug, hg, uv, hv, wg, bgr, wv, bvr, a_ref):
        first = pl.program_id(1) == 0
        cg, _, _ = _conv_apply(ug[...], hg[...], wg[...], bgr[...], first)
        cv, _, _ = _conv_apply(uv[...], hv[...], wv[...], bvr[...], first)
        a_ref[...] = (cg * _sigmoid(cg) * cv).astype(BF16)

    return pl.pallas_call(
        body, name=name, grid=(F // tn, S // tm), in_specs=_conv_in_specs(tm, tn, F),
        out_specs=pl.BlockSpec((tm, tn), lambda j, i: (i, j)),
        out_shape=jax.ShapeDtypeStruct((S, F), BF16),
        compiler_params=_params(("parallel", "parallel")),
    )(up, up, up, up, cw, cb, cw, cb)


def _glu_bwd(up, cw, cb, da, name):
    S, F2 = up.shape
    F = F2 // 2
    tm, tn = _conv_tiles(S, F)

    def body(ug, hg, uv, hv, wg, bgr, wv, bvr, da_ref, dg_ref, dv_ref, sg_ref, sv_ref):
        i = pl.program_id(1)

        @pl.when(i == 0)
        def _():
            sg_ref[...] = jnp.zeros_like(sg_ref)
            sv_ref[...] = jnp.zeros_like(sv_ref)

        first = i == 0
        xg, xv = ug[...], uv[...]
        cg, xg1, xg2 = _conv_apply(xg, hg[...], wg[...], bgr[...], first)
        cv, xv1, xv2 = _conv_apply(xv, hv[...], wv[...], bvr[...], first)
        dav = da_ref[...]
        sg = _sigmoid(cg)
        dcv = dav * cg * sg
        dcg = dav * cv * sg * (1.0 + cg * (1.0 - sg))
        dg_ref[...] = dcg
        dv_ref[...] = dcv
        rows = lax.broadcasted_iota(jnp.int32, (SUBLANES, tn), 0)

        def stats(dc, x, x1, x2):
            acc = jnp.zeros((SUBLANES, tn), F32)
            for r, val in enumerate((dc * x2, dc * x1, dc * x, dc)):
                acc += jnp.where(rows == r, jnp.sum(val, axis=0, keepdims=True), 0.0)
            return acc

        sg_ref[...] += stats(dcg, xg, xg1, xg2)
        sv_ref[...] += stats(dcv, xv, xv1, xv2)

    tile = pl.BlockSpec((tm, tn), lambda j, i: (i, j))
    stat = pl.BlockSpec((SUBLANES, tn), lambda j, i: (0, j))
    return pl.pallas_call(
        body, name=name, grid=(F // tn, S // tm), in_specs=_conv_in_specs(tm, tn, F) + [tile],
        out_specs=[tile, tile, stat, stat],
        out_shape=[jax.ShapeDtypeStruct((S, F), F32), jax.ShapeDtypeStruct((S, F), F32),
                   jax.ShapeDtypeStruct((SUBLANES, F), F32), jax.ShapeDtypeStruct((SUBLANES, F), F32)],
        compiler_params=_params(("parallel", "arbitrary")),
    )(up, up, up, up, cw, cb, cw, cb, da)


def _conv_bwd(dcg, dcv, w, name):
    S, F = dcg.shape
    tm, tn = _conv_tiles(S, F)
    r8 = tm // SUBLANES
    ni = S // tm
    nf = F // tn

    def body(g_ref, gh_ref, v_ref, vh_ref, w_ref, o_ref):
        gate = pl.program_id(0) == 0
        x = jnp.where(gate, g_ref[...], v_ref[...])
        halo = jnp.where(gate, gh_ref[...], vh_ref[...])
        halo = jnp.where(pl.program_id(2) == ni - 1, 0.0, halo)
        wv = w_ref[...]
        o_ref[...] = (x * wv[2:3, :] + _shift_up(x, halo, 1) * wv[1:2, :] + _shift_up(x, halo, 2) * wv[0:1, :]).astype(BF16)

    def tile(half):
        return pl.BlockSpec((tm, tn), lambda h, j, i: (jnp.where(h == half, i, 0), jnp.where(h == half, j, 0)))

    def below(half):
        return pl.BlockSpec((SUBLANES, tn), lambda h, j, i: (
            jnp.where(h == half, jnp.minimum((i + 1) * r8, S // SUBLANES - 1), 0), jnp.where(h == half, j, 0)))

    return pl.pallas_call(
        body, name=name, grid=(2, nf, ni),
        in_specs=[tile(0), below(0), tile(1), below(1), pl.BlockSpec((3, tn), lambda h, j, i: (0, h * nf + j))],
        out_specs=pl.BlockSpec((tm, tn), lambda h, j, i: (i, h * nf + j)),
        out_shape=jax.ShapeDtypeStruct((S, 2 * F), BF16),
        compiler_params=_params(("parallel", "parallel", "parallel")),
    )(dcg, dcg, dcv, dcv, w)


def _adamw(slot_list, own_list, me, w, m, v, name, dep=None):
    L = len(slot_list)
    P, K, C = slot_list[0].shape
    tr = _tile(K, (256, 128, 64, 32, 16))
    while tr * C * 4 > (1 << 20) and tr % 32 == 0:
        tr //= 2
    nb = K // tr
    has_own = own_list is not None

    def body(me_ref, *refs):
        s_refs = refs[:L]
        o_refs = refs[L:2 * L] if has_own else None
        w_ref, m_ref, v_ref = refs[L * (1 + has_own):L * (1 + has_own) + 3]
        g_ref, d_ref, nm_ref, nv_ref = refs[-4:]
        layer = pl.program_id(0)
        g = None
        for l in range(L):
            gl = None
            for p in range(P):
                term = s_refs[l][p].astype(F32)
                if has_own:
                    term = jnp.where(me_ref[0] == p, o_refs[l][0].astype(F32), term)
                gl = term if gl is None else gl + term
            g = gl if g is None else jnp.where(layer == l, gl, g)
        nm = ADAM_B1 * m_ref[...] + (1.0 - ADAM_B1) * g
        nv = ADAM_B2 * v_ref[...] + (1.0 - ADAM_B2) * (g * g)
        m_hat = nm / (1.0 - ADAM_B1 ** ADAM_STEP)
        v_hat = nv / (1.0 - ADAM_B2 ** ADAM_STEP)
        g_ref[...] = g
        d_ref[...] = -ADAM_LR * (m_hat / (jnp.sqrt(v_hat) + ADAM_EPS) + ADAM_WD * w_ref[...])
        nm_ref[...] = nm
        nv_ref[...] = nv

    blk = pl.BlockSpec((None, tr, C), lambda li, i, me_ref: (li, i, 0))
    specs = [pl.BlockSpec((P, tr, C), functools.partial(lambda li, i, me_ref, l: (0, jnp.where(li == l, i, 0), 0), l=l))
             for l in range(L)]
    if has_own:
        specs += [pl.BlockSpec((1, tr, C), functools.partial(lambda li, i, me_ref, l: (me_ref[0], jnp.where(li == l, i, 0), 0), l=l))
                  for l in range(L)]
    return pl.pallas_call(
        body, name=name,
        grid_spec=pltpu.PrefetchScalarGridSpec(
            num_scalar_prefetch=1, grid=(L, nb), in_specs=specs + [blk, blk, blk] + [_ANY] * (dep is not None),
            out_specs=[blk] * 4),
        out_shape=[jax.ShapeDtypeStruct((L, K, C), F32)] * 4,
        compiler_params=_params(("arbitrary", "arbitrary")),
    )(me, *slot_list, *(own_list if has_own else []), w, m, v, *([] if dep is None else [dep]))


_HBM = pl.BlockSpec(memory_space=pltpu.HBM)
_SEM = pl.BlockSpec(memory_space=pltpu.SEMAPHORE)
_ANY = pl.BlockSpec(memory_space=pl.ANY)


def _peers():
    x, y, c = lax.axis_index("x"), lax.axis_index("y"), lax.axis_index("c")

    def flip(v, bit):
        return 1 - v if bit else v

    def peer(k):
        return (flip(x, (k >> 2) & 1), flip(y, (k >> 1) & 1), flip(c, k & 1))

    def peer_index(k):
        px, py, pc = peer(k)
        return 4 * px + 2 * py + pc

    return 4 * x + 2 * y + c, peer, peer_index


def _split_copy(src_refs, land_refs, send_sems, recv_sems, scatter, a, k, outgoing):
    me, peer, peer_index = _peers()
    if outgoing:
        src = src_refs[a].at[peer_index(k)] if scatter else src_refs[a]
        dst = land_refs[a].at[me]
    else:
        src = src_refs[a].at[me] if scatter else src_refs[a]
        dst = land_refs[a].at[peer_index(k)]
    pair = a * (N_DEV - 1) + k - 1
    return pltpu.make_async_remote_copy(src_ref=src, dst_ref=dst, send_sem=send_sems.at[pair],
                                        recv_sem=recv_sems.at[pair], device_id=peer(k),
                                        device_id_type=pl.DeviceIdType.MESH)


def _gather_two_level(srcs, name):
    na = len(srcs)

    def body(*refs):
        src_refs, out_refs = refs[:na], refs[na:2 * na]
        send_sems, recv_sems = refs[2 * na:]
        x, y, c = lax.axis_index("x"), lax.axis_index("y"), lax.axis_index("c")
        me, sibling = (x, y, c), (x, y, 1 - c)
        chips = [(1 - x, y), (x, 1 - y), (1 - x, 1 - y)]

        def copy(a, k, block, to, src=None):
            px, py, pc = block
            slot = out_refs[a].at[4 * px + 2 * py + pc]
            return pltpu.make_async_remote_copy(
                src_ref=slot if src is None else src, dst_ref=slot, send_sem=send_sems.at[a * (N_DEV - 1) + k],
                recv_sem=recv_sems.at[a * (N_DEV - 1) + k], device_id=to, device_id_type=pl.DeviceIdType.MESH)

        first = [copy(a, 0, me, sibling, src_refs[a]) for a in range(na)]
        first += [copy(a, 1 + j, me, (*chip, c), src_refs[a]) for j, chip in enumerate(chips) for a in range(na)]
        for cp in first:
            cp.start()
        passed = []
        for j, chip in enumerate(chips):
            for a in range(na):
                copy(a, 1 + j, (*chip, c), me).wait_recv()
                passed.append(copy(a, 4 + j, (*chip, c), sibling))
                passed[-1].start()
        for a in range(na):
            copy(a, 0, sibling, me).wait_recv()
        for j, chip in enumerate(chips):
            for a in range(na):
                copy(a, 4 + j, (*chip, 1 - c), me).wait_recv()
        for cp in first + passed:
            cp.wait_send()

    return pl.pallas_call(
        body, name=name, in_specs=[_ANY] * na, out_specs=[_ANY] * na,
        out_shape=[jax.ShapeDtypeStruct((N_DEV,) + s.shape, s.dtype) for s in srcs],
        scratch_shapes=[pltpu.SemaphoreType.DMA((na * (N_DEV - 1),)), pltpu.SemaphoreType.DMA((na * (N_DEV - 1),))],
    )(*srcs)


def _exchange_start(srcs, scatter, after, name):
    na = len(srcs)
    land_shapes = [s.shape if scatter else (N_DEV,) + s.shape for s in srcs]
    has_after = after is not None

    def body(*refs):
        src_refs, land_refs = refs[:na], refs[na:2 * na]
        send_sems, recv_sems = refs[2 * na + has_after], refs[2 * na + has_after + 1]
        token = refs[-1]
        for k in range(1, N_DEV):
            for a in range(na):
                _split_copy(src_refs, land_refs, send_sems, recv_sems, scatter, a, k, True).start()
        token[...] = jnp.zeros_like(token)

    sems = pltpu.SemaphoreType.DMA((na * (N_DEV - 1),))
    out_shape = ([sems, sems] + [pltpu.HBM(s.shape, s.dtype) for s in srcs]
                 + [pltpu.HBM(shp, s.dtype) for shp, s in zip(land_shapes, srcs)] + [jax.ShapeDtypeStruct((SUBLANES, 128), F32)])
    args = [pltpu.with_memory_space_constraint(s, pltpu.HBM) for s in srcs]
    args += [pltpu.with_memory_space_constraint(lax.empty(shp, s.dtype), pltpu.HBM) for shp, s in zip(land_shapes, srcs)]
    if has_after:
        args.append(after)
    res = pl.pallas_call(
        body, name=name, in_specs=[_HBM] * (2 * na) + [_ANY] * has_after,
        out_specs=[_SEM, _SEM] + [_HBM] * (2 * na) + [pl.BlockSpec(memory_space=pltpu.VMEM)], out_shape=out_shape,
        input_output_aliases={i: 2 + i for i in range(2 * na)},
        compiler_params=pltpu.CompilerParams(has_side_effects=pltpu.SideEffectType.DATAFLOW_SIDE_EFFECTING),
    )(*args)
    handle = dict(send=res[0], recv=res[1], srcs=list(res[2:2 + na]), lands=list(res[2 + na:2 + 2 * na]), scatter=scatter)
    return handle, res[-1]


def _exchange_wait(handle, after, name):
    srcs, lands, scatter = handle["srcs"], handle["lands"], handle["scatter"]
    na = len(srcs)

    def body(*refs):
        src_refs, land_refs = refs[:na], refs[na:2 * na]
        send_sems, recv_sems = refs[2 * na], refs[2 * na + 1]
        for k in range(1, N_DEV):
            for a in range(na):
                _split_copy(src_refs, land_refs, send_sems, recv_sems, scatter, a, k, True).wait_send()
                _split_copy(src_refs, land_refs, send_sems, recv_sems, scatter, a, k, False).wait_recv()

    res = pl.pallas_call(
        body, name=name, in_specs=[_HBM] * (2 * na) + [_SEM, _SEM, _ANY], out_specs=[_HBM] * (2 * na),
        out_shape=[pltpu.HBM(s.shape, s.dtype) for s in srcs] + [pltpu.HBM(s.shape, s.dtype) for s in lands],
        input_output_aliases={i: i for i in range(2 * na)},
        compiler_params=pltpu.CompilerParams(has_side_effects=pltpu.SideEffectType.DATAFLOW_SIDE_EFFECTING),
    )(*srcs, *lands, handle["send"], handle["recv"], after)
    return list(res[:na]), list(res[na:])


def _layout(D):
    aq, akv = SWA_Q_HEADS * SWA_HEAD_DIM, SWA_KV_HEADS * SWA_HEAD_DIM
    w = SGU_GROUPS * SGU_DIM
    return aq, akv, w


class _Seg:
    def __init__(self, D, rq, rkv):
        aq, akv, w = _layout(D)
        src = {}
        o = 0
        for nm, wd in (("qa", aq), ("ka", akv), ("va", akv), ("cq", rq), ("ckv", rkv), ("kr", MLA_ROPE), ("hu", w), ("hv", w),
                       ("g", 3 * D)):
            src[nm] = (o, wd)
            o += wd
        self.n_in = o
        self.order = ("g", "qa", "hu", "hv", "cq", "ckv", "ka", "va", "kr")
        self.src = src
        self.off = {}
        o = 0
        for nm in self.order:
            self.off[nm] = o
            o += src[nm][1]
        self.width = {nm: src[nm][1] for nm in self.order}
        self.n_pad = -(-o // 1536) * 1536 if o > 1536 else -(-o // 512) * 512
        self.used = o

    def permute(self, w):
        parts = [w[:, self.src[nm][0]:self.src[nm][0] + self.src[nm][1]] for nm in self.order]
        parts.append(jnp.zeros((w.shape[0], self.n_pad - self.used), w.dtype))
        return jnp.concatenate(parts, axis=1)

    def unpermute(self, w):
        names = sorted(self.order, key=lambda nm: self.src[nm][0])
        return jnp.concatenate([w[:, self.off[nm]:self.off[nm] + self.width[nm]] for nm in names], axis=1)


def _uq_permute(w):
    R = w.shape[0]
    H = MLA_HEADS
    w3 = w.reshape(R, H, MLA_NOPE + MLA_ROPE)
    return jnp.concatenate([w3[:, :, :MLA_NOPE].reshape(R, H * MLA_NOPE), w3[:, :, MLA_NOPE:].reshape(R, H * MLA_ROPE)], axis=1)


def _uq_unpermute(w):
    R = w.shape[0]
    H = MLA_HEADS
    n = w[:, :H * MLA_NOPE].reshape(R, H, MLA_NOPE)
    r = w[:, H * MLA_NOPE:].reshape(R, H, MLA_ROPE)
    return jnp.concatenate([n, r], axis=2).reshape(R, H * (MLA_NOPE + MLA_ROPE))


def _ukv_permute(w):
    R = w.shape[0]
    w3 = w.reshape(R, MLA_HEADS, MLA_NOPE + MLA_V)
    return jnp.concatenate([w3[:, :, :MLA_NOPE].reshape(R, -1), w3[:, :, MLA_NOPE:].reshape(R, -1)], axis=1)


def _ukv_unpermute(w):
    R = w.shape[0]
    H = MLA_HEADS
    k = w[:, :H * MLA_NOPE].reshape(R, H, MLA_NOPE)
    v = w[:, H * MLA_NOPE:].reshape(R, H, MLA_V)
    return jnp.concatenate([k, v], axis=2).reshape(R, H * (MLA_NOPE + MLA_V))


GROUPS = {"a": ("w_in",), "b": ("w_uq", "w_ukv", "w_proj_a", "w_proj_b", "w_proj_c", "w_o", "b_gate"),
          "c": ("w_up", "w_down", "conv_w")}


def _layer_fwd(l, x, x16, fetch, P, cs, sn, seg, alpha):
    S, D = x.shape
    H, half = MLA_HEADS, MLA_ROPE // 2
    off = seg.off
    nm = lambda s: f"l{l}_{s}"
    sv = {"x16": x16}
    W = {"w_in": seg.permute(fetch(l, "a", x16)["w_in"])}
    h = _mm(x16, W["w_in"], "nn", [(F32, "n")], nm("h"))[0]
    sv["h"] = h
    ya, lse_a = _swa_fwd(h, off["qa"], off["ka"], off["va"], P["sinks"], nm("swa_fwd"))
    cqn, rq = _rms_fwd(h, off["cq"], seg.width["cq"], P["q_norm_g"], nm("rmsq_fwd"))
    ckvn, rkv = _rms_fwd(h, off["ckv"], seg.width["ckv"], P["kv_norm_g"], nm("rmskv_fwd"))
    W.update(fetch(l, "b", cqn))
    W["w_uq"] = _uq_permute(W["w_uq"])
    W["w_ukv"] = _ukv_permute(W["w_ukv"])
    qf = _mm(cqn, W["w_uq"], "nn", [(F32, "n")], nm("uq"))[0]
    kvf = _mm(ckvn, W["w_ukv"], "nn", [(BF16, "n")], nm("ukv"))[0]
    qr = _rope(qf, H * MLA_NOPE, H * MLA_ROPE, cs, sn, False, nm("ropeq_fwd"))
    kr = _rope(h, off["kr"], LANES, cs, sn, False, nm("ropek_fwd"))
    yb, lse_b = _mla_fwd(qf, qr, kvf, kr, nm("mla_fwd"))
    w16 = jnp.where(jnp.tril(jnp.ones((SGU_CHUNK, SGU_CHUNK), bool))[None], P["sgu_w"], 0.0).astype(BF16)
    bt = P["sgu_b"].T
    yc = _sgu_fwd(h, off["hu"], off["hv"], P["sgu_ln_g"], P["sgu_ln_b"], w16, bt, nm("sgu_fwd"))
    merged, z = _merge_fwd([ya, yb, yc], [W["w_proj_a"], W["w_proj_b"], W["w_proj_c"]], h, W["b_gate"], nm("merge_fwd"))
    x1, x1_16, xh1, rs1 = _mm_ln(merged, W["w_o"], x, P["ln1_g"], P["ln1_b"], alpha, nm("wo_ln1"))
    W.update(fetch(l, "c", x1_16))
    up = _mm(x1_16, W["w_up"], "nn", [(F32, "n")], nm("up"))[0]
    a = _glu_fwd(up, W["conv_w"], P["conv_b"], nm("glu_fwd"))
    x2, x2_16, xh2, rs2 = _mm_ln(a, W["w_down"], x1, P["ln2_g"], P["ln2_b"], alpha, nm("down_ln2"))
    sv.update(W=W, ya=ya, lse_a=lse_a, cqn=cqn, rq=rq, ckvn=ckvn, rkv=rkv, qf=qf, qr=qr, kvf=kvf, kr=kr, lse_b=lse_b, yb=yb,
              w16=w16, bt=bt, yc=yc, merged=merged, z=z, x1_16=x1_16, xh1=xh1, rs1=rs1, up=up, a=a, xh2=xh2, rs2=rs2)
    return x2, x2_16, sv


def _dw_chunks(k, a, dy, name, post=None):
    n = dy.shape[1]
    if k not in ROW_SHARDED and post is None and (n // N_DEV) % 128 == 0:
        return _mm(a, dy, "tn", [(BF16, "n")], name, chunk=n // N_DEV)[0]
    g = _mm(a, dy, "tn", [(BF16, "n")], name)[0]
    return _to_chunks(k, g if post is None else post(g))


def _after(arr, token):
    return arr if token is None else arr + token[0:1, 0:1].astype(arr.dtype)


def _layer_bwd(l, dx2, sv, P, cs, sn, seg, alpha, emit):
    S, D = dx2.shape
    H, half = MLA_HEADS, MLA_ROPE // 2
    off = seg.off
    h, W = sv["h"], sv["W"]
    nm = lambda s: f"l{l}_{s}"
    g = {}
    dr2, dr2_16, g["ln2_g"], g["ln2_b"] = _ln_bwd(dx2, sv["xh2"], sv["rs2"], P["ln2_g"], nm("ln2_bwd"))
    g["w_down"] = _dw_chunks("w_down", sv["a"], dr2_16, nm("dw_down"))
    da = _mm(dr2_16, W["w_down"], "nt", [(F32, "n")], nm("da"))[0]
    dcg, dcv, st_g, st_v = _glu_bwd(sv["up"], W["conv_w"], P["conv_b"], da, nm("glu_bwd"))
    F = dcg.shape[1]
    g["conv_w"] = _to_chunks("conv_w", jnp.concatenate([st_g[0:3], st_v[0:3]], axis=1))
    g["conv_b"] = jnp.concatenate([st_g[3:4], st_v[3:4]], axis=1)
    dup = _conv_bwd(dcg, dcv, W["conv_w"], nm("conv_bwd"))
    g["w_up"] = _dw_chunks("w_up", sv["x1_16"], dup, nm("dw_up"))
    token = emit(l, "c", {k: g.pop(k) for k in GROUPS["c"]})
    dx1 = _mm_axpy(dup, W["w_up"], "nt", dr2, alpha, nm("dx1"), dep=token)
    dr1, dr1_16, g["ln1_g"], g["ln1_b"] = _ln_bwd(dx1, sv["xh1"], sv["rs1"], P["ln1_g"], nm("ln1_bwd"))
    g["w_o"] = _dw_chunks("w_o", sv["merged"], dr1_16, nm("dw_o"))
    dmerged = _mm(dr1_16, W["w_o"], "nt", [(F32, "n")], nm("dmerged"))[0]
    dz, dlog, dbg = _merge_bwd(dmerged, sv["z"], h, W["b_gate"], nm("merge_bwd"))
    g["b_gate"] = _to_chunks("b_gate", dbg[0:3])
    g["w_proj_a"] = _dw_chunks("w_proj_a", sv["ya"], dz[0], nm("dw_pa"))
    g["w_proj_b"] = _dw_chunks("w_proj_b", sv["yb"], dz[1], nm("dw_pb"))
    g["w_proj_c"] = _dw_chunks("w_proj_c", sv["yc"], dz[2], nm("dw_pc"))
    dya = _mm(dz[0], W["w_proj_a"], "nt", [(F32, "n")], nm("dya"))[0]
    dyb = _mm(dz[1], W["w_proj_b"], "nt", [(F32, "n")], nm("dyb"))[0]
    dyc = _mm(dz[2], W["w_proj_c"], "nt", [(F32, "n")], nm("dyc"))[0]
    dhu, dhv, g["sgu_w"], db_s, g["sgu_ln_g"], g["sgu_ln_b"] = _sgu_bwd(
        h, off["hu"], off["hv"], P["sgu_ln_g"], P["sgu_ln_b"], sv["w16"], sv["bt"], dyc, nm("sgu_bwd"))
    g["sgu_b"] = db_s[:, :SGU_GROUPS].T
    dqa, dka, dva, dsk = _swa_bwd(h, off["qa"], off["ka"], off["va"], P["sinks"], dya, sv["lse_a"], nm("swa_bwd"))
    g["sinks"] = dsk[0, :SWA_Q_HEADS]
    delta = _rowdot(dyb, sv["yb"], nm("mla_delta"))
    dqn, dqr, dkn, dvv, dkr = _mla_bwd(sv["qf"], sv["qr"], sv["kvf"], sv["kr"], dyb, sv["lse_b"], delta, nm("mla_bwd"))
    dqf = jnp.concatenate([dqn.astype(BF16), _rope(dqr, 0, H * MLA_ROPE, cs, sn, True, nm("ropeq_bwd"))], axis=1)
    dkvf = jnp.concatenate([dkn, dvv], axis=1).astype(BF16)
    dkr16 = _rope(dkr, 0, LANES, cs, sn, True, nm("ropek_bwd"))
    g["w_uq"] = _dw_chunks("w_uq", sv["cqn"], dqf, nm("dw_uq"), _uq_unpermute)
    g["w_ukv"] = _dw_chunks("w_ukv", sv["ckvn"], dkvf, nm("dw_ukv"), _ukv_unpermute)
    token = emit(l, "b", {k: g.pop(k) for k in GROUPS["b"]})
    dcqn = _mm(dqf, W["w_uq"], "nt", [(F32, "n")], nm("dcqn"), dep=token)[0]
    dckvn = _mm(dkvf, W["w_ukv"], "nt", [(F32, "n")], nm("dckvn"), dep=token)[0]
    dcq, g["q_norm_g"] = _rms_bwd(dcqn, h, off["cq"], seg.width["cq"], sv["rq"], P["q_norm_g"], nm("rmsq_bwd"))
    dckv, g["kv_norm_g"] = _rms_bwd(dckvn, h, off["ckv"], seg.width["ckv"], sv["rkv"], P["kv_norm_g"], nm("rmskv_bwd"))
    assert seg.order[-1] == "kr" and seg.n_pad - seg.used >= LANES - MLA_ROPE
    parts = {"g": jnp.concatenate([dlog[0], dlog[1], dlog[2]], axis=1), "qa": dqa, "hu": dhu, "hv": dhv, "cq": dcq, "ckv": dckv,
             "ka": dka.astype(BF16), "va": dva.astype(BF16), "kr": dkr16}
    dh = jnp.concatenate([parts[k] for k in seg.order] + [jnp.zeros((S, seg.n_pad - seg.used - (LANES - MLA_ROPE)), BF16)],
                         axis=1)
    token = emit(l, "a", {"w_in": _dw_chunks("w_in", sv["x16"], dh, nm("dw_in"), seg.unpermute)})
    dx = _mm_axpy(dh, W["w_in"], "nt", dr1, alpha, nm("dx"), dep=token)
    return dx, g


BIG = ("w_in", "w_uq", "w_ukv", "w_proj_a", "w_proj_b", "w_proj_c", "w_o", "w_up", "w_down")
ROW_SHARDED = ("w_proj_b", "w_o", "w_down")
SHARDED_F32 = ("b_gate", "conv_w")
REPLICATED = ("sinks", "q_norm_g", "kv_norm_g", "sgu_ln_g", "sgu_ln_b", "sgu_w", "sgu_b", "ln1_g", "ln1_b", "conv_b", "ln2_g",
              "ln2_b")
WEIGHTS = ("w_in", "b_gate", "sinks", "q_norm_g", "kv_norm_g", "w_uq", "w_ukv", "sgu_ln_g", "sgu_ln_b", "sgu_w", "sgu_b",
           "w_proj_a", "w_proj_b", "w_proj_c", "w_o", "ln1_g", "ln1_b", "w_up", "conv_w", "conv_b", "w_down", "ln2_g", "ln2_b")


def _step_local(x, positions, target, small, rq, rkv, fetch, emit, token=None):
    S, D = x.shape
    L = small["sinks"].shape[0]
    alpha = (2 * L) ** 0.25
    seg = _Seg(D, rq, rkv)
    inv_freq = ROPE_THETA ** (-jnp.arange(0, MLA_ROPE, 2, dtype=F32) / MLA_ROPE)
    ang = positions.astype(F32)[:, None] * inv_freq
    reps = LANES // (MLA_ROPE // 2)
    cs, sn = jnp.tile(jnp.cos(ang), (1, reps)), jnp.tile(jnp.sin(ang), (1, reps))
    rows = ("q_norm_g", "kv_norm_g", "sgu_ln_g", "sgu_ln_b", "ln1_g", "ln1_b", "conv_b", "ln2_g", "ln2_b")
    layers = [{k: small[k][l].reshape(1, -1) if k in rows else small[k][l] for k in small} for l in range(L)]
    saved = []
    x16 = _after(x, token).astype(BF16)
    for l in range(L):
        x, x16, sv = _layer_fwd(l, x, x16, fetch, layers[l], cs, sn, seg, alpha)
        saved.append(sv)
    loss, dx = _loss(x, target, "loss")
    grads = [None] * L
    for l in reversed(range(L)):
        dx, grads[l] = _layer_bwd(l, dx, saved[l], layers[l], cs, sn, seg, alpha, emit)
    out = {k: jnp.stack([grads[l][k].reshape(small[k].shape[1:]) for l in range(L)]) for k in small}
    return loss, dx, out


def _unshard(k, gathered):
    n, r, c = gathered.shape
    if k in ROW_SHARDED:
        return gathered.reshape(n * r, c)
    return gathered.transpose(1, 0, 2).reshape(r, n * c)


def _to_chunks(k, gfull):
    r, c = gfull.shape
    if k in ROW_SHARDED:
        return gfull.reshape(N_DEV, r // N_DEV, c)
    return gfull.reshape(r, N_DEV, c // N_DEV).transpose(1, 0, 2)


def _pack(arrs):
    P = arrs[0].shape[0]
    flat, sizes = [], []
    for a in arrs:
        f = a.reshape(P, -1)
        n = f.shape[1]
        pad = -n % (SUBLANES * 128)
        flat.append(jnp.pad(f, ((0, 0), (0, pad))))
        sizes.append((n, n + pad))
    return jnp.concatenate(flat, axis=1).reshape(P, -1, 128), sizes


def _unpack(packed, sizes, shapes):
    flat = packed.reshape(-1)
    out, o = [], 0
    for (n, npad), shp in zip(sizes, shapes):
        out.append(flat[o:o + n].reshape(shp))
        o += npad
    return out


def kernel(x, positions, w_in, b_gate, sinks, q_norm_g, kv_norm_g, w_uq, w_ukv, sgu_ln_g, sgu_ln_b, sgu_w, sgu_b, w_proj_a, w_proj_b, w_proj_c, w_o, ln1_g, ln1_b, w_up, conv_w, conv_b, w_down, ln2_g, ln2_b, loss_target, m_w_in, m_b_gate, m_sinks, m_q_norm_g, m_kv_norm_g, m_w_uq, m_w_ukv, m_sgu_ln_g, m_sgu_ln_b, m_sgu_w, m_sgu_b, m_w_proj_a, m_w_proj_b, m_w_proj_c, m_w_o, m_ln1_g, m_ln1_b, m_w_up, m_conv_w, m_conv_b, m_w_down, m_ln2_g, m_ln2_b, v_w_in, v_b_gate, v_sinks, v_q_norm_g, v_kv_norm_g, v_w_uq, v_w_ukv, v_sgu_ln_g, v_sgu_ln_b, v_sgu_w, v_sgu_b, v_w_proj_a, v_w_proj_b, v_w_proj_c, v_w_o, v_ln1_g, v_ln1_b, v_w_up, v_conv_w, v_conv_b, v_w_down, v_ln2_g, v_ln2_b):
    given = dict(locals())
    w = {k: given[k] for k in WEIGHTS}
    mom = {k: given["m_" + k] for k in WEIGHTS}
    var = {k: given["v_" + k] for k in WEIGHTS}

    L = w_in.shape[0]
    order = [(l, grp) for l in range(L) for grp in ("a", "b", "c")]

    first = [w[k][0].astype(BF16) for k in GROUPS["a"]]
    first_lands = _gather_two_level(first, "gather_first")
    gathers, token = {}, first_lands[0]
    for l, grp in order[1:]:
        srcs = [w[k][l].astype(BF16) if k in BIG else w[k][l] for k in GROUPS[grp]]
        gathers[l, grp], token = _exchange_start(srcs, False, token, f"gather_start_l{l}{grp}")

    me = 4 * lax.axis_index("x") + 2 * lax.axis_index("y") + lax.axis_index("c")
    mine = (jnp.arange(N_DEV) == me)[:, None, None]

    def fetch(l, grp, after):
        if (l, grp) == order[0]:
            srcs, lands = first, first_lands
        else:
            srcs, lands = _exchange_wait(gathers[l, grp], after, f"gather_wait_l{l}{grp}")
        return {k: _unshard(k, jnp.where(mine, srcs[i][None], lands[i])) for i, k in enumerate(GROUPS[grp])}

    scatters = {}

    def emit(l, grp, chunks):
        scatters[l, grp], tok = _exchange_start([chunks[k] for k in GROUPS[grp]], True, None, f"scatter_start_l{l}{grp}")
        return tok

    small = {k: w[k] for k in REPLICATED}
    loss, grad_x, g = _step_local(x[0], positions[0], loss_target[0], small, w_uq.shape[1], w_ukv.shape[1], fetch, emit, token)
    loss = lax.psum(loss[0, 0], AXES)

    packed, sizes = _pack([g[k][None] for k in REPLICATED])
    small_grads, after = _exchange_start([packed[0]], False, grad_x, "gather_small_grads_start")

    me1 = me.astype(jnp.int32).reshape(1)
    res = {}
    for grp in ("c", "b", "a"):
        slots, own = {}, {}
        for l in reversed(range(L)):
            srcs, lands = _exchange_wait(scatters[l, grp], after, f"scatter_wait_l{l}{grp}")
            for i, k in enumerate(GROUPS[grp]):
                slots[k, l], own[k, l] = lands[i], srcs[i]
        for k in GROUPS[grp]:
            res[k] = _adamw([slots[k, l] for l in range(L)], [own[k, l] for l in range(L)], me1, w[k], mom[k], var[k],
                            "adamw_" + k, dep=after)
            after = res[k][1]

    srcs, lands = _exchange_wait(small_grads, after, "gather_small_grads_wait")
    parts = jnp.where(mine, srcs[0][None], lands[0])
    shapes = [w[k].shape for k in REPLICATED]
    pw, _ = _pack([w[k][None] for k in REPLICATED])
    pm, _ = _pack([mom[k][None] for k in REPLICATED])
    pv, _ = _pack([var[k][None] for k in REPLICATED])
    outs = _adamw([parts], None, me1, pw, pm, pv, "adamw_small")
    unpacked = [_unpack(o, sizes, shapes) for o in outs]
    for i, k in enumerate(REPLICATED):
        res[k] = [unpacked[j][i] for j in range(4)]

    return (loss, grad_x[None], *[res[k][0] for k in WEIGHTS], *[res[k][1] for k in WEIGHTS],
            *[res[k][2] for k in WEIGHTS], *[res[k][3] for k in WEIGHTS])
```

```python
import functools
import math

import jax
import jax.numpy as jnp
from jax import lax
from jax.experimental import pallas as pl
from jax.experimental.pallas import tpu as pltpu

F32 = jnp.float32
BF16 = jnp.bfloat16

SWA_Q_HEADS = 16
SWA_KV_HEADS = 2
SWA_HEAD_DIM = 64
SWA_BLOCK = 128
MLA_HEADS = 16
MLA_NOPE = 128
MLA_ROPE = 64
MLA_V = 128
SGU_GROUPS = 8
SGU_DIM = 128
SGU_CHUNK = 128
ROPE_THETA = 10000.0
EPS = 1e-5
MASK = -1e30
ADAM_LR = 0.001
ADAM_B1 = 0.9
ADAM_B2 = 0.999
ADAM_EPS = 1e-08
ADAM_WD = 0.01
ADAM_STEP = 10

N_DEV = 8
AXES = ("x", "y", "c")
VMEM_LIMIT = 56 * 1024 * 1024
MLA_TILE = 512
MLA_FWD_TILE = 1024
ROW_TILE = 512
MAX_TK = 2816
SUBLANES = 8


def _tile(n, prefs):
    for p in prefs:
        if n % p == 0:
            return p
    return n


def _params(sem):
    return pltpu.CompilerParams(dimension_semantics=sem, vmem_limit_bytes=VMEM_LIMIT)


def _cols(tm, width, off):
    assert off % width == 0, (off, width)
    blk = off // width
    return pl.BlockSpec((tm, width), lambda i, *_: (i, blk))


def _full(shape):
    nd = len(shape)
    return pl.BlockSpec(shape, lambda *_: (0,) * nd)


def _sigmoid(v):
    return 1.0 / (1.0 + jnp.exp(-v))


def _gelu(v):
    return 0.5 * v * (1.0 + lax.erf(v * (2.0 ** -0.5)))


def _gelu_grad(v):
    return 0.5 * (1.0 + lax.erf(v * (2.0 ** -0.5))) + v * jnp.exp(-0.5 * v * v) * (1.0 / math.sqrt(2.0 * math.pi))


_DIMS = {"nn": (((1,), (0,)), ((), ())), "nt": (((1,), (1,)), ((), ())), "tn": (((0,), (0,)), ((), ()))}


def _mm(a, b, mode, outs, name, *, extras=(), epilogue=None, full_n=False, dep=None, chunk=None):
    if mode == "nn":
        (M, K), (K2, N) = a.shape, b.shape
    elif mode == "nt":
        (M, K), (N, K2) = a.shape, b.shape
    else:
        (K, M), (K2, N) = a.shape, b.shape
    assert K == K2, (a.shape, b.shape, mode)
    tm = _tile(M, (1024, 512, 256, 128))
    tn = N if full_n else _tile(N, (1024, 768, 512, 384, 256, 128))
    if full_n:
        tm = _tile(M, (512, 256, 128))
    if chunk is not None:
        tn = chunk if chunk <= 1536 else _tile(chunk, (1024, 768, 512, 384, 256, 128))
        assert N % chunk == 0 and chunk % tn == 0 and tn % 128 == 0, (N, chunk, tn)
    max_tk = MAX_TK // 2 if full_n else MAX_TK
    tk = max(d for d in range(128, min(K, max_tk) + 1, 128) if K % d == 0) if K % 128 == 0 else K
    nk = K // tk
    if mode == "nn":
        a_spec = pl.BlockSpec((tm, tk), lambda i, j, k: (i, k))
        b_spec = pl.BlockSpec((tk, tn), lambda i, j, k: (k, j))
    elif mode == "nt":
        a_spec = pl.BlockSpec((tm, tk), lambda i, j, k: (i, k))
        b_spec = pl.BlockSpec((tn, tk), lambda i, j, k: (j, k))
    else:
        a_spec = pl.BlockSpec((tk, tm), lambda i, j, k: (k, i))
        b_spec = pl.BlockSpec((tk, tn), lambda i, j, k: (k, j))
    in_specs = [a_spec, b_spec]
    for arr, kind in extras:
        if kind == "tile":
            in_specs.append(pl.BlockSpec((tm, tn), lambda i, j, k: (i, j)))
        else:
            in_specs.append(pl.BlockSpec((1, tn), lambda i, j, k: (0, j)))
    out_specs, out_shape = [], []
    for dt, kind in outs:
        if kind == "n" and chunk is not None:
            per = chunk // tn
            out_specs.append(pl.BlockSpec((None, tm, tn), lambda i, j, k: (lax.div(j, per), i, lax.rem(j, per))))
            out_shape.append(jax.ShapeDtypeStruct((N // chunk, M, chunk), dt))
        elif kind == "n":
            out_specs.append(pl.BlockSpec((tm, tn), lambda i, j, k: (i, j)))
            out_shape.append(jax.ShapeDtypeStruct((M, N), dt))
        else:
            assert tn == N
            out_specs.append(pl.BlockSpec((tm, 1), lambda i, j, k: (i, 0)))
            out_shape.append(jax.ShapeDtypeStruct((M, 1), dt))
    ne, no = len(extras), len(outs)
    deps = []
    if dep is not None:
        in_specs.append(_full(dep.shape))
        deps = [dep]
    dims = _DIMS[mode]
    if epilogue is None:
        epilogue = lambda acc: (acc,) * no

    def body(*refs):
        a_ref, b_ref = refs[0], refs[1]
        ex = refs[2:2 + ne]
        out = refs[len(refs) - 1 - no:len(refs) - 1]
        acc = refs[-1]
        k = pl.program_id(2)
        part = lax.dot_general(a_ref[...].astype(BF16), b_ref[...].astype(BF16), dims, preferred_element_type=F32)

        def finish(total):
            res = epilogue(total, *[e[...] for e in ex])
            for o, r in zip(out, res):
                o[...] = r.astype(o.dtype)

        if nk == 1:
            finish(part)
            return

        @pl.when(k == 0)
        def _():
            acc[...] = part

        @pl.when((k > 0) & (k < nk - 1))
        def _():
            acc[...] += part

        @pl.when(k == nk - 1)
        def _():
            finish(acc[...] + part)

    res = pl.pallas_call(
        body, name=name, grid=(M // tm, N // tn, nk), in_specs=in_specs, out_specs=out_specs, out_shape=out_shape,
        scratch_shapes=[pltpu.VMEM((tm, tn), F32)],
        compiler_params=_params(("parallel", "parallel", "arbitrary")),
    )(a, b, *[e[0] for e in extras], *deps)
    return res


def _ln_epilogue(alpha):
    def epi(acc, x, g, b):
        r = alpha * x + acc
        mu = jnp.mean(r, axis=-1, keepdims=True)
        d = r - mu
        var = jnp.mean(d * d, axis=-1, keepdims=True)
        rstd = lax.rsqrt(var + EPS)
        xhat = d * rstd
        y = xhat * g + b
        return y, y, xhat, rstd
    return epi


def _mm_ln(a, w, x, g, b, alpha, name):
    return _mm(a, w, "nn", [(F32, "n"), (BF16, "n"), (F32, "n"), (F32, "1")], name,
               extras=[(x, "tile"), (g, "row"), (b, "row")], epilogue=_ln_epilogue(alpha), full_n=True)


def _mm_axpy(a, w, mode, r, alpha, name, dep=None):
    return _mm(a, w, mode, [(F32, "n")], name, extras=[(r, "tile")],
               epilogue=lambda acc, rv: (acc + alpha * rv,), dep=dep)[0]


def _ln_bwd(dy, xhat, rstd, g, name):
    S, D = dy.shape
    tm = _tile(S, (256, 128))

    def body(dy_ref, xh_ref, rs_ref, g_ref, dr_ref, dr16_ref, dg_ref, db_ref):
        @pl.when(pl.program_id(0) == 0)
        def _():
            dg_ref[...] = jnp.zeros_like(dg_ref)
            db_ref[...] = jnp.zeros_like(db_ref)

        dyv, xh = dy_ref[...], xh_ref[...]
        dxh = dyv * g_ref[...]
        m1 = jnp.mean(dxh, axis=-1, keepdims=True)
        m2 = jnp.mean(dxh * xh, axis=-1, keepdims=True)
        dr = rs_ref[...] * (dxh - m1 - xh * m2)
        dr_ref[...] = dr
        dr16_ref[...] = dr.astype(BF16)
        dg_ref[...] += jnp.sum(dyv * xh, axis=0, keepdims=True)
        db_ref[...] += jnp.sum(dyv, axis=0, keepdims=True)

    row = pl.BlockSpec((tm, D), lambda i: (i, 0))
    return pl.pallas_call(
        body, name=name, grid=(S // tm,),
        in_specs=[row, row, pl.BlockSpec((tm, 1), lambda i: (i, 0)), _full((1, D))],
        out_specs=[row, row, _full((1, D)), _full((1, D))],
        out_shape=[jax.ShapeDtypeStruct((S, D), F32), jax.ShapeDtypeStruct((S, D), BF16),
                   jax.ShapeDtypeStruct((1, D), F32), jax.ShapeDtypeStruct((1, D), F32)],
        compiler_params=_params(("arbitrary",)),
    )(dy, xhat, rstd, g)


def _rms_fwd(h, off, width, g, name):
    S = h.shape[0]
    tm = _tile(S, (ROW_TILE, 256, 128))

    def body(c_ref, g_ref, y_ref, r_ref):
        c = c_ref[...]
        r = lax.rsqrt(jnp.mean(c * c, axis=-1, keepdims=True) + EPS)
        y_ref[...] = (c * r * g_ref[...]).astype(BF16)
        r_ref[...] = r

    return pl.pallas_call(
        body, name=name, grid=(S // tm,),
        in_specs=[_cols(tm, width, off), _full((1, width))],
        out_specs=[pl.BlockSpec((tm, width), lambda i: (i, 0)), pl.BlockSpec((tm, 1), lambda i: (i, 0))],
        out_shape=[jax.ShapeDtypeStruct((S, width), BF16), jax.ShapeDtypeStruct((S, 1), F32)],
        compiler_params=_params(("parallel",)),
    )(h, g)


def _rms_bwd(dy, h, off, width, rstd, g, name):
    S = h.shape[0]
    tm = _tile(S, (ROW_TILE, 256, 128))

    def body(dy_ref, c_ref, r_ref, g_ref, dc_ref, dg_ref):
        @pl.when(pl.program_id(0) == 0)
        def _():
            dg_ref[...] = jnp.zeros_like(dg_ref)

        dyv, c, r = dy_ref[...], c_ref[...], r_ref[...]
        dyg = dyv * g_ref[...]
        m = jnp.mean(dyg * c, axis=-1, keepdims=True)
        dc_ref[...] = (r * dyg - c * (r * r * r) * m).astype(BF16)
        dg_ref[...] += jnp.sum(dyv * c * r, axis=0, keepdims=True)

    return pl.pallas_call(
        body, name=name, grid=(S // tm,),
        in_specs=[pl.BlockSpec((tm, width), lambda i: (i, 0)), _cols(tm, width, off),
                  pl.BlockSpec((tm, 1), lambda i: (i, 0)), _full((1, width))],
        out_specs=[pl.BlockSpec((tm, width), lambda i: (i, 0)), _full((1, width))],
        out_shape=[jax.ShapeDtypeStruct((S, width), BF16), jax.ShapeDtypeStruct((1, width), F32)],
        compiler_params=_params(("arbitrary",)),
    )(dy, h, rstd, g)


def _loss(y, target, name):
    S, D = y.shape
    tm = _tile(S, (256, 128))

    def body(y_ref, t_ref, l_ref, dy_ref):
        @pl.when(pl.program_id(0) == 0)
        def _():
            l_ref[...] = jnp.zeros_like(l_ref)

        err = y_ref[...] - t_ref[...]
        dy_ref[...] = err * (1.0 / D)
        per_tok = jnp.mean(err * err, axis=-1, keepdims=True)
        l_ref[...] += 0.5 * jnp.sum(per_tok, axis=0, keepdims=True)

    row = pl.BlockSpec((tm, D), lambda i: (i, 0))
    return pl.pallas_call(
        body, name=name, grid=(S // tm,), in_specs=[row, row], out_specs=[_full((1, 1)), row],
        out_shape=[jax.ShapeDtypeStruct((1, 1), F32), jax.ShapeDtypeStruct((S, D), F32)],
        compiler_params=_params(("arbitrary",)),
    )(y, target)


LANES = 128


def _rope(x, off, width, cs2, sn2, bwd, name):
    S = cs2.shape[0]
    tm = _tile(S, (ROW_TILE, 256, 128))
    stacked = x.ndim == 3
    half = MLA_ROPE // 2
    assert width % LANES == 0 and MLA_ROPE * 2 == LANES

    def rot(v):
        lane = lax.broadcasted_iota(jnp.int32, v.shape, 1)
        return jnp.where((lane & (MLA_ROPE - 1)) < half, -pltpu.roll(v, LANES - half, 1), pltpu.roll(v, half, 1))

    def body(x_ref, c_ref, s_ref, y_ref):
        c, s = c_ref[...], s_ref[...]
        for g in range(width // LANES):
            cols = slice(g * LANES, (g + 1) * LANES)
            v = jnp.sum(x_ref[...], axis=0) if stacked else x_ref[:, cols].astype(F32)
            y = v * c - rot(v * s) if bwd else v * c + rot(v) * s
            y_ref[:, cols] = y.astype(BF16)

    row = pl.BlockSpec((tm, LANES), lambda i: (i, 0))
    x_spec = pl.BlockSpec((x.shape[0], tm, LANES), lambda i: (0, i, 0)) if stacked else _cols(tm, width, off)
    return pl.pallas_call(
        body, name=name, grid=(S // tm,), in_specs=[x_spec, row, row],
        out_specs=pl.BlockSpec((tm, width), lambda i: (i, 0)), out_shape=jax.ShapeDtypeStruct((S, width), BF16),
        compiler_params=_params(("parallel",)),
    )(x, cs2, sn2)


def _swa_mask(n, rows):
    blk = SWA_BLOCK
    row = lax.broadcasted_iota(jnp.int32, (rows, 2 * blk), 0) & (blk - 1)
    col = lax.broadcasted_iota(jnp.int32, (rows, 2 * blk), 1)
    rel = row + blk - col
    return (rel >= 0) & (rel < blk) & ((n > 0) | (col >= blk))


def _swa_specs(off_q, off_k, off_v, stacked):
    blk, aq, akv = SWA_BLOCK, SWA_Q_HEADS * SWA_HEAD_DIM, SWA_KV_HEADS * SWA_HEAD_DIM
    grp = SWA_Q_HEADS // SWA_KV_HEADS
    assert off_q % aq == 0 and off_k % akv == 0 and off_v % akv == 0 and blk & (blk - 1) == 0
    prev = lambda off: pl.BlockSpec((blk, akv), lambda n: (jnp.maximum(n - 1, 0), off // akv))
    cur = lambda off: pl.BlockSpec((blk, akv), lambda n: (n, off // akv))
    sink = _full((SWA_KV_HEADS, grp * blk, 1)) if stacked else pl.BlockSpec(memory_space=pltpu.SMEM)
    return [sink, _cols(blk, aq, off_q), prev(off_k), cur(off_k), prev(off_v), cur(off_v)]


def _swa_sinks(sinks):
    grp = SWA_Q_HEADS // SWA_KV_HEADS
    return jnp.repeat(sinks.reshape(SWA_KV_HEADS, grp), SWA_BLOCK, axis=1)[:, :, None]


def _swa_stack(x, kv):
    hd, grp = SWA_HEAD_DIM, SWA_Q_HEADS // SWA_KV_HEADS
    return jnp.concatenate([x[:, (kv * grp + g) * hd:(kv * grp + g + 1) * hd] for g in range(grp)], axis=0)


def _swa_fwd(h, off_q, off_k, off_v, sinks, name):
    S = h.shape[0]
    blk, hd, nh, nkv = SWA_BLOCK, SWA_HEAD_DIM, SWA_Q_HEADS, SWA_KV_HEADS
    grp = nh // nkv
    aq = nh * hd
    scale = hd ** -0.5

    def body(sink_ref, q_ref, kp_ref, kc_ref, vp_ref, vc_ref, o_ref, lse_ref):
        valid = _swa_mask(pl.program_id(0), blk)
        q = q_ref[...].astype(BF16)
        k2 = jnp.concatenate([kp_ref[...], kc_ref[...]], axis=0).astype(BF16)
        v2 = jnp.concatenate([vp_ref[...], vc_ref[...]], axis=0).astype(BF16)
        for hh in range(nh):
            kv = hh // grp
            qh = q[:, hh * hd:(hh + 1) * hd]
            kh = k2[:, kv * hd:(kv + 1) * hd]
            vh = v2[:, kv * hd:(kv + 1) * hd]
            s = lax.dot_general(qh, kh, _DIMS["nt"], preferred_element_type=F32) * scale
            s = jnp.where(valid, s, MASK)
            sk = sink_ref[hh]
            m = jnp.maximum(jnp.max(s, axis=1, keepdims=True), sk)
            p = jnp.exp(s - m)
            l = jnp.sum(p, axis=1, keepdims=True) + jnp.exp(sk - m)
            o_ref[:, hh * hd:(hh + 1) * hd] = jnp.dot((p / l).astype(BF16), vh, preferred_element_type=F32).astype(BF16)
            lse_ref[:, hh:hh + 1] = m + jnp.log(l)

    return pl.pallas_call(
        body, name=name, grid=(S // blk,), in_specs=_swa_specs(off_q, off_k, off_v, False),
        out_specs=[pl.BlockSpec((blk, aq), lambda n: (n, 0)), pl.BlockSpec((blk, nh), lambda n: (n, 0))],
        out_shape=[jax.ShapeDtypeStruct((S, aq), BF16), jax.ShapeDtypeStruct((S, nh), F32)],
        compiler_params=_params(("parallel",)),
    )(sinks, h, h, h, h, h)


def _swa_bwd(h, off_q, off_k, off_v, sinks, dout, lse, name):
    S = h.shape[0]
    blk, hd, nh, nkv = SWA_BLOCK, SWA_HEAD_DIM, SWA_Q_HEADS, SWA_KV_HEADS
    grp = nh // nkv
    aq, akv = nh * hd, nkv * hd
    scale = hd ** -0.5

    def body(sink_ref, q_ref, kp_ref, kc_ref, vp_ref, vc_ref, do_ref, lse_ref, dq_ref, dk_ref, dv_ref, ds_ref):
        n = pl.program_id(0)

        @pl.when(n == 0)
        def _():
            dk_ref[...] = jnp.zeros_like(dk_ref)
            dv_ref[...] = jnp.zeros_like(dv_ref)
            ds_ref[...] = jnp.zeros_like(ds_ref)

        valid = _swa_mask(n, grp * blk)
        q = q_ref[...].astype(BF16)
        k2 = jnp.concatenate([kp_ref[...], kc_ref[...]], axis=0).astype(BF16)
        v2 = jnp.concatenate([vp_ref[...], vc_ref[...]], axis=0).astype(BF16)
        do = do_ref[...]
        lane = lax.broadcasted_iota(jnp.int32, (1, 128), 1)
        dsink = jnp.zeros((1, 128), F32)
        cur = pl.ds(pl.multiple_of(n * blk, blk), blk)
        prev = pl.ds(pl.multiple_of(jnp.maximum(n - 1, 0) * blk, blk), blk)
        for kv in range(nkv):
            kh = k2[:, kv * hd:(kv + 1) * hd]
            vh = v2[:, kv * hd:(kv + 1) * hd]
            qs = _swa_stack(q, kv)
            dos = _swa_stack(do, kv)
            dos16 = dos.astype(BF16)
            lse = jnp.concatenate([lse_ref[:, kv * grp + g:kv * grp + g + 1] for g in range(grp)], axis=0)
            s = lax.dot_general(qs, kh, _DIMS["nt"], preferred_element_type=F32) * scale
            s = jnp.where(valid, s, MASK)
            p = jnp.exp(s - lse)
            p16 = p.astype(BF16)
            o = jnp.dot(p16, vh, preferred_element_type=F32)
            delta = jnp.sum(dos * o, axis=1, keepdims=True)
            dp = lax.dot_general(dos16, vh, _DIMS["nt"], preferred_element_type=F32)
            ds16 = (p * (dp - delta) * scale).astype(BF16)
            dqs = jnp.dot(ds16, kh, preferred_element_type=F32).astype(BF16)
            dk_acc = lax.dot_general(ds16, qs, _DIMS["tn"], preferred_element_type=F32)
            dv_acc = lax.dot_general(p16, dos16, _DIMS["tn"], preferred_element_type=F32)
            dsk = jnp.exp(sink_ref[kv] - lse) * delta
            for g in range(grp):
                hh = kv * grp + g
                dq_ref[:, hh * hd:(hh + 1) * hd] = dqs[g * blk:(g + 1) * blk]
                dsink += jnp.where(lane == hh, -jnp.sum(dsk[g * blk:(g + 1) * blk], axis=0, keepdims=True), 0.0)
            cols = slice(kv * hd, (kv + 1) * hd)
            dk_ref[cur, cols] += dk_acc[blk:]
            dv_ref[cur, cols] += dv_acc[blk:]

            @pl.when(n > 0)
            def _():
                dk_ref[prev, cols] += dk_acc[:blk]
                dv_ref[prev, cols] += dv_acc[:blk]

        ds_ref[...] += dsink

    return pl.pallas_call(
        body, name=name, grid=(S // blk,),
        in_specs=_swa_specs(off_q, off_k, off_v, True) + [pl.BlockSpec((blk, aq), lambda n: (n, 0)),
                                                    pl.BlockSpec((blk, nh), lambda n: (n, 0))],
        out_specs=[pl.BlockSpec((blk, aq), lambda n: (n, 0)), _full((S, akv)), _full((S, akv)), _full((1, 128))],
        out_shape=[jax.ShapeDtypeStruct((S, aq), BF16), jax.ShapeDtypeStruct((S, akv), F32),
                   jax.ShapeDtypeStruct((S, akv), F32), jax.ShapeDtypeStruct((1, 128), F32)],
        compiler_params=_params(("arbitrary",)),
    )(_swa_sinks(sinks), h, h, h, h, h, dout, lse)


def _causal(i, j, t):
    row = i * t + lax.broadcasted_iota(jnp.int32, (t, t), 0)
    col = j * t + lax.broadcasted_iota(jnp.int32, (t, t), 1)
    return col <= row


def _pair_rope(qr, hh):
    lane = lax.broadcasted_iota(jnp.int32, qr.shape, 1)
    return jnp.where(lane < MLA_ROPE, qr if hh == 0 else pltpu.roll(qr, MLA_ROPE, 1), jnp.zeros_like(qr))


def _mla_fwd(qf, qr, kvf, kr, name):
    S = qr.shape[0]
    H, dn, dv = MLA_HEADS, MLA_NOPE, MLA_V
    assert H % 2 == 0 and dn == LANES and dv == LANES and 2 * MLA_ROPE == LANES
    t = _tile(S, (MLA_FWD_TILE, 512, 256, 128))
    nq = S // t
    scale = (MLA_NOPE + MLA_ROPE) ** -0.5

    def body(qn_ref, qr_ref, kn_ref, kr_ref, v_ref, o_ref, o16_ref, lse_ref, *state):
        i = pl.program_id(1)
        for hh in range(2):
            m_s, l_s, acc_s = state[3 * hh:3 * hh + 3]
            m_s[...] = jnp.full_like(m_s, -jnp.inf)
            l_s[...] = jnp.zeros_like(l_s)
            acc_s[...] = jnp.zeros_like(acc_s)

        def block(j, masked):
            rows = pl.ds(pl.multiple_of(j * t, t), t)
            for hh in range(2):
                m_s, l_s, acc_s = state[3 * hh:3 * hh + 3]
                cols = slice(hh * LANES, (hh + 1) * LANES)
                q = jnp.concatenate([qn_ref[:, cols].astype(BF16), _pair_rope(qr_ref[...], hh)], axis=1)
                k = jnp.concatenate([kn_ref[rows, cols], kr_ref[rows, :]], axis=1)
                s = lax.dot_general(q, k, _DIMS["nt"], preferred_element_type=F32) * scale
                if masked:
                    s = jnp.where(_causal(0, 0, t), s, MASK)
                m_old = m_s[...]
                m_new = jnp.maximum(m_old, jnp.max(s, axis=1, keepdims=True))
                alpha = jnp.exp(m_old - m_new)
                p = jnp.exp(s - m_new)
                l_s[...] = alpha * l_s[...] + jnp.sum(p, axis=1, keepdims=True)
                acc_s[...] = alpha * acc_s[...] + jnp.dot(p.astype(BF16), v_ref[rows, cols], preferred_element_type=F32)
                m_s[...] = m_new

        def full_block(j, carry):
            block(j, False)
            return carry

        lax.fori_loop(0, i, full_block, 0)
        block(i, True)
        for hh in range(2):
            m_s, l_s, acc_s = state[3 * hh:3 * hh + 3]
            out = acc_s[...] / l_s[...]
            o_ref[:, hh * LANES:(hh + 1) * LANES] = out
            o16_ref[:, hh * LANES:(hh + 1) * LANES] = out.astype(BF16)
            lse_ref[hh] = m_s[...] + jnp.log(l_s[...])

    P = H // 2
    return pl.pallas_call(
        body, name=name, grid=(P, nq),
        in_specs=[pl.BlockSpec((t, 2 * LANES), lambda p, i: (i, p)), pl.BlockSpec((t, LANES), lambda p, i: (i, p)),
                  pl.BlockSpec((S, 2 * LANES), lambda p, i: (0, p)), pl.BlockSpec((S, LANES), lambda p, i: (0, 0)),
                  pl.BlockSpec((S, 2 * LANES), lambda p, i: (0, P + p))],
        out_specs=[pl.BlockSpec((t, 2 * LANES), lambda p, i: (i, p)), pl.BlockSpec((t, 2 * LANES), lambda p, i: (i, p)),
                   pl.BlockSpec((2, t, 1), lambda p, i: (p, i, 0))],
        out_shape=[jax.ShapeDtypeStruct((S, H * dv), F32), jax.ShapeDtypeStruct((S, H * dv), BF16),
                   jax.ShapeDtypeStruct((H, S, 1), F32)],
        scratch_shapes=[pltpu.VMEM((t, 1), F32), pltpu.VMEM((t, 1), F32), pltpu.VMEM((t, dv), F32)] * 2,
        compiler_params=_params(("parallel", "arbitrary")),
    )(qf, qr, kvf, kr, kvf)


def _rowdot(a, b, name):
    S = a.shape[0]
    H = a.shape[1] // LANES
    t = _tile(S, (ROW_TILE, 256, 128))

    def body(a_ref, b_ref, o_ref):
        o_ref[0] = jnp.sum(a_ref[...] * b_ref[...], axis=-1, keepdims=True)

    spec = pl.BlockSpec((t, LANES), lambda h, i: (i, h))
    return pl.pallas_call(
        body, name=name, grid=(H, S // t), in_specs=[spec, spec],
        out_specs=pl.BlockSpec((1, t, 1), lambda h, i: (h, i, 0)),
        out_shape=jax.ShapeDtypeStruct((H, S, 1), F32),
        compiler_params=_params(("parallel", "parallel")),
    )(a, b)


def _mla_bwd(qf, qr, kvf, kr, do, lse, delta, name):
    S = qr.shape[0]
    H, dv = MLA_HEADS, MLA_V
    P = H // 2
    t = _tile(S, (MLA_TILE, 256, 128))
    nq = S // t
    scale = (MLA_NOPE + MLA_ROPE) ** -0.5

    def body(qn_ref, qr_ref, kn_ref, kr_ref, v_ref, do_ref, lse_ref, dl_ref, dqn_ref, dqr_ref, dkn_ref, dv_ref, dkr_ref,
             dkn_s, dv_s, dkr_s):
        j, i = pl.program_id(1), pl.program_id(2)

        @pl.when((j == 0) & (i == 0))
        def _():
            dqn_ref[...] = jnp.zeros_like(dqn_ref)
            dqr_ref[...] = jnp.zeros_like(dqr_ref)

        @pl.when(i == j)
        def _():
            dkn_s[...] = jnp.zeros_like(dkn_s)
            dv_s[...] = jnp.zeros_like(dv_s)
            dkr_s[...] = jnp.zeros_like(dkr_s)

        def block(masked):
            rows = pl.ds(pl.multiple_of(i * t, t), t)
            krv = kr_ref[...]
            for hh in range(2):
                cols = slice(hh * LANES, (hh + 1) * LANES)
                q = jnp.concatenate([qn_ref[:, cols].astype(BF16), _pair_rope(qr_ref[...], hh)], axis=1)
                k = jnp.concatenate([kn_ref[:, cols], krv], axis=1)
                vv, dov = v_ref[:, cols], do_ref[:, cols].astype(BF16)
                s = lax.dot_general(q, k, _DIMS["nt"], preferred_element_type=F32) * scale
                if masked:
                    s = jnp.where(_causal(0, 0, t), s, MASK)
                p = jnp.exp(s - lse_ref[hh])
                p16 = p.astype(BF16)
                dp = lax.dot_general(dov, vv, _DIMS["nt"], preferred_element_type=F32)
                ds16 = (p * (dp - dl_ref[hh]) * scale).astype(BF16)
                dv_s[:, cols] += lax.dot_general(p16, dov, _DIMS["tn"], preferred_element_type=F32)
                dk = lax.dot_general(ds16, q, _DIMS["tn"], preferred_element_type=F32)
                dkn_s[:, cols] += dk[:, :LANES]
                dkr_s[...] += dk[:, LANES:]
                dq = jnp.dot(ds16, k, preferred_element_type=F32)
                dqn_ref[rows, cols] += dq[:, :LANES]
                dqr = dq[:, LANES:]
                dqr_ref[rows, :] += dqr if hh == 0 else pltpu.roll(dqr, MLA_ROPE, 1)

        @pl.when(i == j)
        def _():
            block(True)

        @pl.when(i > j)
        def _():
            block(False)

        @pl.when(i == nq - 1)
        def _():
            dkn_ref[...] = dkn_s[...]
            dv_ref[...] = dv_s[...]
            dkr_ref[...] = dkr_s[...]

    qi = lambda i, j: jnp.maximum(i, j)
    return pl.pallas_call(
        body, name=name, grid=(P, nq, nq),
        in_specs=[pl.BlockSpec((t, 2 * LANES), lambda p, j, i: (qi(i, j), p)), pl.BlockSpec((t, LANES), lambda p, j, i: (qi(i, j), p)),
                  pl.BlockSpec((t, 2 * LANES), lambda p, j, i: (j, p)), pl.BlockSpec((t, LANES), lambda p, j, i: (j, 0)),
                  pl.BlockSpec((t, 2 * LANES), lambda p, j, i: (j, P + p)),
                  pl.BlockSpec((t, 2 * LANES), lambda p, j, i: (qi(i, j), p)),
                  pl.BlockSpec((2, t, 1), lambda p, j, i: (p, qi(i, j), 0)), pl.BlockSpec((2, t, 1), lambda p, j, i: (p, qi(i, j), 0))],
        out_specs=[pl.BlockSpec((S, 2 * LANES), lambda p, j, i: (0, p)), pl.BlockSpec((S, LANES), lambda p, j, i: (0, p)),
                   pl.BlockSpec((t, 2 * LANES), lambda p, j, i: (j, p)), pl.BlockSpec((t, 2 * LANES), lambda p, j, i: (j, p)),
                   pl.BlockSpec((None, t, LANES), lambda p, j, i: (p, j, 0))],
        out_shape=[jax.ShapeDtypeStruct((S, H * MLA_NOPE), F32), jax.ShapeDtypeStruct((S, H * MLA_ROPE), F32),
                   jax.ShapeDtypeStruct((S, H * MLA_NOPE), F32), jax.ShapeDtypeStruct((S, H * dv), F32),
                   jax.ShapeDtypeStruct((P, S, LANES), F32)],
        scratch_shapes=[pltpu.VMEM((t, 2 * LANES), F32), pltpu.VMEM((t, 2 * LANES), F32), pltpu.VMEM((t, LANES), F32)],
        compiler_params=_params(("parallel", "arbitrary", "arbitrary")),
    )(qf, qr, kvf, kr, kvf, do, lse, delta)


def _sgu_norm(hv, lg, lb):
    vg = _gelu(hv)
    mu = jnp.mean(vg, axis=-1, keepdims=True)
    d = vg - mu
    rstd = lax.rsqrt(jnp.mean(d * d, axis=-1, keepdims=True) + EPS)
    xhat = d * rstd
    return xhat, rstd, xhat * lg + lb


def _sgu_fwd(h, off_u, off_v, lg, lb, w16, bt, name):
    S = h.shape[0]
    T, G, C = SGU_CHUNK, SGU_GROUPS, SGU_DIM
    W = G * C

    def body(hu_ref, hv_ref, lg_ref, lb_ref, w_ref, bt_ref, y_ref):
        u = _gelu(hu_ref[...])
        _, _, vn = _sgu_norm(hv_ref[...], lg_ref[...], lb_ref[...])
        vn16 = vn.astype(BF16)
        for g in range(G):
            cols = slice(g * C, (g + 1) * C)
            mixed = jnp.dot(w_ref[g], vn16[:, cols], preferred_element_type=F32) + bt_ref[:, g:g + 1]
            y_ref[:, cols] = (u[:, cols] * mixed).astype(BF16)

    return pl.pallas_call(
        body, name=name, grid=(S // T,),
        in_specs=[_cols(T, W, off_u), _cols(T, W, off_v), _full((1, W)), _full((1, W)), _full((G, T, T)), _full((T, G))],
        out_specs=pl.BlockSpec((T, W), lambda n: (n, 0)),
        out_shape=jax.ShapeDtypeStruct((S, W), BF16),
        compiler_params=_params(("parallel",)),
    )(h, h, lg, lb, w16, bt)


def _sgu_bwd(h, off_u, off_v, lg, lb, w16, bt, dy, name):
    S = h.shape[0]
    T, G, C = SGU_CHUNK, SGU_GROUPS, SGU_DIM
    W = G * C
    nc = S // T

    def body(hu_ref, hv_ref, lg_ref, lb_ref, w_ref, bt_ref, dy_ref, dhu_ref, dhv_ref, dw_ref, db_ref, dlg_ref, dlb_ref,
             dmix_s, dvn_s):
        n = pl.program_id(0)

        @pl.when(n == 0)
        def _():
            dw_ref[...] = jnp.zeros_like(dw_ref)
            dlg_ref[...] = jnp.zeros_like(dlg_ref)
            dlb_ref[...] = jnp.zeros_like(dlb_ref)
            dmix_s[...] = jnp.zeros_like(dmix_s)

        hu, hv, lgv = hu_ref[...], hv_ref[...], lg_ref[...]
        u = _gelu(hu)
        xhat, rstd, vn = _sgu_norm(hv, lgv, lb_ref[...])
        vn16 = vn.astype(BF16)
        dyv = dy_ref[...]
        dmixed = dyv * u
        dmix_s[...] += dmixed
        dmixed16 = dmixed.astype(BF16)
        for g in range(G):
            cols = slice(g * C, (g + 1) * C)
            mixed = jnp.dot(w_ref[g], vn16[:, cols], preferred_element_type=F32) + bt_ref[:, g:g + 1]
            dhu_ref[:, cols] = (dyv[:, cols] * mixed * _gelu_grad(hu[:, cols])).astype(BF16)
            dvn_s[:, cols] = lax.dot_general(w_ref[g], dmixed16[:, cols], _DIMS["tn"], preferred_element_type=F32)
            dw_ref[g] += lax.dot_general(dmixed16[:, cols], vn16[:, cols], _DIMS["nt"], preferred_element_type=F32)
        dvn = dvn_s[...]
        dlg_ref[...] += jnp.sum(dvn * xhat, axis=0, keepdims=True)
        dlb_ref[...] += jnp.sum(dvn, axis=0, keepdims=True)
        dxh = dvn * lgv
        m1 = jnp.mean(dxh, axis=-1, keepdims=True)
        m2 = jnp.mean(dxh * xhat, axis=-1, keepdims=True)
        dvg = rstd * (dxh - m1 - xhat * m2)
        dhv_ref[...] = (dvg * _gelu_grad(hv)).astype(BF16)

        @pl.when(n == nc - 1)
        def _():
            tril = lax.broadcasted_iota(jnp.int32, (T, T), 1) <= lax.broadcasted_iota(jnp.int32, (T, T), 0)
            lane = lax.broadcasted_iota(jnp.int32, (T, 128), 1)
            db = jnp.zeros((T, 128), F32)
            for g in range(G):
                dw_ref[g] = jnp.where(tril, dw_ref[g], 0.0)
                db += jnp.where(lane == g, jnp.sum(dmix_s[:, g * C:(g + 1) * C], axis=1, keepdims=True), 0.0)
            db_ref[...] = db

    row = pl.BlockSpec((T, W), lambda n: (n, 0))
    return pl.pallas_call(
        body, name=name, grid=(nc,),
        in_specs=[_cols(T, W, off_u), _cols(T, W, off_v), _full((1, W)), _full((1, W)), _full((G, T, T)), _full((T, G)), row],
        out_specs=[row, row, _full((G, T, T)), _full((T, 128)), _full((1, W)), _full((1, W))],
        out_shape=[jax.ShapeDtypeStruct((S, W), BF16), jax.ShapeDtypeStruct((S, W), BF16),
                   jax.ShapeDtypeStruct((G, T, T), F32), jax.ShapeDtypeStruct((T, 128), F32),
                   jax.ShapeDtypeStruct((1, W), F32), jax.ShapeDtypeStruct((1, W), F32)],
        scratch_shapes=[pltpu.VMEM((T, W), F32), pltpu.VMEM((T, W), F32)],
        compiler_params=_params(("arbitrary",)),
    )(h, h, lg, lb, w16, bt, dy)


def _merge_fwd(ys, ps, h, bg, name):
    S = h.shape[0]
    D = ps[0].shape[1]
    tm = _tile(S, (1024, 512, 256, 128))
    tn = _tile(D, (256, 128))
    nb = len(ys)

    def body(*refs):
        y_refs, p_refs, l_refs = refs[:nb], refs[nb:2 * nb], refs[2 * nb:3 * nb]
        bg_ref, mg_ref, z_ref = refs[3 * nb:]
        acc = jnp.zeros((tm, tn), F32)
        for b in range(nb):
            z = jnp.dot(y_refs[b][...].astype(BF16), p_refs[b][...], preferred_element_type=F32)
            z_ref[b] = z
            acc += _sigmoid(l_refs[b][...] + bg_ref[b:b + 1, :]) * z
        mg_ref[...] = acc.astype(BF16)

    in_specs = [pl.BlockSpec((tm, y.shape[1]), lambda i, j: (i, 0)) for y in ys]
    in_specs += [pl.BlockSpec((p.shape[0], tn), lambda i, j: (0, j)) for p in ps]
    in_specs += [pl.BlockSpec((tm, tn), functools.partial(lambda i, j, b: (i, b * (D // tn) + j), b=b)) for b in range(nb)]
    in_specs += [pl.BlockSpec((nb, tn), lambda i, j: (0, j))]
    return pl.pallas_call(
        body, name=name, grid=(S // tm, D // tn), in_specs=in_specs,
        out_specs=[pl.BlockSpec((tm, tn), lambda i, j: (i, j)), pl.BlockSpec((nb, tm, tn), lambda i, j: (0, i, j))],
        out_shape=[jax.ShapeDtypeStruct((S, D), BF16), jax.ShapeDtypeStruct((nb, S, D), F32)],
        compiler_params=_params(("parallel", "parallel")),
    )(*ys, *ps, *([h] * nb), bg)


def _merge_bwd(dm, z, h, bg, name):
    nb, S, D = z.shape
    tm = _tile(S, (256, 128))
    tn = _tile(D, (512, 256, 128))

    def body(*refs):
        dm_ref, z_ref = refs[0], refs[1]
        l_refs = refs[2:2 + nb]
        bg_ref, dz_ref, dl_ref, dbg_ref = refs[2 + nb:]

        @pl.when(pl.program_id(1) == 0)
        def _():
            dbg_ref[...] = jnp.zeros_like(dbg_ref)

        dmv = dm_ref[...]
        rows = lax.broadcasted_iota(jnp.int32, (SUBLANES, tn), 0)
        dbg = jnp.zeros((SUBLANES, tn), F32)
        for b in range(nb):
            gt = _sigmoid(l_refs[b][...] + bg_ref[b:b + 1, :])
            dz_ref[b] = (dmv * gt).astype(BF16)
            dl = dmv * z_ref[b] * gt * (1.0 - gt)
            dl_ref[b] = dl.astype(BF16)
            dbg += jnp.where(rows == b, jnp.sum(dl, axis=0, keepdims=True), 0.0)
        dbg_ref[...] += dbg

    in_specs = [pl.BlockSpec((tm, tn), lambda j, i: (i, j)), pl.BlockSpec((nb, tm, tn), lambda j, i: (0, i, j))]
    in_specs += [pl.BlockSpec((tm, tn), functools.partial(lambda j, i, b: (i, b * (D // tn) + j), b=b)) for b in range(nb)]
    in_specs += [pl.BlockSpec((nb, tn), lambda j, i: (0, j))]
    blk3 = pl.BlockSpec((nb, tm, tn), lambda j, i: (0, i, j))
    return pl.pallas_call(
        body, name=name, grid=(D // tn, S // tm), in_specs=in_specs,
        out_specs=[blk3, blk3, pl.BlockSpec((SUBLANES, tn), lambda j, i: (0, j))],
        out_shape=[jax.ShapeDtypeStruct((nb, S, D), BF16), jax.ShapeDtypeStruct((nb, S, D), BF16),
                   jax.ShapeDtypeStruct((SUBLANES, D), F32)],
        compiler_params=_params(("parallel", "arbitrary")),
    )(dm, z, *([h] * nb), bg)


def _shift_down(x, halo, k):
    xr = pltpu.roll(x, k, 0)
    hr = pltpu.roll(halo, k, 0)
    rows = lax.broadcasted_iota(jnp.int32, halo.shape, 0)
    top = jnp.where(rows < k, hr, xr[:SUBLANES])
    return jnp.concatenate([top, xr[SUBLANES:]], axis=0)


def _shift_up(x, halo, k):
    tm = x.shape[0]
    xr = pltpu.roll(x, tm - k, 0)
    hr = pltpu.roll(halo, SUBLANES - k, 0)
    rows = lax.broadcasted_iota(jnp.int32, halo.shape, 0)
    bot = jnp.where(rows >= SUBLANES - k, hr, xr[tm - SUBLANES:])
    return jnp.concatenate([xr[:tm - SUBLANES], bot], axis=0)


def _conv_tiles(S, F):
    return _tile(S, (ROW_TILE, 256, 128)), _tile(F, (512, 256, 128))


def _conv_in_specs(tm, tn, F):
    r8 = tm // SUBLANES
    nf = F // tn
    specs = []
    for half in range(2):
        specs.append(pl.BlockSpec((tm, tn), functools.partial(lambda j, i, o: (i, o + j), o=half * nf)))
        specs.append(pl.BlockSpec((SUBLANES, tn), functools.partial(lambda j, i, o: (jnp.maximum(i * r8 - 1, 0), o + j), o=half * nf)))
    for half in range(2):
        specs.append(pl.BlockSpec((3, tn), functools.partial(lambda j, i, o: (0, o + j), o=half * nf)))
        specs.append(pl.BlockSpec((1, tn), functools.partial(lambda j, i, o: (0, o + j), o=half * nf)))
    return specs


def _conv_apply(x, halo, w, b, first):
    halo = jnp.where(first, 0.0, halo)
    x1 = _shift_down(x, halo, 1)
    x2 = _shift_down(x, halo, 2)
    return b + x2 * w[0:1, :] + x1 * w[1:2, :] + x * w[2:3, :], x1, x2


def _glu_fwd(up, cw, cb, name):
    S, F2 = up.shape
    F = F2 // 2
    tm, tn = _conv_tiles(S, F)

    def body(ug, hg, uv, hv, wg, bgr, wv, bvr, a_ref):
        first = pl.program_id(1) == 0
        cg, _, _ = _conv_apply(ug[...], hg[...], wg[...], bgr[...], first)
        cv, _, _ = _conv_apply(uv[...], hv[...], wv[...], bvr[...], first)
        a_ref[...] = (cg * _sigmoid(cg) * cv).astype(BF16)

    return pl.pallas_call(
        body, name=name, grid=(F // tn, S // tm), in_specs=_conv_in_specs(tm, tn, F),
        out_specs=pl.BlockSpec((tm, tn), lambda j, i: (i, j)),
        out_shape=jax.ShapeDtypeStruct((S, F), BF16),
        compiler_params=_params(("parallel", "parallel")),
    )(up, up, up, up, cw, cb, cw, cb)


def _glu_bwd(up, cw, cb, da, name):
    S, F2 = up.shape
    F = F2 // 2
    tm, tn = _conv_tiles(S, F)

    def body(ug, hg, uv, hv, wg, bgr, wv, bvr, da_ref, dg_ref, dv_ref, sg_ref, sv_ref):
        i = pl.program_id(1)

        @pl.when(i == 0)
        def _():
            sg_ref[...] = jnp.zeros_like(sg_ref)
            sv_ref[...] = jnp.zeros_like(sv_ref)

        first = i == 0
        xg, xv = ug[...], uv[...]
        cg, xg1, xg2 = _conv_apply(xg, hg[...], wg[...], bgr[...], first)
        cv, xv1, xv2 = _conv_apply(xv, hv[...], wv[...], bvr[...], first)
        dav = da_ref[...]
        sg = _sigmoid(cg)
        dcv = dav * cg * sg
        dcg = dav * cv * sg * (1.0 + cg * (1.0 - sg))
        dg_ref[...] = dcg
        dv_ref[...] = dcv
        rows = lax.broadcasted_iota(jnp.int32, (SUBLANES, tn), 0)

        def stats(dc, x, x1, x2):
            acc = jnp.zeros((SUBLANES, tn), F32)
            for r, val in enumerate((dc * x2, dc * x1, dc * x, dc)):
                acc += jnp.where(rows == r, jnp.sum(val, axis=0, keepdims=True), 0.0)
            return acc

        sg_ref[...] += stats(dcg, xg, xg1, xg2)
        sv_ref[...] += stats(dcv, xv, xv1, xv2)

    tile = pl.BlockSpec((tm, tn), lambda j, i: (i, j))
    stat = pl.BlockSpec((SUBLANES, tn), lambda j, i: (0, j))
    return pl.pallas_call(
        body, name=name, grid=(F // tn, S // tm), in_specs=_conv_in_specs(tm, tn, F) + [tile],
        out_specs=[tile, tile, stat, stat],
        out_shape=[jax.ShapeDtypeStruct((S, F), F32), jax.ShapeDtypeStruct((S, F), F32),
                   jax.ShapeDtypeStruct((SUBLANES, F), F32), jax.ShapeDtypeStruct((SUBLANES, F), F32)],
        compiler_params=_params(("parallel", "arbitrary")),
    )(up, up, up, up, cw, cb, cw, cb, da)


def _conv_bwd(dcg, dcv, w, name):
    S, F = dcg.shape
    tm, tn = _conv_tiles(S, F)
    r8 = tm // SUBLANES
    ni = S // tm
    nf = F // tn

    def body(g_ref, gh_ref, v_ref, vh_ref, w_ref, o_ref):
        gate = pl.program_id(0) == 0
        x = jnp.where(gate, g_ref[...], v_ref[...])
        halo = jnp.where(gate, gh_ref[...], vh_ref[...])
        halo = jnp.where(pl.program_id(2) == ni - 1, 0.0, halo)
        wv = w_ref[...]
        o_ref[...] = (x * wv[2:3, :] + _shift_up(x, halo, 1) * wv[1:2, :] + _shift_up(x, halo, 2) * wv[0:1, :]).astype(BF16)

    def tile(half):
        return pl.BlockSpec((tm, tn), lambda h, j, i: (jnp.where(h == half, i, 0), jnp.where(h == half, j, 0)))

    def below(half):
        return pl.BlockSpec((SUBLANES, tn), lambda h, j, i: (
            jnp.where(h == half, jnp.minimum((i + 1) * r8, S // SUBLANES - 1), 0), jnp.where(h == half, j, 0)))

    return pl.pallas_call(
        body, name=name, grid=(2, nf, ni),
        in_specs=[tile(0), below(0), tile(1), below(1), pl.BlockSpec((3, tn), lambda h, j, i: (0, h * nf + j))],
        out_specs=pl.BlockSpec((tm, tn), lambda h, j, i: (i, h * nf + j)),
        out_shape=jax.ShapeDtypeStruct((S, 2 * F), BF16),
        compiler_params=_params(("parallel", "parallel", "parallel")),
    )(dcg, dcg, dcv, dcv, w)


def _adamw(slot_list, own_list, me, w, m, v, name, dep=None):
    L = len(slot_list)
    P, K, C = slot_list[0].shape
    tr = _tile(K, (256, 128, 64, 32, 16))
    while tr * C * 4 > (1 << 20) and tr % 32 == 0:
        tr //= 2
    nb = K // tr
    has_own = own_list is not None

    def body(me_ref, *refs):
        s_refs = refs[:L]
        o_refs = refs[L:2 * L] if has_own else None
        w_ref, m_ref, v_ref = refs[L * (1 + has_own):L * (1 + has_own) + 3]
        g_ref, d_ref, nm_ref, nv_ref = refs[-4:]
        layer = pl.program_id(0)
        g = None
        for l in range(L):
            gl = None
            for p in range(P):
                term = s_refs[l][p].astype(F32)
                if has_own:
                    term = jnp.where(me_ref[0] == p, o_refs[l][0].astype(F32), term)
                gl = term if gl is None else gl + term
            g = gl if g is None else jnp.where(layer == l, gl, g)
        nm = ADAM_B1 * m_ref[...] + (1.0 - ADAM_B1) * g
        nv = ADAM_B2 * v_ref[...] + (1.0 - ADAM_B2) * (g * g)
        m_hat = nm / (1.0 - ADAM_B1 ** ADAM_STEP)
        v_hat = nv / (1.0 - ADAM_B2 ** ADAM_STEP)
        g_ref[...] = g
        d_ref[...] = -ADAM_LR * (m_hat / (jnp.sqrt(v_hat) + ADAM_EPS) + ADAM_WD * w_ref[...])
        nm_ref[...] = nm
        nv_ref[...] = nv

    blk = pl.BlockSpec((None, tr, C), lambda li, i, me_ref: (li, i, 0))
    specs = [pl.BlockSpec((P, tr, C), functools.partial(lambda li, i, me_ref, l: (0, jnp.where(li == l, i, 0), 0), l=l))
             for l in range(L)]
    if has_own:
        specs += [pl.BlockSpec((1, tr, C), functools.partial(lambda li, i, me_ref, l: (me_ref[0], jnp.where(li == l, i, 0), 0), l=l))
                  for l in range(L)]
    return pl.pallas_call(
        body, name=name,
        grid_spec=pltpu.PrefetchScalarGridSpec(
            num_scalar_prefetch=1, grid=(L, nb), in_specs=specs + [blk, blk, blk] + [_ANY] * (dep is not None),
            out_specs=[blk] * 4),
        out_shape=[jax.ShapeDtypeStruct((L, K, C), F32)] * 4,
        compiler_params=_params(("arbitrary", "arbitrary")),
    )(me, *slot_list, *(own_list if has_own else []), w, m, v, *([] if dep is None else [dep]))


_HBM = pl.BlockSpec(memory_space=pltpu.HBM)
_SEM = pl.BlockSpec(memory_space=pltpu.SEMAPHORE)
_ANY = pl.BlockSpec(memory_space=pl.ANY)


def _peers():
    x, y, c = lax.axis_index("x"), lax.axis_index("y"), lax.axis_index("c")

    def flip(v, bit):
        return 1 - v if bit else v

    def peer(k):
        return (flip(x, (k >> 2) & 1), flip(y, (k >> 1) & 1), flip(c, k & 1))

    def peer_index(k):
        px, py, pc = peer(k)
        return 4 * px + 2 * py + pc

    return 4 * x + 2 * y + c, peer, peer_index


def _split_copy(src_refs, land_refs, send_sems, recv_sems, scatter, a, k, outgoing):
    me, peer, peer_index = _peers()
    if outgoing:
        src = src_refs[a].at[peer_index(k)] if scatter else src_refs[a]
        dst = land_refs[a].at[me]
    else:
        src = src_refs[a].at[me] if scatter else src_refs[a]
        dst = land_refs[a].at[peer_index(k)]
    pair = a * (N_DEV - 1) + k - 1
    return pltpu.make_async_remote_copy(src_ref=src, dst_ref=dst, send_sem=send_sems.at[pair],
                                        recv_sem=recv_sems.at[pair], device_id=peer(k),
                                        device_id_type=pl.DeviceIdType.MESH)


def _gather_two_level(srcs, name):
    na = len(srcs)

    def body(*refs):
        src_refs, out_refs = refs[:na], refs[na:2 * na]
        send_sems, recv_sems = refs[2 * na:]
        x, y, c = lax.axis_index("x"), lax.axis_index("y"), lax.axis_index("c")
        me, sibling = (x, y, c), (x, y, 1 - c)
        chips = [(1 - x, y), (x, 1 - y), (1 - x, 1 - y)]

        def copy(a, k, block, to, src=None):
            px, py, pc = block
            slot = out_refs[a].at[4 * px + 2 * py + pc]
            return pltpu.make_async_remote_copy(
                src_ref=slot if src is None else src, dst_ref=slot, send_sem=send_sems.at[a * (N_DEV - 1) + k],
                recv_sem=recv_sems.at[a * (N_DEV - 1) + k], device_id=to, device_id_type=pl.DeviceIdType.MESH)

        first = [copy(a, 0, me, sibling, src_refs[a]) for a in range(na)]
        first += [copy(a, 1 + j, me, (*chip, c), src_refs[a]) for j, chip in enumerate(chips) for a in range(na)]
        for cp in first:
            cp.start()
        passed = []
        for j, chip in enumerate(chips):
            for a in range(na):
                copy(a, 1 + j, (*chip, c), me).wait_recv()
                passed.append(copy(a, 4 + j, (*chip, c), sibling))
                passed[-1].start()
        for a in range(na):
            copy(a, 0, sibling, me).wait_recv()
        for j, chip in enumerate(chips):
            for a in range(na):
                copy(a, 4 + j, (*chip, 1 - c), me).wait_recv()
        for cp in first + passed:
            cp.wait_send()

    return pl.pallas_call(
        body, name=name, in_specs=[_ANY] * na, out_specs=[_ANY] * na,
        out_shape=[jax.ShapeDtypeStruct((N_DEV,) + s.shape, s.dtype) for s in srcs],
        scratch_shapes=[pltpu.SemaphoreType.DMA((na * (N_DEV - 1),)), pltpu.SemaphoreType.DMA((na * (N_DEV - 1),))],
    )(*srcs)


def _exchange_start(srcs, scatter, after, name):
    na = len(srcs)
    land_shapes = [s.shape if scatter else (N_DEV,) + s.shape for s in srcs]
    has_after = after is not None

    def body(*refs):
        src_refs, land_refs = refs[:na], refs[na:2 * na]
        send_sems, recv_sems = refs[2 * na + has_after], refs[2 * na + has_after + 1]
        token = refs[-1]
        for k in range(1, N_DEV):
            for a in range(na):
                _split_copy(src_refs, land_refs, send_sems, recv_sems, scatter, a, k, True).start()
        token[...] = jnp.zeros_like(token)

    sems = pltpu.SemaphoreType.DMA((na * (N_DEV - 1),))
    out_shape = ([sems, sems] + [pltpu.HBM(s.shape, s.dtype) for s in srcs]
                 + [pltpu.HBM(shp, s.dtype) for shp, s in zip(land_shapes, srcs)] + [jax.ShapeDtypeStruct((SUBLANES, 128), F32)])
    args = [pltpu.with_memory_space_constraint(s, pltpu.HBM) for s in srcs]
    args += [pltpu.with_memory_space_constraint(lax.empty(shp, s.dtype), pltpu.HBM) for shp, s in zip(land_shapes, srcs)]
    if has_after:
        args.append(after)
    res = pl.pallas_call(
        body, name=name, in_specs=[_HBM] * (2 * na) + [_ANY] * has_after,
        out_specs=[_SEM, _SEM] + [_HBM] * (2 * na) + [pl.BlockSpec(memory_space=pltpu.VMEM)], out_shape=out_shape,
        input_output_aliases={i: 2 + i for i in range(2 * na)},
        compiler_params=pltpu.CompilerParams(has_side_effects=pltpu.SideEffectType.DATAFLOW_SIDE_EFFECTING),
    )(*args)
    handle = dict(send=res[0], recv=res[1], srcs=list(res[2:2 + na]), lands=list(res[2 + na:2 + 2 * na]), scatter=scatter)
    return handle, res[-1]


def _exchange_wait(handle, after, name):
    srcs, lands, scatter = handle["srcs"], handle["lands"], handle["scatter"]
    na = len(srcs)

    def body(*refs):
        src_refs, land_refs = refs[:na], refs[na:2 * na]
        send_sems, recv_sems = refs[2 * na], refs[2 * na + 1]
        for k in range(1, N_DEV):
            for a in range(na):
                _split_copy(src_refs, land_refs, send_sems, recv_sems, scatter, a, k, True).wait_send()
                _split_copy(src_refs, land_refs, send_sems, recv_sems, scatter, a, k, False).wait_recv()

    res = pl.pallas_call(
        body, name=name, in_specs=[_HBM] * (2 * na) + [_SEM, _SEM, _ANY], out_specs=[_HBM] * (2 * na),
        out_shape=[pltpu.HBM(s.shape, s.dtype) for s in srcs] + [pltpu.HBM(s.shape, s.dtype) for s in lands],
        input_output_aliases={i: i for i in range(2 * na)},
        compiler_params=pltpu.CompilerParams(has_side_effects=pltpu.SideEffectType.DATAFLOW_SIDE_EFFECTING),
    )(*srcs, *lands, handle["send"], handle["recv"], after)
    return list(res[:na]), list(res[na:])


def _layout(D):
    aq, akv = SWA_Q_HEADS * SWA_HEAD_DIM, SWA_KV_HEADS * SWA_HEAD_DIM
    w = SGU_GROUPS * SGU_DIM
    return aq, akv, w


class _Seg:
    def __init__(self, D, rq, rkv):
        aq, akv, w = _layout(D)
        src = {}
        o = 0
        for nm, wd in (("qa", aq), ("ka", akv), ("va", akv), ("cq", rq), ("ckv", rkv), ("kr", MLA_ROPE), ("hu", w), ("hv", w),
                       ("g", 3 * D)):
            src[nm] = (o, wd)
            o += wd
        self.n_in = o
        self.order = ("g", "qa", "hu", "hv", "cq", "ckv", "ka", "va", "kr")
        self.src = src
        self.off = {}
        o = 0
        for nm in self.order:
            self.off[nm] = o
            o += src[nm][1]
        self.width = {nm: src[nm][1] for nm in self.order}
        self.n_pad = -(-o // 1536) * 1536 if o > 1536 else -(-o // 512) * 512
        self.used = o

    def from_shards(self, shards):
        c = shards.shape[2]
        parts = []
        for nm in self.order:
            s0, wd = self.src[nm]
            for d in range(N_DEV):
                lo, hi = max(s0, c * d), min(s0 + wd, c * (d + 1))
                if lo < hi:
                    parts.append(shards[d][:, lo - c * d:hi - c * d])
        parts.append(jnp.zeros((shards.shape[1], self.n_pad - self.used), shards.dtype))
        return jnp.concatenate(parts, axis=1)

    def to_shards(self, w):
        c = self.n_in // N_DEV
        names = sorted(self.order, key=lambda nm: self.src[nm][0])
        shards = []
        for d in range(N_DEV):
            parts = []
            for nm in names:
                s0, wd = self.src[nm]
                lo, hi = max(s0, c * d), min(s0 + wd, c * (d + 1))
                if lo < hi:
                    parts.append(w[:, self.off[nm] + lo - s0:self.off[nm] + hi - s0])
            shards.append(jnp.concatenate(parts, axis=1))
        return jnp.stack(shards)


def _uq_permute(w):
    R = w.shape[0]
    H = MLA_HEADS
    w3 = w.reshape(R, H, MLA_NOPE + MLA_ROPE)
    return jnp.concatenate([w3[:, :, :MLA_NOPE].reshape(R, H * MLA_NOPE), w3[:, :, MLA_NOPE:].reshape(R, H * MLA_ROPE)], axis=1)


def _uq_unpermute(w):
    R = w.shape[0]
    H = MLA_HEADS
    n = w[:, :H * MLA_NOPE].reshape(R, H, MLA_NOPE)
    r = w[:, H * MLA_NOPE:].reshape(R, H, MLA_ROPE)
    return jnp.concatenate([n, r], axis=2).reshape(R, H * (MLA_NOPE + MLA_ROPE))


def _ukv_permute(w):
    R = w.shape[0]
    w3 = w.reshape(R, MLA_HEADS, MLA_NOPE + MLA_V)
    return jnp.concatenate([w3[:, :, :MLA_NOPE].reshape(R, -1), w3[:, :, MLA_NOPE:].reshape(R, -1)], axis=1)


def _ukv_unpermute(w):
    R = w.shape[0]
    H = MLA_HEADS
    k = w[:, :H * MLA_NOPE].reshape(R, H, MLA_NOPE)
    v = w[:, H * MLA_NOPE:].reshape(R, H, MLA_V)
    return jnp.concatenate([k, v], axis=2).reshape(R, H * (MLA_NOPE + MLA_V))


GROUPS = {"a": ("w_in",), "b": ("w_uq", "w_ukv", "w_proj_a", "w_proj_b", "w_proj_c", "w_o", "b_gate"),
          "c": ("w_up", "w_down", "conv_w")}


def _layer_fwd(l, x, x16, fetch, P, cs, sn, seg, alpha):
    S, D = x.shape
    H, half = MLA_HEADS, MLA_ROPE // 2
    off = seg.off
    nm = lambda s: f"l{l}_{s}"
    sv = {"x16": x16}
    W = {"w_in": seg.from_shards(fetch(l, "a", x16)["w_in"])}
    h = _mm(x16, W["w_in"], "nn", [(F32, "n")], nm("h"))[0]
    sv["h"] = h
    ya, lse_a = _swa_fwd(h, off["qa"], off["ka"], off["va"], P["sinks"], nm("swa_fwd"))
    cqn, rq = _rms_fwd(h, off["cq"], seg.width["cq"], P["q_norm_g"], nm("rmsq_fwd"))
    ckvn, rkv = _rms_fwd(h, off["ckv"], seg.width["ckv"], P["kv_norm_g"], nm("rmskv_fwd"))
    W.update(fetch(l, "b", cqn))
    W["w_uq"] = _uq_permute(W["w_uq"])
    W["w_ukv"] = _ukv_permute(W["w_ukv"])
    qf = _mm(cqn, W["w_uq"], "nn", [(F32, "n")], nm("uq"))[0]
    kvf = _mm(ckvn, W["w_ukv"], "nn", [(BF16, "n")], nm("ukv"))[0]
    qr = _rope(qf, H * MLA_NOPE, H * MLA_ROPE, cs, sn, False, nm("ropeq_fwd"))
    kr = _rope(h, off["kr"], LANES, cs, sn, False, nm("ropek_fwd"))
    yb, yb16, lse_b = _mla_fwd(qf, qr, kvf, kr, nm("mla_fwd"))
    w16 = jnp.where(jnp.tril(jnp.ones((SGU_CHUNK, SGU_CHUNK), bool))[None], P["sgu_w"], 0.0).astype(BF16)
    bt = P["sgu_b"].T
    yc = _sgu_fwd(h, off["hu"], off["hv"], P["sgu_ln_g"], P["sgu_ln_b"], w16, bt, nm("sgu_fwd"))
    merged, z = _merge_fwd([ya, yb16, yc], [W["w_proj_a"], W["w_proj_b"], W["w_proj_c"]], h, W["b_gate"], nm("merge_fwd"))
    x1, x1_16, xh1, rs1 = _mm_ln(merged, W["w_o"], x, P["ln1_g"], P["ln1_b"], alpha, nm("wo_ln1"))
    W.update(fetch(l, "c", x1_16))
    up = _mm(x1_16, W["w_up"], "nn", [(F32, "n")], nm("up"))[0]
    a = _glu_fwd(up, W["conv_w"], P["conv_b"], nm("glu_fwd"))
    x2, x2_16, xh2, rs2 = _mm_ln(a, W["w_down"], x1, P["ln2_g"], P["ln2_b"], alpha, nm("down_ln2"))
    sv.update(W=W, ya=ya, lse_a=lse_a, cqn=cqn, rq=rq, ckvn=ckvn, rkv=rkv, qf=qf, qr=qr, kvf=kvf, kr=kr, lse_b=lse_b, yb=yb, yb16=yb16,
              w16=w16, bt=bt, yc=yc, merged=merged, z=z, x1_16=x1_16, xh1=xh1, rs1=rs1, up=up, a=a, xh2=xh2, rs2=rs2)
    return x2, x2_16, sv


def _dw_chunks(k, a, dy, name, post=None, chunker=None):
    n = dy.shape[1]
    if k not in ROW_SHARDED and post is None and chunker is None and (n // N_DEV) % 128 == 0:
        return _mm(a, dy, "tn", [(BF16, "n")], name, chunk=n // N_DEV)[0]
    g = _mm(a, dy, "tn", [(BF16, "n")], name)[0]
    if chunker is not None:
        return chunker(g)
    return _to_chunks(k, g if post is None else post(g))


def _after(arr, token):
    return arr if token is None else arr + token[0:1, 0:1].astype(arr.dtype)


def _layer_bwd(l, dx2, sv, P, cs, sn, seg, alpha, emit):
    S, D = dx2.shape
    H, half = MLA_HEADS, MLA_ROPE // 2
    off = seg.off
    h, W = sv["h"], sv["W"]
    nm = lambda s: f"l{l}_{s}"
    g = {}
    dr2, dr2_16, g["ln2_g"], g["ln2_b"] = _ln_bwd(dx2, sv["xh2"], sv["rs2"], P["ln2_g"], nm("ln2_bwd"))
    g["w_down"] = _dw_chunks("w_down", sv["a"], dr2_16, nm("dw_down"))
    da = _mm(dr2_16, W["w_down"], "nt", [(F32, "n")], nm("da"))[0]
    dcg, dcv, st_g, st_v = _glu_bwd(sv["up"], W["conv_w"], P["conv_b"], da, nm("glu_bwd"))
    F = dcg.shape[1]
    g["conv_w"] = _to_chunks("conv_w", jnp.concatenate([st_g[0:3], st_v[0:3]], axis=1))
    g["conv_b"] = jnp.concatenate([st_g[3:4], st_v[3:4]], axis=1)
    dup = _conv_bwd(dcg, dcv, W["conv_w"], nm("conv_bwd"))
    g["w_up"] = _dw_chunks("w_up", sv["x1_16"], dup, nm("dw_up"))
    token = emit(l, "c", {k: g.pop(k) for k in GROUPS["c"]})
    dx1 = _mm_axpy(dup, W["w_up"], "nt", dr2, alpha, nm("dx1"), dep=token)
    dr1, dr1_16, g["ln1_g"], g["ln1_b"] = _ln_bwd(dx1, sv["xh1"], sv["rs1"], P["ln1_g"], nm("ln1_bwd"))
    g["w_o"] = _dw_chunks("w_o", sv["merged"], dr1_16, nm("dw_o"))
    dmerged = _mm(dr1_16, W["w_o"], "nt", [(F32, "n")], nm("dmerged"))[0]
    dz, dlog, dbg = _merge_bwd(dmerged, sv["z"], h, W["b_gate"], nm("merge_bwd"))
    g["b_gate"] = _to_chunks("b_gate", dbg[0:3])
    g["w_proj_a"] = _dw_chunks("w_proj_a", sv["ya"], dz[0], nm("dw_pa"))
    g["w_proj_b"] = _dw_chunks("w_proj_b", sv["yb16"], dz[1], nm("dw_pb"))
    g["w_proj_c"] = _dw_chunks("w_proj_c", sv["yc"], dz[2], nm("dw_pc"))
    dya = _mm(dz[0], W["w_proj_a"], "nt", [(F32, "n")], nm("dya"))[0]
    dyb = _mm(dz[1], W["w_proj_b"], "nt", [(F32, "n")], nm("dyb"))[0]
    dyc = _mm(dz[2], W["w_proj_c"], "nt", [(F32, "n")], nm("dyc"))[0]
    dhu, dhv, g["sgu_w"], db_s, g["sgu_ln_g"], g["sgu_ln_b"] = _sgu_bwd(
        h, off["hu"], off["hv"], P["sgu_ln_g"], P["sgu_ln_b"], sv["w16"], sv["bt"], dyc, nm("sgu_bwd"))
    g["sgu_b"] = db_s[:, :SGU_GROUPS].T
    dqa, dka, dva, dsk = _swa_bwd(h, off["qa"], off["ka"], off["va"], P["sinks"], dya, sv["lse_a"], nm("swa_bwd"))
    g["sinks"] = dsk[0, :SWA_Q_HEADS]
    delta = _rowdot(dyb, sv["yb"], nm("mla_delta"))
    dqn, dqr, dkn, dvv, dkr = _mla_bwd(sv["qf"], sv["qr"], sv["kvf"], sv["kr"], dyb, sv["lse_b"], delta, nm("mla_bwd"))
    dqf = jnp.concatenate([dqn.astype(BF16), _rope(dqr, 0, H * MLA_ROPE, cs, sn, True, nm("ropeq_bwd"))], axis=1)
    dkvf = jnp.concatenate([dkn, dvv], axis=1).astype(BF16)
    dkr16 = _rope(dkr, 0, LANES, cs, sn, True, nm("ropek_bwd"))
    g["w_uq"] = _dw_chunks("w_uq", sv["cqn"], dqf, nm("dw_uq"), _uq_unpermute)
    g["w_ukv"] = _dw_chunks("w_ukv", sv["ckvn"], dkvf, nm("dw_ukv"), _ukv_unpermute)
    token = emit(l, "b", {k: g.pop(k) for k in GROUPS["b"]})
    dcqn = _mm(dqf, W["w_uq"], "nt", [(F32, "n")], nm("dcqn"), dep=token)[0]
    dckvn = _mm(dkvf, W["w_ukv"], "nt", [(F32, "n")], nm("dckvn"), dep=token)[0]
    dcq, g["q_norm_g"] = _rms_bwd(dcqn, h, off["cq"], seg.width["cq"], sv["rq"], P["q_norm_g"], nm("rmsq_bwd"))
    dckv, g["kv_norm_g"] = _rms_bwd(dckvn, h, off["ckv"], seg.width["ckv"], sv["rkv"], P["kv_norm_g"], nm("rmskv_bwd"))
    assert seg.order[-1] == "kr" and seg.n_pad - seg.used >= LANES - MLA_ROPE
    parts = {"g": jnp.concatenate([dlog[0], dlog[1], dlog[2]], axis=1), "qa": dqa, "hu": dhu, "hv": dhv, "cq": dcq, "ckv": dckv,
             "ka": dka.astype(BF16), "va": dva.astype(BF16), "kr": dkr16}
    dh = jnp.concatenate([parts[k] for k in seg.order] + [jnp.zeros((S, seg.n_pad - seg.used - (LANES - MLA_ROPE)), BF16)],
                         axis=1)
    token = emit(l, "a", {"w_in": _dw_chunks("w_in", sv["x16"], dh, nm("dw_in"), chunker=seg.to_shards)})
    dx = _mm_axpy(dh, W["w_in"], "nt", dr1, alpha, nm("dx"), dep=token)
    return dx, g


BIG = ("w_in", "w_uq", "w_ukv", "w_proj_a", "w_proj_b", "w_proj_c", "w_o", "w_up", "w_down")
ROW_SHARDED = ("w_proj_b", "w_o", "w_down")
SHARDED_F32 = ("b_gate", "conv_w")
REPLICATED = ("sinks", "q_norm_g", "kv_norm_g", "sgu_ln_g", "sgu_ln_b", "sgu_w", "sgu_b", "ln1_g", "ln1_b", "conv_b", "ln2_g",
              "ln2_b")
WEIGHTS = ("w_in", "b_gate", "sinks", "q_norm_g", "kv_norm_g", "w_uq", "w_ukv", "sgu_ln_g", "sgu_ln_b", "sgu_w", "sgu_b",
           "w_proj_a", "w_proj_b", "w_proj_c", "w_o", "ln1_g", "ln1_b", "w_up", "conv_w", "conv_b", "w_down", "ln2_g", "ln2_b")


def _step_local(x, positions, target, small, rq, rkv, fetch, emit, token=None):
    S, D = x.shape
    L = small["sinks"].shape[0]
    alpha = (2 * L) ** 0.25
    seg = _Seg(D, rq, rkv)
    inv_freq = ROPE_THETA ** (-jnp.arange(0, MLA_ROPE, 2, dtype=F32) / MLA_ROPE)
    ang = positions.astype(F32)[:, None] * inv_freq
    reps = LANES // (MLA_ROPE // 2)
    cs, sn = jnp.tile(jnp.cos(ang), (1, reps)), jnp.tile(jnp.sin(ang), (1, reps))
    rows = ("q_norm_g", "kv_norm_g", "sgu_ln_g", "sgu_ln_b", "ln1_g", "ln1_b", "conv_b", "ln2_g", "ln2_b")
    layers = [{k: small[k][l].reshape(1, -1) if k in rows else small[k][l] for k in small} for l in range(L)]
    saved = []
    x16 = _after(x, token).astype(BF16)
    for l in range(L):
        x, x16, sv = _layer_fwd(l, x, x16, fetch, layers[l], cs, sn, seg, alpha)
        saved.append(sv)
    loss, dx = _loss(x, target, "loss")
    grads = [None] * L
    for l in reversed(range(L)):
        dx, grads[l] = _layer_bwd(l, dx, saved[l], layers[l], cs, sn, seg, alpha, emit)
    out = {k: jnp.stack([grads[l][k].reshape(small[k].shape[1:]) for l in range(L)]) for k in small}
    return loss, dx, out


def _unshard(k, gathered):
    n, r, c = gathered.shape
    if k in ROW_SHARDED:
        return gathered.reshape(n * r, c)
    return gathered.transpose(1, 0, 2).reshape(r, n * c)


def _to_chunks(k, gfull):
    r, c = gfull.shape
    if k in ROW_SHARDED:
        return gfull.reshape(N_DEV, r // N_DEV, c)
    return gfull.reshape(r, N_DEV, c // N_DEV).transpose(1, 0, 2)


def _pack(arrs):
    P = arrs[0].shape[0]
    flat = jnp.concatenate([a.reshape(P, -1) for a in arrs], axis=1)
    sizes = [(a.size // P, a.size // P) for a in arrs]
    flat = jnp.pad(flat, ((0, 0), (0, -flat.shape[1] % (SUBLANES * 128))))
    return flat.reshape(P, -1, 128), sizes


def _unpack(packed, sizes, shapes):
    flat = packed.reshape(-1)
    out, o = [], 0
    for (n, npad), shp in zip(sizes, shapes):
        out.append(flat[o:o + n].reshape(shp))
        o += npad
    return out


def kernel(x, positions, w_in, b_gate, sinks, q_norm_g, kv_norm_g, w_uq, w_ukv, sgu_ln_g, sgu_ln_b, sgu_w, sgu_b, w_proj_a, w_proj_b, w_proj_c, w_o, ln1_g, ln1_b, w_up, conv_w, conv_b, w_down, ln2_g, ln2_b, loss_target, m_w_in, m_b_gate, m_sinks, m_q_norm_g, m_kv_norm_g, m_w_uq, m_w_ukv, m_sgu_ln_g, m_sgu_ln_b, m_sgu_w, m_sgu_b, m_w_proj_a, m_w_proj_b, m_w_proj_c, m_w_o, m_ln1_g, m_ln1_b, m_w_up, m_conv_w, m_conv_b, m_w_down, m_ln2_g, m_ln2_b, v_w_in, v_b_gate, v_sinks, v_q_norm_g, v_kv_norm_g, v_w_uq, v_w_ukv, v_sgu_ln_g, v_sgu_ln_b, v_sgu_w, v_sgu_b, v_w_proj_a, v_w_proj_b, v_w_proj_c, v_w_o, v_ln1_g, v_ln1_b, v_w_up, v_conv_w, v_conv_b, v_w_down, v_ln2_g, v_ln2_b):
    given = dict(locals())
    w = {k: given[k] for k in WEIGHTS}
    mom = {k: given["m_" + k] for k in WEIGHTS}
    var = {k: given["v_" + k] for k in WEIGHTS}

    L = w_in.shape[0]
    order = [(l, grp) for l in range(L) for grp in ("a", "b", "c")]

    first = [w[k][0].astype(BF16) for k in GROUPS["a"]]
    first_lands = _gather_two_level(first, "gather_first")
    gathers, token = {}, first_lands[0]
    for l, grp in order[1:]:
        srcs = [w[k][l].astype(BF16) if k in BIG else w[k][l] for k in GROUPS[grp]]
        gathers[l, grp], token = _exchange_start(srcs, False, token, f"gather_start_l{l}{grp}")

    me = 4 * lax.axis_index("x") + 2 * lax.axis_index("y") + lax.axis_index("c")
    mine = (jnp.arange(N_DEV) == me)[:, None, None]

    def fetch(l, grp, after):
        if (l, grp) == order[0]:
            srcs, lands = first, first_lands
        else:
            srcs, lands = _exchange_wait(gathers[l, grp], after, f"gather_wait_l{l}{grp}")
        full = {k: jnp.where(mine, srcs[i][None], lands[i]) for i, k in enumerate(GROUPS[grp])}
        return {k: v if k == "w_in" else _unshard(k, v) for k, v in full.items()}

    scatters = {}

    def emit(l, grp, chunks):
        scatters[l, grp], tok = _exchange_start([chunks[k] for k in GROUPS[grp]], True, None, f"scatter_start_l{l}{grp}")
        return tok

    small = {k: w[k] for k in REPLICATED}
    loss, grad_x, g = _step_local(x[0], positions[0], loss_target[0], small, w_uq.shape[1], w_ukv.shape[1], fetch, emit, token)
    loss = lax.psum(loss[0, 0], AXES)

    packed, sizes = _pack([g[k][None] for k in REPLICATED])
    small_grads, after = _exchange_start([packed[0]], False, grad_x, "gather_small_grads_start")

    me1 = me.astype(jnp.int32).reshape(1)
    res = {}
    for grp in ("c", "b", "a"):
        slots, own = {}, {}
        for l in reversed(range(L)):
            srcs, lands = _exchange_wait(scatters[l, grp], after, f"scatter_wait_l{l}{grp}")
            for i, k in enumerate(GROUPS[grp]):
                slots[k, l], own[k, l] = lands[i], srcs[i]
        for k in GROUPS[grp]:
            res[k] = _adamw([slots[k, l] for l in range(L)], [own[k, l] for l in range(L)], me1, w[k], mom[k], var[k],
                            "adamw_" + k, dep=after)
            after = res[k][1]

    srcs, lands = _exchange_wait(small_grads, after, "gather_small_grads_wait")
    parts = jnp.where(mine, srcs[0][None], lands[0])
    shapes = [w[k].shape for k in REPLICATED]
    pw, _ = _pack([w[k][None] for k in REPLICATED])
    pm, _ = _pack([mom[k][None] for k in REPLICATED])
    pv, _ = _pack([var[k][None] for k in REPLICATED])
    outs = _adamw([parts], None, me1, pw, pm, pv, "adamw_small")
    unpacked = [_unpack(o, sizes, shapes) for o in outs]
    for i, k in enumerate(REPLICATED):
        res[k] = [unpacked[j][i] for j in range(4)]

    return (loss, grad_x[None], *[res[k][0] for k in WEIGHTS], *[res[k][1] for k in WEIGHTS],
            *[res[k][2] for k in WEIGHTS], *[res[k][3] for k in WEIGHTS])
```

```python
import functools
import math

import jax
import jax.numpy as jnp
from jax import lax
from jax.experimental import pallas as pl
from jax.experimental.pallas import tpu as pltpu

F32 = jnp.float32
BF16 = jnp.bfloat16

SWA_Q_HEADS = 16
SWA_KV_HEADS = 2
SWA_HEAD_DIM = 64
SWA_BLOCK = 128
MLA_HEADS = 16
MLA_NOPE = 128
MLA_ROPE = 64
MLA_V = 128
SGU_GROUPS = 8
SGU_DIM = 128
SGU_CHUNK = 128
ROPE_THETA = 10000.0
EPS = 1e-5
MASK = -1e30
ADAM_LR = 0.001
ADAM_B1 = 0.9
ADAM_B2 = 0.999
ADAM_EPS = 1e-08
ADAM_WD = 0.01
ADAM_STEP = 10

N_DEV = 8
AXES = ("x", "y", "c")
VMEM_LIMIT = 56 * 1024 * 1024
MLA_TILE = 512
MLA_FWD_TILE = 1024
ROW_TILE = 512
CONV_BWD_ROWS = 2048
MAX_TK = 2816
SUBLANES = 8


def _tile(n, prefs):
    for p in prefs:
        if n % p == 0:
            return p
    return n


def _params(sem):
    return pltpu.CompilerParams(dimension_semantics=sem, vmem_limit_bytes=VMEM_LIMIT)


def _cols(tm, width, off):
    assert off % width == 0, (off, width)
    blk = off // width
    return pl.BlockSpec((tm, width), lambda i, *_: (i, blk))


def _full(shape):
    nd = len(shape)
    return pl.BlockSpec(shape, lambda *_: (0,) * nd)


def _sigmoid(v):
    return 1.0 / (1.0 + jnp.exp(-v))


def _gelu(v):
    return 0.5 * v * (1.0 + lax.erf(v * (2.0 ** -0.5)))


def _gelu_grad(v):
    return 0.5 * (1.0 + lax.erf(v * (2.0 ** -0.5))) + v * jnp.exp(-0.5 * v * v) * (1.0 / math.sqrt(2.0 * math.pi))


_DIMS = {"nn": (((1,), (0,)), ((), ())), "nt": (((1,), (1,)), ((), ())), "tn": (((0,), (0,)), ((), ()))}


def _mm(a, b, mode, outs, name, *, extras=(), epilogue=None, full_n=False, dep=None, chunk=None):
    if mode == "nn":
        (M, K), (K2, N) = a.shape, b.shape
    elif mode == "nt":
        (M, K), (N, K2) = a.shape, b.shape
    else:
        (K, M), (K2, N) = a.shape, b.shape
    assert K == K2, (a.shape, b.shape, mode)
    tm = _tile(M, (1024, 512, 256, 128))
    tn = N if full_n else _tile(N, (1024, 768, 512, 384, 256, 128))
    if full_n:
        tm = _tile(M, (512, 256, 128))
    if chunk is not None:
        tn = chunk if chunk <= 1536 else _tile(chunk, (1024, 768, 512, 384, 256, 128))
        assert N % chunk == 0 and chunk % tn == 0 and tn % 128 == 0, (N, chunk, tn)
    max_tk = MAX_TK // 2 if full_n else MAX_TK
    tk = max(d for d in range(128, min(K, max_tk) + 1, 128) if K % d == 0) if K % 128 == 0 else K
    nk = K // tk
    if mode == "nn":
        a_spec = pl.BlockSpec((tm, tk), lambda i, j, k: (i, k))
        b_spec = pl.BlockSpec((tk, tn), lambda i, j, k: (k, j))
    elif mode == "nt":
        a_spec = pl.BlockSpec((tm, tk), lambda i, j, k: (i, k))
        b_spec = pl.BlockSpec((tn, tk), lambda i, j, k: (j, k))
    else:
        a_spec = pl.BlockSpec((tk, tm), lambda i, j, k: (k, i))
        b_spec = pl.BlockSpec((tk, tn), lambda i, j, k: (k, j))
    in_specs = [a_spec, b_spec]
    for arr, kind in extras:
        if kind == "tile":
            in_specs.append(pl.BlockSpec((tm, tn), lambda i, j, k: (i, j)))
        else:
            in_specs.append(pl.BlockSpec((1, tn), lambda i, j, k: (0, j)))
    out_specs, out_shape = [], []
    for dt, kind in outs:
        if kind == "n" and chunk is not None:
            per = chunk // tn
            out_specs.append(pl.BlockSpec((None, tm, tn), lambda i, j, k: (lax.div(j, per), i, lax.rem(j, per))))
            out_shape.append(jax.ShapeDtypeStruct((N // chunk, M, chunk), dt))
        elif kind == "n":
            out_specs.append(pl.BlockSpec((tm, tn), lambda i, j, k: (i, j)))
            out_shape.append(jax.ShapeDtypeStruct((M, N), dt))
        else:
            assert tn == N
            out_specs.append(pl.BlockSpec((tm, 1), lambda i, j, k: (i, 0)))
            out_shape.append(jax.ShapeDtypeStruct((M, 1), dt))
    ne, no = len(extras), len(outs)
    deps = []
    if dep is not None:
        in_specs.append(_full(dep.shape))
        deps = [dep]
    dims = _DIMS[mode]
    if epilogue is None:
        epilogue = lambda acc: (acc,) * no

    def body(*refs):
        a_ref, b_ref = refs[0], refs[1]
        ex = refs[2:2 + ne]
        out = refs[len(refs) - 1 - no:len(refs) - 1]
        acc = refs[-1]
        k = pl.program_id(2)
        part = lax.dot_general(a_ref[...].astype(BF16), b_ref[...].astype(BF16), dims, preferred_element_type=F32)

        def finish(total):
            res = epilogue(total, *[e[...] for e in ex])
            for o, r in zip(out, res):
                o[...] = r.astype(o.dtype)

        if nk == 1:
            finish(part)
            return

        @pl.when(k == 0)
        def _():
            acc[...] = part

        @pl.when((k > 0) & (k < nk - 1))
        def _():
            acc[...] += part

        @pl.when(k == nk - 1)
        def _():
            finish(acc[...] + part)

    res = pl.pallas_call(
        body, name=name, grid=(M // tm, N // tn, nk), in_specs=in_specs, out_specs=out_specs, out_shape=out_shape,
        scratch_shapes=[pltpu.VMEM((tm, tn), F32)],
        compiler_params=_params(("parallel", "parallel", "arbitrary")),
    )(a, b, *[e[0] for e in extras], *deps)
    return res


def _ln_epilogue(alpha):
    def epi(acc, x, g, b):
        r = alpha * x + acc
        mu = jnp.mean(r, axis=-1, keepdims=True)
        d = r - mu
        var = jnp.mean(d * d, axis=-1, keepdims=True)
        rstd = lax.rsqrt(var + EPS)
        xhat = d * rstd
        y = xhat * g + b
        return y, y, xhat, rstd
    return epi


def _mm_ln(a, w, x, g, b, alpha, name):
    return _mm(a, w, "nn", [(F32, "n"), (BF16, "n"), (F32, "n"), (F32, "1")], name,
               extras=[(x, "tile"), (g, "row"), (b, "row")], epilogue=_ln_epilogue(alpha), full_n=True)


def _mm_axpy(a, w, mode, r, alpha, name, dep=None):
    return _mm(a, w, mode, [(F32, "n")], name, extras=[(r, "tile")],
               epilogue=lambda acc, rv: (acc + alpha * rv,), dep=dep)[0]


def _ln_bwd(dy, xhat, rstd, g, name):
    S, D = dy.shape
    tm = _tile(S, (256, 128))

    def body(dy_ref, xh_ref, rs_ref, g_ref, dr_ref, dr16_ref, dg_ref, db_ref):
        @pl.when(pl.program_id(0) == 0)
        def _():
            dg_ref[...] = jnp.zeros_like(dg_ref)
            db_ref[...] = jnp.zeros_like(db_ref)

        dyv, xh = dy_ref[...], xh_ref[...]
        dxh = dyv * g_ref[...]
        m1 = jnp.mean(dxh, axis=-1, keepdims=True)
        m2 = jnp.mean(dxh * xh, axis=-1, keepdims=True)
        dr = rs_ref[...] * (dxh - m1 - xh * m2)
        dr_ref[...] = dr
        dr16_ref[...] = dr.astype(BF16)
        dg_ref[...] += jnp.sum(dyv * xh, axis=0, keepdims=True)
        db_ref[...] += jnp.sum(dyv, axis=0, keepdims=True)

    row = pl.BlockSpec((tm, D), lambda i: (i, 0))
    return pl.pallas_call(
        body, name=name, grid=(S // tm,),
        in_specs=[row, row, pl.BlockSpec((tm, 1), lambda i: (i, 0)), _full((1, D))],
        out_specs=[row, row, _full((1, D)), _full((1, D))],
        out_shape=[jax.ShapeDtypeStruct((S, D), F32), jax.ShapeDtypeStruct((S, D), BF16),
                   jax.ShapeDtypeStruct((1, D), F32), jax.ShapeDtypeStruct((1, D), F32)],
        compiler_params=_params(("arbitrary",)),
    )(dy, xhat, rstd, g)


def _rms_fwd(h, off, width, g, name):
    S = h.shape[0]
    tm = _tile(S, (ROW_TILE, 256, 128))

    def body(c_ref, g_ref, y_ref, r_ref):
        c = c_ref[...]
        r = lax.rsqrt(jnp.mean(c * c, axis=-1, keepdims=True) + EPS)
        y_ref[...] = (c * r * g_ref[...]).astype(BF16)
        r_ref[...] = r

    return pl.pallas_call(
        body, name=name, grid=(S // tm,),
        in_specs=[_cols(tm, width, off), _full((1, width))],
        out_specs=[pl.BlockSpec((tm, width), lambda i: (i, 0)), pl.BlockSpec((tm, 1), lambda i: (i, 0))],
        out_shape=[jax.ShapeDtypeStruct((S, width), BF16), jax.ShapeDtypeStruct((S, 1), F32)],
        compiler_params=_params(("parallel",)),
    )(h, g)


def _rms_bwd(dy, h, off, width, rstd, g, name):
    S = h.shape[0]
    tm = _tile(S, (ROW_TILE, 256, 128))

    def body(dy_ref, c_ref, r_ref, g_ref, dc_ref, dg_ref):
        @pl.when(pl.program_id(0) == 0)
        def _():
            dg_ref[...] = jnp.zeros_like(dg_ref)

        dyv, c, r = dy_ref[...], c_ref[...], r_ref[...]
        dyg = dyv * g_ref[...]
        m = jnp.mean(dyg * c, axis=-1, keepdims=True)
        dc_ref[...] = (r * dyg - c * (r * r * r) * m).astype(BF16)
        dg_ref[...] += jnp.sum(dyv * c * r, axis=0, keepdims=True)

    return pl.pallas_call(
        body, name=name, grid=(S // tm,),
        in_specs=[pl.BlockSpec((tm, width), lambda i: (i, 0)), _cols(tm, width, off),
                  pl.BlockSpec((tm, 1), lambda i: (i, 0)), _full((1, width))],
        out_specs=[pl.BlockSpec((tm, width), lambda i: (i, 0)), _full((1, width))],
        out_shape=[jax.ShapeDtypeStruct((S, width), BF16), jax.ShapeDtypeStruct((1, width), F32)],
        compiler_params=_params(("arbitrary",)),
    )(dy, h, rstd, g)


def _loss(y, target, name):
    S, D = y.shape
    tm = _tile(S, (256, 128))

    def body(y_ref, t_ref, l_ref, dy_ref):
        @pl.when(pl.program_id(0) == 0)
        def _():
            l_ref[...] = jnp.zeros_like(l_ref)

        err = y_ref[...] - t_ref[...]
        dy_ref[...] = err * (1.0 / D)
        per_tok = jnp.mean(err * err, axis=-1, keepdims=True)
        l_ref[...] += 0.5 * jnp.sum(per_tok, axis=0, keepdims=True)

    row = pl.BlockSpec((tm, D), lambda i: (i, 0))
    return pl.pallas_call(
        body, name=name, grid=(S // tm,), in_specs=[row, row], out_specs=[_full((1, 1)), row],
        out_shape=[jax.ShapeDtypeStruct((1, 1), F32), jax.ShapeDtypeStruct((S, D), F32)],
        compiler_params=_params(("arbitrary",)),
    )(y, target)


LANES = 128


def _rope(x, off, width, cs2, sn2, bwd, name):
    S = cs2.shape[0]
    tm = _tile(S, (ROW_TILE, 256, 128))
    stacked = x.ndim == 3
    half = MLA_ROPE // 2
    assert width % LANES == 0 and MLA_ROPE * 2 == LANES

    def rot(v):
        lane = lax.broadcasted_iota(jnp.int32, v.shape, 1)
        return jnp.where((lane & (MLA_ROPE - 1)) < half, -pltpu.roll(v, LANES - half, 1), pltpu.roll(v, half, 1))

    def body(x_ref, c_ref, s_ref, y_ref):
        c, s = c_ref[...], s_ref[...]
        for g in range(width // LANES):
            cols = slice(g * LANES, (g + 1) * LANES)
            v = jnp.sum(x_ref[...], axis=0) if stacked else x_ref[:, cols].astype(F32)
            y = v * c - rot(v * s) if bwd else v * c + rot(v) * s
            y_ref[:, cols] = y.astype(BF16)

    row = pl.BlockSpec((tm, LANES), lambda i: (i, 0))
    x_spec = pl.BlockSpec((x.shape[0], tm, LANES), lambda i: (0, i, 0)) if stacked else _cols(tm, width, off)
    return pl.pallas_call(
        body, name=name, grid=(S // tm,), in_specs=[x_spec, row, row],
        out_specs=pl.BlockSpec((tm, width), lambda i: (i, 0)), out_shape=jax.ShapeDtypeStruct((S, width), BF16),
        compiler_params=_params(("parallel",)),
    )(x, cs2, sn2)


def _swa_mask(n, rows):
    blk = SWA_BLOCK
    row = lax.broadcasted_iota(jnp.int32, (rows, 2 * blk), 0) & (blk - 1)
    col = lax.broadcasted_iota(jnp.int32, (rows, 2 * blk), 1)
    rel = row + blk - col
    return (rel >= 0) & (rel < blk) & ((n > 0) | (col >= blk))


def _swa_specs(off_q, off_k, off_v, stacked):
    blk, aq, akv = SWA_BLOCK, SWA_Q_HEADS * SWA_HEAD_DIM, SWA_KV_HEADS * SWA_HEAD_DIM
    grp = SWA_Q_HEADS // SWA_KV_HEADS
    assert off_q % aq == 0 and off_k % akv == 0 and off_v % akv == 0 and blk & (blk - 1) == 0
    prev = lambda off: pl.BlockSpec((blk, akv), lambda n: (jnp.maximum(n - 1, 0), off // akv))
    cur = lambda off: pl.BlockSpec((blk, akv), lambda n: (n, off // akv))
    sink = _full((SWA_KV_HEADS, grp * blk, 1)) if stacked else pl.BlockSpec(memory_space=pltpu.SMEM)
    return [sink, _cols(blk, aq, off_q), prev(off_k), cur(off_k), prev(off_v), cur(off_v)]


def _swa_sinks(sinks):
    grp = SWA_Q_HEADS // SWA_KV_HEADS
    return jnp.repeat(sinks.reshape(SWA_KV_HEADS, grp), SWA_BLOCK, axis=1)[:, :, None]


def _swa_stack(x, kv):
    hd, grp = SWA_HEAD_DIM, SWA_Q_HEADS // SWA_KV_HEADS
    return jnp.concatenate([x[:, (kv * grp + g) * hd:(kv * grp + g + 1) * hd] for g in range(grp)], axis=0)


def _swa_fwd(h, off_q, off_k, off_v, sinks, name):
    S = h.shape[0]
    blk, hd, nh, nkv = SWA_BLOCK, SWA_HEAD_DIM, SWA_Q_HEADS, SWA_KV_HEADS
    grp = nh // nkv
    aq = nh * hd
    scale = hd ** -0.5

    def body(sink_ref, q_ref, kp_ref, kc_ref, vp_ref, vc_ref, o_ref, lse_ref):
        valid = _swa_mask(pl.program_id(0), blk)
        q = q_ref[...].astype(BF16)
        k2 = jnp.concatenate([kp_ref[...], kc_ref[...]], axis=0).astype(BF16)
        v2 = jnp.concatenate([vp_ref[...], vc_ref[...]], axis=0).astype(BF16)
        for hh in range(nh):
            kv = hh // grp
            qh = q[:, hh * hd:(hh + 1) * hd]
            kh = k2[:, kv * hd:(kv + 1) * hd]
            vh = v2[:, kv * hd:(kv + 1) * hd]
            s = lax.dot_general(qh, kh, _DIMS["nt"], preferred_element_type=F32) * scale
            s = jnp.where(valid, s, MASK)
            sk = sink_ref[hh]
            m = jnp.maximum(jnp.max(s, axis=1, keepdims=True), sk)
            p = jnp.exp(s - m)
            l = jnp.sum(p, axis=1, keepdims=True) + jnp.exp(sk - m)
            o_ref[:, hh * hd:(hh + 1) * hd] = jnp.dot((p / l).astype(BF16), vh, preferred_element_type=F32).astype(BF16)
            lse_ref[:, hh:hh + 1] = m + jnp.log(l)

    return pl.pallas_call(
        body, name=name, grid=(S // blk,), in_specs=_swa_specs(off_q, off_k, off_v, False),
        out_specs=[pl.BlockSpec((blk, aq), lambda n: (n, 0)), pl.BlockSpec((blk, nh), lambda n: (n, 0))],
        out_shape=[jax.ShapeDtypeStruct((S, aq), BF16), jax.ShapeDtypeStruct((S, nh), F32)],
        compiler_params=_params(("parallel",)),
    )(sinks, h, h, h, h, h)


def _swa_bwd(h, off_q, off_k, off_v, sinks, dout, lse, name):
    S = h.shape[0]
    blk, hd, nh, nkv = SWA_BLOCK, SWA_HEAD_DIM, SWA_Q_HEADS, SWA_KV_HEADS
    grp = nh // nkv
    aq, akv = nh * hd, nkv * hd
    scale = hd ** -0.5

    def body(sink_ref, q_ref, kp_ref, kc_ref, vp_ref, vc_ref, do_ref, lse_ref, dq_ref, dk_ref, dv_ref, ds_ref):
        n = pl.program_id(0)

        @pl.when(n == 0)
        def _():
            dk_ref[...] = jnp.zeros_like(dk_ref)
            dv_ref[...] = jnp.zeros_like(dv_ref)
            ds_ref[...] = jnp.zeros_like(ds_ref)

        valid = _swa_mask(n, grp * blk)
        q = q_ref[...].astype(BF16)
        k2 = jnp.concatenate([kp_ref[...], kc_ref[...]], axis=0).astype(BF16)
        v2 = jnp.concatenate([vp_ref[...], vc_ref[...]], axis=0).astype(BF16)
        do = do_ref[...]
        lane = lax.broadcasted_iota(jnp.int32, (1, 128), 1)
        dsink = jnp.zeros((1, 128), F32)
        cur = pl.ds(pl.multiple_of(n * blk, blk), blk)
        prev = pl.ds(pl.multiple_of(jnp.maximum(n - 1, 0) * blk, blk), blk)
        for kv in range(nkv):
            kh = k2[:, kv * hd:(kv + 1) * hd]
            vh = v2[:, kv * hd:(kv + 1) * hd]
            qs = _swa_stack(q, kv)
            dos = _swa_stack(do, kv)
            dos16 = dos.astype(BF16)
            lse = jnp.concatenate([lse_ref[:, kv * grp + g:kv * grp + g + 1] for g in range(grp)], axis=0)
            s = lax.dot_general(qs, kh, _DIMS["nt"], preferred_element_type=F32) * scale
            s = jnp.where(valid, s, MASK)
            p = jnp.exp(s - lse)
            p16 = p.astype(BF16)
            o = jnp.dot(p16, vh, preferred_element_type=F32)
            delta = jnp.sum(dos * o, axis=1, keepdims=True)
            dp = lax.dot_general(dos16, vh, _DIMS["nt"], preferred_element_type=F32)
            ds16 = (p * (dp - delta) * scale).astype(BF16)
            dqs = jnp.dot(ds16, kh, preferred_element_type=F32).astype(BF16)
            dk_acc = lax.dot_general(ds16, qs, _DIMS["tn"], preferred_element_type=F32)
            dv_acc = lax.dot_general(p16, dos16, _DIMS["tn"], preferred_element_type=F32)
            dsk = jnp.exp(sink_ref[kv] - lse) * delta
            for g in range(grp):
                hh = kv * grp + g
                dq_ref[:, hh * hd:(hh + 1) * hd] = dqs[g * blk:(g + 1) * blk]
                dsink += jnp.where(lane == hh, -jnp.sum(dsk[g * blk:(g + 1) * blk], axis=0, keepdims=True), 0.0)
            cols = slice(kv * hd, (kv + 1) * hd)
            dk_ref[cur, cols] += dk_acc[blk:]
            dv_ref[cur, cols] += dv_acc[blk:]

            @pl.when(n > 0)
            def _():
                dk_ref[prev, cols] += dk_acc[:blk]
                dv_ref[prev, cols] += dv_acc[:blk]

        ds_ref[...] += dsink

    return pl.pallas_call(
        body, name=name, grid=(S // blk,),
        in_specs=_swa_specs(off_q, off_k, off_v, True) + [pl.BlockSpec((blk, aq), lambda n: (n, 0)),
                                                    pl.BlockSpec((blk, nh), lambda n: (n, 0))],
        out_specs=[pl.BlockSpec((blk, aq), lambda n: (n, 0)), _full((S, akv)), _full((S, akv)), _full((1, 128))],
        out_shape=[jax.ShapeDtypeStruct((S, aq), BF16), jax.ShapeDtypeStruct((S, akv), F32),
                   jax.ShapeDtypeStruct((S, akv), F32), jax.ShapeDtypeStruct((1, 128), F32)],
        compiler_params=_params(("arbitrary",)),
    )(_swa_sinks(sinks), h, h, h, h, h, dout, lse)


def _causal(i, j, t):
    row = i * t + lax.broadcasted_iota(jnp.int32, (t, t), 0)
    col = j * t + lax.broadcasted_iota(jnp.int32, (t, t), 1)
    return col <= row


def _pair_rope(qr, hh):
    lane = lax.broadcasted_iota(jnp.int32, qr.shape, 1)
    return jnp.where(lane < MLA_ROPE, qr if hh == 0 else pltpu.roll(qr, MLA_ROPE, 1), jnp.zeros_like(qr))


def _mla_fwd(qf, qr, kvf, kr, name):
    S = qr.shape[0]
    H, dn, dv = MLA_HEADS, MLA_NOPE, MLA_V
    assert H % 2 == 0 and dn == LANES and dv == LANES and 2 * MLA_ROPE == LANES
    t = _tile(S, (MLA_FWD_TILE, 512, 256, 128))
    nq = S // t
    scale = (MLA_NOPE + MLA_ROPE) ** -0.5

    def body(qn_ref, qr_ref, kn_ref, kr_ref, v_ref, o_ref, o16_ref, lse_ref, *state):
        i = pl.program_id(1)
        for hh in range(2):
            m_s, l_s, acc_s = state[3 * hh:3 * hh + 3]
            m_s[...] = jnp.full_like(m_s, -jnp.inf)
            l_s[...] = jnp.zeros_like(l_s)
            acc_s[...] = jnp.zeros_like(acc_s)

        def block(j, masked):
            rows = pl.ds(pl.multiple_of(j * t, t), t)
            for hh in range(2):
                m_s, l_s, acc_s = state[3 * hh:3 * hh + 3]
                cols = slice(hh * LANES, (hh + 1) * LANES)
                q = jnp.concatenate([qn_ref[:, cols].astype(BF16), _pair_rope(qr_ref[...], hh)], axis=1)
                k = jnp.concatenate([kn_ref[rows, cols], kr_ref[rows, :]], axis=1)
                s = lax.dot_general(q, k, _DIMS["nt"], preferred_element_type=F32) * scale
                if masked:
                    s = jnp.where(_causal(0, 0, t), s, MASK)
                m_old = m_s[...]
                m_new = jnp.maximum(m_old, jnp.max(s, axis=1, keepdims=True))
                alpha = jnp.exp(m_old - m_new)
                p = jnp.exp(s - m_new)
                l_s[...] = alpha * l_s[...] + jnp.sum(p, axis=1, keepdims=True)
                acc_s[...] = alpha * acc_s[...] + jnp.dot(p.astype(BF16), v_ref[rows, cols], preferred_element_type=F32)
                m_s[...] = m_new

        def full_block(j, carry):
            block(j, False)
            return carry

        lax.fori_loop(0, i, full_block, 0)
        block(i, True)
        for hh in range(2):
            m_s, l_s, acc_s = state[3 * hh:3 * hh + 3]
            out = acc_s[...] / l_s[...]
            o_ref[:, hh * LANES:(hh + 1) * LANES] = out
            o16_ref[:, hh * LANES:(hh + 1) * LANES] = out.astype(BF16)
            lse_ref[hh] = m_s[...] + jnp.log(l_s[...])

    P = H // 2
    return pl.pallas_call(
        body, name=name, grid=(P, nq),
        in_specs=[pl.BlockSpec((t, 2 * LANES), lambda p, i: (i, p)), pl.BlockSpec((t, LANES), lambda p, i: (i, p)),
                  pl.BlockSpec((S, 2 * LANES), lambda p, i: (0, p)), pl.BlockSpec((S, LANES), lambda p, i: (0, 0)),
                  pl.BlockSpec((S, 2 * LANES), lambda p, i: (0, P + p))],
        out_specs=[pl.BlockSpec((t, 2 * LANES), lambda p, i: (i, p)), pl.BlockSpec((t, 2 * LANES), lambda p, i: (i, p)),
                   pl.BlockSpec((2, t, 1), lambda p, i: (p, i, 0))],
        out_shape=[jax.ShapeDtypeStruct((S, H * dv), F32), jax.ShapeDtypeStruct((S, H * dv), BF16),
                   jax.ShapeDtypeStruct((H, S, 1), F32)],
        scratch_shapes=[pltpu.VMEM((t, 1), F32), pltpu.VMEM((t, 1), F32), pltpu.VMEM((t, dv), F32)] * 2,
        compiler_params=_params(("parallel", "arbitrary")),
    )(qf, qr, kvf, kr, kvf)


def _mla_bwd(qf, qr, kvf, kr, do, out, lse, name):
    S = qr.shape[0]
    H, dv = MLA_HEADS, MLA_V
    P = H // 2
    t = _tile(S, (MLA_TILE, 256, 128))
    nq = S // t
    scale = (MLA_NOPE + MLA_ROPE) ** -0.5

    def body(qn_ref, qr_ref, kn_ref, kr_ref, v_ref, do_ref, o_ref, lse_ref, dqn_ref, dqr_ref, dkn_ref, dv_ref, dkr_ref,
             dkn_s, dv_s, dkr_s):
        j, i = pl.program_id(1), pl.program_id(2)

        @pl.when((j == 0) & (i == 0))
        def _():
            dqn_ref[...] = jnp.zeros_like(dqn_ref)
            dqr_ref[...] = jnp.zeros_like(dqr_ref)

        @pl.when(i == j)
        def _():
            dkn_s[...] = jnp.zeros_like(dkn_s)
            dv_s[...] = jnp.zeros_like(dv_s)
            dkr_s[...] = jnp.zeros_like(dkr_s)

        def block(masked):
            rows = pl.ds(pl.multiple_of(i * t, t), t)
            krv = kr_ref[...]
            for hh in range(2):
                cols = slice(hh * LANES, (hh + 1) * LANES)
                q = jnp.concatenate([qn_ref[:, cols].astype(BF16), _pair_rope(qr_ref[...], hh)], axis=1)
                k = jnp.concatenate([kn_ref[:, cols], krv], axis=1)
                vv, dof = v_ref[:, cols], do_ref[:, cols]
                dov = dof.astype(BF16)
                delta = jnp.sum(dof * o_ref[:, cols], axis=1, keepdims=True)
                s = lax.dot_general(q, k, _DIMS["nt"], preferred_element_type=F32) * scale
                if masked:
                    s = jnp.where(_causal(0, 0, t), s, MASK)
                p = jnp.exp(s - lse_ref[hh])
                p16 = p.astype(BF16)
                dp = lax.dot_general(dov, vv, _DIMS["nt"], preferred_element_type=F32)
                ds16 = (p * (dp - delta) * scale).astype(BF16)
                dv_s[:, cols] += lax.dot_general(p16, dov, _DIMS["tn"], preferred_element_type=F32)
                dk = lax.dot_general(ds16, q, _DIMS["tn"], preferred_element_type=F32)
                dkn_s[:, cols] += dk[:, :LANES]
                dkr_s[...] += dk[:, LANES:]
                dq = jnp.dot(ds16, k, preferred_element_type=F32)
                dqn_ref[rows, cols] += dq[:, :LANES]
                dqr = dq[:, LANES:]
                dqr_ref[rows, :] += dqr if hh == 0 else pltpu.roll(dqr, MLA_ROPE, 1)

        @pl.when(i == j)
        def _():
            block(True)

        @pl.when(i > j)
        def _():
            block(False)

        @pl.when(i == nq - 1)
        def _():
            dkn_ref[...] = dkn_s[...]
            dv_ref[...] = dv_s[...]
            dkr_ref[...] = dkr_s[...]

    qi = lambda i, j: jnp.maximum(i, j)
    return pl.pallas_call(
        body, name=name, grid=(P, nq, nq),
        in_specs=[pl.BlockSpec((t, 2 * LANES), lambda p, j, i: (qi(i, j), p)), pl.BlockSpec((t, LANES), lambda p, j, i: (qi(i, j), p)),
                  pl.BlockSpec((t, 2 * LANES), lambda p, j, i: (j, p)), pl.BlockSpec((t, LANES), lambda p, j, i: (j, 0)),
                  pl.BlockSpec((t, 2 * LANES), lambda p, j, i: (j, P + p)),
                  pl.BlockSpec((t, 2 * LANES), lambda p, j, i: (qi(i, j), p)),
                  pl.BlockSpec((t, 2 * LANES), lambda p, j, i: (qi(i, j), p)),
                  pl.BlockSpec((2, t, 1), lambda p, j, i: (p, qi(i, j), 0))],
        out_specs=[pl.BlockSpec((S, 2 * LANES), lambda p, j, i: (0, p)), pl.BlockSpec((S, LANES), lambda p, j, i: (0, p)),
                   pl.BlockSpec((t, 2 * LANES), lambda p, j, i: (j, p)), pl.BlockSpec((t, 2 * LANES), lambda p, j, i: (j, p)),
                   pl.BlockSpec((None, t, LANES), lambda p, j, i: (p, j, 0))],
        out_shape=[jax.ShapeDtypeStruct((S, H * MLA_NOPE), F32), jax.ShapeDtypeStruct((S, H * MLA_ROPE), F32),
                   jax.ShapeDtypeStruct((S, H * MLA_NOPE), F32), jax.ShapeDtypeStruct((S, H * dv), F32),
                   jax.ShapeDtypeStruct((P, S, LANES), F32)],
        scratch_shapes=[pltpu.VMEM((t, 2 * LANES), F32), pltpu.VMEM((t, 2 * LANES), F32), pltpu.VMEM((t, LANES), F32)],
        compiler_params=_params(("parallel", "arbitrary", "arbitrary")),
    )(qf, qr, kvf, kr, kvf, do, out, lse)


def _sgu_norm(hv, lg, lb):
    vg = _gelu(hv)
    mu = jnp.mean(vg, axis=-1, keepdims=True)
    d = vg - mu
    rstd = lax.rsqrt(jnp.mean(d * d, axis=-1, keepdims=True) + EPS)
    xhat = d * rstd
    return xhat, rstd, xhat * lg + lb


def _sgu_fwd(h, off_u, off_v, lg, lb, w16, bt, name):
    S = h.shape[0]
    T, G, C = SGU_CHUNK, SGU_GROUPS, SGU_DIM
    W = G * C

    def body(hu_ref, hv_ref, lg_ref, lb_ref, w_ref, bt_ref, y_ref):
        u = _gelu(hu_ref[...])
        _, _, vn = _sgu_norm(hv_ref[...], lg_ref[...], lb_ref[...])
        vn16 = vn.astype(BF16)
        for g in range(G):
            cols = slice(g * C, (g + 1) * C)
            mixed = jnp.dot(w_ref[g], vn16[:, cols], preferred_element_type=F32) + bt_ref[:, g:g + 1]
            y_ref[:, cols] = (u[:, cols] * mixed).astype(BF16)

    return pl.pallas_call(
        body, name=name, grid=(S // T,),
        in_specs=[_cols(T, W, off_u), _cols(T, W, off_v), _full((1, W)), _full((1, W)), _full((G, T, T)), _full((T, G))],
        out_specs=pl.BlockSpec((T, W), lambda n: (n, 0)),
        out_shape=jax.ShapeDtypeStruct((S, W), BF16),
        compiler_params=_params(("parallel",)),
    )(h, h, lg, lb, w16, bt)


def _sgu_bwd(h, off_u, off_v, lg, lb, w16, bt, dy, name):
    S = h.shape[0]
    T, G, C = SGU_CHUNK, SGU_GROUPS, SGU_DIM
    W = G * C
    nc = S // T

    def body(hu_ref, hv_ref, lg_ref, lb_ref, w_ref, bt_ref, dy_ref, dhu_ref, dhv_ref, dw_ref, db_ref, dlg_ref, dlb_ref,
             dmix_s, dvn_s):
        n = pl.program_id(0)

        @pl.when(n == 0)
        def _():
            dw_ref[...] = jnp.zeros_like(dw_ref)
            dlg_ref[...] = jnp.zeros_like(dlg_ref)
            dlb_ref[...] = jnp.zeros_like(dlb_ref)
            dmix_s[...] = jnp.zeros_like(dmix_s)

        hu, hv, lgv = hu_ref[...], hv_ref[...], lg_ref[...]
        u = _gelu(hu)
        xhat, rstd, vn = _sgu_norm(hv, lgv, lb_ref[...])
        vn16 = vn.astype(BF16)
        dyv = dy_ref[...]
        dmixed = dyv * u
        dmix_s[...] += dmixed
        dmixed16 = dmixed.astype(BF16)
        for g in range(G):
            cols = slice(g * C, (g + 1) * C)
            mixed = jnp.dot(w_ref[g], vn16[:, cols], preferred_element_type=F32) + bt_ref[:, g:g + 1]
            dhu_ref[:, cols] = (dyv[:, cols] * mixed * _gelu_grad(hu[:, cols])).astype(BF16)
            dvn_s[:, cols] = lax.dot_general(w_ref[g], dmixed16[:, cols], _DIMS["tn"], preferred_element_type=F32)
            dw_ref[g] += lax.dot_general(dmixed16[:, cols], vn16[:, cols], _DIMS["nt"], preferred_element_type=F32)
        dvn = dvn_s[...]
        dlg_ref[...] += jnp.sum(dvn * xhat, axis=0, keepdims=True)
        dlb_ref[...] += jnp.sum(dvn, axis=0, keepdims=True)
        dxh = dvn * lgv
        m1 = jnp.mean(dxh, axis=-1, keepdims=True)
        m2 = jnp.mean(dxh * xhat, axis=-1, keepdims=True)
        dvg = rstd * (dxh - m1 - xhat * m2)
        dhv_ref[...] = (dvg * _gelu_grad(hv)).astype(BF16)

        @pl.when(n == nc - 1)
        def _():
            tril = lax.broadcasted_iota(jnp.int32, (T, T), 1) <= lax.broadcasted_iota(jnp.int32, (T, T), 0)
            lane = lax.broadcasted_iota(jnp.int32, (T, 128), 1)
            db = jnp.zeros((T, 128), F32)
            for g in range(G):
                dw_ref[g] = jnp.where(tril, dw_ref[g], 0.0)
                db += jnp.where(lane == g, jnp.sum(dmix_s[:, g * C:(g + 1) * C], axis=1, keepdims=True), 0.0)
            db_ref[...] = db

    row = pl.BlockSpec((T, W), lambda n: (n, 0))
    return pl.pallas_call(
        body, name=name, grid=(nc,),
        in_specs=[_cols(T, W, off_u), _cols(T, W, off_v), _full((1, W)), _full((1, W)), _full((G, T, T)), _full((T, G)), row],
        out_specs=[row, row, _full((G, T, T)), _full((T, 128)), _full((1, W)), _full((1, W))],
        out_shape=[jax.ShapeDtypeStruct((S, W), BF16), jax.ShapeDtypeStruct((S, W), BF16),
                   jax.ShapeDtypeStruct((G, T, T), F32), jax.ShapeDtypeStruct((T, 128), F32),
                   jax.ShapeDtypeStruct((1, W), F32), jax.ShapeDtypeStruct((1, W), F32)],
        scratch_shapes=[pltpu.VMEM((T, W), F32), pltpu.VMEM((T, W), F32)],
        compiler_params=_params(("arbitrary",)),
    )(h, h, lg, lb, w16, bt, dy)


def _merge_fwd(ys, ps, h, bg, name):
    S = h.shape[0]
    D = ps[0].shape[1]
    tm = _tile(S, (1024, 512, 256, 128))
    tn = _tile(D, (256, 128))
    nb = len(ys)

    def body(*refs):
        y_refs, p_refs, l_refs = refs[:nb], refs[nb:2 * nb], refs[2 * nb:3 * nb]
        bg_ref, mg_ref, z_ref = refs[3 * nb:]
        acc = jnp.zeros((tm, tn), F32)
        for b in range(nb):
            z = jnp.dot(y_refs[b][...].astype(BF16), p_refs[b][...], preferred_element_type=F32)
            z_ref[b] = z
            acc += _sigmoid(l_refs[b][...] + bg_ref[b:b + 1, :]) * z
        mg_ref[...] = acc.astype(BF16)

    in_specs = [pl.BlockSpec((tm, y.shape[1]), lambda i, j: (i, 0)) for y in ys]
    in_specs += [pl.BlockSpec((p.shape[0], tn), lambda i, j: (0, j)) for p in ps]
    in_specs += [pl.BlockSpec((tm, tn), functools.partial(lambda i, j, b: (i, b * (D // tn) + j), b=b)) for b in range(nb)]
    in_specs += [pl.BlockSpec((nb, tn), lambda i, j: (0, j))]
    return pl.pallas_call(
        body, name=name, grid=(S // tm, D // tn), in_specs=in_specs,
        out_specs=[pl.BlockSpec((tm, tn), lambda i, j: (i, j)), pl.BlockSpec((nb, tm, tn), lambda i, j: (0, i, j))],
        out_shape=[jax.ShapeDtypeStruct((S, D), BF16), jax.ShapeDtypeStruct((nb, S, D), F32)],
        compiler_params=_params(("parallel", "parallel")),
    )(*ys, *ps, *([h] * nb), bg)


def _merge_bwd(dm, z, h, bg, name):
    nb, S, D = z.shape
    tm = _tile(S, (256, 128))
    tn = _tile(D, (512, 256, 128))

    def body(*refs):
        dm_ref, z_ref = refs[0], refs[1]
        l_refs = refs[2:2 + nb]
        bg_ref, dz_ref, dl_ref, dbg_ref = refs[2 + nb:]

        @pl.when(pl.program_id(1) == 0)
        def _():
            dbg_ref[...] = jnp.zeros_like(dbg_ref)

        dmv = dm_ref[...]
        rows = lax.broadcasted_iota(jnp.int32, (SUBLANES, tn), 0)
        dbg = jnp.zeros((SUBLANES, tn), F32)
        for b in range(nb):
            gt = _sigmoid(l_refs[b][...] + bg_ref[b:b + 1, :])
            dz_ref[b] = (dmv * gt).astype(BF16)
            dl = dmv * z_ref[b] * gt * (1.0 - gt)
            dl_ref[b] = dl.astype(BF16)
            dbg += jnp.where(rows == b, jnp.sum(dl, axis=0, keepdims=True), 0.0)
        dbg_ref[...] += dbg

    in_specs = [pl.BlockSpec((tm, tn), lambda j, i: (i, j)), pl.BlockSpec((nb, tm, tn), lambda j, i: (0, i, j))]
    in_specs += [pl.BlockSpec((tm, tn), functools.partial(lambda j, i, b: (i, b * (D // tn) + j), b=b)) for b in range(nb)]
    in_specs += [pl.BlockSpec((nb, tn), lambda j, i: (0, j))]
    blk3 = pl.BlockSpec((nb, tm, tn), lambda j, i: (0, i, j))
    return pl.pallas_call(
        body, name=name, grid=(D // tn, S // tm), in_specs=in_specs,
        out_specs=[blk3, blk3, pl.BlockSpec((SUBLANES, tn), lambda j, i: (0, j))],
        out_shape=[jax.ShapeDtypeStruct((nb, S, D), BF16), jax.ShapeDtypeStruct((nb, S, D), BF16),
                   jax.ShapeDtypeStruct((SUBLANES, D), F32)],
        compiler_params=_params(("parallel", "arbitrary")),
    )(dm, z, *([h] * nb), bg)


def _shift_down(x, halo, k):
    xr = pltpu.roll(x, k, 0)
    hr = pltpu.roll(halo, k, 0)
    rows = lax.broadcasted_iota(jnp.int32, halo.shape, 0)
    top = jnp.where(rows < k, hr, xr[:SUBLANES])
    return jnp.concatenate([top, xr[SUBLANES:]], axis=0)


def _shift_up(x, halo, k):
    tm = x.shape[0]
    xr = pltpu.roll(x, tm - k, 0)
    hr = pltpu.roll(halo, SUBLANES - k, 0)
    rows = lax.broadcasted_iota(jnp.int32, halo.shape, 0)
    bot = jnp.where(rows >= SUBLANES - k, hr, xr[tm - SUBLANES:])
    return jnp.concatenate([xr[:tm - SUBLANES], bot], axis=0)


def _conv_tiles(S, F):
    return _tile(S, (ROW_TILE, 256, 128)), _tile(F, (512, 256, 128))


def _conv_in_specs(tm, tn, F):
    r8 = tm // SUBLANES
    nf = F // tn
    specs = []
    for half in range(2):
        specs.append(pl.BlockSpec((tm, tn), functools.partial(lambda j, i, o: (i, o + j), o=half * nf)))
        specs.append(pl.BlockSpec((SUBLANES, tn), functools.partial(lambda j, i, o: (jnp.maximum(i * r8 - 1, 0), o + j), o=half * nf)))
    for half in range(2):
        specs.append(pl.BlockSpec((3, tn), functools.partial(lambda j, i, o: (0, o + j), o=half * nf)))
        specs.append(pl.BlockSpec((1, tn), functools.partial(lambda j, i, o: (0, o + j), o=half * nf)))
    return specs


def _conv_apply(x, halo, w, b, first):
    halo = jnp.where(first, 0.0, halo)
    x1 = _shift_down(x, halo, 1)
    x2 = _shift_down(x, halo, 2)
    return b + x2 * w[0:1, :] + x1 * w[1:2, :] + x * w[2:3, :], x1, x2


def _glu_fwd(up, cw, cb, name):
    S, F2 = up.shape
    F = F2 // 2
    tm, tn = _conv_tiles(S, F)

    def body(ug, hg, uv, hv, wg, bgr, wv, bvr, a_ref):
        first = pl.program_id(1) == 0
        cg, _, _ = _conv_apply(ug[...], hg[...], wg[...], bgr[...], first)
        cv, _, _ = _conv_apply(uv[...], hv[...], wv[...], bvr[...], first)
        a_ref[...] = (cg * _sigmoid(cg) * cv).astype(BF16)

    return pl.pallas_call(
        body, name=name, grid=(F // tn, S // tm), in_specs=_conv_in_specs(tm, tn, F),
        out_specs=pl.BlockSpec((tm, tn), lambda j, i: (i, j)),
        out_shape=jax.ShapeDtypeStruct((S, F), BF16),
        compiler_params=_params(("parallel", "parallel")),
    )(up, up, up, up, cw, cb, cw, cb)


def _glu_bwd(up, cw, cb, da, name):
    S, F2 = up.shape
    F = F2 // 2
    tm, tn = _conv_tiles(S, F)

    def body(ug, hg, uv, hv, wg, bgr, wv, bvr, da_ref, dg_ref, dv_ref, sg_ref, sv_ref):
        i = pl.program_id(1)

        @pl.when(i == 0)
        def _():
            sg_ref[...] = jnp.zeros_like(sg_ref)
            sv_ref[...] = jnp.zeros_like(sv_ref)

        first = i == 0
        xg, xv = ug[...], uv[...]
        cg, xg1, xg2 = _conv_apply(xg, hg[...], wg[...], bgr[...], first)
        cv, xv1, xv2 = _conv_apply(xv, hv[...], wv[...], bvr[...], first)
        dav = da_ref[...]
        sg = _sigmoid(cg)
        dcv = dav * cg * sg
        dcg = dav * cv * sg * (1.0 + cg * (1.0 - sg))
        dg_ref[...] = dcg
        dv_ref[...] = dcv
        rows = lax.broadcasted_iota(jnp.int32, (SUBLANES, tn), 0)

        def stats(dc, x, x1, x2):
            acc = jnp.zeros((SUBLANES, tn), F32)
            for r, val in enumerate((dc * x2, dc * x1, dc * x, dc)):
                acc += jnp.where(rows == r, jnp.sum(val, axis=0, keepdims=True), 0.0)
            return acc

        sg_ref[...] += stats(dcg, xg, xg1, xg2)
        sv_ref[...] += stats(dcv, xv, xv1, xv2)

    tile = pl.BlockSpec((tm, tn), lambda j, i: (i, j))
    stat = pl.BlockSpec((SUBLANES, tn), lambda j, i: (0, j))
    return pl.pallas_call(
        body, name=name, grid=(F // tn, S // tm), in_specs=_conv_in_specs(tm, tn, F) + [tile],
        out_specs=[tile, tile, stat, stat],
        out_shape=[jax.ShapeDtypeStruct((S, F), F32), jax.ShapeDtypeStruct((S, F), F32),
                   jax.ShapeDtypeStruct((SUBLANES, F), F32), jax.ShapeDtypeStruct((SUBLANES, F), F32)],
        compiler_params=_params(("parallel", "arbitrary")),
    )(up, up, up, up, cw, cb, cw, cb, da)


def _conv_bwd(dcg, dcv, w, name):
    S, F = dcg.shape
    tm, tn = _tile(S, (CONV_BWD_ROWS, 1024, 512, 256, 128)), _conv_tiles(S, F)[1]
    r8 = tm // SUBLANES
    ni = S // tm
    nf = F // tn

    def body(g_ref, gh_ref, v_ref, vh_ref, w_ref, o_ref):
        gate = pl.program_id(0) == 0
        x = jnp.where(gate, g_ref[...], v_ref[...])
        halo = jnp.where(gate, gh_ref[...], vh_ref[...])
        halo = jnp.where(pl.program_id(2) == ni - 1, 0.0, halo)
        wv = w_ref[...]
        o_ref[...] = (x * wv[2:3, :] + _shift_up(x, halo, 1) * wv[1:2, :] + _shift_up(x, halo, 2) * wv[0:1, :]).astype(BF16)

    def tile(half):
        return pl.BlockSpec((tm, tn), lambda h, j, i: (jnp.where(h == half, i, 0), jnp.where(h == half, j, 0)))

    def below(half):
        return pl.BlockSpec((SUBLANES, tn), lambda h, j, i: (
            jnp.where(h == half, jnp.minimum((i + 1) * r8, S // SUBLANES - 1), 0), jnp.where(h == half, j, 0)))

    return pl.pallas_call(
        body, name=name, grid=(2, nf, ni),
        in_specs=[tile(0), below(0), tile(1), below(1), pl.BlockSpec((3, tn), lambda h, j, i: (0, h * nf + j))],
        out_specs=pl.BlockSpec((tm, tn), lambda h, j, i: (i, h * nf + j)),
        out_shape=jax.ShapeDtypeStruct((S, 2 * F), BF16),
        compiler_params=_params(("parallel", "parallel", "parallel")),
    )(dcg, dcg, dcv, dcv, w)


def _adamw(slot_list, own_list, me, w, m, v, name, dep=None):
    L = len(slot_list)
    P, K, C = slot_list[0].shape
    tr = _tile(K, (256, 128, 64, 32, 16))
    while tr * C * 4 > (1 << 20) and tr % 32 == 0:
        tr //= 2
    nb = K // tr
    has_own = own_list is not None

    def body(me_ref, *refs):
        s_refs = refs[:L]
        o_refs = refs[L:2 * L] if has_own else None
        w_ref, m_ref, v_ref = refs[L * (1 + has_own):L * (1 + has_own) + 3]
        g_ref, d_ref, nm_ref, nv_ref = refs[-4:]
        layer = pl.program_id(0)
        g = None
        for l in range(L):
            gl = None
            for p in range(P):
                term = s_refs[l][p].astype(F32)
                if has_own:
                    term = jnp.where(me_ref[0] == p, o_refs[l][0].astype(F32), term)
                gl = term if gl is None else gl + term
            g = gl if g is None else jnp.where(layer == l, gl, g)
        nm = ADAM_B1 * m_ref[...] + (1.0 - ADAM_B1) * g
        nv = ADAM_B2 * v_ref[...] + (1.0 - ADAM_B2) * (g * g)
        m_hat = nm / (1.0 - ADAM_B1 ** ADAM_STEP)
        v_hat = nv / (1.0 - ADAM_B2 ** ADAM_STEP)
        g_ref[...] = g
        d_ref[...] = -ADAM_LR * (m_hat / (jnp.sqrt(v_hat) + ADAM_EPS) + ADAM_WD * w_ref[...])
        nm_ref[...] = nm
        nv_ref[...] = nv

    blk = pl.BlockSpec((None, tr, C), lambda li, i, me_ref: (li, i, 0))
    specs = [pl.BlockSpec((P, tr, C), functools.partial(lambda li, i, me_ref, l: (0, jnp.where(li == l, i, 0), 0), l=l))
             for l in range(L)]
    if has_own:
        specs += [pl.BlockSpec((1, tr, C), functools.partial(lambda li, i, me_ref, l: (me_ref[0], jnp.where(li == l, i, 0), 0), l=l))
                  for l in range(L)]
    return pl.pallas_call(
        body, name=name,
        grid_spec=pltpu.PrefetchScalarGridSpec(
            num_scalar_prefetch=1, grid=(L, nb), in_specs=specs + [blk, blk, blk] + [_ANY] * (dep is not None),
            out_specs=[blk] * 4),
        out_shape=[jax.ShapeDtypeStruct((L, K, C), F32)] * 4,
        compiler_params=_params(("arbitrary", "arbitrary")),
    )(me, *slot_list, *(own_list if has_own else []), w, m, v, *([] if dep is None else [dep]))


_HBM = pl.BlockSpec(memory_space=pltpu.HBM)
_SEM = pl.BlockSpec(memory_space=pltpu.SEMAPHORE)
_ANY = pl.BlockSpec(memory_space=pl.ANY)


def _peers():
    x, y, c = lax.axis_index("x"), lax.axis_index("y"), lax.axis_index("c")

    def flip(v, bit):
        return 1 - v if bit else v

    def peer(k):
        return (flip(x, (k >> 2) & 1), flip(y, (k >> 1) & 1), flip(c, k & 1))

    def peer_index(k):
        px, py, pc = peer(k)
        return 4 * px + 2 * py + pc

    return 4 * x + 2 * y + c, peer, peer_index


def _split_copy(src_refs, land_refs, send_sems, recv_sems, scatter, a, k, outgoing):
    me, peer, peer_index = _peers()
    if outgoing:
        src = src_refs[a].at[peer_index(k)] if scatter else src_refs[a]
        dst = land_refs[a].at[me]
    else:
        src = src_refs[a].at[me] if scatter else src_refs[a]
        dst = land_refs[a].at[peer_index(k)]
    pair = a * (N_DEV - 1) + k - 1
    return pltpu.make_async_remote_copy(src_ref=src, dst_ref=dst, send_sem=send_sems.at[pair],
                                        recv_sem=recv_sems.at[pair], device_id=peer(k),
                                        device_id_type=pl.DeviceIdType.MESH)


def _gather_two_level(srcs, name):
    na = len(srcs)

    def body(*refs):
        src_refs, out_refs = refs[:na], refs[na:2 * na]
        send_sems, recv_sems = refs[2 * na:]
        x, y, c = lax.axis_index("x"), lax.axis_index("y"), lax.axis_index("c")
        me, sibling = (x, y, c), (x, y, 1 - c)
        chips = [(1 - x, y), (x, 1 - y), (1 - x, 1 - y)]

        def copy(a, k, block, to, src=None):
            px, py, pc = block
            slot = out_refs[a].at[4 * px + 2 * py + pc]
            return pltpu.make_async_remote_copy(
                src_ref=slot if src is None else src, dst_ref=slot, send_sem=send_sems.at[a * (N_DEV - 1) + k],
                recv_sem=recv_sems.at[a * (N_DEV - 1) + k], device_id=to, device_id_type=pl.DeviceIdType.MESH)

        first = [copy(a, 0, me, sibling, src_refs[a]) for a in range(na)]
        first += [copy(a, 1 + j, me, (*chip, c), src_refs[a]) for j, chip in enumerate(chips) for a in range(na)]
        for cp in first:
            cp.start()
        passed = []
        for j, chip in enumerate(chips):
            for a in range(na):
                copy(a, 1 + j, (*chip, c), me).wait_recv()
                passed.append(copy(a, 4 + j, (*chip, c), sibling))
                passed[-1].start()
        for a in range(na):
            copy(a, 0, sibling, me).wait_recv()
        for j, chip in enumerate(chips):
            for a in range(na):
                copy(a, 4 + j, (*chip, 1 - c), me).wait_recv()
        for cp in first + passed:
            cp.wait_send()

    return pl.pallas_call(
        body, name=name, in_specs=[_ANY] * na, out_specs=[_ANY] * na,
        out_shape=[jax.ShapeDtypeStruct((N_DEV,) + s.shape, s.dtype) for s in srcs],
        scratch_shapes=[pltpu.SemaphoreType.DMA((na * (N_DEV - 1),)), pltpu.SemaphoreType.DMA((na * (N_DEV - 1),))],
    )(*srcs)


def _exchange_start(srcs, scatter, after, name):
    na = len(srcs)
    land_shapes = [s.shape if scatter else (N_DEV,) + s.shape for s in srcs]
    has_after = after is not None

    def body(*refs):
        src_refs, land_refs = refs[:na], refs[na:2 * na]
        send_sems, recv_sems = refs[2 * na + has_after], refs[2 * na + has_after + 1]
        token = refs[-1]
        for k in range(1, N_DEV):
            for a in range(na):
                _split_copy(src_refs, land_refs, send_sems, recv_sems, scatter, a, k, True).start()
        token[...] = jnp.zeros_like(token)

    sems = pltpu.SemaphoreType.DMA((na * (N_DEV - 1),))
    out_shape = ([sems, sems] + [pltpu.HBM(s.shape, s.dtype) for s in srcs]
                 + [pltpu.HBM(shp, s.dtype) for shp, s in zip(land_shapes, srcs)] + [jax.ShapeDtypeStruct((SUBLANES, 128), F32)])
    args = [pltpu.with_memory_space_constraint(s, pltpu.HBM) for s in srcs]
    args += [pltpu.with_memory_space_constraint(lax.empty(shp, s.dtype), pltpu.HBM) for shp, s in zip(land_shapes, srcs)]
    if has_after:
        args.append(after)
    res = pl.pallas_call(
        body, name=name, in_specs=[_HBM] * (2 * na) + [_ANY] * has_after,
        out_specs=[_SEM, _SEM] + [_HBM] * (2 * na) + [pl.BlockSpec(memory_space=pltpu.VMEM)], out_shape=out_shape,
        input_output_aliases={i: 2 + i for i in range(2 * na)},
        compiler_params=pltpu.CompilerParams(has_side_effects=pltpu.SideEffectType.DATAFLOW_SIDE_EFFECTING),
    )(*args)
    handle = dict(send=res[0], recv=res[1], srcs=list(res[2:2 + na]), lands=list(res[2 + na:2 + 2 * na]), scatter=scatter)
    return handle, res[-1]


def _exchange_wait(handle, after, name):
    srcs, lands, scatter = handle["srcs"], handle["lands"], handle["scatter"]
    na = len(srcs)

    def body(*refs):
        src_refs, land_refs = refs[:na], refs[na:2 * na]
        send_sems, recv_sems = refs[2 * na], refs[2 * na + 1]
        for k in range(1, N_DEV):
            for a in range(na):
                _split_copy(src_refs, land_refs, send_sems, recv_sems, scatter, a, k, True).wait_send()
                _split_copy(src_refs, land_refs, send_sems, recv_sems, scatter, a, k, False).wait_recv()

    res = pl.pallas_call(
        body, name=name, in_specs=[_HBM] * (2 * na) + [_SEM, _SEM, _ANY], out_specs=[_HBM] * (2 * na),
        out_shape=[pltpu.HBM(s.shape, s.dtype) for s in srcs] + [pltpu.HBM(s.shape, s.dtype) for s in lands],
        input_output_aliases={i: i for i in range(2 * na)},
        compiler_params=pltpu.CompilerParams(has_side_effects=pltpu.SideEffectType.DATAFLOW_SIDE_EFFECTING),
    )(*srcs, *lands, handle["send"], handle["recv"], after)
    return list(res[:na]), list(res[na:])


def _layout(D):
    aq, akv = SWA_Q_HEADS * SWA_HEAD_DIM, SWA_KV_HEADS * SWA_HEAD_DIM
    w = SGU_GROUPS * SGU_DIM
    return aq, akv, w


class _Seg:
    def __init__(self, D, rq, rkv):
        aq, akv, w = _layout(D)
        src = {}
        o = 0
        for nm, wd in (("qa", aq), ("ka", akv), ("va", akv), ("cq", rq), ("ckv", rkv), ("kr", MLA_ROPE), ("hu", w), ("hv", w),
                       ("g", 3 * D)):
            src[nm] = (o, wd)
            o += wd
        self.n_in = o
        self.order = ("g", "qa", "hu", "hv", "cq", "ckv", "ka", "va", "kr")
        self.src = src
        self.off = {}
        o = 0
        for nm in self.order:
            self.off[nm] = o
            o += src[nm][1]
        self.width = {nm: src[nm][1] for nm in self.order}
        self.n_pad = -(-o // 1536) * 1536 if o > 1536 else -(-o // 512) * 512
        self.used = o

    def from_shards(self, shards):
        c = shards.shape[2]
        parts = []
        for nm in self.order:
            s0, wd = self.src[nm]
            for d in range(N_DEV):
                lo, hi = max(s0, c * d), min(s0 + wd, c * (d + 1))
                if lo < hi:
                    parts.append(shards[d][:, lo - c * d:hi - c * d])
        parts.append(jnp.zeros((shards.shape[1], self.n_pad - self.used), shards.dtype))
        return jnp.concatenate(parts, axis=1)

    def to_shards(self, w):
        c = self.n_in // N_DEV
        names = sorted(self.order, key=lambda nm: self.src[nm][0])
        shards = []
        for d in range(N_DEV):
            parts = []
            for nm in names:
                s0, wd = self.src[nm]
                lo, hi = max(s0, c * d), min(s0 + wd, c * (d + 1))
                if lo < hi:
                    parts.append(w[:, self.off[nm] + lo - s0:self.off[nm] + hi - s0])
            shards.append(jnp.concatenate(parts, axis=1))
        return jnp.stack(shards)


def _uq_permute(w):
    R = w.shape[0]
    H = MLA_HEADS
    w3 = w.reshape(R, H, MLA_NOPE + MLA_ROPE)
    return jnp.concatenate([w3[:, :, :MLA_NOPE].reshape(R, H * MLA_NOPE), w3[:, :, MLA_NOPE:].reshape(R, H * MLA_ROPE)], axis=1)


def _uq_unpermute(w):
    R = w.shape[0]
    H = MLA_HEADS
    n = w[:, :H * MLA_NOPE].reshape(R, H, MLA_NOPE)
    r = w[:, H * MLA_NOPE:].reshape(R, H, MLA_ROPE)
    return jnp.concatenate([n, r], axis=2).reshape(R, H * (MLA_NOPE + MLA_ROPE))


def _ukv_permute(w):
    R = w.shape[0]
    w3 = w.reshape(R, MLA_HEADS, MLA_NOPE + MLA_V)
    return jnp.concatenate([w3[:, :, :MLA_NOPE].reshape(R, -1), w3[:, :, MLA_NOPE:].reshape(R, -1)], axis=1)


def _ukv_unpermute(w):
    R = w.shape[0]
    H = MLA_HEADS
    k = w[:, :H * MLA_NOPE].reshape(R, H, MLA_NOPE)
    v = w[:, H * MLA_NOPE:].reshape(R, H, MLA_V)
    return jnp.concatenate([k, v], axis=2).reshape(R, H * (MLA_NOPE + MLA_V))


GROUPS = {"a": ("w_in",), "b": ("w_uq", "w_ukv", "w_proj_a", "w_proj_b", "w_proj_c", "w_o", "b_gate"),
          "c": ("w_up", "w_down", "conv_w")}


def _layer_fwd(l, x, x16, fetch, P, cs, sn, seg, alpha):
    S, D = x.shape
    H, half = MLA_HEADS, MLA_ROPE // 2
    off = seg.off
    nm = lambda s: f"l{l}_{s}"
    sv = {"x16": x16}
    W = {"w_in": seg.from_shards(fetch(l, "a", x16)["w_in"])}
    h = _mm(x16, W["w_in"], "nn", [(F32, "n")], nm("h"))[0]
    sv["h"] = h
    ya, lse_a = _swa_fwd(h, off["qa"], off["ka"], off["va"], P["sinks"], nm("swa_fwd"))
    cqn, rq = _rms_fwd(h, off["cq"], seg.width["cq"], P["q_norm_g"], nm("rmsq_fwd"))
    ckvn, rkv = _rms_fwd(h, off["ckv"], seg.width["ckv"], P["kv_norm_g"], nm("rmskv_fwd"))
    W.update(fetch(l, "b", cqn))
    W["w_uq"] = _uq_permute(W["w_uq"])
    W["w_ukv"] = _ukv_permute(W["w_ukv"])
    qf = _mm(cqn, W["w_uq"], "nn", [(F32, "n")], nm("uq"))[0]
    kvf = _mm(ckvn, W["w_ukv"], "nn", [(BF16, "n")], nm("ukv"))[0]
    qr = _rope(qf, H * MLA_NOPE, H * MLA_ROPE, cs, sn, False, nm("ropeq_fwd"))
    kr = _rope(h, off["kr"], LANES, cs, sn, False, nm("ropek_fwd"))
    yb, yb16, lse_b = _mla_fwd(qf, qr, kvf, kr, nm("mla_fwd"))
    w16 = jnp.where(jnp.tril(jnp.ones((SGU_CHUNK, SGU_CHUNK), bool))[None], P["sgu_w"], 0.0).astype(BF16)
    bt = P["sgu_b"].T
    yc = _sgu_fwd(h, off["hu"], off["hv"], P["sgu_ln_g"], P["sgu_ln_b"], w16, bt, nm("sgu_fwd"))
    merged, z = _merge_fwd([ya, yb16, yc], [W["w_proj_a"], W["w_proj_b"], W["w_proj_c"]], h, W["b_gate"], nm("merge_fwd"))
    x1, x1_16, xh1, rs1 = _mm_ln(merged, W["w_o"], x, P["ln1_g"], P["ln1_b"], alpha, nm("wo_ln1"))
    W.update(fetch(l, "c", x1_16))
    up = _mm(x1_16, W["w_up"], "nn", [(F32, "n")], nm("up"))[0]
    a = _glu_fwd(up, W["conv_w"], P["conv_b"], nm("glu_fwd"))
    x2, x2_16, xh2, rs2 = _mm_ln(a, W["w_down"], x1, P["ln2_g"], P["ln2_b"], alpha, nm("down_ln2"))
    sv.update(W=W, ya=ya, lse_a=lse_a, cqn=cqn, rq=rq, ckvn=ckvn, rkv=rkv, qf=qf, qr=qr, kvf=kvf, kr=kr, lse_b=lse_b, yb=yb, yb16=yb16,
              w16=w16, bt=bt, yc=yc, merged=merged, z=z, x1_16=x1_16, xh1=xh1, rs1=rs1, up=up, a=a, xh2=xh2, rs2=rs2)
    return x2, x2_16, sv


def _dw_chunks(k, a, dy, name, post=None, chunker=None):
    n = dy.shape[1]
    if k not in ROW_SHARDED and post is None and chunker is None and (n // N_DEV) % 128 == 0:
        return _mm(a, dy, "tn", [(BF16, "n")], name, chunk=n // N_DEV)[0]
    g = _mm(a, dy, "tn", [(BF16, "n")], name)[0]
    if chunker is not None:
        return chunker(g)
    return _to_chunks(k, g if post is None else post(g))


def _after(arr, token):
    return arr if token is None else arr + token[0:1, 0:1].astype(arr.dtype)


def _layer_bwd(l, dx2, sv, P, cs, sn, seg, alpha, emit):
    S, D = dx2.shape
    H, half = MLA_HEADS, MLA_ROPE // 2
    off = seg.off
    h, W = sv["h"], sv["W"]
    nm = lambda s: f"l{l}_{s}"
    g = {}
    dr2, dr2_16, g["ln2_g"], g["ln2_b"] = _ln_bwd(dx2, sv["xh2"], sv["rs2"], P["ln2_g"], nm("ln2_bwd"))
    g["w_down"] = _dw_chunks("w_down", sv["a"], dr2_16, nm("dw_down"))
    da = _mm(dr2_16, W["w_down"], "nt", [(F32, "n")], nm("da"))[0]
    dcg, dcv, st_g, st_v = _glu_bwd(sv["up"], W["conv_w"], P["conv_b"], da, nm("glu_bwd"))
    F = dcg.shape[1]
    g["conv_w"] = _to_chunks("conv_w", jnp.concatenate([st_g[0:3], st_v[0:3]], axis=1))
    g["conv_b"] = jnp.concatenate([st_g[3:4], st_v[3:4]], axis=1)
    dup = _conv_bwd(dcg, dcv, W["conv_w"], nm("conv_bwd"))
    g["w_up"] = _dw_chunks("w_up", sv["x1_16"], dup, nm("dw_up"))
    token = emit(l, "c", {k: g.pop(k) for k in GROUPS["c"]})
    dx1 = _mm_axpy(dup, W["w_up"], "nt", dr2, alpha, nm("dx1"), dep=token)
    dr1, dr1_16, g["ln1_g"], g["ln1_b"] = _ln_bwd(dx1, sv["xh1"], sv["rs1"], P["ln1_g"], nm("ln1_bwd"))
    g["w_o"] = _dw_chunks("w_o", sv["merged"], dr1_16, nm("dw_o"))
    dmerged = _mm(dr1_16, W["w_o"], "nt", [(F32, "n")], nm("dmerged"))[0]
    dz, dlog, dbg = _merge_bwd(dmerged, sv["z"], h, W["b_gate"], nm("merge_bwd"))
    g["b_gate"] = _to_chunks("b_gate", dbg[0:3])
    g["w_proj_a"] = _dw_chunks("w_proj_a", sv["ya"], dz[0], nm("dw_pa"))
    g["w_proj_b"] = _dw_chunks("w_proj_b", sv["yb16"], dz[1], nm("dw_pb"))
    g["w_proj_c"] = _dw_chunks("w_proj_c", sv["yc"], dz[2], nm("dw_pc"))
    dya = _mm(dz[0], W["w_proj_a"], "nt", [(F32, "n")], nm("dya"))[0]
    dyb = _mm(dz[1], W["w_proj_b"], "nt", [(F32, "n")], nm("dyb"))[0]
    dyc = _mm(dz[2], W["w_proj_c"], "nt", [(F32, "n")], nm("dyc"))[0]
    dhu, dhv, g["sgu_w"], db_s, g["sgu_ln_g"], g["sgu_ln_b"] = _sgu_bwd(
        h, off["hu"], off["hv"], P["sgu_ln_g"], P["sgu_ln_b"], sv["w16"], sv["bt"], dyc, nm("sgu_bwd"))
    g["sgu_b"] = db_s[:, :SGU_GROUPS].T
    dqa, dka, dva, dsk = _swa_bwd(h, off["qa"], off["ka"], off["va"], P["sinks"], dya, sv["lse_a"], nm("swa_bwd"))
    g["sinks"] = dsk[0, :SWA_Q_HEADS]
    dqn, dqr, dkn, dvv, dkr = _mla_bwd(sv["qf"], sv["qr"], sv["kvf"], sv["kr"], dyb, sv["yb"], sv["lse_b"], nm("mla_bwd"))
    dqf = jnp.concatenate([dqn.astype(BF16), _rope(dqr, 0, H * MLA_ROPE, cs, sn, True, nm("ropeq_bwd"))], axis=1)
    dkvf = jnp.concatenate([dkn, dvv], axis=1).astype(BF16)
    dkr16 = _rope(dkr, 0, LANES, cs, sn, True, nm("ropek_bwd"))
    g["w_uq"] = _dw_chunks("w_uq", sv["cqn"], dqf, nm("dw_uq"), _uq_unpermute)
    g["w_ukv"] = _dw_chunks("w_ukv", sv["ckvn"], dkvf, nm("dw_ukv"), _ukv_unpermute)
    token = emit(l, "b", {k: g.pop(k) for k in GROUPS["b"]})
    dcqn = _mm(dqf, W["w_uq"], "nt", [(F32, "n")], nm("dcqn"), dep=token)[0]
    dckvn = _mm(dkvf, W["w_ukv"], "nt", [(F32, "n")], nm("dckvn"), dep=token)[0]
    dcq, g["q_norm_g"] = _rms_bwd(dcqn, h, off["cq"], seg.width["cq"], sv["rq"], P["q_norm_g"], nm("rmsq_bwd"))
    dckv, g["kv_norm_g"] = _rms_bwd(dckvn, h, off["ckv"], seg.width["ckv"], sv["rkv"], P["kv_norm_g"], nm("rmskv_bwd"))
    assert seg.order[-1] == "kr" and seg.n_pad - seg.used >= LANES - MLA_ROPE
    parts = {"g": jnp.concatenate([dlog[0], dlog[1], dlog[2]], axis=1), "qa": dqa, "hu": dhu, "hv": dhv, "cq": dcq, "ckv": dckv,
             "ka": dka.astype(BF16), "va": dva.astype(BF16), "kr": dkr16}
    dh = jnp.concatenate([parts[k] for k in seg.order] + [jnp.zeros((S, seg.n_pad - seg.used - (LANES - MLA_ROPE)), BF16)],
                         axis=1)
    token = emit(l, "a", {"w_in": _dw_chunks("w_in", sv["x16"], dh, nm("dw_in"), chunker=seg.to_shards)})
    dx = _mm_axpy(dh, W["w_in"], "nt", dr1, alpha, nm("dx"), dep=token)
    return dx, g


BIG = ("w_in", "w_uq", "w_ukv", "w_proj_a", "w_proj_b", "w_proj_c", "w_o", "w_up", "w_down")
ROW_SHARDED = ("w_proj_b", "w_o", "w_down")
SHARDED_F32 = ("b_gate", "conv_w")
REPLICATED = ("sinks", "q_norm_g", "kv_norm_g", "sgu_ln_g", "sgu_ln_b", "sgu_w", "sgu_b", "ln1_g", "ln1_b", "conv_b", "ln2_g",
              "ln2_b")
WEIGHTS = ("w_in", "b_gate", "sinks", "q_norm_g", "kv_norm_g", "w_uq", "w_ukv", "sgu_ln_g", "sgu_ln_b", "sgu_w", "sgu_b",
           "w_proj_a", "w_proj_b", "w_proj_c", "w_o", "ln1_g", "ln1_b", "w_up", "conv_w", "conv_b", "w_down", "ln2_g", "ln2_b")


def _step_local(x, positions, target, small, rq, rkv, fetch, emit, token=None):
    S, D = x.shape
    L = small["sinks"].shape[0]
    alpha = (2 * L) ** 0.25
    seg = _Seg(D, rq, rkv)
    inv_freq = ROPE_THETA ** (-jnp.arange(0, MLA_ROPE, 2, dtype=F32) / MLA_ROPE)
    ang = positions.astype(F32)[:, None] * inv_freq
    reps = LANES // (MLA_ROPE // 2)
    cs, sn = jnp.tile(jnp.cos(ang), (1, reps)), jnp.tile(jnp.sin(ang), (1, reps))
    rows = ("q_norm_g", "kv_norm_g", "sgu_ln_g", "sgu_ln_b", "ln1_g", "ln1_b", "conv_b", "ln2_g", "ln2_b")
    layers = [{k: small[k][l].reshape(1, -1) if k in rows else small[k][l] for k in small} for l in range(L)]
    saved = []
    x16 = _after(x, token).astype(BF16)
    for l in range(L):
        x, x16, sv = _layer_fwd(l, x, x16, fetch, layers[l], cs, sn, seg, alpha)
        saved.append(sv)
    loss, dx = _loss(x, target, "loss")
    grads = [None] * L
    for l in reversed(range(L)):
        dx, grads[l] = _layer_bwd(l, dx, saved[l], layers[l], cs, sn, seg, alpha, emit)
    out = {k: jnp.stack([grads[l][k].reshape(small[k].shape[1:]) for l in range(L)]) for k in small}
    return loss, dx, out


def _unshard(k, gathered):
    n, r, c = gathered.shape
    if k in ROW_SHARDED:
        return gathered.reshape(n * r, c)
    return gathered.transpose(1, 0, 2).reshape(r, n * c)


def _to_chunks(k, gfull):
    r, c = gfull.shape
    if k in ROW_SHARDED:
        return gfull.reshape(N_DEV, r // N_DEV, c)
    return gfull.reshape(r, N_DEV, c // N_DEV).transpose(1, 0, 2)


def _pack(arrs):
    P = arrs[0].shape[0]
    flat = jnp.concatenate([a.reshape(P, -1) for a in arrs], axis=1)
    sizes = [(a.size // P, a.size // P) for a in arrs]
    flat = jnp.pad(flat, ((0, 0), (0, -flat.shape[1] % (SUBLANES * 128))))
    return flat.reshape(P, -1, 128), sizes


def _unpack(packed, sizes, shapes):
    flat = packed.reshape(-1)
    out, o = [], 0
    for (n, npad), shp in zip(sizes, shapes):
        out.append(flat[o:o + n].reshape(shp))
        o += npad
    return out


def kernel(x, positions, w_in, b_gate, sinks, q_norm_g, kv_norm_g, w_uq, w_ukv, sgu_ln_g, sgu_ln_b, sgu_w, sgu_b, w_proj_a, w_proj_b, w_proj_c, w_o, ln1_g, ln1_b, w_up, conv_w, conv_b, w_down, ln2_g, ln2_b, loss_target, m_w_in, m_b_gate, m_sinks, m_q_norm_g, m_kv_norm_g, m_w_uq, m_w_ukv, m_sgu_ln_g, m_sgu_ln_b, m_sgu_w, m_sgu_b, m_w_proj_a, m_w_proj_b, m_w_proj_c, m_w_o, m_ln1_g, m_ln1_b, m_w_up, m_conv_w, m_conv_b, m_w_down, m_ln2_g, m_ln2_b, v_w_in, v_b_gate, v_sinks, v_q_norm_g, v_kv_norm_g, v_w_uq, v_w_ukv, v_sgu_ln_g, v_sgu_ln_b, v_sgu_w, v_sgu_b, v_w_proj_a, v_w_proj_b, v_w_proj_c, v_w_o, v_ln1_g, v_ln1_b, v_w_up, v_conv_w, v_conv_b, v_w_down, v_ln2_g, v_ln2_b):
    given = dict(locals())
    w = {k: given[k] for k in WEIGHTS}
    mom = {k: given["m_" + k] for k in WEIGHTS}
    var = {k: given["v_" + k] for k in WEIGHTS}

    L = w_in.shape[0]
    order = [(l, grp) for l in range(L) for grp in ("a", "b", "c")]

    first = [w[k][0].astype(BF16) for k in GROUPS["a"]]
    first_lands = _gather_two_level(first, "gather_first")
    gathers, token = {}, first_lands[0]
    for l, grp in order[1:]:
        srcs = [w[k][l].astype(BF16) if k in BIG else w[k][l] for k in GROUPS[grp]]
        gathers[l, grp], token = _exchange_start(srcs, False, token, f"gather_start_l{l}{grp}")

    me = 4 * lax.axis_index("x") + 2 * lax.axis_index("y") + lax.axis_index("c")
    mine = (jnp.arange(N_DEV) == me)[:, None, None]

    def fetch(l, grp, after):
        if (l, grp) == order[0]:
            srcs, lands = first, first_lands
        else:
            srcs, lands = _exchange_wait(gathers[l, grp], after, f"gather_wait_l{l}{grp}")
        full = {k: jnp.where(mine, srcs[i][None], lands[i]) for i, k in enumerate(GROUPS[grp])}
        return {k: v if k == "w_in" else _unshard(k, v) for k, v in full.items()}

    scatters = {}

    def emit(l, grp, chunks):
        scatters[l, grp], tok = _exchange_start([chunks[k] for k in GROUPS[grp]], True, None, f"scatter_start_l{l}{grp}")
        return tok

    small = {k: w[k] for k in REPLICATED}
    loss, grad_x, g = _step_local(x[0], positions[0], loss_target[0], small, w_uq.shape[1], w_ukv.shape[1], fetch, emit, token)
    loss = lax.psum(loss[0, 0], AXES)

    packed, sizes = _pack([g[k][None] for k in REPLICATED])
    small_grads, after = _exchange_start([packed[0]], False, grad_x, "gather_small_grads_start")

    me1 = me.astype(jnp.int32).reshape(1)
    res = {}
    for grp in ("c", "b", "a"):
        slots, own = {}, {}
        for l in reversed(range(L)):
            srcs, lands = _exchange_wait(scatters[l, grp], after, f"scatter_wait_l{l}{grp}")
            for i, k in enumerate(GROUPS[grp]):
                slots[k, l], own[k, l] = lands[i], srcs[i]
        for k in GROUPS[grp]:
            res[k] = _adamw([slots[k, l] for l in range(L)], [own[k, l] for l in range(L)], me1, w[k], mom[k], var[k],
                            "adamw_" + k, dep=after)
            after = res[k][1]

    srcs, lands = _exchange_wait(small_grads, after, "gather_small_grads_wait")
    parts = jnp.where(mine, srcs[0][None], lands[0])
    shapes = [w[k].shape for k in REPLICATED]
    pw, _ = _pack([w[k][None] for k in REPLICATED])
    pm, _ = _pack([mom[k][None] for k in REPLICATED])
    pv, _ = _pack([var[k][None] for k in REPLICATED])
    outs = _adamw([parts], None, me1, pw, pm, pv, "adamw_small")
    unpacked = [_unpack(o, sizes, shapes) for o in outs]
    for i, k in enumerate(REPLICATED):
        res[k] = [unpacked[j][i] for j in range(4)]

    return (loss, grad_x[None], *[res[k][0] for k in WEIGHTS], *[res[k][1] for k in WEIGHTS],
            *[res[k][2] for k in WEIGHTS], *[res[k][3] for k in WEIGHTS])
```

```python
import functools
import math

import jax
import jax.numpy as jnp
from jax import lax
from jax.experimental import pallas as pl
from jax.experimental.pallas import tpu as pltpu

F32 = jnp.float32
BF16 = jnp.bfloat16

SWA_Q_HEADS = 16
SWA_KV_HEADS = 2
SWA_HEAD_DIM = 64
SWA_BLOCK = 128
MLA_HEADS = 16
MLA_NOPE = 128
MLA_ROPE = 64
MLA_V = 128
SGU_GROUPS = 8
SGU_DIM = 128
SGU_CHUNK = 128
ROPE_THETA = 10000.0
EPS = 1e-5
MASK = -1e30
ADAM_LR = 0.001
ADAM_B1 = 0.9
ADAM_B2 = 0.999
ADAM_EPS = 1e-08
ADAM_WD = 0.01
ADAM_STEP = 10

N_DEV = 8
AXES = ("x", "y", "c")
VMEM_LIMIT = 56 * 1024 * 1024
MLA_TILE = 512
MLA_FWD_TILE = 1024
ROW_TILE = 512
CONV_BWD_ROWS = 2048
GLU_ROWS = 1024
MAX_TK = 2816
SUBLANES = 8


def _tile(n, prefs):
    for p in prefs:
        if n % p == 0:
            return p
    return n


def _params(sem):
    return pltpu.CompilerParams(dimension_semantics=sem, vmem_limit_bytes=VMEM_LIMIT)


def _cols(tm, width, off):
    assert off % width == 0, (off, width)
    blk = off // width
    return pl.BlockSpec((tm, width), lambda i, *_: (i, blk))


def _full(shape):
    nd = len(shape)
    return pl.BlockSpec(shape, lambda *_: (0,) * nd)


def _sigmoid(v):
    return 1.0 / (1.0 + jnp.exp(-v))


def _gelu(v):
    return 0.5 * v * (1.0 + lax.erf(v * (2.0 ** -0.5)))


def _gelu_grad(v):
    return 0.5 * (1.0 + lax.erf(v * (2.0 ** -0.5))) + v * jnp.exp(-0.5 * v * v) * (1.0 / math.sqrt(2.0 * math.pi))


_DIMS = {"nn": (((1,), (0,)), ((), ())), "nt": (((1,), (1,)), ((), ())), "tn": (((0,), (0,)), ((), ()))}


def _mm(a, b, mode, outs, name, *, extras=(), epilogue=None, full_n=False, dep=None, chunk=None):
    if mode == "nn":
        (M, K), (K2, N) = a.shape, b.shape
    elif mode == "nt":
        (M, K), (N, K2) = a.shape, b.shape
    else:
        (K, M), (K2, N) = a.shape, b.shape
    assert K == K2, (a.shape, b.shape, mode)
    tm = _tile(M, (1024, 512, 256, 128))
    tn = N if full_n else _tile(N, (1024, 768, 512, 384, 256, 128))
    if full_n:
        tm = _tile(M, (512, 256, 128))
    if chunk is not None:
        tn = chunk if chunk <= 1536 else _tile(chunk, (1024, 768, 512, 384, 256, 128))
        assert N % chunk == 0 and chunk % tn == 0 and tn % 128 == 0, (N, chunk, tn)
    max_tk = MAX_TK // 2 if full_n else MAX_TK
    tk = max(d for d in range(128, min(K, max_tk) + 1, 128) if K % d == 0) if K % 128 == 0 else K
    nk = K // tk
    if mode == "nn":
        a_spec = pl.BlockSpec((tm, tk), lambda i, j, k: (i, k))
        b_spec = pl.BlockSpec((tk, tn), lambda i, j, k: (k, j))
    elif mode == "nt":
        a_spec = pl.BlockSpec((tm, tk), lambda i, j, k: (i, k))
        b_spec = pl.BlockSpec((tn, tk), lambda i, j, k: (j, k))
    else:
        a_spec = pl.BlockSpec((tk, tm), lambda i, j, k: (k, i))
        b_spec = pl.BlockSpec((tk, tn), lambda i, j, k: (k, j))
    in_specs = [a_spec, b_spec]
    for arr, kind in extras:
        if kind == "tile":
            in_specs.append(pl.BlockSpec((tm, tn), lambda i, j, k: (i, j)))
        else:
            in_specs.append(pl.BlockSpec((1, tn), lambda i, j, k: (0, j)))
    out_specs, out_shape = [], []
    for dt, kind in outs:
        if kind == "n" and chunk is not None:
            per = chunk // tn
            out_specs.append(pl.BlockSpec((None, tm, tn), lambda i, j, k: (lax.div(j, per), i, lax.rem(j, per))))
            out_shape.append(jax.ShapeDtypeStruct((N // chunk, M, chunk), dt))
        elif kind == "n":
            out_specs.append(pl.BlockSpec((tm, tn), lambda i, j, k: (i, j)))
            out_shape.append(jax.ShapeDtypeStruct((M, N), dt))
        else:
            assert tn == N
            out_specs.append(pl.BlockSpec((tm, 1), lambda i, j, k: (i, 0)))
            out_shape.append(jax.ShapeDtypeStruct((M, 1), dt))
    ne, no = len(extras), len(outs)
    deps = []
    if dep is not None:
        in_specs.append(_full(dep.shape))
        deps = [dep]
    dims = _DIMS[mode]
    if epilogue is None:
        epilogue = lambda acc: (acc,) * no

    def body(*refs):
        a_ref, b_ref = refs[0], refs[1]
        ex = refs[2:2 + ne]
        out = refs[len(refs) - 1 - no:len(refs) - 1]
        acc = refs[-1]
        k = pl.program_id(2)
        part = lax.dot_general(a_ref[...].astype(BF16), b_ref[...].astype(BF16), dims, preferred_element_type=F32)

        def finish(total):
            res = epilogue(total, *[e[...] for e in ex])
            for o, r in zip(out, res):
                o[...] = r.astype(o.dtype)

        if nk == 1:
            finish(part)
            return

        @pl.when(k == 0)
        def _():
            acc[...] = part

        @pl.when((k > 0) & (k < nk - 1))
        def _():
            acc[...] += part

        @pl.when(k == nk - 1)
        def _():
            finish(acc[...] + part)

    res = pl.pallas_call(
        body, name=name, grid=(M // tm, N // tn, nk), in_specs=in_specs, out_specs=out_specs, out_shape=out_shape,
        scratch_shapes=[pltpu.VMEM((tm, tn), F32)],
        compiler_params=_params(("parallel", "parallel", "arbitrary")),
    )(a, b, *[e[0] for e in extras], *deps)
    return res


def _ln_epilogue(alpha):
    def epi(acc, x, g, b):
        r = alpha * x + acc
        mu = jnp.mean(r, axis=-1, keepdims=True)
        d = r - mu
        var = jnp.mean(d * d, axis=-1, keepdims=True)
        rstd = lax.rsqrt(var + EPS)
        xhat = d * rstd
        y = xhat * g + b
        return y, y, xhat, rstd
    return epi


def _mm_ln(a, w, x, g, b, alpha, name):
    return _mm(a, w, "nn", [(F32, "n"), (BF16, "n"), (F32, "n"), (F32, "1")], name,
               extras=[(x, "tile"), (g, "row"), (b, "row")], epilogue=_ln_epilogue(alpha), full_n=True)


def _mm_axpy(a, w, mode, r, alpha, name, dep=None):
    return _mm(a, w, mode, [(F32, "n")], name, extras=[(r, "tile")],
               epilogue=lambda acc, rv: (acc + alpha * rv,), dep=dep)[0]


def _ln_bwd(dy, xhat, rstd, g, name):
    S, D = dy.shape
    tm = _tile(S, (256, 128))

    def body(dy_ref, xh_ref, rs_ref, g_ref, dr_ref, dr16_ref, dg_ref, db_ref):
        @pl.when(pl.program_id(0) == 0)
        def _():
            dg_ref[...] = jnp.zeros_like(dg_ref)
            db_ref[...] = jnp.zeros_like(db_ref)

        dyv, xh = dy_ref[...], xh_ref[...]
        dxh = dyv * g_ref[...]
        m1 = jnp.mean(dxh, axis=-1, keepdims=True)
        m2 = jnp.mean(dxh * xh, axis=-1, keepdims=True)
        dr = rs_ref[...] * (dxh - m1 - xh * m2)
        dr_ref[...] = dr
        dr16_ref[...] = dr.astype(BF16)
        dg_ref[...] += jnp.sum(dyv * xh, axis=0, keepdims=True)
        db_ref[...] += jnp.sum(dyv, axis=0, keepdims=True)

    row = pl.BlockSpec((tm, D), lambda i: (i, 0))
    return pl.pallas_call(
        body, name=name, grid=(S // tm,),
        in_specs=[row, row, pl.BlockSpec((tm, 1), lambda i: (i, 0)), _full((1, D))],
        out_specs=[row, row, _full((1, D)), _full((1, D))],
        out_shape=[jax.ShapeDtypeStruct((S, D), F32), jax.ShapeDtypeStruct((S, D), BF16),
                   jax.ShapeDtypeStruct((1, D), F32), jax.ShapeDtypeStruct((1, D), F32)],
        compiler_params=_params(("arbitrary",)),
    )(dy, xhat, rstd, g)


def _rms_fwd(h, off, width, g, name):
    S = h.shape[0]
    tm = _tile(S, (ROW_TILE, 256, 128))

    def body(c_ref, g_ref, y_ref, r_ref):
        c = c_ref[...]
        r = lax.rsqrt(jnp.mean(c * c, axis=-1, keepdims=True) + EPS)
        y_ref[...] = (c * r * g_ref[...]).astype(BF16)
        r_ref[...] = r

    return pl.pallas_call(
        body, name=name, grid=(S // tm,),
        in_specs=[_cols(tm, width, off), _full((1, width))],
        out_specs=[pl.BlockSpec((tm, width), lambda i: (i, 0)), pl.BlockSpec((tm, 1), lambda i: (i, 0))],
        out_shape=[jax.ShapeDtypeStruct((S, width), BF16), jax.ShapeDtypeStruct((S, 1), F32)],
        compiler_params=_params(("parallel",)),
    )(h, g)


def _rms_bwd(dy, h, off, width, rstd, g, name):
    S = h.shape[0]
    tm = _tile(S, (ROW_TILE, 256, 128))

    def body(dy_ref, c_ref, r_ref, g_ref, dc_ref, dg_ref):
        @pl.when(pl.program_id(0) == 0)
        def _():
            dg_ref[...] = jnp.zeros_like(dg_ref)

        dyv, c, r = dy_ref[...], c_ref[...], r_ref[...]
        dyg = dyv * g_ref[...]
        m = jnp.mean(dyg * c, axis=-1, keepdims=True)
        dc_ref[...] = (r * dyg - c * (r * r * r) * m).astype(BF16)
        dg_ref[...] += jnp.sum(dyv * c * r, axis=0, keepdims=True)

    return pl.pallas_call(
        body, name=name, grid=(S // tm,),
        in_specs=[pl.BlockSpec((tm, width), lambda i: (i, 0)), _cols(tm, width, off),
                  pl.BlockSpec((tm, 1), lambda i: (i, 0)), _full((1, width))],
        out_specs=[pl.BlockSpec((tm, width), lambda i: (i, 0)), _full((1, width))],
        out_shape=[jax.ShapeDtypeStruct((S, width), BF16), jax.ShapeDtypeStruct((1, width), F32)],
        compiler_params=_params(("arbitrary",)),
    )(dy, h, rstd, g)


def _loss(y, target, name):
    S, D = y.shape
    tm = _tile(S, (256, 128))

    def body(y_ref, t_ref, l_ref, dy_ref):
        @pl.when(pl.program_id(0) == 0)
        def _():
            l_ref[...] = jnp.zeros_like(l_ref)

        err = y_ref[...] - t_ref[...]
        dy_ref[...] = err * (1.0 / D)
        per_tok = jnp.mean(err * err, axis=-1, keepdims=True)
        l_ref[...] += 0.5 * jnp.sum(per_tok, axis=0, keepdims=True)

    row = pl.BlockSpec((tm, D), lambda i: (i, 0))
    return pl.pallas_call(
        body, name=name, grid=(S // tm,), in_specs=[row, row], out_specs=[_full((1, 1)), row],
        out_shape=[jax.ShapeDtypeStruct((1, 1), F32), jax.ShapeDtypeStruct((S, D), F32)],
        compiler_params=_params(("arbitrary",)),
    )(y, target)


LANES = 128
LOG2E = 1.4426950408889634


def _rope(x, off, width, cs2, sn2, bwd, name):
    S = cs2.shape[0]
    tm = _tile(S, (ROW_TILE, 256, 128))
    stacked = x.ndim == 3
    half = MLA_ROPE // 2
    assert width % LANES == 0 and MLA_ROPE * 2 == LANES

    def rot(v):
        lane = lax.broadcasted_iota(jnp.int32, v.shape, 1)
        return jnp.where((lane & (MLA_ROPE - 1)) < half, -pltpu.roll(v, LANES - half, 1), pltpu.roll(v, half, 1))

    def body(x_ref, c_ref, s_ref, y_ref):
        c, s = c_ref[...], s_ref[...]
        for g in range(width // LANES):
            cols = slice(g * LANES, (g + 1) * LANES)
            v = jnp.sum(x_ref[...], axis=0) if stacked else x_ref[:, cols].astype(F32)
            y = v * c - rot(v * s) if bwd else v * c + rot(v) * s
            y_ref[:, cols] = y.astype(BF16)

    row = pl.BlockSpec((tm, LANES), lambda i: (i, 0))
    x_spec = pl.BlockSpec((x.shape[0], tm, LANES), lambda i: (0, i, 0)) if stacked else _cols(tm, width, off)
    return pl.pallas_call(
        body, name=name, grid=(S // tm,), in_specs=[x_spec, row, row],
        out_specs=pl.BlockSpec((tm, width), lambda i: (i, 0)), out_shape=jax.ShapeDtypeStruct((S, width), BF16),
        compiler_params=_params(("parallel",)),
    )(x, cs2, sn2)


def _swa_mask(n, rows):
    blk = SWA_BLOCK
    row = lax.broadcasted_iota(jnp.int32, (rows, 2 * blk), 0) & (blk - 1)
    col = lax.broadcasted_iota(jnp.int32, (rows, 2 * blk), 1)
    rel = row + blk - col
    return (rel >= 0) & (rel < blk) & ((n > 0) | (col >= blk))


def _swa_specs(off_q, off_k, off_v, stacked):
    blk, aq, akv = SWA_BLOCK, SWA_Q_HEADS * SWA_HEAD_DIM, SWA_KV_HEADS * SWA_HEAD_DIM
    grp = SWA_Q_HEADS // SWA_KV_HEADS
    assert off_q % aq == 0 and off_k % akv == 0 and off_v % akv == 0 and blk & (blk - 1) == 0
    prev = lambda off: pl.BlockSpec((blk, akv), lambda n: (jnp.maximum(n - 1, 0), off // akv))
    cur = lambda off: pl.BlockSpec((blk, akv), lambda n: (n, off // akv))
    sink = _full((SWA_KV_HEADS, grp * blk, 1)) if stacked else pl.BlockSpec(memory_space=pltpu.SMEM)
    return [sink, _cols(blk, aq, off_q), prev(off_k), cur(off_k), prev(off_v), cur(off_v)]


def _swa_sinks(sinks):
    grp = SWA_Q_HEADS // SWA_KV_HEADS
    return jnp.repeat(sinks.reshape(SWA_KV_HEADS, grp), SWA_BLOCK, axis=1)[:, :, None]


def _swa_stack(x, kv):
    hd, grp = SWA_HEAD_DIM, SWA_Q_HEADS // SWA_KV_HEADS
    return jnp.concatenate([x[:, (kv * grp + g) * hd:(kv * grp + g + 1) * hd] for g in range(grp)], axis=0)


def _swa_fwd(h, off_q, off_k, off_v, sinks, name):
    S = h.shape[0]
    blk, hd, nh, nkv = SWA_BLOCK, SWA_HEAD_DIM, SWA_Q_HEADS, SWA_KV_HEADS
    grp = nh // nkv
    aq = nh * hd
    scale = hd ** -0.5

    def body(sink_ref, q_ref, kp_ref, kc_ref, vp_ref, vc_ref, o_ref, lse_ref):
        valid = _swa_mask(pl.program_id(0), blk)
        q = q_ref[...].astype(BF16)
        k2 = jnp.concatenate([kp_ref[...], kc_ref[...]], axis=0).astype(BF16)
        v2 = jnp.concatenate([vp_ref[...], vc_ref[...]], axis=0).astype(BF16)
        for hh in range(nh):
            kv = hh // grp
            qh = q[:, hh * hd:(hh + 1) * hd]
            kh = k2[:, kv * hd:(kv + 1) * hd]
            vh = v2[:, kv * hd:(kv + 1) * hd]
            s = lax.dot_general(qh, kh, _DIMS["nt"], preferred_element_type=F32) * scale
            s = jnp.where(valid, s, MASK)
            sk = sink_ref[hh]
            m = jnp.maximum(jnp.max(s, axis=1, keepdims=True), sk)
            p = jnp.exp(s - m)
            l = jnp.sum(p, axis=1, keepdims=True) + jnp.exp(sk - m)
            o_ref[:, hh * hd:(hh + 1) * hd] = jnp.dot((p / l).astype(BF16), vh, preferred_element_type=F32).astype(BF16)
            lse_ref[:, hh:hh + 1] = m + jnp.log(l)

    return pl.pallas_call(
        body, name=name, grid=(S // blk,), in_specs=_swa_specs(off_q, off_k, off_v, False),
        out_specs=[pl.BlockSpec((blk, aq), lambda n: (n, 0)), pl.BlockSpec((blk, nh), lambda n: (n, 0))],
        out_shape=[jax.ShapeDtypeStruct((S, aq), BF16), jax.ShapeDtypeStruct((S, nh), F32)],
        compiler_params=_params(("parallel",)),
    )(sinks, h, h, h, h, h)


def _swa_bwd(h, off_q, off_k, off_v, sinks, dout, lse, name):
    S = h.shape[0]
    blk, hd, nh, nkv = SWA_BLOCK, SWA_HEAD_DIM, SWA_Q_HEADS, SWA_KV_HEADS
    grp = nh // nkv
    aq, akv = nh * hd, nkv * hd
    scale = hd ** -0.5

    def body(sink_ref, q_ref, kp_ref, kc_ref, vp_ref, vc_ref, do_ref, lse_ref, dq_ref, dk_ref, dv_ref, ds_ref):
        n = pl.program_id(0)

        @pl.when(n == 0)
        def _():
            dk_ref[...] = jnp.zeros_like(dk_ref)
            dv_ref[...] = jnp.zeros_like(dv_ref)
            ds_ref[...] = jnp.zeros_like(ds_ref)

        valid = _swa_mask(n, grp * blk)
        q = q_ref[...].astype(BF16)
        k2 = jnp.concatenate([kp_ref[...], kc_ref[...]], axis=0).astype(BF16)
        v2 = jnp.concatenate([vp_ref[...], vc_ref[...]], axis=0).astype(BF16)
        do = do_ref[...]
        lane = lax.broadcasted_iota(jnp.int32, (1, 128), 1)
        dsink = jnp.zeros((1, 128), F32)
        cur = pl.ds(pl.multiple_of(n * blk, blk), blk)
        prev = pl.ds(pl.multiple_of(jnp.maximum(n - 1, 0) * blk, blk), blk)
        for kv in range(nkv):
            kh = k2[:, kv * hd:(kv + 1) * hd]
            vh = v2[:, kv * hd:(kv + 1) * hd]
            qs = _swa_stack(q, kv)
            dos = _swa_stack(do, kv)
            dos16 = dos.astype(BF16)
            lse = jnp.concatenate([lse_ref[:, kv * grp + g:kv * grp + g + 1] for g in range(grp)], axis=0)
            s = lax.dot_general(qs, kh, _DIMS["nt"], preferred_element_type=F32) * scale
            s = jnp.where(valid, s, MASK)
            p = jnp.exp(s - lse)
            p16 = p.astype(BF16)
            o = jnp.dot(p16, vh, preferred_element_type=F32)
            delta = jnp.sum(dos * o, axis=1, keepdims=True)
            dp = lax.dot_general(dos16, vh, _DIMS["nt"], preferred_element_type=F32)
            ds16 = (p * (dp - delta) * scale).astype(BF16)
            dqs = jnp.dot(ds16, kh, preferred_element_type=F32).astype(BF16)
            dk_acc = lax.dot_general(ds16, qs, _DIMS["tn"], preferred_element_type=F32)
            dv_acc = lax.dot_general(p16, dos16, _DIMS["tn"], preferred_element_type=F32)
            dsk = jnp.exp(sink_ref[kv] - lse) * delta
            for g in range(grp):
                hh = kv * grp + g
                dq_ref[:, hh * hd:(hh + 1) * hd] = dqs[g * blk:(g + 1) * blk]
                dsink += jnp.where(lane == hh, -jnp.sum(dsk[g * blk:(g + 1) * blk], axis=0, keepdims=True), 0.0)
            cols = slice(kv * hd, (kv + 1) * hd)
            dk_ref[cur, cols] += dk_acc[blk:]
            dv_ref[cur, cols] += dv_acc[blk:]

            @pl.when(n > 0)
            def _():
                dk_ref[prev, cols] += dk_acc[:blk]
                dv_ref[prev, cols] += dv_acc[:blk]

        ds_ref[...] += dsink

    return pl.pallas_call(
        body, name=name, grid=(S // blk,),
        in_specs=_swa_specs(off_q, off_k, off_v, True) + [pl.BlockSpec((blk, aq), lambda n: (n, 0)),
                                                    pl.BlockSpec((blk, nh), lambda n: (n, 0))],
        out_specs=[pl.BlockSpec((blk, aq), lambda n: (n, 0)), _full((S, akv)), _full((S, akv)), _full((1, 128))],
        out_shape=[jax.ShapeDtypeStruct((S, aq), BF16), jax.ShapeDtypeStruct((S, akv), F32),
                   jax.ShapeDtypeStruct((S, akv), F32), jax.ShapeDtypeStruct((1, 128), F32)],
        compiler_params=_params(("arbitrary",)),
    )(_swa_sinks(sinks), h, h, h, h, h, dout, lse)


def _causal(i, j, t):
    row = i * t + lax.broadcasted_iota(jnp.int32, (t, t), 0)
    col = j * t + lax.broadcasted_iota(jnp.int32, (t, t), 1)
    return col <= row


def _pair_rope(qr, hh):
    lane = lax.broadcasted_iota(jnp.int32, qr.shape, 1)
    return jnp.where(lane < MLA_ROPE, qr if hh == 0 else pltpu.roll(qr, MLA_ROPE, 1), jnp.zeros_like(qr))


def _mla_fwd(qf, qr, kvf, kr, name):
    S = qr.shape[0]
    H, dn, dv = MLA_HEADS, MLA_NOPE, MLA_V
    assert H % 2 == 0 and dn == LANES and dv == LANES and 2 * MLA_ROPE == LANES
    t = _tile(S, (MLA_FWD_TILE, 512, 256, 128))
    nq = S // t
    scale = (MLA_NOPE + MLA_ROPE) ** -0.5

    def body(qn_ref, qr_ref, kn_ref, kr_ref, v_ref, o_ref, o16_ref, lse_ref, *state):
        i = pl.program_id(1)
        for hh in range(2):
            m_s, l_s, acc_s = state[3 * hh:3 * hh + 3]
            m_s[...] = jnp.full_like(m_s, -jnp.inf)
            l_s[...] = jnp.zeros_like(l_s)
            acc_s[...] = jnp.zeros_like(acc_s)

        def block(j, masked):
            rows = pl.ds(pl.multiple_of(j * t, t), t)
            for hh in range(2):
                m_s, l_s, acc_s = state[3 * hh:3 * hh + 3]
                cols = slice(hh * LANES, (hh + 1) * LANES)
                q = jnp.concatenate([qn_ref[:, cols].astype(BF16), _pair_rope(qr_ref[...], hh)], axis=1)
                k = jnp.concatenate([kn_ref[rows, cols], kr_ref[rows, :]], axis=1)
                s = lax.dot_general(q, k, _DIMS["nt"], preferred_element_type=F32) * (scale * LOG2E)
                if masked:
                    s = jnp.where(_causal(0, 0, t), s, MASK)
                m_old = m_s[...]
                m_new = jnp.maximum(m_old, jnp.max(s, axis=1, keepdims=True))
                alpha = jnp.exp2(m_old - m_new)
                p = jnp.exp2(s - m_new)
                l_s[...] = alpha * l_s[...] + jnp.sum(p, axis=1, keepdims=True)
                acc_s[...] = alpha * acc_s[...] + jnp.dot(p.astype(BF16), v_ref[rows, cols], preferred_element_type=F32)
                m_s[...] = m_new

        def full_block(j, carry):
            block(j, False)
            return carry

        lax.fori_loop(0, i, full_block, 0)
        block(i, True)
        for hh in range(2):
            m_s, l_s, acc_s = state[3 * hh:3 * hh + 3]
            out = acc_s[...] / l_s[...]
            o_ref[:, hh * LANES:(hh + 1) * LANES] = out
            o16_ref[:, hh * LANES:(hh + 1) * LANES] = out.astype(BF16)
            lse_ref[hh] = (m_s[...] + jnp.log2(l_s[...])) * (1.0 / LOG2E)

    P = H // 2
    return pl.pallas_call(
        body, name=name, grid=(P, nq),
        in_specs=[pl.BlockSpec((t, 2 * LANES), lambda p, i: (i, p)), pl.BlockSpec((t, LANES), lambda p, i: (i, p)),
                  pl.BlockSpec((S, 2 * LANES), lambda p, i: (0, p)), pl.BlockSpec((S, LANES), lambda p, i: (0, 0)),
                  pl.BlockSpec((S, 2 * LANES), lambda p, i: (0, P + p))],
        out_specs=[pl.BlockSpec((t, 2 * LANES), lambda p, i: (i, p)), pl.BlockSpec((t, 2 * LANES), lambda p, i: (i, p)),
                   pl.BlockSpec((2, t, 1), lambda p, i: (p, i, 0))],
        out_shape=[jax.ShapeDtypeStruct((S, H * dv), F32), jax.ShapeDtypeStruct((S, H * dv), BF16),
                   jax.ShapeDtypeStruct((H, S, 1), F32)],
        scratch_shapes=[pltpu.VMEM((t, 1), F32), pltpu.VMEM((t, 1), F32), pltpu.VMEM((t, dv), F32)] * 2,
        compiler_params=_params(("parallel", "arbitrary")),
    )(qf, qr, kvf, kr, kvf)


def _mla_bwd(qf, qr, kvf, kr, do, out, lse, name):
    S = qr.shape[0]
    H, dv = MLA_HEADS, MLA_V
    P = H // 2
    t = _tile(S, (MLA_TILE, 256, 128))
    nq = S // t
    scale = (MLA_NOPE + MLA_ROPE) ** -0.5

    def body(qn_ref, qr_ref, kn_ref, kr_ref, v_ref, do_ref, o_ref, lse_ref, dqn_ref, dqr_ref, dkn_ref, dv_ref, dkr_ref,
             dkn_s, dv_s, dkr_s):
        j, i = pl.program_id(1), pl.program_id(2)

        @pl.when((j == 0) & (i == 0))
        def _():
            dqn_ref[...] = jnp.zeros_like(dqn_ref)
            dqr_ref[...] = jnp.zeros_like(dqr_ref)

        @pl.when(i == j)
        def _():
            dkn_s[...] = jnp.zeros_like(dkn_s)
            dv_s[...] = jnp.zeros_like(dv_s)
            dkr_s[...] = jnp.zeros_like(dkr_s)

        def block(masked):
            rows = pl.ds(pl.multiple_of(i * t, t), t)
            krv = kr_ref[...]
            for hh in range(2):
                cols = slice(hh * LANES, (hh + 1) * LANES)
                q = jnp.concatenate([qn_ref[:, cols].astype(BF16), _pair_rope(qr_ref[...], hh)], axis=1)
                k = jnp.concatenate([kn_ref[:, cols], krv], axis=1)
                vv, dof = v_ref[:, cols], do_ref[:, cols]
                dov = dof.astype(BF16)
                delta = jnp.sum(dof * o_ref[:, cols], axis=1, keepdims=True)
                s = lax.dot_general(q, k, _DIMS["nt"], preferred_element_type=F32) * (scale * LOG2E)
                if masked:
                    s = jnp.where(_causal(0, 0, t), s, MASK)
                p = jnp.exp2(s - lse_ref[hh] * LOG2E)
                p16 = p.astype(BF16)
                dp = lax.dot_general(dov, vv, _DIMS["nt"], preferred_element_type=F32)
                ds16 = (p * (dp - delta) * scale).astype(BF16)
                dv_s[:, cols] += lax.dot_general(p16, dov, _DIMS["tn"], preferred_element_type=F32)
                dk = lax.dot_general(ds16, q, _DIMS["tn"], preferred_element_type=F32)
                dkn_s[:, cols] += dk[:, :LANES]
                dkr_s[...] += dk[:, LANES:]
                dq = jnp.dot(ds16, k, preferred_element_type=F32)
                dqn_ref[rows, cols] += dq[:, :LANES]
                dqr = dq[:, LANES:]
                dqr_ref[rows, :] += dqr if hh == 0 else pltpu.roll(dqr, MLA_ROPE, 1)

        @pl.when(i == j)
        def _():
            block(True)

        @pl.when(i > j)
        def _():
            block(False)

        @pl.when(i == nq - 1)
        def _():
            dkn_ref[...] = dkn_s[...]
            dv_ref[...] = dv_s[...]
            dkr_ref[...] = dkr_s[...]

    qi = lambda i, j: jnp.maximum(i, j)
    return pl.pallas_call(
        body, name=name, grid=(P, nq, nq),
        in_specs=[pl.BlockSpec((t, 2 * LANES), lambda p, j, i: (qi(i, j), p)), pl.BlockSpec((t, LANES), lambda p, j, i: (qi(i, j), p)),
                  pl.BlockSpec((t, 2 * LANES), lambda p, j, i: (j, p)), pl.BlockSpec((t, LANES), lambda p, j, i: (j, 0)),
                  pl.BlockSpec((t, 2 * LANES), lambda p, j, i: (j, P + p)),
                  pl.BlockSpec((t, 2 * LANES), lambda p, j, i: (qi(i, j), p)),
                  pl.BlockSpec((t, 2 * LANES), lambda p, j, i: (qi(i, j), p)),
                  pl.BlockSpec((2, t, 1), lambda p, j, i: (p, qi(i, j), 0))],
        out_specs=[pl.BlockSpec((S, 2 * LANES), lambda p, j, i: (0, p)), pl.BlockSpec((S, LANES), lambda p, j, i: (0, p)),
                   pl.BlockSpec((t, 2 * LANES), lambda p, j, i: (j, p)), pl.BlockSpec((t, 2 * LANES), lambda p, j, i: (j, p)),
                   pl.BlockSpec((None, t, LANES), lambda p, j, i: (p, j, 0))],
        out_shape=[jax.ShapeDtypeStruct((S, H * MLA_NOPE), F32), jax.ShapeDtypeStruct((S, H * MLA_ROPE), F32),
                   jax.ShapeDtypeStruct((S, H * MLA_NOPE), F32), jax.ShapeDtypeStruct((S, H * dv), F32),
                   jax.ShapeDtypeStruct((P, S, LANES), F32)],
        scratch_shapes=[pltpu.VMEM((t, 2 * LANES), F32), pltpu.VMEM((t, 2 * LANES), F32), pltpu.VMEM((t, LANES), F32)],
        compiler_params=_params(("parallel", "arbitrary", "arbitrary")),
    )(qf, qr, kvf, kr, kvf, do, out, lse)


def _sgu_norm(hv, lg, lb):
    vg = _gelu(hv)
    mu = jnp.mean(vg, axis=-1, keepdims=True)
    d = vg - mu
    rstd = lax.rsqrt(jnp.mean(d * d, axis=-1, keepdims=True) + EPS)
    xhat = d * rstd
    return xhat, rstd, xhat * lg + lb


def _sgu_fwd(h, off_u, off_v, lg, lb, w16, bt, name):
    S = h.shape[0]
    T, G, C = SGU_CHUNK, SGU_GROUPS, SGU_DIM
    W = G * C

    def body(hu_ref, hv_ref, lg_ref, lb_ref, w_ref, bt_ref, y_ref):
        u = _gelu(hu_ref[...])
        _, _, vn = _sgu_norm(hv_ref[...], lg_ref[...], lb_ref[...])
        vn16 = vn.astype(BF16)
        for g in range(G):
            cols = slice(g * C, (g + 1) * C)
            mixed = jnp.dot(w_ref[g], vn16[:, cols], preferred_element_type=F32) + bt_ref[:, g:g + 1]
            y_ref[:, cols] = (u[:, cols] * mixed).astype(BF16)

    return pl.pallas_call(
        body, name=name, grid=(S // T,),
        in_specs=[_cols(T, W, off_u), _cols(T, W, off_v), _full((1, W)), _full((1, W)), _full((G, T, T)), _full((T, G))],
        out_specs=pl.BlockSpec((T, W), lambda n: (n, 0)),
        out_shape=jax.ShapeDtypeStruct((S, W), BF16),
        compiler_params=_params(("parallel",)),
    )(h, h, lg, lb, w16, bt)


def _sgu_bwd(h, off_u, off_v, lg, lb, w16, bt, dy, name):
    S = h.shape[0]
    T, G, C = SGU_CHUNK, SGU_GROUPS, SGU_DIM
    W = G * C
    nc = S // T

    def body(hu_ref, hv_ref, lg_ref, lb_ref, w_ref, bt_ref, dy_ref, dhu_ref, dhv_ref, dw_ref, db_ref, dlg_ref, dlb_ref,
             dmix_s, dvn_s):
        n = pl.program_id(0)

        @pl.when(n == 0)
        def _():
            dw_ref[...] = jnp.zeros_like(dw_ref)
            dlg_ref[...] = jnp.zeros_like(dlg_ref)
            dlb_ref[...] = jnp.zeros_like(dlb_ref)
            dmix_s[...] = jnp.zeros_like(dmix_s)

        hu, hv, lgv = hu_ref[...], hv_ref[...], lg_ref[...]
        u = _gelu(hu)
        xhat, rstd, vn = _sgu_norm(hv, lgv, lb_ref[...])
        vn16 = vn.astype(BF16)
        dyv = dy_ref[...]
        dmixed = dyv * u
        dmix_s[...] += dmixed
        dmixed16 = dmixed.astype(BF16)
        for g in range(G):
            cols = slice(g * C, (g + 1) * C)
            mixed = jnp.dot(w_ref[g], vn16[:, cols], preferred_element_type=F32) + bt_ref[:, g:g + 1]
            dhu_ref[:, cols] = (dyv[:, cols] * mixed * _gelu_grad(hu[:, cols])).astype(BF16)
            dvn_s[:, cols] = lax.dot_general(w_ref[g], dmixed16[:, cols], _DIMS["tn"], preferred_element_type=F32)
            dw_ref[g] += lax.dot_general(dmixed16[:, cols], vn16[:, cols], _DIMS["nt"], preferred_element_type=F32)
        dvn = dvn_s[...]
        dlg_ref[...] += jnp.sum(dvn * xhat, axis=0, keepdims=True)
        dlb_ref[...] += jnp.sum(dvn, axis=0, keepdims=True)
        dxh = dvn * lgv
        m1 = jnp.mean(dxh, axis=-1, keepdims=True)
        m2 = jnp.mean(dxh * xhat, axis=-1, keepdims=True)
        dvg = rstd * (dxh - m1 - xhat * m2)
        dhv_ref[...] = (dvg * _gelu_grad(hv)).astype(BF16)

        @pl.when(n == nc - 1)
        def _():
            tril = lax.broadcasted_iota(jnp.int32, (T, T), 1) <= lax.broadcasted_iota(jnp.int32, (T, T), 0)
            lane = lax.broadcasted_iota(jnp.int32, (T, 128), 1)
            db = jnp.zeros((T, 128), F32)
            for g in range(G):
                dw_ref[g] = jnp.where(tril, dw_ref[g], 0.0)
                db += jnp.where(lane == g, jnp.sum(dmix_s[:, g * C:(g + 1) * C], axis=1, keepdims=True), 0.0)
            db_ref[...] = db

    row = pl.BlockSpec((T, W), lambda n: (n, 0))
    return pl.pallas_call(
        body, name=name, grid=(nc,),
        in_specs=[_cols(T, W, off_u), _cols(T, W, off_v), _full((1, W)), _full((1, W)), _full((G, T, T)), _full((T, G)), row],
        out_specs=[row, row, _full((G, T, T)), _full((T, 128)), _full((1, W)), _full((1, W))],
        out_shape=[jax.ShapeDtypeStruct((S, W), BF16), jax.ShapeDtypeStruct((S, W), BF16),
                   jax.ShapeDtypeStruct((G, T, T), F32), jax.ShapeDtypeStruct((T, 128), F32),
                   jax.ShapeDtypeStruct((1, W), F32), jax.ShapeDtypeStruct((1, W), F32)],
        scratch_shapes=[pltpu.VMEM((T, W), F32), pltpu.VMEM((T, W), F32)],
        compiler_params=_params(("arbitrary",)),
    )(h, h, lg, lb, w16, bt, dy)


def _merge_fwd(ys, ps, h, bg, name):
    S = h.shape[0]
    D = ps[0].shape[1]
    tm = _tile(S, (1024, 512, 256, 128))
    tn = _tile(D, (256, 128))
    nb = len(ys)

    def body(*refs):
        y_refs, p_refs, l_refs = refs[:nb], refs[nb:2 * nb], refs[2 * nb:3 * nb]
        bg_ref, mg_ref, z_ref = refs[3 * nb:]
        acc = jnp.zeros((tm, tn), F32)
        for b in range(nb):
            z = jnp.dot(y_refs[b][...].astype(BF16), p_refs[b][...], preferred_element_type=F32)
            z_ref[b] = z
            acc += _sigmoid(l_refs[b][...] + bg_ref[b:b + 1, :]) * z
        mg_ref[...] = acc.astype(BF16)

    in_specs = [pl.BlockSpec((tm, y.shape[1]), lambda i, j: (i, 0)) for y in ys]
    in_specs += [pl.BlockSpec((p.shape[0], tn), lambda i, j: (0, j)) for p in ps]
    in_specs += [pl.BlockSpec((tm, tn), functools.partial(lambda i, j, b: (i, b * (D // tn) + j), b=b)) for b in range(nb)]
    in_specs += [pl.BlockSpec((nb, tn), lambda i, j: (0, j))]
    return pl.pallas_call(
        body, name=name, grid=(S // tm, D // tn), in_specs=in_specs,
        out_specs=[pl.BlockSpec((tm, tn), lambda i, j: (i, j)), pl.BlockSpec((nb, tm, tn), lambda i, j: (0, i, j))],
        out_shape=[jax.ShapeDtypeStruct((S, D), BF16), jax.ShapeDtypeStruct((nb, S, D), F32)],
        compiler_params=_params(("parallel", "parallel")),
    )(*ys, *ps, *([h] * nb), bg)


def _merge_bwd(dm, z, h, bg, name):
    nb, S, D = z.shape
    tm = _tile(S, (256, 128))
    tn = _tile(D, (512, 256, 128))

    def body(*refs):
        dm_ref, z_ref = refs[0], refs[1]
        l_refs = refs[2:2 + nb]
        bg_ref, dz_ref, dl_ref, dbg_ref = refs[2 + nb:]

        @pl.when(pl.program_id(1) == 0)
        def _():
            dbg_ref[...] = jnp.zeros_like(dbg_ref)

        dmv = dm_ref[...]
        rows = lax.broadcasted_iota(jnp.int32, (SUBLANES, tn), 0)
        dbg = jnp.zeros((SUBLANES, tn), F32)
        for b in range(nb):
            gt = _sigmoid(l_refs[b][...] + bg_ref[b:b + 1, :])
            dz_ref[b] = (dmv * gt).astype(BF16)
            dl = dmv * z_ref[b] * gt * (1.0 - gt)
            dl_ref[b] = dl.astype(BF16)
            dbg += jnp.where(rows == b, jnp.sum(dl, axis=0, keepdims=True), 0.0)
        dbg_ref[...] += dbg

    in_specs = [pl.BlockSpec((tm, tn), lambda j, i: (i, j)), pl.BlockSpec((nb, tm, tn), lambda j, i: (0, i, j))]
    in_specs += [pl.BlockSpec((tm, tn), functools.partial(lambda j, i, b: (i, b * (D // tn) + j), b=b)) for b in range(nb)]
    in_specs += [pl.BlockSpec((nb, tn), lambda j, i: (0, j))]
    blk3 = pl.BlockSpec((nb, tm, tn), lambda j, i: (0, i, j))
    return pl.pallas_call(
        body, name=name, grid=(D // tn, S // tm), in_specs=in_specs,
        out_specs=[blk3, blk3, pl.BlockSpec((SUBLANES, tn), lambda j, i: (0, j))],
        out_shape=[jax.ShapeDtypeStruct((nb, S, D), BF16), jax.ShapeDtypeStruct((nb, S, D), BF16),
                   jax.ShapeDtypeStruct((SUBLANES, D), F32)],
        compiler_params=_params(("parallel", "arbitrary")),
    )(dm, z, *([h] * nb), bg)


def _shift_down(x, halo, k):
    xr = pltpu.roll(x, k, 0)
    hr = pltpu.roll(halo, k, 0)
    rows = lax.broadcasted_iota(jnp.int32, halo.shape, 0)
    top = jnp.where(rows < k, hr, xr[:SUBLANES])
    return jnp.concatenate([top, xr[SUBLANES:]], axis=0)


def _shift_up(x, halo, k):
    tm = x.shape[0]
    xr = pltpu.roll(x, tm - k, 0)
    hr = pltpu.roll(halo, SUBLANES - k, 0)
    rows = lax.broadcasted_iota(jnp.int32, halo.shape, 0)
    bot = jnp.where(rows >= SUBLANES - k, hr, xr[tm - SUBLANES:])
    return jnp.concatenate([xr[:tm - SUBLANES], bot], axis=0)


def _conv_tiles(S, F):
    return _tile(S, (GLU_ROWS, 512, 256, 128)), _tile(F, (512, 256, 128))


def _conv_in_specs(tm, tn, F):
    r8 = tm // SUBLANES
    nf = F // tn
    specs = []
    for half in range(2):
        specs.append(pl.BlockSpec((tm, tn), functools.partial(lambda j, i, o: (i, o + j), o=half * nf)))
        specs.append(pl.BlockSpec((SUBLANES, tn), functools.partial(lambda j, i, o: (jnp.maximum(i * r8 - 1, 0), o + j), o=half * nf)))
    for half in range(2):
        specs.append(pl.BlockSpec((3, tn), functools.partial(lambda j, i, o: (0, o + j), o=half * nf)))
        specs.append(pl.BlockSpec((1, tn), functools.partial(lambda j, i, o: (0, o + j), o=half * nf)))
    return specs


def _conv_apply(x, halo, w, b, first):
    halo = jnp.where(first, 0.0, halo)
    x1 = _shift_down(x, halo, 1)
    x2 = _shift_down(x, halo, 2)
    return b + x2 * w[0:1, :] + x1 * w[1:2, :] + x * w[2:3, :], x1, x2


def _glu_fwd(up, cw, cb, name):
    S, F2 = up.shape
    F = F2 // 2
    tm, tn = _conv_tiles(S, F)

    def body(ug, hg, uv, hv, wg, bgr, wv, bvr, a_ref):
        first = pl.program_id(1) == 0
        cg, _, _ = _conv_apply(ug[...], hg[...], wg[...], bgr[...], first)
        cv, _, _ = _conv_apply(uv[...], hv[...], wv[...], bvr[...], first)
        a_ref[...] = (cg * _sigmoid(cg) * cv).astype(BF16)

    return pl.pallas_call(
        body, name=name, grid=(F // tn, S // tm), in_specs=_conv_in_specs(tm, tn, F),
        out_specs=pl.BlockSpec((tm, tn), lambda j, i: (i, j)),
        out_shape=jax.ShapeDtypeStruct((S, F), BF16),
        compiler_params=_params(("parallel", "parallel")),
    )(up, up, up, up, cw, cb, cw, cb)


def _glu_bwd(up, cw, cb, da, name):
    S, F2 = up.shape
    F = F2 // 2
    tm, tn = _conv_tiles(S, F)

    def body(ug, hg, uv, hv, wg, bgr, wv, bvr, da_ref, dg_ref, dv_ref, sg_ref, sv_ref):
        i = pl.program_id(1)

        @pl.when(i == 0)
        def _():
            sg_ref[...] = jnp.zeros_like(sg_ref)
            sv_ref[...] = jnp.zeros_like(sv_ref)

        first = i == 0
        xg, xv = ug[...], uv[...]
        cg, xg1, xg2 = _conv_apply(xg, hg[...], wg[...], bgr[...], first)
        cv, xv1, xv2 = _conv_apply(xv, hv[...], wv[...], bvr[...], first)
        dav = da_ref[...]
        sg = _sigmoid(cg)
        dcv = dav * cg * sg
        dcg = dav * cv * sg * (1.0 + cg * (1.0 - sg))
        dg_ref[...] = dcg
        dv_ref[...] = dcv
        rows = lax.broadcasted_iota(jnp.int32, (SUBLANES, tn), 0)

        def stats(dc, x, x1, x2):
            acc = jnp.zeros((SUBLANES, tn), F32)
            for r, val in enumerate((dc * x2, dc * x1, dc * x, dc)):
                acc += jnp.where(rows == r, jnp.sum(val, axis=0, keepdims=True), 0.0)
            return acc

        sg_ref[...] += stats(dcg, xg, xg1, xg2)
        sv_ref[...] += stats(dcv, xv, xv1, xv2)

    tile = pl.BlockSpec((tm, tn), lambda j, i: (i, j))
    stat = pl.BlockSpec((SUBLANES, tn), lambda j, i: (0, j))
    return pl.pallas_call(
        body, name=name, grid=(F // tn, S // tm), in_specs=_conv_in_specs(tm, tn, F) + [tile],
        out_specs=[tile, tile, stat, stat],
        out_shape=[jax.ShapeDtypeStruct((S, F), F32), jax.ShapeDtypeStruct((S, F), F32),
                   jax.ShapeDtypeStruct((SUBLANES, F), F32), jax.ShapeDtypeStruct((SUBLANES, F), F32)],
        compiler_params=_params(("parallel", "arbitrary")),
    )(up, up, up, up, cw, cb, cw, cb, da)


def _conv_bwd(dcg, dcv, w, name):
    S, F = dcg.shape
    tm, tn = _tile(S, (CONV_BWD_ROWS, 1024, 512, 256, 128)), _conv_tiles(S, F)[1]
    r8 = tm // SUBLANES
    ni = S // tm
    nf = F // tn

    def body(g_ref, gh_ref, v_ref, vh_ref, w_ref, o_ref):
        gate = pl.program_id(0) == 0
        x = jnp.where(gate, g_ref[...], v_ref[...])
        halo = jnp.where(gate, gh_ref[...], vh_ref[...])
        halo = jnp.where(pl.program_id(2) == ni - 1, 0.0, halo)
        wv = w_ref[...]
        o_ref[...] = (x * wv[2:3, :] + _shift_up(x, halo, 1) * wv[1:2, :] + _shift_up(x, halo, 2) * wv[0:1, :]).astype(BF16)

    def tile(half):
        return pl.BlockSpec((tm, tn), lambda h, j, i: (jnp.where(h == half, i, 0), jnp.where(h == half, j, 0)))

    def below(half):
        return pl.BlockSpec((SUBLANES, tn), lambda h, j, i: (
            jnp.where(h == half, jnp.minimum((i + 1) * r8, S // SUBLANES - 1), 0), jnp.where(h == half, j, 0)))

    return pl.pallas_call(
        body, name=name, grid=(2, nf, ni),
        in_specs=[tile(0), below(0), tile(1), below(1), pl.BlockSpec((3, tn), lambda h, j, i: (0, h * nf + j))],
        out_specs=pl.BlockSpec((tm, tn), lambda h, j, i: (i, h * nf + j)),
        out_shape=jax.ShapeDtypeStruct((S, 2 * F), BF16),
        compiler_params=_params(("parallel", "parallel", "parallel")),
    )(dcg, dcg, dcv, dcv, w)


def _adamw(slot_list, own_list, me, w, m, v, name, dep=None):
    L = len(slot_list)
    P, K, C = slot_list[0].shape
    tr = _tile(K, (256, 128, 64, 32, 16))
    while tr * C * 4 > (1 << 20) and tr % 32 == 0:
        tr //= 2
    nb = K // tr
    has_own = own_list is not None

    def body(me_ref, *refs):
        s_refs = refs[:L]
        o_refs = refs[L:2 * L] if has_own else None
        w_ref, m_ref, v_ref = refs[L * (1 + has_own):L * (1 + has_own) + 3]
        g_ref, d_ref, nm_ref, nv_ref = refs[-4:]
        layer = pl.program_id(0)
        g = None
        for l in range(L):
            gl = None
            for p in range(P):
                term = s_refs[l][p].astype(F32)
                if has_own:
                    term = jnp.where(me_ref[0] == p, o_refs[l][0].astype(F32), term)
                gl = term if gl is None else gl + term
            g = gl if g is None else jnp.where(layer == l, gl, g)
        nm = ADAM_B1 * m_ref[...] + (1.0 - ADAM_B1) * g
        nv = ADAM_B2 * v_ref[...] + (1.0 - ADAM_B2) * (g * g)
        m_hat = nm / (1.0 - ADAM_B1 ** ADAM_STEP)
        v_hat = nv / (1.0 - ADAM_B2 ** ADAM_STEP)
        g_ref[...] = g
        d_ref[...] = -ADAM_LR * (m_hat / (jnp.sqrt(v_hat) + ADAM_EPS) + ADAM_WD * w_ref[...])
        nm_ref[...] = nm
        nv_ref[...] = nv

    blk = pl.BlockSpec((None, tr, C), lambda li, i, me_ref: (li, i, 0))
    specs = [pl.BlockSpec((P, tr, C), functools.partial(lambda li, i, me_ref, l: (0, jnp.where(li == l, i, 0), 0), l=l))
             for l in range(L)]
    if has_own:
        specs += [pl.BlockSpec((1, tr, C), functools.partial(lambda li, i, me_ref, l: (me_ref[0], jnp.where(li == l, i, 0), 0), l=l))
                  for l in range(L)]
    return pl.pallas_call(
        body, name=name,
        grid_spec=pltpu.PrefetchScalarGridSpec(
            num_scalar_prefetch=1, grid=(L, nb), in_specs=specs + [blk, blk, blk] + [_ANY] * (dep is not None),
            out_specs=[blk] * 4),
        out_shape=[jax.ShapeDtypeStruct((L, K, C), F32)] * 4,
        compiler_params=_params(("arbitrary", "arbitrary")),
    )(me, *slot_list, *(own_list if has_own else []), w, m, v, *([] if dep is None else [dep]))


_HBM = pl.BlockSpec(memory_space=pltpu.HBM)
_SEM = pl.BlockSpec(memory_space=pltpu.SEMAPHORE)
_ANY = pl.BlockSpec(memory_space=pl.ANY)


def _peers():
    x, y, c = lax.axis_index("x"), lax.axis_index("y"), lax.axis_index("c")

    def flip(v, bit):
        return 1 - v if bit else v

    def peer(k):
        return (flip(x, (k >> 2) & 1), flip(y, (k >> 1) & 1), flip(c, k & 1))

    def peer_index(k):
        px, py, pc = peer(k)
        return 4 * px + 2 * py + pc

    return 4 * x + 2 * y + c, peer, peer_index


def _split_copy(src_refs, land_refs, send_sems, recv_sems, scatter, a, k, outgoing):
    me, peer, peer_index = _peers()
    if outgoing:
        src = src_refs[a].at[peer_index(k)] if scatter else src_refs[a]
        dst = land_refs[a].at[me]
    else:
        src = src_refs[a].at[me] if scatter else src_refs[a]
        dst = land_refs[a].at[peer_index(k)]
    pair = a * (N_DEV - 1) + k - 1
    return pltpu.make_async_remote_copy(src_ref=src, dst_ref=dst, send_sem=send_sems.at[pair],
                                        recv_sem=recv_sems.at[pair], device_id=peer(k),
                                        device_id_type=pl.DeviceIdType.MESH)


def _gather_two_level(srcs, name):
    na = len(srcs)

    def body(*refs):
        src_refs, out_refs = refs[:na], refs[na:2 * na]
        send_sems, recv_sems = refs[2 * na:]
        x, y, c = lax.axis_index("x"), lax.axis_index("y"), lax.axis_index("c")
        me, sibling = (x, y, c), (x, y, 1 - c)
        chips = [(1 - x, y), (x, 1 - y), (1 - x, 1 - y)]

        def copy(a, k, block, to, src=None):
            px, py, pc = block
            slot = out_refs[a].at[4 * px + 2 * py + pc]
            return pltpu.make_async_remote_copy(
                src_ref=slot if src is None else src, dst_ref=slot, send_sem=send_sems.at[a * (N_DEV - 1) + k],
                recv_sem=recv_sems.at[a * (N_DEV - 1) + k], device_id=to, device_id_type=pl.DeviceIdType.MESH)

        first = [copy(a, 0, me, sibling, src_refs[a]) for a in range(na)]
        first += [copy(a, 1 + j, me, (*chip, c), src_refs[a]) for j, chip in enumerate(chips) for a in range(na)]
        for cp in first:
            cp.start()
        passed = []
        for j, chip in enumerate(chips):
            for a in range(na):
                copy(a, 1 + j, (*chip, c), me).wait_recv()
                passed.append(copy(a, 4 + j, (*chip, c), sibling))
                passed[-1].start()
        for a in range(na):
            copy(a, 0, sibling, me).wait_recv()
        for j, chip in enumerate(chips):
            for a in range(na):
                copy(a, 4 + j, (*chip, 1 - c), me).wait_recv()
        for cp in first + passed:
            cp.wait_send()

    return pl.pallas_call(
        body, name=name, in_specs=[_ANY] * na, out_specs=[_ANY] * na,
        out_shape=[jax.ShapeDtypeStruct((N_DEV,) + s.shape, s.dtype) for s in srcs],
        scratch_shapes=[pltpu.SemaphoreType.DMA((na * (N_DEV - 1),)), pltpu.SemaphoreType.DMA((na * (N_DEV - 1),))],
    )(*srcs)


def _exchange_start(srcs, scatter, after, name):
    na = len(srcs)
    land_shapes = [s.shape if scatter else (N_DEV,) + s.shape for s in srcs]
    has_after = after is not None

    def body(*refs):
        src_refs, land_refs = refs[:na], refs[na:2 * na]
        send_sems, recv_sems = refs[2 * na + has_after], refs[2 * na + has_after + 1]
        token = refs[-1]
        for k in range(1, N_DEV):
            for a in range(na):
                _split_copy(src_refs, land_refs, send_sems, recv_sems, scatter, a, k, True).start()
        token[...] = jnp.zeros_like(token)

    sems = pltpu.SemaphoreType.DMA((na * (N_DEV - 1),))
    out_shape = ([sems, sems] + [pltpu.HBM(s.shape, s.dtype) for s in srcs]
                 + [pltpu.HBM(shp, s.dtype) for shp, s in zip(land_shapes, srcs)] + [jax.ShapeDtypeStruct((SUBLANES, 128), F32)])
    args = [pltpu.with_memory_space_constraint(s, pltpu.HBM) for s in srcs]
    args += [pltpu.with_memory_space_constraint(lax.empty(shp, s.dtype), pltpu.HBM) for shp, s in zip(land_shapes, srcs)]
    if has_after:
        args.append(after)
    res = pl.pallas_call(
        body, name=name, in_specs=[_HBM] * (2 * na) + [_ANY] * has_after,
        out_specs=[_SEM, _SEM] + [_HBM] * (2 * na) + [pl.BlockSpec(memory_space=pltpu.VMEM)], out_shape=out_shape,
        input_output_aliases={i: 2 + i for i in range(2 * na)},
        compiler_params=pltpu.CompilerParams(has_side_effects=pltpu.SideEffectType.DATAFLOW_SIDE_EFFECTING),
    )(*args)
    handle = dict(send=res[0], recv=res[1], srcs=list(res[2:2 + na]), lands=list(res[2 + na:2 + 2 * na]), scatter=scatter)
    return handle, res[-1]


def _exchange_wait(handle, after, name):
    srcs, lands, scatter = handle["srcs"], handle["lands"], handle["scatter"]
    na = len(srcs)

    def body(*refs):
        src_refs, land_refs = refs[:na], refs[na:2 * na]
        send_sems, recv_sems = refs[2 * na], refs[2 * na + 1]
        for k in range(1, N_DEV):
            for a in range(na):
                _split_copy(src_refs, land_refs, send_sems, recv_sems, scatter, a, k, True).wait_send()
                _split_copy(src_refs, land_refs, send_sems, recv_sems, scatter, a, k, False).wait_recv()

    res = pl.pallas_call(
        body, name=name, in_specs=[_HBM] * (2 * na) + [_SEM, _SEM, _ANY], out_specs=[_HBM] * (2 * na),
        out_shape=[pltpu.HBM(s.shape, s.dtype) for s in srcs] + [pltpu.HBM(s.shape, s.dtype) for s in lands],
        input_output_aliases={i: i for i in range(2 * na)},
        compiler_params=pltpu.CompilerParams(has_side_effects=pltpu.SideEffectType.DATAFLOW_SIDE_EFFECTING),
    )(*srcs, *lands, handle["send"], handle["recv"], after)
    return list(res[:na]), list(res[na:])


def _layout(D):
    aq, akv = SWA_Q_HEADS * SWA_HEAD_DIM, SWA_KV_HEADS * SWA_HEAD_DIM
    w = SGU_GROUPS * SGU_DIM
    return aq, akv, w


class _Seg:
    def __init__(self, D, rq, rkv):
        aq, akv, w = _layout(D)
        src = {}
        o = 0
        for nm, wd in (("qa", aq), ("ka", akv), ("va", akv), ("cq", rq), ("ckv", rkv), ("kr", MLA_ROPE), ("hu", w), ("hv", w),
                       ("g", 3 * D)):
            src[nm] = (o, wd)
            o += wd
        self.n_in = o
        self.order = ("g", "qa", "hu", "hv", "cq", "ckv", "ka", "va", "kr")
        self.src = src
        self.off = {}
        o = 0
        for nm in self.order:
            self.off[nm] = o
            o += src[nm][1]
        self.width = {nm: src[nm][1] for nm in self.order}
        self.n_pad = -(-o // 1536) * 1536 if o > 1536 else -(-o // 512) * 512
        self.used = o

    def from_shards(self, shards):
        c = shards.shape[2]
        parts = []
        for nm in self.order:
            s0, wd = self.src[nm]
            for d in range(N_DEV):
                lo, hi = max(s0, c * d), min(s0 + wd, c * (d + 1))
                if lo < hi:
                    parts.append(shards[d][:, lo - c * d:hi - c * d])
        parts.append(jnp.zeros((shards.shape[1], self.n_pad - self.used), shards.dtype))
        return jnp.concatenate(parts, axis=1)

    def to_shards(self, w):
        c = self.n_in // N_DEV
        names = sorted(self.order, key=lambda nm: self.src[nm][0])
        shards = []
        for d in range(N_DEV):
            parts = []
            for nm in names:
                s0, wd = self.src[nm]
                lo, hi = max(s0, c * d), min(s0 + wd, c * (d + 1))
                if lo < hi:
                    parts.append(w[:, self.off[nm] + lo - s0:self.off[nm] + hi - s0])
            shards.append(jnp.concatenate(parts, axis=1))
        return jnp.stack(shards)


def _uq_permute(w):
    R = w.shape[0]
    H = MLA_HEADS
    w3 = w.reshape(R, H, MLA_NOPE + MLA_ROPE)
    return jnp.concatenate([w3[:, :, :MLA_NOPE].reshape(R, H * MLA_NOPE), w3[:, :, MLA_NOPE:].reshape(R, H * MLA_ROPE)], axis=1)


def _uq_unpermute(w):
    R = w.shape[0]
    H = MLA_HEADS
    n = w[:, :H * MLA_NOPE].reshape(R, H, MLA_NOPE)
    r = w[:, H * MLA_NOPE:].reshape(R, H, MLA_ROPE)
    return jnp.concatenate([n, r], axis=2).reshape(R, H * (MLA_NOPE + MLA_ROPE))


def _ukv_permute(w):
    R = w.shape[0]
    w3 = w.reshape(R, MLA_HEADS, MLA_NOPE + MLA_V)
    return jnp.concatenate([w3[:, :, :MLA_NOPE].reshape(R, -1), w3[:, :, MLA_NOPE:].reshape(R, -1)], axis=1)


def _ukv_unpermute(w):
    R = w.shape[0]
    H = MLA_HEADS
    k = w[:, :H * MLA_NOPE].reshape(R, H, MLA_NOPE)
    v = w[:, H * MLA_NOPE:].reshape(R, H, MLA_V)
    return jnp.concatenate([k, v], axis=2).reshape(R, H * (MLA_NOPE + MLA_V))


GROUPS = {"a": ("w_in",), "b": ("w_uq", "w_ukv", "w_proj_a", "w_proj_b", "w_proj_c", "w_o", "b_gate"),
          "c": ("w_up", "w_down", "conv_w")}


def _layer_fwd(l, x, x16, fetch, P, cs, sn, seg, alpha):
    S, D = x.shape
    H, half = MLA_HEADS, MLA_ROPE // 2
    off = seg.off
    nm = lambda s: f"l{l}_{s}"
    sv = {"x16": x16}
    W = {"w_in": seg.from_shards(fetch(l, "a", x16)["w_in"])}
    h = _mm(x16, W["w_in"], "nn", [(F32, "n")], nm("h"))[0]
    sv["h"] = h
    ya, lse_a = _swa_fwd(h, off["qa"], off["ka"], off["va"], P["sinks"], nm("swa_fwd"))
    cqn, rq = _rms_fwd(h, off["cq"], seg.width["cq"], P["q_norm_g"], nm("rmsq_fwd"))
    ckvn, rkv = _rms_fwd(h, off["ckv"], seg.width["ckv"], P["kv_norm_g"], nm("rmskv_fwd"))
    W.update(fetch(l, "b", cqn))
    W["w_uq"] = _uq_permute(W["w_uq"])
    W["w_ukv"] = _ukv_permute(W["w_ukv"])
    qf = _mm(cqn, W["w_uq"], "nn", [(F32, "n")], nm("uq"))[0]
    kvf = _mm(ckvn, W["w_ukv"], "nn", [(BF16, "n")], nm("ukv"))[0]
    qr = _rope(qf, H * MLA_NOPE, H * MLA_ROPE, cs, sn, False, nm("ropeq_fwd"))
    kr = _rope(h, off["kr"], LANES, cs, sn, False, nm("ropek_fwd"))
    yb, yb16, lse_b = _mla_fwd(qf, qr, kvf, kr, nm("mla_fwd"))
    w16 = jnp.where(jnp.tril(jnp.ones((SGU_CHUNK, SGU_CHUNK), bool))[None], P["sgu_w"], 0.0).astype(BF16)
    bt = P["sgu_b"].T
    yc = _sgu_fwd(h, off["hu"], off["hv"], P["sgu_ln_g"], P["sgu_ln_b"], w16, bt, nm("sgu_fwd"))
    merged, z = _merge_fwd([ya, yb16, yc], [W["w_proj_a"], W["w_proj_b"], W["w_proj_c"]], h, W["b_gate"], nm("merge_fwd"))
    x1, x1_16, xh1, rs1 = _mm_ln(merged, W["w_o"], x, P["ln1_g"], P["ln1_b"], alpha, nm("wo_ln1"))
    W.update(fetch(l, "c", x1_16))
    up = _mm(x1_16, W["w_up"], "nn", [(F32, "n")], nm("up"))[0]
    a = _glu_fwd(up, W["conv_w"], P["conv_b"], nm("glu_fwd"))
    x2, x2_16, xh2, rs2 = _mm_ln(a, W["w_down"], x1, P["ln2_g"], P["ln2_b"], alpha, nm("down_ln2"))
    sv.update(W=W, ya=ya, lse_a=lse_a, cqn=cqn, rq=rq, ckvn=ckvn, rkv=rkv, qf=qf, qr=qr, kvf=kvf, kr=kr, lse_b=lse_b, yb=yb, yb16=yb16,
              w16=w16, bt=bt, yc=yc, merged=merged, z=z, x1_16=x1_16, xh1=xh1, rs1=rs1, up=up, a=a, xh2=xh2, rs2=rs2)
    return x2, x2_16, sv


def _dw_chunks(k, a, dy, name, post=None, chunker=None):
    n = dy.shape[1]
    if k not in ROW_SHARDED and post is None and chunker is None and (n // N_DEV) % 128 == 0:
        return _mm(a, dy, "tn", [(BF16, "n")], name, chunk=n // N_DEV)[0]
    g = _mm(a, dy, "tn", [(BF16, "n")], name)[0]
    if chunker is not None:
        return chunker(g)
    return _to_chunks(k, g if post is None else post(g))


def _after(arr, token):
    return arr if token is None else arr + token[0:1, 0:1].astype(arr.dtype)


def _layer_bwd(l, dx2, sv, P, cs, sn, seg, alpha, emit):
    S, D = dx2.shape
    H, half = MLA_HEADS, MLA_ROPE // 2
    off = seg.off
    h, W = sv["h"], sv["W"]
    nm = lambda s: f"l{l}_{s}"
    g = {}
    dr2, dr2_16, g["ln2_g"], g["ln2_b"] = _ln_bwd(dx2, sv["xh2"], sv["rs2"], P["ln2_g"], nm("ln2_bwd"))
    g["w_down"] = _dw_chunks("w_down", sv["a"], dr2_16, nm("dw_down"))
    da = _mm(dr2_16, W["w_down"], "nt", [(F32, "n")], nm("da"))[0]
    dcg, dcv, st_g, st_v = _glu_bwd(sv["up"], W["conv_w"], P["conv_b"], da, nm("glu_bwd"))
    F = dcg.shape[1]
    g["conv_w"] = _to_chunks("conv_w", jnp.concatenate([st_g[0:3], st_v[0:3]], axis=1))
    g["conv_b"] = jnp.concatenate([st_g[3:4], st_v[3:4]], axis=1)
    dup = _conv_bwd(dcg, dcv, W["conv_w"], nm("conv_bwd"))
    g["w_up"] = _dw_chunks("w_up", sv["x1_16"], dup, nm("dw_up"))
    token = emit(l, "c", {k: g.pop(k) for k in GROUPS["c"]})
    dx1 = _mm_axpy(dup, W["w_up"], "nt", dr2, alpha, nm("dx1"), dep=token)
    dr1, dr1_16, g["ln1_g"], g["ln1_b"] = _ln_bwd(dx1, sv["xh1"], sv["rs1"], P["ln1_g"], nm("ln1_bwd"))
    g["w_o"] = _dw_chunks("w_o", sv["merged"], dr1_16, nm("dw_o"))
    dmerged = _mm(dr1_16, W["w_o"], "nt", [(F32, "n")], nm("dmerged"))[0]
    dz, dlog, dbg = _merge_bwd(dmerged, sv["z"], h, W["b_gate"], nm("merge_bwd"))
    g["b_gate"] = _to_chunks("b_gate", dbg[0:3])
    g["w_proj_a"] = _dw_chunks("w_proj_a", sv["ya"], dz[0], nm("dw_pa"))
    g["w_proj_b"] = _dw_chunks("w_proj_b", sv["yb16"], dz[1], nm("dw_pb"))
    g["w_proj_c"] = _dw_chunks("w_proj_c", sv["yc"], dz[2], nm("dw_pc"))
    dya = _mm(dz[0], W["w_proj_a"], "nt", [(F32, "n")], nm("dya"))[0]
    dyb = _mm(dz[1], W["w_proj_b"], "nt", [(F32, "n")], nm("dyb"))[0]
    dyc = _mm(dz[2], W["w_proj_c"], "nt", [(F32, "n")], nm("dyc"))[0]
    dhu, dhv, g["sgu_w"], db_s, g["sgu_ln_g"], g["sgu_ln_b"] = _sgu_bwd(
        h, off["hu"], off["hv"], P["sgu_ln_g"], P["sgu_ln_b"], sv["w16"], sv["bt"], dyc, nm("sgu_bwd"))
    g["sgu_b"] = db_s[:, :SGU_GROUPS].T
    dqa, dka, dva, dsk = _swa_bwd(h, off["qa"], off["ka"], off["va"], P["sinks"], dya, sv["lse_a"], nm("swa_bwd"))
    g["sinks"] = dsk[0, :SWA_Q_HEADS]
    dqn, dqr, dkn, dvv, dkr = _mla_bwd(sv["qf"], sv["qr"], sv["kvf"], sv["kr"], dyb, sv["yb"], sv["lse_b"], nm("mla_bwd"))
    dqf = jnp.concatenate([dqn.astype(BF16), _rope(dqr, 0, H * MLA_ROPE, cs, sn, True, nm("ropeq_bwd"))], axis=1)
    dkvf = jnp.concatenate([dkn, dvv], axis=1).astype(BF16)
    dkr16 = _rope(dkr, 0, LANES, cs, sn, True, nm("ropek_bwd"))
    g["w_uq"] = _dw_chunks("w_uq", sv["cqn"], dqf, nm("dw_uq"), _uq_unpermute)
    g["w_ukv"] = _dw_chunks("w_ukv", sv["ckvn"], dkvf, nm("dw_ukv"), _ukv_unpermute)
    token = emit(l, "b", {k: g.pop(k) for k in GROUPS["b"]})
    dcqn = _mm(dqf, W["w_uq"], "nt", [(F32, "n")], nm("dcqn"), dep=token)[0]
    dckvn = _mm(dkvf, W["w_ukv"], "nt", [(F32, "n")], nm("dckvn"), dep=token)[0]
    dcq, g["q_norm_g"] = _rms_bwd(dcqn, h, off["cq"], seg.width["cq"], sv["rq"], P["q_norm_g"], nm("rmsq_bwd"))
    dckv, g["kv_norm_g"] = _rms_bwd(dckvn, h, off["ckv"], seg.width["ckv"], sv["rkv"], P["kv_norm_g"], nm("rmskv_bwd"))
    assert seg.order[-1] == "kr" and seg.n_pad - seg.used >= LANES - MLA_ROPE
    parts = {"g": jnp.concatenate([dlog[0], dlog[1], dlog[2]], axis=1), "qa": dqa, "hu": dhu, "hv": dhv, "cq": dcq, "ckv": dckv,
             "ka": dka.astype(BF16), "va": dva.astype(BF16), "kr": dkr16}
    dh = jnp.concatenate([parts[k] for k in seg.order] + [jnp.zeros((S, seg.n_pad - seg.used - (LANES - MLA_ROPE)), BF16)],
                         axis=1)
    token = emit(l, "a", {"w_in": _dw_chunks("w_in", sv["x16"], dh, nm("dw_in"), chunker=seg.to_shards)})
    dx = _mm_axpy(dh, W["w_in"], "nt", dr1, alpha, nm("dx"), dep=token)
    return dx, g


BIG = ("w_in", "w_uq", "w_ukv", "w_proj_a", "w_proj_b", "w_proj_c", "w_o", "w_up", "w_down")
ROW_SHARDED = ("w_proj_b", "w_o", "w_down")
SHARDED_F32 = ("b_gate", "conv_w")
REPLICATED = ("sinks", "q_norm_g", "kv_norm_g", "sgu_ln_g", "sgu_ln_b", "sgu_w", "sgu_b", "ln1_g", "ln1_b", "conv_b", "ln2_g",
              "ln2_b")
WEIGHTS = ("w_in", "b_gate", "sinks", "q_norm_g", "kv_norm_g", "w_uq", "w_ukv", "sgu_ln_g", "sgu_ln_b", "sgu_w", "sgu_b",
           "w_proj_a", "w_proj_b", "w_proj_c", "w_o", "ln1_g", "ln1_b", "w_up", "conv_w", "conv_b", "w_down", "ln2_g", "ln2_b")


def _step_local(x, positions, target, small, rq, rkv, fetch, emit, token=None):
    S, D = x.shape
    L = small["sinks"].shape[0]
    alpha = (2 * L) ** 0.25
    seg = _Seg(D, rq, rkv)
    inv_freq = ROPE_THETA ** (-jnp.arange(0, MLA_ROPE, 2, dtype=F32) / MLA_ROPE)
    ang = positions.astype(F32)[:, None] * inv_freq
    reps = LANES // (MLA_ROPE // 2)
    cs, sn = jnp.tile(jnp.cos(ang), (1, reps)), jnp.tile(jnp.sin(ang), (1, reps))
    rows = ("q_norm_g", "kv_norm_g", "sgu_ln_g", "sgu_ln_b", "ln1_g", "ln1_b", "conv_b", "ln2_g", "ln2_b")
    layers = [{k: small[k][l].reshape(1, -1) if k in rows else small[k][l] for k in small} for l in range(L)]
    saved = []
    x16 = _after(x, token).astype(BF16)
    for l in range(L):
        x, x16, sv = _layer_fwd(l, x, x16, fetch, layers[l], cs, sn, seg, alpha)
        saved.append(sv)
    loss, dx = _loss(x, target, "loss")
    grads = [None] * L
    for l in reversed(range(L)):
        dx, grads[l] = _layer_bwd(l, dx, saved[l], layers[l], cs, sn, seg, alpha, emit)
    out = {k: jnp.stack([grads[l][k].reshape(small[k].shape[1:]) for l in range(L)]) for k in small}
    return loss, dx, out


def _unshard(k, gathered):
    n, r, c = gathered.shape
    if k in ROW_SHARDED:
        return gathered.reshape(n * r, c)
    return gathered.transpose(1, 0, 2).reshape(r, n * c)


def _to_chunks(k, gfull):
    r, c = gfull.shape
    if k in ROW_SHARDED:
        return gfull.reshape(N_DEV, r // N_DEV, c)
    return gfull.reshape(r, N_DEV, c // N_DEV).transpose(1, 0, 2)


def _pack(arrs):
    P = arrs[0].shape[0]
    flat = jnp.concatenate([a.reshape(P, -1) for a in arrs], axis=1)
    sizes = [(a.size // P, a.size // P) for a in arrs]
    flat = jnp.pad(flat, ((0, 0), (0, -flat.shape[1] % (SUBLANES * 128))))
    return flat.reshape(P, -1, 128), sizes


def _unpack(packed, sizes, shapes):
    flat = packed.reshape(-1)
    out, o = [], 0
    for (n, npad), shp in zip(sizes, shapes):
        out.append(flat[o:o + n].reshape(shp))
        o += npad
    return out


def kernel(x, positions, w_in, b_gate, sinks, q_norm_g, kv_norm_g, w_uq, w_ukv, sgu_ln_g, sgu_ln_b, sgu_w, sgu_b, w_proj_a, w_proj_b, w_proj_c, w_o, ln1_g, ln1_b, w_up, conv_w, conv_b, w_down, ln2_g, ln2_b, loss_target, m_w_in, m_b_gate, m_sinks, m_q_norm_g, m_kv_norm_g, m_w_uq, m_w_ukv, m_sgu_ln_g, m_sgu_ln_b, m_sgu_w, m_sgu_b, m_w_proj_a, m_w_proj_b, m_w_proj_c, m_w_o, m_ln1_g, m_ln1_b, m_w_up, m_conv_w, m_conv_b, m_w_down, m_ln2_g, m_ln2_b, v_w_in, v_b_gate, v_sinks, v_q_norm_g, v_kv_norm_g, v_w_uq, v_w_ukv, v_sgu_ln_g, v_sgu_ln_b, v_sgu_w, v_sgu_b, v_w_proj_a, v_w_proj_b, v_w_proj_c, v_w_o, v_ln1_g, v_ln1_b, v_w_up, v_conv_w, v_conv_b, v_w_down, v_ln2_g, v_ln2_b):
    given = dict(locals())
    w = {k: given[k] for k in WEIGHTS}
    mom = {k: given["m_" + k] for k in WEIGHTS}
    var = {k: given["v_" + k] for k in WEIGHTS}

    L = w_in.shape[0]
    order = [(l, grp) for l in range(L) for grp in ("a", "b", "c")]

    first = [w[k][0].astype(BF16) for k in GROUPS["a"]]
    first_lands = _gather_two_level(first, "gather_first")
    gathers, token = {}, first_lands[0]
    for l, grp in order[1:]:
        srcs = [w[k][l].astype(BF16) if k in BIG else w[k][l] for k in GROUPS[grp]]
        gathers[l, grp], token = _exchange_start(srcs, False, token, f"gather_start_l{l}{grp}")

    me = 4 * lax.axis_index("x") + 2 * lax.axis_index("y") + lax.axis_index("c")
    mine = (jnp.arange(N_DEV) == me)[:, None, None]

    def fetch(l, grp, after):
        if (l, grp) == order[0]:
            srcs, lands = first, first_lands
        else:
            srcs, lands = _exchange_wait(gathers[l, grp], after, f"gather_wait_l{l}{grp}")
        full = {k: jnp.where(mine, srcs[i][None], lands[i]) for i, k in enumerate(GROUPS[grp])}
        return {k: v if k == "w_in" else _unshard(k, v) for k, v in full.items()}

    scatters = {}

    def emit(l, grp, chunks):
        scatters[l, grp], tok = _exchange_start([chunks[k] for k in GROUPS[grp]], True, None, f"scatter_start_l{l}{grp}")
        return tok

    small = {k: w[k] for k in REPLICATED}
    loss, grad_x, g = _step_local(x[0], positions[0], loss_target[0], small, w_uq.shape[1], w_ukv.shape[1], fetch, emit, token)
    loss = lax.psum(loss[0, 0], AXES)

    packed, sizes = _pack([g[k][None] for k in REPLICATED])
    small_grads, after = _exchange_start([packed[0]], False, grad_x, "gather_small_grads_start")

    me1 = me.astype(jnp.int32).reshape(1)
    res = {}
    for grp in ("c", "b", "a"):
        slots, own = {}, {}
        for l in reversed(range(L)):
            srcs, lands = _exchange_wait(scatters[l, grp], after, f"scatter_wait_l{l}{grp}")
            for i, k in enumerate(GROUPS[grp]):
                slots[k, l], own[k, l] = lands[i], srcs[i]
        for k in GROUPS[grp]:
            res[k] = _adamw([slots[k, l] for l in range(L)], [own[k, l] for l in range(L)], me1, w[k], mom[k], var[k],
                            "adamw_" + k, dep=after)
            after = res[k][1]

    srcs, lands = _exchange_wait(small_grads, after, "gather_small_grads_wait")
    parts = jnp.where(mine, srcs[0][None], lands[0])
    shapes = [w[k].shape for k in REPLICATED]
    pw, _ = _pack([w[k][None] for k in REPLICATED])
    pm, _ = _pack([mom[k][None] for k in REPLICATED])
    pv, _ = _pack([var[k][None] for k in REPLICATED])
    outs = _adamw([parts], None, me1, pw, pm, pv, "adamw_small")
    unpacked = [_unpack(o, sizes, shapes) for o in outs]
    for i, k in enumerate(REPLICATED):
        res[k] = [unpacked[j][i] for j in range(4)]

    return (loss, grad_x[None], *[res[k][0] for k in WEIGHTS], *[res[k][1] for k in WEIGHTS],
            *[res[k][2] for k in WEIGHTS], *[res[k][3] for k in WEIGHTS])
```

```python
import functools
import math

import jax
import jax.numpy as jnp
from jax import lax
from jax.experimental import pallas as pl
from jax.experimental.pallas import tpu as pltpu

F32 = jnp.float32
BF16 = jnp.bfloat16

SWA_Q_HEADS = 16
SWA_KV_HEADS = 2
SWA_HEAD_DIM = 64
SWA_BLOCK = 128
MLA_HEADS = 16
MLA_NOPE = 128
MLA_ROPE = 64
MLA_V = 128
SGU_GROUPS = 8
SGU_DIM = 128
SGU_CHUNK = 128
ROPE_THETA = 10000.0
EPS = 1e-5
MASK = -1e30
ADAM_LR = 0.001
ADAM_B1 = 0.9
ADAM_B2 = 0.999
ADAM_EPS = 1e-08
ADAM_WD = 0.01
ADAM_STEP = 10

N_DEV = 8
AXES = ("x", "y", "c")
VMEM_LIMIT = 56 * 1024 * 1024
MLA_TILE = 512
MLA_FWD_TILE = 1024
ROW_TILE = 512
CONV_BWD_ROWS = 2048
GLU_ROWS = 1024
MAX_TK = 2816
SUBLANES = 8


def _tile(n, prefs):
    for p in prefs:
        if n % p == 0:
            return p
    return n


def _params(sem):
    return pltpu.CompilerParams(dimension_semantics=sem, vmem_limit_bytes=VMEM_LIMIT)


def _cols(tm, width, off):
    assert off % width == 0, (off, width)
    blk = off // width
    return pl.BlockSpec((tm, width), lambda i, *_: (i, blk))


def _full(shape):
    nd = len(shape)
    return pl.BlockSpec(shape, lambda *_: (0,) * nd)


def _sigmoid(v):
    return 1.0 / (1.0 + jnp.exp(-v))


def _gelu(v):
    return 0.5 * v * (1.0 + lax.erf(v * (2.0 ** -0.5)))


def _gelu_grad(v):
    return 0.5 * (1.0 + lax.erf(v * (2.0 ** -0.5))) + v * jnp.exp(-0.5 * v * v) * (1.0 / math.sqrt(2.0 * math.pi))


_DIMS = {"nn": (((1,), (0,)), ((), ())), "nt": (((1,), (1,)), ((), ())), "tn": (((0,), (0,)), ((), ()))}


def _mm(a, b, mode, outs, name, *, extras=(), epilogue=None, full_n=False, dep=None, chunk=None):
    if mode == "nn":
        (M, K), (K2, N) = a.shape, b.shape
    elif mode == "nt":
        (M, K), (N, K2) = a.shape, b.shape
    else:
        (K, M), (K2, N) = a.shape, b.shape
    assert K == K2, (a.shape, b.shape, mode)
    tm = _tile(M, (1024, 512, 256, 128))
    tn = N if full_n else _tile(N, (1024, 768, 512, 384, 256, 128))
    if full_n:
        tm = _tile(M, (512, 256, 128))
    if chunk is not None:
        tn = chunk if chunk <= 1536 else _tile(chunk, (1024, 768, 512, 384, 256, 128))
        assert N % chunk == 0 and chunk % tn == 0 and tn % 128 == 0, (N, chunk, tn)
    max_tk = MAX_TK // 2 if full_n else MAX_TK
    tk = max(d for d in range(128, min(K, max_tk) + 1, 128) if K % d == 0) if K % 128 == 0 else K
    nk = K // tk
    if mode == "nn":
        a_spec = pl.BlockSpec((tm, tk), lambda i, j, k: (i, k))
        b_spec = pl.BlockSpec((tk, tn), lambda i, j, k: (k, j))
    elif mode == "nt":
        a_spec = pl.BlockSpec((tm, tk), lambda i, j, k: (i, k))
        b_spec = pl.BlockSpec((tn, tk), lambda i, j, k: (j, k))
    else:
        a_spec = pl.BlockSpec((tk, tm), lambda i, j, k: (k, i))
        b_spec = pl.BlockSpec((tk, tn), lambda i, j, k: (k, j))
    in_specs = [a_spec, b_spec]
    for arr, kind in extras:
        if kind == "tile":
            in_specs.append(pl.BlockSpec((tm, tn), lambda i, j, k: (i, j)))
        else:
            in_specs.append(pl.BlockSpec((1, tn), lambda i, j, k: (0, j)))
    out_specs, out_shape = [], []
    for dt, kind in outs:
        if kind == "n" and chunk is not None:
            per = chunk // tn
            out_specs.append(pl.BlockSpec((None, tm, tn), lambda i, j, k: (lax.div(j, per), i, lax.rem(j, per))))
            out_shape.append(jax.ShapeDtypeStruct((N // chunk, M, chunk), dt))
        elif kind == "n":
            out_specs.append(pl.BlockSpec((tm, tn), lambda i, j, k: (i, j)))
            out_shape.append(jax.ShapeDtypeStruct((M, N), dt))
        else:
            assert tn == N
            out_specs.append(pl.BlockSpec((tm, 1), lambda i, j, k: (i, 0)))
            out_shape.append(jax.ShapeDtypeStruct((M, 1), dt))
    ne, no = len(extras), len(outs)
    deps = []
    if dep is not None:
        in_specs.append(_full(dep.shape))
        deps = [dep]
    dims = _DIMS[mode]
    if epilogue is None:
        epilogue = lambda acc: (acc,) * no

    def body(*refs):
        a_ref, b_ref = refs[0], refs[1]
        ex = refs[2:2 + ne]
        out = refs[len(refs) - 1 - no:len(refs) - 1]
        acc = refs[-1]
        k = pl.program_id(2)
        part = lax.dot_general(a_ref[...].astype(BF16), b_ref[...].astype(BF16), dims, preferred_element_type=F32)

        def finish(total):
            res = epilogue(total, *[e[...] for e in ex])
            for o, r in zip(out, res):
                o[...] = r.astype(o.dtype)

        if nk == 1:
            finish(part)
            return

        @pl.when(k == 0)
        def _():
            acc[...] = part

        @pl.when((k > 0) & (k < nk - 1))
        def _():
            acc[...] += part

        @pl.when(k == nk - 1)
        def _():
            finish(acc[...] + part)

    res = pl.pallas_call(
        body, name=name, grid=(M // tm, N // tn, nk), in_specs=in_specs, out_specs=out_specs, out_shape=out_shape,
        scratch_shapes=[pltpu.VMEM((tm, tn), F32)],
        compiler_params=_params(("parallel", "parallel", "arbitrary")),
    )(a, b, *[e[0] for e in extras], *deps)
    return res


def _ln_epilogue(alpha):
    def epi(acc, x, g, b):
        r = alpha * x + acc
        mu = jnp.mean(r, axis=-1, keepdims=True)
        d = r - mu
        var = jnp.mean(d * d, axis=-1, keepdims=True)
        rstd = lax.rsqrt(var + EPS)
        xhat = d * rstd
        y = xhat * g + b
        return y, y, xhat, rstd
    return epi


def _mm_ln(a, w, x, g, b, alpha, name):
    return _mm(a, w, "nn", [(F32, "n"), (BF16, "n"), (F32, "n"), (F32, "1")], name,
               extras=[(x, "tile"), (g, "row"), (b, "row")], epilogue=_ln_epilogue(alpha), full_n=True)


def _mm_axpy(a, w, mode, r, alpha, name, dep=None):
    return _mm(a, w, mode, [(F32, "n")], name, extras=[(r, "tile")],
               epilogue=lambda acc, rv: (acc + alpha * rv,), dep=dep)[0]


def _ln_bwd(dy, xhat, rstd, g, name):
    S, D = dy.shape
    tm = _tile(S, (ROW_TILE, 256, 128))

    def body(dy_ref, xh_ref, rs_ref, g_ref, dr_ref, dr16_ref, dg_ref, db_ref):
        @pl.when(pl.program_id(0) == 0)
        def _():
            dg_ref[...] = jnp.zeros_like(dg_ref)
            db_ref[...] = jnp.zeros_like(db_ref)

        dyv, xh = dy_ref[...], xh_ref[...]
        dxh = dyv * g_ref[...]
        m1 = jnp.mean(dxh, axis=-1, keepdims=True)
        m2 = jnp.mean(dxh * xh, axis=-1, keepdims=True)
        dr = rs_ref[...] * (dxh - m1 - xh * m2)
        dr_ref[...] = dr
        dr16_ref[...] = dr.astype(BF16)
        dg_ref[...] += jnp.sum(dyv * xh, axis=0, keepdims=True)
        db_ref[...] += jnp.sum(dyv, axis=0, keepdims=True)

    row = pl.BlockSpec((tm, D), lambda i: (i, 0))
    return pl.pallas_call(
        body, name=name, grid=(S // tm,),
        in_specs=[row, row, pl.BlockSpec((tm, 1), lambda i: (i, 0)), _full((1, D))],
        out_specs=[row, row, _full((1, D)), _full((1, D))],
        out_shape=[jax.ShapeDtypeStruct((S, D), F32), jax.ShapeDtypeStruct((S, D), BF16),
                   jax.ShapeDtypeStruct((1, D), F32), jax.ShapeDtypeStruct((1, D), F32)],
        compiler_params=_params(("arbitrary",)),
    )(dy, xhat, rstd, g)


def _rms_fwd(h, off, width, g, name):
    S = h.shape[0]
    tm = _tile(S, (ROW_TILE, 256, 128))

    def body(c_ref, g_ref, y_ref, r_ref):
        c = c_ref[...]
        r = lax.rsqrt(jnp.mean(c * c, axis=-1, keepdims=True) + EPS)
        y_ref[...] = (c * r * g_ref[...]).astype(BF16)
        r_ref[...] = r

    return pl.pallas_call(
        body, name=name, grid=(S // tm,),
        in_specs=[_cols(tm, width, off), _full((1, width))],
        out_specs=[pl.BlockSpec((tm, width), lambda i: (i, 0)), pl.BlockSpec((tm, 1), lambda i: (i, 0))],
        out_shape=[jax.ShapeDtypeStruct((S, width), BF16), jax.ShapeDtypeStruct((S, 1), F32)],
        compiler_params=_params(("parallel",)),
    )(h, g)


def _rms_bwd(dy, h, off, width, rstd, g, name):
    S = h.shape[0]
    tm = _tile(S, (ROW_TILE, 256, 128))

    def body(dy_ref, c_ref, r_ref, g_ref, dc_ref, dg_ref):
        @pl.when(pl.program_id(0) == 0)
        def _():
            dg_ref[...] = jnp.zeros_like(dg_ref)

        dyv, c, r = dy_ref[...], c_ref[...], r_ref[...]
        dyg = dyv * g_ref[...]
        m = jnp.mean(dyg * c, axis=-1, keepdims=True)
        dc_ref[...] = (r * dyg - c * (r * r * r) * m).astype(BF16)
        dg_ref[...] += jnp.sum(dyv * c * r, axis=0, keepdims=True)

    return pl.pallas_call(
        body, name=name, grid=(S // tm,),
        in_specs=[pl.BlockSpec((tm, width), lambda i: (i, 0)), _cols(tm, width, off),
                  pl.BlockSpec((tm, 1), lambda i: (i, 0)), _full((1, width))],
        out_specs=[pl.BlockSpec((tm, width), lambda i: (i, 0)), _full((1, width))],
        out_shape=[jax.ShapeDtypeStruct((S, width), BF16), jax.ShapeDtypeStruct((1, width), F32)],
        compiler_params=_params(("arbitrary",)),
    )(dy, h, rstd, g)


def _loss(y, target, name):
    S, D = y.shape
    tm = _tile(S, (ROW_TILE, 256, 128))

    def body(y_ref, t_ref, l_ref, dy_ref):
        @pl.when(pl.program_id(0) == 0)
        def _():
            l_ref[...] = jnp.zeros_like(l_ref)

        err = y_ref[...] - t_ref[...]
        dy_ref[...] = err * (1.0 / D)
        per_tok = jnp.mean(err * err, axis=-1, keepdims=True)
        l_ref[...] += 0.5 * jnp.sum(per_tok, axis=0, keepdims=True)

    row = pl.BlockSpec((tm, D), lambda i: (i, 0))
    return pl.pallas_call(
        body, name=name, grid=(S // tm,), in_specs=[row, row], out_specs=[_full((1, 1)), row],
        out_shape=[jax.ShapeDtypeStruct((1, 1), F32), jax.ShapeDtypeStruct((S, D), F32)],
        compiler_params=_params(("arbitrary",)),
    )(y, target)


LANES = 128
LOG2E = 1.4426950408889634


def _rope(x, off, width, cs2, sn2, bwd, name):
    S = cs2.shape[0]
    tm = _tile(S, (ROW_TILE, 256, 128))
    stacked = x.ndim == 3
    half = MLA_ROPE // 2
    assert width % LANES == 0 and MLA_ROPE * 2 == LANES

    def rot(v):
        lane = lax.broadcasted_iota(jnp.int32, v.shape, 1)
        return jnp.where((lane & (MLA_ROPE - 1)) < half, -pltpu.roll(v, LANES - half, 1), pltpu.roll(v, half, 1))

    def body(x_ref, c_ref, s_ref, y_ref):
        c, s = c_ref[...], s_ref[...]
        for g in range(width // LANES):
            cols = slice(g * LANES, (g + 1) * LANES)
            v = jnp.sum(x_ref[...], axis=0) if stacked else x_ref[:, cols].astype(F32)
            y = v * c - rot(v * s) if bwd else v * c + rot(v) * s
            y_ref[:, cols] = y.astype(BF16)

    row = pl.BlockSpec((tm, LANES), lambda i: (i, 0))
    x_spec = pl.BlockSpec((x.shape[0], tm, LANES), lambda i: (0, i, 0)) if stacked else _cols(tm, width, off)
    return pl.pallas_call(
        body, name=name, grid=(S // tm,), in_specs=[x_spec, row, row],
        out_specs=pl.BlockSpec((tm, width), lambda i: (i, 0)), out_shape=jax.ShapeDtypeStruct((S, width), BF16),
        compiler_params=_params(("parallel",)),
    )(x, cs2, sn2)


def _swa_mask(n, rows):
    blk = SWA_BLOCK
    row = lax.broadcasted_iota(jnp.int32, (rows, 2 * blk), 0) & (blk - 1)
    col = lax.broadcasted_iota(jnp.int32, (rows, 2 * blk), 1)
    rel = row + blk - col
    return (rel >= 0) & (rel < blk) & ((n > 0) | (col >= blk))


def _swa_specs(off_q, off_k, off_v, stacked):
    blk, aq, akv = SWA_BLOCK, SWA_Q_HEADS * SWA_HEAD_DIM, SWA_KV_HEADS * SWA_HEAD_DIM
    grp = SWA_Q_HEADS // SWA_KV_HEADS
    assert off_q % aq == 0 and off_k % akv == 0 and off_v % akv == 0 and blk & (blk - 1) == 0
    prev = lambda off: pl.BlockSpec((blk, akv), lambda n: (jnp.maximum(n - 1, 0), off // akv))
    cur = lambda off: pl.BlockSpec((blk, akv), lambda n: (n, off // akv))
    sink = _full((SWA_KV_HEADS, grp * blk, 1)) if stacked else pl.BlockSpec(memory_space=pltpu.SMEM)
    return [sink, _cols(blk, aq, off_q), prev(off_k), cur(off_k), prev(off_v), cur(off_v)]


def _swa_sinks(sinks):
    grp = SWA_Q_HEADS // SWA_KV_HEADS
    return jnp.repeat(sinks.reshape(SWA_KV_HEADS, grp), SWA_BLOCK, axis=1)[:, :, None]


def _swa_stack(x, kv):
    hd, grp = SWA_HEAD_DIM, SWA_Q_HEADS // SWA_KV_HEADS
    return jnp.concatenate([x[:, (kv * grp + g) * hd:(kv * grp + g + 1) * hd] for g in range(grp)], axis=0)


def _swa_fwd(h, off_q, off_k, off_v, sinks, name):
    S = h.shape[0]
    blk, hd, nh, nkv = SWA_BLOCK, SWA_HEAD_DIM, SWA_Q_HEADS, SWA_KV_HEADS
    grp = nh // nkv
    aq = nh * hd
    scale = hd ** -0.5

    def body(sink_ref, q_ref, kp_ref, kc_ref, vp_ref, vc_ref, o_ref, lse_ref):
        valid = _swa_mask(pl.program_id(0), blk)
        q = q_ref[...].astype(BF16)
        k2 = jnp.concatenate([kp_ref[...], kc_ref[...]], axis=0).astype(BF16)
        v2 = jnp.concatenate([vp_ref[...], vc_ref[...]], axis=0).astype(BF16)
        for hh in range(nh):
            kv = hh // grp
            qh = q[:, hh * hd:(hh + 1) * hd]
            kh = k2[:, kv * hd:(kv + 1) * hd]
            vh = v2[:, kv * hd:(kv + 1) * hd]
            s = lax.dot_general(qh, kh, _DIMS["nt"], preferred_element_type=F32) * scale
            s = jnp.where(valid, s, MASK)
            sk = sink_ref[hh]
            m = jnp.maximum(jnp.max(s, axis=1, keepdims=True), sk)
            p = jnp.exp(s - m)
            l = jnp.sum(p, axis=1, keepdims=True) + jnp.exp(sk - m)
            o_ref[:, hh * hd:(hh + 1) * hd] = jnp.dot((p / l).astype(BF16), vh, preferred_element_type=F32).astype(BF16)
            lse_ref[:, hh:hh + 1] = m + jnp.log(l)

    return pl.pallas_call(
        body, name=name, grid=(S // blk,), in_specs=_swa_specs(off_q, off_k, off_v, False),
        out_specs=[pl.BlockSpec((blk, aq), lambda n: (n, 0)), pl.BlockSpec((blk, nh), lambda n: (n, 0))],
        out_shape=[jax.ShapeDtypeStruct((S, aq), BF16), jax.ShapeDtypeStruct((S, nh), F32)],
        compiler_params=_params(("parallel",)),
    )(sinks, h, h, h, h, h)


def _swa_bwd(h, off_q, off_k, off_v, sinks, dout, lse, name):
    S = h.shape[0]
    blk, hd, nh, nkv = SWA_BLOCK, SWA_HEAD_DIM, SWA_Q_HEADS, SWA_KV_HEADS
    grp = nh // nkv
    aq, akv = nh * hd, nkv * hd
    scale = hd ** -0.5

    def body(sink_ref, q_ref, kp_ref, kc_ref, vp_ref, vc_ref, do_ref, lse_ref, dq_ref, dk_ref, dv_ref, ds_ref):
        n = pl.program_id(0)

        @pl.when(n == 0)
        def _():
            dk_ref[...] = jnp.zeros_like(dk_ref)
            dv_ref[...] = jnp.zeros_like(dv_ref)
            ds_ref[...] = jnp.zeros_like(ds_ref)

        valid = _swa_mask(n, grp * blk)
        q = q_ref[...].astype(BF16)
        k2 = jnp.concatenate([kp_ref[...], kc_ref[...]], axis=0).astype(BF16)
        v2 = jnp.concatenate([vp_ref[...], vc_ref[...]], axis=0).astype(BF16)
        do = do_ref[...]
        lane = lax.broadcasted_iota(jnp.int32, (1, 128), 1)
        dsink = jnp.zeros((1, 128), F32)
        cur = pl.ds(pl.multiple_of(n * blk, blk), blk)
        prev = pl.ds(pl.multiple_of(jnp.maximum(n - 1, 0) * blk, blk), blk)
        for kv in range(nkv):
            kh = k2[:, kv * hd:(kv + 1) * hd]
            vh = v2[:, kv * hd:(kv + 1) * hd]
            qs = _swa_stack(q, kv)
            dos = _swa_stack(do, kv)
            dos16 = dos.astype(BF16)
            lse = jnp.concatenate([lse_ref[:, kv * grp + g:kv * grp + g + 1] for g in range(grp)], axis=0)
            s = lax.dot_general(qs, kh, _DIMS["nt"], preferred_element_type=F32) * scale
            s = jnp.where(valid, s, MASK)
            p = jnp.exp(s - lse)
            p16 = p.astype(BF16)
            o = jnp.dot(p16, vh, preferred_element_type=F32)
            delta = jnp.sum(dos * o, axis=1, keepdims=True)
            dp = lax.dot_general(dos16, vh, _DIMS["nt"], preferred_element_type=F32)
            ds16 = (p * (dp - delta) * scale).astype(BF16)
            dqs = jnp.dot(ds16, kh, preferred_element_type=F32).astype(BF16)
            dk_acc = lax.dot_general(ds16, qs, _DIMS["tn"], preferred_element_type=F32)
            dv_acc = lax.dot_general(p16, dos16, _DIMS["tn"], preferred_element_type=F32)
            dsk = jnp.exp(sink_ref[kv] - lse) * delta
            for g in range(grp):
                hh = kv * grp + g
                dq_ref[:, hh * hd:(hh + 1) * hd] = dqs[g * blk:(g + 1) * blk]
                dsink += jnp.where(lane == hh, -jnp.sum(dsk[g * blk:(g + 1) * blk], axis=0, keepdims=True), 0.0)
            cols = slice(kv * hd, (kv + 1) * hd)
            dk_ref[cur, cols] += dk_acc[blk:]
            dv_ref[cur, cols] += dv_acc[blk:]

            @pl.when(n > 0)
            def _():
                dk_ref[prev, cols] += dk_acc[:blk]
                dv_ref[prev, cols] += dv_acc[:blk]

        ds_ref[...] += dsink

    return pl.pallas_call(
        body, name=name, grid=(S // blk,),
        in_specs=_swa_specs(off_q, off_k, off_v, True) + [pl.BlockSpec((blk, aq), lambda n: (n, 0)),
                                                    pl.BlockSpec((blk, nh), lambda n: (n, 0))],
        out_specs=[pl.BlockSpec((blk, aq), lambda n: (n, 0)), _full((S, akv)), _full((S, akv)), _full((1, 128))],
        out_shape=[jax.ShapeDtypeStruct((S, aq), BF16), jax.ShapeDtypeStruct((S, akv), F32),
                   jax.ShapeDtypeStruct((S, akv), F32), jax.ShapeDtypeStruct((1, 128), F32)],
        compiler_params=_params(("arbitrary",)),
    )(_swa_sinks(sinks), h, h, h, h, h, dout, lse)


def _causal(i, j, t):
    row = i * t + lax.broadcasted_iota(jnp.int32, (t, t), 0)
    col = j * t + lax.broadcasted_iota(jnp.int32, (t, t), 1)
    return col <= row


def _pair_rope(qr, hh):
    lane = lax.broadcasted_iota(jnp.int32, qr.shape, 1)
    return jnp.where(lane < MLA_ROPE, qr if hh == 0 else pltpu.roll(qr, MLA_ROPE, 1), jnp.zeros_like(qr))


def _mla_fwd(qf, qr, kvf, kr, name):
    S = qr.shape[0]
    H, dn, dv = MLA_HEADS, MLA_NOPE, MLA_V
    assert H % 2 == 0 and dn == LANES and dv == LANES and 2 * MLA_ROPE == LANES
    t = _tile(S, (MLA_FWD_TILE, 512, 256, 128))
    nq = S // t
    scale = (MLA_NOPE + MLA_ROPE) ** -0.5

    def body(qn_ref, qr_ref, kn_ref, kr_ref, v_ref, o_ref, o16_ref, lse_ref, *state):
        i = pl.program_id(1)
        for hh in range(2):
            m_s, l_s, acc_s = state[3 * hh:3 * hh + 3]
            m_s[...] = jnp.full_like(m_s, -jnp.inf)
            l_s[...] = jnp.zeros_like(l_s)
            acc_s[...] = jnp.zeros_like(acc_s)

        def block(j, masked):
            rows = pl.ds(pl.multiple_of(j * t, t), t)
            for hh in range(2):
                m_s, l_s, acc_s = state[3 * hh:3 * hh + 3]
                cols = slice(hh * LANES, (hh + 1) * LANES)
                q = jnp.concatenate([qn_ref[:, cols].astype(BF16), _pair_rope(qr_ref[...], hh)], axis=1)
                k = jnp.concatenate([kn_ref[rows, cols], kr_ref[rows, :]], axis=1)
                s = lax.dot_general(q, k, _DIMS["nt"], preferred_element_type=F32) * (scale * LOG2E)
                if masked:
                    s = jnp.where(_causal(0, 0, t), s, MASK)
                m_old = m_s[...]
                m_new = jnp.maximum(m_old, jnp.max(s, axis=1, keepdims=True))
                alpha = jnp.exp2(m_old - m_new)
                p = jnp.exp2(s - m_new)
                l_s[...] = alpha * l_s[...] + jnp.sum(p, axis=1, keepdims=True)
                acc_s[...] = alpha * acc_s[...] + jnp.dot(p.astype(BF16), v_ref[rows, cols], preferred_element_type=F32)
                m_s[...] = m_new

        def full_block(j, carry):
            block(j, False)
            return carry

        lax.fori_loop(0, i, full_block, 0)
        block(i, True)
        for hh in range(2):
            m_s, l_s, acc_s = state[3 * hh:3 * hh + 3]
            out = acc_s[...] / l_s[...]
            o_ref[:, hh * LANES:(hh + 1) * LANES] = out
            o16_ref[:, hh * LANES:(hh + 1) * LANES] = out.astype(BF16)
            lse_ref[hh] = (m_s[...] + jnp.log2(l_s[...])) * (1.0 / LOG2E)

    P = H // 2
    return pl.pallas_call(
        body, name=name, grid=(P, nq),
        in_specs=[pl.BlockSpec((t, 2 * LANES), lambda p, i: (i, p)), pl.BlockSpec((t, LANES), lambda p, i: (i, p)),
                  pl.BlockSpec((S, 2 * LANES), lambda p, i: (0, p)), pl.BlockSpec((S, LANES), lambda p, i: (0, 0)),
                  pl.BlockSpec((S, 2 * LANES), lambda p, i: (0, P + p))],
        out_specs=[pl.BlockSpec((t, 2 * LANES), lambda p, i: (i, p)), pl.BlockSpec((t, 2 * LANES), lambda p, i: (i, p)),
                   pl.BlockSpec((2, t, 1), lambda p, i: (p, i, 0))],
        out_shape=[jax.ShapeDtypeStruct((S, H * dv), F32), jax.ShapeDtypeStruct((S, H * dv), BF16),
                   jax.ShapeDtypeStruct((H, S, 1), F32)],
        scratch_shapes=[pltpu.VMEM((t, 1), F32), pltpu.VMEM((t, 1), F32), pltpu.VMEM((t, dv), F32)] * 2,
        compiler_params=_params(("parallel", "arbitrary")),
    )(qf, qr, kvf, kr, kvf)


def _mla_bwd(qf, qr, kvf, kr, do, out, lse, name):
    S = qr.shape[0]
    H, dv = MLA_HEADS, MLA_V
    P = H // 2
    t = _tile(S, (MLA_TILE, 256, 128))
    nq = S // t
    scale = (MLA_NOPE + MLA_ROPE) ** -0.5

    def body(qn_ref, qr_ref, kn_ref, kr_ref, v_ref, do_ref, o_ref, lse_ref, dqn_ref, dqr_ref, dkn_ref, dv_ref, dkr_ref,
             dkn_s, dv_s, dkr_s):
        j, i = pl.program_id(1), pl.program_id(2)

        @pl.when((j == 0) & (i == 0))
        def _():
            dqn_ref[...] = jnp.zeros_like(dqn_ref)
            dqr_ref[...] = jnp.zeros_like(dqr_ref)

        @pl.when(i == j)
        def _():
            dkn_s[...] = jnp.zeros_like(dkn_s)
            dv_s[...] = jnp.zeros_like(dv_s)
            dkr_s[...] = jnp.zeros_like(dkr_s)

        def block(masked):
            rows = pl.ds(pl.multiple_of(i * t, t), t)
            krv = kr_ref[...]
            for hh in range(2):
                cols = slice(hh * LANES, (hh + 1) * LANES)
                q = jnp.concatenate([qn_ref[:, cols].astype(BF16), _pair_rope(qr_ref[...], hh)], axis=1)
                k = jnp.concatenate([kn_ref[:, cols], krv], axis=1)
                vv, dof = v_ref[:, cols], do_ref[:, cols]
                dov = dof.astype(BF16)
                delta = jnp.sum(dof * o_ref[:, cols], axis=1, keepdims=True)
                s = lax.dot_general(q, k, _DIMS["nt"], preferred_element_type=F32) * (scale * LOG2E)
                if masked:
                    s = jnp.where(_causal(0, 0, t), s, MASK)
                p = jnp.exp2(s - lse_ref[hh] * LOG2E)
                p16 = p.astype(BF16)
                dp = lax.dot_general(dov, vv, _DIMS["nt"], preferred_element_type=F32)
                ds16 = (p * (dp - delta) * scale).astype(BF16)
                dv_s[:, cols] += lax.dot_general(p16, dov, _DIMS["tn"], preferred_element_type=F32)
                dk = lax.dot_general(ds16, q, _DIMS["tn"], preferred_element_type=F32)
                dkn_s[:, cols] += dk[:, :LANES]
                dkr_s[...] += dk[:, LANES:]
                dq = jnp.dot(ds16, k, preferred_element_type=F32)
                dqn_ref[rows, cols] += dq[:, :LANES]
                dqr = dq[:, LANES:]
                dqr_ref[rows, :] += dqr if hh == 0 else pltpu.roll(dqr, MLA_ROPE, 1)

        @pl.when(i == j)
        def _():
            block(True)

        @pl.when(i > j)
        def _():
            block(False)

        @pl.when(i == nq - 1)
        def _():
            dkn_ref[...] = dkn_s[...]
            dv_ref[...] = dv_s[...]
            dkr_ref[...] = dkr_s[...]

    qi = lambda i, j: jnp.maximum(i, j)
    return pl.pallas_call(
        body, name=name, grid=(P, nq, nq),
        in_specs=[pl.BlockSpec((t, 2 * LANES), lambda p, j, i: (qi(i, j), p)), pl.BlockSpec((t, LANES), lambda p, j, i: (qi(i, j), p)),
                  pl.BlockSpec((t, 2 * LANES), lambda p, j, i: (j, p)), pl.BlockSpec((t, LANES), lambda p, j, i: (j, 0)),
                  pl.BlockSpec((t, 2 * LANES), lambda p, j, i: (j, P + p)),
                  pl.BlockSpec((t, 2 * LANES), lambda p, j, i: (qi(i, j), p)),
                  pl.BlockSpec((t, 2 * LANES), lambda p, j, i: (qi(i, j), p)),
                  pl.BlockSpec((2, t, 1), lambda p, j, i: (p, qi(i, j), 0))],
        out_specs=[pl.BlockSpec((S, 2 * LANES), lambda p, j, i: (0, p)), pl.BlockSpec((S, LANES), lambda p, j, i: (0, p)),
                   pl.BlockSpec((t, 2 * LANES), lambda p, j, i: (j, p)), pl.BlockSpec((t, 2 * LANES), lambda p, j, i: (j, p)),
                   pl.BlockSpec((None, t, LANES), lambda p, j, i: (p, j, 0))],
        out_shape=[jax.ShapeDtypeStruct((S, H * MLA_NOPE), F32), jax.ShapeDtypeStruct((S, H * MLA_ROPE), F32),
                   jax.ShapeDtypeStruct((S, H * MLA_NOPE), F32), jax.ShapeDtypeStruct((S, H * dv), F32),
                   jax.ShapeDtypeStruct((P, S, LANES), F32)],
        scratch_shapes=[pltpu.VMEM((t, 2 * LANES), F32), pltpu.VMEM((t, 2 * LANES), F32), pltpu.VMEM((t, LANES), F32)],
        compiler_params=_params(("parallel", "arbitrary", "arbitrary")),
    )(qf, qr, kvf, kr, kvf, do, out, lse)


def _sgu_norm(hv, lg, lb):
    vg = _gelu(hv)
    mu = jnp.mean(vg, axis=-1, keepdims=True)
    d = vg - mu
    rstd = lax.rsqrt(jnp.mean(d * d, axis=-1, keepdims=True) + EPS)
    xhat = d * rstd
    return xhat, rstd, xhat * lg + lb


def _sgu_fwd(h, off_u, off_v, lg, lb, w16, bt, name):
    S = h.shape[0]
    T, G, C = SGU_CHUNK, SGU_GROUPS, SGU_DIM
    W = G * C

    def body(hu_ref, hv_ref, lg_ref, lb_ref, w_ref, bt_ref, y_ref):
        u = _gelu(hu_ref[...])
        _, _, vn = _sgu_norm(hv_ref[...], lg_ref[...], lb_ref[...])
        vn16 = vn.astype(BF16)
        for g in range(G):
            cols = slice(g * C, (g + 1) * C)
            mixed = jnp.dot(w_ref[g], vn16[:, cols], preferred_element_type=F32) + bt_ref[:, g:g + 1]
            y_ref[:, cols] = (u[:, cols] * mixed).astype(BF16)

    return pl.pallas_call(
        body, name=name, grid=(S // T,),
        in_specs=[_cols(T, W, off_u), _cols(T, W, off_v), _full((1, W)), _full((1, W)), _full((G, T, T)), _full((T, G))],
        out_specs=pl.BlockSpec((T, W), lambda n: (n, 0)),
        out_shape=jax.ShapeDtypeStruct((S, W), BF16),
        compiler_params=_params(("parallel",)),
    )(h, h, lg, lb, w16, bt)


def _sgu_bwd(h, off_u, off_v, lg, lb, w16, bt, dy, name):
    S = h.shape[0]
    T, G, C = SGU_CHUNK, SGU_GROUPS, SGU_DIM
    W = G * C
    nc = S // T

    def body(hu_ref, hv_ref, lg_ref, lb_ref, w_ref, bt_ref, dy_ref, dhu_ref, dhv_ref, dw_ref, db_ref, dlg_ref, dlb_ref,
             dmix_s, dvn_s):
        n = pl.program_id(0)

        @pl.when(n == 0)
        def _():
            dw_ref[...] = jnp.zeros_like(dw_ref)
            dlg_ref[...] = jnp.zeros_like(dlg_ref)
            dlb_ref[...] = jnp.zeros_like(dlb_ref)
            dmix_s[...] = jnp.zeros_like(dmix_s)

        hu, hv, lgv = hu_ref[...], hv_ref[...], lg_ref[...]
        u = _gelu(hu)
        xhat, rstd, vn = _sgu_norm(hv, lgv, lb_ref[...])
        vn16 = vn.astype(BF16)
        dyv = dy_ref[...]
        dmixed = dyv * u
        dmix_s[...] += dmixed
        dmixed16 = dmixed.astype(BF16)
        for g in range(G):
            cols = slice(g * C, (g + 1) * C)
            mixed = jnp.dot(w_ref[g], vn16[:, cols], preferred_element_type=F32) + bt_ref[:, g:g + 1]
            dhu_ref[:, cols] = (dyv[:, cols] * mixed * _gelu_grad(hu[:, cols])).astype(BF16)
            dvn_s[:, cols] = lax.dot_general(w_ref[g], dmixed16[:, cols], _DIMS["tn"], preferred_element_type=F32)
            dw_ref[g] += lax.dot_general(dmixed16[:, cols], vn16[:, cols], _DIMS["nt"], preferred_element_type=F32)
        dvn = dvn_s[...]
        dlg_ref[...] += jnp.sum(dvn * xhat, axis=0, keepdims=True)
        dlb_ref[...] += jnp.sum(dvn, axis=0, keepdims=True)
        dxh = dvn * lgv
        m1 = jnp.mean(dxh, axis=-1, keepdims=True)
        m2 = jnp.mean(dxh * xhat, axis=-1, keepdims=True)
        dvg = rstd * (dxh - m1 - xhat * m2)
        dhv_ref[...] = (dvg * _gelu_grad(hv)).astype(BF16)

        @pl.when(n == nc - 1)
        def _():
            tril = lax.broadcasted_iota(jnp.int32, (T, T), 1) <= lax.broadcasted_iota(jnp.int32, (T, T), 0)
            lane = lax.broadcasted_iota(jnp.int32, (T, 128), 1)
            db = jnp.zeros((T, 128), F32)
            for g in range(G):
                dw_ref[g] = jnp.where(tril, dw_ref[g], 0.0)
                db += jnp.where(lane == g, jnp.sum(dmix_s[:, g * C:(g + 1) * C], axis=1, keepdims=True), 0.0)
            db_ref[...] = db

    row = pl.BlockSpec((T, W), lambda n: (n, 0))
    return pl.pallas_call(
        body, name=name, grid=(nc,),
        in_specs=[_cols(T, W, off_u), _cols(T, W, off_v), _full((1, W)), _full((1, W)), _full((G, T, T)), _full((T, G)), row],
        out_specs=[row, row, _full((G, T, T)), _full((T, 128)), _full((1, W)), _full((1, W))],
        out_shape=[jax.ShapeDtypeStruct((S, W), BF16), jax.ShapeDtypeStruct((S, W), BF16),
                   jax.ShapeDtypeStruct((G, T, T), F32), jax.ShapeDtypeStruct((T, 128), F32),
                   jax.ShapeDtypeStruct((1, W), F32), jax.ShapeDtypeStruct((1, W), F32)],
        scratch_shapes=[pltpu.VMEM((T, W), F32), pltpu.VMEM((T, W), F32)],
        compiler_params=_params(("arbitrary",)),
    )(h, h, lg, lb, w16, bt, dy)


def _merge_fwd(ys, ps, h, bg, name):
    S = h.shape[0]
    D = ps[0].shape[1]
    tm = _tile(S, (1024, 512, 256, 128))
    tn = _tile(D, (256, 128))
    nb = len(ys)

    def body(*refs):
        y_refs, p_refs, l_refs = refs[:nb], refs[nb:2 * nb], refs[2 * nb:3 * nb]
        bg_ref, mg_ref, z_ref = refs[3 * nb:]
        acc = jnp.zeros((tm, tn), F32)
        for b in range(nb):
            z = jnp.dot(y_refs[b][...].astype(BF16), p_refs[b][...], preferred_element_type=F32)
            z_ref[b] = z
            acc += _sigmoid(l_refs[b][...] + bg_ref[b:b + 1, :]) * z
        mg_ref[...] = acc.astype(BF16)

    in_specs = [pl.BlockSpec((tm, y.shape[1]), lambda i, j: (i, 0)) for y in ys]
    in_specs += [pl.BlockSpec((p.shape[0], tn), lambda i, j: (0, j)) for p in ps]
    in_specs += [pl.BlockSpec((tm, tn), functools.partial(lambda i, j, b: (i, b * (D // tn) + j), b=b)) for b in range(nb)]
    in_specs += [pl.BlockSpec((nb, tn), lambda i, j: (0, j))]
    return pl.pallas_call(
        body, name=name, grid=(S // tm, D // tn), in_specs=in_specs,
        out_specs=[pl.BlockSpec((tm, tn), lambda i, j: (i, j)), pl.BlockSpec((nb, tm, tn), lambda i, j: (0, i, j))],
        out_shape=[jax.ShapeDtypeStruct((S, D), BF16), jax.ShapeDtypeStruct((nb, S, D), F32)],
        compiler_params=_params(("parallel", "parallel")),
    )(*ys, *ps, *([h] * nb), bg)


def _merge_bwd(dm, z, h, bg, name):
    nb, S, D = z.shape
    tm = _tile(S, (ROW_TILE, 256, 128))
    tn = _tile(D, (512, 256, 128))

    def body(*refs):
        dm_ref, z_ref = refs[0], refs[1]
        l_refs = refs[2:2 + nb]
        bg_ref, dz_ref, dl_ref, dbg_ref = refs[2 + nb:]

        @pl.when(pl.program_id(1) == 0)
        def _():
            dbg_ref[...] = jnp.zeros_like(dbg_ref)

        dmv = dm_ref[...]
        rows = lax.broadcasted_iota(jnp.int32, (SUBLANES, tn), 0)
        dbg = jnp.zeros((SUBLANES, tn), F32)
        for b in range(nb):
            gt = _sigmoid(l_refs[b][...] + bg_ref[b:b + 1, :])
            dz_ref[b] = (dmv * gt).astype(BF16)
            dl = dmv * z_ref[b] * gt * (1.0 - gt)
            dl_ref[b] = dl.astype(BF16)
            dbg += jnp.where(rows == b, jnp.sum(dl, axis=0, keepdims=True), 0.0)
        dbg_ref[...] += dbg

    in_specs = [pl.BlockSpec((tm, tn), lambda j, i: (i, j)), pl.BlockSpec((nb, tm, tn), lambda j, i: (0, i, j))]
    in_specs += [pl.BlockSpec((tm, tn), functools.partial(lambda j, i, b: (i, b * (D // tn) + j), b=b)) for b in range(nb)]
    in_specs += [pl.BlockSpec((nb, tn), lambda j, i: (0, j))]
    blk3 = pl.BlockSpec((nb, tm, tn), lambda j, i: (0, i, j))
    return pl.pallas_call(
        body, name=name, grid=(D // tn, S // tm), in_specs=in_specs,
        out_specs=[blk3, blk3, pl.BlockSpec((SUBLANES, tn), lambda j, i: (0, j))],
        out_shape=[jax.ShapeDtypeStruct((nb, S, D), BF16), jax.ShapeDtypeStruct((nb, S, D), BF16),
                   jax.ShapeDtypeStruct((SUBLANES, D), F32)],
        compiler_params=_params(("parallel", "arbitrary")),
    )(dm, z, *([h] * nb), bg)


def _shift_down(x, halo, k):
    xr = pltpu.roll(x, k, 0)
    hr = pltpu.roll(halo, k, 0)
    rows = lax.broadcasted_iota(jnp.int32, halo.shape, 0)
    top = jnp.where(rows < k, hr, xr[:SUBLANES])
    return jnp.concatenate([top, xr[SUBLANES:]], axis=0)


def _shift_up(x, halo, k):
    tm = x.shape[0]
    xr = pltpu.roll(x, tm - k, 0)
    hr = pltpu.roll(halo, SUBLANES - k, 0)
    rows = lax.broadcasted_iota(jnp.int32, halo.shape, 0)
    bot = jnp.where(rows >= SUBLANES - k, hr, xr[tm - SUBLANES:])
    return jnp.concatenate([xr[:tm - SUBLANES], bot], axis=0)


def _conv_tiles(S, F):
    return _tile(S, (GLU_ROWS, 512, 256, 128)), _tile(F, (512, 256, 128))


def _conv_in_specs(tm, tn, F):
    r8 = tm // SUBLANES
    nf = F // tn
    specs = []
    for half in range(2):
        specs.append(pl.BlockSpec((tm, tn), functools.partial(lambda j, i, o: (i, o + j), o=half * nf)))
        specs.append(pl.BlockSpec((SUBLANES, tn), functools.partial(lambda j, i, o: (jnp.maximum(i * r8 - 1, 0), o + j), o=half * nf)))
    for half in range(2):
        specs.append(pl.BlockSpec((3, tn), functools.partial(lambda j, i, o: (0, o + j), o=half * nf)))
        specs.append(pl.BlockSpec((1, tn), functools.partial(lambda j, i, o: (0, o + j), o=half * nf)))
    return specs


def _conv_apply(x, halo, w, b, first):
    halo = jnp.where(first, 0.0, halo)
    x1 = _shift_down(x, halo, 1)
    x2 = _shift_down(x, halo, 2)
    return b + x2 * w[0:1, :] + x1 * w[1:2, :] + x * w[2:3, :], x1, x2


def _glu_fwd(up, cw, cb, name):
    S, F2 = up.shape
    F = F2 // 2
    tm, tn = _conv_tiles(S, F)

    def body(ug, hg, uv, hv, wg, bgr, wv, bvr, a_ref):
        first = pl.program_id(1) == 0
        cg, _, _ = _conv_apply(ug[...], hg[...], wg[...], bgr[...], first)
        cv, _, _ = _conv_apply(uv[...], hv[...], wv[...], bvr[...], first)
        a_ref[...] = (cg * _sigmoid(cg) * cv).astype(BF16)

    return pl.pallas_call(
        body, name=name, grid=(F // tn, S // tm), in_specs=_conv_in_specs(tm, tn, F),
        out_specs=pl.BlockSpec((tm, tn), lambda j, i: (i, j)),
        out_shape=jax.ShapeDtypeStruct((S, F), BF16),
        compiler_params=_params(("parallel", "parallel")),
    )(up, up, up, up, cw, cb, cw, cb)


def _glu_bwd(up, cw, cb, da, name):
    S, F2 = up.shape
    F = F2 // 2
    tm, tn = _conv_tiles(S, F)

    def body(ug, hg, uv, hv, wg, bgr, wv, bvr, da_ref, dg_ref, dv_ref, sg_ref, sv_ref):
        i = pl.program_id(1)

        @pl.when(i == 0)
        def _():
            sg_ref[...] = jnp.zeros_like(sg_ref)
            sv_ref[...] = jnp.zeros_like(sv_ref)

        first = i == 0
        xg, xv = ug[...], uv[...]
        cg, xg1, xg2 = _conv_apply(xg, hg[...], wg[...], bgr[...], first)
        cv, xv1, xv2 = _conv_apply(xv, hv[...], wv[...], bvr[...], first)
        dav = da_ref[...]
        sg = _sigmoid(cg)
        dcv = dav * cg * sg
        dcg = dav * cv * sg * (1.0 + cg * (1.0 - sg))
        dg_ref[...] = dcg
        dv_ref[...] = dcv
        rows = lax.broadcasted_iota(jnp.int32, (SUBLANES, tn), 0)

        def stats(dc, x, x1, x2):
            acc = jnp.zeros((SUBLANES, tn), F32)
            for r, val in enumerate((dc * x2, dc * x1, dc * x, dc)):
                acc += jnp.where(rows == r, jnp.sum(val, axis=0, keepdims=True), 0.0)
            return acc

        sg_ref[...] += stats(dcg, xg, xg1, xg2)
        sv_ref[...] += stats(dcv, xv, xv1, xv2)

    tile = pl.BlockSpec((tm, tn), lambda j, i: (i, j))
    stat = pl.BlockSpec((SUBLANES, tn), lambda j, i: (0, j))
    return pl.pallas_call(
        body, name=name, grid=(F // tn, S // tm), in_specs=_conv_in_specs(tm, tn, F) + [tile],
        out_specs=[tile, tile, stat, stat],
        out_shape=[jax.ShapeDtypeStruct((S, F), F32), jax.ShapeDtypeStruct((S, F), F32),
                   jax.ShapeDtypeStruct((SUBLANES, F), F32), jax.ShapeDtypeStruct((SUBLANES, F), F32)],
        compiler_params=_params(("parallel", "arbitrary")),
    )(up, up, up, up, cw, cb, cw, cb, da)


def _conv_bwd(dcg, dcv, w, name):
    S, F = dcg.shape
    tm, tn = _tile(S, (CONV_BWD_ROWS, 1024, 512, 256, 128)), _conv_tiles(S, F)[1]
    r8 = tm // SUBLANES
    ni = S // tm
    nf = F // tn

    def body(g_ref, gh_ref, v_ref, vh_ref, w_ref, o_ref):
        gate = pl.program_id(0) == 0
        x = jnp.where(gate, g_ref[...], v_ref[...])
        halo = jnp.where(gate, gh_ref[...], vh_ref[...])
        halo = jnp.where(pl.program_id(2) == ni - 1, 0.0, halo)
        wv = w_ref[...]
        o_ref[...] = (x * wv[2:3, :] + _shift_up(x, halo, 1) * wv[1:2, :] + _shift_up(x, halo, 2) * wv[0:1, :]).astype(BF16)

    def tile(half):
        return pl.BlockSpec((tm, tn), lambda h, j, i: (jnp.where(h == half, i, 0), jnp.where(h == half, j, 0)))

    def below(half):
        return pl.BlockSpec((SUBLANES, tn), lambda h, j, i: (
            jnp.where(h == half, jnp.minimum((i + 1) * r8, S // SUBLANES - 1), 0), jnp.where(h == half, j, 0)))

    return pl.pallas_call(
        body, name=name, grid=(2, nf, ni),
        in_specs=[tile(0), below(0), tile(1), below(1), pl.BlockSpec((3, tn), lambda h, j, i: (0, h * nf + j))],
        out_specs=pl.BlockSpec((tm, tn), lambda h, j, i: (i, h * nf + j)),
        out_shape=jax.ShapeDtypeStruct((S, 2 * F), BF16),
        compiler_params=_params(("parallel", "parallel", "parallel")),
    )(dcg, dcg, dcv, dcv, w)


def _adamw(slot_list, own_list, me, w, m, v, name, dep=None):
    L = len(slot_list)
    P, K, C = slot_list[0].shape
    tr = _tile(K, (256, 128, 64, 32, 16))
    while tr * C * 4 > (1 << 20) and tr % 32 == 0:
        tr //= 2
    nb = K // tr
    has_own = own_list is not None

    def body(me_ref, *refs):
        s_refs = refs[:L]
        o_refs = refs[L:2 * L] if has_own else None
        w_ref, m_ref, v_ref = refs[L * (1 + has_own):L * (1 + has_own) + 3]
        g_ref, d_ref, nm_ref, nv_ref = refs[-4:]
        layer = pl.program_id(0)
        g = None
        for l in range(L):
            gl = None
            for p in range(P):
                term = s_refs[l][p].astype(F32)
                if has_own:
                    term = jnp.where(me_ref[0] == p, o_refs[l][0].astype(F32), term)
                gl = term if gl is None else gl + term
            g = gl if g is None else jnp.where(layer == l, gl, g)
        nm = ADAM_B1 * m_ref[...] + (1.0 - ADAM_B1) * g
        nv = ADAM_B2 * v_ref[...] + (1.0 - ADAM_B2) * (g * g)
        m_hat = nm / (1.0 - ADAM_B1 ** ADAM_STEP)
        v_hat = nv / (1.0 - ADAM_B2 ** ADAM_STEP)
        g_ref[...] = g
        d_ref[...] = -ADAM_LR * (m_hat / (jnp.sqrt(v_hat) + ADAM_EPS) + ADAM_WD * w_ref[...])
        nm_ref[...] = nm
        nv_ref[...] = nv

    blk = pl.BlockSpec((None, tr, C), lambda li, i, me_ref: (li, i, 0))
    specs = [pl.BlockSpec((P, tr, C), functools.partial(lambda li, i, me_ref, l: (0, jnp.where(li == l, i, 0), 0), l=l))
             for l in range(L)]
    if has_own:
        specs += [pl.BlockSpec((1, tr, C), functools.partial(lambda li, i, me_ref, l: (me_ref[0], jnp.where(li == l, i, 0), 0), l=l))
                  for l in range(L)]
    return pl.pallas_call(
        body, name=name,
        grid_spec=pltpu.PrefetchScalarGridSpec(
            num_scalar_prefetch=1, grid=(L, nb), in_specs=specs + [blk, blk, blk] + [_ANY] * (dep is not None),
            out_specs=[blk] * 4),
        out_shape=[jax.ShapeDtypeStruct((L, K, C), F32)] * 4,
        compiler_params=_params(("arbitrary", "arbitrary")),
    )(me, *slot_list, *(own_list if has_own else []), w, m, v, *([] if dep is None else [dep]))


_HBM = pl.BlockSpec(memory_space=pltpu.HBM)
_SEM = pl.BlockSpec(memory_space=pltpu.SEMAPHORE)
_ANY = pl.BlockSpec(memory_space=pl.ANY)


def _peers():
    x, y, c = lax.axis_index("x"), lax.axis_index("y"), lax.axis_index("c")

    def flip(v, bit):
        return 1 - v if bit else v

    def peer(k):
        return (flip(x, (k >> 2) & 1), flip(y, (k >> 1) & 1), flip(c, k & 1))

    def peer_index(k):
        px, py, pc = peer(k)
        return 4 * px + 2 * py + pc

    return 4 * x + 2 * y + c, peer, peer_index


def _split_copy(src_refs, land_refs, send_sems, recv_sems, scatter, a, k, outgoing):
    me, peer, peer_index = _peers()
    if outgoing:
        src = src_refs[a].at[peer_index(k)] if scatter else src_refs[a]
        dst = land_refs[a].at[me]
    else:
        src = src_refs[a].at[me] if scatter else src_refs[a]
        dst = land_refs[a].at[peer_index(k)]
    pair = a * (N_DEV - 1) + k - 1
    return pltpu.make_async_remote_copy(src_ref=src, dst_ref=dst, send_sem=send_sems.at[pair],
                                        recv_sem=recv_sems.at[pair], device_id=peer(k),
                                        device_id_type=pl.DeviceIdType.MESH)


def _gather_two_level(srcs, name):
    na = len(srcs)

    def body(*refs):
        src_refs, out_refs = refs[:na], refs[na:2 * na]
        send_sems, recv_sems = refs[2 * na:]
        x, y, c = lax.axis_index("x"), lax.axis_index("y"), lax.axis_index("c")
        me, sibling = (x, y, c), (x, y, 1 - c)
        chips = [(1 - x, y), (x, 1 - y), (1 - x, 1 - y)]

        def copy(a, k, block, to, src=None):
            px, py, pc = block
            slot = out_refs[a].at[4 * px + 2 * py + pc]
            return pltpu.make_async_remote_copy(
                src_ref=slot if src is None else src, dst_ref=slot, send_sem=send_sems.at[a * (N_DEV - 1) + k],
                recv_sem=recv_sems.at[a * (N_DEV - 1) + k], device_id=to, device_id_type=pl.DeviceIdType.MESH)

        first = [copy(a, 0, me, sibling, src_refs[a]) for a in range(na)]
        first += [copy(a, 1 + j, me, (*chip, c), src_refs[a]) for j, chip in enumerate(chips) for a in range(na)]
        for cp in first:
            cp.start()
        passed = []
        for j, chip in enumerate(chips):
            for a in range(na):
                copy(a, 1 + j, (*chip, c), me).wait_recv()
                passed.append(copy(a, 4 + j, (*chip, c), sibling))
                passed[-1].start()
        for a in range(na):
            copy(a, 0, sibling, me).wait_recv()
        for j, chip in enumerate(chips):
            for a in range(na):
                copy(a, 4 + j, (*chip, 1 - c), me).wait_recv()
        for cp in first + passed:
            cp.wait_send()

    return pl.pallas_call(
        body, name=name, in_specs=[_ANY] * na, out_specs=[_ANY] * na,
        out_shape=[jax.ShapeDtypeStruct((N_DEV,) + s.shape, s.dtype) for s in srcs],
        scratch_shapes=[pltpu.SemaphoreType.DMA((na * (N_DEV - 1),)), pltpu.SemaphoreType.DMA((na * (N_DEV - 1),))],
    )(*srcs)


def _exchange_start(srcs, scatter, after, name):
    na = len(srcs)
    land_shapes = [s.shape if scatter else (N_DEV,) + s.shape for s in srcs]
    has_after = after is not None

    def body(*refs):
        src_refs, land_refs = refs[:na], refs[na:2 * na]
        send_sems, recv_sems = refs[2 * na + has_after], refs[2 * na + has_after + 1]
        token = refs[-1]
        for k in range(1, N_DEV):
            for a in range(na):
                _split_copy(src_refs, land_refs, send_sems, recv_sems, scatter, a, k, True).start()
        token[...] = jnp.zeros_like(token)

    sems = pltpu.SemaphoreType.DMA((na * (N_DEV - 1),))
    out_shape = ([sems, sems] + [pltpu.HBM(s.shape, s.dtype) for s in srcs]
                 + [pltpu.HBM(shp, s.dtype) for shp, s in zip(land_shapes, srcs)] + [jax.ShapeDtypeStruct((SUBLANES, 128), F32)])
    args = [pltpu.with_memory_space_constraint(s, pltpu.HBM) for s in srcs]
    args += [pltpu.with_memory_space_constraint(lax.empty(shp, s.dtype), pltpu.HBM) for shp, s in zip(land_shapes, srcs)]
    if has_after:
        args.append(after)
    res = pl.pallas_call(
        body, name=name, in_specs=[_HBM] * (2 * na) + [_ANY] * has_after,
        out_specs=[_SEM, _SEM] + [_HBM] * (2 * na) + [pl.BlockSpec(memory_space=pltpu.VMEM)], out_shape=out_shape,
        input_output_aliases={i: 2 + i for i in range(2 * na)},
        compiler_params=pltpu.CompilerParams(has_side_effects=pltpu.SideEffectType.DATAFLOW_SIDE_EFFECTING),
    )(*args)
    handle = dict(send=res[0], recv=res[1], srcs=list(res[2:2 + na]), lands=list(res[2 + na:2 + 2 * na]), scatter=scatter)
    return handle, res[-1]


def _exchange_wait(handle, after, name):
    srcs, lands, scatter = handle["srcs"], handle["lands"], handle["scatter"]
    na = len(srcs)

    def body(*refs):
        src_refs, land_refs = refs[:na], refs[na:2 * na]
        send_sems, recv_sems = refs[2 * na], refs[2 * na + 1]
        for k in range(1, N_DEV):
            for a in range(na):
                _split_copy(src_refs, land_refs, send_sems, recv_sems, scatter, a, k, True).wait_send()
                _split_copy(src_refs, land_refs, send_sems, recv_sems, scatter, a, k, False).wait_recv()

    res = pl.pallas_call(
        body, name=name, in_specs=[_HBM] * (2 * na) + [_SEM, _SEM, _ANY], out_specs=[_HBM] * (2 * na),
        out_shape=[pltpu.HBM(s.shape, s.dtype) for s in srcs] + [pltpu.HBM(s.shape, s.dtype) for s in lands],
        input_output_aliases={i: i for i in range(2 * na)},
        compiler_params=pltpu.CompilerParams(has_side_effects=pltpu.SideEffectType.DATAFLOW_SIDE_EFFECTING),
    )(*srcs, *lands, handle["send"], handle["recv"], after)
    return list(res[:na]), list(res[na:])


def _layout(D):
    aq, akv = SWA_Q_HEADS * SWA_HEAD_DIM, SWA_KV_HEADS * SWA_HEAD_DIM
    w = SGU_GROUPS * SGU_DIM
    return aq, akv, w


class _Seg:
    def __init__(self, D, rq, rkv):
        aq, akv, w = _layout(D)
        src = {}
        o = 0
        for nm, wd in (("qa", aq), ("ka", akv), ("va", akv), ("cq", rq), ("ckv", rkv), ("kr", MLA_ROPE), ("hu", w), ("hv", w),
                       ("g", 3 * D)):
            src[nm] = (o, wd)
            o += wd
        self.n_in = o
        self.order = ("g", "qa", "hu", "hv", "cq", "ckv", "ka", "va", "kr")
        self.src = src
        self.off = {}
        o = 0
        for nm in self.order:
            self.off[nm] = o
            o += src[nm][1]
        self.width = {nm: src[nm][1] for nm in self.order}
        self.n_pad = -(-o // 1536) * 1536 if o > 1536 else -(-o // 512) * 512
        self.used = o

    def from_shards(self, shards):
        c = shards.shape[2]
        parts = []
        for nm in self.order:
            s0, wd = self.src[nm]
            for d in range(N_DEV):
                lo, hi = max(s0, c * d), min(s0 + wd, c * (d + 1))
                if lo < hi:
                    parts.append(shards[d][:, lo - c * d:hi - c * d])
        parts.append(jnp.zeros((shards.shape[1], self.n_pad - self.used), shards.dtype))
        return jnp.concatenate(parts, axis=1)

    def to_shards(self, w):
        c = self.n_in // N_DEV
        names = sorted(self.order, key=lambda nm: self.src[nm][0])
        shards = []
        for d in range(N_DEV):
            parts = []
            for nm in names:
                s0, wd = self.src[nm]
                lo, hi = max(s0, c * d), min(s0 + wd, c * (d + 1))
                if lo < hi:
                    parts.append(w[:, self.off[nm] + lo - s0:self.off[nm] + hi - s0])
            shards.append(jnp.concatenate(parts, axis=1))
        return jnp.stack(shards)


def _uq_permute(w):
    R = w.shape[0]
    H = MLA_HEADS
    w3 = w.reshape(R, H, MLA_NOPE + MLA_ROPE)
    return jnp.concatenate([w3[:, :, :MLA_NOPE].reshape(R, H * MLA_NOPE), w3[:, :, MLA_NOPE:].reshape(R, H * MLA_ROPE)], axis=1)


def _uq_unpermute(w):
    R = w.shape[0]
    H = MLA_HEADS
    n = w[:, :H * MLA_NOPE].reshape(R, H, MLA_NOPE)
    r = w[:, H * MLA_NOPE:].reshape(R, H, MLA_ROPE)
    return jnp.concatenate([n, r], axis=2).reshape(R, H * (MLA_NOPE + MLA_ROPE))


def _ukv_permute(w):
    R = w.shape[0]
    w3 = w.reshape(R, MLA_HEADS, MLA_NOPE + MLA_V)
    return jnp.concatenate([w3[:, :, :MLA_NOPE].reshape(R, -1), w3[:, :, MLA_NOPE:].reshape(R, -1)], axis=1)


def _ukv_unpermute(w):
    R = w.shape[0]
    H = MLA_HEADS
    k = w[:, :H * MLA_NOPE].reshape(R, H, MLA_NOPE)
    v = w[:, H * MLA_NOPE:].reshape(R, H, MLA_V)
    return jnp.concatenate([k, v], axis=2).reshape(R, H * (MLA_NOPE + MLA_V))


GROUPS = {"a": ("w_in",), "b": ("w_uq", "w_ukv", "w_proj_a", "w_proj_b", "w_proj_c", "w_o", "b_gate"),
          "c": ("w_up", "w_down", "conv_w")}


def _layer_fwd(l, x, x16, fetch, P, cs, sn, seg, alpha):
    S, D = x.shape
    H, half = MLA_HEADS, MLA_ROPE // 2
    off = seg.off
    nm = lambda s: f"l{l}_{s}"
    sv = {"x16": x16}
    W = {"w_in": seg.from_shards(fetch(l, "a", x16)["w_in"])}
    h = _mm(x16, W["w_in"], "nn", [(F32, "n")], nm("h"))[0]
    sv["h"] = h
    ya, lse_a = _swa_fwd(h, off["qa"], off["ka"], off["va"], P["sinks"], nm("swa_fwd"))
    cqn, rq = _rms_fwd(h, off["cq"], seg.width["cq"], P["q_norm_g"], nm("rmsq_fwd"))
    ckvn, rkv = _rms_fwd(h, off["ckv"], seg.width["ckv"], P["kv_norm_g"], nm("rmskv_fwd"))
    W.update(fetch(l, "b", cqn))
    W["w_uq"] = _uq_permute(W["w_uq"])
    W["w_ukv"] = _ukv_permute(W["w_ukv"])
    qf = _mm(cqn, W["w_uq"], "nn", [(F32, "n")], nm("uq"))[0]
    kvf = _mm(ckvn, W["w_ukv"], "nn", [(BF16, "n")], nm("ukv"))[0]
    qr = _rope(qf, H * MLA_NOPE, H * MLA_ROPE, cs, sn, False, nm("ropeq_fwd"))
    kr = _rope(h, off["kr"], LANES, cs, sn, False, nm("ropek_fwd"))
    yb, yb16, lse_b = _mla_fwd(qf, qr, kvf, kr, nm("mla_fwd"))
    w16 = jnp.where(jnp.tril(jnp.ones((SGU_CHUNK, SGU_CHUNK), bool))[None], P["sgu_w"], 0.0).astype(BF16)
    bt = P["sgu_b"].T
    yc = _sgu_fwd(h, off["hu"], off["hv"], P["sgu_ln_g"], P["sgu_ln_b"], w16, bt, nm("sgu_fwd"))
    merged, z = _merge_fwd([ya, yb16, yc], [W["w_proj_a"], W["w_proj_b"], W["w_proj_c"]], h, W["b_gate"], nm("merge_fwd"))
    x1, x1_16, xh1, rs1 = _mm_ln(merged, W["w_o"], x, P["ln1_g"], P["ln1_b"], alpha, nm("wo_ln1"))
    W.update(fetch(l, "c", x1_16))
    up = _mm(x1_16, W["w_up"], "nn", [(F32, "n")], nm("up"))[0]
    a = _glu_fwd(up, W["conv_w"], P["conv_b"], nm("glu_fwd"))
    x2, x2_16, xh2, rs2 = _mm_ln(a, W["w_down"], x1, P["ln2_g"], P["ln2_b"], alpha, nm("down_ln2"))
    sv.update(W=W, ya=ya, lse_a=lse_a, cqn=cqn, rq=rq, ckvn=ckvn, rkv=rkv, qf=qf, qr=qr, kvf=kvf, kr=kr, lse_b=lse_b, yb=yb, yb16=yb16,
              w16=w16, bt=bt, yc=yc, merged=merged, z=z, x1_16=x1_16, xh1=xh1, rs1=rs1, up=up, a=a, xh2=xh2, rs2=rs2)
    return x2, x2_16, sv


def _dw_chunks(k, a, dy, name, post=None, chunker=None):
    n = dy.shape[1]
    if k not in ROW_SHARDED and post is None and chunker is None and (n // N_DEV) % 128 == 0:
        return _mm(a, dy, "tn", [(BF16, "n")], name, chunk=n // N_DEV)[0]
    g = _mm(a, dy, "tn", [(BF16, "n")], name)[0]
    if chunker is not None:
        return chunker(g)
    return _to_chunks(k, g if post is None else post(g))


def _after(arr, token):
    return arr if token is None else arr + token[0:1, 0:1].astype(arr.dtype)


def _layer_bwd(l, dx2, sv, P, cs, sn, seg, alpha, emit):
    S, D = dx2.shape
    H, half = MLA_HEADS, MLA_ROPE // 2
    off = seg.off
    h, W = sv["h"], sv["W"]
    nm = lambda s: f"l{l}_{s}"
    g = {}
    dr2, dr2_16, g["ln2_g"], g["ln2_b"] = _ln_bwd(dx2, sv["xh2"], sv["rs2"], P["ln2_g"], nm("ln2_bwd"))
    g["w_down"] = _dw_chunks("w_down", sv["a"], dr2_16, nm("dw_down"))
    da = _mm(dr2_16, W["w_down"], "nt", [(F32, "n")], nm("da"))[0]
    dcg, dcv, st_g, st_v = _glu_bwd(sv["up"], W["conv_w"], P["conv_b"], da, nm("glu_bwd"))
    F = dcg.shape[1]
    g["conv_w"] = _to_chunks("conv_w", jnp.concatenate([st_g[0:3], st_v[0:3]], axis=1))
    g["conv_b"] = jnp.concatenate([st_g[3:4], st_v[3:4]], axis=1)
    dup = _conv_bwd(dcg, dcv, W["conv_w"], nm("conv_bwd"))
    g["w_up"] = _dw_chunks("w_up", sv["x1_16"], dup, nm("dw_up"))
    token = emit(l, "c", {k: g.pop(k) for k in GROUPS["c"]})
    dx1 = _mm_axpy(dup, W["w_up"], "nt", dr2, alpha, nm("dx1"), dep=token)
    dr1, dr1_16, g["ln1_g"], g["ln1_b"] = _ln_bwd(dx1, sv["xh1"], sv["rs1"], P["ln1_g"], nm("ln1_bwd"))
    g["w_o"] = _dw_chunks("w_o", sv["merged"], dr1_16, nm("dw_o"))
    dmerged = _mm(dr1_16, W["w_o"], "nt", [(F32, "n")], nm("dmerged"))[0]
    dz, dlog, dbg = _merge_bwd(dmerged, sv["z"], h, W["b_gate"], nm("merge_bwd"))
    g["b_gate"] = _to_chunks("b_gate", dbg[0:3])
    g["w_proj_a"] = _dw_chunks("w_proj_a", sv["ya"], dz[0], nm("dw_pa"))
    g["w_proj_b"] = _dw_chunks("w_proj_b", sv["yb16"], dz[1], nm("dw_pb"))
    g["w_proj_c"] = _dw_chunks("w_proj_c", sv["yc"], dz[2], nm("dw_pc"))
    dya = _mm(dz[0], W["w_proj_a"], "nt", [(F32, "n")], nm("dya"))[0]
    dyb = _mm(dz[1], W["w_proj_b"], "nt", [(F32, "n")], nm("dyb"))[0]
    dyc = _mm(dz[2], W["w_proj_c"], "nt", [(F32, "n")], nm("dyc"))[0]
    dhu, dhv, g["sgu_w"], db_s, g["sgu_ln_g"], g["sgu_ln_b"] = _sgu_bwd(
        h, off["hu"], off["hv"], P["sgu_ln_g"], P["sgu_ln_b"], sv["w16"], sv["bt"], dyc, nm("sgu_bwd"))
    g["sgu_b"] = db_s[:, :SGU_GROUPS].T
    dqa, dka, dva, dsk = _swa_bwd(h, off["qa"], off["ka"], off["va"], P["sinks"], dya, sv["lse_a"], nm("swa_bwd"))
    g["sinks"] = dsk[0, :SWA_Q_HEADS]
    dqn, dqr, dkn, dvv, dkr = _mla_bwd(sv["qf"], sv["qr"], sv["kvf"], sv["kr"], dyb, sv["yb"], sv["lse_b"], nm("mla_bwd"))
    dqf = jnp.concatenate([dqn.astype(BF16), _rope(dqr, 0, H * MLA_ROPE, cs, sn, True, nm("ropeq_bwd"))], axis=1)
    dkvf = jnp.concatenate([dkn, dvv], axis=1).astype(BF16)
    dkr16 = _rope(dkr, 0, LANES, cs, sn, True, nm("ropek_bwd"))
    g["w_uq"] = _dw_chunks("w_uq", sv["cqn"], dqf, nm("dw_uq"), _uq_unpermute)
    g["w_ukv"] = _dw_chunks("w_ukv", sv["ckvn"], dkvf, nm("dw_ukv"), _ukv_unpermute)
    token = emit(l, "b", {k: g.pop(k) for k in GROUPS["b"]})
    dcqn = _mm(dqf, W["w_uq"], "nt", [(F32, "n")], nm("dcqn"), dep=token)[0]
    dckvn = _mm(dkvf, W["w_ukv"], "nt", [(F32, "n")], nm("dckvn"), dep=token)[0]
    dcq, g["q_norm_g"] = _rms_bwd(dcqn, h, off["cq"], seg.width["cq"], sv["rq"], P["q_norm_g"], nm("rmsq_bwd"))
    dckv, g["kv_norm_g"] = _rms_bwd(dckvn, h, off["ckv"], seg.width["ckv"], sv["rkv"], P["kv_norm_g"], nm("rmskv_bwd"))
    assert seg.order[-1] == "kr" and seg.n_pad - seg.used >= LANES - MLA_ROPE
    parts = {"g": jnp.concatenate([dlog[0], dlog[1], dlog[2]], axis=1), "qa": dqa, "hu": dhu, "hv": dhv, "cq": dcq, "ckv": dckv,
             "ka": dka.astype(BF16), "va": dva.astype(BF16), "kr": dkr16}
    dh = jnp.concatenate([parts[k] for k in seg.order] + [jnp.zeros((S, seg.n_pad - seg.used - (LANES - MLA_ROPE)), BF16)],
                         axis=1)
    token = emit(l, "a", {"w_in": _dw_chunks("w_in", sv["x16"], dh, nm("dw_in"), chunker=seg.to_shards)})
    dx = _mm_axpy(dh, W["w_in"], "nt", dr1, alpha, nm("dx"), dep=token)
    return dx, g


BIG = ("w_in", "w_uq", "w_ukv", "w_proj_a", "w_proj_b", "w_proj_c", "w_o", "w_up", "w_down")
ROW_SHARDED = ("w_proj_b", "w_o", "w_down")
SHARDED_F32 = ("b_gate", "conv_w")
REPLICATED = ("sinks", "q_norm_g", "kv_norm_g", "sgu_ln_g", "sgu_ln_b", "sgu_w", "sgu_b", "ln1_g", "ln1_b", "conv_b", "ln2_g",
              "ln2_b")
WEIGHTS = ("w_in", "b_gate", "sinks", "q_norm_g", "kv_norm_g", "w_uq", "w_ukv", "sgu_ln_g", "sgu_ln_b", "sgu_w", "sgu_b",
           "w_proj_a", "w_proj_b", "w_proj_c", "w_o", "ln1_g", "ln1_b", "w_up", "conv_w", "conv_b", "w_down", "ln2_g", "ln2_b")


def _step_local(x, positions, target, small, rq, rkv, fetch, emit, token=None):
    S, D = x.shape
    L = small["sinks"].shape[0]
    alpha = (2 * L) ** 0.25
    seg = _Seg(D, rq, rkv)
    inv_freq = ROPE_THETA ** (-jnp.arange(0, MLA_ROPE, 2, dtype=F32) / MLA_ROPE)
    ang = positions.astype(F32)[:, None] * inv_freq
    reps = LANES // (MLA_ROPE // 2)
    cs, sn = jnp.tile(jnp.cos(ang), (1, reps)), jnp.tile(jnp.sin(ang), (1, reps))
    rows = ("q_norm_g", "kv_norm_g", "sgu_ln_g", "sgu_ln_b", "ln1_g", "ln1_b", "conv_b", "ln2_g", "ln2_b")
    layers = [{k: small[k][l].reshape(1, -1) if k in rows else small[k][l] for k in small} for l in range(L)]
    saved = []
    x16 = _after(x, token).astype(BF16)
    for l in range(L):
        x, x16, sv = _layer_fwd(l, x, x16, fetch, layers[l], cs, sn, seg, alpha)
        saved.append(sv)
    loss, dx = _loss(x, target, "loss")
    grads = [None] * L
    for l in reversed(range(L)):
        dx, grads[l] = _layer_bwd(l, dx, saved[l], layers[l], cs, sn, seg, alpha, emit)
    out = {k: jnp.stack([grads[l][k].reshape(small[k].shape[1:]) for l in range(L)]) for k in small}
    return loss, dx, out


def _unshard(k, gathered):
    n, r, c = gathered.shape
    if k in ROW_SHARDED:
        return gathered.reshape(n * r, c)
    return gathered.transpose(1, 0, 2).reshape(r, n * c)


def _to_chunks(k, gfull):
    r, c = gfull.shape
    if k in ROW_SHARDED:
        return gfull.reshape(N_DEV, r // N_DEV, c)
    return gfull.reshape(r, N_DEV, c // N_DEV).transpose(1, 0, 2)


def _pack(arrs):
    P = arrs[0].shape[0]
    flat = jnp.concatenate([a.reshape(P, -1) for a in arrs], axis=1)
    sizes = [(a.size // P, a.size // P) for a in arrs]
    flat = jnp.pad(flat, ((0, 0), (0, -flat.shape[1] % (SUBLANES * 128))))
    return flat.reshape(P, -1, 128), sizes


def _unpack(packed, sizes, shapes):
    flat = packed.reshape(-1)
    out, o = [], 0
    for (n, npad), shp in zip(sizes, shapes):
        out.append(flat[o:o + n].reshape(shp))
        o += npad
    return out


def kernel(x, positions, w_in, b_gate, sinks, q_norm_g, kv_norm_g, w_uq, w_ukv, sgu_ln_g, sgu_ln_b, sgu_w, sgu_b, w_proj_a, w_proj_b, w_proj_c, w_o, ln1_g, ln1_b, w_up, conv_w, conv_b, w_down, ln2_g, ln2_b, loss_target, m_w_in, m_b_gate, m_sinks, m_q_norm_g, m_kv_norm_g, m_w_uq, m_w_ukv, m_sgu_ln_g, m_sgu_ln_b, m_sgu_w, m_sgu_b, m_w_proj_a, m_w_proj_b, m_w_proj_c, m_w_o, m_ln1_g, m_ln1_b, m_w_up, m_conv_w, m_conv_b, m_w_down, m_ln2_g, m_ln2_b, v_w_in, v_b_gate, v_sinks, v_q_norm_g, v_kv_norm_g, v_w_uq, v_w_ukv, v_sgu_ln_g, v_sgu_ln_b, v_sgu_w, v_sgu_b, v_w_proj_a, v_w_proj_b, v_w_proj_c, v_w_o, v_ln1_g, v_ln1_b, v_w_up, v_conv_w, v_conv_b, v_w_down, v_ln2_g, v_ln2_b):
    given = dict(locals())
    w = {k: given[k] for k in WEIGHTS}
    mom = {k: given["m_" + k] for k in WEIGHTS}
    var = {k: given["v_" + k] for k in WEIGHTS}

    L = w_in.shape[0]
    order = [(l, grp) for l in range(L) for grp in ("a", "b", "c")]

    first = [w[k][0].astype(BF16) for k in GROUPS["a"]]
    first_lands = _gather_two_level(first, "gather_first")
    gathers, token = {}, first_lands[0]
    for l, grp in order[1:]:
        srcs = [w[k][l].astype(BF16) if k in BIG else w[k][l] for k in GROUPS[grp]]
        gathers[l, grp], token = _exchange_start(srcs, False, token, f"gather_start_l{l}{grp}")

    me = 4 * lax.axis_index("x") + 2 * lax.axis_index("y") + lax.axis_index("c")
    mine = (jnp.arange(N_DEV) == me)[:, None, None]

    def fetch(l, grp, after):
        if (l, grp) == order[0]:
            srcs, lands = first, first_lands
        else:
            srcs, lands = _exchange_wait(gathers[l, grp], after, f"gather_wait_l{l}{grp}")
        full = {k: jnp.where(mine, srcs[i][None], lands[i]) for i, k in enumerate(GROUPS[grp])}
        return {k: v if k == "w_in" else _unshard(k, v) for k, v in full.items()}

    scatters = {}

    def emit(l, grp, chunks):
        scatters[l, grp], tok = _exchange_start([chunks[k] for k in GROUPS[grp]], True, None, f"scatter_start_l{l}{grp}")
        return tok

    small = {k: w[k] for k in REPLICATED}
    loss, grad_x, g = _step_local(x[0], positions[0], loss_target[0], small, w_uq.shape[1], w_ukv.shape[1], fetch, emit, token)
    loss = lax.psum(loss[0, 0], AXES)

    packed, sizes = _pack([g[k][None] for k in REPLICATED])
    small_grads, after = _exchange_start([packed[0]], False, grad_x, "gather_small_grads_start")

    me1 = me.astype(jnp.int32).reshape(1)
    res = {}
    for grp in ("c", "b", "a"):
        slots, own = {}, {}
        for l in reversed(range(L)):
            srcs, lands = _exchange_wait(scatters[l, grp], after, f"scatter_wait_l{l}{grp}")
            for i, k in enumerate(GROUPS[grp]):
                slots[k, l], own[k, l] = lands[i], srcs[i]
        for k in GROUPS[grp]:
            res[k] = _adamw([slots[k, l] for l in range(L)], [own[k, l] for l in range(L)], me1, w[k], mom[k], var[k],
                            "adamw_" + k, dep=after)
            after = res[k][1]

    srcs, lands = _exchange_wait(small_grads, after, "gather_small_grads_wait")
    parts = jnp.where(mine, srcs[0][None], lands[0])
    shapes = [w[k].shape for k in REPLICATED]
    pw, _ = _pack([w[k][None] for k in REPLICATED])
    pm, _ = _pack([mom[k][None] for k in REPLICATED])
    pv, _ = _pack([var[k][None] for k in REPLICATED])
    outs = _adamw([parts], None, me1, pw, pm, pv, "adamw_small")
    unpacked = [_unpack(o, sizes, shapes) for o in outs]
    for i, k in enumerate(REPLICATED):
        res[k] = [unpacked[j][i] for j in range(4)]

    return (loss, grad_x[None], *[res[k][0] for k in WEIGHTS], *[res[k][1] for k in WEIGHTS],
            *[res[k][2] for k in WEIGHTS], *[res[k][3] for k in WEIGHTS])
```

```python
import functools
import math

import jax
import jax.numpy as jnp
from jax import lax
from jax.experimental import pallas as pl
from jax.experimental.pallas import tpu as pltpu

F32 = jnp.float32
BF16 = jnp.bfloat16

SWA_Q_HEADS = 16
SWA_KV_HEADS = 2
SWA_HEAD_DIM = 64
SWA_BLOCK = 128
MLA_HEADS = 16
MLA_NOPE = 128
MLA_ROPE = 64
MLA_V = 128
SGU_GROUPS = 8
SGU_DIM = 128
SGU_CHUNK = 128
ROPE_THETA = 10000.0
EPS = 1e-5
MASK = -1e30
ADAM_LR = 0.001
ADAM_B1 = 0.9
ADAM_B2 = 0.999
ADAM_EPS = 1e-08
ADAM_WD = 0.01
ADAM_STEP = 10

N_DEV = 8
AXES = ("x", "y", "c")
VMEM_LIMIT = 56 * 1024 * 1024
MLA_TILE = 512
MLA_FWD_TILE = 1024
ROW_TILE = 512
CONV_BWD_ROWS = 2048
GLU_ROWS = 1024
MAX_TK = 2816
SUBLANES = 8


def _tile(n, prefs):
    for p in prefs:
        if n % p == 0:
            return p
    return n


def _params(sem):
    return pltpu.CompilerParams(dimension_semantics=sem, vmem_limit_bytes=VMEM_LIMIT)


def _cols(tm, width, off):
    assert off % width == 0, (off, width)
    blk = off // width
    return pl.BlockSpec((tm, width), lambda i, *_: (i, blk))


def _full(shape):
    nd = len(shape)
    return pl.BlockSpec(shape, lambda *_: (0,) * nd)


def _sigmoid(v):
    return 1.0 / (1.0 + jnp.exp(-v))


def _gelu(v):
    return 0.5 * v * (1.0 + lax.erf(v * (2.0 ** -0.5)))


def _gelu_grad(v):
    return 0.5 * (1.0 + lax.erf(v * (2.0 ** -0.5))) + v * jnp.exp(-0.5 * v * v) * (1.0 / math.sqrt(2.0 * math.pi))


_DIMS = {"nn": (((1,), (0,)), ((), ())), "nt": (((1,), (1,)), ((), ())), "tn": (((0,), (0,)), ((), ()))}


def _mm(a, b, mode, outs, name, *, extras=(), epilogue=None, full_n=False, dep=None, chunk=None):
    if mode == "nn":
        (M, K), (K2, N) = a.shape, b.shape
    elif mode == "nt":
        (M, K), (N, K2) = a.shape, b.shape
    else:
        (K, M), (K2, N) = a.shape, b.shape
    assert K == K2, (a.shape, b.shape, mode)
    tm = _tile(M, (1024, 512, 256, 128))
    tn = N if full_n else _tile(N, (1024, 768, 512, 384, 256, 128))
    if full_n:
        tm = _tile(M, (512, 256, 128))
    if chunk is not None:
        tn = chunk if chunk <= 1536 else _tile(chunk, (1024, 768, 512, 384, 256, 128))
        assert N % chunk == 0 and chunk % tn == 0 and tn % 128 == 0, (N, chunk, tn)
    max_tk = MAX_TK // 2 if full_n else MAX_TK
    tk = max(d for d in range(128, min(K, max_tk) + 1, 128) if K % d == 0) if K % 128 == 0 else K
    nk = K // tk
    if mode == "nn":
        a_spec = pl.BlockSpec((tm, tk), lambda i, j, k: (i, k))
        b_spec = pl.BlockSpec((tk, tn), lambda i, j, k: (k, j))
    elif mode == "nt":
        a_spec = pl.BlockSpec((tm, tk), lambda i, j, k: (i, k))
        b_spec = pl.BlockSpec((tn, tk), lambda i, j, k: (j, k))
    else:
        a_spec = pl.BlockSpec((tk, tm), lambda i, j, k: (k, i))
        b_spec = pl.BlockSpec((tk, tn), lambda i, j, k: (k, j))
    in_specs = [a_spec, b_spec]
    for arr, kind in extras:
        if kind == "tile":
            in_specs.append(pl.BlockSpec((tm, tn), lambda i, j, k: (i, j)))
        else:
            in_specs.append(pl.BlockSpec((1, tn), lambda i, j, k: (0, j)))
    out_specs, out_shape = [], []
    for dt, kind in outs:
        if kind == "n" and chunk is not None:
            per = chunk // tn
            out_specs.append(pl.BlockSpec((None, tm, tn), lambda i, j, k: (lax.div(j, per), i, lax.rem(j, per))))
            out_shape.append(jax.ShapeDtypeStruct((N // chunk, M, chunk), dt))
        elif kind == "n":
            out_specs.append(pl.BlockSpec((tm, tn), lambda i, j, k: (i, j)))
            out_shape.append(jax.ShapeDtypeStruct((M, N), dt))
        else:
            assert tn == N
            out_specs.append(pl.BlockSpec((tm, 1), lambda i, j, k: (i, 0)))
            out_shape.append(jax.ShapeDtypeStruct((M, 1), dt))
    ne, no = len(extras), len(outs)
    deps = []
    if dep is not None:
        in_specs.append(_full(dep.shape))
        deps = [dep]
    dims = _DIMS[mode]
    if epilogue is None:
        epilogue = lambda acc: (acc,) * no

    def body(*refs):
        a_ref, b_ref = refs[0], refs[1]
        ex = refs[2:2 + ne]
        out = refs[len(refs) - 1 - no:len(refs) - 1]
        acc = refs[-1]
        k = pl.program_id(2)
        part = lax.dot_general(a_ref[...].astype(BF16), b_ref[...].astype(BF16), dims, preferred_element_type=F32)

        def finish(total):
            res = epilogue(total, *[e[...] for e in ex])
            for o, r in zip(out, res):
                o[...] = r.astype(o.dtype)

        if nk == 1:
            finish(part)
            return

        @pl.when(k == 0)
        def _():
            acc[...] = part

        @pl.when((k > 0) & (k < nk - 1))
        def _():
            acc[...] += part

        @pl.when(k == nk - 1)
        def _():
            finish(acc[...] + part)

    res = pl.pallas_call(
        body, name=name, grid=(M // tm, N // tn, nk), in_specs=in_specs, out_specs=out_specs, out_shape=out_shape,
        scratch_shapes=[pltpu.VMEM((tm, tn), F32)],
        compiler_params=_params(("parallel", "parallel", "arbitrary")),
    )(a, b, *[e[0] for e in extras], *deps)
    return res


def _ln_epilogue(alpha):
    def epi(acc, x, g, b):
        r = alpha * x + acc
        mu = jnp.mean(r, axis=-1, keepdims=True)
        d = r - mu
        var = jnp.mean(d * d, axis=-1, keepdims=True)
        rstd = lax.rsqrt(var + EPS)
        xhat = d * rstd
        y = xhat * g + b
        return y, y, xhat, rstd
    return epi


def _mm_ln(a, w, x, g, b, alpha, name):
    return _mm(a, w, "nn", [(F32, "n"), (BF16, "n"), (F32, "n"), (F32, "1")], name,
               extras=[(x, "tile"), (g, "row"), (b, "row")], epilogue=_ln_epilogue(alpha), full_n=True)


def _mm_axpy(a, w, mode, r, alpha, name, dep=None):
    return _mm(a, w, mode, [(F32, "n")], name, extras=[(r, "tile")],
               epilogue=lambda acc, rv: (acc + alpha * rv,), dep=dep)[0]


def _ln_bwd(dy, xhat, rstd, g, name):
    S, D = dy.shape
    tm = _tile(S, (ROW_TILE, 256, 128))

    def body(dy_ref, xh_ref, rs_ref, g_ref, dr_ref, dr16_ref, dg_ref, db_ref):
        @pl.when(pl.program_id(0) == 0)
        def _():
            dg_ref[...] = jnp.zeros_like(dg_ref)
            db_ref[...] = jnp.zeros_like(db_ref)

        dyv, xh = dy_ref[...], xh_ref[...]
        dxh = dyv * g_ref[...]
        m1 = jnp.mean(dxh, axis=-1, keepdims=True)
        m2 = jnp.mean(dxh * xh, axis=-1, keepdims=True)
        dr = rs_ref[...] * (dxh - m1 - xh * m2)
        dr_ref[...] = dr
        dr16_ref[...] = dr.astype(BF16)
        dg_ref[...] += jnp.sum(dyv * xh, axis=0, keepdims=True)
        db_ref[...] += jnp.sum(dyv, axis=0, keepdims=True)

    row = pl.BlockSpec((tm, D), lambda i: (i, 0))
    return pl.pallas_call(
        body, name=name, grid=(S // tm,),
        in_specs=[row, row, pl.BlockSpec((tm, 1), lambda i: (i, 0)), _full((1, D))],
        out_specs=[row, row, _full((1, D)), _full((1, D))],
        out_shape=[jax.ShapeDtypeStruct((S, D), F32), jax.ShapeDtypeStruct((S, D), BF16),
                   jax.ShapeDtypeStruct((1, D), F32), jax.ShapeDtypeStruct((1, D), F32)],
        compiler_params=_params(("arbitrary",)),
    )(dy, xhat, rstd, g)


def _rms_fwd(h, off, width, g, name):
    S = h.shape[0]
    tm = _tile(S, (ROW_TILE, 256, 128))

    def body(c_ref, g_ref, y_ref, r_ref):
        c = c_ref[...]
        r = lax.rsqrt(jnp.mean(c * c, axis=-1, keepdims=True) + EPS)
        y_ref[...] = (c * r * g_ref[...]).astype(BF16)
        r_ref[...] = r

    return pl.pallas_call(
        body, name=name, grid=(S // tm,),
        in_specs=[_cols(tm, width, off), _full((1, width))],
        out_specs=[pl.BlockSpec((tm, width), lambda i: (i, 0)), pl.BlockSpec((tm, 1), lambda i: (i, 0))],
        out_shape=[jax.ShapeDtypeStruct((S, width), BF16), jax.ShapeDtypeStruct((S, 1), F32)],
        compiler_params=_params(("parallel",)),
    )(h, g)


def _rms_bwd(dy, h, off, width, rstd, g, name):
    S = h.shape[0]
    tm = _tile(S, (ROW_TILE, 256, 128))

    def body(dy_ref, c_ref, r_ref, g_ref, dc_ref, dg_ref):
        @pl.when(pl.program_id(0) == 0)
        def _():
            dg_ref[...] = jnp.zeros_like(dg_ref)

        dyv, c, r = dy_ref[...], c_ref[...], r_ref[...]
        dyg = dyv * g_ref[...]
        m = jnp.mean(dyg * c, axis=-1, keepdims=True)
        dc_ref[...] = (r * dyg - c * (r * r * r) * m).astype(BF16)
        dg_ref[...] += jnp.sum(dyv * c * r, axis=0, keepdims=True)

    return pl.pallas_call(
        body, name=name, grid=(S // tm,),
        in_specs=[pl.BlockSpec((tm, width), lambda i: (i, 0)), _cols(tm, width, off),
                  pl.BlockSpec((tm, 1), lambda i: (i, 0)), _full((1, width))],
        out_specs=[pl.BlockSpec((tm, width), lambda i: (i, 0)), _full((1, width))],
        out_shape=[jax.ShapeDtypeStruct((S, width), BF16), jax.ShapeDtypeStruct((1, width), F32)],
        compiler_params=_params(("arbitrary",)),
    )(dy, h, rstd, g)


def _loss(y, target, name):
    S, D = y.shape
    tm = _tile(S, (ROW_TILE, 256, 128))

    def body(y_ref, t_ref, l_ref, dy_ref):
        @pl.when(pl.program_id(0) == 0)
        def _():
            l_ref[...] = jnp.zeros_like(l_ref)

        err = y_ref[...] - t_ref[...]
        dy_ref[...] = err * (1.0 / D)
        per_tok = jnp.mean(err * err, axis=-1, keepdims=True)
        l_ref[...] += 0.5 * jnp.sum(per_tok, axis=0, keepdims=True)

    row = pl.BlockSpec((tm, D), lambda i: (i, 0))
    return pl.pallas_call(
        body, name=name, grid=(S // tm,), in_specs=[row, row], out_specs=[_full((1, 1)), row],
        out_shape=[jax.ShapeDtypeStruct((1, 1), F32), jax.ShapeDtypeStruct((S, D), F32)],
        compiler_params=_params(("arbitrary",)),
    )(y, target)


LANES = 128
LOG2E = 1.4426950408889634


def _rope(x, off, width, cs2, sn2, bwd, name):
    S = cs2.shape[0]
    tm = _tile(S, (ROW_TILE, 256, 128))
    stacked = x.ndim == 3
    half = MLA_ROPE // 2
    assert width % LANES == 0 and MLA_ROPE * 2 == LANES

    def rot(v):
        lane = lax.broadcasted_iota(jnp.int32, v.shape, 1)
        return jnp.where((lane & (MLA_ROPE - 1)) < half, -pltpu.roll(v, LANES - half, 1), pltpu.roll(v, half, 1))

    def body(x_ref, c_ref, s_ref, y_ref):
        c, s = c_ref[...], s_ref[...]
        for g in range(width // LANES):
            cols = slice(g * LANES, (g + 1) * LANES)
            v = jnp.sum(x_ref[...], axis=0) if stacked else x_ref[:, cols].astype(F32)
            y = v * c - rot(v * s) if bwd else v * c + rot(v) * s
            y_ref[:, cols] = y.astype(BF16)

    row = pl.BlockSpec((tm, LANES), lambda i: (i, 0))
    x_spec = pl.BlockSpec((x.shape[0], tm, LANES), lambda i: (0, i, 0)) if stacked else _cols(tm, width, off)
    return pl.pallas_call(
        body, name=name, grid=(S // tm,), in_specs=[x_spec, row, row],
        out_specs=pl.BlockSpec((tm, width), lambda i: (i, 0)), out_shape=jax.ShapeDtypeStruct((S, width), BF16),
        compiler_params=_params(("parallel",)),
    )(x, cs2, sn2)


def _swa_mask(n, rows):
    blk = SWA_BLOCK
    row = lax.broadcasted_iota(jnp.int32, (rows, 2 * blk), 0) & (blk - 1)
    col = lax.broadcasted_iota(jnp.int32, (rows, 2 * blk), 1)
    rel = row + blk - col
    return (rel >= 0) & (rel < blk) & ((n > 0) | (col >= blk))


def _swa_specs(off_q, off_k, off_v, stacked):
    blk, aq, akv = SWA_BLOCK, SWA_Q_HEADS * SWA_HEAD_DIM, SWA_KV_HEADS * SWA_HEAD_DIM
    grp = SWA_Q_HEADS // SWA_KV_HEADS
    assert off_q % aq == 0 and off_k % akv == 0 and off_v % akv == 0 and blk & (blk - 1) == 0
    prev = lambda off: pl.BlockSpec((blk, akv), lambda n: (jnp.maximum(n - 1, 0), off // akv))
    cur = lambda off: pl.BlockSpec((blk, akv), lambda n: (n, off // akv))
    sink = _full((SWA_KV_HEADS, grp * blk, 1)) if stacked else pl.BlockSpec(memory_space=pltpu.SMEM)
    return [sink, _cols(blk, aq, off_q), prev(off_k), cur(off_k), prev(off_v), cur(off_v)]


def _swa_sinks(sinks):
    grp = SWA_Q_HEADS // SWA_KV_HEADS
    return jnp.repeat(sinks.reshape(SWA_KV_HEADS, grp), SWA_BLOCK, axis=1)[:, :, None]


def _swa_stack(x, kv):
    hd, grp = SWA_HEAD_DIM, SWA_Q_HEADS // SWA_KV_HEADS
    return jnp.concatenate([x[:, (kv * grp + g) * hd:(kv * grp + g + 1) * hd] for g in range(grp)], axis=0)


def _swa_fwd(h, off_q, off_k, off_v, sinks, name):
    S = h.shape[0]
    blk, hd, nh, nkv = SWA_BLOCK, SWA_HEAD_DIM, SWA_Q_HEADS, SWA_KV_HEADS
    grp = nh // nkv
    aq = nh * hd
    scale = hd ** -0.5

    def body(sink_ref, q_ref, kp_ref, kc_ref, vp_ref, vc_ref, o_ref, lse_ref):
        valid = _swa_mask(pl.program_id(0), blk)
        q = q_ref[...].astype(BF16)
        k2 = jnp.concatenate([kp_ref[...], kc_ref[...]], axis=0).astype(BF16)
        v2 = jnp.concatenate([vp_ref[...], vc_ref[...]], axis=0).astype(BF16)
        for hh in range(nh):
            kv = hh // grp
            qh = q[:, hh * hd:(hh + 1) * hd]
            kh = k2[:, kv * hd:(kv + 1) * hd]
            vh = v2[:, kv * hd:(kv + 1) * hd]
            s = lax.dot_general(qh, kh, _DIMS["nt"], preferred_element_type=F32) * scale
            s = jnp.where(valid, s, MASK)
            sk = sink_ref[hh]
            m = jnp.maximum(jnp.max(s, axis=1, keepdims=True), sk)
            p = jnp.exp(s - m)
            l = jnp.sum(p, axis=1, keepdims=True) + jnp.exp(sk - m)
            o_ref[:, hh * hd:(hh + 1) * hd] = jnp.dot((p / l).astype(BF16), vh, preferred_element_type=F32).astype(BF16)
            lse_ref[:, hh:hh + 1] = m + jnp.log(l)

    return pl.pallas_call(
        body, name=name, grid=(S // blk,), in_specs=_swa_specs(off_q, off_k, off_v, False),
        out_specs=[pl.BlockSpec((blk, aq), lambda n: (n, 0)), pl.BlockSpec((blk, nh), lambda n: (n, 0))],
        out_shape=[jax.ShapeDtypeStruct((S, aq), BF16), jax.ShapeDtypeStruct((S, nh), F32)],
        compiler_params=_params(("parallel",)),
    )(sinks, h, h, h, h, h)


def _swa_bwd(h, off_q, off_k, off_v, sinks, dout, lse, name):
    S = h.shape[0]
    blk, hd, nh, nkv = SWA_BLOCK, SWA_HEAD_DIM, SWA_Q_HEADS, SWA_KV_HEADS
    grp = nh // nkv
    aq, akv = nh * hd, nkv * hd
    scale = hd ** -0.5

    def body(sink_ref, q_ref, kp_ref, kc_ref, vp_ref, vc_ref, do_ref, lse_ref, dq_ref, dk_ref, dv_ref, ds_ref):
        n = pl.program_id(0)

        @pl.when(n == 0)
        def _():
            dk_ref[...] = jnp.zeros_like(dk_ref)
            dv_ref[...] = jnp.zeros_like(dv_ref)
            ds_ref[...] = jnp.zeros_like(ds_ref)

        valid = _swa_mask(n, grp * blk)
        q = q_ref[...].astype(BF16)
        k2 = jnp.concatenate([kp_ref[...], kc_ref[...]], axis=0).astype(BF16)
        v2 = jnp.concatenate([vp_ref[...], vc_ref[...]], axis=0).astype(BF16)
        do = do_ref[...]
        lane = lax.broadcasted_iota(jnp.int32, (1, 128), 1)
        dsink = jnp.zeros((1, 128), F32)
        cur = pl.ds(pl.multiple_of(n * blk, blk), blk)
        prev = pl.ds(pl.multiple_of(jnp.maximum(n - 1, 0) * blk, blk), blk)
        for kv in range(nkv):
            kh = k2[:, kv * hd:(kv + 1) * hd]
            vh = v2[:, kv * hd:(kv + 1) * hd]
            qs = _swa_stack(q, kv)
            dos = _swa_stack(do, kv)
            dos16 = dos.astype(BF16)
            lse = jnp.concatenate([lse_ref[:, kv * grp + g:kv * grp + g + 1] for g in range(grp)], axis=0)
            s = lax.dot_general(qs, kh, _DIMS["nt"], preferred_element_type=F32) * scale
            s = jnp.where(valid, s, MASK)
            p = jnp.exp(s - lse)
            p16 = p.astype(BF16)
            o = jnp.dot(p16, vh, preferred_element_type=F32)
            delta = jnp.sum(dos * o, axis=1, keepdims=True)
            dp = lax.dot_general(dos16, vh, _DIMS["nt"], preferred_element_type=F32)
            ds16 = (p * (dp - delta) * scale).astype(BF16)
            dqs = jnp.dot(ds16, kh, preferred_element_type=F32).astype(BF16)
            dk_acc = lax.dot_general(ds16, qs, _DIMS["tn"], preferred_element_type=F32)
            dv_acc = lax.dot_general(p16, dos16, _DIMS["tn"], preferred_element_type=F32)
            dsk = jnp.exp(sink_ref[kv] - lse) * delta
            for g in range(grp):
                hh = kv * grp + g
                dq_ref[:, hh * hd:(hh + 1) * hd] = dqs[g * blk:(g + 1) * blk]
                dsink += jnp.where(lane == hh, -jnp.sum(dsk[g * blk:(g + 1) * blk], axis=0, keepdims=True), 0.0)
            cols = slice(kv * hd, (kv + 1) * hd)
            dk_ref[cur, cols] += dk_acc[blk:]
            dv_ref[cur, cols] += dv_acc[blk:]

            @pl.when(n > 0)
            def _():
                dk_ref[prev, cols] += dk_acc[:blk]
                dv_ref[prev, cols] += dv_acc[:blk]

        ds_ref[...] += dsink

    return pl.pallas_call(
        body, name=name, grid=(S // blk,),
        in_specs=_swa_specs(off_q, off_k, off_v, True) + [pl.BlockSpec((blk, aq), lambda n: (n, 0)),
                                                    pl.BlockSpec((blk, nh), lambda n: (n, 0))],
        out_specs=[pl.BlockSpec((blk, aq), lambda n: (n, 0)), _full((S, akv)), _full((S, akv)), _full((1, 128))],
        out_shape=[jax.ShapeDtypeStruct((S, aq), BF16), jax.ShapeDtypeStruct((S, akv), F32),
                   jax.ShapeDtypeStruct((S, akv), F32), jax.ShapeDtypeStruct((1, 128), F32)],
        compiler_params=_params(("arbitrary",)),
    )(_swa_sinks(sinks), h, h, h, h, h, dout, lse)


def _causal(i, j, t):
    row = i * t + lax.broadcasted_iota(jnp.int32, (t, t), 0)
    col = j * t + lax.broadcasted_iota(jnp.int32, (t, t), 1)
    return col <= row


def _pair_rope(qr, hh):
    lane = lax.broadcasted_iota(jnp.int32, qr.shape, 1)
    return jnp.where(lane < MLA_ROPE, qr if hh == 0 else pltpu.roll(qr, MLA_ROPE, 1), jnp.zeros_like(qr))


def _mla_fwd(qf, qr, kvf, kr, name):
    S = qr.shape[0]
    H, dn, dv = MLA_HEADS, MLA_NOPE, MLA_V
    assert H % 2 == 0 and dn == LANES and dv == LANES and 2 * MLA_ROPE == LANES
    t = _tile(S, (MLA_FWD_TILE, 512, 256, 128))
    nq = S // t
    scale = (MLA_NOPE + MLA_ROPE) ** -0.5

    def body(qn_ref, qr_ref, kn_ref, kr_ref, v_ref, o_ref, o16_ref, lse_ref, *state):
        i = pl.program_id(1)
        for hh in range(2):
            m_s, l_s, acc_s = state[3 * hh:3 * hh + 3]
            m_s[...] = jnp.full_like(m_s, -jnp.inf)
            l_s[...] = jnp.zeros_like(l_s)
            acc_s[...] = jnp.zeros_like(acc_s)

        def block(j, masked):
            rows = pl.ds(pl.multiple_of(j * t, t), t)
            for hh in range(2):
                m_s, l_s, acc_s = state[3 * hh:3 * hh + 3]
                cols = slice(hh * LANES, (hh + 1) * LANES)
                q = jnp.concatenate([qn_ref[:, cols].astype(BF16), _pair_rope(qr_ref[...], hh)], axis=1)
                k = jnp.concatenate([kn_ref[rows, cols], kr_ref[rows, :]], axis=1)
                s = lax.dot_general(q, k, _DIMS["nt"], preferred_element_type=F32) * (scale * LOG2E)
                if masked:
                    s = jnp.where(_causal(0, 0, t), s, MASK)
                m_old = m_s[...]
                m_new = jnp.maximum(m_old, jnp.max(s, axis=1, keepdims=True))
                alpha = jnp.exp2(m_old - m_new)
                p = jnp.exp2(s - m_new)
                l_s[...] = alpha * l_s[...] + jnp.sum(p, axis=1, keepdims=True)
                acc_s[...] = alpha * acc_s[...] + jnp.dot(p.astype(BF16), v_ref[rows, cols], preferred_element_type=F32)
                m_s[...] = m_new

        def full_block(j, carry):
            block(j, False)
            return carry

        lax.fori_loop(0, i, full_block, 0)
        block(i, True)
        for hh in range(2):
            m_s, l_s, acc_s = state[3 * hh:3 * hh + 3]
            out = acc_s[...] / l_s[...]
            o_ref[:, hh * LANES:(hh + 1) * LANES] = out
            o16_ref[:, hh * LANES:(hh + 1) * LANES] = out.astype(BF16)
            lse_ref[hh] = (m_s[...] + jnp.log2(l_s[...])) * (1.0 / LOG2E)

    P = H // 2
    return pl.pallas_call(
        body, name=name, grid=(P, nq),
        in_specs=[pl.BlockSpec((t, 2 * LANES), lambda p, i: (i, p)), pl.BlockSpec((t, LANES), lambda p, i: (i, p)),
                  pl.BlockSpec((S, 2 * LANES), lambda p, i: (0, p)), pl.BlockSpec((S, LANES), lambda p, i: (0, 0)),
                  pl.BlockSpec((S, 2 * LANES), lambda p, i: (0, P + p))],
        out_specs=[pl.BlockSpec((t, 2 * LANES), lambda p, i: (i, p)), pl.BlockSpec((t, 2 * LANES), lambda p, i: (i, p)),
                   pl.BlockSpec((2, t, 1), lambda p, i: (p, i, 0))],
        out_shape=[jax.ShapeDtypeStruct((S, H * dv), F32), jax.ShapeDtypeStruct((S, H * dv), BF16),
                   jax.ShapeDtypeStruct((H, S, 1), F32)],
        scratch_shapes=[pltpu.VMEM((t, 1), F32), pltpu.VMEM((t, 1), F32), pltpu.VMEM((t, dv), F32)] * 2,
        compiler_params=_params(("parallel", "arbitrary")),
    )(qf, qr, kvf, kr, kvf)


def _mla_bwd(qf, qr, kvf, kr, do, out, lse, name):
    S = qr.shape[0]
    H, dv = MLA_HEADS, MLA_V
    P = H // 2
    t = _tile(S, (MLA_TILE, 256, 128))
    nq = S // t
    scale = (MLA_NOPE + MLA_ROPE) ** -0.5

    def body(qn_ref, qr_ref, kn_ref, kr_ref, v_ref, do_ref, o_ref, lse_ref, dqn_ref, dqr_ref, dkn_ref, dv_ref, dkr_ref,
             dkn_s, dv_s, dkr_s, dqn_s, dqr_s):
        j, i = pl.program_id(1), pl.program_id(2)

        @pl.when((j == 0) & (i == 0))
        def _():
            dqn_s[...] = jnp.zeros_like(dqn_s)
            dqr_s[...] = jnp.zeros_like(dqr_s)

        @pl.when(i == j)
        def _():
            dkn_s[...] = jnp.zeros_like(dkn_s)
            dv_s[...] = jnp.zeros_like(dv_s)
            dkr_s[...] = jnp.zeros_like(dkr_s)

        def block(masked):
            rows = pl.ds(pl.multiple_of(i * t, t), t)
            krv = kr_ref[...]
            for hh in range(2):
                cols = slice(hh * LANES, (hh + 1) * LANES)
                q = jnp.concatenate([qn_ref[:, cols].astype(BF16), _pair_rope(qr_ref[...], hh)], axis=1)
                k = jnp.concatenate([kn_ref[:, cols], krv], axis=1)
                vv, dof = v_ref[:, cols], do_ref[:, cols]
                dov = dof.astype(BF16)
                delta = jnp.sum(dof * o_ref[:, cols], axis=1, keepdims=True)
                s = lax.dot_general(q, k, _DIMS["nt"], preferred_element_type=F32) * (scale * LOG2E)
                if masked:
                    s = jnp.where(_causal(0, 0, t), s, MASK)
                p = jnp.exp2(s - lse_ref[hh] * LOG2E)
                p16 = p.astype(BF16)
                dp = lax.dot_general(dov, vv, _DIMS["nt"], preferred_element_type=F32)
                ds16 = (p * (dp - delta) * scale).astype(BF16)
                dv_s[:, cols] += lax.dot_general(p16, dov, _DIMS["tn"], preferred_element_type=F32)
                dk = lax.dot_general(ds16, q, _DIMS["tn"], preferred_element_type=F32)
                dkn_s[:, cols] += dk[:, :LANES]
                dkr_s[...] += dk[:, LANES:]
                dq = jnp.dot(ds16, k, preferred_element_type=F32)
                dqn_s[rows, cols] += dq[:, :LANES]
                dqr = dq[:, LANES:]
                dqr_s[rows, :] += dqr if hh == 0 else pltpu.roll(dqr, MLA_ROPE, 1)

        @pl.when(i == j)
        def _():
            block(True)

        @pl.when(i > j)
        def _():
            block(False)

        @pl.when(i == nq - 1)
        def _():
            dkn_ref[...] = dkn_s[...].astype(BF16)
            dv_ref[...] = dv_s[...].astype(BF16)
            dkr_ref[...] = dkr_s[...]

        @pl.when((j == nq - 1) & (i == nq - 1))
        def _():
            dqn_ref[...] = dqn_s[...].astype(BF16)
            dqr_ref[...] = dqr_s[...]

    qi = lambda i, j: jnp.maximum(i, j)
    return pl.pallas_call(
        body, name=name, grid=(P, nq, nq),
        in_specs=[pl.BlockSpec((t, 2 * LANES), lambda p, j, i: (qi(i, j), p)), pl.BlockSpec((t, LANES), lambda p, j, i: (qi(i, j), p)),
                  pl.BlockSpec((t, 2 * LANES), lambda p, j, i: (j, p)), pl.BlockSpec((t, LANES), lambda p, j, i: (j, 0)),
                  pl.BlockSpec((t, 2 * LANES), lambda p, j, i: (j, P + p)),
                  pl.BlockSpec((t, 2 * LANES), lambda p, j, i: (qi(i, j), p)),
                  pl.BlockSpec((t, 2 * LANES), lambda p, j, i: (qi(i, j), p)),
                  pl.BlockSpec((2, t, 1), lambda p, j, i: (p, qi(i, j), 0))],
        out_specs=[pl.BlockSpec((S, 2 * LANES), lambda p, j, i: (0, p)), pl.BlockSpec((S, LANES), lambda p, j, i: (0, p)),
                   pl.BlockSpec((t, 2 * LANES), lambda p, j, i: (j, p)), pl.BlockSpec((t, 2 * LANES), lambda p, j, i: (j, p)),
                   pl.BlockSpec((None, t, LANES), lambda p, j, i: (p, j, 0))],
        out_shape=[jax.ShapeDtypeStruct((S, H * MLA_NOPE), BF16), jax.ShapeDtypeStruct((S, H * MLA_ROPE), F32),
                   jax.ShapeDtypeStruct((S, H * MLA_NOPE), BF16), jax.ShapeDtypeStruct((S, H * dv), BF16),
                   jax.ShapeDtypeStruct((P, S, LANES), F32)],
        scratch_shapes=[pltpu.VMEM((t, 2 * LANES), F32), pltpu.VMEM((t, 2 * LANES), F32), pltpu.VMEM((t, LANES), F32),
                        pltpu.VMEM((S, 2 * LANES), F32), pltpu.VMEM((S, LANES), F32)],
        compiler_params=_params(("parallel", "arbitrary", "arbitrary")),
    )(qf, qr, kvf, kr, kvf, do, out, lse)


def _sgu_norm(hv, lg, lb):
    vg = _gelu(hv)
    mu = jnp.mean(vg, axis=-1, keepdims=True)
    d = vg - mu
    rstd = lax.rsqrt(jnp.mean(d * d, axis=-1, keepdims=True) + EPS)
    xhat = d * rstd
    return xhat, rstd, xhat * lg + lb


def _sgu_fwd(h, off_u, off_v, lg, lb, w16, bt, name):
    S = h.shape[0]
    T, G, C = SGU_CHUNK, SGU_GROUPS, SGU_DIM
    W = G * C

    def body(hu_ref, hv_ref, lg_ref, lb_ref, w_ref, bt_ref, y_ref):
        u = _gelu(hu_ref[...])
        _, _, vn = _sgu_norm(hv_ref[...], lg_ref[...], lb_ref[...])
        vn16 = vn.astype(BF16)
        for g in range(G):
            cols = slice(g * C, (g + 1) * C)
            mixed = jnp.dot(w_ref[g], vn16[:, cols], preferred_element_type=F32) + bt_ref[:, g:g + 1]
            y_ref[:, cols] = (u[:, cols] * mixed).astype(BF16)

    return pl.pallas_call(
        body, name=name, grid=(S // T,),
        in_specs=[_cols(T, W, off_u), _cols(T, W, off_v), _full((1, W)), _full((1, W)), _full((G, T, T)), _full((T, G))],
        out_specs=pl.BlockSpec((T, W), lambda n: (n, 0)),
        out_shape=jax.ShapeDtypeStruct((S, W), BF16),
        compiler_params=_params(("parallel",)),
    )(h, h, lg, lb, w16, bt)


def _sgu_bwd(h, off_u, off_v, lg, lb, w16, bt, dy, name):
    S = h.shape[0]
    T, G, C = SGU_CHUNK, SGU_GROUPS, SGU_DIM
    W = G * C
    nc = S // T

    def body(hu_ref, hv_ref, lg_ref, lb_ref, w_ref, bt_ref, dy_ref, dhu_ref, dhv_ref, dw_ref, db_ref, dlg_ref, dlb_ref,
             dmix_s, dvn_s):
        n = pl.program_id(0)

        @pl.when(n == 0)
        def _():
            dw_ref[...] = jnp.zeros_like(dw_ref)
            dlg_ref[...] = jnp.zeros_like(dlg_ref)
            dlb_ref[...] = jnp.zeros_like(dlb_ref)
            dmix_s[...] = jnp.zeros_like(dmix_s)

        hu, hv, lgv = hu_ref[...], hv_ref[...], lg_ref[...]
        u = _gelu(hu)
        xhat, rstd, vn = _sgu_norm(hv, lgv, lb_ref[...])
        vn16 = vn.astype(BF16)
        dyv = dy_ref[...]
        dmixed = dyv * u
        dmix_s[...] += dmixed
        dmixed16 = dmixed.astype(BF16)
        for g in range(G):
            cols = slice(g * C, (g + 1) * C)
            mixed = jnp.dot(w_ref[g], vn16[:, cols], preferred_element_type=F32) + bt_ref[:, g:g + 1]
            dhu_ref[:, cols] = (dyv[:, cols] * mixed * _gelu_grad(hu[:, cols])).astype(BF16)
            dvn_s[:, cols] = lax.dot_general(w_ref[g], dmixed16[:, cols], _DIMS["tn"], preferred_element_type=F32)
            dw_ref[g] += lax.dot_general(dmixed16[:, cols], vn16[:, cols], _DIMS["nt"], preferred_element_type=F32)
        dvn = dvn_s[...]
        dlg_ref[...] += jnp.sum(dvn * xhat, axis=0, keepdims=True)
        dlb_ref[...] += jnp.sum(dvn, axis=0, keepdims=True)
        dxh = dvn * lgv
        m1 = jnp.mean(dxh, axis=-1, keepdims=True)
        m2 = jnp.mean(dxh * xhat, axis=-1, keepdims=True)
        dvg = rstd * (dxh - m1 - xhat * m2)
        dhv_ref[...] = (dvg * _gelu_grad(hv)).astype(BF16)

        @pl.when(n == nc - 1)
        def _():
            tril = lax.broadcasted_iota(jnp.int32, (T, T), 1) <= lax.broadcasted_iota(jnp.int32, (T, T), 0)
            lane = lax.broadcasted_iota(jnp.int32, (T, 128), 1)
            db = jnp.zeros((T, 128), F32)
            for g in range(G):
                dw_ref[g] = jnp.where(tril, dw_ref[g], 0.0)
                db += jnp.where(lane == g, jnp.sum(dmix_s[:, g * C:(g + 1) * C], axis=1, keepdims=True), 0.0)
            db_ref[...] = db

    row = pl.BlockSpec((T, W), lambda n: (n, 0))
    return pl.pallas_call(
        body, name=name, grid=(nc,),
        in_specs=[_cols(T, W, off_u), _cols(T, W, off_v), _full((1, W)), _full((1, W)), _full((G, T, T)), _full((T, G)), row],
        out_specs=[row, row, _full((G, T, T)), _full((T, 128)), _full((1, W)), _full((1, W))],
        out_shape=[jax.ShapeDtypeStruct((S, W), BF16), jax.ShapeDtypeStruct((S, W), BF16),
                   jax.ShapeDtypeStruct((G, T, T), F32), jax.ShapeDtypeStruct((T, 128), F32),
                   jax.ShapeDtypeStruct((1, W), F32), jax.ShapeDtypeStruct((1, W), F32)],
        scratch_shapes=[pltpu.VMEM((T, W), F32), pltpu.VMEM((T, W), F32)],
        compiler_params=_params(("arbitrary",)),
    )(h, h, lg, lb, w16, bt, dy)


def _merge_fwd(ys, ps, h, bg, name):
    S = h.shape[0]
    D = ps[0].shape[1]
    tm = _tile(S, (1024, 512, 256, 128))
    tn = _tile(D, (256, 128))
    nb = len(ys)

    def body(*refs):
        y_refs, p_refs, l_refs = refs[:nb], refs[nb:2 * nb], refs[2 * nb:3 * nb]
        bg_ref, mg_ref, z_ref = refs[3 * nb:]
        acc = jnp.zeros((tm, tn), F32)
        for b in range(nb):
            z = jnp.dot(y_refs[b][...].astype(BF16), p_refs[b][...], preferred_element_type=F32)
            z_ref[b] = z
            acc += _sigmoid(l_refs[b][...] + bg_ref[b:b + 1, :]) * z
        mg_ref[...] = acc.astype(BF16)

    in_specs = [pl.BlockSpec((tm, y.shape[1]), lambda i, j: (i, 0)) for y in ys]
    in_specs += [pl.BlockSpec((p.shape[0], tn), lambda i, j: (0, j)) for p in ps]
    in_specs += [pl.BlockSpec((tm, tn), functools.partial(lambda i, j, b: (i, b * (D // tn) + j), b=b)) for b in range(nb)]
    in_specs += [pl.BlockSpec((nb, tn), lambda i, j: (0, j))]
    return pl.pallas_call(
        body, name=name, grid=(S // tm, D // tn), in_specs=in_specs,
        out_specs=[pl.BlockSpec((tm, tn), lambda i, j: (i, j)), pl.BlockSpec((nb, tm, tn), lambda i, j: (0, i, j))],
        out_shape=[jax.ShapeDtypeStruct((S, D), BF16), jax.ShapeDtypeStruct((nb, S, D), F32)],
        compiler_params=_params(("parallel", "parallel")),
    )(*ys, *ps, *([h] * nb), bg)


def _merge_bwd(dm, z, h, bg, name):
    nb, S, D = z.shape
    tm = _tile(S, (ROW_TILE, 256, 128))
    tn = _tile(D, (512, 256, 128))

    def body(*refs):
        dm_ref, z_ref = refs[0], refs[1]
        l_refs = refs[2:2 + nb]
        bg_ref = refs[2 + nb]
        dz_ref, dl_ref, dbg_ref = refs[3 + nb:3 + 2 * nb], refs[3 + 2 * nb:3 + 3 * nb], refs[3 + 3 * nb]

        @pl.when(pl.program_id(1) == 0)
        def _():
            dbg_ref[...] = jnp.zeros_like(dbg_ref)

        dmv = dm_ref[...]
        rows = lax.broadcasted_iota(jnp.int32, (SUBLANES, tn), 0)
        dbg = jnp.zeros((SUBLANES, tn), F32)
        for b in range(nb):
            gt = _sigmoid(l_refs[b][...] + bg_ref[b:b + 1, :])
            dz_ref[b][...] = (dmv * gt).astype(BF16)
            dl = dmv * z_ref[b] * gt * (1.0 - gt)
            dl_ref[b][...] = dl.astype(BF16)
            dbg += jnp.where(rows == b, jnp.sum(dl, axis=0, keepdims=True), 0.0)
        dbg_ref[...] += dbg

    in_specs = [pl.BlockSpec((tm, tn), lambda j, i: (i, j)), pl.BlockSpec((nb, tm, tn), lambda j, i: (0, i, j))]
    in_specs += [pl.BlockSpec((tm, tn), functools.partial(lambda j, i, b: (i, b * (D // tn) + j), b=b)) for b in range(nb)]
    in_specs += [pl.BlockSpec((nb, tn), lambda j, i: (0, j))]
    tile = pl.BlockSpec((tm, tn), lambda j, i: (i, j))
    res = pl.pallas_call(
        body, name=name, grid=(D // tn, S // tm), in_specs=in_specs,
        out_specs=[tile] * (2 * nb) + [pl.BlockSpec((SUBLANES, tn), lambda j, i: (0, j))],
        out_shape=[jax.ShapeDtypeStruct((S, D), BF16)] * (2 * nb) + [jax.ShapeDtypeStruct((SUBLANES, D), F32)],
        compiler_params=_params(("parallel", "arbitrary")),
    )(dm, z, *([h] * nb), bg)
    return res[:nb], res[nb:2 * nb], res[2 * nb]


def _shift_down(x, halo, k):
    xr = pltpu.roll(x, k, 0)
    hr = pltpu.roll(halo, k, 0)
    rows = lax.broadcasted_iota(jnp.int32, halo.shape, 0)
    top = jnp.where(rows < k, hr, xr[:SUBLANES])
    return jnp.concatenate([top, xr[SUBLANES:]], axis=0)


def _shift_up(x, halo, k):
    tm = x.shape[0]
    xr = pltpu.roll(x, tm - k, 0)
    hr = pltpu.roll(halo, SUBLANES - k, 0)
    rows = lax.broadcasted_iota(jnp.int32, halo.shape, 0)
    bot = jnp.where(rows >= SUBLANES - k, hr, xr[tm - SUBLANES:])
    return jnp.concatenate([xr[:tm - SUBLANES], bot], axis=0)


def _conv_tiles(S, F):
    return _tile(S, (GLU_ROWS, 512, 256, 128)), _tile(F, (512, 256, 128))


def _conv_in_specs(tm, tn, F):
    r8 = tm // SUBLANES
    nf = F // tn
    specs = []
    for half in range(2):
        specs.append(pl.BlockSpec((tm, tn), functools.partial(lambda j, i, o: (i, o + j), o=half * nf)))
        specs.append(pl.BlockSpec((SUBLANES, tn), functools.partial(lambda j, i, o: (jnp.maximum(i * r8 - 1, 0), o + j), o=half * nf)))
    for half in range(2):
        specs.append(pl.BlockSpec((3, tn), functools.partial(lambda j, i, o: (0, o + j), o=half * nf)))
        specs.append(pl.BlockSpec((1, tn), functools.partial(lambda j, i, o: (0, o + j), o=half * nf)))
    return specs


def _conv_apply(x, halo, w, b, first):
    halo = jnp.where(first, 0.0, halo)
    x1 = _shift_down(x, halo, 1)
    x2 = _shift_down(x, halo, 2)
    return b + x2 * w[0:1, :] + x1 * w[1:2, :] + x * w[2:3, :], x1, x2


def _glu_fwd(up, cw, cb, name):
    S, F2 = up.shape
    F = F2 // 2
    tm, tn = _conv_tiles(S, F)

    def body(ug, hg, uv, hv, wg, bgr, wv, bvr, a_ref):
        first = pl.program_id(1) == 0
        cg, _, _ = _conv_apply(ug[...], hg[...], wg[...], bgr[...], first)
        cv, _, _ = _conv_apply(uv[...], hv[...], wv[...], bvr[...], first)
        a_ref[...] = (cg * _sigmoid(cg) * cv).astype(BF16)

    return pl.pallas_call(
        body, name=name, grid=(F // tn, S // tm), in_specs=_conv_in_specs(tm, tn, F),
        out_specs=pl.BlockSpec((tm, tn), lambda j, i: (i, j)),
        out_shape=jax.ShapeDtypeStruct((S, F), BF16),
        compiler_params=_params(("parallel", "parallel")),
    )(up, up, up, up, cw, cb, cw, cb)


def _glu_bwd(up, cw, cb, da, name):
    S, F2 = up.shape
    F = F2 // 2
    tm, tn = _conv_tiles(S, F)

    def body(ug, hg, uv, hv, wg, bgr, wv, bvr, da_ref, dg_ref, dv_ref, sg_ref, sv_ref):
        i = pl.program_id(1)

        @pl.when(i == 0)
        def _():
            sg_ref[...] = jnp.zeros_like(sg_ref)
            sv_ref[...] = jnp.zeros_like(sv_ref)

        first = i == 0
        xg, xv = ug[...], uv[...]
        cg, xg1, xg2 = _conv_apply(xg, hg[...], wg[...], bgr[...], first)
        cv, xv1, xv2 = _conv_apply(xv, hv[...], wv[...], bvr[...], first)
        dav = da_ref[...]
        sg = _sigmoid(cg)
        dcv = dav * cg * sg
        dcg = dav * cv * sg * (1.0 + cg * (1.0 - sg))
        dg_ref[...] = dcg
        dv_ref[...] = dcv
        rows = lax.broadcasted_iota(jnp.int32, (SUBLANES, tn), 0)

        def stats(dc, x, x1, x2):
            acc = jnp.zeros((SUBLANES, tn), F32)
            for r, val in enumerate((dc * x2, dc * x1, dc * x, dc)):
                acc += jnp.where(rows == r, jnp.sum(val, axis=0, keepdims=True), 0.0)
            return acc

        sg_ref[...] += stats(dcg, xg, xg1, xg2)
        sv_ref[...] += stats(dcv, xv, xv1, xv2)

    tile = pl.BlockSpec((tm, tn), lambda j, i: (i, j))
    stat = pl.BlockSpec((SUBLANES, tn), lambda j, i: (0, j))
    return pl.pallas_call(
        body, name=name, grid=(F // tn, S // tm), in_specs=_conv_in_specs(tm, tn, F) + [tile],
        out_specs=[tile, tile, stat, stat],
        out_shape=[jax.ShapeDtypeStruct((S, F), F32), jax.ShapeDtypeStruct((S, F), F32),
                   jax.ShapeDtypeStruct((SUBLANES, F), F32), jax.ShapeDtypeStruct((SUBLANES, F), F32)],
        compiler_params=_params(("parallel", "arbitrary")),
    )(up, up, up, up, cw, cb, cw, cb, da)


def _conv_bwd(dcg, dcv, w, name):
    S, F = dcg.shape
    tm, tn = _tile(S, (CONV_BWD_ROWS, 1024, 512, 256, 128)), _conv_tiles(S, F)[1]
    r8 = tm // SUBLANES
    ni = S // tm
    nf = F // tn

    def body(g_ref, gh_ref, v_ref, vh_ref, w_ref, o_ref):
        gate = pl.program_id(0) == 0
        x = jnp.where(gate, g_ref[...], v_ref[...])
        halo = jnp.where(gate, gh_ref[...], vh_ref[...])
        halo = jnp.where(pl.program_id(2) == ni - 1, 0.0, halo)
        wv = w_ref[...]
        o_ref[...] = (x * wv[2:3, :] + _shift_up(x, halo, 1) * wv[1:2, :] + _shift_up(x, halo, 2) * wv[0:1, :]).astype(BF16)

    def tile(half):
        return pl.BlockSpec((tm, tn), lambda h, j, i: (jnp.where(h == half, i, 0), jnp.where(h == half, j, 0)))

    def below(half):
        return pl.BlockSpec((SUBLANES, tn), lambda h, j, i: (
            jnp.where(h == half, jnp.minimum((i + 1) * r8, S // SUBLANES - 1), 0), jnp.where(h == half, j, 0)))

    return pl.pallas_call(
        body, name=name, grid=(2, nf, ni),
        in_specs=[tile(0), below(0), tile(1), below(1), pl.BlockSpec((3, tn), lambda h, j, i: (0, h * nf + j))],
        out_specs=pl.BlockSpec((tm, tn), lambda h, j, i: (i, h * nf + j)),
        out_shape=jax.ShapeDtypeStruct((S, 2 * F), BF16),
        compiler_params=_params(("parallel", "parallel", "parallel")),
    )(dcg, dcg, dcv, dcv, w)


def _adamw(slot_list, own_list, me, w, m, v, name, dep=None):
    L = len(slot_list)
    P, K, C = slot_list[0].shape
    tr = _tile(K, (256, 128, 64, 32, 16))
    while tr * C * 4 > (1 << 20) and tr % 32 == 0:
        tr //= 2
    nb = K // tr
    has_own = own_list is not None

    def body(me_ref, *refs):
        s_refs = refs[:L]
        o_refs = refs[L:2 * L] if has_own else None
        w_ref, m_ref, v_ref = refs[L * (1 + has_own):L * (1 + has_own) + 3]
        g_ref, d_ref, nm_ref, nv_ref = refs[-4:]
        layer = pl.program_id(0)
        g = None
        for l in range(L):
            gl = None
            for p in range(P):
                term = s_refs[l][p].astype(F32)
                if has_own:
                    term = jnp.where(me_ref[0] == p, o_refs[l][0].astype(F32), term)
                gl = term if gl is None else gl + term
            g = gl if g is None else jnp.where(layer == l, gl, g)
        nm = ADAM_B1 * m_ref[...] + (1.0 - ADAM_B1) * g
        nv = ADAM_B2 * v_ref[...] + (1.0 - ADAM_B2) * (g * g)
        m_hat = nm / (1.0 - ADAM_B1 ** ADAM_STEP)
        v_hat = nv / (1.0 - ADAM_B2 ** ADAM_STEP)
        g_ref[...] = g
        d_ref[...] = -ADAM_LR * (m_hat / (jnp.sqrt(v_hat) + ADAM_EPS) + ADAM_WD * w_ref[...])
        nm_ref[...] = nm
        nv_ref[...] = nv

    blk = pl.BlockSpec((None, tr, C), lambda li, i, me_ref: (li, i, 0))
    specs = [pl.BlockSpec((P, tr, C), functools.partial(lambda li, i, me_ref, l: (0, jnp.where(li == l, i, 0), 0), l=l))
             for l in range(L)]
    if has_own:
        specs += [pl.BlockSpec((1, tr, C), functools.partial(lambda li, i, me_ref, l: (me_ref[0], jnp.where(li == l, i, 0), 0), l=l))
                  for l in range(L)]
    return pl.pallas_call(
        body, name=name,
        grid_spec=pltpu.PrefetchScalarGridSpec(
            num_scalar_prefetch=1, grid=(L, nb), in_specs=specs + [blk, blk, blk] + [_ANY] * (dep is not None),
            out_specs=[blk] * 4),
        out_shape=[jax.ShapeDtypeStruct((L, K, C), F32)] * 4,
        compiler_params=_params(("arbitrary", "arbitrary")),
    )(me, *slot_list, *(own_list if has_own else []), w, m, v, *([] if dep is None else [dep]))


_HBM = pl.BlockSpec(memory_space=pltpu.HBM)
_SEM = pl.BlockSpec(memory_space=pltpu.SEMAPHORE)
_ANY = pl.BlockSpec(memory_space=pl.ANY)


def _peers():
    x, y, c = lax.axis_index("x"), lax.axis_index("y"), lax.axis_index("c")

    def flip(v, bit):
        return 1 - v if bit else v

    def peer(k):
        return (flip(x, (k >> 2) & 1), flip(y, (k >> 1) & 1), flip(c, k & 1))

    def peer_index(k):
        px, py, pc = peer(k)
        return 4 * px + 2 * py + pc

    return 4 * x + 2 * y + c, peer, peer_index


def _split_copy(src_refs, land_refs, send_sems, recv_sems, scatter, a, k, outgoing):
    me, peer, peer_index = _peers()
    if outgoing:
        src = src_refs[a].at[peer_index(k)] if scatter else src_refs[a]
        dst = land_refs[a].at[me]
    else:
        src = src_refs[a].at[me] if scatter else src_refs[a]
        dst = land_refs[a].at[peer_index(k)]
    pair = a * (N_DEV - 1) + k - 1
    return pltpu.make_async_remote_copy(src_ref=src, dst_ref=dst, send_sem=send_sems.at[pair],
                                        recv_sem=recv_sems.at[pair], device_id=peer(k),
                                        device_id_type=pl.DeviceIdType.MESH)


def _gather_two_level(srcs, name):
    na = len(srcs)

    def body(*refs):
        src_refs, out_refs = refs[:na], refs[na:2 * na]
        send_sems, recv_sems = refs[2 * na:]
        x, y, c = lax.axis_index("x"), lax.axis_index("y"), lax.axis_index("c")
        me, sibling = (x, y, c), (x, y, 1 - c)
        chips = [(1 - x, y), (x, 1 - y), (1 - x, 1 - y)]

        def copy(a, k, block, to, src=None):
            px, py, pc = block
            slot = out_refs[a].at[4 * px + 2 * py + pc]
            return pltpu.make_async_remote_copy(
                src_ref=slot if src is None else src, dst_ref=slot, send_sem=send_sems.at[a * (N_DEV - 1) + k],
                recv_sem=recv_sems.at[a * (N_DEV - 1) + k], device_id=to, device_id_type=pl.DeviceIdType.MESH)

        first = [copy(a, 0, me, sibling, src_refs[a]) for a in range(na)]
        first += [copy(a, 1 + j, me, (*chip, c), src_refs[a]) for j, chip in enumerate(chips) for a in range(na)]
        for cp in first:
            cp.start()
        passed = []
        for j, chip in enumerate(chips):
            for a in range(na):
                copy(a, 1 + j, (*chip, c), me).wait_recv()
                passed.append(copy(a, 4 + j, (*chip, c), sibling))
                passed[-1].start()
        for a in range(na):
            copy(a, 0, sibling, me).wait_recv()
        for j, chip in enumerate(chips):
            for a in range(na):
                copy(a, 4 + j, (*chip, 1 - c), me).wait_recv()
        for cp in first + passed:
            cp.wait_send()

    return pl.pallas_call(
        body, name=name, in_specs=[_ANY] * na, out_specs=[_ANY] * na,
        out_shape=[jax.ShapeDtypeStruct((N_DEV,) + s.shape, s.dtype) for s in srcs],
        scratch_shapes=[pltpu.SemaphoreType.DMA((na * (N_DEV - 1),)), pltpu.SemaphoreType.DMA((na * (N_DEV - 1),))],
    )(*srcs)


def _exchange_start(srcs, scatter, after, name):
    na = len(srcs)
    land_shapes = [s.shape if scatter else (N_DEV,) + s.shape for s in srcs]
    has_after = after is not None

    def body(*refs):
        src_refs, land_refs = refs[:na], refs[na:2 * na]
        send_sems, recv_sems = refs[2 * na + has_after], refs[2 * na + has_after + 1]
        token = refs[-1]
        for k in range(1, N_DEV):
            for a in range(na):
                _split_copy(src_refs, land_refs, send_sems, recv_sems, scatter, a, k, True).start()
        token[...] = jnp.zeros_like(token)

    sems = pltpu.SemaphoreType.DMA((na * (N_DEV - 1),))
    out_shape = ([sems, sems] + [pltpu.HBM(s.shape, s.dtype) for s in srcs]
                 + [pltpu.HBM(shp, s.dtype) for shp, s in zip(land_shapes, srcs)] + [jax.ShapeDtypeStruct((SUBLANES, 128), F32)])
    args = [pltpu.with_memory_space_constraint(s, pltpu.HBM) for s in srcs]
    args += [pltpu.with_memory_space_constraint(lax.empty(shp, s.dtype), pltpu.HBM) for shp, s in zip(land_shapes, srcs)]
    if has_after:
        args.append(after)
    res = pl.pallas_call(
        body, name=name, in_specs=[_HBM] * (2 * na) + [_ANY] * has_after,
        out_specs=[_SEM, _SEM] + [_HBM] * (2 * na) + [pl.BlockSpec(memory_space=pltpu.VMEM)], out_shape=out_shape,
        input_output_aliases={i: 2 + i for i in range(2 * na)},
        compiler_params=pltpu.CompilerParams(has_side_effects=pltpu.SideEffectType.DATAFLOW_SIDE_EFFECTING),
    )(*args)
    handle = dict(send=res[0], recv=res[1], srcs=list(res[2:2 + na]), lands=list(res[2 + na:2 + 2 * na]), scatter=scatter)
    return handle, res[-1]


def _exchange_wait(handle, after, name):
    srcs, lands, scatter = handle["srcs"], handle["lands"], handle["scatter"]
    na = len(srcs)

    def body(*refs):
        src_refs, land_refs = refs[:na], refs[na:2 * na]
        send_sems, recv_sems = refs[2 * na], refs[2 * na + 1]
        for k in range(1, N_DEV):
            for a in range(na):
                _split_copy(src_refs, land_refs, send_sems, recv_sems, scatter, a, k, True).wait_send()
                _split_copy(src_refs, land_refs, send_sems, recv_sems, scatter, a, k, False).wait_recv()

    res = pl.pallas_call(
        body, name=name, in_specs=[_HBM] * (2 * na) + [_SEM, _SEM, _ANY], out_specs=[_HBM] * (2 * na),
        out_shape=[pltpu.HBM(s.shape, s.dtype) for s in srcs] + [pltpu.HBM(s.shape, s.dtype) for s in lands],
        input_output_aliases={i: i for i in range(2 * na)},
        compiler_params=pltpu.CompilerParams(has_side_effects=pltpu.SideEffectType.DATAFLOW_SIDE_EFFECTING),
    )(*srcs, *lands, handle["send"], handle["recv"], after)
    return list(res[:na]), list(res[na:])


def _layout(D):
    aq, akv = SWA_Q_HEADS * SWA_HEAD_DIM, SWA_KV_HEADS * SWA_HEAD_DIM
    w = SGU_GROUPS * SGU_DIM
    return aq, akv, w


class _Seg:
    def __init__(self, D, rq, rkv):
        aq, akv, w = _layout(D)
        src = {}
        o = 0
        for nm, wd in (("qa", aq), ("ka", akv), ("va", akv), ("cq", rq), ("ckv", rkv), ("kr", MLA_ROPE), ("hu", w), ("hv", w),
                       ("g", 3 * D)):
            src[nm] = (o, wd)
            o += wd
        self.n_in = o
        self.order = ("g", "qa", "hu", "hv", "cq", "ckv", "ka", "va", "kr")
        self.src = src
        self.off = {}
        o = 0
        for nm in self.order:
            self.off[nm] = o
            o += src[nm][1]
        self.width = {nm: src[nm][1] for nm in self.order}
        self.n_pad = -(-o // 1536) * 1536 if o > 1536 else -(-o // 512) * 512
        self.used = o

    def from_shards(self, shards):
        c = shards.shape[2]
        parts = []
        for nm in self.order:
            s0, wd = self.src[nm]
            for d in range(N_DEV):
                lo, hi = max(s0, c * d), min(s0 + wd, c * (d + 1))
                if lo < hi:
                    parts.append(shards[d][:, lo - c * d:hi - c * d])
        parts.append(jnp.zeros((shards.shape[1], self.n_pad - self.used), shards.dtype))
        return jnp.concatenate(parts, axis=1)

    def to_shards(self, w):
        c = self.n_in // N_DEV
        names = sorted(self.order, key=lambda nm: self.src[nm][0])
        shards = []
        for d in range(N_DEV):
            parts = []
            for nm in names:
                s0, wd = self.src[nm]
                lo, hi = max(s0, c * d), min(s0 + wd, c * (d + 1))
                if lo < hi:
                    parts.append(w[:, self.off[nm] + lo - s0:self.off[nm] + hi - s0])
            shards.append(jnp.concatenate(parts, axis=1))
        return jnp.stack(shards)


def _uq_permute(w):
    R = w.shape[0]
    H = MLA_HEADS
    w3 = w.reshape(R, H, MLA_NOPE + MLA_ROPE)
    return jnp.concatenate([w3[:, :, :MLA_NOPE].reshape(R, H * MLA_NOPE), w3[:, :, MLA_NOPE:].reshape(R, H * MLA_ROPE)], axis=1)


def _uq_unpermute(w):
    R = w.shape[0]
    H = MLA_HEADS
    n = w[:, :H * MLA_NOPE].reshape(R, H, MLA_NOPE)
    r = w[:, H * MLA_NOPE:].reshape(R, H, MLA_ROPE)
    return jnp.concatenate([n, r], axis=2).reshape(R, H * (MLA_NOPE + MLA_ROPE))


def _ukv_permute(w):
    R = w.shape[0]
    w3 = w.reshape(R, MLA_HEADS, MLA_NOPE + MLA_V)
    return jnp.concatenate([w3[:, :, :MLA_NOPE].reshape(R, -1), w3[:, :, MLA_NOPE:].reshape(R, -1)], axis=1)


def _ukv_unpermute(w):
    R = w.shape[0]
    H = MLA_HEADS
    k = w[:, :H * MLA_NOPE].reshape(R, H, MLA_NOPE)
    v = w[:, H * MLA_NOPE:].reshape(R, H, MLA_V)
    return jnp.concatenate([k, v], axis=2).reshape(R, H * (MLA_NOPE + MLA_V))


GROUPS = {"a": ("w_in",), "b": ("w_uq", "w_ukv", "w_proj_a", "w_proj_b", "w_proj_c", "w_o", "b_gate"),
          "c": ("w_up", "w_down", "conv_w")}


def _layer_fwd(l, x, x16, fetch, P, cs, sn, seg, alpha):
    S, D = x.shape
    H, half = MLA_HEADS, MLA_ROPE // 2
    off = seg.off
    nm = lambda s: f"l{l}_{s}"
    sv = {"x16": x16}
    W = {"w_in": seg.from_shards(fetch(l, "a", x16)["w_in"])}
    h = _mm(x16, W["w_in"], "nn", [(F32, "n")], nm("h"))[0]
    sv["h"] = h
    ya, lse_a = _swa_fwd(h, off["qa"], off["ka"], off["va"], P["sinks"], nm("swa_fwd"))
    cqn, rq = _rms_fwd(h, off["cq"], seg.width["cq"], P["q_norm_g"], nm("rmsq_fwd"))
    ckvn, rkv = _rms_fwd(h, off["ckv"], seg.width["ckv"], P["kv_norm_g"], nm("rmskv_fwd"))
    W.update(fetch(l, "b", cqn))
    W["w_uq"] = _uq_permute(W["w_uq"])
    W["w_ukv"] = _ukv_permute(W["w_ukv"])
    qf = _mm(cqn, W["w_uq"], "nn", [(F32, "n")], nm("uq"))[0]
    kvf = _mm(ckvn, W["w_ukv"], "nn", [(BF16, "n")], nm("ukv"))[0]
    qr = _rope(qf, H * MLA_NOPE, H * MLA_ROPE, cs, sn, False, nm("ropeq_fwd"))
    kr = _rope(h, off["kr"], LANES, cs, sn, False, nm("ropek_fwd"))
    yb, yb16, lse_b = _mla_fwd(qf, qr, kvf, kr, nm("mla_fwd"))
    w16 = jnp.where(jnp.tril(jnp.ones((SGU_CHUNK, SGU_CHUNK), bool))[None], P["sgu_w"], 0.0).astype(BF16)
    bt = P["sgu_b"].T
    yc = _sgu_fwd(h, off["hu"], off["hv"], P["sgu_ln_g"], P["sgu_ln_b"], w16, bt, nm("sgu_fwd"))
    merged, z = _merge_fwd([ya, yb16, yc], [W["w_proj_a"], W["w_proj_b"], W["w_proj_c"]], h, W["b_gate"], nm("merge_fwd"))
    x1, x1_16, xh1, rs1 = _mm_ln(merged, W["w_o"], x, P["ln1_g"], P["ln1_b"], alpha, nm("wo_ln1"))
    W.update(fetch(l, "c", x1_16))
    up = _mm(x1_16, W["w_up"], "nn", [(F32, "n")], nm("up"))[0]
    a = _glu_fwd(up, W["conv_w"], P["conv_b"], nm("glu_fwd"))
    x2, x2_16, xh2, rs2 = _mm_ln(a, W["w_down"], x1, P["ln2_g"], P["ln2_b"], alpha, nm("down_ln2"))
    sv.update(W=W, ya=ya, lse_a=lse_a, cqn=cqn, rq=rq, ckvn=ckvn, rkv=rkv, qf=qf, qr=qr, kvf=kvf, kr=kr, lse_b=lse_b, yb=yb, yb16=yb16,
              w16=w16, bt=bt, yc=yc, merged=merged, z=z, x1_16=x1_16, xh1=xh1, rs1=rs1, up=up, a=a, xh2=xh2, rs2=rs2)
    return x2, x2_16, sv


def _dw_chunks(k, a, dy, name, post=None, chunker=None):
    n = dy.shape[1]
    if k not in ROW_SHARDED and post is None and chunker is None and (n // N_DEV) % 128 == 0:
        return _mm(a, dy, "tn", [(BF16, "n")], name, chunk=n // N_DEV)[0]
    g = _mm(a, dy, "tn", [(BF16, "n")], name)[0]
    if chunker is not None:
        return chunker(g)
    return _to_chunks(k, g if post is None else post(g))


def _after(arr, token):
    return arr if token is None else arr + token[0:1, 0:1].astype(arr.dtype)


def _layer_bwd(l, dx2, sv, P, cs, sn, seg, alpha, emit):
    S, D = dx2.shape
    H, half = MLA_HEADS, MLA_ROPE // 2
    off = seg.off
    h, W = sv["h"], sv["W"]
    nm = lambda s: f"l{l}_{s}"
    g = {}
    dr2, dr2_16, g["ln2_g"], g["ln2_b"] = _ln_bwd(dx2, sv["xh2"], sv["rs2"], P["ln2_g"], nm("ln2_bwd"))
    g["w_down"] = _dw_chunks("w_down", sv["a"], dr2_16, nm("dw_down"))
    da = _mm(dr2_16, W["w_down"], "nt", [(F32, "n")], nm("da"))[0]
    dcg, dcv, st_g, st_v = _glu_bwd(sv["up"], W["conv_w"], P["conv_b"], da, nm("glu_bwd"))
    F = dcg.shape[1]
    g["conv_w"] = _to_chunks("conv_w", jnp.concatenate([st_g[0:3], st_v[0:3]], axis=1))
    g["conv_b"] = jnp.concatenate([st_g[3:4], st_v[3:4]], axis=1)
    dup = _conv_bwd(dcg, dcv, W["conv_w"], nm("conv_bwd"))
    g["w_up"] = _dw_chunks("w_up", sv["x1_16"], dup, nm("dw_up"))
    token = emit(l, "c", {k: g.pop(k) for k in GROUPS["c"]})
    dx1 = _mm_axpy(dup, W["w_up"], "nt", dr2, alpha, nm("dx1"), dep=token)
    dr1, dr1_16, g["ln1_g"], g["ln1_b"] = _ln_bwd(dx1, sv["xh1"], sv["rs1"], P["ln1_g"], nm("ln1_bwd"))
    g["w_o"] = _dw_chunks("w_o", sv["merged"], dr1_16, nm("dw_o"))
    dmerged = _mm(dr1_16, W["w_o"], "nt", [(F32, "n")], nm("dmerged"))[0]
    dz, dlog, dbg = _merge_bwd(dmerged, sv["z"], h, W["b_gate"], nm("merge_bwd"))
    g["b_gate"] = _to_chunks("b_gate", dbg[0:3])
    g["w_proj_a"] = _dw_chunks("w_proj_a", sv["ya"], dz[0], nm("dw_pa"))
    g["w_proj_b"] = _dw_chunks("w_proj_b", sv["yb16"], dz[1], nm("dw_pb"))
    g["w_proj_c"] = _dw_chunks("w_proj_c", sv["yc"], dz[2], nm("dw_pc"))
    dya = _mm(dz[0], W["w_proj_a"], "nt", [(F32, "n")], nm("dya"))[0]
    dyb = _mm(dz[1], W["w_proj_b"], "nt", [(F32, "n")], nm("dyb"))[0]
    dyc = _mm(dz[2], W["w_proj_c"], "nt", [(F32, "n")], nm("dyc"))[0]
    dhu, dhv, g["sgu_w"], db_s, g["sgu_ln_g"], g["sgu_ln_b"] = _sgu_bwd(
        h, off["hu"], off["hv"], P["sgu_ln_g"], P["sgu_ln_b"], sv["w16"], sv["bt"], dyc, nm("sgu_bwd"))
    g["sgu_b"] = db_s[:, :SGU_GROUPS].T
    dqa, dka, dva, dsk = _swa_bwd(h, off["qa"], off["ka"], off["va"], P["sinks"], dya, sv["lse_a"], nm("swa_bwd"))
    g["sinks"] = dsk[0, :SWA_Q_HEADS]
    dqn, dqr, dkn, dvv, dkr = _mla_bwd(sv["qf"], sv["qr"], sv["kvf"], sv["kr"], dyb, sv["yb"], sv["lse_b"], nm("mla_bwd"))
    dqf = jnp.concatenate([dqn.astype(BF16), _rope(dqr, 0, H * MLA_ROPE, cs, sn, True, nm("ropeq_bwd"))], axis=1)
    dkvf = jnp.concatenate([dkn, dvv], axis=1).astype(BF16)
    dkr16 = _rope(dkr, 0, LANES, cs, sn, True, nm("ropek_bwd"))
    g["w_uq"] = _dw_chunks("w_uq", sv["cqn"], dqf, nm("dw_uq"), _uq_unpermute)
    g["w_ukv"] = _dw_chunks("w_ukv", sv["ckvn"], dkvf, nm("dw_ukv"), _ukv_unpermute)
    token = emit(l, "b", {k: g.pop(k) for k in GROUPS["b"]})
    dcqn = _mm(dqf, W["w_uq"], "nt", [(F32, "n")], nm("dcqn"), dep=token)[0]
    dckvn = _mm(dkvf, W["w_ukv"], "nt", [(F32, "n")], nm("dckvn"), dep=token)[0]
    dcq, g["q_norm_g"] = _rms_bwd(dcqn, h, off["cq"], seg.width["cq"], sv["rq"], P["q_norm_g"], nm("rmsq_bwd"))
    dckv, g["kv_norm_g"] = _rms_bwd(dckvn, h, off["ckv"], seg.width["ckv"], sv["rkv"], P["kv_norm_g"], nm("rmskv_bwd"))
    assert seg.order[-1] == "kr" and seg.n_pad - seg.used >= LANES - MLA_ROPE
    parts = {"g": jnp.concatenate([dlog[0], dlog[1], dlog[2]], axis=1), "qa": dqa, "hu": dhu, "hv": dhv, "cq": dcq, "ckv": dckv,
             "ka": dka.astype(BF16), "va": dva.astype(BF16), "kr": dkr16}
    dh = jnp.concatenate([parts[k] for k in seg.order] + [jnp.zeros((S, seg.n_pad - seg.used - (LANES - MLA_ROPE)), BF16)],
                         axis=1)
    token = emit(l, "a", {"w_in": _dw_chunks("w_in", sv["x16"], dh, nm("dw_in"), chunker=seg.to_shards)})
    dx = _mm_axpy(dh, W["w_in"], "nt", dr1, alpha, nm("dx"), dep=token)
    return dx, g


BIG = ("w_in", "w_uq", "w_ukv", "w_proj_a", "w_proj_b", "w_proj_c", "w_o", "w_up", "w_down")
ROW_SHARDED = ("w_proj_b", "w_o", "w_down")
SHARDED_F32 = ("b_gate", "conv_w")
REPLICATED = ("sinks", "q_norm_g", "kv_norm_g", "sgu_ln_g", "sgu_ln_b", "sgu_w", "sgu_b", "ln1_g", "ln1_b", "conv_b", "ln2_g",
              "ln2_b")
WEIGHTS = ("w_in", "b_gate", "sinks", "q_norm_g", "kv_norm_g", "w_uq", "w_ukv", "sgu_ln_g", "sgu_ln_b", "sgu_w", "sgu_b",
           "w_proj_a", "w_proj_b", "w_proj_c", "w_o", "ln1_g", "ln1_b", "w_up", "conv_w", "conv_b", "w_down", "ln2_g", "ln2_b")


def _step_local(x, positions, target, small, rq, rkv, fetch, emit, token=None):
    S, D = x.shape
    L = small["sinks"].shape[0]
    alpha = (2 * L) ** 0.25
    seg = _Seg(D, rq, rkv)
    inv_freq = ROPE_THETA ** (-jnp.arange(0, MLA_ROPE, 2, dtype=F32) / MLA_ROPE)
    ang = positions.astype(F32)[:, None] * inv_freq
    reps = LANES // (MLA_ROPE // 2)
    cs, sn = jnp.tile(jnp.cos(ang), (1, reps)), jnp.tile(jnp.sin(ang), (1, reps))
    rows = ("q_norm_g", "kv_norm_g", "sgu_ln_g", "sgu_ln_b", "ln1_g", "ln1_b", "conv_b", "ln2_g", "ln2_b")
    layers = [{k: small[k][l].reshape(1, -1) if k in rows else small[k][l] for k in small} for l in range(L)]
    saved = []
    x16 = _after(x, token).astype(BF16)
    for l in range(L):
        x, x16, sv = _layer_fwd(l, x, x16, fetch, layers[l], cs, sn, seg, alpha)
        saved.append(sv)
    loss, dx = _loss(x, target, "loss")
    grads = [None] * L
    for l in reversed(range(L)):
        dx, grads[l] = _layer_bwd(l, dx, saved[l], layers[l], cs, sn, seg, alpha, emit)
    out = {k: jnp.stack([grads[l][k].reshape(small[k].shape[1:]) for l in range(L)]) for k in small}
    return loss, dx, out


def _unshard(k, gathered):
    n, r, c = gathered.shape
    if k in ROW_SHARDED:
        return gathered.reshape(n * r, c)
    return gathered.transpose(1, 0, 2).reshape(r, n * c)


def _to_chunks(k, gfull):
    r, c = gfull.shape
    if k in ROW_SHARDED:
        return gfull.reshape(N_DEV, r // N_DEV, c)
    return gfull.reshape(r, N_DEV, c // N_DEV).transpose(1, 0, 2)


def _pack(arrs):
    P = arrs[0].shape[0]
    flat = jnp.concatenate([a.reshape(P, -1) for a in arrs], axis=1)
    sizes = [(a.size // P, a.size // P) for a in arrs]
    flat = jnp.pad(flat, ((0, 0), (0, -flat.shape[1] % (SUBLANES * 128))))
    return flat.reshape(P, -1, 128), sizes


def _unpack(packed, sizes, shapes):
    flat = packed.reshape(-1)
    out, o = [], 0
    for (n, npad), shp in zip(sizes, shapes):
        out.append(flat[o:o + n].reshape(shp))
        o += npad
    return out


def kernel(x, positions, w_in, b_gate, sinks, q_norm_g, kv_norm_g, w_uq, w_ukv, sgu_ln_g, sgu_ln_b, sgu_w, sgu_b, w_proj_a, w_proj_b, w_proj_c, w_o, ln1_g, ln1_b, w_up, conv_w, conv_b, w_down, ln2_g, ln2_b, loss_target, m_w_in, m_b_gate, m_sinks, m_q_norm_g, m_kv_norm_g, m_w_uq, m_w_ukv, m_sgu_ln_g, m_sgu_ln_b, m_sgu_w, m_sgu_b, m_w_proj_a, m_w_proj_b, m_w_proj_c, m_w_o, m_ln1_g, m_ln1_b, m_w_up, m_conv_w, m_conv_b, m_w_down, m_ln2_g, m_ln2_b, v_w_in, v_b_gate, v_sinks, v_q_norm_g, v_kv_norm_g, v_w_uq, v_w_ukv, v_sgu_ln_g, v_sgu_ln_b, v_sgu_w, v_sgu_b, v_w_proj_a, v_w_proj_b, v_w_proj_c, v_w_o, v_ln1_g, v_ln1_b, v_w_up, v_conv_w, v_conv_b, v_w_down, v_ln2_g, v_ln2_b):
    given = dict(locals())
    w = {k: given[k] for k in WEIGHTS}
    mom = {k: given["m_" + k] for k in WEIGHTS}
    var = {k: given["v_" + k] for k in WEIGHTS}

    L = w_in.shape[0]
    order = [(l, grp) for l in range(L) for grp in ("a", "b", "c")]

    first = [w[k][0].astype(BF16) for k in GROUPS["a"]]
    first_lands = _gather_two_level(first, "gather_first")
    gathers, token = {}, first_lands[0]
    for l, grp in order[1:]:
        srcs = [w[k][l].astype(BF16) if k in BIG else w[k][l] for k in GROUPS[grp]]
        gathers[l, grp], token = _exchange_start(srcs, False, token, f"gather_start_l{l}{grp}")

    me = 4 * lax.axis_index("x") + 2 * lax.axis_index("y") + lax.axis_index("c")
    mine = (jnp.arange(N_DEV) == me)[:, None, None]

    def fetch(l, grp, after):
        if (l, grp) == order[0]:
            srcs, lands = first, first_lands
        else:
            srcs, lands = _exchange_wait(gathers[l, grp], after, f"gather_wait_l{l}{grp}")
        full = {k: jnp.where(mine, srcs[i][None], lands[i]) for i, k in enumerate(GROUPS[grp])}
        return {k: v if k == "w_in" else _unshard(k, v) for k, v in full.items()}

    scatters = {}

    def emit(l, grp, chunks):
        scatters[l, grp], tok = _exchange_start([chunks[k] for k in GROUPS[grp]], True, None, f"scatter_start_l{l}{grp}")
        return tok

    small = {k: w[k] for k in REPLICATED}
    loss, grad_x, g = _step_local(x[0], positions[0], loss_target[0], small, w_uq.shape[1], w_ukv.shape[1], fetch, emit, token)
    loss = lax.psum(loss[0, 0], AXES)

    packed, sizes = _pack([g[k][None] for k in REPLICATED])
    small_grads, after = _exchange_start([packed[0]], False, grad_x, "gather_small_grads_start")

    me1 = me.astype(jnp.int32).reshape(1)
    res = {}
    for grp in ("c", "b", "a"):
        slots, own = {}, {}
        for l in reversed(range(L)):
            srcs, lands = _exchange_wait(scatters[l, grp], after, f"scatter_wait_l{l}{grp}")
            for i, k in enumerate(GROUPS[grp]):
                slots[k, l], own[k, l] = lands[i], srcs[i]
        for k in GROUPS[grp]:
            res[k] = _adamw([slots[k, l] for l in range(L)], [own[k, l] for l in range(L)], me1, w[k], mom[k], var[k],
                            "adamw_" + k, dep=after)
            after = res[k][1]

    srcs, lands = _exchange_wait(small_grads, after, "gather_small_grads_wait")
    parts = jnp.where(mine, srcs[0][None], lands[0])
    shapes = [w[k].shape for k in REPLICATED]
    pw, _ = _pack([w[k][None] for k in REPLICATED])
    pm, _ = _pack([mom[k][None] for k in REPLICATED])
    pv, _ = _pack([var[k][None] for k in REPLICATED])
    outs = _adamw([parts], None, me1, pw, pm, pv, "adamw_small")
    unpacked = [_unpack(o, sizes, shapes) for o in outs]
    for i, k in enumerate(REPLICATED):
        res[k] = [unpacked[j][i] for j in range(4)]

    return (loss, grad_x[None], *[res[k][0] for k in WEIGHTS], *[res[k][1] for k in WEIGHTS],
            *[res[k][2] for k in WEIGHTS], *[res[k][3] for k in WEIGHTS])
```

```python
import functools
import math

import jax
import jax.numpy as jnp
from jax import lax
from jax.experimental import pallas as pl
from jax.experimental.pallas import tpu as pltpu

F32 = jnp.float32
BF16 = jnp.bfloat16

SWA_Q_HEADS = 16
SWA_KV_HEADS = 2
SWA_HEAD_DIM = 64
SWA_BLOCK = 128
MLA_HEADS = 16
MLA_NOPE = 128
MLA_ROPE = 64
MLA_V = 128
SGU_GROUPS = 8
SGU_DIM = 128
SGU_CHUNK = 128
ROPE_THETA = 10000.0
EPS = 1e-5
MASK = -1e30
ADAM_LR = 0.001
ADAM_B1 = 0.9
ADAM_B2 = 0.999
ADAM_EPS = 1e-08
ADAM_WD = 0.01
ADAM_STEP = 10

N_DEV = 8
AXES = ("x", "y", "c")
VMEM_LIMIT = 56 * 1024 * 1024
MLA_TILE = 512
MLA_FWD_TILE = 1024
ROW_TILE = 512
CONV_BWD_ROWS = 2048
GLU_ROWS = 1024
MAX_TK = 2816
SUBLANES = 8


def _tile(n, prefs):
    for p in prefs:
        if n % p == 0:
            return p
    return n


def _params(sem):
    return pltpu.CompilerParams(dimension_semantics=sem, vmem_limit_bytes=VMEM_LIMIT)


def _cols(tm, width, off):
    assert off % width == 0, (off, width)
    blk = off // width
    return pl.BlockSpec((tm, width), lambda i, *_: (i, blk))


def _full(shape):
    nd = len(shape)
    return pl.BlockSpec(shape, lambda *_: (0,) * nd)


def _sigmoid(v):
    return 1.0 / (1.0 + jnp.exp(-v))


def _gelu(v):
    return 0.5 * v * (1.0 + lax.erf(v * (2.0 ** -0.5)))


def _gelu_grad(v):
    return 0.5 * (1.0 + lax.erf(v * (2.0 ** -0.5))) + v * jnp.exp(-0.5 * v * v) * (1.0 / math.sqrt(2.0 * math.pi))


_DIMS = {"nn": (((1,), (0,)), ((), ())), "nt": (((1,), (1,)), ((), ())), "tn": (((0,), (0,)), ((), ()))}


def _mm(a, b, mode, outs, name, *, extras=(), epilogue=None, full_n=False, dep=None, chunk=None):
    if mode == "nn":
        (M, K), (K2, N) = a.shape, b.shape
    elif mode == "nt":
        (M, K), (N, K2) = a.shape, b.shape
    else:
        (K, M), (K2, N) = a.shape, b.shape
    assert K == K2, (a.shape, b.shape, mode)
    tm = _tile(M, (1024, 512, 256, 128))
    tn = N if full_n else _tile(N, (1024, 768, 512, 384, 256, 128))
    if full_n:
        tm = _tile(M, (512, 256, 128))
    if chunk is not None:
        tn = chunk if chunk <= 1536 else _tile(chunk, (1024, 768, 512, 384, 256, 128))
        assert N % chunk == 0 and chunk % tn == 0 and tn % 128 == 0, (N, chunk, tn)
    max_tk = MAX_TK // 2 if full_n else MAX_TK
    tk = max(d for d in range(128, min(K, max_tk) + 1, 128) if K % d == 0) if K % 128 == 0 else K
    nk = K // tk
    if mode == "nn":
        a_spec = pl.BlockSpec((tm, tk), lambda i, j, k: (i, k))
        b_spec = pl.BlockSpec((tk, tn), lambda i, j, k: (k, j))
    elif mode == "nt":
        a_spec = pl.BlockSpec((tm, tk), lambda i, j, k: (i, k))
        b_spec = pl.BlockSpec((tn, tk), lambda i, j, k: (j, k))
    else:
        a_spec = pl.BlockSpec((tk, tm), lambda i, j, k: (k, i))
        b_spec = pl.BlockSpec((tk, tn), lambda i, j, k: (k, j))
    in_specs = [a_spec, b_spec]
    for arr, kind in extras:
        if kind == "tile":
            in_specs.append(pl.BlockSpec((tm, tn), lambda i, j, k: (i, j)))
        else:
            in_specs.append(pl.BlockSpec((1, tn), lambda i, j, k: (0, j)))
    out_specs, out_shape = [], []
    for dt, kind in outs:
        if kind == "n" and chunk is not None:
            per = chunk // tn
            out_specs.append(pl.BlockSpec((None, tm, tn), lambda i, j, k: (lax.div(j, per), i, lax.rem(j, per))))
            out_shape.append(jax.ShapeDtypeStruct((N // chunk, M, chunk), dt))
        elif kind == "n":
            out_specs.append(pl.BlockSpec((tm, tn), lambda i, j, k: (i, j)))
            out_shape.append(jax.ShapeDtypeStruct((M, N), dt))
        else:
            assert tn == N
            out_specs.append(pl.BlockSpec((tm, 1), lambda i, j, k: (i, 0)))
            out_shape.append(jax.ShapeDtypeStruct((M, 1), dt))
    ne, no = len(extras), len(outs)
    deps = []
    if dep is not None:
        in_specs.append(_full(dep.shape))
        deps = [dep]
    dims = _DIMS[mode]
    if epilogue is None:
        epilogue = lambda acc: (acc,) * no

    def body(*refs):
        a_ref, b_ref = refs[0], refs[1]
        ex = refs[2:2 + ne]
        out = refs[len(refs) - 1 - no:len(refs) - 1]
        acc = refs[-1]
        k = pl.program_id(2)
        part = lax.dot_general(a_ref[...].astype(BF16), b_ref[...].astype(BF16), dims, preferred_element_type=F32)

        def finish(total):
            res = epilogue(total, *[e[...] for e in ex])
            for o, r in zip(out, res):
                o[...] = r.astype(o.dtype)

        if nk == 1:
            finish(part)
            return

        @pl.when(k == 0)
        def _():
            acc[...] = part

        @pl.when((k > 0) & (k < nk - 1))
        def _():
            acc[...] += part

        @pl.when(k == nk - 1)
        def _():
            finish(acc[...] + part)

    res = pl.pallas_call(
        body, name=name, grid=(M // tm, N // tn, nk), in_specs=in_specs, out_specs=out_specs, out_shape=out_shape,
        scratch_shapes=[pltpu.VMEM((tm, tn), F32)],
        compiler_params=_params(("parallel", "parallel", "arbitrary")),
    )(a, b, *[e[0] for e in extras], *deps)
    return res


def _ln_epilogue(alpha):
    def epi(acc, x, g, b):
        r = alpha * x + acc
        mu = jnp.mean(r, axis=-1, keepdims=True)
        d = r - mu
        var = jnp.mean(d * d, axis=-1, keepdims=True)
        rstd = lax.rsqrt(var + EPS)
        xhat = d * rstd
        y = xhat * g + b
        return y, y, xhat, rstd
    return epi


def _mm_ln(a, w, x, g, b, alpha, name):
    return _mm(a, w, "nn", [(F32, "n"), (BF16, "n"), (F32, "n"), (F32, "1")], name,
               extras=[(x, "tile"), (g, "row"), (b, "row")], epilogue=_ln_epilogue(alpha), full_n=True)


def _mm_axpy(a, w, mode, r, alpha, name, dep=None):
    return _mm(a, w, mode, [(F32, "n")], name, extras=[(r, "tile")],
               epilogue=lambda acc, rv: (acc + alpha * rv,), dep=dep)[0]


def _ln_bwd(dy, xhat, rstd, g, name):
    S, D = dy.shape
    tm = _tile(S, (ROW_TILE, 256, 128))

    def body(dy_ref, xh_ref, rs_ref, g_ref, dr_ref, dr16_ref, dg_ref, db_ref):
        @pl.when(pl.program_id(0) == 0)
        def _():
            dg_ref[...] = jnp.zeros_like(dg_ref)
            db_ref[...] = jnp.zeros_like(db_ref)

        dyv, xh = dy_ref[...], xh_ref[...]
        dxh = dyv * g_ref[...]
        m1 = jnp.mean(dxh, axis=-1, keepdims=True)
        m2 = jnp.mean(dxh * xh, axis=-1, keepdims=True)
        dr = rs_ref[...] * (dxh - m1 - xh * m2)
        dr_ref[...] = dr
        dr16_ref[...] = dr.astype(BF16)
        dg_ref[...] += jnp.sum(dyv * xh, axis=0, keepdims=True)
        db_ref[...] += jnp.sum(dyv, axis=0, keepdims=True)

    row = pl.BlockSpec((tm, D), lambda i: (i, 0))
    return pl.pallas_call(
        body, name=name, grid=(S // tm,),
        in_specs=[row, row, pl.BlockSpec((tm, 1), lambda i: (i, 0)), _full((1, D))],
        out_specs=[row, row, _full((1, D)), _full((1, D))],
        out_shape=[jax.ShapeDtypeStruct((S, D), F32), jax.ShapeDtypeStruct((S, D), BF16),
                   jax.ShapeDtypeStruct((1, D), F32), jax.ShapeDtypeStruct((1, D), F32)],
        compiler_params=_params(("arbitrary",)),
    )(dy, xhat, rstd, g)


def _rms_fwd(h, off, width, g, name):
    S = h.shape[0]
    tm = _tile(S, (ROW_TILE, 256, 128))

    def body(c_ref, g_ref, y_ref, r_ref):
        c = c_ref[...]
        r = lax.rsqrt(jnp.mean(c * c, axis=-1, keepdims=True) + EPS)
        y_ref[...] = (c * r * g_ref[...]).astype(BF16)
        r_ref[...] = r

    return pl.pallas_call(
        body, name=name, grid=(S // tm,),
        in_specs=[_cols(tm, width, off), _full((1, width))],
        out_specs=[pl.BlockSpec((tm, width), lambda i: (i, 0)), pl.BlockSpec((tm, 1), lambda i: (i, 0))],
        out_shape=[jax.ShapeDtypeStruct((S, width), BF16), jax.ShapeDtypeStruct((S, 1), F32)],
        compiler_params=_params(("parallel",)),
    )(h, g)


def _rms_bwd(dy, h, off, width, rstd, g, name):
    S = h.shape[0]
    tm = _tile(S, (ROW_TILE, 256, 128))

    def body(dy_ref, c_ref, r_ref, g_ref, dc_ref, dg_ref):
        @pl.when(pl.program_id(0) == 0)
        def _():
            dg_ref[...] = jnp.zeros_like(dg_ref)

        dyv, c, r = dy_ref[...], c_ref[...], r_ref[...]
        dyg = dyv * g_ref[...]
        m = jnp.mean(dyg * c, axis=-1, keepdims=True)
        dc_ref[...] = (r * dyg - c * (r * r * r) * m).astype(BF16)
        dg_ref[...] += jnp.sum(dyv * c * r, axis=0, keepdims=True)

    return pl.pallas_call(
        body, name=name, grid=(S // tm,),
        in_specs=[pl.BlockSpec((tm, width), lambda i: (i, 0)), _cols(tm, width, off),
                  pl.BlockSpec((tm, 1), lambda i: (i, 0)), _full((1, width))],
        out_specs=[pl.BlockSpec((tm, width), lambda i: (i, 0)), _full((1, width))],
        out_shape=[jax.ShapeDtypeStruct((S, width), BF16), jax.ShapeDtypeStruct((1, width), F32)],
        compiler_params=_params(("arbitrary",)),
    )(dy, h, rstd, g)


def _loss(y, target, name):
    S, D = y.shape
    tm = _tile(S, (ROW_TILE, 256, 128))

    def body(y_ref, t_ref, l_ref, dy_ref):
        @pl.when(pl.program_id(0) == 0)
        def _():
            l_ref[...] = jnp.zeros_like(l_ref)

        err = y_ref[...] - t_ref[...]
        dy_ref[...] = err * (1.0 / D)
        per_tok = jnp.mean(err * err, axis=-1, keepdims=True)
        l_ref[...] += 0.5 * jnp.sum(per_tok, axis=0, keepdims=True)

    row = pl.BlockSpec((tm, D), lambda i: (i, 0))
    return pl.pallas_call(
        body, name=name, grid=(S // tm,), in_specs=[row, row], out_specs=[_full((1, 1)), row],
        out_shape=[jax.ShapeDtypeStruct((1, 1), F32), jax.ShapeDtypeStruct((S, D), F32)],
        compiler_params=_params(("arbitrary",)),
    )(y, target)


LANES = 128
LOG2E = 1.4426950408889634


def _rope(x, off, width, cs2, sn2, bwd, name):
    S = cs2.shape[0]
    tm = _tile(S, (ROW_TILE, 256, 128))
    stacked = x.ndim == 3
    half = MLA_ROPE // 2
    assert width % LANES == 0 and MLA_ROPE * 2 == LANES

    def rot(v):
        lane = lax.broadcasted_iota(jnp.int32, v.shape, 1)
        return jnp.where((lane & (MLA_ROPE - 1)) < half, -pltpu.roll(v, LANES - half, 1), pltpu.roll(v, half, 1))

    def body(x_ref, c_ref, s_ref, y_ref):
        c, s = c_ref[...], s_ref[...]
        for g in range(width // LANES):
            cols = slice(g * LANES, (g + 1) * LANES)
            v = jnp.sum(x_ref[...], axis=0) if stacked else x_ref[:, cols].astype(F32)
            y = v * c - rot(v * s) if bwd else v * c + rot(v) * s
            y_ref[:, cols] = y.astype(BF16)

    row = pl.BlockSpec((tm, LANES), lambda i: (i, 0))
    x_spec = pl.BlockSpec((x.shape[0], tm, LANES), lambda i: (0, i, 0)) if stacked else _cols(tm, width, off)
    return pl.pallas_call(
        body, name=name, grid=(S // tm,), in_specs=[x_spec, row, row],
        out_specs=pl.BlockSpec((tm, width), lambda i: (i, 0)), out_shape=jax.ShapeDtypeStruct((S, width), BF16),
        compiler_params=_params(("parallel",)),
    )(x, cs2, sn2)


def _swa_mask(n, rows):
    blk = SWA_BLOCK
    row = lax.broadcasted_iota(jnp.int32, (rows, 2 * blk), 0) & (blk - 1)
    col = lax.broadcasted_iota(jnp.int32, (rows, 2 * blk), 1)
    rel = row + blk - col
    return (rel >= 0) & (rel < blk) & ((n > 0) | (col >= blk))


def _swa_specs(off_q, off_k, off_v, stacked):
    blk, aq, akv = SWA_BLOCK, SWA_Q_HEADS * SWA_HEAD_DIM, SWA_KV_HEADS * SWA_HEAD_DIM
    grp = SWA_Q_HEADS // SWA_KV_HEADS
    assert off_q % aq == 0 and off_k % akv == 0 and off_v % akv == 0 and blk & (blk - 1) == 0
    prev = lambda off: pl.BlockSpec((blk, akv), lambda n: (jnp.maximum(n - 1, 0), off // akv))
    cur = lambda off: pl.BlockSpec((blk, akv), lambda n: (n, off // akv))
    sink = _full((SWA_KV_HEADS, grp * blk, 1)) if stacked else pl.BlockSpec(memory_space=pltpu.SMEM)
    return [sink, _cols(blk, aq, off_q), prev(off_k), cur(off_k), prev(off_v), cur(off_v)]


def _swa_sinks(sinks):
    grp = SWA_Q_HEADS // SWA_KV_HEADS
    return jnp.repeat(sinks.reshape(SWA_KV_HEADS, grp), SWA_BLOCK, axis=1)[:, :, None]


def _swa_stack(x, kv):
    hd, grp = SWA_HEAD_DIM, SWA_Q_HEADS // SWA_KV_HEADS
    return jnp.concatenate([x[:, (kv * grp + g) * hd:(kv * grp + g + 1) * hd] for g in range(grp)], axis=0)


def _swa_fwd(h, off_q, off_k, off_v, sinks, name):
    S = h.shape[0]
    blk, hd, nh, nkv = SWA_BLOCK, SWA_HEAD_DIM, SWA_Q_HEADS, SWA_KV_HEADS
    grp = nh // nkv
    aq = nh * hd
    scale = hd ** -0.5

    def body(sink_ref, q_ref, kp_ref, kc_ref, vp_ref, vc_ref, o_ref, lse_ref):
        valid = _swa_mask(pl.program_id(0), blk)
        q = q_ref[...].astype(BF16)
        k2 = jnp.concatenate([kp_ref[...], kc_ref[...]], axis=0).astype(BF16)
        v2 = jnp.concatenate([vp_ref[...], vc_ref[...]], axis=0).astype(BF16)
        for hh in range(nh):
            kv = hh // grp
            qh = q[:, hh * hd:(hh + 1) * hd]
            kh = k2[:, kv * hd:(kv + 1) * hd]
            vh = v2[:, kv * hd:(kv + 1) * hd]
            s = lax.dot_general(qh, kh, _DIMS["nt"], preferred_element_type=F32) * scale
            s = jnp.where(valid, s, MASK)
            sk = sink_ref[hh]
            m = jnp.maximum(jnp.max(s, axis=1, keepdims=True), sk)
            p = jnp.exp(s - m)
            l = jnp.sum(p, axis=1, keepdims=True) + jnp.exp(sk - m)
            o_ref[:, hh * hd:(hh + 1) * hd] = jnp.dot((p / l).astype(BF16), vh, preferred_element_type=F32).astype(BF16)
            lse_ref[:, hh:hh + 1] = m + jnp.log(l)

    return pl.pallas_call(
        body, name=name, grid=(S // blk,), in_specs=_swa_specs(off_q, off_k, off_v, False),
        out_specs=[pl.BlockSpec((blk, aq), lambda n: (n, 0)), pl.BlockSpec((blk, nh), lambda n: (n, 0))],
        out_shape=[jax.ShapeDtypeStruct((S, aq), BF16), jax.ShapeDtypeStruct((S, nh), F32)],
        compiler_params=_params(("parallel",)),
    )(sinks, h, h, h, h, h)


def _swa_bwd(h, off_q, off_k, off_v, sinks, dout, lse, name):
    S = h.shape[0]
    blk, hd, nh, nkv = SWA_BLOCK, SWA_HEAD_DIM, SWA_Q_HEADS, SWA_KV_HEADS
    grp = nh // nkv
    aq, akv = nh * hd, nkv * hd
    scale = hd ** -0.5

    def body(sink_ref, q_ref, kp_ref, kc_ref, vp_ref, vc_ref, do_ref, lse_ref, dq_ref, dk_ref, dv_ref, ds_ref):
        n = pl.program_id(0)

        @pl.when(n == 0)
        def _():
            dk_ref[...] = jnp.zeros_like(dk_ref)
            dv_ref[...] = jnp.zeros_like(dv_ref)
            ds_ref[...] = jnp.zeros_like(ds_ref)

        valid = _swa_mask(n, grp * blk)
        q = q_ref[...].astype(BF16)
        k2 = jnp.concatenate([kp_ref[...], kc_ref[...]], axis=0).astype(BF16)
        v2 = jnp.concatenate([vp_ref[...], vc_ref[...]], axis=0).astype(BF16)
        do = do_ref[...]
        lane = lax.broadcasted_iota(jnp.int32, (1, 128), 1)
        dsink = jnp.zeros((1, 128), F32)
        cur = pl.ds(pl.multiple_of(n * blk, blk), blk)
        prev = pl.ds(pl.multiple_of(jnp.maximum(n - 1, 0) * blk, blk), blk)
        for kv in range(nkv):
            kh = k2[:, kv * hd:(kv + 1) * hd]
            vh = v2[:, kv * hd:(kv + 1) * hd]
            qs = _swa_stack(q, kv)
            dos = _swa_stack(do, kv)
            dos16 = dos.astype(BF16)
            lse = jnp.concatenate([lse_ref[:, kv * grp + g:kv * grp + g + 1] for g in range(grp)], axis=0)
            s = lax.dot_general(qs, kh, _DIMS["nt"], preferred_element_type=F32) * scale
            s = jnp.where(valid, s, MASK)
            p = jnp.exp(s - lse)
            p16 = p.astype(BF16)
            o = jnp.dot(p16, vh, preferred_element_type=F32)
            delta = jnp.sum(dos * o, axis=1, keepdims=True)
            dp = lax.dot_general(dos16, vh, _DIMS["nt"], preferred_element_type=F32)
            ds16 = (p * (dp - delta) * scale).astype(BF16)
            dqs = jnp.dot(ds16, kh, preferred_element_type=F32).astype(BF16)
            dk_acc = lax.dot_general(ds16, qs, _DIMS["tn"], preferred_element_type=F32)
            dv_acc = lax.dot_general(p16, dos16, _DIMS["tn"], preferred_element_type=F32)
            dsk = jnp.exp(sink_ref[kv] - lse) * delta
            for g in range(grp):
                hh = kv * grp + g
                dq_ref[:, hh * hd:(hh + 1) * hd] = dqs[g * blk:(g + 1) * blk]
                dsink += jnp.where(lane == hh, -jnp.sum(dsk[g * blk:(g + 1) * blk], axis=0, keepdims=True), 0.0)
            cols = slice(kv * hd, (kv + 1) * hd)
            dk_ref[cur, cols] += dk_acc[blk:]
            dv_ref[cur, cols] += dv_acc[blk:]

            @pl.when(n > 0)
            def _():
                dk_ref[prev, cols] += dk_acc[:blk]
                dv_ref[prev, cols] += dv_acc[:blk]

        ds_ref[...] += dsink

    return pl.pallas_call(
        body, name=name, grid=(S // blk,),
        in_specs=_swa_specs(off_q, off_k, off_v, True) + [pl.BlockSpec((blk, aq), lambda n: (n, 0)),
                                                    pl.BlockSpec((blk, nh), lambda n: (n, 0))],
        out_specs=[pl.BlockSpec((blk, aq), lambda n: (n, 0)), _full((S, akv)), _full((S, akv)), _full((1, 128))],
        out_shape=[jax.ShapeDtypeStruct((S, aq), BF16), jax.ShapeDtypeStruct((S, akv), F32),
                   jax.ShapeDtypeStruct((S, akv), F32), jax.ShapeDtypeStruct((1, 128), F32)],
        compiler_params=_params(("arbitrary",)),
    )(_swa_sinks(sinks), h, h, h, h, h, dout, lse)


def _causal(i, j, t):
    row = i * t + lax.broadcasted_iota(jnp.int32, (t, t), 0)
    col = j * t + lax.broadcasted_iota(jnp.int32, (t, t), 1)
    return col <= row


def _pair_rope(qr, hh):
    lane = lax.broadcasted_iota(jnp.int32, qr.shape, 1)
    return jnp.where(lane < MLA_ROPE, qr if hh == 0 else pltpu.roll(qr, MLA_ROPE, 1), jnp.zeros_like(qr))


def _mla_fwd(qf, qr, kvf, kr, name):
    S = qr.shape[0]
    H, dn, dv = MLA_HEADS, MLA_NOPE, MLA_V
    assert H % 2 == 0 and dn == LANES and dv == LANES and 2 * MLA_ROPE == LANES
    t = _tile(S, (MLA_FWD_TILE, 512, 256, 128))
    nq = S // t
    scale = (MLA_NOPE + MLA_ROPE) ** -0.5

    def body(qn_ref, qr_ref, kn_ref, kr_ref, v_ref, o_ref, o16_ref, lse_ref, *state):
        i = pl.program_id(1)
        for hh in range(2):
            m_s, l_s, acc_s = state[3 * hh:3 * hh + 3]
            m_s[...] = jnp.full_like(m_s, -jnp.inf)
            l_s[...] = jnp.zeros_like(l_s)
            acc_s[...] = jnp.zeros_like(acc_s)

        def block(j, masked):
            rows = pl.ds(pl.multiple_of(j * t, t), t)
            for hh in range(2):
                m_s, l_s, acc_s = state[3 * hh:3 * hh + 3]
                cols = slice(hh * LANES, (hh + 1) * LANES)
                q = jnp.concatenate([qn_ref[:, cols].astype(BF16), _pair_rope(qr_ref[...], hh)], axis=1)
                k = jnp.concatenate([kn_ref[rows, cols], kr_ref[rows, :]], axis=1)
                s = lax.dot_general(q, k, _DIMS["nt"], preferred_element_type=F32) * (scale * LOG2E)
                if masked:
                    s = jnp.where(_causal(0, 0, t), s, MASK)
                m_old = m_s[...]
                m_new = jnp.maximum(m_old, jnp.max(s, axis=1, keepdims=True))
                alpha = jnp.exp2(m_old - m_new)
                p = jnp.exp2(s - m_new)
                l_s[...] = alpha * l_s[...] + jnp.sum(p, axis=1, keepdims=True)
                acc_s[...] = alpha * acc_s[...] + jnp.dot(p.astype(BF16), v_ref[rows, cols], preferred_element_type=F32)
                m_s[...] = m_new

        def full_block(j, carry):
            block(j, False)
            return carry

        lax.fori_loop(0, i, full_block, 0)
        block(i, True)
        for hh in range(2):
            m_s, l_s, acc_s = state[3 * hh:3 * hh + 3]
            out = acc_s[...] / l_s[...]
            o_ref[:, hh * LANES:(hh + 1) * LANES] = out
            o16_ref[:, hh * LANES:(hh + 1) * LANES] = out.astype(BF16)
            lse_ref[hh] = (m_s[...] + jnp.log2(l_s[...])) * (1.0 / LOG2E)

    P = H // 2
    return pl.pallas_call(
        body, name=name, grid=(P, nq),
        in_specs=[pl.BlockSpec((t, 2 * LANES), lambda p, i: (i, p)), pl.BlockSpec((t, LANES), lambda p, i: (i, p)),
                  pl.BlockSpec((S, 2 * LANES), lambda p, i: (0, p)), pl.BlockSpec((S, LANES), lambda p, i: (0, 0)),
                  pl.BlockSpec((S, 2 * LANES), lambda p, i: (0, P + p))],
        out_specs=[pl.BlockSpec((t, 2 * LANES), lambda p, i: (i, p)), pl.BlockSpec((t, 2 * LANES), lambda p, i: (i, p)),
                   pl.BlockSpec((2, t, 1), lambda p, i: (p, i, 0))],
        out_shape=[jax.ShapeDtypeStruct((S, H * dv), F32), jax.ShapeDtypeStruct((S, H * dv), BF16),
                   jax.ShapeDtypeStruct((H, S, 1), F32)],
        scratch_shapes=[pltpu.VMEM((t, 1), F32), pltpu.VMEM((t, 1), F32), pltpu.VMEM((t, dv), F32)] * 2,
        compiler_params=_params(("parallel", "arbitrary")),
    )(qf, qr, kvf, kr, kvf)


def _mla_bwd(qf, qr, kvf, kr, do, out, lse, name):
    S = qr.shape[0]
    H, dv = MLA_HEADS, MLA_V
    P = H // 2
    t = _tile(S, (MLA_TILE, 256, 128))
    nq = S // t
    scale = (MLA_NOPE + MLA_ROPE) ** -0.5

    pairs = [(j, i) for j in range(nq) for i in range(j, nq)]
    jt = jnp.asarray([p[0] for p in pairs], jnp.int32)
    it = jnp.asarray([p[1] for p in pairs], jnp.int32)

    def body(jt_ref, it_ref, qn_ref, qr_ref, kn_ref, kr_ref, v_ref, do_ref, o_ref, lse_ref, dqn_ref, dqr_ref, dkn_ref, dv_ref,
             dkr_ref, dkn_s, dv_s, dkr_s, dqn_s, dqr_s):
        k = pl.program_id(1)
        j, i = jt_ref[k], it_ref[k]

        @pl.when(k == 0)
        def _():
            dqn_s[...] = jnp.zeros_like(dqn_s)
            dqr_s[...] = jnp.zeros_like(dqr_s)

        @pl.when(i == j)
        def _():
            dkn_s[...] = jnp.zeros_like(dkn_s)
            dv_s[...] = jnp.zeros_like(dv_s)
            dkr_s[...] = jnp.zeros_like(dkr_s)

        def block(masked):
            rows = pl.ds(pl.multiple_of(i * t, t), t)
            krv = kr_ref[...]
            for hh in range(2):
                cols = slice(hh * LANES, (hh + 1) * LANES)
                q = jnp.concatenate([qn_ref[:, cols].astype(BF16), _pair_rope(qr_ref[...], hh)], axis=1)
                k = jnp.concatenate([kn_ref[:, cols], krv], axis=1)
                vv, dof = v_ref[:, cols], do_ref[:, cols]
                dov = dof.astype(BF16)
                delta = jnp.sum(dof * o_ref[:, cols], axis=1, keepdims=True)
                s = lax.dot_general(q, k, _DIMS["nt"], preferred_element_type=F32) * (scale * LOG2E)
                if masked:
                    s = jnp.where(_causal(0, 0, t), s, MASK)
                p = jnp.exp2(s - lse_ref[hh] * LOG2E)
                p16 = p.astype(BF16)
                dp = lax.dot_general(dov, vv, _DIMS["nt"], preferred_element_type=F32)
                ds16 = (p * (dp - delta) * scale).astype(BF16)
                dv_s[:, cols] += lax.dot_general(p16, dov, _DIMS["tn"], preferred_element_type=F32)
                dk = lax.dot_general(ds16, q, _DIMS["tn"], preferred_element_type=F32)
                dkn_s[:, cols] += dk[:, :LANES]
                dkr_s[...] += dk[:, LANES:]
                dq = jnp.dot(ds16, k, preferred_element_type=F32)
                dqn_s[rows, cols] += dq[:, :LANES]
                dqr = dq[:, LANES:]
                dqr_s[rows, :] += dqr if hh == 0 else pltpu.roll(dqr, MLA_ROPE, 1)

        @pl.when(i == j)
        def _():
            block(True)

        @pl.when(i > j)
        def _():
            block(False)

        @pl.when(i == nq - 1)
        def _():
            dkn_ref[...] = dkn_s[...].astype(BF16)
            dv_ref[...] = dv_s[...].astype(BF16)
            dkr_ref[...] = dkr_s[...]

        @pl.when(k == len(pairs) - 1)
        def _():
            dqn_ref[...] = dqn_s[...].astype(BF16)
            dqr_ref[...] = dqr_s[...]

    q_blk = lambda w: pl.BlockSpec((t, w), lambda p, k, jt, it: (it[k], p))
    kv_blk = lambda off: pl.BlockSpec((t, 2 * LANES), lambda p, k, jt, it: (jt[k], off + p))
    return pl.pallas_call(
        body, name=name,
        grid_spec=pltpu.PrefetchScalarGridSpec(
            num_scalar_prefetch=2, grid=(P, len(pairs)),
            in_specs=[q_blk(2 * LANES), q_blk(LANES), kv_blk(0), pl.BlockSpec((t, LANES), lambda p, k, jt, it: (jt[k], 0)),
                      kv_blk(P), q_blk(2 * LANES), q_blk(2 * LANES),
                      pl.BlockSpec((2, t, 1), lambda p, k, jt, it: (p, it[k], 0))],
            out_specs=[pl.BlockSpec((S, 2 * LANES), lambda p, k, jt, it: (0, p)), pl.BlockSpec((S, LANES), lambda p, k, jt, it: (0, p)),
                       kv_blk(0), kv_blk(0), pl.BlockSpec((None, t, LANES), lambda p, k, jt, it: (p, jt[k], 0))],
            scratch_shapes=[pltpu.VMEM((t, 2 * LANES), F32), pltpu.VMEM((t, 2 * LANES), F32), pltpu.VMEM((t, LANES), F32),
                            pltpu.VMEM((S, 2 * LANES), F32), pltpu.VMEM((S, LANES), F32)]),
        out_shape=[jax.ShapeDtypeStruct((S, H * MLA_NOPE), BF16), jax.ShapeDtypeStruct((S, H * MLA_ROPE), F32),
                   jax.ShapeDtypeStruct((S, H * MLA_NOPE), BF16), jax.ShapeDtypeStruct((S, H * dv), BF16),
                   jax.ShapeDtypeStruct((P, S, LANES), F32)],
        compiler_params=_params(("parallel", "arbitrary")),
    )(jt, it, qf, qr, kvf, kr, kvf, do, out, lse)


def _sgu_norm(hv, lg, lb):
    vg = _gelu(hv)
    mu = jnp.mean(vg, axis=-1, keepdims=True)
    d = vg - mu
    rstd = lax.rsqrt(jnp.mean(d * d, axis=-1, keepdims=True) + EPS)
    xhat = d * rstd
    return xhat, rstd, xhat * lg + lb


def _sgu_fwd(h, off_u, off_v, lg, lb, w16, bt, name):
    S = h.shape[0]
    T, G, C = SGU_CHUNK, SGU_GROUPS, SGU_DIM
    W = G * C

    def body(hu_ref, hv_ref, lg_ref, lb_ref, w_ref, bt_ref, y_ref):
        u = _gelu(hu_ref[...])
        _, _, vn = _sgu_norm(hv_ref[...], lg_ref[...], lb_ref[...])
        vn16 = vn.astype(BF16)
        for g in range(G):
            cols = slice(g * C, (g + 1) * C)
            mixed = jnp.dot(w_ref[g], vn16[:, cols], preferred_element_type=F32) + bt_ref[:, g:g + 1]
            y_ref[:, cols] = (u[:, cols] * mixed).astype(BF16)

    return pl.pallas_call(
        body, name=name, grid=(S // T,),
        in_specs=[_cols(T, W, off_u), _cols(T, W, off_v), _full((1, W)), _full((1, W)), _full((G, T, T)), _full((T, G))],
        out_specs=pl.BlockSpec((T, W), lambda n: (n, 0)),
        out_shape=jax.ShapeDtypeStruct((S, W), BF16),
        compiler_params=_params(("parallel",)),
    )(h, h, lg, lb, w16, bt)


def _sgu_bwd(h, off_u, off_v, lg, lb, w16, bt, dy, name):
    S = h.shape[0]
    T, G, C = SGU_CHUNK, SGU_GROUPS, SGU_DIM
    W = G * C
    nc = S // T

    def body(hu_ref, hv_ref, lg_ref, lb_ref, w_ref, bt_ref, dy_ref, dhu_ref, dhv_ref, dw_ref, db_ref, dlg_ref, dlb_ref,
             dmix_s, dvn_s):
        n = pl.program_id(0)

        @pl.when(n == 0)
        def _():
            dw_ref[...] = jnp.zeros_like(dw_ref)
            dlg_ref[...] = jnp.zeros_like(dlg_ref)
            dlb_ref[...] = jnp.zeros_like(dlb_ref)
            dmix_s[...] = jnp.zeros_like(dmix_s)

        hu, hv, lgv = hu_ref[...], hv_ref[...], lg_ref[...]
        u = _gelu(hu)
        xhat, rstd, vn = _sgu_norm(hv, lgv, lb_ref[...])
        vn16 = vn.astype(BF16)
        dyv = dy_ref[...]
        dmixed = dyv * u
        dmix_s[...] += dmixed
        dmixed16 = dmixed.astype(BF16)
        for g in range(G):
            cols = slice(g * C, (g + 1) * C)
            mixed = jnp.dot(w_ref[g], vn16[:, cols], preferred_element_type=F32) + bt_ref[:, g:g + 1]
            dhu_ref[:, cols] = (dyv[:, cols] * mixed * _gelu_grad(hu[:, cols])).astype(BF16)
            dvn_s[:, cols] = lax.dot_general(w_ref[g], dmixed16[:, cols], _DIMS["tn"], preferred_element_type=F32)
            dw_ref[g] += lax.dot_general(dmixed16[:, cols], vn16[:, cols], _DIMS["nt"], preferred_element_type=F32)
        dvn = dvn_s[...]
        dlg_ref[...] += jnp.sum(dvn * xhat, axis=0, keepdims=True)
        dlb_ref[...] += jnp.sum(dvn, axis=0, keepdims=True)
        dxh = dvn * lgv
        m1 = jnp.mean(dxh, axis=-1, keepdims=True)
        m2 = jnp.mean(dxh * xhat, axis=-1, keepdims=True)
        dvg = rstd * (dxh - m1 - xhat * m2)
        dhv_ref[...] = (dvg * _gelu_grad(hv)).astype(BF16)

        @pl.when(n == nc - 1)
        def _():
            tril = lax.broadcasted_iota(jnp.int32, (T, T), 1) <= lax.broadcasted_iota(jnp.int32, (T, T), 0)
            lane = lax.broadcasted_iota(jnp.int32, (T, 128), 1)
            db = jnp.zeros((T, 128), F32)
            for g in range(G):
                dw_ref[g] = jnp.where(tril, dw_ref[g], 0.0)
                db += jnp.where(lane == g, jnp.sum(dmix_s[:, g * C:(g + 1) * C], axis=1, keepdims=True), 0.0)
            db_ref[...] = db

    row = pl.BlockSpec((T, W), lambda n: (n, 0))
    return pl.pallas_call(
        body, name=name, grid=(nc,),
        in_specs=[_cols(T, W, off_u), _cols(T, W, off_v), _full((1, W)), _full((1, W)), _full((G, T, T)), _full((T, G)), row],
        out_specs=[row, row, _full((G, T, T)), _full((T, 128)), _full((1, W)), _full((1, W))],
        out_shape=[jax.ShapeDtypeStruct((S, W), BF16), jax.ShapeDtypeStruct((S, W), BF16),
                   jax.ShapeDtypeStruct((G, T, T), F32), jax.ShapeDtypeStruct((T, 128), F32),
                   jax.ShapeDtypeStruct((1, W), F32), jax.ShapeDtypeStruct((1, W), F32)],
        scratch_shapes=[pltpu.VMEM((T, W), F32), pltpu.VMEM((T, W), F32)],
        compiler_params=_params(("arbitrary",)),
    )(h, h, lg, lb, w16, bt, dy)


def _merge_fwd(ys, ps, h, bg, name):
    S = h.shape[0]
    D = ps[0].shape[1]
    tm = _tile(S, (1024, 512, 256, 128))
    tn = _tile(D, (256, 128))
    nb = len(ys)

    def body(*refs):
        y_refs, p_refs, l_refs = refs[:nb], refs[nb:2 * nb], refs[2 * nb:3 * nb]
        bg_ref, mg_ref, z_ref = refs[3 * nb:]
        acc = jnp.zeros((tm, tn), F32)
        for b in range(nb):
            z = jnp.dot(y_refs[b][...].astype(BF16), p_refs[b][...], preferred_element_type=F32)
            z_ref[b] = z
            acc += _sigmoid(l_refs[b][...] + bg_ref[b:b + 1, :]) * z
        mg_ref[...] = acc.astype(BF16)

    in_specs = [pl.BlockSpec((tm, y.shape[1]), lambda i, j: (i, 0)) for y in ys]
    in_specs += [pl.BlockSpec((p.shape[0], tn), lambda i, j: (0, j)) for p in ps]
    in_specs += [pl.BlockSpec((tm, tn), functools.partial(lambda i, j, b: (i, b * (D // tn) + j), b=b)) for b in range(nb)]
    in_specs += [pl.BlockSpec((nb, tn), lambda i, j: (0, j))]
    return pl.pallas_call(
        body, name=name, grid=(S // tm, D // tn), in_specs=in_specs,
        out_specs=[pl.BlockSpec((tm, tn), lambda i, j: (i, j)), pl.BlockSpec((nb, tm, tn), lambda i, j: (0, i, j))],
        out_shape=[jax.ShapeDtypeStruct((S, D), BF16), jax.ShapeDtypeStruct((nb, S, D), F32)],
        compiler_params=_params(("parallel", "parallel")),
    )(*ys, *ps, *([h] * nb), bg)


def _merge_bwd(dm, z, h, bg, name):
    nb, S, D = z.shape
    tm = _tile(S, (ROW_TILE, 256, 128))
    tn = _tile(D, (512, 256, 128))

    def body(*refs):
        dm_ref, z_ref = refs[0], refs[1]
        l_refs = refs[2:2 + nb]
        bg_ref = refs[2 + nb]
        dz_ref, dl_ref, dbg_ref = refs[3 + nb:3 + 2 * nb], refs[3 + 2 * nb:3 + 3 * nb], refs[3 + 3 * nb]

        @pl.when(pl.program_id(1) == 0)
        def _():
            dbg_ref[...] = jnp.zeros_like(dbg_ref)

        dmv = dm_ref[...]
        rows = lax.broadcasted_iota(jnp.int32, (SUBLANES, tn), 0)
        dbg = jnp.zeros((SUBLANES, tn), F32)
        for b in range(nb):
            gt = _sigmoid(l_refs[b][...] + bg_ref[b:b + 1, :])
            dz_ref[b][...] = (dmv * gt).astype(BF16)
            dl = dmv * z_ref[b] * gt * (1.0 - gt)
            dl_ref[b][...] = dl.astype(BF16)
            dbg += jnp.where(rows == b, jnp.sum(dl, axis=0, keepdims=True), 0.0)
        dbg_ref[...] += dbg

    in_specs = [pl.BlockSpec((tm, tn), lambda j, i: (i, j)), pl.BlockSpec((nb, tm, tn), lambda j, i: (0, i, j))]
    in_specs += [pl.BlockSpec((tm, tn), functools.partial(lambda j, i, b: (i, b * (D // tn) + j), b=b)) for b in range(nb)]
    in_specs += [pl.BlockSpec((nb, tn), lambda j, i: (0, j))]
    tile = pl.BlockSpec((tm, tn), lambda j, i: (i, j))
    res = pl.pallas_call(
        body, name=name, grid=(D // tn, S // tm), in_specs=in_specs,
        out_specs=[tile] * (2 * nb) + [pl.BlockSpec((SUBLANES, tn), lambda j, i: (0, j))],
        out_shape=[jax.ShapeDtypeStruct((S, D), BF16)] * (2 * nb) + [jax.ShapeDtypeStruct((SUBLANES, D), F32)],
        compiler_params=_params(("parallel", "arbitrary")),
    )(dm, z, *([h] * nb), bg)
    return res[:nb], res[nb:2 * nb], res[2 * nb]


def _shift_down(x, halo, k):
    xr = pltpu.roll(x, k, 0)
    hr = pltpu.roll(halo, k, 0)
    rows = lax.broadcasted_iota(jnp.int32, halo.shape, 0)
    top = jnp.where(rows < k, hr, xr[:SUBLANES])
    return jnp.concatenate([top, xr[SUBLANES:]], axis=0)


def _shift_up(x, halo, k):
    tm = x.shape[0]
    xr = pltpu.roll(x, tm - k, 0)
    hr = pltpu.roll(halo, SUBLANES - k, 0)
    rows = lax.broadcasted_iota(jnp.int32, halo.shape, 0)
    bot = jnp.where(rows >= SUBLANES - k, hr, xr[tm - SUBLANES:])
    return jnp.concatenate([xr[:tm - SUBLANES], bot], axis=0)


def _conv_tiles(S, F):
    return _tile(S, (GLU_ROWS, 512, 256, 128)), _tile(F, (512, 256, 128))


def _conv_in_specs(tm, tn, F):
    r8 = tm // SUBLANES
    nf = F // tn
    specs = []
    for half in range(2):
        specs.append(pl.BlockSpec((tm, tn), functools.partial(lambda j, i, o: (i, o + j), o=half * nf)))
        specs.append(pl.BlockSpec((SUBLANES, tn), functools.partial(lambda j, i, o: (jnp.maximum(i * r8 - 1, 0), o + j), o=half * nf)))
    for half in range(2):
        specs.append(pl.BlockSpec((3, tn), functools.partial(lambda j, i, o: (0, o + j), o=half * nf)))
        specs.append(pl.BlockSpec((1, tn), functools.partial(lambda j, i, o: (0, o + j), o=half * nf)))
    return specs


def _conv_apply(x, halo, w, b, first):
    halo = jnp.where(first, 0.0, halo)
    x1 = _shift_down(x, halo, 1)
    x2 = _shift_down(x, halo, 2)
    return b + x2 * w[0:1, :] + x1 * w[1:2, :] + x * w[2:3, :], x1, x2


def _glu_fwd(up, cw, cb, name):
    S, F2 = up.shape
    F = F2 // 2
    tm, tn = _conv_tiles(S, F)

    def body(ug, hg, uv, hv, wg, bgr, wv, bvr, a_ref):
        first = pl.program_id(1) == 0
        cg, _, _ = _conv_apply(ug[...], hg[...], wg[...], bgr[...], first)
        cv, _, _ = _conv_apply(uv[...], hv[...], wv[...], bvr[...], first)
        a_ref[...] = (cg * _sigmoid(cg) * cv).astype(BF16)

    return pl.pallas_call(
        body, name=name, grid=(F // tn, S // tm), in_specs=_conv_in_specs(tm, tn, F),
        out_specs=pl.BlockSpec((tm, tn), lambda j, i: (i, j)),
        out_shape=jax.ShapeDtypeStruct((S, F), BF16),
        compiler_params=_params(("parallel", "parallel")),
    )(up, up, up, up, cw, cb, cw, cb)


def _glu_bwd(up, cw, cb, da, name):
    S, F2 = up.shape
    F = F2 // 2
    tm, tn = _conv_tiles(S, F)

    def body(ug, hg, uv, hv, wg, bgr, wv, bvr, da_ref, dg_ref, dv_ref, sg_ref, sv_ref):
        i = pl.program_id(1)

        @pl.when(i == 0)
        def _():
            sg_ref[...] = jnp.zeros_like(sg_ref)
            sv_ref[...] = jnp.zeros_like(sv_ref)

        first = i == 0
        xg, xv = ug[...], uv[...]
        cg, xg1, xg2 = _conv_apply(xg, hg[...], wg[...], bgr[...], first)
        cv, xv1, xv2 = _conv_apply(xv, hv[...], wv[...], bvr[...], first)
        dav = da_ref[...]
        sg = _sigmoid(cg)
        dcv = dav * cg * sg
        dcg = dav * cv * sg * (1.0 + cg * (1.0 - sg))
        dg_ref[...] = dcg
        dv_ref[...] = dcv
        rows = lax.broadcasted_iota(jnp.int32, (SUBLANES, tn), 0)

        def stats(dc, x, x1, x2):
            acc = jnp.zeros((SUBLANES, tn), F32)
            for r, val in enumerate((dc * x2, dc * x1, dc * x, dc)):
                acc += jnp.where(rows == r, jnp.sum(val, axis=0, keepdims=True), 0.0)
            return acc

        sg_ref[...] += stats(dcg, xg, xg1, xg2)
        sv_ref[...] += stats(dcv, xv, xv1, xv2)

    tile = pl.BlockSpec((tm, tn), lambda j, i: (i, j))
    stat = pl.BlockSpec((SUBLANES, tn), lambda j, i: (0, j))
    return pl.pallas_call(
        body, name=name, grid=(F // tn, S // tm), in_specs=_conv_in_specs(tm, tn, F) + [tile],
        out_specs=[tile, tile, stat, stat],
        out_shape=[jax.ShapeDtypeStruct((S, F), F32), jax.ShapeDtypeStruct((S, F), F32),
                   jax.ShapeDtypeStruct((SUBLANES, F), F32), jax.ShapeDtypeStruct((SUBLANES, F), F32)],
        compiler_params=_params(("parallel", "arbitrary")),
    )(up, up, up, up, cw, cb, cw, cb, da)


def _conv_bwd(dcg, dcv, w, name):
    S, F = dcg.shape
    tm, tn = _tile(S, (CONV_BWD_ROWS, 1024, 512, 256, 128)), _conv_tiles(S, F)[1]
    r8 = tm // SUBLANES
    ni = S // tm
    nf = F // tn

    def body(g_ref, gh_ref, v_ref, vh_ref, w_ref, o_ref):
        gate = pl.program_id(0) == 0
        x = jnp.where(gate, g_ref[...], v_ref[...])
        halo = jnp.where(gate, gh_ref[...], vh_ref[...])
        halo = jnp.where(pl.program_id(2) == ni - 1, 0.0, halo)
        wv = w_ref[...]
        o_ref[...] = (x * wv[2:3, :] + _shift_up(x, halo, 1) * wv[1:2, :] + _shift_up(x, halo, 2) * wv[0:1, :]).astype(BF16)

    def tile(half):
        return pl.BlockSpec((tm, tn), lambda h, j, i: (jnp.where(h == half, i, 0), jnp.where(h == half, j, 0)))

    def below(half):
        return pl.BlockSpec((SUBLANES, tn), lambda h, j, i: (
            jnp.where(h == half, jnp.minimum((i + 1) * r8, S // SUBLANES - 1), 0), jnp.where(h == half, j, 0)))

    return pl.pallas_call(
        body, name=name, grid=(2, nf, ni),
        in_specs=[tile(0), below(0), tile(1), below(1), pl.BlockSpec((3, tn), lambda h, j, i: (0, h * nf + j))],
        out_specs=pl.BlockSpec((tm, tn), lambda h, j, i: (i, h * nf + j)),
        out_shape=jax.ShapeDtypeStruct((S, 2 * F), BF16),
        compiler_params=_params(("parallel", "parallel", "parallel")),
    )(dcg, dcg, dcv, dcv, w)


def _adamw(slot_list, own_list, me, w, m, v, name, dep=None):
    L = len(slot_list)
    P, K, C = slot_list[0].shape
    tr = _tile(K, (256, 128, 64, 32, 16))
    while tr * C * 4 > (1 << 20) and tr % 32 == 0:
        tr //= 2
    nb = K // tr
    has_own = own_list is not None

    def body(me_ref, *refs):
        s_refs = refs[:L]
        o_refs = refs[L:2 * L] if has_own else None
        w_ref, m_ref, v_ref = refs[L * (1 + has_own):L * (1 + has_own) + 3]
        g_ref, d_ref, nm_ref, nv_ref = refs[-4:]
        layer = pl.program_id(0)
        g = None
        for l in range(L):
            gl = None
            for p in range(P):
                term = s_refs[l][p].astype(F32)
                if has_own:
                    term = jnp.where(me_ref[0] == p, o_refs[l][0].astype(F32), term)
                gl = term if gl is None else gl + term
            g = gl if g is None else jnp.where(layer == l, gl, g)
        nm = ADAM_B1 * m_ref[...] + (1.0 - ADAM_B1) * g
        nv = ADAM_B2 * v_ref[...] + (1.0 - ADAM_B2) * (g * g)
        m_hat = nm / (1.0 - ADAM_B1 ** ADAM_STEP)
        v_hat = nv / (1.0 - ADAM_B2 ** ADAM_STEP)
        g_ref[...] = g
        d_ref[...] = -ADAM_LR * (m_hat / (jnp.sqrt(v_hat) + ADAM_EPS) + ADAM_WD * w_ref[...])
        nm_ref[...] = nm
        nv_ref[...] = nv

    blk = pl.BlockSpec((None, tr, C), lambda li, i, me_ref: (li, i, 0))
    specs = [pl.BlockSpec((P, tr, C), functools.partial(lambda li, i, me_ref, l: (0, jnp.where(li == l, i, 0), 0), l=l))
             for l in range(L)]
    if has_own:
        specs += [pl.BlockSpec((1, tr, C), functools.partial(lambda li, i, me_ref, l: (me_ref[0], jnp.where(li == l, i, 0), 0), l=l))
                  for l in range(L)]
    return pl.pallas_call(
        body, name=name,
        grid_spec=pltpu.PrefetchScalarGridSpec(
            num_scalar_prefetch=1, grid=(L, nb), in_specs=specs + [blk, blk, blk] + [_ANY] * (dep is not None),
            out_specs=[blk] * 4),
        out_shape=[jax.ShapeDtypeStruct((L, K, C), F32)] * 4,
        compiler_params=_params(("arbitrary", "arbitrary")),
    )(me, *slot_list, *(own_list if has_own else []), w, m, v, *([] if dep is None else [dep]))


_HBM = pl.BlockSpec(memory_space=pltpu.HBM)
_SEM = pl.BlockSpec(memory_space=pltpu.SEMAPHORE)
_ANY = pl.BlockSpec(memory_space=pl.ANY)


def _peers():
    x, y, c = lax.axis_index("x"), lax.axis_index("y"), lax.axis_index("c")

    def flip(v, bit):
        return 1 - v if bit else v

    def peer(k):
        return (flip(x, (k >> 2) & 1), flip(y, (k >> 1) & 1), flip(c, k & 1))

    def peer_index(k):
        px, py, pc = peer(k)
        return 4 * px + 2 * py + pc

    return 4 * x + 2 * y + c, peer, peer_index


def _split_copy(src_refs, land_refs, send_sems, recv_sems, scatter, a, k, outgoing):
    me, peer, peer_index = _peers()
    if outgoing:
        src = src_refs[a].at[peer_index(k)] if scatter else src_refs[a]
        dst = land_refs[a].at[me]
    else:
        src = src_refs[a].at[me] if scatter else src_refs[a]
        dst = land_refs[a].at[peer_index(k)]
    pair = a * (N_DEV - 1) + k - 1
    return pltpu.make_async_remote_copy(src_ref=src, dst_ref=dst, send_sem=send_sems.at[pair],
                                        recv_sem=recv_sems.at[pair], device_id=peer(k),
                                        device_id_type=pl.DeviceIdType.MESH)


def _gather_two_level(srcs, name):
    na = len(srcs)

    def body(*refs):
        src_refs, out_refs = refs[:na], refs[na:2 * na]
        send_sems, recv_sems = refs[2 * na:]
        x, y, c = lax.axis_index("x"), lax.axis_index("y"), lax.axis_index("c")
        me, sibling = (x, y, c), (x, y, 1 - c)
        chips = [(1 - x, y), (x, 1 - y), (1 - x, 1 - y)]

        def copy(a, k, block, to, src=None):
            px, py, pc = block
            slot = out_refs[a].at[4 * px + 2 * py + pc]
            return pltpu.make_async_remote_copy(
                src_ref=slot if src is None else src, dst_ref=slot, send_sem=send_sems.at[a * (N_DEV - 1) + k],
                recv_sem=recv_sems.at[a * (N_DEV - 1) + k], device_id=to, device_id_type=pl.DeviceIdType.MESH)

        first = [copy(a, 0, me, sibling, src_refs[a]) for a in range(na)]
        first += [copy(a, 1 + j, me, (*chip, c), src_refs[a]) for j, chip in enumerate(chips) for a in range(na)]
        for cp in first:
            cp.start()
        passed = []
        for j, chip in enumerate(chips):
            for a in range(na):
                copy(a, 1 + j, (*chip, c), me).wait_recv()
                passed.append(copy(a, 4 + j, (*chip, c), sibling))
                passed[-1].start()
        for a in range(na):
            copy(a, 0, sibling, me).wait_recv()
        for j, chip in enumerate(chips):
            for a in range(na):
                copy(a, 4 + j, (*chip, 1 - c), me).wait_recv()
        for cp in first + passed:
            cp.wait_send()

    return pl.pallas_call(
        body, name=name, in_specs=[_ANY] * na, out_specs=[_ANY] * na,
        out_shape=[jax.ShapeDtypeStruct((N_DEV,) + s.shape, s.dtype) for s in srcs],
        scratch_shapes=[pltpu.SemaphoreType.DMA((na * (N_DEV - 1),)), pltpu.SemaphoreType.DMA((na * (N_DEV - 1),))],
    )(*srcs)


def _exchange_start(srcs, scatter, after, name):
    na = len(srcs)
    land_shapes = [s.shape if scatter else (N_DEV,) + s.shape for s in srcs]
    has_after = after is not None

    def body(*refs):
        src_refs, land_refs = refs[:na], refs[na:2 * na]
        send_sems, recv_sems = refs[2 * na + has_after], refs[2 * na + has_after + 1]
        token = refs[-1]
        for k in range(1, N_DEV):
            for a in range(na):
                _split_copy(src_refs, land_refs, send_sems, recv_sems, scatter, a, k, True).start()
        token[...] = jnp.zeros_like(token)

    sems = pltpu.SemaphoreType.DMA((na * (N_DEV - 1),))
    out_shape = ([sems, sems] + [pltpu.HBM(s.shape, s.dtype) for s in srcs]
                 + [pltpu.HBM(shp, s.dtype) for shp, s in zip(land_shapes, srcs)] + [jax.ShapeDtypeStruct((SUBLANES, 128), F32)])
    args = [pltpu.with_memory_space_constraint(s, pltpu.HBM) for s in srcs]
    args += [pltpu.with_memory_space_constraint(lax.empty(shp, s.dtype), pltpu.HBM) for shp, s in zip(land_shapes, srcs)]
    if has_after:
        args.append(after)
    res = pl.pallas_call(
        body, name=name, in_specs=[_HBM] * (2 * na) + [_ANY] * has_after,
        out_specs=[_SEM, _SEM] + [_HBM] * (2 * na) + [pl.BlockSpec(memory_space=pltpu.VMEM)], out_shape=out_shape,
        input_output_aliases={i: 2 + i for i in range(2 * na)},
        compiler_params=pltpu.CompilerParams(has_side_effects=pltpu.SideEffectType.DATAFLOW_SIDE_EFFECTING),
    )(*args)
    handle = dict(send=res[0], recv=res[1], srcs=list(res[2:2 + na]), lands=list(res[2 + na:2 + 2 * na]), scatter=scatter)
    return handle, res[-1]


def _exchange_wait(handle, after, name):
    srcs, lands, scatter = handle["srcs"], handle["lands"], handle["scatter"]
    na = len(srcs)

    def body(*refs):
        src_refs, land_refs = refs[:na], refs[na:2 * na]
        send_sems, recv_sems = refs[2 * na], refs[2 * na + 1]
        for k in range(1, N_DEV):
            for a in range(na):
                _split_copy(src_refs, land_refs, send_sems, recv_sems, scatter, a, k, True).wait_send()
                _split_copy(src_refs, land_refs, send_sems, recv_sems, scatter, a, k, False).wait_recv()

    res = pl.pallas_call(
        body, name=name, in_specs=[_HBM] * (2 * na) + [_SEM, _SEM, _ANY], out_specs=[_HBM] * (2 * na),
        out_shape=[pltpu.HBM(s.shape, s.dtype) for s in srcs] + [pltpu.HBM(s.shape, s.dtype) for s in lands],
        input_output_aliases={i: i for i in range(2 * na)},
        compiler_params=pltpu.CompilerParams(has_side_effects=pltpu.SideEffectType.DATAFLOW_SIDE_EFFECTING),
    )(*srcs, *lands, handle["send"], handle["recv"], after)
    return list(res[:na]), list(res[na:])


def _layout(D):
    aq, akv = SWA_Q_HEADS * SWA_HEAD_DIM, SWA_KV_HEADS * SWA_HEAD_DIM
    w = SGU_GROUPS * SGU_DIM
    return aq, akv, w


class _Seg:
    def __init__(self, D, rq, rkv):
        aq, akv, w = _layout(D)
        src = {}
        o = 0
        for nm, wd in (("qa", aq), ("ka", akv), ("va", akv), ("cq", rq), ("ckv", rkv), ("kr", MLA_ROPE), ("hu", w), ("hv", w),
                       ("g", 3 * D)):
            src[nm] = (o, wd)
            o += wd
        self.n_in = o
        self.order = ("g", "qa", "hu", "hv", "cq", "ckv", "ka", "va", "kr")
        self.src = src
        self.off = {}
        o = 0
        for nm in self.order:
            self.off[nm] = o
            o += src[nm][1]
        self.width = {nm: src[nm][1] for nm in self.order}
        self.n_pad = -(-o // 1536) * 1536 if o > 1536 else -(-o // 512) * 512
        self.used = o

    def from_shards(self, shards):
        c = shards.shape[2]
        parts = []
        for nm in self.order:
            s0, wd = self.src[nm]
            for d in range(N_DEV):
                lo, hi = max(s0, c * d), min(s0 + wd, c * (d + 1))
                if lo < hi:
                    parts.append(shards[d][:, lo - c * d:hi - c * d])
        parts.append(jnp.zeros((shards.shape[1], self.n_pad - self.used), shards.dtype))
        return jnp.concatenate(parts, axis=1)

    def to_shards(self, w):
        c = self.n_in // N_DEV
        names = sorted(self.order, key=lambda nm: self.src[nm][0])
        shards = []
        for d in range(N_DEV):
            parts = []
            for nm in names:
                s0, wd = self.src[nm]
                lo, hi = max(s0, c * d), min(s0 + wd, c * (d + 1))
                if lo < hi:
                    parts.append(w[:, self.off[nm] + lo - s0:self.off[nm] + hi - s0])
            shards.append(jnp.concatenate(parts, axis=1))
        return jnp.stack(shards)


def _uq_permute(w):
    R = w.shape[0]
    H = MLA_HEADS
    w3 = w.reshape(R, H, MLA_NOPE + MLA_ROPE)
    return jnp.concatenate([w3[:, :, :MLA_NOPE].reshape(R, H * MLA_NOPE), w3[:, :, MLA_NOPE:].reshape(R, H * MLA_ROPE)], axis=1)


def _uq_unpermute(w):
    R = w.shape[0]
    H = MLA_HEADS
    n = w[:, :H * MLA_NOPE].reshape(R, H, MLA_NOPE)
    r = w[:, H * MLA_NOPE:].reshape(R, H, MLA_ROPE)
    return jnp.concatenate([n, r], axis=2).reshape(R, H * (MLA_NOPE + MLA_ROPE))


def _ukv_permute(w):
    R = w.shape[0]
    w3 = w.reshape(R, MLA_HEADS, MLA_NOPE + MLA_V)
    return jnp.concatenate([w3[:, :, :MLA_NOPE].reshape(R, -1), w3[:, :, MLA_NOPE:].reshape(R, -1)], axis=1)


def _ukv_unpermute(w):
    R = w.shape[0]
    H = MLA_HEADS
    k = w[:, :H * MLA_NOPE].reshape(R, H, MLA_NOPE)
    v = w[:, H * MLA_NOPE:].reshape(R, H, MLA_V)
    return jnp.concatenate([k, v], axis=2).reshape(R, H * (MLA_NOPE + MLA_V))


GROUPS = {"a": ("w_in",), "b": ("w_uq", "w_ukv", "w_proj_a", "w_proj_b", "w_proj_c", "w_o", "b_gate"),
          "c": ("w_up", "w_down", "conv_w")}


def _layer_fwd(l, x, x16, fetch, P, cs, sn, seg, alpha):
    S, D = x.shape
    H, half = MLA_HEADS, MLA_ROPE // 2
    off = seg.off
    nm = lambda s: f"l{l}_{s}"
    sv = {"x16": x16}
    W = {"w_in": seg.from_shards(fetch(l, "a", x16)["w_in"])}
    h = _mm(x16, W["w_in"], "nn", [(F32, "n")], nm("h"))[0]
    sv["h"] = h
    ya, lse_a = _swa_fwd(h, off["qa"], off["ka"], off["va"], P["sinks"], nm("swa_fwd"))
    cqn, rq = _rms_fwd(h, off["cq"], seg.width["cq"], P["q_norm_g"], nm("rmsq_fwd"))
    ckvn, rkv = _rms_fwd(h, off["ckv"], seg.width["ckv"], P["kv_norm_g"], nm("rmskv_fwd"))
    W.update(fetch(l, "b", cqn))
    W["w_uq"] = _uq_permute(W["w_uq"])
    W["w_ukv"] = _ukv_permute(W["w_ukv"])
    qf = _mm(cqn, W["w_uq"], "nn", [(F32, "n")], nm("uq"))[0]
    kvf = _mm(ckvn, W["w_ukv"], "nn", [(BF16, "n")], nm("ukv"))[0]
    qr = _rope(qf, H * MLA_NOPE, H * MLA_ROPE, cs, sn, False, nm("ropeq_fwd"))
    kr = _rope(h, off["kr"], LANES, cs, sn, False, nm("ropek_fwd"))
    yb, yb16, lse_b = _mla_fwd(qf, qr, kvf, kr, nm("mla_fwd"))
    w16 = jnp.where(jnp.tril(jnp.ones((SGU_CHUNK, SGU_CHUNK), bool))[None], P["sgu_w"], 0.0).astype(BF16)
    bt = P["sgu_b"].T
    yc = _sgu_fwd(h, off["hu"], off["hv"], P["sgu_ln_g"], P["sgu_ln_b"], w16, bt, nm("sgu_fwd"))
    merged, z = _merge_fwd([ya, yb16, yc], [W["w_proj_a"], W["w_proj_b"], W["w_proj_c"]], h, W["b_gate"], nm("merge_fwd"))
    x1, x1_16, xh1, rs1 = _mm_ln(merged, W["w_o"], x, P["ln1_g"], P["ln1_b"], alpha, nm("wo_ln1"))
    W.update(fetch(l, "c", x1_16))
    up = _mm(x1_16, W["w_up"], "nn", [(F32, "n")], nm("up"))[0]
    a = _glu_fwd(up, W["conv_w"], P["conv_b"], nm("glu_fwd"))
    x2, x2_16, xh2, rs2 = _mm_ln(a, W["w_down"], x1, P["ln2_g"], P["ln2_b"], alpha, nm("down_ln2"))
    sv.update(W=W, ya=ya, lse_a=lse_a, cqn=cqn, rq=rq, ckvn=ckvn, rkv=rkv, qf=qf, qr=qr, kvf=kvf, kr=kr, lse_b=lse_b, yb=yb, yb16=yb16,
              w16=w16, bt=bt, yc=yc, merged=merged, z=z, x1_16=x1_16, xh1=xh1, rs1=rs1, up=up, a=a, xh2=xh2, rs2=rs2)
    return x2, x2_16, sv


def _dw_chunks(k, a, dy, name, post=None, chunker=None):
    n = dy.shape[1]
    if k not in ROW_SHARDED and post is None and chunker is None and (n // N_DEV) % 128 == 0:
        return _mm(a, dy, "tn", [(BF16, "n")], name, chunk=n // N_DEV)[0]
    g = _mm(a, dy, "tn", [(BF16, "n")], name)[0]
    if chunker is not None:
        return chunker(g)
    return _to_chunks(k, g if post is None else post(g))


def _after(arr, token):
    return arr if token is None else arr + token[0:1, 0:1].astype(arr.dtype)


def _layer_bwd(l, dx2, sv, P, cs, sn, seg, alpha, emit):
    S, D = dx2.shape
    H, half = MLA_HEADS, MLA_ROPE // 2
    off = seg.off
    h, W = sv["h"], sv["W"]
    nm = lambda s: f"l{l}_{s}"
    g = {}
    dr2, dr2_16, g["ln2_g"], g["ln2_b"] = _ln_bwd(dx2, sv["xh2"], sv["rs2"], P["ln2_g"], nm("ln2_bwd"))
    g["w_down"] = _dw_chunks("w_down", sv["a"], dr2_16, nm("dw_down"))
    da = _mm(dr2_16, W["w_down"], "nt", [(F32, "n")], nm("da"))[0]
    dcg, dcv, st_g, st_v = _glu_bwd(sv["up"], W["conv_w"], P["conv_b"], da, nm("glu_bwd"))
    F = dcg.shape[1]
    g["conv_w"] = _to_chunks("conv_w", jnp.concatenate([st_g[0:3], st_v[0:3]], axis=1))
    g["conv_b"] = jnp.concatenate([st_g[3:4], st_v[3:4]], axis=1)
    dup = _conv_bwd(dcg, dcv, W["conv_w"], nm("conv_bwd"))
    g["w_up"] = _dw_chunks("w_up", sv["x1_16"], dup, nm("dw_up"))
    token = emit(l, "c", {k: g.pop(k) for k in GROUPS["c"]})
    dx1 = _mm_axpy(dup, W["w_up"], "nt", dr2, alpha, nm("dx1"), dep=token)
    dr1, dr1_16, g["ln1_g"], g["ln1_b"] = _ln_bwd(dx1, sv["xh1"], sv["rs1"], P["ln1_g"], nm("ln1_bwd"))
    g["w_o"] = _dw_chunks("w_o", sv["merged"], dr1_16, nm("dw_o"))
    dmerged = _mm(dr1_16, W["w_o"], "nt", [(F32, "n")], nm("dmerged"))[0]
    dz, dlog, dbg = _merge_bwd(dmerged, sv["z"], h, W["b_gate"], nm("merge_bwd"))
    g["b_gate"] = _to_chunks("b_gate", dbg[0:3])
    g["w_proj_a"] = _dw_chunks("w_proj_a", sv["ya"], dz[0], nm("dw_pa"))
    g["w_proj_b"] = _dw_chunks("w_proj_b", sv["yb16"], dz[1], nm("dw_pb"))
    g["w_proj_c"] = _dw_chunks("w_proj_c", sv["yc"], dz[2], nm("dw_pc"))
    dya = _mm(dz[0], W["w_proj_a"], "nt", [(F32, "n")], nm("dya"))[0]
    dyb = _mm(dz[1], W["w_proj_b"], "nt", [(F32, "n")], nm("dyb"))[0]
    dyc = _mm(dz[2], W["w_proj_c"], "nt", [(F32, "n")], nm("dyc"))[0]
    dhu, dhv, g["sgu_w"], db_s, g["sgu_ln_g"], g["sgu_ln_b"] = _sgu_bwd(
        h, off["hu"], off["hv"], P["sgu_ln_g"], P["sgu_ln_b"], sv["w16"], sv["bt"], dyc, nm("sgu_bwd"))
    g["sgu_b"] = db_s[:, :SGU_GROUPS].T
    dqa, dka, dva, dsk = _swa_bwd(h, off["qa"], off["ka"], off["va"], P["sinks"], dya, sv["lse_a"], nm("swa_bwd"))
    g["sinks"] = dsk[0, :SWA_Q_HEADS]
    dqn, dqr, dkn, dvv, dkr = _mla_bwd(sv["qf"], sv["qr"], sv["kvf"], sv["kr"], dyb, sv["yb"], sv["lse_b"], nm("mla_bwd"))
    dqf = jnp.concatenate([dqn.astype(BF16), _rope(dqr, 0, H * MLA_ROPE, cs, sn, True, nm("ropeq_bwd"))], axis=1)
    dkvf = jnp.concatenate([dkn, dvv], axis=1).astype(BF16)
    dkr16 = _rope(dkr, 0, LANES, cs, sn, True, nm("ropek_bwd"))
    g["w_uq"] = _dw_chunks("w_uq", sv["cqn"], dqf, nm("dw_uq"), _uq_unpermute)
    g["w_ukv"] = _dw_chunks("w_ukv", sv["ckvn"], dkvf, nm("dw_ukv"), _ukv_unpermute)
    token = emit(l, "b", {k: g.pop(k) for k in GROUPS["b"]})
    dcqn = _mm(dqf, W["w_uq"], "nt", [(F32, "n")], nm("dcqn"), dep=token)[0]
    dckvn = _mm(dkvf, W["w_ukv"], "nt", [(F32, "n")], nm("dckvn"), dep=token)[0]
    dcq, g["q_norm_g"] = _rms_bwd(dcqn, h, off["cq"], seg.width["cq"], sv["rq"], P["q_norm_g"], nm("rmsq_bwd"))
    dckv, g["kv_norm_g"] = _rms_bwd(dckvn, h, off["ckv"], seg.width["ckv"], sv["rkv"], P["kv_norm_g"], nm("rmskv_bwd"))
    assert seg.order[-1] == "kr" and seg.n_pad - seg.used >= LANES - MLA_ROPE
    parts = {"g": jnp.concatenate([dlog[0], dlog[1], dlog[2]], axis=1), "qa": dqa, "hu": dhu, "hv": dhv, "cq": dcq, "ckv": dckv,
             "ka": dka.astype(BF16), "va": dva.astype(BF16), "kr": dkr16}
    dh = jnp.concatenate([parts[k] for k in seg.order] + [jnp.zeros((S, seg.n_pad - seg.used - (LANES - MLA_ROPE)), BF16)],
                         axis=1)
    token = emit(l, "a", {"w_in": _dw_chunks("w_in", sv["x16"], dh, nm("dw_in"), chunker=seg.to_shards)})
    dx = _mm_axpy(dh, W["w_in"], "nt", dr1, alpha, nm("dx"), dep=token)
    return dx, g


BIG = ("w_in", "w_uq", "w_ukv", "w_proj_a", "w_proj_b", "w_proj_c", "w_o", "w_up", "w_down")
ROW_SHARDED = ("w_proj_b", "w_o", "w_down")
SHARDED_F32 = ("b_gate", "conv_w")
REPLICATED = ("sinks", "q_norm_g", "kv_norm_g", "sgu_ln_g", "sgu_ln_b", "sgu_w", "sgu_b", "ln1_g", "ln1_b", "conv_b", "ln2_g",
              "ln2_b")
WEIGHTS = ("w_in", "b_gate", "sinks", "q_norm_g", "kv_norm_g", "w_uq", "w_ukv", "sgu_ln_g", "sgu_ln_b", "sgu_w", "sgu_b",
           "w_proj_a", "w_proj_b", "w_proj_c", "w_o", "ln1_g", "ln1_b", "w_up", "conv_w", "conv_b", "w_down", "ln2_g", "ln2_b")


def _step_local(x, positions, target, small, rq, rkv, fetch, emit, token=None):
    S, D = x.shape
    L = small["sinks"].shape[0]
    alpha = (2 * L) ** 0.25
    seg = _Seg(D, rq, rkv)
    inv_freq = ROPE_THETA ** (-jnp.arange(0, MLA_ROPE, 2, dtype=F32) / MLA_ROPE)
    ang = positions.astype(F32)[:, None] * inv_freq
    reps = LANES // (MLA_ROPE // 2)
    cs, sn = jnp.tile(jnp.cos(ang), (1, reps)), jnp.tile(jnp.sin(ang), (1, reps))
    rows = ("q_norm_g", "kv_norm_g", "sgu_ln_g", "sgu_ln_b", "ln1_g", "ln1_b", "conv_b", "ln2_g", "ln2_b")
    layers = [{k: small[k][l].reshape(1, -1) if k in rows else small[k][l] for k in small} for l in range(L)]
    saved = []
    x16 = _after(x, token).astype(BF16)
    for l in range(L):
        x, x16, sv = _layer_fwd(l, x, x16, fetch, layers[l], cs, sn, seg, alpha)
        saved.append(sv)
    loss, dx = _loss(x, target, "loss")
    grads = [None] * L
    for l in reversed(range(L)):
        dx, grads[l] = _layer_bwd(l, dx, saved[l], layers[l], cs, sn, seg, alpha, emit)
    out = {k: jnp.stack([grads[l][k].reshape(small[k].shape[1:]) for l in range(L)]) for k in small}
    return loss, dx, out


def _unshard(k, gathered):
    n, r, c = gathered.shape
    if k in ROW_SHARDED:
        return gathered.reshape(n * r, c)
    return gathered.transpose(1, 0, 2).reshape(r, n * c)


def _to_chunks(k, gfull):
    r, c = gfull.shape
    if k in ROW_SHARDED:
        return gfull.reshape(N_DEV, r // N_DEV, c)
    return gfull.reshape(r, N_DEV, c // N_DEV).transpose(1, 0, 2)


def _pack(arrs):
    P = arrs[0].shape[0]
    flat = jnp.concatenate([a.reshape(P, -1) for a in arrs], axis=1)
    sizes = [(a.size // P, a.size // P) for a in arrs]
    flat = jnp.pad(flat, ((0, 0), (0, -flat.shape[1] % (SUBLANES * 128))))
    return flat.reshape(P, -1, 128), sizes


def _unpack(packed, sizes, shapes):
    flat = packed.reshape(-1)
    out, o = [], 0
    for (n, npad), shp in zip(sizes, shapes):
        out.append(flat[o:o + n].reshape(shp))
        o += npad
    return out


def kernel(x, positions, w_in, b_gate, sinks, q_norm_g, kv_norm_g, w_uq, w_ukv, sgu_ln_g, sgu_ln_b, sgu_w, sgu_b, w_proj_a, w_proj_b, w_proj_c, w_o, ln1_g, ln1_b, w_up, conv_w, conv_b, w_down, ln2_g, ln2_b, loss_target, m_w_in, m_b_gate, m_sinks, m_q_norm_g, m_kv_norm_g, m_w_uq, m_w_ukv, m_sgu_ln_g, m_sgu_ln_b, m_sgu_w, m_sgu_b, m_w_proj_a, m_w_proj_b, m_w_proj_c, m_w_o, m_ln1_g, m_ln1_b, m_w_up, m_conv_w, m_conv_b, m_w_down, m_ln2_g, m_ln2_b, v_w_in, v_b_gate, v_sinks, v_q_norm_g, v_kv_norm_g, v_w_uq, v_w_ukv, v_sgu_ln_g, v_sgu_ln_b, v_sgu_w, v_sgu_b, v_w_proj_a, v_w_proj_b, v_w_proj_c, v_w_o, v_ln1_g, v_ln1_b, v_w_up, v_conv_w, v_conv_b, v_w_down, v_ln2_g, v_ln2_b):
    given = dict(locals())
    w = {k: given[k] for k in WEIGHTS}
    mom = {k: given["m_" + k] for k in WEIGHTS}
    var = {k: given["v_" + k] for k in WEIGHTS}

    L = w_in.shape[0]
    order = [(l, grp) for l in range(L) for grp in ("a", "b", "c")]

    first = [w[k][0].astype(BF16) for k in GROUPS["a"]]
    first_lands = _gather_two_level(first, "gather_first")
    gathers, token = {}, first_lands[0]
    for l, grp in order[1:]:
        srcs = [w[k][l].astype(BF16) if k in BIG else w[k][l] for k in GROUPS[grp]]
        gathers[l, grp], token = _exchange_start(srcs, False, token, f"gather_start_l{l}{grp}")

    me = 4 * lax.axis_index("x") + 2 * lax.axis_index("y") + lax.axis_index("c")
    mine = (jnp.arange(N_DEV) == me)[:, None, None]

    def fetch(l, grp, after):
        if (l, grp) == order[0]:
            srcs, lands = first, first_lands
        else:
            srcs, lands = _exchange_wait(gathers[l, grp], after, f"gather_wait_l{l}{grp}")
        full = {k: jnp.where(mine, srcs[i][None], lands[i]) for i, k in enumerate(GROUPS[grp])}
        return {k: v if k == "w_in" else _unshard(k, v) for k, v in full.items()}

    scatters = {}

    def emit(l, grp, chunks):
        scatters[l, grp], tok = _exchange_start([chunks[k] for k in GROUPS[grp]], True, None, f"scatter_start_l{l}{grp}")
        return tok

    small = {k: w[k] for k in REPLICATED}
    loss, grad_x, g = _step_local(x[0], positions[0], loss_target[0], small, w_uq.shape[1], w_ukv.shape[1], fetch, emit, token)
    loss = lax.psum(loss[0, 0], AXES)

    packed, sizes = _pack([g[k][None] for k in REPLICATED])
    small_grads, after = _exchange_start([packed[0]], False, grad_x, "gather_small_grads_start")

    me1 = me.astype(jnp.int32).reshape(1)
    res = {}
    for grp in ("c", "b", "a"):
        slots, own = {}, {}
        for l in reversed(range(L)):
            srcs, lands = _exchange_wait(scatters[l, grp], after, f"scatter_wait_l{l}{grp}")
            for i, k in enumerate(GROUPS[grp]):
                slots[k, l], own[k, l] = lands[i], srcs[i]
        for k in GROUPS[grp]:
            res[k] = _adamw([slots[k, l] for l in range(L)], [own[k, l] for l in range(L)], me1, w[k], mom[k], var[k],
                            "adamw_" + k, dep=after)
            after = res[k][1]

    srcs, lands = _exchange_wait(small_grads, after, "gather_small_grads_wait")
    parts = jnp.where(mine, srcs[0][None], lands[0])
    shapes = [w[k].shape for k in REPLICATED]
    pw, _ = _pack([w[k][None] for k in REPLICATED])
    pm, _ = _pack([mom[k][None] for k in REPLICATED])
    pv, _ = _pack([var[k][None] for k in REPLICATED])
    outs = _adamw([parts], None, me1, pw, pm, pv, "adamw_small")
    unpacked = [_unpack(o, sizes, shapes) for o in outs]
    for i, k in enumerate(REPLICATED):
        res[k] = [unpacked[j][i] for j in range(4)]

    return (loss, grad_x[None], *[res[k][0] for k in WEIGHTS], *[res[k][1] for k in WEIGHTS],
            *[res[k][2] for k in WEIGHTS], *[res[k][3] for k in WEIGHTS])
```
